```python
import jax
import jax.numpy as jnp
from jax import lax
import numpy as np

D_MODEL = 2048
BATCH = 4
SEQ = 2048
DEPTH = 2
DEC_BATCH = 32
DEC_SEQ = 1
PAST_LEN = 16384
PAGE_SIZE = 128

EXPAND = 2
WIDTH = EXPAND * D_MODEL
HEAD_DIM = 64
N_HEADS_A = WIDTH // HEAD_DIM
N_Q_HEADS = WIDTH // HEAD_DIM
N_KV_HEADS = 8
GQA_GROUP = N_Q_HEADS // N_KV_HEADS
WINDOW = 128
BLOCK = 128
ROPE_DIM = HEAD_DIM // 4
ROPE_THETA = 500000.0
N_META = 16
N_A_LAYERS = DEPTH // 2
N_B_LAYERS = DEPTH - N_A_LAYERS
LORA_W = max(32, int(round(1.8 * D_MODEL ** 0.5 / 32)) * 32)
LORA_A = max(32, int(round(1.8 * D_MODEL ** 0.5 / 32)) * 32)
RMS_EPS = 1e-6
GN_EPS = 64e-5

kernel_name = 'rwkv7_yoco_swa_sink_decoder_step'


def rmsnorm(x, g):
    xf = x.astype(jnp.float32)
    y = xf * lax.rsqrt(jnp.mean(xf * xf, axis=-1, keepdims=True) + RMS_EPS)
    return (y * g.astype(jnp.float32)).astype(x.dtype)


def rotary(x, pos):
    half = ROPE_DIM // 2
    inv_freq = ROPE_THETA ** (-jnp.arange(half, dtype=jnp.float32) * 2.0 / ROPE_DIM)
    ang = pos.astype(jnp.float32)[:, None] * inv_freq[None, :]
    cos = jnp.cos(ang)[:, None, :]
    sin = jnp.sin(ang)[:, None, :]
    xr = x[..., :ROPE_DIM].astype(jnp.float32)
    x1, x2 = xr[..., :half], xr[..., half:]
    rot = jnp.concatenate([x1 * cos - x2 * sin, x2 * cos + x1 * sin], axis=-1)
    return jnp.concatenate([rot.astype(x.dtype), x[..., ROPE_DIM:]], axis=-1)


def rwkv7_step(S, inp):
    r, d, k, v, kk, b = inp
    sa = jnp.einsum('bhij,bhj->bhi', S, kk)
    S = S * d[:, :, None, :] - sa[..., None] * b[:, :, None, :] + v[..., None] * k[:, :, None, :]
    return S, jnp.einsum('bhij,bhj->bhi', S, r)


def rwkv7_layer(xn, shift_prev, S0, mu, w_rkvz, w0, w1, w2, a0, a1, a2, k_k, k_a, r_k, gn_g, gn_b, w_out):
    f32 = jnp.float32
    B, T, _ = xn.shape
    dt = xn.dtype
    hn = (N_HEADS_A, HEAD_DIM)
    prev = jnp.concatenate([shift_prev[:, None, :].astype(dt), xn[:, :-1]], axis=1)
    xx = prev - xn
    xm = xn[:, :, None, :] + xx[:, :, None, :] * mu.astype(dt)
    rkvz = jnp.einsum('btpd,pde->btpe', xm[:, :, :4], w_rkvz)
    r, k, v, z = rkvz[:, :, 0], rkvz[:, :, 1], rkvz[:, :, 2], rkvz[:, :, 3]
    xw, xa = xm[:, :, 4], xm[:, :, 5]
    w_log = -jax.nn.softplus(-(w0 + jnp.tanh(xw @ w1) @ w2).astype(f32)) - 0.5
    decay = jnp.exp(-jnp.exp(w_log))
    a = jax.nn.sigmoid((a0 + (xa @ a1) @ a2).astype(f32))
    heads = lambda t: t.astype(f32).reshape(B, T, N_HEADS_A, HEAD_DIM)
    r_h, k_h, v_h, a_h, d_h = heads(r), heads(k), heads(v), heads(a), heads(decay)
    kk = k_h * k_k.astype(f32).reshape(hn)
    kk = kk / jnp.maximum(jnp.sqrt(jnp.sum(kk * kk, axis=-1, keepdims=True)), 1e-12)
    k_h = k_h * (1.0 + (a_h - 1.0) * k_a.astype(f32).reshape(hn))
    b_h = kk * a_h
    tm = lambda t: jnp.swapaxes(t, 0, 1)
    S_fin, y = lax.scan(rwkv7_step, S0.astype(f32),
                        (tm(r_h), tm(d_h), tm(k_h), tm(v_h), tm(kk), tm(b_h)))
    y = jnp.swapaxes(y, 0, 1)
    mean = jnp.mean(y, axis=-1, keepdims=True)
    var = jnp.mean(jnp.square(y - mean), axis=-1, keepdims=True)
    y = (y - mean) * lax.rsqrt(var + GN_EPS) * gn_g.astype(f32).reshape(hn) + gn_b.astype(f32).reshape(hn)
    y = y + jnp.sum(r_h * k_h * r_k.astype(f32), axis=-1, keepdims=True) * v_h
    y = y.reshape(B, T, WIDTH).astype(dt) * jax.nn.silu(z)
    return y @ w_out, S_fin.astype(S0.dtype), xn[:, -1]


def shared_kv(h, kv_norm, w_kv, pos):
    B, T, _ = h.shape
    kv = (rmsnorm(h, kv_norm) @ w_kv).reshape(B, T, 2, N_KV_HEADS, HEAD_DIM)
    return rotary(kv[:, :, 0], pos), kv[:, :, 1]


def sink_attend(q, k, v, valid, sinks):
    f32 = jnp.float32
    s = jnp.einsum('bqhgd,bkhd->bhgqk', q, k).astype(f32) * (HEAD_DIM ** -0.5)
    s = jnp.where(valid, s, -jnp.inf)
    sk = sinks.astype(f32)[None, :, :, None, None]
    m = jnp.maximum(jnp.max(s, axis=-1, keepdims=True), sk)
    p = jnp.exp(s - m)
    p = p / (jnp.sum(p, axis=-1, keepdims=True) + jnp.exp(sk - m))
    return jnp.einsum('bhgqk,bkhd->bqhgd', p.astype(v.dtype), v)


def banded_prompt_attention(q, k, v, sinks):
    B, L = q.shape[:2]
    lead = (-N_META) % BLOCK
    tail = (-(lead + L)) % BLOCK
    P = lead + L + tail
    nb = P // BLOCK
    padt = lambda t: jnp.pad(t, ((0, 0), (lead, tail)) + ((0, 0),) * (t.ndim - 2))
    qb = padt(q).reshape(B, nb, BLOCK, N_KV_HEADS, GQA_GROUP, HEAD_DIM)
    kb = padt(k).reshape(B, nb, BLOCK, N_KV_HEADS, HEAD_DIM)
    vb = padt(v).reshape(B, nb, BLOCK, N_KV_HEADS, HEAD_DIM)
    band = lambda t: jnp.concatenate([jnp.pad(t, ((0, 0), (1, 0), (0, 0), (0, 0), (0, 0)))[:, :-1], t], axis=2)
    kband, vband = band(kb), band(vb)
    qi = jnp.arange(BLOCK)
    kj = jnp.arange(2 * BLOCK) - BLOCK

    def one_block(args):
        q_blk, k_blk, v_blk, n = args
        qp = n * BLOCK + qi
        kp = n * BLOCK + kj
        diff = qp[:, None] - kp[None, :]
        valid = (kp[None, :] >= lead) & (diff >= 0) & (diff <= WINDOW)
        return sink_attend(q_blk, k_blk, v_blk, valid, sinks)

    out = lax.map(one_block, (jnp.moveaxis(qb, 1, 0), jnp.moveaxis(kband, 1, 0),
                              jnp.moveaxis(vband, 1, 0), jnp.arange(nb)))
    out = jnp.moveaxis(out, 0, 1).reshape(B, P, N_Q_HEADS, HEAD_DIM)[:, lead:lead + L]
    return out.reshape(B, L, WIDTH)


def setup_inputs(seed: int = 0) -> dict:
    key = jax.random.key(seed)
    ks = iter(jax.random.split(key, 40))
    f32 = jnp.float32
    nrm = lambda shape, scale: scale * jax.random.normal(next(ks), shape, f32)
    NA, NB, D, E = N_A_LAYERS, N_B_LAYERS, D_MODEL, WIDTH
    win = min(WINDOW, PAST_LEN)
    return {
        'x_prompt': nrm((BATCH, SEQ, D), 1.0),
        'x_sample': nrm((DEC_BATCH, DEC_SEQ, D), 1.0),
        'state_wkv': nrm((NA, DEC_BATCH, N_HEADS_A, HEAD_DIM, HEAD_DIM), 0.3),
        'state_shift': nrm((NA, DEC_BATCH, D), 1.0),
        'cache_k': nrm((DEC_BATCH, win, N_KV_HEADS, HEAD_DIM), 1.0),
        'cache_v': nrm((DEC_BATCH, win, N_KV_HEADS, HEAD_DIM), 1.0),
        'meta_tokens': nrm((N_META, D), 1.0),
        'a_norm': 1.0 + nrm((NA, D), 0.02),
        'a_mu': jax.random.uniform(next(ks), (NA, 6, D), f32),
        'a_w_rkvz': nrm((NA, 4, D, E), D ** -0.5),
        'a_w0': jax.random.uniform(next(ks), (NA, E), f32, minval=-4.0, maxval=1.0),
        'a_w1': nrm((NA, D, LORA_W), D ** -0.5),
        'a_w2': nrm((NA, LORA_W, E), 0.1 * LORA_W ** -0.5),
        'a_a0': nrm((NA, E), 0.1),
        'a_a1': nrm((NA, D, LORA_A), D ** -0.5),
        'a_a2': nrm((NA, LORA_A, E), 0.1 * LORA_A ** -0.5),
        'a_k_k': 0.85 + nrm((NA, E), 0.02),
        'a_k_a': 1.0 + nrm((NA, E), 0.02),
        'a_r_k': nrm((NA, N_HEADS_A, HEAD_DIM), 0.1),
        'a_gn_g': 1.0 + nrm((NA, E), 0.02),
        'a_gn_b': nrm((NA, E), 0.01),
        'a_w_out': nrm((NA, E, D), E ** -0.5),
        'kv_norm': 1.0 + nrm((D,), 0.02),
        'w_kv': nrm((D, 2 * N_KV_HEADS * HEAD_DIM), D ** -0.5),
        'b_norm': 1.0 + nrm((NB, D), 0.02),
        'b_w_qz': nrm((NB, D, 2 * E), D ** -0.5),
        'b_sinks': nrm((NB, N_Q_HEADS), 0.5),
        'b_w_o': nrm((NB, E, D), E ** -0.5),
        'final_norm': 1.0 + nrm((D,), 0.02),
    }


def reference(x_prompt, x_sample, state_wkv, state_shift, cache_k, cache_v, meta_tokens,
              a_norm, a_mu, a_w_rkvz, a_w0, a_w1, a_w2, a_a0, a_a1, a_a2, a_k_k, a_k_a, a_r_k,
              a_gn_g, a_gn_b, a_w_out, kv_norm, w_kv, b_norm, b_w_qz, b_sinks, b_w_o, final_norm):
    B = x_prompt.shape[0]
    L = x_prompt.shape[1] + N_META
    DB, S_dec = x_sample.shape[0], x_sample.shape[1]
    win = cache_k.shape[1]
    dt = x_prompt.dtype
    hp = jnp.concatenate([jnp.broadcast_to(meta_tokens.astype(dt)[None], (B, N_META, D_MODEL)), x_prompt], axis=1)
    hs = x_sample
    pos_p = jnp.arange(L)
    pos_s = PAST_LEN + jnp.arange(S_dec)
    k_pos = PAST_LEN - win + jnp.arange(win + S_dec)
    diff_s = pos_s[:, None] - k_pos[None, :]
    valid_s = (diff_s >= 0) & (diff_s <= WINDOW)
    zero_S = jnp.zeros((B, N_HEADS_A, HEAD_DIM, HEAD_DIM), state_wkv.dtype)
    zero_shift = jnp.zeros((B, D_MODEL), dt)
    p_wkv, p_shift, s_wkv, s_shift = [], [], [], []
    for i in range(DEPTH):
        if i < N_A_LAYERS:
            pa = (a_mu[i], a_w_rkvz[i], a_w0[i], a_w1[i], a_w2[i], a_a0[i], a_a1[i], a_a2[i],
                  a_k_k[i], a_k_a[i], a_r_k[i], a_gn_g[i], a_gn_b[i], a_w_out[i])
            o, S_new, sh = rwkv7_layer(rmsnorm(hp, a_norm[i]), zero_shift, zero_S, *pa)
            hp = hp + o
            p_wkv.append(S_new)
            p_shift.append(sh)
            o, S_new, sh = rwkv7_layer(rmsnorm(hs, a_norm[i]), state_shift[i], state_wkv[i], *pa)
            hs = hs + o
            s_wkv.append(S_new)
            s_shift.append(sh)
            if i == N_A_LAYERS - 1:
                kp, vp = shared_kv(hp, kv_norm, w_kv, pos_p)
                ksn, vsn = shared_kv(hs, kv_norm, w_kv, pos_s)
                k_all = jnp.concatenate([cache_k.astype(ksn.dtype), ksn], axis=1)
                v_all = jnp.concatenate([cache_v.astype(vsn.dtype), vsn], axis=1)
        else:
            j = i - N_A_LAYERS
            sinks = b_sinks[j].reshape(N_KV_HEADS, GQA_GROUP)
            q, z = jnp.split(rmsnorm(hp, b_norm[j]) @ b_w_qz[j], 2, axis=-1)
            q = rotary(q.reshape(B, L, N_Q_HEADS, HEAD_DIM), pos_p)
            att = banded_prompt_attention(q, kp, vp, sinks)
            hp = hp + (att * jax.nn.silu(z)) @ b_w_o[j]
            q, z = jnp.split(rmsnorm(hs, b_norm[j]) @ b_w_qz[j], 2, axis=-1)
            q = rotary(q.reshape(DB, S_dec, N_Q_HEADS, HEAD_DIM), pos_s)
            q = q.reshape(DB, S_dec, N_KV_HEADS, GQA_GROUP, HEAD_DIM)
            att = sink_attend(q, k_all, v_all, valid_s, sinks).reshape(DB, S_dec, WIDTH)
            hs = hs + (att * jax.nn.silu(z)) @ b_w_o[j]
    y_prompt = rmsnorm(hp, final_norm)[:, N_META:]
    y_sample = rmsnorm(hs, final_norm)
    p_state_wkv = jnp.stack(p_wkv)
    p_state_shift = jnp.stack(p_shift)
    p_cache_k = kp[:, -win:]
    p_cache_v = vp[:, -win:]
    s_state_wkv = jnp.stack(s_wkv)
    s_state_shift = jnp.stack(s_shift)
    s_cache_k = k_all[:, -win:]
    s_cache_v = v_all[:, -win:]
    return (y_prompt, y_sample, p_state_wkv, p_state_shift, p_cache_k, p_cache_v,
            s_state_wkv, s_state_shift, s_cache_k, s_cache_v)
```

```python
import functools
import math

import jax
import jax.numpy as jnp
import numpy as np
from jax import lax
from jax.experimental import pallas as pl
from jax.experimental.pallas import tpu as pltpu

F32 = jnp.float32
BF16 = jnp.bfloat16

HEAD_DIM = 64
N_KV_HEADS = 8
WINDOW = 128
BLOCK = 128
ROPE_DIM = HEAD_DIM // 4
ROPE_THETA = 500000.0
N_META = 16
PAST_LEN = 16384
RMS_EPS = 1e-6
GN_EPS = 64e-5
LEAD = (-N_META) % BLOCK
CHUNK = 64
HEADS_PER_STEP = 8
LORA_PAD = 128
MXU_TILE = 256
VMEM_LIMIT = 48 * 1024 * 1024
DECAY_SCALE = math.exp(-0.5)


def _cparams(sem):
    return pltpu.CompilerParams(dimension_semantics=sem, vmem_limit_bytes=VMEM_LIMIT)


def _sigmoid(x):
    return 1.0 / (1.0 + jnp.exp(-x))


def _silu(x):
    return x * _sigmoid(x)


def _dot(a, b):
    return jnp.dot(a, b, preferred_element_type=F32)


def _dot_nt(a, b):
    return lax.dot_general(a, b, (((1,), (1,)), ((), ())), preferred_element_type=F32)


def _dot_tn(a, b):
    return lax.dot_general(a, b, (((0,), (0,)), ((), ())), preferred_element_type=F32)


def _split_hi_lo(x):
    hi = x.astype(BF16)
    lo = (x - hi.astype(F32)).astype(BF16)
    return hi, lo


def _norm_shift_kernel(x_ref, g_ref, xn_ref, xx_ref, carry_ref):
    @pl.when(pl.program_id(1) == 0)
    def _():
        carry_ref[...] = jnp.zeros_like(carry_ref)

    x = x_ref[0]
    tm = x.shape[0]
    xn = x * lax.rsqrt(jnp.mean(x * x, axis=-1, keepdims=True) + RMS_EPS) * g_ref[...]
    rolled = pltpu.roll(xn, 1, axis=0)
    row = lax.broadcasted_iota(jnp.int32, xn.shape, 0)
    prev = jnp.where(row == 0, carry_ref[0:1, :], rolled)
    xn_ref[0] = xn
    xx_ref[0] = prev - xn
    carry_ref[0:1, :] = xn[tm - 1:tm, :]


def norm_shift_prompt(x, g, tm):
    b, p, d = x.shape
    return pl.pallas_call(
        _norm_shift_kernel,
        grid=(b, p // tm),
        in_specs=[pl.BlockSpec((1, tm, d), lambda i, t: (i, t, 0)),
                  pl.BlockSpec((1, d), lambda i, t: (0, 0))],
        out_specs=[pl.BlockSpec((1, tm, d), lambda i, t: (i, t, 0)),
                   pl.BlockSpec((1, tm, d), lambda i, t: (i, t, 0))],
        out_shape=[jax.ShapeDtypeStruct((b, p, d), F32), jax.ShapeDtypeStruct((b, p, d), F32)],
        scratch_shapes=[pltpu.VMEM((8, d), F32)],
        compiler_params=_cparams(("arbitrary", "arbitrary")),
        name="norm_shift_prompt",
    )(x, g.reshape(1, d))


def _norm_shift_sample_kernel(x_ref, prev_ref, g_ref, xn_ref, xx_ref):
    x = x_ref[...]
    xn = x * lax.rsqrt(jnp.mean(x * x, axis=-1, keepdims=True) + RMS_EPS) * g_ref[...]
    xn_ref[...] = xn
    xx_ref[...] = prev_ref[...] - xn


def norm_shift_sample(x, prev, g):
    m, d = x.shape
    return pl.pallas_call(
        _norm_shift_sample_kernel,
        out_shape=[jax.ShapeDtypeStruct((m, d), F32), jax.ShapeDtypeStruct((m, d), F32)],
        name="norm_shift_sample",
    )(x, prev, g.reshape(1, d))


def _rkvz_kernel(xn_ref, xx_ref, mu_ref, w_ref, o_ref, xm_ref):
    @pl.when(pl.program_id(2) == 0)
    def _():
        xm_ref[...] = (xn_ref[...] + xx_ref[...] * mu_ref[0]).astype(BF16)

    o_ref[0] = _dot(xm_ref[...], w_ref[0])


def rkvz_proj(xn, xx, mu4, w4, tm, tn):
    m, d = xn.shape
    e = w4.shape[2]
    return pl.pallas_call(
        _rkvz_kernel,
        grid=(m // tm, 4, e // tn),
        in_specs=[pl.BlockSpec((tm, d), lambda i, p, j: (i, 0)),
                  pl.BlockSpec((tm, d), lambda i, p, j: (i, 0)),
                  pl.BlockSpec((1, 1, d), lambda i, p, j: (p, 0, 0)),
                  pl.BlockSpec((1, d, tn), lambda i, p, j: (p, 0, j))],
        out_specs=pl.BlockSpec((1, tm, tn), lambda i, p, j: (p, i, j)),
        out_shape=jax.ShapeDtypeStruct((4, m, e), F32),
        scratch_shapes=[pltpu.VMEM((tm, d), BF16)],
        compiler_params=_cparams(("arbitrary", "arbitrary", "arbitrary")),
        name="rkvz_proj",
    )(xn, xx, mu4.reshape(4, 1, d), w4)


def _lora_kernel(xn_ref, xx_ref, mu_ref, w1_ref, w2_ref, w0_ref, a1_ref, a2_ref, a0_ref, wl_ref, al_ref):
    xn = xn_ref[...]
    xx = xx_ref[...]
    xw = (xn + xx * mu_ref[0:1, :]).astype(BF16)
    xa = (xn + xx * mu_ref[1:2, :]).astype(BF16)
    hw = jnp.tanh(_dot(xw, w1_ref[...])).astype(BF16)
    wl_ref[...] = w0_ref[...] + _dot(hw, w2_ref[...])
    ha = _dot(xa, a1_ref[...]).astype(BF16)
    al_ref[...] = a0_ref[...] + _dot(ha, a2_ref[...])


def lora_proj(xn, xx, mu2, w1, w2, w0, a1, a2, a0, tm):
    m, d = xn.shape
    e = w2.shape[1]
    lr = w1.shape[1]
    full = lambda shape: pl.BlockSpec(shape, lambda i: (0,) * len(shape))
    return pl.pallas_call(
        _lora_kernel,
        grid=(m // tm,),
        in_specs=[pl.BlockSpec((tm, d), lambda i: (i, 0)),
                  pl.BlockSpec((tm, d), lambda i: (i, 0)),
                  full((2, d)), full((d, lr)), full((lr, e)), full((1, e)),
                  full((d, lr)), full((lr, e)), full((1, e))],
        out_specs=[pl.BlockSpec((tm, e), lambda i: (i, 0)), pl.BlockSpec((tm, e), lambda i: (i, 0))],
        out_shape=[jax.ShapeDtypeStruct((m, e), F32), jax.ShapeDtypeStruct((m, e), F32)],
        compiler_params=_cparams(("arbitrary",)),
        name="lora_proj",
    )(xn, xx, mu2, w1, w2, w0.reshape(1, e), a1, a2, a0.reshape(1, e))


def _seg_sum(x, ones_bd):
    hi, lo = _split_hi_lo(x)
    outs = []
    for c in range(x.shape[1] // MXU_TILE):
        sl = slice(c * MXU_TILE, (c + 1) * MXU_TILE)
        outs.append(_dot(hi[:, sl], ones_bd) + _dot(lo[:, sl], ones_bd))
    return jnp.concatenate(outs, axis=1) if len(outs) > 1 else outs[0]


def _unit_lower_inverse(l_strict, eye):
    lp = l_strict
    t = eye - l_strict
    steps = int(math.log2(CHUNK)) - 1
    for _ in range(steps):
        lpb = lp.astype(BF16)
        lp = _dot(lpb, lpb)
        t = _dot(t.astype(BF16), (eye + lp).astype(BF16))
    return t


def _wkv_prompt_kernel(r_ref, k_ref, v_ref, z_ref, wl_ref, al_ref, kk_ref, ka_ref, rk_ref, gg_ref, gb_ref,
                       yg_ref, sout_ref, s_ref):
    t_idx = pl.program_id(2)
    c = CHUNK
    hd = HEAD_DIM

    @pl.when(t_idx == 0)
    def _():
        s_ref[...] = jnp.zeros_like(s_ref)

    r = r_ref[0]
    k = k_ref[0]
    v = v_ref[0]
    hw = r.shape[1]
    nh = hw // hd

    li = lax.broadcasted_iota(jnp.int32, (MXU_TILE, MXU_TILE), 0) // hd
    lj = lax.broadcasted_iota(jnp.int32, (MXU_TILE, MXU_TILE), 1) // hd
    ones_bd = jnp.where(li == lj, 1.0, 0.0).astype(BF16)
    ti = lax.broadcasted_iota(jnp.int32, (c, c), 0)
    tj = lax.broadcasted_iota(jnp.int32, (c, c), 1)
    tri_incl = jnp.where(tj <= ti, 1.0, 0.0).astype(BF16)
    eye = jnp.where(ti == tj, 1.0, 0.0).astype(F32)
    ai = lax.broadcasted_iota(jnp.int32, (c, 2 * c), 0)
    aj = lax.broadcasted_iota(jnp.int32, (c, 2 * c), 1)
    aj_mod = jnp.where(aj >= c, aj - c, aj)
    mask_top_k = (aj >= c) & (aj_mod < ai)
    mask_bot = aj_mod <= ai

    a = _sigmoid(al_ref[...])
    lw = -DECAY_SCALE * _sigmoid(wl_ref[...])
    kk = k * kk_ref[...]
    n2 = _seg_sum(kk * kk, ones_bd)
    kk = kk / jnp.maximum(jnp.sqrt(n2), 1e-12)
    k2 = k * (1.0 + (a - 1.0) * ka_ref[...])
    bb = kk * a

    lw_hi, lw_lo = _split_hi_lo(lw)
    g = _dot(tri_incl, lw_hi) + _dot(tri_incl, lw_lo)
    gm = g[c // 2 - 1:c // 2, :]
    gl = g[c - 1:c, :]
    e_a = jnp.exp(g - gm)
    e_prev = jnp.exp(g - lw - gm)
    e_inv = jnp.exp(gm - g)
    e1 = jnp.exp(gm)
    e2 = jnp.exp(gl - gm)

    x_all = jnp.concatenate([kk * e_prev, r * e_a], axis=0).astype(BF16)
    r1_all = jnp.concatenate([bb * e_inv, k2 * e_inv], axis=0).astype(BF16)
    v_bf = v.astype(BF16)
    zeros_cv = jnp.zeros((c, hd), BF16)

    ys = []
    for h in range(nh):
        sl = slice(h * hd, (h + 1) * hd)
        x_h = x_all[:, sl]
        r1_h = r1_all[:, sl]
        v_h = v_bf[:, sl]
        s_m = s_ref[h] * e1[:, sl]
        a_mat = _dot_nt(x_h, r1_h)
        p_mat = _dot_nt(x_h, s_m.astype(BF16))
        a_top = a_mat[:c, :]
        a_bot = a_mat[c:, :]
        l_b = jnp.where(tj < ti, a_top[:, :c], 0.0)
        lk_v = _dot(jnp.where(mask_top_k, a_top, 0.0).astype(BF16), jnp.concatenate([zeros_cv, v_h], axis=0))
        t_inv = _unit_lower_inverse(l_b, eye)
        u = -_dot(t_inv.astype(BF16), (p_mat[:c, :] + lk_v).astype(BF16))
        uv = jnp.concatenate([u.astype(BF16), v_h], axis=0)
        y = p_mat[c:, :] + _dot(jnp.where(mask_bot, a_bot, 0.0).astype(BF16), uv)
        s_ref[h] = (s_m + _dot_tn(uv, r1_h)) * e2[:, sl]
        ys.append(y)
    y = jnp.concatenate(ys, axis=1)

    inv_n = 1.0 / hd
    mean = _seg_sum(y, ones_bd) * inv_n
    yc = y - mean
    var = _seg_sum(yc * yc, ones_bd) * inv_n
    yn = yc * lax.rsqrt(var + GN_EPS) * gg_ref[...] + gb_ref[...]
    bonus = _seg_sum(r * k2 * rk_ref[...], ones_bd)
    yg_ref[...] = ((yn + bonus * v) * _silu(z_ref[0])).astype(BF16)

    @pl.when(t_idx == pl.num_programs(2) - 1)
    def _():
        sout_ref[0] = s_ref[...]


def wkv_prompt(rkvz, wl, al, k_k, k_a, r_k, gn_g, gn_b, n_batch):
    _, m, e = rkvz.shape
    p = m // n_batch
    nt = p // CHUNK
    hw = HEADS_PER_STEP * HEAD_DIM
    nh = e // HEAD_DIM
    row = lambda i, g, t: (i * nt + t, g)
    proj = lambda q: pl.BlockSpec((1, CHUNK, hw), lambda i, g, t: (q, i * nt + t, g))
    par = pl.BlockSpec((1, hw), lambda i, g, t: (0, g))
    return pl.pallas_call(
        _wkv_prompt_kernel,
        grid=(n_batch, e // hw, nt),
        in_specs=[proj(0), proj(1), proj(2), proj(3),
                  pl.BlockSpec((CHUNK, hw), row), pl.BlockSpec((CHUNK, hw), row),
                  par, par, par, par, par],
        out_specs=[pl.BlockSpec((CHUNK, hw), row),
                   pl.BlockSpec((1, HEADS_PER_STEP, HEAD_DIM, HEAD_DIM), lambda i, g, t: (i, g, 0, 0))],
        out_shape=[jax.ShapeDtypeStruct((m, e), BF16),
                   jax.ShapeDtypeStruct((n_batch, nh, HEAD_DIM, HEAD_DIM), F32)],
        scratch_shapes=[pltpu.VMEM((HEADS_PER_STEP, HEAD_DIM, HEAD_DIM), F32)],
        compiler_params=_cparams(("arbitrary", "arbitrary", "arbitrary")),
        name="wkv_prompt",
    )(rkvz, rkvz, rkvz, rkvz, wl, al, k_k.reshape(1, e), k_a.reshape(1, e), r_k.reshape(1, e),
      gn_g.reshape(1, e), gn_b.reshape(1, e))


def _wkv_sample_kernel(r_ref, k_ref, v_ref, z_ref, wl_ref, al_ref, kk_ref, ka_ref, rk_ref, gg_ref, gb_ref,
                       s_ref, yg_ref, sout_ref, y_scr):
    hd = HEAD_DIM
    r = r_ref[0]
    k = k_ref[0]
    v = v_ref[0]
    a = _sigmoid(al_ref[0])
    d = jnp.exp(-DECAY_SCALE * _sigmoid(wl_ref[0]))
    kk = k * kk_ref[...]
    kk = kk / jnp.maximum(jnp.sqrt(jnp.sum(kk * kk, axis=-1, keepdims=True)), 1e-12)
    k2 = k * (1.0 + (a - 1.0) * ka_ref[...])
    bb = kk * a
    nh = r.shape[0]
    ii = lax.broadcasted_iota(jnp.int32, (hd, hd), 0)
    jj = lax.broadcasted_iota(jnp.int32, (hd, hd), 1)
    eye = ii == jj

    for h in range(nh):
        row = lambda x: x[h:h + 1, :]
        s = s_ref[0, h]
        sa = jnp.sum(s * row(kk), axis=-1, keepdims=True)
        v_col = jnp.sum(jnp.where(eye, row(v), 0.0), axis=-1, keepdims=True)
        s_new = s * row(d) - sa * row(bb) + v_col * row(k2)
        sout_ref[0, h] = s_new
        y_col = jnp.sum(s_new * row(r), axis=-1, keepdims=True)
        y_scr[h:h + 1, :] = jnp.sum(jnp.where(eye, y_col, 0.0), axis=0, keepdims=True)

    y = y_scr[...]
    mean = jnp.mean(y, axis=-1, keepdims=True)
    yc = y - mean
    var = jnp.mean(yc * yc, axis=-1, keepdims=True)
    yn = yc * lax.rsqrt(var + GN_EPS) * gg_ref[...] + gb_ref[...]
    bonus = jnp.sum(r * k2 * rk_ref[...], axis=-1, keepdims=True)
    yg_ref[0] = ((yn + bonus * v) * _silu(z_ref[0])).astype(BF16)


def wkv_sample(rkvz, wl, al, k_k, k_a, r_k, gn_g, gn_b, state):
    _, m, e = rkvz.shape
    nh = e // HEAD_DIM
    hd = HEAD_DIM
    rkvz4 = rkvz.reshape(4, m, nh, hd)
    proj = lambda q: pl.BlockSpec((None, 1, nh, hd), lambda i: (q, i, 0, 0))
    tok = pl.BlockSpec((1, nh, hd), lambda i: (i, 0, 0))
    par = pl.BlockSpec((nh, hd), lambda i: (0, 0))
    st = pl.BlockSpec((1, nh, hd, hd), lambda i: (i, 0, 0, 0))
    as_heads = lambda x: x.reshape(nh, hd)
    return pl.pallas_call(
        _wkv_sample_kernel,
        grid=(m,),
        in_specs=[proj(0), proj(1), proj(2), proj(3), tok, tok, par, par, par, par, par, st],
        out_specs=[tok, st],
        out_shape=[jax.ShapeDtypeStruct((m, nh, hd), BF16), jax.ShapeDtypeStruct(state.shape, F32)],
        scratch_shapes=[pltpu.VMEM((nh, hd), F32)],
        compiler_params=_cparams(("arbitrary",)),
        name="wkv_sample",
    )(rkvz4, rkvz4, rkvz4, rkvz4, wl.reshape(m, nh, hd), al.reshape(m, nh, hd),
      as_heads(k_k), as_heads(k_a), as_heads(r_k), as_heads(gn_g), as_heads(gn_b), state)


def _mm_res_kernel(x_ref, w_ref, res_ref, o_ref):
    o_ref[...] = res_ref[...] + _dot(x_ref[...], w_ref[...])


def matmul_residual(x, w, res, tm, tn):
    m, kdim = x.shape
    n = w.shape[1]
    return pl.pallas_call(
        _mm_res_kernel,
        grid=(m // tm, n // tn),
        in_specs=[pl.BlockSpec((tm, kdim), lambda i, j: (i, 0)),
                  pl.BlockSpec((kdim, tn), lambda i, j: (0, j)),
                  pl.BlockSpec((tm, tn), lambda i, j: (i, j))],
        out_specs=pl.BlockSpec((tm, tn), lambda i, j: (i, j)),
        out_shape=jax.ShapeDtypeStruct((m, n), F32),
        compiler_params=_cparams(("arbitrary", "arbitrary")),
        name="matmul_residual",
    )(x, w, res)


def _rope(y, cos, sin_a, sin_b):
    tn = y.shape[1]
    rep = tn // cos.shape[1]
    tile = lambda t: jnp.concatenate([t] * rep, axis=1) if rep > 1 else t
    half = ROPE_DIM // 2
    return (y * tile(cos) + pltpu.roll(y, tn - half, axis=1) * tile(sin_a)
            + pltpu.roll(y, half, axis=1) * tile(sin_b))


def _norm_mm_kernel(h_ref, g_ref, w_ref, *rest, rope):
    if rope:
        cos_ref, sa_ref, sb_ref, o_ref, xn_ref = rest
    else:
        o_ref, xn_ref = rest

    @pl.when(pl.program_id(1) == 0)
    def _():
        x = h_ref[...]
        xn = x * lax.rsqrt(jnp.mean(x * x, axis=-1, keepdims=True) + RMS_EPS) * g_ref[...]
        xn_ref[...] = xn.astype(BF16)

    y = _dot(xn_ref[...], w_ref[...])
    if rope:
        y = _rope(y, cos_ref[...], sa_ref[...], sb_ref[...])
    o_ref[...] = y.astype(o_ref.dtype)


def norm_matmul(h, g, w, tm, tn, out_dtype, rope_tables=None):
    m, d = h.shape
    n = w.shape[1]
    in_specs = [pl.BlockSpec((tm, d), lambda i, j: (i, 0)),
                pl.BlockSpec((1, d), lambda i, j: (0, 0)),
                pl.BlockSpec((d, tn), lambda i, j: (0, j))]
    args = [h, g.reshape(1, d), w]
    if rope_tables is not None:
        lanes = rope_tables[0].shape[1]
        in_specs += [pl.BlockSpec((tm, lanes), lambda i, j: (i, 0))] * 3
        args += list(rope_tables)
    return pl.pallas_call(
        functools.partial(_norm_mm_kernel, rope=rope_tables is not None),
        grid=(m // tm, n // tn),
        in_specs=in_specs,
        out_specs=pl.BlockSpec((tm, tn), lambda i, j: (i, j)),
        out_shape=jax.ShapeDtypeStruct((m, n), out_dtype),
        scratch_shapes=[pltpu.VMEM((tm, d), BF16)],
        compiler_params=_cparams(("arbitrary", "arbitrary")),
        name="norm_matmul_rope" if rope_tables is not None else "norm_matmul",
    )(*args)


def rope_tables(pos):
    half = ROPE_DIM // 2
    inv_freq = ROPE_THETA ** (-jnp.arange(half, dtype=F32) * 2.0 / ROPE_DIM)
    ang = pos.astype(F32)[:, None] * inv_freq[None, :]
    cos = jnp.cos(ang)
    sin = jnp.sin(ang)
    rows = pos.shape[0]
    ones = jnp.ones((rows, HEAD_DIM - ROPE_DIM), F32)
    zeros_h = jnp.zeros((rows, half), F32)
    zeros_r = jnp.zeros((rows, HEAD_DIM - ROPE_DIM), F32)
    cos_h = jnp.concatenate([cos, cos, ones], axis=1)
    sa_h = jnp.concatenate([-sin, zeros_h, zeros_r], axis=1)
    sb_h = jnp.concatenate([zeros_h, sin, zeros_r], axis=1)
    two = lambda t: jnp.concatenate([t, t], axis=1)
    return two(cos_h), two(sa_h), two(sb_h)


def _attn_prompt_kernel(sink_ref, q_ref, kc_ref, kp_ref, vc_ref, vp_ref, z_ref, o_ref):
    n = pl.program_id(1)
    hd = HEAD_DIM
    blk = q_ref.shape[0]
    n_kv = kc_ref.shape[1] // hd
    grp = q_ref.shape[1] // (n_kv * hd)
    qi = lax.broadcasted_iota(jnp.int32, (blk, 2 * blk), 0)
    kj = lax.broadcasted_iota(jnp.int32, (blk, 2 * blk), 1) - blk
    kpos = n * blk + kj
    diff = qi - kj
    valid = (kpos >= LEAD) & (diff >= 0) & (diff <= WINDOW)
    k_all = jnp.concatenate([kp_ref[...], kc_ref[...]], axis=0).astype(BF16)
    v_all = jnp.concatenate([vp_ref[...], vc_ref[...]], axis=0).astype(BF16)
    scale = hd ** -0.5
    outs = []
    for h in range(n_kv):
        k_h = k_all[:, h * hd:(h + 1) * hd]
        v_h = v_all[:, h * hd:(h + 1) * hd]
        for gi in range(grp):
            qh = h * grp + gi
            q_h = q_ref[:, qh * hd:(qh + 1) * hd]
            s = _dot_nt(q_h, k_h) * scale
            s = jnp.where(valid, s, -jnp.inf)
            sk = sink_ref[qh]
            m = jnp.maximum(jnp.max(s, axis=-1, keepdims=True), sk)
            p = jnp.exp(s - m)
            den = jnp.sum(p, axis=-1, keepdims=True) + jnp.exp(sk - m)
            outs.append(_dot(p.astype(BF16), v_h) / den)
    att = jnp.concatenate(outs, axis=1)
    o_ref[...] = (att * _silu(z_ref[...])).astype(o_ref.dtype)


def attn_prompt(sinks, q, k, v, z, n_batch):
    m, e = q.shape
    nb = m // (n_batch * BLOCK)
    kw = k.shape[1]
    cur = lambda i, n: (i * nb + n, 0)
    prv = lambda i, n: (i * nb + jnp.maximum(n - 1, 0), 0)
    return pl.pallas_call(
        _attn_prompt_kernel,
        grid=(n_batch, nb),
        in_specs=[pl.BlockSpec(memory_space=pltpu.SMEM),
                  pl.BlockSpec((BLOCK, e), cur),
                  pl.BlockSpec((BLOCK, kw), cur), pl.BlockSpec((BLOCK, kw), prv),
                  pl.BlockSpec((BLOCK, kw), cur), pl.BlockSpec((BLOCK, kw), prv),
                  pl.BlockSpec((BLOCK, e), cur)],
        out_specs=pl.BlockSpec((BLOCK, e), cur),
        out_shape=jax.ShapeDtypeStruct((m, e), BF16),
        compiler_params=_cparams(("arbitrary", "arbitrary")),
        name="attn_prompt",
    )(sinks, q, k, k, v, v, z)


def _attn_sample_kernel(sink_ref, q_ref, kc_ref, vc_ref, kn_ref, vn_ref, z_ref, o_ref, ko_ref, vo_ref):
    hd = HEAD_DIM
    win = kc_ref.shape[1]
    n_kv = kc_ref.shape[2] // hd
    nq = q_ref.shape[1]
    grp = nq // n_kv
    pad = 8
    kc = kc_ref[0]
    vc = vc_ref[0]
    kn = kn_ref[0]
    vn = vn_ref[0]
    first = lax.broadcasted_iota(jnp.int32, (pad, kc.shape[1]), 0) == 0
    k_all = jnp.concatenate([kc, jnp.where(first, kn, 0.0)], axis=0).astype(BF16)
    v_all = jnp.concatenate([vc, jnp.where(first, vn, 0.0)], axis=0).astype(BF16)
    col = lax.broadcasted_iota(jnp.int32, (grp, win + pad), 1)
    valid = (col <= win) & (win - col <= WINDOW)
    q = q_ref[0].astype(BF16)
    scale = hd ** -0.5
    row_i = lax.broadcasted_iota(jnp.int32, (grp, 1), 0)
    outs = []
    for h in range(n_kv):
        k_h = k_all[:, h * hd:(h + 1) * hd]
        v_h = v_all[:, h * hd:(h + 1) * hd]
        q_h = q[h * grp:(h + 1) * grp, :]
        s = _dot_nt(q_h, k_h) * scale
        s = jnp.where(valid, s, -jnp.inf)
        sk = jnp.zeros((grp, 1), F32)
        for gi in range(grp):
            sk = jnp.where(row_i == gi, sink_ref[h * grp + gi], sk)
        m = jnp.maximum(jnp.max(s, axis=-1, keepdims=True), sk)
        p = jnp.exp(s - m)
        den = jnp.sum(p, axis=-1, keepdims=True) + jnp.exp(sk - m)
        outs.append(_dot(p.astype(BF16), v_h) / den)
    att = jnp.concatenate(outs, axis=0)
    o_ref[0] = (att * _silu(z_ref[0])).astype(o_ref.dtype)
    last = lax.broadcasted_iota(jnp.int32, kc.shape, 0) == win - 1
    ko_ref[0] = jnp.where(last, kn, pltpu.roll(kc, win - 1, axis=0))
    vo_ref[0] = jnp.where(last, vn, pltpu.roll(vc, win - 1, axis=0))


def attn_sample(sinks, q, cache_k, cache_v, k_new, v_new, z):
    m, win, kw = cache_k.shape
    nq = q.shape[1]
    hd = HEAD_DIM
    tok = pl.BlockSpec((1, nq, hd), lambda i: (i, 0, 0))
    cache = pl.BlockSpec((1, win, kw), lambda i: (i, 0, 0))
    new = pl.BlockSpec((1, 1, kw), lambda i: (i, 0, 0))
    return pl.pallas_call(
        _attn_sample_kernel,
        grid=(m,),
        in_specs=[pl.BlockSpec(memory_space=pltpu.SMEM), tok, cache, cache, new, new, tok],
        out_specs=[tok, cache, cache],
        out_shape=[jax.ShapeDtypeStruct((m, nq, hd), BF16),
                   jax.ShapeDtypeStruct(cache_k.shape, F32), jax.ShapeDtypeStruct(cache_v.shape, F32)],
        compiler_params=_cparams(("arbitrary",)),
        name="attn_sample",
    )(sinks, q, cache_k, cache_v, k_new, v_new, z)


def _rmsnorm_kernel(x_ref, g_ref, o_ref):
    x = x_ref[0]
    o_ref[0] = x * lax.rsqrt(jnp.mean(x * x, axis=-1, keepdims=True) + RMS_EPS) * g_ref[...]


def final_norm_prompt(h3, g, skip_blocks, tm):
    b, p, d = h3.shape
    n_out = p - skip_blocks * tm
    return pl.pallas_call(
        _rmsnorm_kernel,
        grid=(b, n_out // tm),
        in_specs=[pl.BlockSpec((1, tm, d), lambda i, t: (i, t + skip_blocks, 0)),
                  pl.BlockSpec((1, d), lambda i, t: (0, 0))],
        out_specs=pl.BlockSpec((1, tm, d), lambda i, t: (i, t, 0)),
        out_shape=jax.ShapeDtypeStruct((b, n_out, d), F32),
        compiler_params=_cparams(("arbitrary", "arbitrary")),
        name="final_norm_prompt",
    )(h3, g.reshape(1, d))


def final_norm_sample(h, g):
    m, d = h.shape
    return pl.pallas_call(
        _rmsnorm_kernel,
        out_shape=jax.ShapeDtypeStruct((1, m, d), F32),
        name="final_norm_sample",
    )(h.reshape(1, m, d), g.reshape(1, d))[0]


def _pad_lora(w_down, w_up):
    r = w_down.shape[1]
    return (jnp.pad(w_down, ((0, 0), (0, LORA_PAD - r))).astype(BF16),
            jnp.pad(w_up, ((0, LORA_PAD - r), (0, 0))).astype(BF16))


def kernel(x_prompt, x_sample, state_wkv, state_shift, cache_k, cache_v, meta_tokens, a_norm, a_mu, a_w_rkvz,
           a_w0, a_w1, a_w2, a_a0, a_a1, a_a2, a_k_k, a_k_a, a_r_k, a_gn_g, a_gn_b, a_w_out, kv_norm, w_kv,
           b_norm, b_w_qz, b_sinks, b_w_o, final_norm):
    nb, seq, d = x_prompt.shape
    db, dseq, _ = x_sample.shape
    assert dseq == 1 and a_norm.shape[0] == 1 and b_norm.shape[0] == 1
    e = a_w_rkvz.shape[3]
    win = cache_k.shape[1]
    p_len = LEAD + N_META + seq
    assert p_len % BLOCK == 0 and (LEAD + N_META) == BLOCK
    m_p = nb * p_len
    kvw = N_KV_HEADS * HEAD_DIM

    w_rkvz = a_w_rkvz[0].astype(BF16)
    w1, w2 = _pad_lora(a_w1[0], a_w2[0])
    a1, a2 = _pad_lora(a_a1[0], a_a2[0])
    w_out = a_w_out[0].astype(BF16)
    w_k = w_kv[:, :kvw].astype(BF16)
    w_v = w_kv[:, kvw:].astype(BF16)
    w_q = b_w_qz[0][:, :e].astype(BF16)
    w_z = b_w_qz[0][:, e:].astype(BF16)
    w_o = b_w_o[0].astype(BF16)
    mu = a_mu[0]
    sinks = b_sinks[0]

    tm_p = p_len // 4
    tn = 512

    hp = jnp.concatenate([jnp.zeros((nb, LEAD, d), F32),
                          jnp.broadcast_to(meta_tokens[None], (nb, N_META, d)), x_prompt], axis=1)
    xn, xx = norm_shift_prompt(hp, a_norm[0], p_len // 8)
    p_state_shift = xn[:, -1][None]
    xn = xn.reshape(m_p, d)
    xx = xx.reshape(m_p, d)
    rkvz = rkvz_proj(xn, xx, mu[:4], w_rkvz, tm_p, tn)
    wl, al = lora_proj(xn, xx, mu[4:6], w1, w2, a_w0[0], a1, a2, a_a0[0], p_len // 8)
    yg, p_state = wkv_prompt(rkvz, wl, al, a_k_k[0], a_k_a[0], a_r_k[0].reshape(-1), a_gn_g[0], a_gn_b[0], nb)
    hp = matmul_residual(yg, w_out, hp.reshape(m_p, d), tm_p, tn)

    pos_p = jnp.maximum(jnp.arange(p_len, dtype=jnp.int32) - LEAD, 0)
    tabs_p = tuple(jnp.tile(t, (nb, 1)) for t in rope_tables(pos_p))
    k_p = norm_matmul(hp, kv_norm, w_k, tm_p, tn, F32, tabs_p)
    v_p = norm_matmul(hp, kv_norm, w_v, tm_p, tn, F32)
    q_p = norm_matmul(hp, b_norm[0], w_q, tm_p, tn, BF16, tabs_p)
    z_p = norm_matmul(hp, b_norm[0], w_z, tm_p, tn, F32)
    att = attn_prompt(sinks, q_p, k_p, v_p, z_p, nb)
    hp = matmul_residual(att, w_o, hp, tm_p, tn)
    y_prompt = final_norm_prompt(hp.reshape(nb, p_len, d), final_norm, (LEAD + N_META) // BLOCK, BLOCK)
    p_cache_k = k_p.reshape(nb, p_len, N_KV_HEADS, HEAD_DIM)[:, -win:]
    p_cache_v = v_p.reshape(nb, p_len, N_KV_HEADS, HEAD_DIM)[:, -win:]

    hs = x_sample.reshape(db, d)
    xn_s, xx_s = norm_shift_sample(hs, state_shift[0], a_norm[0])
    rkvz_s = rkvz_proj(xn_s, xx_s, mu[:4], w_rkvz, db, tn)
    wl_s, al_s = lora_proj(xn_s, xx_s, mu[4:6], w1, w2, a_w0[0], a1, a2, a_a0[0], db)
    yg_s, s_state = wkv_sample(rkvz_s, wl_s, al_s, a_k_k[0], a_k_a[0], a_r_k[0].reshape(-1), a_gn_g[0],
                               a_gn_b[0], state_wkv[0])
    hs = matmul_residual(yg_s.reshape(db, e), w_out, hs, db, tn)
    pos_s = jnp.full((db,), PAST_LEN, jnp.int32)
    tabs_s = rope_tables(pos_s)
    k_s = norm_matmul(hs, kv_norm, w_k, db, tn, F32, tabs_s)
    v_s = norm_matmul(hs, kv_norm, w_v, db, tn, F32)
    q_s = norm_matmul(hs, b_norm[0], w_q, db, tn, F32, tabs_s)
    z_s = norm_matmul(hs, b_norm[0], w_z, db, tn, F32)
    nq = e // HEAD_DIM
    att_s, s_cache_k, s_cache_v = attn_sample(
        sinks, q_s.reshape(db, nq, HEAD_DIM), cache_k.reshape(db, win, kvw), cache_v.reshape(db, win, kvw),
        k_s.reshape(db, 1, kvw), v_s.reshape(db, 1, kvw), z_s.reshape(db, nq, HEAD_DIM))
    hs = matmul_residual(att_s.reshape(db, e), w_o, hs, db, tn)
    y_sample = final_norm_sample(hs, final_norm).reshape(db, 1, d)

    return (y_prompt, y_sample, p_state[None], p_state_shift,
            p_cache_k, p_cache_v,
            s_state[None], xn_s[None],
            s_cache_k.reshape(cache_k.shape), s_cache_v.reshape(cache_v.shape))
```

```python
import functools
import math

import jax
import jax.numpy as jnp
import numpy as np
from jax import lax
from jax.experimental import pallas as pl
from jax.experimental.pallas import tpu as pltpu

F32 = jnp.float32
BF16 = jnp.bfloat16

HEAD_DIM = 64
N_KV_HEADS = 8
WINDOW = 128
BLOCK = 128
ROPE_DIM = HEAD_DIM // 4
ROPE_THETA = 500000.0
N_META = 16
PAST_LEN = 16384
RMS_EPS = 1e-6
GN_EPS = 64e-5
LEAD = (-N_META) % BLOCK
CHUNK = 64
HEADS_PER_STEP = 8
LORA_PAD = 128
MXU_TILE = 256
VMEM_LIMIT = 48 * 1024 * 1024
DECAY_SCALE = math.exp(-0.5)


def _cparams(sem):
    return pltpu.CompilerParams(dimension_semantics=sem, vmem_limit_bytes=VMEM_LIMIT)


def _sigmoid(x):
    return 1.0 / (1.0 + jnp.exp(-x))


def _silu(x):
    return x * _sigmoid(x)


def _dot(a, b):
    return jnp.dot(a, b, preferred_element_type=F32)


def _dot_nt(a, b):
    return lax.dot_general(a, b, (((1,), (1,)), ((), ())), preferred_element_type=F32)


def _dot_tn(a, b):
    return lax.dot_general(a, b, (((0,), (0,)), ((), ())), preferred_element_type=F32)


def _split_hi_lo(x):
    hi = x.astype(BF16)
    lo = (x - hi.astype(F32)).astype(BF16)
    return hi, lo


def _norm_shift_kernel(x_ref, g_ref, xn_ref, xx_ref, carry_ref):
    @pl.when(pl.program_id(1) == 0)
    def _():
        carry_ref[...] = jnp.zeros_like(carry_ref)

    x = x_ref[0]
    tm = x.shape[0]
    xn = x * lax.rsqrt(jnp.mean(x * x, axis=-1, keepdims=True) + RMS_EPS) * g_ref[...]
    rolled = pltpu.roll(xn, 1, axis=0)
    row = lax.broadcasted_iota(jnp.int32, xn.shape, 0)
    prev = jnp.where(row == 0, carry_ref[0:1, :], rolled)
    xn_ref[0] = xn
    xx_ref[0] = prev - xn
    carry_ref[0:1, :] = xn[tm - 1:tm, :]


def norm_shift_prompt(x, g, tm):
    b, p, d = x.shape
    return pl.pallas_call(
        _norm_shift_kernel,
        grid=(b, p // tm),
        in_specs=[pl.BlockSpec((1, tm, d), lambda i, t: (i, t, 0)),
                  pl.BlockSpec((1, d), lambda i, t: (0, 0))],
        out_specs=[pl.BlockSpec((1, tm, d), lambda i, t: (i, t, 0)),
                   pl.BlockSpec((1, tm, d), lambda i, t: (i, t, 0))],
        out_shape=[jax.ShapeDtypeStruct((b, p, d), F32), jax.ShapeDtypeStruct((b, p, d), F32)],
        scratch_shapes=[pltpu.VMEM((8, d), F32)],
        compiler_params=_cparams(("arbitrary", "arbitrary")),
        name="norm_shift_prompt",
    )(x, g.reshape(1, d))


def _norm_shift_sample_kernel(x_ref, prev_ref, g_ref, xn_ref, xx_ref):
    x = x_ref[...]
    xn = x * lax.rsqrt(jnp.mean(x * x, axis=-1, keepdims=True) + RMS_EPS) * g_ref[...]
    xn_ref[...] = xn
    xx_ref[...] = prev_ref[...] - xn


def norm_shift_sample(x, prev, g):
    m, d = x.shape
    return pl.pallas_call(
        _norm_shift_sample_kernel,
        out_shape=[jax.ShapeDtypeStruct((m, d), F32), jax.ShapeDtypeStruct((m, d), F32)],
        name="norm_shift_sample",
    )(x, prev, g.reshape(1, d))


def _rkvz_kernel(xn_ref, xx_ref, mu_ref, w_ref, o_ref, xm_ref):
    @pl.when(pl.program_id(2) == 0)
    def _():
        xm_ref[...] = (xn_ref[...] + xx_ref[...] * mu_ref[0]).astype(BF16)

    o_ref[0] = _dot(xm_ref[...], w_ref[0])


def rkvz_proj(xn, xx, mu4, w4, tm, tn):
    m, d = xn.shape
    e = w4.shape[2]
    return pl.pallas_call(
        _rkvz_kernel,
        grid=(m // tm, 4, e // tn),
        in_specs=[pl.BlockSpec((tm, d), lambda i, p, j: (i, 0)),
                  pl.BlockSpec((tm, d), lambda i, p, j: (i, 0)),
                  pl.BlockSpec((1, 1, d), lambda i, p, j: (p, 0, 0)),
                  pl.BlockSpec((1, d, tn), lambda i, p, j: (p, 0, j))],
        out_specs=pl.BlockSpec((1, tm, tn), lambda i, p, j: (p, i, j)),
        out_shape=jax.ShapeDtypeStruct((4, m, e), F32),
        scratch_shapes=[pltpu.VMEM((tm, d), BF16)],
        compiler_params=_cparams(("arbitrary", "arbitrary", "arbitrary")),
        name="rkvz_proj",
    )(xn, xx, mu4.reshape(4, 1, d), w4)


def _lora_kernel(xn_ref, xx_ref, mu_ref, w1_ref, w2_ref, w0_ref, a1_ref, a2_ref, a0_ref, wl_ref, al_ref):
    xn = xn_ref[...]
    xx = xx_ref[...]
    xw = (xn + xx * mu_ref[0:1, :]).astype(BF16)
    xa = (xn + xx * mu_ref[1:2, :]).astype(BF16)
    hw = jnp.tanh(_dot(xw, w1_ref[...])).astype(BF16)
    wl_ref[...] = w0_ref[...] + _dot(hw, w2_ref[...])
    ha = _dot(xa, a1_ref[...]).astype(BF16)
    al_ref[...] = a0_ref[...] + _dot(ha, a2_ref[...])


def lora_proj(xn, xx, mu2, w1, w2, w0, a1, a2, a0, tm):
    m, d = xn.shape
    e = w2.shape[1]
    lr = w1.shape[1]
    full = lambda shape: pl.BlockSpec(shape, lambda i: (0,) * len(shape))
    return pl.pallas_call(
        _lora_kernel,
        grid=(m // tm,),
        in_specs=[pl.BlockSpec((tm, d), lambda i: (i, 0)),
                  pl.BlockSpec((tm, d), lambda i: (i, 0)),
                  full((2, d)), full((d, lr)), full((lr, e)), full((1, e)),
                  full((d, lr)), full((lr, e)), full((1, e))],
        out_specs=[pl.BlockSpec((tm, e), lambda i: (i, 0)), pl.BlockSpec((tm, e), lambda i: (i, 0))],
        out_shape=[jax.ShapeDtypeStruct((m, e), F32), jax.ShapeDtypeStruct((m, e), F32)],
        compiler_params=_cparams(("arbitrary",)),
        name="lora_proj",
    )(xn, xx, mu2, w1, w2, w0.reshape(1, e), a1, a2, a0.reshape(1, e))


def _seg_sum(x, ones_bd):
    hi, lo = _split_hi_lo(x)
    outs = []
    for c in range(x.shape[1] // MXU_TILE):
        sl = slice(c * MXU_TILE, (c + 1) * MXU_TILE)
        outs.append(_dot(hi[:, sl], ones_bd) + _dot(lo[:, sl], ones_bd))
    return jnp.concatenate(outs, axis=1) if len(outs) > 1 else outs[0]


def _wkv_prompt_kernel(r_ref, k_ref, v_ref, z_ref, wl_ref, al_ref, kk_ref, ka_ref, rk_ref, gg_ref, gb_ref,
                       yg_ref, sout_ref, s_ref):
    t_idx = pl.program_id(2)
    c = CHUNK
    hd = HEAD_DIM

    @pl.when(t_idx == 0)
    def _():
        s_ref[...] = jnp.zeros_like(s_ref)

    r = r_ref[0]
    k = k_ref[0]
    v = v_ref[0]
    hw = r.shape[1]
    nh = hw // hd

    li = lax.broadcasted_iota(jnp.int32, (MXU_TILE, MXU_TILE), 0) // hd
    lj = lax.broadcasted_iota(jnp.int32, (MXU_TILE, MXU_TILE), 1) // hd
    ones_bd = jnp.where(li == lj, 1.0, 0.0).astype(BF16)
    ti = lax.broadcasted_iota(jnp.int32, (c, c), 0)
    tj = lax.broadcasted_iota(jnp.int32, (c, c), 1)
    tri_incl = jnp.where(tj <= ti, 1.0, 0.0).astype(BF16)
    eye = jnp.where(ti == tj, 1.0, 0.0).astype(F32)
    ai = lax.broadcasted_iota(jnp.int32, (c, 2 * c), 0)
    aj = lax.broadcasted_iota(jnp.int32, (c, 2 * c), 1)
    aj_mod = jnp.where(aj >= c, aj - c, aj)
    mask_top_k = (aj >= c) & (aj_mod < ai)
    mask_bot = aj_mod <= ai

    a = _sigmoid(al_ref[...])
    lw = -DECAY_SCALE * _sigmoid(wl_ref[...])
    kk = k * kk_ref[...]
    n2 = _seg_sum(kk * kk, ones_bd)
    kk = kk / jnp.maximum(jnp.sqrt(n2), 1e-12)
    k2 = k * (1.0 + (a - 1.0) * ka_ref[...])
    bb = kk * a

    lw_hi, lw_lo = _split_hi_lo(lw)
    g = _dot(tri_incl, lw_hi) + _dot(tri_incl, lw_lo)
    gm = g[c // 2 - 1:c // 2, :]
    gl = g[c - 1:c, :]
    e_a = jnp.exp(g - gm)
    e_prev = jnp.exp(g - lw - gm)
    e_inv = jnp.exp(gm - g)
    e1 = jnp.exp(gm)
    e2 = jnp.exp(gl - gm)

    x_all = jnp.concatenate([kk * e_prev, r * e_a], axis=0).astype(BF16)
    r1_all = jnp.concatenate([bb * e_inv, k2 * e_inv], axis=0).astype(BF16)
    v_bf = v.astype(BF16)
    zeros_cv = jnp.zeros((c, hd), BF16)

    heads = range(nh)
    sls = [slice(h * hd, (h + 1) * hd) for h in heads]
    xs = [x_all[:, sl] for sl in sls]
    r1s = [r1_all[:, sl] for sl in sls]
    vs = [v_bf[:, sl] for sl in sls]
    sms = [s_ref[h] * e1[:, sls[h]] for h in heads]
    a_mats = [_dot_nt(xs[h], r1s[h]) for h in heads]
    p_mats = [_dot_nt(xs[h], sms[h].astype(BF16)) for h in heads]
    lk_vs = [_dot(jnp.where(mask_top_k, a_mats[h][:c, :], 0.0).astype(BF16),
                  jnp.concatenate([zeros_cv, vs[h]], axis=0)) for h in heads]
    lps = [jnp.where(tj < ti, a_mats[h][:c, :c], 0.0) for h in heads]
    ts = [eye - lp for lp in lps]
    for _ in range(int(math.log2(c)) - 1):
        lpb = [lp.astype(BF16) for lp in lps]
        lps = [_dot(b, b) for b in lpb]
        ts = [_dot(ts[h].astype(BF16), (eye + lps[h]).astype(BF16)) for h in heads]
    us = [-_dot(ts[h].astype(BF16), (p_mats[h][:c, :] + lk_vs[h]).astype(BF16)) for h in heads]
    uvs = [jnp.concatenate([us[h].astype(BF16), vs[h]], axis=0) for h in heads]
    ys = [p_mats[h][c:, :] + _dot(jnp.where(mask_bot, a_mats[h][c:, :], 0.0).astype(BF16), uvs[h])
          for h in heads]
    for h in heads:
        s_ref[h] = (sms[h] + _dot_tn(uvs[h], r1s[h])) * e2[:, sls[h]]
    y = jnp.concatenate(ys, axis=1)

    inv_n = 1.0 / hd
    mean = _seg_sum(y, ones_bd) * inv_n
    yc = y - mean
    var = _seg_sum(yc * yc, ones_bd) * inv_n
    yn = yc * lax.rsqrt(var + GN_EPS) * gg_ref[...] + gb_ref[...]
    bonus = _seg_sum(r * k2 * rk_ref[...], ones_bd)
    yg_ref[...] = ((yn + bonus * v) * _silu(z_ref[0])).astype(BF16)

    @pl.when(t_idx == pl.num_programs(2) - 1)
    def _():
        sout_ref[0] = s_ref[...]


def wkv_prompt(rkvz, wl, al, k_k, k_a, r_k, gn_g, gn_b, n_batch):
    _, m, e = rkvz.shape
    p = m // n_batch
    nt = p // CHUNK
    hw = HEADS_PER_STEP * HEAD_DIM
    nh = e // HEAD_DIM
    row = lambda i, g, t: (i * nt + t, g)
    proj = lambda q: pl.BlockSpec((1, CHUNK, hw), lambda i, g, t: (q, i * nt + t, g))
    par = pl.BlockSpec((1, hw), lambda i, g, t: (0, g))
    return pl.pallas_call(
        _wkv_prompt_kernel,
        grid=(n_batch, e // hw, nt),
        in_specs=[proj(0), proj(1), proj(2), proj(3),
                  pl.BlockSpec((CHUNK, hw), row), pl.BlockSpec((CHUNK, hw), row),
                  par, par, par, par, par],
        out_specs=[pl.BlockSpec((CHUNK, hw), row),
                   pl.BlockSpec((1, HEADS_PER_STEP, HEAD_DIM, HEAD_DIM), lambda i, g, t: (i, g, 0, 0))],
        out_shape=[jax.ShapeDtypeStruct((m, e), BF16),
                   jax.ShapeDtypeStruct((n_batch, nh, HEAD_DIM, HEAD_DIM), F32)],
        scratch_shapes=[pltpu.VMEM((HEADS_PER_STEP, HEAD_DIM, HEAD_DIM), F32)],
        compiler_params=_cparams(("arbitrary", "arbitrary", "arbitrary")),
        name="wkv_prompt",
    )(rkvz, rkvz, rkvz, rkvz, wl, al, k_k.reshape(1, e), k_a.reshape(1, e), r_k.reshape(1, e),
      gn_g.reshape(1, e), gn_b.reshape(1, e))


def _wkv_sample_kernel(r_ref, k_ref, v_ref, z_ref, wl_ref, al_ref, kk_ref, ka_ref, rk_ref, gg_ref, gb_ref,
                       s_ref, yg_ref, sout_ref, y_scr):
    hd = HEAD_DIM
    r = r_ref[0]
    k = k_ref[0]
    v = v_ref[0]
    a = _sigmoid(al_ref[0])
    d = jnp.exp(-DECAY_SCALE * _sigmoid(wl_ref[0]))
    kk = k * kk_ref[...]
    kk = kk / jnp.maximum(jnp.sqrt(jnp.sum(kk * kk, axis=-1, keepdims=True)), 1e-12)
    k2 = k * (1.0 + (a - 1.0) * ka_ref[...])
    bb = kk * a
    nh = r.shape[0]
    ii = lax.broadcasted_iota(jnp.int32, (hd, hd), 0)
    jj = lax.broadcasted_iota(jnp.int32, (hd, hd), 1)
    eye = ii == jj

    for h in range(nh):
        row = lambda x: x[h:h + 1, :]
        s = s_ref[0, h]
        sa = jnp.sum(s * row(kk), axis=-1, keepdims=True)
        v_col = jnp.sum(jnp.where(eye, row(v), 0.0), axis=-1, keepdims=True)
        s_new = s * row(d) - sa * row(bb) + v_col * row(k2)
        sout_ref[0, h] = s_new
        y_col = jnp.sum(s_new * row(r), axis=-1, keepdims=True)
        y_scr[h:h + 1, :] = jnp.sum(jnp.where(eye, y_col, 0.0), axis=0, keepdims=True)

    y = y_scr[...]
    mean = jnp.mean(y, axis=-1, keepdims=True)
    yc = y - mean
    var = jnp.mean(yc * yc, axis=-1, keepdims=True)
    yn = yc * lax.rsqrt(var + GN_EPS) * gg_ref[...] + gb_ref[...]
    bonus = jnp.sum(r * k2 * rk_ref[...], axis=-1, keepdims=True)
    yg_ref[0] = ((yn + bonus * v) * _silu(z_ref[0])).astype(BF16)


def wkv_sample(rkvz, wl, al, k_k, k_a, r_k, gn_g, gn_b, state):
    _, m, e = rkvz.shape
    nh = e // HEAD_DIM
    hd = HEAD_DIM
    rkvz4 = rkvz.reshape(4, m, nh, hd)
    proj = lambda q: pl.BlockSpec((None, 1, nh, hd), lambda i: (q, i, 0, 0))
    tok = pl.BlockSpec((1, nh, hd), lambda i: (i, 0, 0))
    par = pl.BlockSpec((nh, hd), lambda i: (0, 0))
    st = pl.BlockSpec((1, nh, hd, hd), lambda i: (i, 0, 0, 0))
    as_heads = lambda x: x.reshape(nh, hd)
    return pl.pallas_call(
        _wkv_sample_kernel,
        grid=(m,),
        in_specs=[proj(0), proj(1), proj(2), proj(3), tok, tok, par, par, par, par, par, st],
        out_specs=[tok, st],
        out_shape=[jax.ShapeDtypeStruct((m, nh, hd), BF16), jax.ShapeDtypeStruct(state.shape, F32)],
        scratch_shapes=[pltpu.VMEM((nh, hd), F32)],
        compiler_params=_cparams(("arbitrary",)),
        name="wkv_sample",
    )(rkvz4, rkvz4, rkvz4, rkvz4, wl.reshape(m, nh, hd), al.reshape(m, nh, hd),
      as_heads(k_k), as_heads(k_a), as_heads(r_k), as_heads(gn_g), as_heads(gn_b), state)


def _mm_res_kernel(x_ref, w_ref, res_ref, o_ref):
    o_ref[...] = res_ref[...] + _dot(x_ref[...], w_ref[...])


def matmul_residual(x, w, res, tm, tn):
    m, kdim = x.shape
    n = w.shape[1]
    return pl.pallas_call(
        _mm_res_kernel,
        grid=(m // tm, n // tn),
        in_specs=[pl.BlockSpec((tm, kdim), lambda i, j: (i, 0)),
                  pl.BlockSpec((kdim, tn), lambda i, j: (0, j)),
                  pl.BlockSpec((tm, tn), lambda i, j: (i, j))],
        out_specs=pl.BlockSpec((tm, tn), lambda i, j: (i, j)),
        out_shape=jax.ShapeDtypeStruct((m, n), F32),
        compiler_params=_cparams(("arbitrary", "arbitrary")),
        name="matmul_residual",
    )(x, w, res)


def _rope(y, cos, sin_a, sin_b):
    tn = y.shape[1]
    rep = tn // cos.shape[1]
    tile = lambda t: jnp.concatenate([t] * rep, axis=1) if rep > 1 else t
    half = ROPE_DIM // 2
    return (y * tile(cos) + pltpu.roll(y, tn - half, axis=1) * tile(sin_a)
            + pltpu.roll(y, half, axis=1) * tile(sin_b))


def _norm_mm_kernel(h_ref, g_ref, w_ref, *rest, rope):
    if rope:
        cos_ref, sa_ref, sb_ref, o_ref, xn_ref = rest
    else:
        o_ref, xn_ref = rest

    @pl.when(pl.program_id(1) == 0)
    def _():
        x = h_ref[...]
        xn = x * lax.rsqrt(jnp.mean(x * x, axis=-1, keepdims=True) + RMS_EPS) * g_ref[...]
        xn_ref[...] = xn.astype(BF16)

    y = _dot(xn_ref[...], w_ref[...])
    if rope:
        y = _rope(y, cos_ref[...], sa_ref[...], sb_ref[...])
    o_ref[...] = y.astype(o_ref.dtype)


def norm_matmul(h, g, w, tm, tn, out_dtype, rope_tables=None):
    m, d = h.shape
    n = w.shape[1]
    in_specs = [pl.BlockSpec((tm, d), lambda i, j: (i, 0)),
                pl.BlockSpec((1, d), lambda i, j: (0, 0)),
                pl.BlockSpec((d, tn), lambda i, j: (0, j))]
    args = [h, g.reshape(1, d), w]
    if rope_tables is not None:
        lanes = rope_tables[0].shape[1]
        in_specs += [pl.BlockSpec((tm, lanes), lambda i, j: (i, 0))] * 3
        args += list(rope_tables)
    return pl.pallas_call(
        functools.partial(_norm_mm_kernel, rope=rope_tables is not None),
        grid=(m // tm, n // tn),
        in_specs=in_specs,
        out_specs=pl.BlockSpec((tm, tn), lambda i, j: (i, j)),
        out_shape=jax.ShapeDtypeStruct((m, n), out_dtype),
        scratch_shapes=[pltpu.VMEM((tm, d), BF16)],
        compiler_params=_cparams(("arbitrary", "arbitrary")),
        name="norm_matmul_rope" if rope_tables is not None else "norm_matmul",
    )(*args)


def rope_tables(pos):
    half = ROPE_DIM // 2
    inv_freq = ROPE_THETA ** (-jnp.arange(half, dtype=F32) * 2.0 / ROPE_DIM)
    ang = pos.astype(F32)[:, None] * inv_freq[None, :]
    cos = jnp.cos(ang)
    sin = jnp.sin(ang)
    rows = pos.shape[0]
    ones = jnp.ones((rows, HEAD_DIM - ROPE_DIM), F32)
    zeros_h = jnp.zeros((rows, half), F32)
    zeros_r = jnp.zeros((rows, HEAD_DIM - ROPE_DIM), F32)
    cos_h = jnp.concatenate([cos, cos, ones], axis=1)
    sa_h = jnp.concatenate([-sin, zeros_h, zeros_r], axis=1)
    sb_h = jnp.concatenate([zeros_h, sin, zeros_r], axis=1)
    two = lambda t: jnp.concatenate([t, t], axis=1)
    return two(cos_h), two(sa_h), two(sb_h)


def _attn_prompt_kernel(sink_ref, q_ref, kc_ref, kp_ref, vc_ref, vp_ref, z_ref, o_ref):
    n = pl.program_id(1)
    hd = HEAD_DIM
    blk = q_ref.shape[0]
    n_kv = kc_ref.shape[1] // hd
    grp = q_ref.shape[1] // (n_kv * hd)
    qi = lax.broadcasted_iota(jnp.int32, (blk, 2 * blk), 0)
    kj = lax.broadcasted_iota(jnp.int32, (blk, 2 * blk), 1) - blk
    kpos = n * blk + kj
    diff = qi - kj
    valid = (kpos >= LEAD) & (diff >= 0) & (diff <= WINDOW)
    k_all = jnp.concatenate([kp_ref[...], kc_ref[...]], axis=0).astype(BF16)
    v_all = jnp.concatenate([vp_ref[...], vc_ref[...]], axis=0).astype(BF16)
    scale = hd ** -0.5

    def scores(h):
        k_h = k_all[:, h * hd:(h + 1) * hd]
        return [_dot_nt(q_ref[:, (h * grp + gi) * hd:(h * grp + gi + 1) * hd], k_h) for gi in range(grp)]

    outs = []
    s_next = scores(0)
    for h in range(n_kv):
        s_cur = s_next
        if h + 1 < n_kv:
            s_next = scores(h + 1)
        v_h = v_all[:, h * hd:(h + 1) * hd]
        ps, dens = [], []
        for gi in range(grp):
            s = jnp.where(valid, s_cur[gi] * scale, -jnp.inf)
            sk = sink_ref[h * grp + gi]
            m = jnp.maximum(jnp.max(s, axis=-1, keepdims=True), sk)
            p = jnp.exp(s - m)
            dens.append(jnp.sum(p, axis=-1, keepdims=True) + jnp.exp(sk - m))
            ps.append(p.astype(BF16))
        outs += [_dot(ps[gi], v_h) / dens[gi] for gi in range(grp)]
    att = jnp.concatenate(outs, axis=1)
    o_ref[...] = (att * _silu(z_ref[...])).astype(o_ref.dtype)


def attn_prompt(sinks, q, k, v, z, n_batch):
    m, e = q.shape
    nb = m // (n_batch * BLOCK)
    kw = k.shape[1]
    cur = lambda i, n: (i * nb + n, 0)
    prv = lambda i, n: (i * nb + jnp.maximum(n - 1, 0), 0)
    return pl.pallas_call(
        _attn_prompt_kernel,
        grid=(n_batch, nb),
        in_specs=[pl.BlockSpec(memory_space=pltpu.SMEM),
                  pl.BlockSpec((BLOCK, e), cur),
                  pl.BlockSpec((BLOCK, kw), cur), pl.BlockSpec((BLOCK, kw), prv),
                  pl.BlockSpec((BLOCK, kw), cur), pl.BlockSpec((BLOCK, kw), prv),
                  pl.BlockSpec((BLOCK, e), cur)],
        out_specs=pl.BlockSpec((BLOCK, e), cur),
        out_shape=jax.ShapeDtypeStruct((m, e), BF16),
        compiler_params=_cparams(("arbitrary", "arbitrary")),
        name="attn_prompt",
    )(sinks, q, k, k, v, v, z)


def _attn_sample_kernel(sink_ref, q_ref, kc_ref, vc_ref, kn_ref, vn_ref, z_ref, o_ref, ko_ref, vo_ref):
    hd = HEAD_DIM
    win = kc_ref.shape[1]
    n_kv = kc_ref.shape[2] // hd
    nq = q_ref.shape[1]
    grp = nq // n_kv
    pad = 8
    kc = kc_ref[0]
    vc = vc_ref[0]
    kn = kn_ref[0]
    vn = vn_ref[0]
    first = lax.broadcasted_iota(jnp.int32, (pad, kc.shape[1]), 0) == 0
    k_all = jnp.concatenate([kc, jnp.where(first, kn, 0.0)], axis=0).astype(BF16)
    v_all = jnp.concatenate([vc, jnp.where(first, vn, 0.0)], axis=0).astype(BF16)
    col = lax.broadcasted_iota(jnp.int32, (grp, win + pad), 1)
    valid = (col <= win) & (win - col <= WINDOW)
    q = q_ref[0].astype(BF16)
    scale = hd ** -0.5
    row_i = lax.broadcasted_iota(jnp.int32, (grp, 1), 0)
    outs = []
    for h in range(n_kv):
        k_h = k_all[:, h * hd:(h + 1) * hd]
        v_h = v_all[:, h * hd:(h + 1) * hd]
        q_h = q[h * grp:(h + 1) * grp, :]
        s = _dot_nt(q_h, k_h) * scale
        s = jnp.where(valid, s, -jnp.inf)
        sk = jnp.zeros((grp, 1), F32)
        for gi in range(grp):
            sk = jnp.where(row_i == gi, sink_ref[h * grp + gi], sk)
        m = jnp.maximum(jnp.max(s, axis=-1, keepdims=True), sk)
        p = jnp.exp(s - m)
        den = jnp.sum(p, axis=-1, keepdims=True) + jnp.exp(sk - m)
        outs.append(_dot(p.astype(BF16), v_h) / den)
    att = jnp.concatenate(outs, axis=0)
    o_ref[0] = (att * _silu(z_ref[0])).astype(o_ref.dtype)
    last = lax.broadcasted_iota(jnp.int32, kc.shape, 0) == win - 1
    ko_ref[0] = jnp.where(last, kn, pltpu.roll(kc, win - 1, axis=0))
    vo_ref[0] = jnp.where(last, vn, pltpu.roll(vc, win - 1, axis=0))


def attn_sample(sinks, q, cache_k, cache_v, k_new, v_new, z):
    m, win, kw = cache_k.shape
    nq = q.shape[1]
    hd = HEAD_DIM
    tok = pl.BlockSpec((1, nq, hd), lambda i: (i, 0, 0))
    cache = pl.BlockSpec((1, win, kw), lambda i: (i, 0, 0))
    new = pl.BlockSpec((1, 1, kw), lambda i: (i, 0, 0))
    return pl.pallas_call(
        _attn_sample_kernel,
        grid=(m,),
        in_specs=[pl.BlockSpec(memory_space=pltpu.SMEM), tok, cache, cache, new, new, tok],
        out_specs=[tok, cache, cache],
        out_shape=[jax.ShapeDtypeStruct((m, nq, hd), BF16),
                   jax.ShapeDtypeStruct(cache_k.shape, F32), jax.ShapeDtypeStruct(cache_v.shape, F32)],
        compiler_params=_cparams(("arbitrary",)),
        name="attn_sample",
    )(sinks, q, cache_k, cache_v, k_new, v_new, z)


def _rmsnorm_kernel(x_ref, g_ref, o_ref):
    x = x_ref[0]
    o_ref[0] = x * lax.rsqrt(jnp.mean(x * x, axis=-1, keepdims=True) + RMS_EPS) * g_ref[...]


def final_norm_prompt(h3, g, skip_blocks, tm):
    b, p, d = h3.shape
    n_out = p - skip_blocks * tm
    return pl.pallas_call(
        _rmsnorm_kernel,
        grid=(b, n_out // tm),
        in_specs=[pl.BlockSpec((1, tm, d), lambda i, t: (i, t + skip_blocks, 0)),
                  pl.BlockSpec((1, d), lambda i, t: (0, 0))],
        out_specs=pl.BlockSpec((1, tm, d), lambda i, t: (i, t, 0)),
        out_shape=jax.ShapeDtypeStruct((b, n_out, d), F32),
        compiler_params=_cparams(("arbitrary", "arbitrary")),
        name="final_norm_prompt",
    )(h3, g.reshape(1, d))


def final_norm_sample(h, g):
    m, d = h.shape
    return pl.pallas_call(
        _rmsnorm_kernel,
        out_shape=jax.ShapeDtypeStruct((1, m, d), F32),
        name="final_norm_sample",
    )(h.reshape(1, m, d), g.reshape(1, d))[0]


def _pad_lora(w_down, w_up):
    r = w_down.shape[1]
    return (jnp.pad(w_down, ((0, 0), (0, LORA_PAD - r))).astype(BF16),
            jnp.pad(w_up, ((0, LORA_PAD - r), (0, 0))).astype(BF16))


def kernel(x_prompt, x_sample, state_wkv, state_shift, cache_k, cache_v, meta_tokens, a_norm, a_mu, a_w_rkvz,
           a_w0, a_w1, a_w2, a_a0, a_a1, a_a2, a_k_k, a_k_a, a_r_k, a_gn_g, a_gn_b, a_w_out, kv_norm, w_kv,
           b_norm, b_w_qz, b_sinks, b_w_o, final_norm):
    nb, seq, d = x_prompt.shape
    db, dseq, _ = x_sample.shape
    assert dseq == 1 and a_norm.shape[0] == 1 and b_norm.shape[0] == 1
    e = a_w_rkvz.shape[3]
    win = cache_k.shape[1]
    p_len = LEAD + N_META + seq
    assert p_len % BLOCK == 0 and (LEAD + N_META) == BLOCK
    m_p = nb * p_len
    kvw = N_KV_HEADS * HEAD_DIM

    w_rkvz = a_w_rkvz[0].astype(BF16)
    w1, w2 = _pad_lora(a_w1[0], a_w2[0])
    a1, a2 = _pad_lora(a_a1[0], a_a2[0])
    w_out = a_w_out[0].astype(BF16)
    w_k = w_kv[:, :kvw].astype(BF16)
    w_v = w_kv[:, kvw:].astype(BF16)
    w_q = b_w_qz[0][:, :e].astype(BF16)
    w_z = b_w_qz[0][:, e:].astype(BF16)
    w_o = b_w_o[0].astype(BF16)
    mu = a_mu[0]
    sinks = b_sinks[0]

    tm_p = p_len // 4
    tn = 512

    hp = jnp.concatenate([jnp.zeros((nb, LEAD, d), F32),
                          jnp.broadcast_to(meta_tokens[None], (nb, N_META, d)), x_prompt], axis=1)
    xn, xx = norm_shift_prompt(hp, a_norm[0], p_len // 8)
    p_state_shift = xn[:, -1][None]
    xn = xn.reshape(m_p, d)
    xx = xx.reshape(m_p, d)
    rkvz = rkvz_proj(xn, xx, mu[:4], w_rkvz, tm_p, tn)
    wl, al = lora_proj(xn, xx, mu[4:6], w1, w2, a_w0[0], a1, a2, a_a0[0], p_len // 8)
    yg, p_state = wkv_prompt(rkvz, wl, al, a_k_k[0], a_k_a[0], a_r_k[0].reshape(-1), a_gn_g[0], a_gn_b[0], nb)
    hp = matmul_residual(yg, w_out, hp.reshape(m_p, d), tm_p, tn)

    pos_p = jnp.maximum(jnp.arange(p_len, dtype=jnp.int32) - LEAD, 0)
    tabs_p = tuple(jnp.tile(t, (nb, 1)) for t in rope_tables(pos_p))
    k_p = norm_matmul(hp, kv_norm, w_k, tm_p, tn, F32, tabs_p)
    v_p = norm_matmul(hp, kv_norm, w_v, tm_p, tn, F32)
    q_p = norm_matmul(hp, b_norm[0], w_q, tm_p, tn, BF16, tabs_p)
    z_p = norm_matmul(hp, b_norm[0], w_z, tm_p, tn, F32)
    att = attn_prompt(sinks, q_p, k_p, v_p, z_p, nb)
    hp = matmul_residual(att, w_o, hp, tm_p, tn)
    y_prompt = final_norm_prompt(hp.reshape(nb, p_len, d), final_norm, (LEAD + N_META) // BLOCK, BLOCK)
    p_cache_k = k_p.reshape(nb, p_len, N_KV_HEADS, HEAD_DIM)[:, -win:]
    p_cache_v = v_p.reshape(nb, p_len, N_KV_HEADS, HEAD_DIM)[:, -win:]

    hs = x_sample.reshape(db, d)
    xn_s, xx_s = norm_shift_sample(hs, state_shift[0], a_norm[0])
    rkvz_s = rkvz_proj(xn_s, xx_s, mu[:4], w_rkvz, db, tn)
    wl_s, al_s = lora_proj(xn_s, xx_s, mu[4:6], w1, w2, a_w0[0], a1, a2, a_a0[0], db)
    yg_s, s_state = wkv_sample(rkvz_s, wl_s, al_s, a_k_k[0], a_k_a[0], a_r_k[0].reshape(-1), a_gn_g[0],
                               a_gn_b[0], state_wkv[0])
    hs = matmul_residual(yg_s.reshape(db, e), w_out, hs, db, tn)
    pos_s = jnp.full((db,), PAST_LEN, jnp.int32)
    tabs_s = rope_tables(pos_s)
    k_s = norm_matmul(hs, kv_norm, w_k, db, tn, F32, tabs_s)
    v_s = norm_matmul(hs, kv_norm, w_v, db, tn, F32)
    q_s = norm_matmul(hs, b_norm[0], w_q, db, tn, F32, tabs_s)
    z_s = norm_matmul(hs, b_norm[0], w_z, db, tn, F32)
    nq = e // HEAD_DIM
    att_s, s_cache_k, s_cache_v = attn_sample(
        sinks, q_s.reshape(db, nq, HEAD_DIM), cache_k.reshape(db, win, kvw), cache_v.reshape(db, win, kvw),
        k_s.reshape(db, 1, kvw), v_s.reshape(db, 1, kvw), z_s.reshape(db, nq, HEAD_DIM))
    hs = matmul_residual(att_s.reshape(db, e), w_o, hs, db, tn)
    y_sample = final_norm_sample(hs, final_norm).reshape(db, 1, d)

    return (y_prompt, y_sample, p_state[None], p_state_shift,
            p_cache_k, p_cache_v,
            s_state[None], xn_s[None],
            s_cache_k.reshape(cache_k.shape), s_cache_v.reshape(cache_v.shape))
```

```python
import functools
import math

import jax
import jax.numpy as jnp
from jax import lax
from jax.experimental import pallas as pl
from jax.experimental.pallas import tpu as pltpu

F32 = jnp.float32
BF16 = jnp.bfloat16

HEAD_DIM = 64
N_KV_HEADS = 8
WINDOW = 128
BLOCK = 128
ROPE_DIM = HEAD_DIM // 4
ROPE_THETA = 500000.0
N_META = 16
PAST_LEN = 16384
RMS_EPS = 1e-6
GN_EPS = 64e-5
LEAD = (-N_META) % BLOCK
CHUNK = 64
WKV_ROWS = 128
HEADS_PER_STEP = 16
LORA_PAD = 128
MXU_TILE = 256
ROPE_SLAB = 512
VMEM_LIMIT = 48 * 1024 * 1024
DECAY_SCALE = math.exp(-0.5)


def _cparams(sem):
    return pltpu.CompilerParams(dimension_semantics=sem, vmem_limit_bytes=VMEM_LIMIT)


def _sigmoid(x):
    return 1.0 / (1.0 + jnp.exp(-x))


def _silu(x):
    return x * _sigmoid(x)


def _dot(a, b):
    return jnp.dot(a, b, preferred_element_type=F32)


def _dot_nt(a, b):
    return lax.dot_general(a, b, (((1,), (1,)), ((), ())), preferred_element_type=F32)


def _dot_tn(a, b):
    return lax.dot_general(a, b, (((0,), (0,)), ((), ())), preferred_element_type=F32)


def _split_hi_lo(x):
    hi = x.astype(BF16)
    lo = (x - hi.astype(F32)).astype(BF16)
    return hi, lo


def _mixes(xn, prev, mu_ref, xm_ref):
    xx = prev - xn
    for p in range(mu_ref.shape[0]):
        xm_ref[p] = (xn + xx * mu_ref[p:p + 1, :]).astype(xm_ref.dtype)


def _norm_shift_kernel(x_ref, g_ref, mu_ref, xm_ref, last_ref, carry_ref):
    @pl.when(pl.program_id(1) == 0)
    def _():
        carry_ref[...] = jnp.zeros_like(carry_ref)

    x = x_ref[0]
    tm = x.shape[0]
    xn = x * lax.rsqrt(jnp.mean(x * x, axis=-1, keepdims=True) + RMS_EPS) * g_ref[...]
    rolled = pltpu.roll(xn, 1, axis=0)
    row = lax.broadcasted_iota(jnp.int32, xn.shape, 0)
    prev = jnp.where(row == 0, carry_ref[0:1, :], rolled)
    _mixes(xn, prev, mu_ref, xm_ref)
    carry_ref[0:1, :] = xn[tm - 1:tm, :]
    last_ref[0] = xn[tm - 1:tm, :]


def norm_shift_prompt(x, g, mu, tm):
    b, p, d = x.shape
    n_mix = mu.shape[0]
    nt = p // tm
    return pl.pallas_call(
        _norm_shift_kernel,
        grid=(b, nt),
        in_specs=[pl.BlockSpec((1, tm, d), lambda i, t: (i, t, 0)),
                  pl.BlockSpec((1, d), lambda i, t: (0, 0)),
                  pl.BlockSpec((n_mix, d), lambda i, t: (0, 0))],
        out_specs=[pl.BlockSpec((n_mix, tm, d), lambda i, t: (0, i * nt + t, 0)),
                   pl.BlockSpec((1, 1, d), lambda i, t: (i, 0, 0))],
        out_shape=[jax.ShapeDtypeStruct((n_mix, b * p, d), BF16), jax.ShapeDtypeStruct((b, 1, d), F32)],
        scratch_shapes=[pltpu.VMEM((8, d), F32)],
        compiler_params=_cparams(("arbitrary", "arbitrary")),
        name="norm_shift_prompt",
    )(x, g.reshape(1, d), mu)


def _norm_shift_sample_kernel(x_ref, prev_ref, g_ref, mu_ref, xm_ref, xn_ref):
    x = x_ref[...]
    xn = x * lax.rsqrt(jnp.mean(x * x, axis=-1, keepdims=True) + RMS_EPS) * g_ref[...]
    xn_ref[...] = xn
    _mixes(xn, prev_ref[...], mu_ref, xm_ref)


def norm_shift_sample(x, prev, g, mu):
    m, d = x.shape
    return pl.pallas_call(
        _norm_shift_sample_kernel,
        out_shape=[jax.ShapeDtypeStruct((mu.shape[0], m, d), BF16), jax.ShapeDtypeStruct((m, d), F32)],
        name="norm_shift_sample",
    )(x, prev, g.reshape(1, d), mu)


def _rope(y, cos, sin_a, sin_b):
    half = ROPE_DIM // 2
    step = ROPE_SLAB
    rep = step // cos.shape[1]
    tile = lambda t: jnp.concatenate([t] * rep, axis=1)
    cos_t, sa_t, sb_t = tile(cos), tile(sin_a), tile(sin_b)
    outs = []
    for j in range(y.shape[1] // step):
        ys = y[:, j * step:(j + 1) * step]
        outs.append(ys * cos_t + pltpu.roll(ys, step - half, axis=1) * sa_t + pltpu.roll(ys, half, axis=1) * sb_t)
    return jnp.concatenate(outs, axis=1) if len(outs) > 1 else outs[0]


def _mm_group_kernel(x_ref, w_ref, o_ref):
    o_ref[0] = _dot(x_ref[0], w_ref[0]).astype(o_ref.dtype)


def matmul_groups(x, w, tm, out_dtype):
    g, m, kdim = x.shape
    n = w.shape[2]
    return pl.pallas_call(
        _mm_group_kernel,
        grid=(g, m // tm),
        in_specs=[pl.BlockSpec((1, tm, kdim), lambda q, i: (q, i, 0)),
                  pl.BlockSpec((1, kdim, n), lambda q, i: (q, 0, 0))],
        out_specs=pl.BlockSpec((1, tm, n), lambda q, i: (q, i, 0)),
        out_shape=jax.ShapeDtypeStruct((g, m, n), out_dtype),
        compiler_params=_cparams(("arbitrary", "arbitrary")),
        name="matmul_groups",
    )(x, w)


def _mm_rope_kernel(x_ref, w_ref, cos_ref, sa_ref, sb_ref, *o_refs, n_rope):
    y = _dot(x_ref[...], w_ref[...])
    rot = _rope(y[:, :n_rope], cos_ref[...], sa_ref[...], sb_ref[...])
    o_refs[0][...] = rot.astype(o_refs[0].dtype)
    if len(o_refs) > 1:
        o_refs[1][...] = y[:, n_rope:].astype(o_refs[1].dtype)


def matmul_rope(x, w, tables, tm, n_rope, out_dtypes):
    m, kdim = x.shape
    n = w.shape[1]
    lanes = tables[0].shape[1]
    widths = [n_rope] + ([n - n_rope] if n > n_rope else [])
    tab = pl.BlockSpec((tm, lanes), lambda i: (i, 0))
    outs = pl.pallas_call(
        functools.partial(_mm_rope_kernel, n_rope=n_rope),
        grid=(m // tm,),
        in_specs=[pl.BlockSpec((tm, kdim), lambda i: (i, 0)),
                  pl.BlockSpec((kdim, n), lambda i: (0, 0), pipeline_mode=pl.Buffered(1)),
                  tab, tab, tab],
        out_specs=[pl.BlockSpec((tm, wd), lambda i: (i, 0)) for wd in widths],
        out_shape=[jax.ShapeDtypeStruct((m, wd), dt) for wd, dt in zip(widths, out_dtypes)],
        compiler_params=_cparams(("arbitrary",)),
        name="matmul_rope",
    )(x, w, *tables)
    return outs


def _mm_res_norm_kernel(x_ref, w_ref, res_ref, g_ref, *out_refs, emit_h):
    h = res_ref[...] + _dot(x_ref[...], w_ref[...])
    hn_refs = out_refs
    if emit_h:
        out_refs[0][...] = h
        hn_refs = out_refs[1:]
    inv = lax.rsqrt(jnp.mean(h * h, axis=-1, keepdims=True) + RMS_EPS)
    for j, hn_ref in enumerate(hn_refs):
        hn_ref[...] = (h * inv * g_ref[j:j + 1, :]).astype(hn_ref.dtype)


def matmul_residual_norm(x, w, res, gains, tm, norm_dtype, emit_h=True):
    m, kdim = x.shape
    n = w.shape[1]
    ng = gains.shape[0]
    row = lambda width: pl.BlockSpec((tm, width), lambda i: (i, 0))
    return pl.pallas_call(
        functools.partial(_mm_res_norm_kernel, emit_h=emit_h),
        grid=(m // tm,),
        in_specs=[row(kdim),
                  pl.BlockSpec((kdim, n), lambda i: (0, 0), pipeline_mode=pl.Buffered(1)),
                  row(n),
                  pl.BlockSpec((ng, n), lambda i: (0, 0))],
        out_specs=[row(n)] * (int(emit_h) + ng),
        out_shape=[jax.ShapeDtypeStruct((m, n), F32)] * int(emit_h) + [jax.ShapeDtypeStruct((m, n), norm_dtype)] * ng,
        compiler_params=_cparams(("arbitrary",)),
        name="matmul_residual_norm",
    )(x, w, res, gains)


def _lora_kernel(xw_ref, xa_ref, w1_ref, w2_ref, w0_ref, a1_ref, a2_ref, a0_ref, wl_ref, al_ref):
    hw = jnp.tanh(_dot(xw_ref[0], w1_ref[...])).astype(BF16)
    wl_ref[...] = w0_ref[...] + _dot(hw, w2_ref[...])
    ha = _dot(xa_ref[0], a1_ref[...]).astype(BF16)
    al_ref[...] = a0_ref[...] + _dot(ha, a2_ref[...])


def lora_proj(xm, i_w, i_a, w1, w2, w0, a1, a2, a0, tm):
    _, m, d = xm.shape
    e = w2.shape[1]
    lr = w1.shape[1]
    full = lambda shape: pl.BlockSpec(shape, lambda i: (0,) * len(shape))
    return pl.pallas_call(
        _lora_kernel,
        grid=(m // tm,),
        in_specs=[pl.BlockSpec((1, tm, d), lambda i: (i_w, i, 0)),
                  pl.BlockSpec((1, tm, d), lambda i: (i_a, i, 0)),
                  full((d, lr)), full((lr, e)), full((1, e)),
                  full((d, lr)), full((lr, e)), full((1, e))],
        out_specs=[pl.BlockSpec((tm, e), lambda i: (i, 0)), pl.BlockSpec((tm, e), lambda i: (i, 0))],
        out_shape=[jax.ShapeDtypeStruct((m, e), F32), jax.ShapeDtypeStruct((m, e), F32)],
        compiler_params=_cparams(("arbitrary",)),
        name="lora_proj",
    )(xm, xm, w1, w2, w0.reshape(1, e), a1, a2, a0.reshape(1, e))


def _seg_sum(x, ones_bd):
    hi, lo = _split_hi_lo(x)
    outs = []
    for c in range(x.shape[1] // MXU_TILE):
        sl = slice(c * MXU_TILE, (c + 1) * MXU_TILE)
        outs.append(_dot(hi[:, sl], ones_bd) + _dot(lo[:, sl], ones_bd))
    return jnp.concatenate(outs, axis=1) if len(outs) > 1 else outs[0]


def _wkv_prompt_kernel(r_ref, k_ref, v_ref, z_ref, wl_ref, al_ref, kk_ref, ka_ref, rk_ref, gg_ref, gb_ref,
                       yg_ref, sout_ref, s_ref):
    t_idx = pl.program_id(2)
    c = CHUNK
    hd = HEAD_DIM

    @pl.when(t_idx == 0)
    def _():
        s_ref[...] = jnp.zeros_like(s_ref)

    r = r_ref[0]
    k = k_ref[0]
    v = v_ref[0]
    tb, hw = r.shape
    nh = hw // hd
    nc = tb // c

    li = lax.broadcasted_iota(jnp.int32, (MXU_TILE, MXU_TILE), 0) // hd
    lj = lax.broadcasted_iota(jnp.int32, (MXU_TILE, MXU_TILE), 1) // hd
    ones_bd = jnp.where(li == lj, 1.0, 0.0).astype(BF16)
    bi_ = lax.broadcasted_iota(jnp.int32, (tb, tb), 0)
    bj_ = lax.broadcasted_iota(jnp.int32, (tb, tb), 1)
    tri_incl = jnp.where((bj_ <= bi_) & (bj_ // c == bi_ // c), 1.0, 0.0).astype(BF16)
    ti = lax.broadcasted_iota(jnp.int32, (c, c), 0)
    tj = lax.broadcasted_iota(jnp.int32, (c, c), 1)
    eye = jnp.where(ti == tj, 1.0, 0.0).astype(F32)
    ai = lax.broadcasted_iota(jnp.int32, (c, 2 * c), 0)
    aj = lax.broadcasted_iota(jnp.int32, (c, 2 * c), 1)
    aj_mod = jnp.where(aj >= c, aj - c, aj)
    mask_top_k = (aj >= c) & (aj_mod < ai)
    mask_bot = aj_mod <= ai

    a = _sigmoid(al_ref[...])
    lw = -DECAY_SCALE * _sigmoid(wl_ref[...])
    kk = k * kk_ref[...]
    n2 = _seg_sum(kk * kk, ones_bd)
    kk = kk / jnp.maximum(jnp.sqrt(n2), 1e-12)
    k2 = k * (1.0 + (a - 1.0) * ka_ref[...])
    bb = kk * a

    lw_hi, lw_lo = _split_hi_lo(lw)
    g = _dot(tri_incl, lw_hi) + _dot(tri_incl, lw_lo)
    mid = lambda ci: g[ci * c + c // 2 - 1:ci * c + c // 2, :]
    gm = jnp.concatenate([jnp.broadcast_to(mid(ci), (c, hw)) for ci in range(nc)], axis=0)
    e_a = jnp.exp(g - gm)
    e_prev = jnp.exp(g - lw - gm)
    e_inv = jnp.exp(gm - g)
    e1 = [jnp.exp(mid(ci)) for ci in range(nc)]
    e2 = [jnp.exp(g[ci * c + c - 1:ci * c + c, :] - mid(ci)) for ci in range(nc)]

    kkd = (kk * e_prev).astype(BF16)
    rd = (r * e_a).astype(BF16)
    bi = (bb * e_inv).astype(BF16)
    ki = (k2 * e_inv).astype(BF16)
    v_bf = v.astype(BF16)
    zeros_cv = jnp.zeros((c, hd), BF16)

    pairs = [(ci, h) for ci in range(nc) for h in range(nh)]
    rows = lambda ci: slice(ci * c, (ci + 1) * c)
    cols = lambda h: slice(h * hd, (h + 1) * hd)
    xs = {(ci, h): jnp.concatenate([kkd[rows(ci), cols(h)], rd[rows(ci), cols(h)]], axis=0) for ci, h in pairs}
    r1s = {(ci, h): jnp.concatenate([bi[rows(ci), cols(h)], ki[rows(ci), cols(h)]], axis=0) for ci, h in pairs}
    vs = {(ci, h): v_bf[rows(ci), cols(h)] for ci, h in pairs}
    a_mats = {p: _dot_nt(xs[p], r1s[p]) for p in pairs}
    lk_vs = {p: _dot(jnp.where(mask_top_k, a_mats[p][:c, :], 0.0).astype(BF16),
                     jnp.concatenate([zeros_cv, vs[p]], axis=0)) for p in pairs}
    lps = {p: jnp.where(tj < ti, a_mats[p][:c, :c], 0.0) for p in pairs}
    ts = {p: eye - lps[p] for p in pairs}
    for _ in range(int(math.log2(c)) - 1):
        lpb = {p: lps[p].astype(BF16) for p in pairs}
        lps = {p: _dot(lpb[p], lpb[p]) for p in pairs}
        ts = {p: _dot(ts[p].astype(BF16), (eye + lps[p]).astype(BF16)) for p in pairs}
    a_bots = {p: jnp.where(mask_bot, a_mats[p][c:, :], 0.0).astype(BF16) for p in pairs}
    t_bf = {p: ts[p].astype(BF16) for p in pairs}

    state = [s_ref[h] for h in range(nh)]
    y_rows = []
    for ci in range(nc):
        hs = range(nh)
        sms = [state[h] * e1[ci][:, cols(h)] for h in hs]
        p_mats = [_dot_nt(xs[ci, h], sms[h].astype(BF16)) for h in hs]
        us = [-_dot(t_bf[ci, h], (p_mats[h][:c, :] + lk_vs[ci, h]).astype(BF16)) for h in hs]
        uvs = [jnp.concatenate([us[h].astype(BF16), vs[ci, h]], axis=0) for h in hs]
        ys = [p_mats[h][c:, :] + _dot(a_bots[ci, h], uvs[h]) for h in hs]
        state = [(sms[h] + _dot_tn(uvs[h], r1s[ci, h])) * e2[ci][:, cols(h)] for h in hs]
        y_rows.append(jnp.concatenate(ys, axis=1))
    for h in range(nh):
        s_ref[h] = state[h]
    y = jnp.concatenate(y_rows, axis=0) if nc > 1 else y_rows[0]

    inv_n = 1.0 / hd
    mean = _seg_sum(y, ones_bd) * inv_n
    yc = y - mean
    var = _seg_sum(yc * yc, ones_bd) * inv_n
    yn = yc * lax.rsqrt(var + GN_EPS) * gg_ref[...] + gb_ref[...]
    bonus = _seg_sum(r * k2 * rk_ref[...], ones_bd)
    yg_ref[...] = ((yn + bonus * v) * _silu(z_ref[0])).astype(BF16)

    @pl.when(t_idx == pl.num_programs(2) - 1)
    def _():
        sout_ref[0] = s_ref[...]


def wkv_prompt(rkvz, wl, al, k_k, k_a, r_k, gn_g, gn_b, n_batch):
    _, m, e = rkvz.shape
    p = m // n_batch
    tb = WKV_ROWS
    nt = p // tb
    hw = HEADS_PER_STEP * HEAD_DIM
    nh = e // HEAD_DIM
    row = lambda i, g, t: (i * nt + t, g)
    proj = lambda q: pl.BlockSpec((1, tb, hw), lambda i, g, t: (q, i * nt + t, g))
    par = pl.BlockSpec((1, hw), lambda i, g, t: (0, g))
    return pl.pallas_call(
        _wkv_prompt_kernel,
        grid=(n_batch, e // hw, nt),
        in_specs=[proj(0), proj(1), proj(2), proj(3),
                  pl.BlockSpec((tb, hw), row), pl.BlockSpec((tb, hw), row),
                  par, par, par, par, par],
        out_specs=[pl.BlockSpec((tb, hw), row),
                   pl.BlockSpec((1, HEADS_PER_STEP, HEAD_DIM, HEAD_DIM), lambda i, g, t: (i, g, 0, 0))],
        out_shape=[jax.ShapeDtypeStruct((m, e), BF16),
                   jax.ShapeDtypeStruct((n_batch, nh, HEAD_DIM, HEAD_DIM), F32)],
        scratch_shapes=[pltpu.VMEM((HEADS_PER_STEP, HEAD_DIM, HEAD_DIM), F32)],
        compiler_params=_cparams(("arbitrary", "arbitrary", "arbitrary")),
        name="wkv_prompt",
    )(rkvz, rkvz, rkvz, rkvz, wl, al, k_k.reshape(1, e), k_a.reshape(1, e), r_k.reshape(1, e),
      gn_g.reshape(1, e), gn_b.reshape(1, e))


def _wkv_sample_kernel(r_ref, k_ref, v_ref, z_ref, wl_ref, al_ref, kk_ref, ka_ref, rk_ref, gg_ref, gb_ref,
                       s_ref, yg_ref, sout_ref, y_scr):
    hd = HEAD_DIM
    r = r_ref[0]
    k = k_ref[0]
    v = v_ref[0]
    a = _sigmoid(al_ref[0])
    d = jnp.exp(-DECAY_SCALE * _sigmoid(wl_ref[0]))
    kk = k * kk_ref[...]
    kk = kk / jnp.maximum(jnp.sqrt(jnp.sum(kk * kk, axis=-1, keepdims=True)), 1e-12)
    k2 = k * (1.0 + (a - 1.0) * ka_ref[...])
    bb = kk * a
    nh = r.shape[0]
    ii = lax.broadcasted_iota(jnp.int32, (hd, hd), 0)
    jj = lax.broadcasted_iota(jnp.int32, (hd, hd), 1)
    eye = ii == jj

    for h in range(nh):
        row = lambda x: x[h:h + 1, :]
        s = s_ref[0, h]
        sa = jnp.sum(s * row(kk), axis=-1, keepdims=True)
        v_col = jnp.sum(jnp.where(eye, row(v), 0.0), axis=-1, keepdims=True)
        s_new = s * row(d) - sa * row(bb) + v_col * row(k2)
        sout_ref[0, h] = s_new
        y_col = jnp.sum(s_new * row(r), axis=-1, keepdims=True)
        y_scr[h:h + 1, :] = jnp.sum(jnp.where(eye, y_col, 0.0), axis=0, keepdims=True)

    y = y_scr[...]
    mean = jnp.mean(y, axis=-1, keepdims=True)
    yc = y - mean
    var = jnp.mean(yc * yc, axis=-1, keepdims=True)
    yn = yc * lax.rsqrt(var + GN_EPS) * gg_ref[...] + gb_ref[...]
    bonus = jnp.sum(r * k2 * rk_ref[...], axis=-1, keepdims=True)
    yg_ref[0] = ((yn + bonus * v) * _silu(z_ref[0])).astype(BF16)


def wkv_sample(rkvz, wl, al, k_k, k_a, r_k, gn_g, gn_b, state):
    _, m, e = rkvz.shape
    nh = e // HEAD_DIM
    hd = HEAD_DIM
    rkvz4 = rkvz.reshape(4, m, nh, hd)
    proj = lambda q: pl.BlockSpec((None, 1, nh, hd), lambda i: (q, i, 0, 0))
    tok = pl.BlockSpec((1, nh, hd), lambda i: (i, 0, 0))
    par = pl.BlockSpec((nh, hd), lambda i: (0, 0))
    st = pl.BlockSpec((1, nh, hd, hd), lambda i: (i, 0, 0, 0))
    as_heads = lambda x: x.reshape(nh, hd)
    return pl.pallas_call(
        _wkv_sample_kernel,
        grid=(m,),
        in_specs=[proj(0), proj(1), proj(2), proj(3), tok, tok, par, par, par, par, par, st],
        out_specs=[tok, st],
        out_shape=[jax.ShapeDtypeStruct((m, nh, hd), BF16), jax.ShapeDtypeStruct(state.shape, F32)],
        scratch_shapes=[pltpu.VMEM((nh, hd), F32)],
        compiler_params=_cparams(("arbitrary",)),
        name="wkv_sample",
    )(rkvz4, rkvz4, rkvz4, rkvz4, wl.reshape(m, nh, hd), al.reshape(m, nh, hd),
      as_heads(k_k), as_heads(k_a), as_heads(r_k), as_heads(gn_g), as_heads(gn_b), state)


def rope_tables(pos):
    half = ROPE_DIM // 2
    inv_freq = ROPE_THETA ** (-jnp.arange(half, dtype=F32) * 2.0 / ROPE_DIM)
    ang = pos.astype(F32)[:, None] * inv_freq[None, :]
    cos = jnp.cos(ang)
    sin = jnp.sin(ang)
    rows = pos.shape[0]
    ones = jnp.ones((rows, HEAD_DIM - ROPE_DIM), F32)
    zeros_h = jnp.zeros((rows, half), F32)
    zeros_r = jnp.zeros((rows, HEAD_DIM - ROPE_DIM), F32)
    cos_h = jnp.concatenate([cos, cos, ones], axis=1)
    sa_h = jnp.concatenate([-sin, zeros_h, zeros_r], axis=1)
    sb_h = jnp.concatenate([zeros_h, sin, zeros_r], axis=1)
    two = lambda t: jnp.concatenate([t, t], axis=1)
    return two(cos_h), two(sa_h), two(sb_h)


def _attn_prompt_kernel(sink_ref, q_ref, kc_ref, kp_ref, vc_ref, vp_ref, z_ref, o_ref):
    n = pl.program_id(1)
    hd = HEAD_DIM
    blk = q_ref.shape[0]
    n_kv = kc_ref.shape[1] // hd
    grp = q_ref.shape[1] // (n_kv * hd)
    qi = lax.broadcasted_iota(jnp.int32, (blk, 2 * blk), 0)
    kj = lax.broadcasted_iota(jnp.int32, (blk, 2 * blk), 1) - blk
    kpos = n * blk + kj
    diff = qi - kj
    valid = (kpos >= LEAD) & (diff >= 0) & (diff <= WINDOW)
    k_all = jnp.concatenate([kp_ref[...], kc_ref[...]], axis=0).astype(BF16)
    v_all = jnp.concatenate([vp_ref[...], vc_ref[...]], axis=0).astype(BF16)
    scale = hd ** -0.5

    def scores(h):
        k_h = k_all[:, h * hd:(h + 1) * hd]
        return [_dot_nt(q_ref[:, (h * grp + gi) * hd:(h * grp + gi + 1) * hd], k_h) for gi in range(grp)]

    outs = []
    s_next = scores(0)
    for h in range(n_kv):
        s_cur = s_next
        if h + 1 < n_kv:
            s_next = scores(h + 1)
        v_h = v_all[:, h * hd:(h + 1) * hd]
        ps, dens = [], []
        for gi in range(grp):
            s = jnp.where(valid, s_cur[gi] * scale, -jnp.inf)
            sk = sink_ref[h * grp + gi]
            m = jnp.maximum(jnp.max(s, axis=-1, keepdims=True), sk)
            p = jnp.exp(s - m)
            dens.append(jnp.sum(p, axis=-1, keepdims=True) + jnp.exp(sk - m))
            ps.append(p.astype(BF16))
        outs += [_dot(ps[gi], v_h) / dens[gi] for gi in range(grp)]
    att = jnp.concatenate(outs, axis=1)
    o_ref[...] = (att * _silu(z_ref[...])).astype(o_ref.dtype)


def attn_prompt(sinks, q, k, v, z, n_batch):
    m, e = q.shape
    nb = m // (n_batch * BLOCK)
    kw = k.shape[1]
    cur = lambda i, n: (i * nb + n, 0)
    prv = lambda i, n: (i * nb + jnp.maximum(n - 1, 0), 0)
    return pl.pallas_call(
        _attn_prompt_kernel,
        grid=(n_batch, nb),
        in_specs=[pl.BlockSpec(memory_space=pltpu.SMEM),
                  pl.BlockSpec((BLOCK, e), cur),
                  pl.BlockSpec((BLOCK, kw), cur), pl.BlockSpec((BLOCK, kw), prv),
                  pl.BlockSpec((BLOCK, kw), cur), pl.BlockSpec((BLOCK, kw), prv),
                  pl.BlockSpec((BLOCK, e), cur)],
        out_specs=pl.BlockSpec((BLOCK, e), cur),
        out_shape=jax.ShapeDtypeStruct((m, e), BF16),
        compiler_params=_cparams(("arbitrary", "arbitrary")),
        name="attn_prompt",
    )(sinks, q, k, k, v, v, z)


def _attn_sample_kernel(sink_ref, q_ref, kc_ref, vc_ref, kn_ref, vn_ref, z_ref, o_ref, ko_ref, vo_ref):
    hd = HEAD_DIM
    win = kc_ref.shape[1]
    n_kv = kc_ref.shape[2] // hd
    nq = q_ref.shape[1]
    grp = nq // n_kv
    pad = 8
    kc = kc_ref[0]
    vc = vc_ref[0]
    kn = kn_ref[0]
    vn = vn_ref[0]
    first = lax.broadcasted_iota(jnp.int32, (pad, kc.shape[1]), 0) == 0
    k_all = jnp.concatenate([kc, jnp.where(first, kn, 0.0)], axis=0).astype(BF16)
    v_all = jnp.concatenate([vc, jnp.where(first, vn, 0.0)], axis=0).astype(BF16)
    col = lax.broadcasted_iota(jnp.int32, (grp, win + pad), 1)
    valid = (col <= win) & (win - col <= WINDOW)
    q = q_ref[0].astype(BF16)
    scale = hd ** -0.5
    row_i = lax.broadcasted_iota(jnp.int32, (grp, 1), 0)
    outs = []
    for h in range(n_kv):
        k_h = k_all[:, h * hd:(h + 1) * hd]
        v_h = v_all[:, h * hd:(h + 1) * hd]
        q_h = q[h * grp:(h + 1) * grp, :]
        s = _dot_nt(q_h, k_h) * scale
        s = jnp.where(valid, s, -jnp.inf)
        sk = jnp.zeros((grp, 1), F32)
        for gi in range(grp):
            sk = jnp.where(row_i == gi, sink_ref[h * grp + gi], sk)
        m = jnp.maximum(jnp.max(s, axis=-1, keepdims=True), sk)
        p = jnp.exp(s - m)
        den = jnp.sum(p, axis=-1, keepdims=True) + jnp.exp(sk - m)
        outs.append(_dot(p.astype(BF16), v_h) / den)
    att = jnp.concatenate(outs, axis=0)
    o_ref[0] = (att * _silu(z_ref[0])).astype(o_ref.dtype)
    last = lax.broadcasted_iota(jnp.int32, kc.shape, 0) == win - 1
    ko_ref[0] = jnp.where(last, kn, pltpu.roll(kc, win - 1, axis=0))
    vo_ref[0] = jnp.where(last, vn, pltpu.roll(vc, win - 1, axis=0))


def attn_sample(sinks, q, cache_k, cache_v, k_new, v_new, z):
    m, win, kw = cache_k.shape
    nq = q.shape[1]
    hd = HEAD_DIM
    tok = pl.BlockSpec((1, nq, hd), lambda i: (i, 0, 0))
    cache = pl.BlockSpec((1, win, kw), lambda i: (i, 0, 0))
    new = pl.BlockSpec((1, 1, kw), lambda i: (i, 0, 0))
    return pl.pallas_call(
        _attn_sample_kernel,
        grid=(m,),
        in_specs=[pl.BlockSpec(memory_space=pltpu.SMEM), tok, cache, cache, new, new, tok],
        out_specs=[tok, cache, cache],
        out_shape=[jax.ShapeDtypeStruct((m, nq, hd), BF16),
                   jax.ShapeDtypeStruct(cache_k.shape, F32), jax.ShapeDtypeStruct(cache_v.shape, F32)],
        compiler_params=_cparams(("arbitrary",)),
        name="attn_sample",
    )(sinks, q, cache_k, cache_v, k_new, v_new, z)


def _pad_lora(w_down, w_up):
    r = w_down.shape[1]
    return (jnp.pad(w_down, ((0, 0), (0, LORA_PAD - r))).astype(BF16),
            jnp.pad(w_up, ((0, LORA_PAD - r), (0, 0))).astype(BF16))


def kernel(x_prompt, x_sample, state_wkv, state_shift, cache_k, cache_v, meta_tokens, a_norm, a_mu, a_w_rkvz,
           a_w0, a_w1, a_w2, a_a0, a_a1, a_a2, a_k_k, a_k_a, a_r_k, a_gn_g, a_gn_b, a_w_out, kv_norm, w_kv,
           b_norm, b_w_qz, b_sinks, b_w_o, final_norm):
    nb, seq, d = x_prompt.shape
    db, dseq, _ = x_sample.shape
    assert dseq == 1 and a_norm.shape[0] == 1 and b_norm.shape[0] == 1
    e = a_w_rkvz.shape[3]
    win = cache_k.shape[1]
    p_len = LEAD + N_META + seq
    assert p_len % BLOCK == 0 and (LEAD + N_META) == BLOCK
    m_p = nb * p_len
    kvw = N_KV_HEADS * HEAD_DIM

    w_rkvz = a_w_rkvz[0].astype(BF16)
    w1, w2 = _pad_lora(a_w1[0], a_w2[0])
    a1, a2 = _pad_lora(a_a1[0], a_a2[0])
    w_out = a_w_out[0].astype(BF16)
    w_kv_bf = w_kv.astype(BF16)
    w_q = b_w_qz[0][:, :e].astype(BF16)
    w_z = b_w_qz[0][:, e:].astype(BF16)
    w_o = b_w_o[0].astype(BF16)
    mu = a_mu[0]
    sinks = b_sinks[0]
    gains_b = jnp.stack([kv_norm, b_norm[0]])

    tm = p_len // 8

    hp = jnp.concatenate([jnp.zeros((nb, LEAD, d), F32),
                          jnp.broadcast_to(meta_tokens[None], (nb, N_META, d)), x_prompt], axis=1)
    xm, x_last = norm_shift_prompt(hp, a_norm[0], mu, tm)
    p_state_shift = x_last.reshape(1, nb, d)
    rkvz = matmul_groups(xm[:4], w_rkvz, tm, F32)
    wl, al = lora_proj(xm, 4, 5, w1, w2, a_w0[0], a1, a2, a_a0[0], tm)
    yg, p_state = wkv_prompt(rkvz, wl, al, a_k_k[0], a_k_a[0], a_r_k[0].reshape(-1), a_gn_g[0], a_gn_b[0], nb)
    hp, hn_kv, hn_b = matmul_residual_norm(yg, w_out, hp.reshape(m_p, d), gains_b, tm, BF16)

    pos_p = jnp.maximum(jnp.arange(p_len, dtype=jnp.int32) - LEAD, 0)
    tabs_p = tuple(jnp.tile(t, (nb, 1)) for t in rope_tables(pos_p))
    k_p, v_p = matmul_rope(hn_kv, w_kv_bf, tabs_p, tm, kvw, (F32, F32))
    q_p, = matmul_rope(hn_b, w_q, tabs_p, tm, e, (BF16,))
    z_p = matmul_groups(hn_b[None], w_z[None], tm, F32)[0]
    att = attn_prompt(sinks, q_p, k_p, v_p, z_p, nb)
    y_pad, = matmul_residual_norm(att, w_o, hp, final_norm[None], tm, F32, emit_h=False)
    y_prompt = y_pad.reshape(nb, p_len, d)[:, LEAD + N_META:]
    p_cache_k = k_p.reshape(nb, p_len, N_KV_HEADS, HEAD_DIM)[:, -win:]
    p_cache_v = v_p.reshape(nb, p_len, N_KV_HEADS, HEAD_DIM)[:, -win:]

    hs = x_sample.reshape(db, d)
    xm_s, xn_s = norm_shift_sample(hs, state_shift[0], a_norm[0], mu)
    rkvz_s = matmul_groups(xm_s[:4], w_rkvz, db, F32)
    wl_s, al_s = lora_proj(xm_s, 4, 5, w1, w2, a_w0[0], a1, a2, a_a0[0], db)
    yg_s, s_state = wkv_sample(rkvz_s, wl_s, al_s, a_k_k[0], a_k_a[0], a_r_k[0].reshape(-1), a_gn_g[0],
                               a_gn_b[0], state_wkv[0])
    hs, hn_kv_s, hn_b_s = matmul_residual_norm(yg_s.reshape(db, e), w_out, hs, gains_b, db, BF16)
    tabs_s = rope_tables(jnp.full((db,), PAST_LEN, jnp.int32))
    k_s, v_s = matmul_rope(hn_kv_s, w_kv_bf, tabs_s, db, kvw, (F32, F32))
    q_s, = matmul_rope(hn_b_s, w_q, tabs_s, db, e, (F32,))
    z_s = matmul_groups(hn_b_s[None], w_z[None], db, F32)[0]
    nq = e // HEAD_DIM
    att_s, s_cache_k, s_cache_v = attn_sample(
        sinks, q_s.reshape(db, nq, HEAD_DIM), cache_k.reshape(db, win, kvw), cache_v.reshape(db, win, kvw),
        k_s.reshape(db, 1, kvw), v_s.reshape(db, 1, kvw), z_s.reshape(db, nq, HEAD_DIM))
    y_s, = matmul_residual_norm(att_s.reshape(db, e), w_o, hs, final_norm[None], db, F32, emit_h=False)
    y_sample = y_s.reshape(db, 1, d)

    return (y_prompt, y_sample, p_state[None], p_state_shift,
            p_cache_k, p_cache_v,
            s_state[None], xn_s[None],
            s_cache_k.reshape(cache_k.shape), s_cache_v.reshape(cache_v.shape))
```

```python
import functools
import math

import jax
import jax.numpy as jnp
from jax import lax
from jax.experimental import pallas as pl
from jax.experimental.pallas import tpu as pltpu

F32 = jnp.float32
BF16 = jnp.bfloat16

HEAD_DIM = 64
N_KV_HEADS = 8
WINDOW = 128
BLOCK = 128
ROPE_DIM = HEAD_DIM // 4
ROPE_THETA = 500000.0
N_META = 16
PAST_LEN = 16384
RMS_EPS = 1e-6
GN_EPS = 64e-5
LEAD = (-N_META) % BLOCK
CHUNK = 64
WKV_ROWS = 128
HEADS_PER_STREAM = 16
HEADS_PER_STEP = 32
LORA_PAD = 128
MXU_TILE = 256
ROPE_SLAB = 512
VMEM_LIMIT = 48 * 1024 * 1024
DECAY_SCALE = math.exp(-0.5)


def _cparams(sem):
    return pltpu.CompilerParams(dimension_semantics=sem, vmem_limit_bytes=VMEM_LIMIT)


def _sigmoid(x):
    return 1.0 / (1.0 + jnp.exp(-x))


def _silu(x):
    return x * _sigmoid(x)


def _dot(a, b):
    return jnp.dot(a, b, preferred_element_type=F32)


def _dot_nt(a, b):
    return lax.dot_general(a, b, (((1,), (1,)), ((), ())), preferred_element_type=F32)


def _dot_tn(a, b):
    return lax.dot_general(a, b, (((0,), (0,)), ((), ())), preferred_element_type=F32)


def _split_hi_lo(x):
    hi = x.astype(BF16)
    lo = (x - hi.astype(F32)).astype(BF16)
    return hi, lo


def _mixes(xn, prev, mu_ref, w1_ref, a1_ref, xm_ref, hw_ref, ha_ref):
    xx = prev - xn
    n_proj = xm_ref.shape[0]
    for p in range(n_proj):
        xm_ref[p] = (xn + xx * mu_ref[p:p + 1, :]).astype(xm_ref.dtype)
    xw = (xn + xx * mu_ref[n_proj:n_proj + 1, :]).astype(BF16)
    xa = (xn + xx * mu_ref[n_proj + 1:n_proj + 2, :]).astype(BF16)
    hw_ref[...] = jnp.tanh(_dot(xw, w1_ref[...])).astype(hw_ref.dtype)
    ha_ref[...] = _dot(xa, a1_ref[...]).astype(ha_ref.dtype)


def _norm_shift_kernel(x_ref, g_ref, mu_ref, w1_ref, a1_ref, xm_ref, hw_ref, ha_ref, last_ref, carry_ref):
    @pl.when(pl.program_id(1) == 0)
    def _():
        carry_ref[...] = jnp.zeros_like(carry_ref)

    x = x_ref[0]
    tm = x.shape[0]
    xn = x * lax.rsqrt(jnp.mean(x * x, axis=-1, keepdims=True) + RMS_EPS) * g_ref[...]
    rolled = pltpu.roll(xn, 1, axis=0)
    row = lax.broadcasted_iota(jnp.int32, xn.shape, 0)
    prev = jnp.where(row == 0, carry_ref[0:1, :], rolled)
    _mixes(xn, prev, mu_ref, w1_ref, a1_ref, xm_ref, hw_ref, ha_ref)
    carry_ref[0:1, :] = xn[tm - 1:tm, :]
    last_ref[0] = xn[tm - 1:tm, :]


def norm_shift_prompt(x, g, mu, w1, a1, tm):
    b, p, d = x.shape
    n_mix = mu.shape[0]
    n_proj = n_mix - 2
    lr = w1.shape[1]
    nt = p // tm
    const = lambda shape: pl.BlockSpec(shape, lambda i, t: (0,) * len(shape))
    hid = pl.BlockSpec((tm, lr), lambda i, t: (i * nt + t, 0))
    return pl.pallas_call(
        _norm_shift_kernel,
        grid=(b, nt),
        in_specs=[pl.BlockSpec((1, tm, d), lambda i, t: (i, t, 0)),
                  const((1, d)), const((n_mix, d)), const((d, lr)), const((d, lr))],
        out_specs=[pl.BlockSpec((n_proj, tm, d), lambda i, t: (0, i * nt + t, 0)), hid, hid,
                   pl.BlockSpec((1, 1, d), lambda i, t: (i, 0, 0))],
        out_shape=[jax.ShapeDtypeStruct((n_proj, b * p, d), BF16),
                   jax.ShapeDtypeStruct((b * p, lr), BF16), jax.ShapeDtypeStruct((b * p, lr), BF16),
                   jax.ShapeDtypeStruct((b, 1, d), F32)],
        scratch_shapes=[pltpu.VMEM((8, d), F32)],
        compiler_params=_cparams(("arbitrary", "arbitrary")),
        name="norm_shift_prompt",
    )(x, g.reshape(1, d), mu, w1, a1)


def _norm_shift_sample_kernel(x_ref, prev_ref, g_ref, mu_ref, w1_ref, a1_ref, xm_ref, hw_ref, ha_ref, xn_ref):
    x = x_ref[...]
    xn = x * lax.rsqrt(jnp.mean(x * x, axis=-1, keepdims=True) + RMS_EPS) * g_ref[...]
    xn_ref[...] = xn
    _mixes(xn, prev_ref[...], mu_ref, w1_ref, a1_ref, xm_ref, hw_ref, ha_ref)


def norm_shift_sample(x, prev, g, mu, w1, a1):
    m, d = x.shape
    lr = w1.shape[1]
    return pl.pallas_call(
        _norm_shift_sample_kernel,
        out_shape=[jax.ShapeDtypeStruct((mu.shape[0] - 2, m, d), BF16),
                   jax.ShapeDtypeStruct((m, lr), BF16), jax.ShapeDtypeStruct((m, lr), BF16),
                   jax.ShapeDtypeStruct((m, d), F32)],
        name="norm_shift_sample",
    )(x, prev, g.reshape(1, d), mu, w1, a1)


def _rope(y, cos, sin_a, sin_b):
    half = ROPE_DIM // 2
    step = ROPE_SLAB
    rep = step // cos.shape[1]
    tile = lambda t: jnp.concatenate([t] * rep, axis=1)
    cos_t, sa_t, sb_t = tile(cos), tile(sin_a), tile(sin_b)
    outs = []
    for j in range(y.shape[1] // step):
        ys = y[:, j * step:(j + 1) * step]
        outs.append(ys * cos_t + pltpu.roll(ys, step - half, axis=1) * sa_t + pltpu.roll(ys, half, axis=1) * sb_t)
    return jnp.concatenate(outs, axis=1) if len(outs) > 1 else outs[0]


def _mm_group_kernel(x_ref, w_ref, o_ref):
    o_ref[0] = _dot(x_ref[0], w_ref[0]).astype(o_ref.dtype)


def matmul_groups(x, w, tm, out_dtype):
    g, _, n = w.shape
    _, m, kdim = x.shape
    return pl.pallas_call(
        _mm_group_kernel,
        grid=(g, m // tm),
        in_specs=[pl.BlockSpec((1, tm, kdim), lambda q, i: (q, i, 0)),
                  pl.BlockSpec((1, kdim, n), lambda q, i: (q, 0, 0))],
        out_specs=pl.BlockSpec((1, tm, n), lambda q, i: (q, i, 0)),
        out_shape=jax.ShapeDtypeStruct((g, m, n), out_dtype),
        compiler_params=_cparams(("arbitrary", "arbitrary")),
        name="matmul_groups",
    )(x, w)


def _mm_rope_kernel(x_ref, w_ref, cos_ref, sa_ref, sb_ref, *o_refs, n_rope):
    y = _dot(x_ref[...], w_ref[...])
    rot = _rope(y[:, :n_rope], cos_ref[...], sa_ref[...], sb_ref[...])
    o_refs[0][...] = rot.astype(o_refs[0].dtype)
    if len(o_refs) > 1:
        o_refs[1][...] = y[:, n_rope:].astype(o_refs[1].dtype)


def matmul_rope(x, w, tables, tm, n_rope, out_dtypes):
    m, kdim = x.shape
    n = w.shape[1]
    lanes = tables[0].shape[1]
    widths = [n_rope] + ([n - n_rope] if n > n_rope else [])
    tab = pl.BlockSpec((tm, lanes), lambda i: (i, 0))
    outs = pl.pallas_call(
        functools.partial(_mm_rope_kernel, n_rope=n_rope),
        grid=(m // tm,),
        in_specs=[pl.BlockSpec((tm, kdim), lambda i: (i, 0)),
                  pl.BlockSpec((kdim, n), lambda i: (0, 0), pipeline_mode=pl.Buffered(1)),
                  tab, tab, tab],
        out_specs=[pl.BlockSpec((tm, wd), lambda i: (i, 0)) for wd in widths],
        out_shape=[jax.ShapeDtypeStruct((m, wd), dt) for wd, dt in zip(widths, out_dtypes)],
        compiler_params=_cparams(("arbitrary",)),
        name="matmul_rope",
    )(x, w, *tables)
    return outs


def _mm_res_norm_kernel(x_ref, w_ref, res_ref, g_ref, *out_refs, emit_h):
    h = res_ref[...] + _dot(x_ref[...], w_ref[...])
    hn_refs = out_refs
    if emit_h:
        out_refs[0][...] = h
        hn_refs = out_refs[1:]
    inv = lax.rsqrt(jnp.mean(h * h, axis=-1, keepdims=True) + RMS_EPS)
    for j, hn_ref in enumerate(hn_refs):
        hn_ref[...] = (h * inv * g_ref[j:j + 1, :]).astype(hn_ref.dtype)


def matmul_residual_norm(x, w, res, gains, tm, norm_dtype, emit_h=True):
    m, kdim = x.shape
    n = w.shape[1]
    ng = gains.shape[0]
    row = lambda width: pl.BlockSpec((tm, width), lambda i: (i, 0))
    return pl.pallas_call(
        functools.partial(_mm_res_norm_kernel, emit_h=emit_h),
        grid=(m // tm,),
        in_specs=[row(kdim),
                  pl.BlockSpec((kdim, n), lambda i: (0, 0), pipeline_mode=pl.Buffered(1)),
                  row(n),
                  pl.BlockSpec((ng, n), lambda i: (0, 0))],
        out_specs=[row(n)] * (int(emit_h) + ng),
        out_shape=[jax.ShapeDtypeStruct((m, n), F32)] * int(emit_h) + [jax.ShapeDtypeStruct((m, n), norm_dtype)] * ng,
        compiler_params=_cparams(("arbitrary",)),
        name="matmul_residual_norm",
    )(x, w, res, gains)


def _lora_up_kernel(hw_ref, ha_ref, w2_ref, w0_ref, a2_ref, a0_ref, wl_ref, al_ref):
    wl_ref[...] = w0_ref[...] + _dot(hw_ref[...], w2_ref[...])
    al_ref[...] = a0_ref[...] + _dot(ha_ref[...], a2_ref[...])


def lora_up(hw, ha, w2, w0, a2, a0):
    m = hw.shape[0]
    e = w2.shape[1]
    return pl.pallas_call(
        _lora_up_kernel,
        out_shape=[jax.ShapeDtypeStruct((m, e), F32), jax.ShapeDtypeStruct((m, e), F32)],
        name="lora_up",
    )(hw, ha, w2, w0.reshape(1, e), a2, a0.reshape(1, e))


def _seg_sum(x, ones_bd):
    hi, lo = _split_hi_lo(x)
    outs = []
    for c in range(x.shape[1] // MXU_TILE):
        sl = slice(c * MXU_TILE, (c + 1) * MXU_TILE)
        outs.append(_dot(hi[:, sl], ones_bd) + _dot(lo[:, sl], ones_bd))
    return jnp.concatenate(outs, axis=1) if len(outs) > 1 else outs[0]


def _wkv_prompt_kernel(r_ref, k_ref, v_ref, z_ref, hw_ref, ha_ref, w2_ref, w0_ref, a2_ref, a0_ref,
                       kk_ref, ka_ref, rk_ref, gg_ref, gb_ref, yg_ref, sout_ref, s_ref):
    t_idx = pl.program_id(2)
    c = CHUNK
    hd = HEAD_DIM

    @pl.when(t_idx == 0)
    def _():
        s_ref[...] = jnp.zeros_like(s_ref)

    tb = r_ref.shape[1]
    nh = HEADS_PER_STREAM
    hw = nh * hd
    nc = tb // c

    li = lax.broadcasted_iota(jnp.int32, (MXU_TILE, MXU_TILE), 0) // hd
    lj = lax.broadcasted_iota(jnp.int32, (MXU_TILE, MXU_TILE), 1) // hd
    ones_bd = jnp.where(li == lj, 1.0, 0.0).astype(BF16)
    bi_ = lax.broadcasted_iota(jnp.int32, (tb, tb), 0)
    bj_ = lax.broadcasted_iota(jnp.int32, (tb, tb), 1)
    tri_incl = jnp.where((bj_ <= bi_) & (bj_ // c == bi_ // c), 1.0, 0.0).astype(BF16)
    ti = lax.broadcasted_iota(jnp.int32, (c, c), 0)
    tj = lax.broadcasted_iota(jnp.int32, (c, c), 1)
    eye = jnp.where(ti == tj, 1.0, 0.0).astype(F32)
    ai = lax.broadcasted_iota(jnp.int32, (c, 2 * c), 0)
    aj = lax.broadcasted_iota(jnp.int32, (c, 2 * c), 1)
    aj_mod = jnp.where(aj >= c, aj - c, aj)
    mask_top_k = (aj >= c) & (aj_mod < ai)
    mask_bot = aj_mod <= ai

    for st in range(r_ref.shape[2] // hw):
        _wkv_stream(st, hw, nc, r_ref, k_ref, v_ref, z_ref, hw_ref, ha_ref, w2_ref, w0_ref, a2_ref, a0_ref,
                    kk_ref, ka_ref, rk_ref, gg_ref, gb_ref, yg_ref, s_ref,
                    ones_bd, tri_incl, eye, ti, tj, mask_top_k, mask_bot)

    @pl.when(t_idx == pl.num_programs(2) - 1)
    def _():
        sout_ref[0] = s_ref[...]


def _wkv_stream(st, hw, nc, r_ref, k_ref, v_ref, z_ref, hw_ref, ha_ref, w2_ref, w0_ref, a2_ref, a0_ref,
                kk_ref, ka_ref, rk_ref, gg_ref, gb_ref, yg_ref, s_ref,
                ones_bd, tri_incl, eye, ti, tj, mask_top_k, mask_bot):
    c = CHUNK
    hd = HEAD_DIM
    nh = hw // hd
    ls = slice(st * hw, (st + 1) * hw)
    h0 = st * nh
    r = r_ref[0, :, ls]
    k = k_ref[0, :, ls]
    v = v_ref[0, :, ls]
    wl = w0_ref[:, ls] + _dot(hw_ref[...], w2_ref[:, ls])
    al = a0_ref[:, ls] + _dot(ha_ref[...], a2_ref[:, ls])
    a = _sigmoid(al)
    lw = -DECAY_SCALE * _sigmoid(wl)
    kk = k * kk_ref[:, ls]
    n2 = _seg_sum(kk * kk, ones_bd)
    kk = kk / jnp.maximum(jnp.sqrt(n2), 1e-12)
    k2 = k * (1.0 + (a - 1.0) * ka_ref[:, ls])
    bb = kk * a

    lw_hi, lw_lo = _split_hi_lo(lw)
    g = _dot(tri_incl, lw_hi) + _dot(tri_incl, lw_lo)
    mid = lambda ci: g[ci * c + c // 2 - 1:ci * c + c // 2, :]
    gm = jnp.concatenate([jnp.broadcast_to(mid(ci), (c, hw)) for ci in range(nc)], axis=0)
    e_a = jnp.exp(g - gm)
    e_prev = jnp.exp(g - lw - gm)
    e_inv = jnp.exp(gm - g)
    e1 = [jnp.exp(mid(ci)) for ci in range(nc)]
    e2 = [jnp.exp(g[ci * c + c - 1:ci * c + c, :] - mid(ci)) for ci in range(nc)]

    kkd = (kk * e_prev).astype(BF16)
    rd = (r * e_a).astype(BF16)
    bi = (bb * e_inv).astype(BF16)
    ki = (k2 * e_inv).astype(BF16)
    v_bf = v.astype(BF16)
    zeros_cv = jnp.zeros((c, hd), BF16)

    pairs = [(ci, h) for ci in range(nc) for h in range(nh)]
    rows = lambda ci: slice(ci * c, (ci + 1) * c)
    cols = lambda h: slice(h * hd, (h + 1) * hd)
    xs = {(ci, h): jnp.concatenate([kkd[rows(ci), cols(h)], rd[rows(ci), cols(h)]], axis=0) for ci, h in pairs}
    r1s = {(ci, h): jnp.concatenate([bi[rows(ci), cols(h)], ki[rows(ci), cols(h)]], axis=0) for ci, h in pairs}
    vs = {(ci, h): v_bf[rows(ci), cols(h)] for ci, h in pairs}
    a_mats = {p: _dot_nt(xs[p], r1s[p]) for p in pairs}
    lk_vs = {p: _dot(jnp.where(mask_top_k, a_mats[p][:c, :], 0.0).astype(BF16),
                     jnp.concatenate([zeros_cv, vs[p]], axis=0)) for p in pairs}
    lps = {p: jnp.where(tj < ti, a_mats[p][:c, :c], 0.0) for p in pairs}
    ts = {p: eye - lps[p] for p in pairs}
    for _ in range(int(math.log2(c)) - 1):
        lpb = {p: lps[p].astype(BF16) for p in pairs}
        lps = {p: _dot(lpb[p], lpb[p]) for p in pairs}
        ts = {p: _dot(ts[p].astype(BF16), (eye + lps[p]).astype(BF16)) for p in pairs}
    a_bots = {p: jnp.where(mask_bot, a_mats[p][c:, :], 0.0).astype(BF16) for p in pairs}
    t_bf = {p: ts[p].astype(BF16) for p in pairs}

    state = [s_ref[h0 + h] for h in range(nh)]
    y_rows = []
    for ci in range(nc):
        hs = range(nh)
        sms = [state[h] * e1[ci][:, cols(h)] for h in hs]
        p_mats = [_dot_nt(xs[ci, h], sms[h].astype(BF16)) for h in hs]
        us = [-_dot(t_bf[ci, h], (p_mats[h][:c, :] + lk_vs[ci, h]).astype(BF16)) for h in hs]
        uvs = [jnp.concatenate([us[h].astype(BF16), vs[ci, h]], axis=0) for h in hs]
        ys = [p_mats[h][c:, :] + _dot(a_bots[ci, h], uvs[h]) for h in hs]
        state = [(sms[h] + _dot_tn(uvs[h], r1s[ci, h])) * e2[ci][:, cols(h)] for h in hs]
        y_rows.append(jnp.concatenate(ys, axis=1))
    for h in range(nh):
        s_ref[h0 + h] = state[h]
    y = jnp.concatenate(y_rows, axis=0) if nc > 1 else y_rows[0]

    inv_n = 1.0 / hd
    mean = _seg_sum(y, ones_bd) * inv_n
    yc = y - mean
    var = _seg_sum(yc * yc, ones_bd) * inv_n
    yn = yc * lax.rsqrt(var + GN_EPS) * gg_ref[:, ls] + gb_ref[:, ls]
    bonus = _seg_sum(r * k2 * rk_ref[:, ls], ones_bd)
    yg_ref[:, ls] = ((yn + bonus * v) * _silu(z_ref[0, :, ls])).astype(BF16)


def wkv_prompt(rkvz, hw_act, ha_act, w2, w0, a2, a0, k_k, k_a, r_k, gn_g, gn_b, n_batch):
    _, m, e = rkvz.shape
    p = m // n_batch
    tb = WKV_ROWS
    nt = p // tb
    hw = HEADS_PER_STEP * HEAD_DIM
    nh = e // HEAD_DIM
    row = lambda i, g, t: (i * nt + t, g)
    proj = lambda q: pl.BlockSpec((1, tb, hw), lambda i, g, t: (q, i * nt + t, g))
    par = pl.BlockSpec((1, hw), lambda i, g, t: (0, g))
    lr = w2.shape[0]
    hid = pl.BlockSpec((tb, lr), lambda i, g, t: (i * nt + t, 0))
    up = pl.BlockSpec((lr, hw), lambda i, g, t: (0, g))
    return pl.pallas_call(
        _wkv_prompt_kernel,
        grid=(n_batch, e // hw, nt),
        in_specs=[proj(0), proj(1), proj(2), proj(3), hid, hid, up, par, up, par,
                  par, par, par, par, par],
        out_specs=[pl.BlockSpec((tb, hw), row),
                   pl.BlockSpec((1, HEADS_PER_STEP, HEAD_DIM, HEAD_DIM), lambda i, g, t: (i, g, 0, 0))],
        out_shape=[jax.ShapeDtypeStruct((m, e), BF16),
                   jax.ShapeDtypeStruct((n_batch, nh, HEAD_DIM, HEAD_DIM), F32)],
        scratch_shapes=[pltpu.VMEM((HEADS_PER_STEP, HEAD_DIM, HEAD_DIM), F32)],
        compiler_params=_cparams(("arbitrary", "arbitrary", "arbitrary")),
        name="wkv_prompt",
    )(rkvz, rkvz, rkvz, rkvz, hw_act, ha_act, w2, w0.reshape(1, e), a2, a0.reshape(1, e),
      k_k.reshape(1, e), k_a.reshape(1, e), r_k.reshape(1, e), gn_g.reshape(1, e), gn_b.reshape(1, e))


def _wkv_sample_kernel(r_ref, k_ref, v_ref, z_ref, wl_ref, al_ref, kk_ref, ka_ref, rk_ref, gg_ref, gb_ref,
                       s_ref, yg_ref, sout_ref, y_scr):
    hd = HEAD_DIM
    r = r_ref[0]
    k = k_ref[0]
    v = v_ref[0]
    a = _sigmoid(al_ref[0])
    d = jnp.exp(-DECAY_SCALE * _sigmoid(wl_ref[0]))
    kk = k * kk_ref[...]
    kk = kk / jnp.maximum(jnp.sqrt(jnp.sum(kk * kk, axis=-1, keepdims=True)), 1e-12)
    k2 = k * (1.0 + (a - 1.0) * ka_ref[...])
    bb = kk * a
    nh = r.shape[0]
    ii = lax.broadcasted_iota(jnp.int32, (hd, hd), 0)
    jj = lax.broadcasted_iota(jnp.int32, (hd, hd), 1)
    eye = ii == jj

    for h in range(nh):
        row = lambda x: x[h:h + 1, :]
        s = s_ref[0, h]
        sa = jnp.sum(s * row(kk), axis=-1, keepdims=True)
        v_col = jnp.sum(jnp.where(eye, row(v), 0.0), axis=-1, keepdims=True)
        s_new = s * row(d) - sa * row(bb) + v_col * row(k2)
        sout_ref[0, h] = s_new
        y_col = jnp.sum(s_new * row(r), axis=-1, keepdims=True)
        y_scr[h:h + 1, :] = jnp.sum(jnp.where(eye, y_col, 0.0), axis=0, keepdims=True)

    y = y_scr[...]
    mean = jnp.mean(y, axis=-1, keepdims=True)
    yc = y - mean
    var = jnp.mean(yc * yc, axis=-1, keepdims=True)
    yn = yc * lax.rsqrt(var + GN_EPS) * gg_ref[...] + gb_ref[...]
    bonus = jnp.sum(r * k2 * rk_ref[...], axis=-1, keepdims=True)
    yg_ref[0] = ((yn + bonus * v) * _silu(z_ref[0])).astype(BF16)


def wkv_sample(rkvz, wl, al, k_k, k_a, r_k, gn_g, gn_b, state):
    _, m, e = rkvz.shape
    nh = e // HEAD_DIM
    hd = HEAD_DIM
    rkvz4 = rkvz.reshape(4, m, nh, hd)
    proj = lambda q: pl.BlockSpec((None, 1, nh, hd), lambda i: (q, i, 0, 0))
    tok = pl.BlockSpec((1, nh, hd), lambda i: (i, 0, 0))
    par = pl.BlockSpec((nh, hd), lambda i: (0, 0))
    st = pl.BlockSpec((1, nh, hd, hd), lambda i: (i, 0, 0, 0))
    as_heads = lambda x: x.reshape(nh, hd)
    return pl.pallas_call(
        _wkv_sample_kernel,
        grid=(m,),
        in_specs=[proj(0), proj(1), proj(2), proj(3), tok, tok, par, par, par, par, par, st],
        out_specs=[tok, st],
        out_shape=[jax.ShapeDtypeStruct((m, nh, hd), BF16), jax.ShapeDtypeStruct(state.shape, F32)],
        scratch_shapes=[pltpu.VMEM((nh, hd), F32)],
        compiler_params=_cparams(("arbitrary",)),
        name="wkv_sample",
    )(rkvz4, rkvz4, rkvz4, rkvz4, wl.reshape(m, nh, hd), al.reshape(m, nh, hd),
      as_heads(k_k), as_heads(k_a), as_heads(r_k), as_heads(gn_g), as_heads(gn_b), state)


def rope_tables(pos):
    half = ROPE_DIM // 2
    inv_freq = ROPE_THETA ** (-jnp.arange(half, dtype=F32) * 2.0 / ROPE_DIM)
    ang = pos.astype(F32)[:, None] * inv_freq[None, :]
    cos = jnp.cos(ang)
    sin = jnp.sin(ang)
    rows = pos.shape[0]
    ones = jnp.ones((rows, HEAD_DIM - ROPE_DIM), F32)
    zeros_h = jnp.zeros((rows, half), F32)
    zeros_r = jnp.zeros((rows, HEAD_DIM - ROPE_DIM), F32)
    cos_h = jnp.concatenate([cos, cos, ones], axis=1)
    sa_h = jnp.concatenate([-sin, zeros_h, zeros_r], axis=1)
    sb_h = jnp.concatenate([zeros_h, sin, zeros_r], axis=1)
    two = lambda t: jnp.concatenate([t, t], axis=1)
    return two(cos_h), two(sa_h), two(sb_h)


def _attn_prompt_kernel(sink_ref, q_ref, kc_ref, kp_ref, vc_ref, vp_ref, z_ref, o_ref):
    n = pl.program_id(1)
    hd = HEAD_DIM
    blk = q_ref.shape[0]
    n_kv = kc_ref.shape[1] // hd
    grp = q_ref.shape[1] // (n_kv * hd)
    qi = lax.broadcasted_iota(jnp.int32, (blk, 2 * blk), 0)
    kj = lax.broadcasted_iota(jnp.int32, (blk, 2 * blk), 1) - blk
    kpos = n * blk + kj
    diff = qi - kj
    valid = (kpos >= LEAD) & (diff >= 0) & (diff <= WINDOW)
    k_all = jnp.concatenate([kp_ref[...], kc_ref[...]], axis=0).astype(BF16)
    v_all = jnp.concatenate([vp_ref[...], vc_ref[...]], axis=0).astype(BF16)
    scale = hd ** -0.5

    def scores(h):
        k_h = k_all[:, h * hd:(h + 1) * hd]
        return [_dot_nt(q_ref[:, (h * grp + gi) * hd:(h * grp + gi + 1) * hd], k_h) for gi in range(grp)]

    outs = []
    s_next = scores(0)
    for h in range(n_kv):
        s_cur = s_next
        if h + 1 < n_kv:
            s_next = scores(h + 1)
        v_h = v_all[:, h * hd:(h + 1) * hd]
        ps, dens = [], []
        for gi in range(grp):
            s = jnp.where(valid, s_cur[gi] * scale, -jnp.inf)
            sk = sink_ref[h * grp + gi]
            m = jnp.maximum(jnp.max(s, axis=-1, keepdims=True), sk)
            p = jnp.exp(s - m)
            dens.append(jnp.sum(p, axis=-1, keepdims=True) + jnp.exp(sk - m))
            ps.append(p.astype(BF16))
        outs += [_dot(ps[gi], v_h) / dens[gi] for gi in range(grp)]
    att = jnp.concatenate(outs, axis=1)
    o_ref[...] = (att * _silu(z_ref[...])).astype(o_ref.dtype)


def attn_prompt(sinks, q, k, v, z, n_batch):
    m, e = q.shape
    nb = m // (n_batch * BLOCK)
    kw = k.shape[1]
    cur = lambda i, n: (i * nb + n, 0)
    prv = lambda i, n: (i * nb + jnp.maximum(n - 1, 0), 0)
    return pl.pallas_call(
        _attn_prompt_kernel,
        grid=(n_batch, nb),
        in_specs=[pl.BlockSpec(memory_space=pltpu.SMEM),
                  pl.BlockSpec((BLOCK, e), cur),
                  pl.BlockSpec((BLOCK, kw), cur), pl.BlockSpec((BLOCK, kw), prv),
                  pl.BlockSpec((BLOCK, kw), cur), pl.BlockSpec((BLOCK, kw), prv),
                  pl.BlockSpec((BLOCK, e), cur)],
        out_specs=pl.BlockSpec((BLOCK, e), cur),
        out_shape=jax.ShapeDtypeStruct((m, e), BF16),
        compiler_params=_cparams(("arbitrary", "arbitrary")),
        name="attn_prompt",
    )(sinks, q, k, k, v, v, z)


def _attn_sample_kernel(sink_ref, q_ref, kc_ref, vc_ref, kn_ref, vn_ref, z_ref, o_ref, ko_ref, vo_ref):
    hd = HEAD_DIM
    win = kc_ref.shape[1]
    n_kv = kc_ref.shape[2] // hd
    nq = q_ref.shape[1]
    grp = nq // n_kv
    pad = 8
    kc = kc_ref[0]
    vc = vc_ref[0]
    kn = kn_ref[0]
    vn = vn_ref[0]
    first = lax.broadcasted_iota(jnp.int32, (pad, kc.shape[1]), 0) == 0
    k_all = jnp.concatenate([kc, jnp.where(first, kn, 0.0)], axis=0).astype(BF16)
    v_all = jnp.concatenate([vc, jnp.where(first, vn, 0.0)], axis=0).astype(BF16)
    col = lax.broadcasted_iota(jnp.int32, (grp, win + pad), 1)
    valid = (col <= win) & (win - col <= WINDOW)
    q = q_ref[0].astype(BF16)
    scale = hd ** -0.5
    row_i = lax.broadcasted_iota(jnp.int32, (grp, 1), 0)
    outs = []
    for h in range(n_kv):
        k_h = k_all[:, h * hd:(h + 1) * hd]
        v_h = v_all[:, h * hd:(h + 1) * hd]
        q_h = q[h * grp:(h + 1) * grp, :]
        s = _dot_nt(q_h, k_h) * scale
        s = jnp.where(valid, s, -jnp.inf)
        sk = jnp.zeros((grp, 1), F32)
        for gi in range(grp):
            sk = jnp.where(row_i == gi, sink_ref[h * grp + gi], sk)
        m = jnp.maximum(jnp.max(s, axis=-1, keepdims=True), sk)
        p = jnp.exp(s - m)
        den = jnp.sum(p, axis=-1, keepdims=True) + jnp.exp(sk - m)
        outs.append(_dot(p.astype(BF16), v_h) / den)
    att = jnp.concatenate(outs, axis=0)
    o_ref[0] = (att * _silu(z_ref[0])).astype(o_ref.dtype)
    last = lax.broadcasted_iota(jnp.int32, kc.shape, 0) == win - 1
    ko_ref[0] = jnp.where(last, kn, pltpu.roll(kc, win - 1, axis=0))
    vo_ref[0] = jnp.where(last, vn, pltpu.roll(vc, win - 1, axis=0))


def attn_sample(sinks, q, cache_k, cache_v, k_new, v_new, z):
    m, win, kw = cache_k.shape
    nq = q.shape[1]
    hd = HEAD_DIM
    tok = pl.BlockSpec((1, nq, hd), lambda i: (i, 0, 0))
    cache = pl.BlockSpec((1, win, kw), lambda i: (i, 0, 0))
    new = pl.BlockSpec((1, 1, kw), lambda i: (i, 0, 0))
    return pl.pallas_call(
        _attn_sample_kernel,
        grid=(m,),
        in_specs=[pl.BlockSpec(memory_space=pltpu.SMEM), tok, cache, cache, new, new, tok],
        out_specs=[tok, cache, cache],
        out_shape=[jax.ShapeDtypeStruct((m, nq, hd), BF16),
                   jax.ShapeDtypeStruct(cache_k.shape, F32), jax.ShapeDtypeStruct(cache_v.shape, F32)],
        compiler_params=_cparams(("arbitrary",)),
        name="attn_sample",
    )(sinks, q, cache_k, cache_v, k_new, v_new, z)


def _pad_lora(w_down, w_up):
    r = w_down.shape[1]
    return (jnp.pad(w_down, ((0, 0), (0, LORA_PAD - r))).astype(BF16),
            jnp.pad(w_up, ((0, LORA_PAD - r), (0, 0))).astype(BF16))


def kernel(x_prompt, x_sample, state_wkv, state_shift, cache_k, cache_v, meta_tokens, a_norm, a_mu, a_w_rkvz,
           a_w0, a_w1, a_w2, a_a0, a_a1, a_a2, a_k_k, a_k_a, a_r_k, a_gn_g, a_gn_b, a_w_out, kv_norm, w_kv,
           b_norm, b_w_qz, b_sinks, b_w_o, final_norm):
    nb, seq, d = x_prompt.shape
    db, dseq, _ = x_sample.shape
    assert dseq == 1 and a_norm.shape[0] == 1 and b_norm.shape[0] == 1
    e = a_w_rkvz.shape[3]
    win = cache_k.shape[1]
    p_len = LEAD + N_META + seq
    assert p_len % BLOCK == 0 and (LEAD + N_META) == BLOCK
    m_p = nb * p_len
    kvw = N_KV_HEADS * HEAD_DIM

    w_rkvz = a_w_rkvz[0].astype(BF16)
    w1, w2 = _pad_lora(a_w1[0], a_w2[0])
    a1, a2 = _pad_lora(a_a1[0], a_a2[0])
    w_out = a_w_out[0].astype(BF16)
    w_kv_bf = w_kv.astype(BF16)
    w_q = b_w_qz[0][:, :e].astype(BF16)
    w_z = b_w_qz[0][:, e:].astype(BF16)
    w_o = b_w_o[0].astype(BF16)
    mu = a_mu[0]
    sinks = b_sinks[0]
    gains_b = jnp.stack([kv_norm, b_norm[0]])

    tm = p_len // 8

    hp = jnp.concatenate([jnp.zeros((nb, LEAD, d), F32),
                          jnp.broadcast_to(meta_tokens[None], (nb, N_META, d)), x_prompt], axis=1)
    xm, hw_p, ha_p, x_last = norm_shift_prompt(hp, a_norm[0], mu, w1, a1, tm)
    p_state_shift = x_last.reshape(1, nb, d)
    rkvz = matmul_groups(xm, w_rkvz, tm, F32)
    yg, p_state = wkv_prompt(rkvz, hw_p, ha_p, w2, a_w0[0], a2, a_a0[0], a_k_k[0], a_k_a[0],
                             a_r_k[0].reshape(-1), a_gn_g[0], a_gn_b[0], nb)
    hp, hn_kv, hn_b = matmul_residual_norm(yg, w_out, hp.reshape(m_p, d), gains_b, tm, BF16)

    pos_p = jnp.maximum(jnp.arange(p_len, dtype=jnp.int32) - LEAD, 0)
    tabs_p = tuple(jnp.tile(t, (nb, 1)) for t in rope_tables(pos_p))
    k_p, v_p = matmul_rope(hn_kv, w_kv_bf, tabs_p, tm, kvw, (F32, F32))
    q_p, = matmul_rope(hn_b, w_q, tabs_p, tm, e, (BF16,))
    z_p = matmul_groups(hn_b[None], w_z[None], tm, F32)[0]
    att = attn_prompt(sinks, q_p, k_p, v_p, z_p, nb)
    y_pad, = matmul_residual_norm(att, w_o, hp, final_norm[None], tm, F32, emit_h=False)
    y_prompt = y_pad.reshape(nb, p_len, d)[:, LEAD + N_META:]
    p_cache_k = k_p.reshape(nb, p_len, N_KV_HEADS, HEAD_DIM)[:, -win:]
    p_cache_v = v_p.reshape(nb, p_len, N_KV_HEADS, HEAD_DIM)[:, -win:]

    hs = x_sample.reshape(db, d)
    xm_s, hw_s, ha_s, xn_s = norm_shift_sample(hs, state_shift[0], a_norm[0], mu, w1, a1)
    rkvz_s = matmul_groups(xm_s, w_rkvz, db, F32)
    wl_s, al_s = lora_up(hw_s, ha_s, w2, a_w0[0], a2, a_a0[0])
    yg_s, s_state = wkv_sample(rkvz_s, wl_s, al_s, a_k_k[0], a_k_a[0], a_r_k[0].reshape(-1), a_gn_g[0],
                               a_gn_b[0], state_wkv[0])
    hs, hn_kv_s, hn_b_s = matmul_residual_norm(yg_s.reshape(db, e), w_out, hs, gains_b, db, BF16)
    tabs_s = rope_tables(jnp.full((db,), PAST_LEN, jnp.int32))
    k_s, v_s = matmul_rope(hn_kv_s, w_kv_bf, tabs_s, db, kvw, (F32, F32))
    q_s, = matmul_rope(hn_b_s, w_q, tabs_s, db, e, (F32,))
    z_s = matmul_groups(hn_b_s[None], w_z[None], db, F32)[0]
    nq = e // HEAD_DIM
    att_s, s_cache_k, s_cache_v = attn_sample(
        sinks, q_s.reshape(db, nq, HEAD_DIM), cache_k.reshape(db, win, kvw), cache_v.reshape(db, win, kvw),
        k_s.reshape(db, 1, kvw), v_s.reshape(db, 1, kvw), z_s.reshape(db, nq, HEAD_DIM))
    y_s, = matmul_residual_norm(att_s.reshape(db, e), w_o, hs, final_norm[None], db, F32, emit_h=False)
    y_sample = y_s.reshape(db, 1, d)

    return (y_prompt, y_sample, p_state[None], p_state_shift,
            p_cache_k, p_cache_v,
            s_state[None], xn_s[None],
            s_cache_k.reshape(cache_k.shape), s_cache_v.reshape(cache_v.shape))
```

```python
import functools
import math

import jax
import jax.numpy as jnp
from jax import lax
from jax.experimental import pallas as pl
from jax.experimental.pallas import tpu as pltpu

F32 = jnp.float32
BF16 = jnp.bfloat16

HEAD_DIM = 64
N_KV_HEADS = 8
WINDOW = 128
BLOCK = 128
ROPE_DIM = HEAD_DIM // 4
ROPE_THETA = 500000.0
N_META = 16
PAST_LEN = 16384
RMS_EPS = 1e-6
GN_EPS = 64e-5
LEAD = (-N_META) % BLOCK
CHUNK = 64
WKV_ROWS = 128
HEADS_PER_STREAM = 16
HEADS_PER_STEP = 32
LORA_PAD = 128
MXU_TILE = 256
ROPE_SLAB = 512
WKV_STREAM_LAG = 8
VMEM_LIMIT = 48 * 1024 * 1024
DECAY_SCALE = math.exp(-0.5)


def _cparams(sem):
    return pltpu.CompilerParams(dimension_semantics=sem, vmem_limit_bytes=VMEM_LIMIT)


def _sigmoid(x):
    return 1.0 / (1.0 + jnp.exp(-x))


def _silu(x):
    return x * _sigmoid(x)


def _dot(a, b):
    return jnp.dot(a, b, preferred_element_type=F32)


def _dot_nt(a, b):
    return lax.dot_general(a, b, (((1,), (1,)), ((), ())), preferred_element_type=F32)


def _dot_tn(a, b):
    return lax.dot_general(a, b, (((0,), (0,)), ((), ())), preferred_element_type=F32)


def _split_hi_lo(x):
    hi = x.astype(BF16)
    lo = (x - hi.astype(F32)).astype(BF16)
    return hi, lo


def _mixes(xn, prev, mu_ref, w1_ref, a1_ref, xm_ref, hw_ref, ha_ref):
    xx = prev - xn
    n_proj = xm_ref.shape[0]
    for p in range(n_proj):
        xm_ref[p] = (xn + xx * mu_ref[p:p + 1, :]).astype(xm_ref.dtype)
    xw = (xn + xx * mu_ref[n_proj:n_proj + 1, :]).astype(BF16)
    xa = (xn + xx * mu_ref[n_proj + 1:n_proj + 2, :]).astype(BF16)
    hw_ref[...] = jnp.tanh(_dot(xw, w1_ref[...])).astype(hw_ref.dtype)
    ha_ref[...] = _dot(xa, a1_ref[...]).astype(ha_ref.dtype)


def _norm_shift_kernel(x_ref, g_ref, mu_ref, w1_ref, a1_ref, xm_ref, hw_ref, ha_ref, last_ref, carry_ref):
    @pl.when(pl.program_id(1) == 0)
    def _():
        carry_ref[...] = jnp.zeros_like(carry_ref)

    x = x_ref[0]
    tm = x.shape[0]
    xn = x * lax.rsqrt(jnp.mean(x * x, axis=-1, keepdims=True) + RMS_EPS) * g_ref[...]
    rolled = pltpu.roll(xn, 1, axis=0)
    row = lax.broadcasted_iota(jnp.int32, xn.shape, 0)
    prev = jnp.where(row == 0, carry_ref[0:1, :], rolled)
    _mixes(xn, prev, mu_ref, w1_ref, a1_ref, xm_ref, hw_ref, ha_ref)
    carry_ref[0:1, :] = xn[tm - 1:tm, :]
    last_ref[0] = xn[tm - 1:tm, :]


def norm_shift_prompt(x, g, mu, w1, a1, tm):
    b, p, d = x.shape
    n_mix = mu.shape[0]
    n_proj = n_mix - 2
    lr = w1.shape[1]
    nt = p // tm
    const = lambda shape: pl.BlockSpec(shape, lambda i, t: (0,) * len(shape))
    hid = pl.BlockSpec((tm, lr), lambda i, t: (i * nt + t, 0))
    return pl.pallas_call(
        _norm_shift_kernel,
        grid=(b, nt),
        in_specs=[pl.BlockSpec((1, tm, d), lambda i, t: (i, t, 0)),
                  const((1, d)), const((n_mix, d)), const((d, lr)), const((d, lr))],
        out_specs=[pl.BlockSpec((n_proj, tm, d), lambda i, t: (0, i * nt + t, 0)), hid, hid,
                   pl.BlockSpec((1, 1, d), lambda i, t: (i, 0, 0))],
        out_shape=[jax.ShapeDtypeStruct((n_proj, b * p, d), BF16),
                   jax.ShapeDtypeStruct((b * p, lr), BF16), jax.ShapeDtypeStruct((b * p, lr), BF16),
                   jax.ShapeDtypeStruct((b, 1, d), F32)],
        scratch_shapes=[pltpu.VMEM((8, d), F32)],
        compiler_params=_cparams(("arbitrary", "arbitrary")),
        name="norm_shift_prompt",
    )(x, g.reshape(1, d), mu, w1, a1)


def _norm_shift_sample_kernel(x_ref, prev_ref, g_ref, mu_ref, w1_ref, a1_ref, xm_ref, hw_ref, ha_ref, xn_ref):
    x = x_ref[...]
    xn = x * lax.rsqrt(jnp.mean(x * x, axis=-1, keepdims=True) + RMS_EPS) * g_ref[...]
    xn_ref[...] = xn
    _mixes(xn, prev_ref[...], mu_ref, w1_ref, a1_ref, xm_ref, hw_ref, ha_ref)


def norm_shift_sample(x, prev, g, mu, w1, a1):
    m, d = x.shape
    lr = w1.shape[1]
    return pl.pallas_call(
        _norm_shift_sample_kernel,
        out_shape=[jax.ShapeDtypeStruct((mu.shape[0] - 2, m, d), BF16),
                   jax.ShapeDtypeStruct((m, lr), BF16), jax.ShapeDtypeStruct((m, lr), BF16),
                   jax.ShapeDtypeStruct((m, d), F32)],
        name="norm_shift_sample",
    )(x, prev, g.reshape(1, d), mu, w1, a1)


def _rope(y, cos, sin_a, sin_b):
    half = ROPE_DIM // 2
    step = ROPE_SLAB
    rep = step // cos.shape[1]
    tile = lambda t: jnp.concatenate([t] * rep, axis=1)
    cos_t, sa_t, sb_t = tile(cos), tile(sin_a), tile(sin_b)
    outs = []
    for j in range(y.shape[1] // step):
        ys = y[:, j * step:(j + 1) * step]
        outs.append(ys * cos_t + pltpu.roll(ys, step - half, axis=1) * sa_t + pltpu.roll(ys, half, axis=1) * sb_t)
    return jnp.concatenate(outs, axis=1) if len(outs) > 1 else outs[0]


def _mm_group_kernel(x_ref, w_ref, o_ref):
    o_ref[0] = _dot(x_ref[0], w_ref[0]).astype(o_ref.dtype)


def matmul_groups(x, w, tm, out_dtype):
    g, _, n = w.shape
    _, m, kdim = x.shape
    return pl.pallas_call(
        _mm_group_kernel,
        grid=(g, m // tm),
        in_specs=[pl.BlockSpec((1, tm, kdim), lambda q, i: (q, i, 0)),
                  pl.BlockSpec((1, kdim, n), lambda q, i: (q, 0, 0))],
        out_specs=pl.BlockSpec((1, tm, n), lambda q, i: (q, i, 0)),
        out_shape=jax.ShapeDtypeStruct((g, m, n), out_dtype),
        compiler_params=_cparams(("arbitrary", "arbitrary")),
        name="matmul_groups",
    )(x, w)


def _mm_rope_kernel(x_ref, w_ref, cos_ref, sa_ref, sb_ref, *o_refs, n_rope):
    y = _dot(x_ref[...], w_ref[...])
    rot = _rope(y[:, :n_rope], cos_ref[...], sa_ref[...], sb_ref[...])
    o_refs[0][...] = rot.astype(o_refs[0].dtype)
    if len(o_refs) > 1:
        o_refs[1][...] = y[:, n_rope:].astype(o_refs[1].dtype)


def matmul_rope(x, w, tables, tm, n_rope, out_dtypes):
    m, kdim = x.shape
    n = w.shape[1]
    lanes = tables[0].shape[1]
    widths = [n_rope] + ([n - n_rope] if n > n_rope else [])
    tab = pl.BlockSpec((tm, lanes), lambda i: (i, 0))
    outs = pl.pallas_call(
        functools.partial(_mm_rope_kernel, n_rope=n_rope),
        grid=(m // tm,),
        in_specs=[pl.BlockSpec((tm, kdim), lambda i: (i, 0)),
                  pl.BlockSpec((kdim, n), lambda i: (0, 0), pipeline_mode=pl.Buffered(1)),
                  tab, tab, tab],
        out_specs=[pl.BlockSpec((tm, wd), lambda i: (i, 0)) for wd in widths],
        out_shape=[jax.ShapeDtypeStruct((m, wd), dt) for wd, dt in zip(widths, out_dtypes)],
        compiler_params=_cparams(("arbitrary",)),
        name="matmul_rope",
    )(x, w, *tables)
    return outs


def _mm_res_norm_kernel(x_ref, w_ref, res_ref, g_ref, *out_refs, emit_h):
    h = res_ref[...] + _dot(x_ref[...], w_ref[...])
    hn_refs = out_refs
    if emit_h:
        out_refs[0][...] = h
        hn_refs = out_refs[1:]
    inv = lax.rsqrt(jnp.mean(h * h, axis=-1, keepdims=True) + RMS_EPS)
    for j, hn_ref in enumerate(hn_refs):
        hn_ref[...] = (h * inv * g_ref[j:j + 1, :]).astype(hn_ref.dtype)


def matmul_residual_norm(x, w, res, gains, tm, norm_dtype, emit_h=True):
    m, kdim = x.shape
    n = w.shape[1]
    ng = gains.shape[0]
    row = lambda width: pl.BlockSpec((tm, width), lambda i: (i, 0))
    return pl.pallas_call(
        functools.partial(_mm_res_norm_kernel, emit_h=emit_h),
        grid=(m // tm,),
        in_specs=[row(kdim),
                  pl.BlockSpec((kdim, n), lambda i: (0, 0), pipeline_mode=pl.Buffered(1)),
                  row(n),
                  pl.BlockSpec((ng, n), lambda i: (0, 0))],
        out_specs=[row(n)] * (int(emit_h) + ng),
        out_shape=[jax.ShapeDtypeStruct((m, n), F32)] * int(emit_h) + [jax.ShapeDtypeStruct((m, n), norm_dtype)] * ng,
        compiler_params=_cparams(("arbitrary",)),
        name="matmul_residual_norm",
    )(x, w, res, gains)


def _lora_up_kernel(hw_ref, ha_ref, w2_ref, w0_ref, a2_ref, a0_ref, wl_ref, al_ref):
    wl_ref[...] = w0_ref[...] + _dot(hw_ref[...], w2_ref[...])
    al_ref[...] = a0_ref[...] + _dot(ha_ref[...], a2_ref[...])


def lora_up(hw, ha, w2, w0, a2, a0):
    m = hw.shape[0]
    e = w2.shape[1]
    return pl.pallas_call(
        _lora_up_kernel,
        out_shape=[jax.ShapeDtypeStruct((m, e), F32), jax.ShapeDtypeStruct((m, e), F32)],
        name="lora_up",
    )(hw, ha, w2, w0.reshape(1, e), a2, a0.reshape(1, e))


def _seg_sum(x, ones_bd):
    hi, lo = _split_hi_lo(x)
    outs = []
    for c in range(x.shape[1] // MXU_TILE):
        sl = slice(c * MXU_TILE, (c + 1) * MXU_TILE)
        outs.append(_dot(hi[:, sl], ones_bd) + _dot(lo[:, sl], ones_bd))
    return jnp.concatenate(outs, axis=1) if len(outs) > 1 else outs[0]


def _wkv_prep(slot, r_ref, k_ref, v_ref, hw_ref, ha_ref, w2_ref, w0_ref, a2_ref, a0_ref, kk_ref, ka_ref, rk_ref,
              scr, ones_bd, tri_incl):
    c = CHUNK
    kkd_s, rd_s, bi_s, ki_s, v_s, e_s, bv_s = scr
    tb, width = r_ref.shape[1], r_ref.shape[2]
    nc = tb // c
    slab = MXU_TILE
    for sb in range(width // slab):
        ls = slice(sb * slab, (sb + 1) * slab)
        r = r_ref[0, :, ls]
        k = k_ref[0, :, ls]
        v = v_ref[0, :, ls]
        wl = w0_ref[:, ls] + _dot(hw_ref[...], w2_ref[:, ls])
        al = a0_ref[:, ls] + _dot(ha_ref[...], a2_ref[:, ls])
        kk = k * kk_ref[:, ls]
        n2 = _seg_sum(kk * kk, ones_bd)
        yield
        a = _sigmoid(al)
        lw = -DECAY_SCALE * _sigmoid(wl)
        lw_hi, lw_lo = _split_hi_lo(lw)
        g = _dot(tri_incl, lw_hi) + _dot(tri_incl, lw_lo)
        k2 = k * (1.0 + (a - 1.0) * ka_ref[:, ls])
        bonus = _seg_sum(r * k2 * rk_ref[:, ls], ones_bd)
        kk = kk / jnp.maximum(jnp.sqrt(n2), 1e-12)
        bb = kk * a
        yield
        mid = lambda ci: g[ci * c + c // 2 - 1:ci * c + c // 2, :]
        gm = jnp.concatenate([jnp.broadcast_to(mid(ci), (c, slab)) for ci in range(nc)], axis=0)
        e_inv = jnp.exp(gm - g)
        kkd_s[slot, :, ls] = (kk * jnp.exp(g - lw - gm)).astype(BF16)
        rd_s[slot, :, ls] = (r * jnp.exp(g - gm)).astype(BF16)
        bi_s[slot, :, ls] = (bb * e_inv).astype(BF16)
        ki_s[slot, :, ls] = (k2 * e_inv).astype(BF16)
        v_s[slot, :, ls] = v.astype(BF16)
        bv_s[slot, :, ls] = bonus * v
        for ci in range(nc):
            e_s[slot, 2 * ci:2 * ci + 1, ls] = jnp.exp(mid(ci))
            e_s[slot, 2 * ci + 1:2 * ci + 2, ls] = jnp.exp(g[ci * c + c - 1:ci * c + c, :] - mid(ci))
        yield


def _wkv_main(st, slot, nh, z_ref, gg_ref, gb_ref, yg_ref, s_ref, scr, ones_bd, eye, ti, tj, mask_top_k, mask_bot):
    c = CHUNK
    hd = HEAD_DIM
    kkd_s, rd_s, bi_s, ki_s, v_s, e_s, bv_s = scr
    tb = kkd_s.shape[1]
    nc = tb // c
    h0 = st * nh
    pairs = [(ci, h) for ci in range(nc) for h in range(nh)]
    rows = lambda ci: slice(ci * c, (ci + 1) * c)
    cols = lambda h: slice((h0 + h) * hd, (h0 + h + 1) * hd)
    zeros_cv = jnp.zeros((c, hd), BF16)

    xs = {(ci, h): jnp.concatenate([kkd_s[slot, rows(ci), cols(h)], rd_s[slot, rows(ci), cols(h)]], axis=0)
          for ci, h in pairs}
    r1s = {(ci, h): jnp.concatenate([bi_s[slot, rows(ci), cols(h)], ki_s[slot, rows(ci), cols(h)]], axis=0)
           for ci, h in pairs}
    vs = {(ci, h): v_s[slot, rows(ci), cols(h)] for ci, h in pairs}
    a_mats = {p: _dot_nt(xs[p], r1s[p]) for p in pairs}
    yield
    lk_vs = {p: _dot(jnp.where(mask_top_k, a_mats[p][:c, :], 0.0).astype(BF16),
                     jnp.concatenate([zeros_cv, vs[p]], axis=0)) for p in pairs}
    lps = {p: jnp.where(tj < ti, a_mats[p][:c, :c], 0.0) for p in pairs}
    ts = {p: eye - lps[p] for p in pairs}
    yield
    for _ in range(int(math.log2(c)) - 1):
        lpb = {p: lps[p].astype(BF16) for p in pairs}
        lps = {p: _dot(lpb[p], lpb[p]) for p in pairs}
        yield
        ts = {p: _dot(ts[p].astype(BF16), (eye + lps[p]).astype(BF16)) for p in pairs}
        yield
    a_bots = {p: jnp.where(mask_bot, a_mats[p][c:, :], 0.0).astype(BF16) for p in pairs}
    t_bf = {p: ts[p].astype(BF16) for p in pairs}

    state = [s_ref[h0 + h] for h in range(nh)]
    y_rows = []
    hs = range(nh)
    for ci in range(nc):
        sms = [state[h] * e_s[slot, 2 * ci:2 * ci + 1, cols(h)] for h in hs]
        p_mats = [_dot_nt(xs[ci, h], sms[h].astype(BF16)) for h in hs]
        yield
        us = [-_dot(t_bf[ci, h], (p_mats[h][:c, :] + lk_vs[ci, h]).astype(BF16)) for h in hs]
        yield
        uvs = [jnp.concatenate([us[h].astype(BF16), vs[ci, h]], axis=0) for h in hs]
        ys = [p_mats[h][c:, :] + _dot(a_bots[ci, h], uvs[h]) for h in hs]
        state = [(sms[h] + _dot_tn(uvs[h], r1s[ci, h])) * e_s[slot, 2 * ci + 1:2 * ci + 2, cols(h)] for h in hs]
        y_rows.append(ys)
        yield
    for h in hs:
        s_ref[h0 + h] = state[h]

    inv_n = 1.0 / hd
    slab = MXU_TILE
    hps = slab // hd
    slabs = range(nh // hps)
    y = [jnp.concatenate([jnp.concatenate(y_rows[ci][sb * hps:(sb + 1) * hps], axis=1) for ci in range(nc)], axis=0)
         for sb in slabs]
    means = [_seg_sum(y[sb], ones_bd) * inv_n for sb in slabs]
    yield
    ycs = [y[sb] - means[sb] for sb in slabs]
    vrs = [_seg_sum(ycs[sb] * ycs[sb], ones_bd) * inv_n for sb in slabs]
    yield
    for sb in slabs:
        ls = slice(h0 * hd + sb * slab, h0 * hd + (sb + 1) * slab)
        yn = ycs[sb] * lax.rsqrt(vrs[sb] + GN_EPS) * gg_ref[:, ls] + gb_ref[:, ls]
        yg_ref[:, ls] = ((yn + bv_s[slot, :, ls]) * _silu(z_ref[0, :, ls])).astype(BF16)
        yield


def _interleave(mains, lag, side, side_every):
    live = dict(enumerate(mains))
    tick = 0
    while live:
        for i in sorted(live):
            if tick >= lag * i:
                try:
                    next(live[i])
                except StopIteration:
                    del live[i]
        if side is not None and tick % side_every == 0:
            try:
                next(side)
            except StopIteration:
                side = None
        tick += 1
    if side is not None:
        for _ in side:
            pass


def _wkv_prompt_kernel(r_ref, k_ref, v_ref, z_ref, hw_ref, ha_ref, w2_ref, w0_ref, a2_ref, a0_ref,
                       kk_ref, ka_ref, rk_ref, gg_ref, gb_ref, yg_ref, sout_ref,
                       s_ref, kkd_s, rd_s, bi_s, ki_s, v_s, e_s, bv_s):
    t_idx = pl.program_id(2)
    c = CHUNK
    hd = HEAD_DIM
    tb = r_ref.shape[1]
    nh = HEADS_PER_STREAM
    scr = (kkd_s, rd_s, bi_s, ki_s, v_s, e_s, bv_s)

    li = lax.broadcasted_iota(jnp.int32, (MXU_TILE, MXU_TILE), 0) // hd
    lj = lax.broadcasted_iota(jnp.int32, (MXU_TILE, MXU_TILE), 1) // hd
    ones_bd = jnp.where(li == lj, 1.0, 0.0).astype(BF16)
    bi_ = lax.broadcasted_iota(jnp.int32, (tb, tb), 0)
    bj_ = lax.broadcasted_iota(jnp.int32, (tb, tb), 1)
    tri_incl = jnp.where((bj_ <= bi_) & (bj_ // c == bi_ // c), 1.0, 0.0).astype(BF16)
    ti = lax.broadcasted_iota(jnp.int32, (c, c), 0)
    tj = lax.broadcasted_iota(jnp.int32, (c, c), 1)
    eye = jnp.where(ti == tj, 1.0, 0.0).astype(F32)
    ai = lax.broadcasted_iota(jnp.int32, (c, 2 * c), 0)
    aj = lax.broadcasted_iota(jnp.int32, (c, 2 * c), 1)
    aj_mod = jnp.where(aj >= c, aj - c, aj)
    mask_top_k = (aj >= c) & (aj_mod < ai)
    mask_bot = aj_mod <= ai

    prep = lambda slot: _wkv_prep(slot, r_ref, k_ref, v_ref, hw_ref, ha_ref, w2_ref, w0_ref, a2_ref, a0_ref,
                                  kk_ref, ka_ref, rk_ref, scr, ones_bd, tri_incl)

    @pl.when(t_idx == 0)
    def _():
        s_ref[...] = jnp.zeros_like(s_ref)
        for _ in prep(0):
            pass

    @pl.when(t_idx > 0)
    def _():
        slot_w = t_idx % 2
        mains = [_wkv_main(st, 1 - slot_w, nh, z_ref, gg_ref, gb_ref, yg_ref, s_ref, scr,
                           ones_bd, eye, ti, tj, mask_top_k, mask_bot)
                 for st in range(r_ref.shape[2] // (nh * hd))]
        _interleave(mains, WKV_STREAM_LAG, prep(slot_w), 1)

    @pl.when(t_idx == pl.num_programs(2) - 1)
    def _():
        sout_ref[0] = s_ref[...]


def wkv_prompt(rkvz, hw_act, ha_act, w2, w0, a2, a0, k_k, k_a, r_k, gn_g, gn_b, n_batch):
    _, m, e = rkvz.shape
    p = m // n_batch
    tb = WKV_ROWS
    nt = p // tb
    hw = HEADS_PER_STEP * HEAD_DIM
    nh = e // HEAD_DIM
    nxt = lambda i, t: i * nt + jnp.minimum(t, nt - 1)
    cur = lambda i, t: i * nt + jnp.maximum(t - 1, 0)
    proj_nxt = lambda q: pl.BlockSpec((1, tb, hw), lambda i, g, t: (q, nxt(i, t), g))
    par = pl.BlockSpec((1, hw), lambda i, g, t: (0, g))
    lr = w2.shape[0]
    hid = pl.BlockSpec((tb, lr), lambda i, g, t: (nxt(i, t), 0))
    up = pl.BlockSpec((lr, hw), lambda i, g, t: (0, g))
    slots = lambda dt: pltpu.VMEM((2, tb, hw), dt)
    return pl.pallas_call(
        _wkv_prompt_kernel,
        grid=(n_batch, e // hw, nt + 1),
        in_specs=[proj_nxt(0), proj_nxt(1), proj_nxt(2),
                  pl.BlockSpec((1, tb, hw), lambda i, g, t: (3, cur(i, t), g)),
                  hid, hid, up, par, up, par, par, par, par, par, par],
        out_specs=[pl.BlockSpec((tb, hw), lambda i, g, t: (cur(i, t), g)),
                   pl.BlockSpec((1, HEADS_PER_STEP, HEAD_DIM, HEAD_DIM), lambda i, g, t: (i, g, 0, 0))],
        out_shape=[jax.ShapeDtypeStruct((m, e), BF16),
                   jax.ShapeDtypeStruct((n_batch, nh, HEAD_DIM, HEAD_DIM), F32)],
        scratch_shapes=[pltpu.VMEM((HEADS_PER_STEP, HEAD_DIM, HEAD_DIM), F32),
                        slots(BF16), slots(BF16), slots(BF16), slots(BF16), slots(BF16),
                        pltpu.VMEM((2, 8, hw), F32), slots(F32)],
        compiler_params=_cparams(("arbitrary", "arbitrary", "arbitrary")),
        name="wkv_prompt",
    )(rkvz, rkvz, rkvz, rkvz, hw_act, ha_act, w2, w0.reshape(1, e), a2, a0.reshape(1, e),
      k_k.reshape(1, e), k_a.reshape(1, e), r_k.reshape(1, e), gn_g.reshape(1, e), gn_b.reshape(1, e))


def _wkv_sample_kernel(r_ref, k_ref, v_ref, z_ref, wl_ref, al_ref, kk_ref, ka_ref, rk_ref, gg_ref, gb_ref,
                       s_ref, yg_ref, sout_ref, y_scr):
    hd = HEAD_DIM
    r = r_ref[0]
    k = k_ref[0]
    v = v_ref[0]
    a = _sigmoid(al_ref[0])
    d = jnp.exp(-DECAY_SCALE * _sigmoid(wl_ref[0]))
    kk = k * kk_ref[...]
    kk = kk / jnp.maximum(jnp.sqrt(jnp.sum(kk * kk, axis=-1, keepdims=True)), 1e-12)
    k2 = k * (1.0 + (a - 1.0) * ka_ref[...])
    bb = kk * a
    nh = r.shape[0]
    ii = lax.broadcasted_iota(jnp.int32, (hd, hd), 0)
    jj = lax.broadcasted_iota(jnp.int32, (hd, hd), 1)
    eye = ii == jj

    row = lambda x, h: x[h:h + 1, :]
    group = 8
    for h0 in range(0, nh, group):
        hs = range(h0, h0 + group)
        s = {h: s_ref[0, h] for h in hs}
        sa = {h: jnp.sum(s[h] * row(kk, h), axis=-1, keepdims=True) for h in hs}
        v_col = {h: jnp.sum(jnp.where(eye, row(v, h), 0.0), axis=-1, keepdims=True) for h in hs}
        s_new = {h: s[h] * row(d, h) - sa[h] * row(bb, h) + v_col[h] * row(k2, h) for h in hs}
        y_col = {h: jnp.sum(s_new[h] * row(r, h), axis=-1, keepdims=True) for h in hs}
        for h in hs:
            sout_ref[0, h] = s_new[h]
            y_scr[h:h + 1, :] = jnp.sum(jnp.where(eye, y_col[h], 0.0), axis=0, keepdims=True)

    y = y_scr[...]
    mean = jnp.mean(y, axis=-1, keepdims=True)
    yc = y - mean
    var = jnp.mean(yc * yc, axis=-1, keepdims=True)
    yn = yc * lax.rsqrt(var + GN_EPS) * gg_ref[...] + gb_ref[...]
    bonus = jnp.sum(r * k2 * rk_ref[...], axis=-1, keepdims=True)
    yg_ref[0] = ((yn + bonus * v) * _silu(z_ref[0])).astype(BF16)


def wkv_sample(rkvz, wl, al, k_k, k_a, r_k, gn_g, gn_b, state):
    _, m, e = rkvz.shape
    nh = e // HEAD_DIM
    hd = HEAD_DIM
    rkvz4 = rkvz.reshape(4, m, nh, hd)
    proj = lambda q: pl.BlockSpec((None, 1, nh, hd), lambda i: (q, i, 0, 0))
    tok = pl.BlockSpec((1, nh, hd), lambda i: (i, 0, 0))
    par = pl.BlockSpec((nh, hd), lambda i: (0, 0))
    st = pl.BlockSpec((1, nh, hd, hd), lambda i: (i, 0, 0, 0))
    as_heads = lambda x: x.reshape(nh, hd)
    return pl.pallas_call(
        _wkv_sample_kernel,
        grid=(m,),
        in_specs=[proj(0), proj(1), proj(2), proj(3), tok, tok, par, par, par, par, par, st],
        out_specs=[tok, st],
        out_shape=[jax.ShapeDtypeStruct((m, nh, hd), BF16), jax.ShapeDtypeStruct(state.shape, F32)],
        scratch_shapes=[pltpu.VMEM((nh, hd), F32)],
        compiler_params=_cparams(("arbitrary",)),
        name="wkv_sample",
    )(rkvz4, rkvz4, rkvz4, rkvz4, wl.reshape(m, nh, hd), al.reshape(m, nh, hd),
      as_heads(k_k), as_heads(k_a), as_heads(r_k), as_heads(gn_g), as_heads(gn_b), state)


def rope_tables(pos):
    half = ROPE_DIM // 2
    inv_freq = ROPE_THETA ** (-jnp.arange(half, dtype=F32) * 2.0 / ROPE_DIM)
    ang = pos.astype(F32)[:, None] * inv_freq[None, :]
    cos = jnp.cos(ang)
    sin = jnp.sin(ang)
    rows = pos.shape[0]
    ones = jnp.ones((rows, HEAD_DIM - ROPE_DIM), F32)
    zeros_h = jnp.zeros((rows, half), F32)
    zeros_r = jnp.zeros((rows, HEAD_DIM - ROPE_DIM), F32)
    cos_h = jnp.concatenate([cos, cos, ones], axis=1)
    sa_h = jnp.concatenate([-sin, zeros_h, zeros_r], axis=1)
    sb_h = jnp.concatenate([zeros_h, sin, zeros_r], axis=1)
    two = lambda t: jnp.concatenate([t, t], axis=1)
    return two(cos_h), two(sa_h), two(sb_h)


def _attn_prompt_kernel(sink_ref, q_ref, kc_ref, kp_ref, vc_ref, vp_ref, z_ref, o_ref):
    n = pl.program_id(1)
    hd = HEAD_DIM
    blk = q_ref.shape[0]
    n_kv = kc_ref.shape[1] // hd
    grp = q_ref.shape[1] // (n_kv * hd)
    qi = lax.broadcasted_iota(jnp.int32, (blk, 2 * blk), 0)
    kj = lax.broadcasted_iota(jnp.int32, (blk, 2 * blk), 1) - blk
    kpos = n * blk + kj
    diff = qi - kj
    valid = (kpos >= LEAD) & (diff >= 0) & (diff <= WINDOW)
    k_all = jnp.concatenate([kp_ref[...], kc_ref[...]], axis=0).astype(BF16)
    v_all = jnp.concatenate([vp_ref[...], vc_ref[...]], axis=0).astype(BF16)
    scale = hd ** -0.5

    def scores(h):
        k_h = k_all[:, h * hd:(h + 1) * hd]
        return [_dot_nt(q_ref[:, (h * grp + gi) * hd:(h * grp + gi + 1) * hd], k_h) for gi in range(grp)]

    outs = []
    s_next = scores(0)
    for h in range(n_kv):
        s_cur = s_next
        if h + 1 < n_kv:
            s_next = scores(h + 1)
        v_h = v_all[:, h * hd:(h + 1) * hd]
        ps, dens = [], []
        for gi in range(grp):
            s = jnp.where(valid, s_cur[gi] * scale, -jnp.inf)
            sk = sink_ref[h * grp + gi]
            m = jnp.maximum(jnp.max(s, axis=-1, keepdims=True), sk)
            p = jnp.exp(s - m)
            dens.append(jnp.sum(p, axis=-1, keepdims=True) + jnp.exp(sk - m))
            ps.append(p.astype(BF16))
        outs += [_dot(ps[gi], v_h) / dens[gi] for gi in range(grp)]
    att = jnp.concatenate(outs, axis=1)
    o_ref[...] = (att * _silu(z_ref[...])).astype(o_ref.dtype)


def attn_prompt(sinks, q, k, v, z, n_batch):
    m, e = q.shape
    nb = m // (n_batch * BLOCK)
    kw = k.shape[1]
    cur = lambda i, n: (i * nb + n, 0)
    prv = lambda i, n: (i * nb + jnp.maximum(n - 1, 0), 0)
    return pl.pallas_call(
        _attn_prompt_kernel,
        grid=(n_batch, nb),
        in_specs=[pl.BlockSpec(memory_space=pltpu.SMEM),
                  pl.BlockSpec((BLOCK, e), cur),
                  pl.BlockSpec((BLOCK, kw), cur), pl.BlockSpec((BLOCK, kw), prv),
                  pl.BlockSpec((BLOCK, kw), cur), pl.BlockSpec((BLOCK, kw), prv),
                  pl.BlockSpec((BLOCK, e), cur)],
        out_specs=pl.BlockSpec((BLOCK, e), cur),
        out_shape=jax.ShapeDtypeStruct((m, e), BF16),
        compiler_params=_cparams(("arbitrary", "arbitrary")),
        name="attn_prompt",
    )(sinks, q, k, k, v, v, z)


def _attn_sample_kernel(sink_ref, q_ref, kc_ref, vc_ref, kn_ref, vn_ref, z_ref, o_ref, ko_ref, vo_ref):
    hd = HEAD_DIM
    win = kc_ref.shape[1]
    n_kv = kc_ref.shape[2] // hd
    nq = q_ref.shape[1]
    grp = nq // n_kv
    pad = 8
    kc = kc_ref[0]
    vc = vc_ref[0]
    kn = kn_ref[0]
    vn = vn_ref[0]
    first = lax.broadcasted_iota(jnp.int32, (pad, kc.shape[1]), 0) == 0
    k_all = jnp.concatenate([kc, jnp.where(first, kn, 0.0)], axis=0).astype(BF16)
    v_all = jnp.concatenate([vc, jnp.where(first, vn, 0.0)], axis=0).astype(BF16)
    col = lax.broadcasted_iota(jnp.int32, (grp, win + pad), 1)
    valid = (col <= win) & (win - col <= WINDOW)
    q = q_ref[0].astype(BF16)
    scale = hd ** -0.5
    row_i = lax.broadcasted_iota(jnp.int32, (grp, 1), 0)
    outs = []
    for h in range(n_kv):
        k_h = k_all[:, h * hd:(h + 1) * hd]
        v_h = v_all[:, h * hd:(h + 1) * hd]
        q_h = q[h * grp:(h + 1) * grp, :]
        s = _dot_nt(q_h, k_h) * scale
        s = jnp.where(valid, s, -jnp.inf)
        sk = jnp.zeros((grp, 1), F32)
        for gi in range(grp):
            sk = jnp.where(row_i == gi, sink_ref[h * grp + gi], sk)
        m = jnp.maximum(jnp.max(s, axis=-1, keepdims=True), sk)
        p = jnp.exp(s - m)
        den = jnp.sum(p, axis=-1, keepdims=True) + jnp.exp(sk - m)
        outs.append(_dot(p.astype(BF16), v_h) / den)
    att = jnp.concatenate(outs, axis=0)
    o_ref[0] = (att * _silu(z_ref[0])).astype(o_ref.dtype)
    last = lax.broadcasted_iota(jnp.int32, kc.shape, 0) == win - 1
    ko_ref[0] = jnp.where(last, kn, pltpu.roll(kc, win - 1, axis=0))
    vo_ref[0] = jnp.where(last, vn, pltpu.roll(vc, win - 1, axis=0))


def attn_sample(sinks, q, cache_k, cache_v, k_new, v_new, z):
    m, win, kw = cache_k.shape
    nq = q.shape[1]
    hd = HEAD_DIM
    tok = pl.BlockSpec((1, nq, hd), lambda i: (i, 0, 0))
    cache = pl.BlockSpec((1, win, kw), lambda i: (i, 0, 0))
    new = pl.BlockSpec((1, 1, kw), lambda i: (i, 0, 0))
    return pl.pallas_call(
        _attn_sample_kernel,
        grid=(m,),
        in_specs=[pl.BlockSpec(memory_space=pltpu.SMEM), tok, cache, cache, new, new, tok],
        out_specs=[tok, cache, cache],
        out_shape=[jax.ShapeDtypeStruct((m, nq, hd), BF16),
                   jax.ShapeDtypeStruct(cache_k.shape, F32), jax.ShapeDtypeStruct(cache_v.shape, F32)],
        compiler_params=_cparams(("arbitrary",)),
        name="attn_sample",
    )(sinks, q, cache_k, cache_v, k_new, v_new, z)


def _pad_lora(w_down, w_up):
    r = w_down.shape[1]
    return (jnp.pad(w_down, ((0, 0), (0, LORA_PAD - r))).astype(BF16),
            jnp.pad(w_up, ((0, LORA_PAD - r), (0, 0))).astype(BF16))


def kernel(x_prompt, x_sample, state_wkv, state_shift, cache_k, cache_v, meta_tokens, a_norm, a_mu, a_w_rkvz,
           a_w0, a_w1, a_w2, a_a0, a_a1, a_a2, a_k_k, a_k_a, a_r_k, a_gn_g, a_gn_b, a_w_out, kv_norm, w_kv,
           b_norm, b_w_qz, b_sinks, b_w_o, final_norm):
    nb, seq, d = x_prompt.shape
    db, dseq, _ = x_sample.shape
    assert dseq == 1 and a_norm.shape[0] == 1 and b_norm.shape[0] == 1
    e = a_w_rkvz.shape[3]
    win = cache_k.shape[1]
    p_len = LEAD + N_META + seq
    assert p_len % BLOCK == 0 and (LEAD + N_META) == BLOCK
    m_p = nb * p_len
    kvw = N_KV_HEADS * HEAD_DIM

    w_rkvz = a_w_rkvz[0].astype(BF16)
    w1, w2 = _pad_lora(a_w1[0], a_w2[0])
    a1, a2 = _pad_lora(a_a1[0], a_a2[0])
    w_out = a_w_out[0].astype(BF16)
    w_kv_bf = w_kv.astype(BF16)
    w_q = b_w_qz[0][:, :e].astype(BF16)
    w_z = b_w_qz[0][:, e:].astype(BF16)
    w_o = b_w_o[0].astype(BF16)
    mu = a_mu[0]
    sinks = b_sinks[0]
    gains_b = jnp.stack([kv_norm, b_norm[0]])

    tm = p_len // 8

    hp = jnp.concatenate([jnp.zeros((nb, LEAD, d), F32),
                          jnp.broadcast_to(meta_tokens[None], (nb, N_META, d)), x_prompt], axis=1)
    xm, hw_p, ha_p, x_last = norm_shift_prompt(hp, a_norm[0], mu, w1, a1, tm)
    p_state_shift = x_last.reshape(1, nb, d)
    rkvz = matmul_groups(xm, w_rkvz, tm, F32)
    yg, p_state = wkv_prompt(rkvz, hw_p, ha_p, w2, a_w0[0], a2, a_a0[0], a_k_k[0], a_k_a[0],
                             a_r_k[0].reshape(-1), a_gn_g[0], a_gn_b[0], nb)
    hp, hn_kv, hn_b = matmul_residual_norm(yg, w_out, hp.reshape(m_p, d), gains_b, tm, BF16)

    pos_p = jnp.maximum(jnp.arange(p_len, dtype=jnp.int32) - LEAD, 0)
    tabs_p = tuple(jnp.tile(t, (nb, 1)) for t in rope_tables(pos_p))
    k_p, v_p = matmul_rope(hn_kv, w_kv_bf, tabs_p, tm, kvw, (F32, F32))
    q_p, = matmul_rope(hn_b, w_q, tabs_p, tm, e, (BF16,))
    z_p = matmul_groups(hn_b[None], w_z[None], tm, F32)[0]
    att = attn_prompt(sinks, q_p, k_p, v_p, z_p, nb)
    y_pad, = matmul_residual_norm(att, w_o, hp, final_norm[None], tm, F32, emit_h=False)
    y_prompt = y_pad.reshape(nb, p_len, d)[:, LEAD + N_META:]
    p_cache_k = k_p.reshape(nb, p_len, N_KV_HEADS, HEAD_DIM)[:, -win:]
    p_cache_v = v_p.reshape(nb, p_len, N_KV_HEADS, HEAD_DIM)[:, -win:]

    hs = x_sample.reshape(db, d)
    xm_s, hw_s, ha_s, xn_s = norm_shift_sample(hs, state_shift[0], a_norm[0], mu, w1, a1)
    rkvz_s = matmul_groups(xm_s, w_rkvz, db, F32)
    wl_s, al_s = lora_up(hw_s, ha_s, w2, a_w0[0], a2, a_a0[0])
    yg_s, s_state = wkv_sample(rkvz_s, wl_s, al_s, a_k_k[0], a_k_a[0], a_r_k[0].reshape(-1), a_gn_g[0],
                               a_gn_b[0], state_wkv[0])
    hs, hn_kv_s, hn_b_s = matmul_residual_norm(yg_s.reshape(db, e), w_out, hs, gains_b, db, BF16)
    tabs_s = rope_tables(jnp.full((db,), PAST_LEN, jnp.int32))
    k_s, v_s = matmul_rope(hn_kv_s, w_kv_bf, tabs_s, db, kvw, (F32, F32))
    q_s, = matmul_rope(hn_b_s, w_q, tabs_s, db, e, (F32,))
    z_s = matmul_groups(hn_b_s[None], w_z[None], db, F32)[0]
    nq = e // HEAD_DIM
    att_s, s_cache_k, s_cache_v = attn_sample(
        sinks, q_s.reshape(db, nq, HEAD_DIM), cache_k.reshape(db, win, kvw), cache_v.reshape(db, win, kvw),
        k_s.reshape(db, 1, kvw), v_s.reshape(db, 1, kvw), z_s.reshape(db, nq, HEAD_DIM))
    y_s, = matmul_residual_norm(att_s.reshape(db, e), w_o, hs, final_norm[None], db, F32, emit_h=False)
    y_sample = y_s.reshape(db, 1, d)

    return (y_prompt, y_sample, p_state[None], p_state_shift,
            p_cache_k, p_cache_v,
            s_state[None], xn_s[None],
            s_cache_k.reshape(cache_k.shape), s_cache_v.reshape(cache_v.shape))
```

```python
import functools
import math

import jax
import jax.numpy as jnp
from jax import lax
from jax.experimental import pallas as pl
from jax.experimental.pallas import tpu as pltpu

F32 = jnp.float32
BF16 = jnp.bfloat16

HEAD_DIM = 64
N_KV_HEADS = 8
WINDOW = 128
BLOCK = 128
ROPE_DIM = HEAD_DIM // 4
ROPE_THETA = 500000.0
N_META = 16
PAST_LEN = 16384
RMS_EPS = 1e-6
GN_EPS = 64e-5
LEAD = (-N_META) % BLOCK
CHUNK = 64
WKV_ROWS = 128
HEADS_PER_STREAM = 32
HEADS_PER_STEP = 32
LORA_PAD = 128
MXU_TILE = 256
ROPE_SLAB = 512
VMEM_LIMIT = 48 * 1024 * 1024
DECAY_SCALE = math.exp(-0.5)
Q_SCALE = HEAD_DIM ** -0.5


def _cparams(sem):
    return pltpu.CompilerParams(dimension_semantics=sem, vmem_limit_bytes=VMEM_LIMIT)


def _sigmoid(x):
    return 1.0 / (1.0 + jnp.exp(-x))


def _silu(x):
    return x * _sigmoid(x)


def _dot(a, b):
    return jnp.dot(a, b, preferred_element_type=F32)


def _dot_nt(a, b):
    return lax.dot_general(a, b, (((1,), (1,)), ((), ())), preferred_element_type=F32)


def _dot_tn(a, b):
    return lax.dot_general(a, b, (((0,), (0,)), ((), ())), preferred_element_type=F32)


def _split_hi_lo(x):
    hi = x.astype(BF16)
    lo = (x - hi.astype(F32)).astype(BF16)
    return hi, lo


def _mixes(xn, prev, mu_ref, w1_ref, a1_ref, xm_ref, hw_ref, ha_ref):
    xx = prev - xn
    n_proj = xm_ref.shape[0]
    for p in range(n_proj):
        xm_ref[p] = (xn + xx * mu_ref[p:p + 1, :]).astype(xm_ref.dtype)
    xw = (xn + xx * mu_ref[n_proj:n_proj + 1, :]).astype(BF16)
    xa = (xn + xx * mu_ref[n_proj + 1:n_proj + 2, :]).astype(BF16)
    hw_ref[...] = jnp.tanh(_dot(xw, w1_ref[...])).astype(hw_ref.dtype)
    ha_ref[...] = _dot(xa, a1_ref[...]).astype(ha_ref.dtype)


def _norm_shift_kernel(x_ref, g_ref, mu_ref, w1_ref, a1_ref, xm_ref, hw_ref, ha_ref, last_ref, carry_ref):
    @pl.when(pl.program_id(1) == 0)
    def _():
        carry_ref[...] = jnp.zeros_like(carry_ref)

    x = x_ref[0]
    tm = x.shape[0]
    xn = x * lax.rsqrt(jnp.mean(x * x, axis=-1, keepdims=True) + RMS_EPS) * g_ref[...]
    rolled = pltpu.roll(xn, 1, axis=0)
    row = lax.broadcasted_iota(jnp.int32, xn.shape, 0)
    prev = jnp.where(row == 0, carry_ref[0:1, :], rolled)
    _mixes(xn, prev, mu_ref, w1_ref, a1_ref, xm_ref, hw_ref, ha_ref)
    carry_ref[0:1, :] = xn[tm - 1:tm, :]
    last_ref[0] = xn[tm - 1:tm, :]


def norm_shift_prompt(x, g, mu, w1, a1, tm):
    b, p, d = x.shape
    n_mix = mu.shape[0]
    n_proj = n_mix - 2
    lr = w1.shape[1]
    nt = p // tm
    const = lambda shape: pl.BlockSpec(shape, lambda i, t: (0,) * len(shape))
    hid = pl.BlockSpec((tm, lr), lambda i, t: (i * nt + t, 0))
    return pl.pallas_call(
        _norm_shift_kernel,
        grid=(b, nt),
        in_specs=[pl.BlockSpec((1, tm, d), lambda i, t: (i, t, 0)),
                  const((1, d)), const((n_mix, d)), const((d, lr)), const((d, lr))],
        out_specs=[pl.BlockSpec((n_proj, tm, d), lambda i, t: (0, i * nt + t, 0)), hid, hid,
                   pl.BlockSpec((1, 1, d), lambda i, t: (i, 0, 0))],
        out_shape=[jax.ShapeDtypeStruct((n_proj, b * p, d), BF16),
                   jax.ShapeDtypeStruct((b * p, lr), BF16), jax.ShapeDtypeStruct((b * p, lr), BF16),
                   jax.ShapeDtypeStruct((b, 1, d), F32)],
        scratch_shapes=[pltpu.VMEM((8, d), F32)],
        compiler_params=_cparams(("arbitrary", "arbitrary")),
        name="norm_shift_prompt",
    )(x, g.reshape(1, d), mu, w1, a1)


def _norm_shift_sample_kernel(x_ref, prev_ref, g_ref, mu_ref, w1_ref, a1_ref, xm_ref, hw_ref, ha_ref, xn_ref):
    x = x_ref[...]
    xn = x * lax.rsqrt(jnp.mean(x * x, axis=-1, keepdims=True) + RMS_EPS) * g_ref[...]
    xn_ref[...] = xn
    _mixes(xn, prev_ref[...], mu_ref, w1_ref, a1_ref, xm_ref, hw_ref, ha_ref)


def norm_shift_sample(x, prev, g, mu, w1, a1):
    m, d = x.shape
    lr = w1.shape[1]
    return pl.pallas_call(
        _norm_shift_sample_kernel,
        out_shape=[jax.ShapeDtypeStruct((mu.shape[0] - 2, m, d), BF16),
                   jax.ShapeDtypeStruct((m, lr), BF16), jax.ShapeDtypeStruct((m, lr), BF16),
                   jax.ShapeDtypeStruct((m, d), F32)],
        name="norm_shift_sample",
    )(x, prev, g.reshape(1, d), mu, w1, a1)


def _rope(y, cos, sin_a, sin_b):
    half = ROPE_DIM // 2
    step = ROPE_SLAB
    rep = step // cos.shape[1]
    tile = lambda t: jnp.concatenate([t] * rep, axis=1)
    cos_t, sa_t, sb_t = tile(cos), tile(sin_a), tile(sin_b)
    outs = []
    for j in range(y.shape[1] // step):
        ys = y[:, j * step:(j + 1) * step]
        outs.append(ys * cos_t + pltpu.roll(ys, step - half, axis=1) * sa_t + pltpu.roll(ys, half, axis=1) * sb_t)
    return jnp.concatenate(outs, axis=1) if len(outs) > 1 else outs[0]


def _mm_group_kernel(x_ref, w_ref, o_ref):
    o_ref[0] = _dot(x_ref[0], w_ref[0]).astype(o_ref.dtype)


def matmul_groups(x, w, tm, out_dtype):
    g, _, n = w.shape
    _, m, kdim = x.shape
    return pl.pallas_call(
        _mm_group_kernel,
        grid=(g, m // tm),
        in_specs=[pl.BlockSpec((1, tm, kdim), lambda q, i: (q, i, 0)),
                  pl.BlockSpec((1, kdim, n), lambda q, i: (q, 0, 0))],
        out_specs=pl.BlockSpec((1, tm, n), lambda q, i: (q, i, 0)),
        out_shape=jax.ShapeDtypeStruct((g, m, n), out_dtype),
        compiler_params=_cparams(("arbitrary", "arbitrary")),
        name="matmul_groups",
    )(x, w)


def _mm_rope_kernel(x_ref, w_ref, cos_ref, sa_ref, sb_ref, *o_refs, n_rope, scale):
    y = _dot(x_ref[...], w_ref[...])
    rot = _rope(y[:, :n_rope], cos_ref[...], sa_ref[...], sb_ref[...])
    if scale != 1.0:
        rot = rot * scale
    o_refs[0][...] = rot.astype(o_refs[0].dtype)
    if len(o_refs) > 1:
        o_refs[1][...] = y[:, n_rope:].astype(o_refs[1].dtype)


def matmul_rope(x, w, tables, tm, n_rope, out_dtypes, scale=1.0):
    m, kdim = x.shape
    n = w.shape[1]
    lanes = tables[0].shape[1]
    widths = [n_rope] + ([n - n_rope] if n > n_rope else [])
    tab = pl.BlockSpec((tm, lanes), lambda i: (i, 0))
    outs = pl.pallas_call(
        functools.partial(_mm_rope_kernel, n_rope=n_rope, scale=scale),
        grid=(m // tm,),
        in_specs=[pl.BlockSpec((tm, kdim), lambda i: (i, 0)),
                  pl.BlockSpec((kdim, n), lambda i: (0, 0), pipeline_mode=pl.Buffered(1)),
                  tab, tab, tab],
        out_specs=[pl.BlockSpec((tm, wd), lambda i: (i, 0)) for wd in widths],
        out_shape=[jax.ShapeDtypeStruct((m, wd), dt) for wd, dt in zip(widths, out_dtypes)],
        compiler_params=_cparams(("arbitrary",)),
        name="matmul_rope",
    )(x, w, *tables)
    return outs


def _mm_res_norm_kernel(x_ref, w_ref, res_ref, g_ref, *out_refs, emit_h):
    h = res_ref[...] + _dot(x_ref[...], w_ref[...])
    hn_refs = out_refs
    if emit_h:
        out_refs[0][...] = h
        hn_refs = out_refs[1:]
    inv = lax.rsqrt(jnp.mean(h * h, axis=-1, keepdims=True) + RMS_EPS)
    for j, hn_ref in enumerate(hn_refs):
        hn_ref[...] = (h * inv * g_ref[j:j + 1, :]).astype(hn_ref.dtype)


def matmul_residual_norm(x, w, res, gains, tm, norm_dtype, emit_h=True):
    m, kdim = x.shape
    n = w.shape[1]
    ng = gains.shape[0]
    row = lambda width: pl.BlockSpec((tm, width), lambda i: (i, 0))
    return pl.pallas_call(
        functools.partial(_mm_res_norm_kernel, emit_h=emit_h),
        grid=(m // tm,),
        in_specs=[row(kdim),
                  pl.BlockSpec((kdim, n), lambda i: (0, 0), pipeline_mode=pl.Buffered(1)),
                  row(n),
                  pl.BlockSpec((ng, n), lambda i: (0, 0))],
        out_specs=[row(n)] * (int(emit_h) + ng),
        out_shape=[jax.ShapeDtypeStruct((m, n), F32)] * int(emit_h) + [jax.ShapeDtypeStruct((m, n), norm_dtype)] * ng,
        compiler_params=_cparams(("arbitrary",)),
        name="matmul_residual_norm",
    )(x, w, res, gains)


def _lora_up_kernel(hw_ref, ha_ref, w2_ref, w0_ref, a2_ref, a0_ref, wl_ref, al_ref):
    wl_ref[...] = w0_ref[...] + _dot(hw_ref[...], w2_ref[...])
    al_ref[...] = a0_ref[...] + _dot(ha_ref[...], a2_ref[...])


def lora_up(hw, ha, w2, w0, a2, a0):
    m = hw.shape[0]
    e = w2.shape[1]
    return pl.pallas_call(
        _lora_up_kernel,
        out_shape=[jax.ShapeDtypeStruct((m, e), F32), jax.ShapeDtypeStruct((m, e), F32)],
        name="lora_up",
    )(hw, ha, w2, w0.reshape(1, e), a2, a0.reshape(1, e))


def _seg_sum(x, ones_bd):
    hi, lo = _split_hi_lo(x)
    outs = []
    for c in range(x.shape[1] // MXU_TILE):
        sl = slice(c * MXU_TILE, (c + 1) * MXU_TILE)
        outs.append(_dot(hi[:, sl], ones_bd) + _dot(lo[:, sl], ones_bd))
    return jnp.concatenate(outs, axis=1) if len(outs) > 1 else outs[0]


def _wkv_prompt_kernel(r_ref, k_ref, v_ref, z_ref, hw_ref, ha_ref, w2_ref, w0_ref, a2_ref, a0_ref,
                       kk_ref, ka_ref, rk_ref, gg_ref, gb_ref, yg_ref, sout_ref, s_ref):
    t_idx = pl.program_id(2)
    c = CHUNK
    hd = HEAD_DIM

    @pl.when(t_idx == 0)
    def _():
        s_ref[...] = jnp.zeros_like(s_ref)

    tb = r_ref.shape[1]
    nh = HEADS_PER_STREAM
    hw = nh * hd
    nc = tb // c

    li = lax.broadcasted_iota(jnp.int32, (MXU_TILE, MXU_TILE), 0) // hd
    lj = lax.broadcasted_iota(jnp.int32, (MXU_TILE, MXU_TILE), 1) // hd
    ones_bd = jnp.where(li == lj, 1.0, 0.0).astype(BF16)
    bi_ = lax.broadcasted_iota(jnp.int32, (tb, tb), 0)
    bj_ = lax.broadcasted_iota(jnp.int32, (tb, tb), 1)
    tri_incl = jnp.where((bj_ <= bi_) & (bj_ // c == bi_ // c), 1.0, 0.0).astype(BF16)
    ti = lax.broadcasted_iota(jnp.int32, (c, c), 0)
    tj = lax.broadcasted_iota(jnp.int32, (c, c), 1)
    eye = jnp.where(ti == tj, 1.0, 0.0).astype(F32)
    ai = lax.broadcasted_iota(jnp.int32, (c, 2 * c), 0)
    aj = lax.broadcasted_iota(jnp.int32, (c, 2 * c), 1)
    aj_mod = jnp.where(aj >= c, aj - c, aj)
    mask_top_k = (aj >= c) & (aj_mod < ai)
    mask_bot = aj_mod <= ai

    for st in range(r_ref.shape[2] // hw):
        _wkv_stream(st, hw, nc, r_ref, k_ref, v_ref, z_ref, hw_ref, ha_ref, w2_ref, w0_ref, a2_ref, a0_ref,
                    kk_ref, ka_ref, rk_ref, gg_ref, gb_ref, yg_ref, s_ref,
                    ones_bd, tri_incl, eye, ti, tj, mask_top_k, mask_bot)

    @pl.when(t_idx == pl.num_programs(2) - 1)
    def _():
        sout_ref[0] = s_ref[...]


def _wkv_stream(st, hw, nc, r_ref, k_ref, v_ref, z_ref, hw_ref, ha_ref, w2_ref, w0_ref, a2_ref, a0_ref,
                kk_ref, ka_ref, rk_ref, gg_ref, gb_ref, yg_ref, s_ref,
                ones_bd, tri_incl, eye, ti, tj, mask_top_k, mask_bot):
    c = CHUNK
    hd = HEAD_DIM
    nh = hw // hd
    ls = slice(st * hw, (st + 1) * hw)
    h0 = st * nh
    r = r_ref[0, :, ls]
    k = k_ref[0, :, ls]
    v = v_ref[0, :, ls]
    wl = w0_ref[:, ls] + _dot(hw_ref[...], w2_ref[:, ls])
    al = a0_ref[:, ls] + _dot(ha_ref[...], a2_ref[:, ls])
    a = _sigmoid(al)
    lw = -DECAY_SCALE * _sigmoid(wl)
    kk = k * kk_ref[:, ls]
    n2 = _seg_sum(kk * kk, ones_bd)
    kk = kk / jnp.maximum(jnp.sqrt(n2), 1e-12)
    k2 = k * (1.0 + (a - 1.0) * ka_ref[:, ls])
    bb = kk * a

    lw_hi, lw_lo = _split_hi_lo(lw)
    g = _dot(tri_incl, lw_hi) + _dot(tri_incl, lw_lo)
    mid = lambda ci: g[ci * c + c // 2 - 1:ci * c + c // 2, :]
    gm = jnp.concatenate([jnp.broadcast_to(mid(ci), (c, hw)) for ci in range(nc)], axis=0)
    e_a = jnp.exp(g - gm)
    e_prev = jnp.exp(g - lw - gm)
    e_inv = jnp.exp(gm - g)
    e1 = [jnp.exp(mid(ci)) for ci in range(nc)]
    e2 = [jnp.exp(g[ci * c + c - 1:ci * c + c, :] - mid(ci)) for ci in range(nc)]

    kkd = (kk * e_prev).astype(BF16)
    rd = (r * e_a).astype(BF16)
    bi = (bb * e_inv).astype(BF16)
    ki = (k2 * e_inv).astype(BF16)
    v_bf = v.astype(BF16)
    zeros_cv = jnp.zeros((c, hd), BF16)

    pairs = [(ci, h) for ci in range(nc) for h in range(nh)]
    rows = lambda ci: slice(ci * c, (ci + 1) * c)
    cols = lambda h: slice(h * hd, (h + 1) * hd)
    xs = {(ci, h): jnp.concatenate([kkd[rows(ci), cols(h)], rd[rows(ci), cols(h)]], axis=0) for ci, h in pairs}
    r1s = {(ci, h): jnp.concatenate([bi[rows(ci), cols(h)], ki[rows(ci), cols(h)]], axis=0) for ci, h in pairs}
    vs = {(ci, h): v_bf[rows(ci), cols(h)] for ci, h in pairs}
    a_mats = {p: _dot_nt(xs[p], r1s[p]) for p in pairs}
    lk_vs = {p: _dot(jnp.where(mask_top_k, a_mats[p][:c, :], 0.0).astype(BF16),
                     jnp.concatenate([zeros_cv, vs[p]], axis=0)) for p in pairs}
    lps = {p: jnp.where(tj < ti, a_mats[p][:c, :c], 0.0) for p in pairs}
    ts = {p: eye - lps[p] for p in pairs}
    for _ in range(int(math.log2(c)) - 1):
        lpb = {p: lps[p].astype(BF16) for p in pairs}
        lps = {p: _dot(lpb[p], lpb[p]) for p in pairs}
        ts = {p: _dot(ts[p].astype(BF16), (eye + lps[p]).astype(BF16)) for p in pairs}
    a_bots = {p: jnp.where(mask_bot, a_mats[p][c:, :], 0.0).astype(BF16) for p in pairs}
    t_bf = {p: ts[p].astype(BF16) for p in pairs}

    state = [s_ref[h0 + h] for h in range(nh)]
    y_rows = []
    for ci in range(nc):
        hs = range(nh)
        sms = [state[h] * e1[ci][:, cols(h)] for h in hs]
        p_mats = [_dot_nt(xs[ci, h], sms[h].astype(BF16)) for h in hs]
        us = [-_dot(t_bf[ci, h], (p_mats[h][:c, :] + lk_vs[ci, h]).astype(BF16)) for h in hs]
        uvs = [jnp.concatenate([us[h].astype(BF16), vs[ci, h]], axis=0) for h in hs]
        ys = [p_mats[h][c:, :] + _dot(a_bots[ci, h], uvs[h]) for h in hs]
        state = [(sms[h] + _dot_tn(uvs[h], r1s[ci, h])) * e2[ci][:, cols(h)] for h in hs]
        y_rows.append(jnp.concatenate(ys, axis=1))
    for h in range(nh):
        s_ref[h0 + h] = state[h]
    y = jnp.concatenate(y_rows, axis=0) if nc > 1 else y_rows[0]

    inv_n = 1.0 / hd
    mean = _seg_sum(y, ones_bd) * inv_n
    yc = y - mean
    var = _seg_sum(yc * yc, ones_bd) * inv_n
    yn = yc * lax.rsqrt(var + GN_EPS) * gg_ref[:, ls] + gb_ref[:, ls]
    bonus = _seg_sum(r * k2 * rk_ref[:, ls], ones_bd)
    yg_ref[:, ls] = ((yn + bonus * v) * _silu(z_ref[0, :, ls])).astype(BF16)


def wkv_prompt(rkvz, hw_act, ha_act, w2, w0, a2, a0, k_k, k_a, r_k, gn_g, gn_b, n_batch):
    _, m, e = rkvz.shape
    p = m // n_batch
    tb = WKV_ROWS
    nt = p // tb
    hw = HEADS_PER_STEP * HEAD_DIM
    nh = e // HEAD_DIM
    row = lambda i, g, t: (i * nt + t, g)
    proj = lambda q: pl.BlockSpec((1, tb, hw), lambda i, g, t: (q, i * nt + t, g))
    par = pl.BlockSpec((1, hw), lambda i, g, t: (0, g))
    lr = w2.shape[0]
    hid = pl.BlockSpec((tb, lr), lambda i, g, t: (i * nt + t, 0))
    up = pl.BlockSpec((lr, hw), lambda i, g, t: (0, g))
    return pl.pallas_call(
        _wkv_prompt_kernel,
        grid=(n_batch, e // hw, nt),
        in_specs=[proj(0), proj(1), proj(2), proj(3), hid, hid, up, par, up, par,
                  par, par, par, par, par],
        out_specs=[pl.BlockSpec((tb, hw), row),
                   pl.BlockSpec((1, HEADS_PER_STEP, HEAD_DIM, HEAD_DIM), lambda i, g, t: (i, g, 0, 0))],
        out_shape=[jax.ShapeDtypeStruct((m, e), BF16),
                   jax.ShapeDtypeStruct((n_batch, nh, HEAD_DIM, HEAD_DIM), F32)],
        scratch_shapes=[pltpu.VMEM((HEADS_PER_STEP, HEAD_DIM, HEAD_DIM), F32)],
        compiler_params=_cparams(("arbitrary", "arbitrary", "arbitrary")),
        name="wkv_prompt",
    )(rkvz, rkvz, rkvz, rkvz, hw_act, ha_act, w2, w0.reshape(1, e), a2, a0.reshape(1, e),
      k_k.reshape(1, e), k_a.reshape(1, e), r_k.reshape(1, e), gn_g.reshape(1, e), gn_b.reshape(1, e))


def _wkv_sample_kernel(r_ref, k_ref, v_ref, z_ref, wl_ref, al_ref, kk_ref, ka_ref, rk_ref, gg_ref, gb_ref,
                       s_ref, yg_ref, sout_ref, y_scr):
    hd = HEAD_DIM
    r = r_ref[0]
    k = k_ref[0]
    v = v_ref[0]
    a = _sigmoid(al_ref[0])
    d = jnp.exp(-DECAY_SCALE * _sigmoid(wl_ref[0]))
    kk = k * kk_ref[...]
    kk = kk / jnp.maximum(jnp.sqrt(jnp.sum(kk * kk, axis=-1, keepdims=True)), 1e-12)
    k2 = k * (1.0 + (a - 1.0) * ka_ref[...])
    bb = kk * a
    nh = r.shape[0]
    ii = lax.broadcasted_iota(jnp.int32, (hd, hd), 0)
    jj = lax.broadcasted_iota(jnp.int32, (hd, hd), 1)
    eye = ii == jj

    row = lambda x, h: x[h:h + 1, :]
    group = 8
    for h0 in range(0, nh, group):
        hs = range(h0, h0 + group)
        s = {h: s_ref[0, h] for h in hs}
        sa = {h: jnp.sum(s[h] * row(kk, h), axis=-1, keepdims=True) for h in hs}
        v_col = {h: jnp.sum(jnp.where(eye, row(v, h), 0.0), axis=-1, keepdims=True) for h in hs}
        s_new = {h: s[h] * row(d, h) - sa[h] * row(bb, h) + v_col[h] * row(k2, h) for h in hs}
        y_col = {h: jnp.sum(s_new[h] * row(r, h), axis=-1, keepdims=True) for h in hs}
        for h in hs:
            sout_ref[0, h] = s_new[h]
            y_scr[h:h + 1, :] = jnp.sum(jnp.where(eye, y_col[h], 0.0), axis=0, keepdims=True)

    y = y_scr[...]
    mean = jnp.mean(y, axis=-1, keepdims=True)
    yc = y - mean
    var = jnp.mean(yc * yc, axis=-1, keepdims=True)
    yn = yc * lax.rsqrt(var + GN_EPS) * gg_ref[...] + gb_ref[...]
    bonus = jnp.sum(r * k2 * rk_ref[...], axis=-1, keepdims=True)
    yg_ref[0] = ((yn + bonus * v) * _silu(z_ref[0])).astype(BF16)


def wkv_sample(rkvz, wl, al, k_k, k_a, r_k, gn_g, gn_b, state):
    _, m, e = rkvz.shape
    nh = e // HEAD_DIM
    hd = HEAD_DIM
    rkvz4 = rkvz.reshape(4, m, nh, hd)
    proj = lambda q: pl.BlockSpec((None, 1, nh, hd), lambda i: (q, i, 0, 0))
    tok = pl.BlockSpec((1, nh, hd), lambda i: (i, 0, 0))
    par = pl.BlockSpec((nh, hd), lambda i: (0, 0))
    st = pl.BlockSpec((1, nh, hd, hd), lambda i: (i, 0, 0, 0))
    as_heads = lambda x: x.reshape(nh, hd)
    return pl.pallas_call(
        _wkv_sample_kernel,
        grid=(m,),
        in_specs=[proj(0), proj(1), proj(2), proj(3), tok, tok, par, par, par, par, par, st],
        out_specs=[tok, st],
        out_shape=[jax.ShapeDtypeStruct((m, nh, hd), BF16), jax.ShapeDtypeStruct(state.shape, F32)],
        scratch_shapes=[pltpu.VMEM((nh, hd), F32)],
        compiler_params=_cparams(("arbitrary",)),
        name="wkv_sample",
    )(rkvz4, rkvz4, rkvz4, rkvz4, wl.reshape(m, nh, hd), al.reshape(m, nh, hd),
      as_heads(k_k), as_heads(k_a), as_heads(r_k), as_heads(gn_g), as_heads(gn_b), state)


def rope_tables(pos):
    half = ROPE_DIM // 2
    inv_freq = ROPE_THETA ** (-jnp.arange(half, dtype=F32) * 2.0 / ROPE_DIM)
    ang = pos.astype(F32)[:, None] * inv_freq[None, :]
    cos = jnp.cos(ang)
    sin = jnp.sin(ang)
    rows = pos.shape[0]
    ones = jnp.ones((rows, HEAD_DIM - ROPE_DIM), F32)
    zeros_h = jnp.zeros((rows, half), F32)
    zeros_r = jnp.zeros((rows, HEAD_DIM - ROPE_DIM), F32)
    cos_h = jnp.concatenate([cos, cos, ones], axis=1)
    sa_h = jnp.concatenate([-sin, zeros_h, zeros_r], axis=1)
    sb_h = jnp.concatenate([zeros_h, sin, zeros_r], axis=1)
    two = lambda t: jnp.concatenate([t, t], axis=1)
    return two(cos_h), two(sa_h), two(sb_h)


def _attn_prompt_kernel(sink_ref, q_ref, kc_ref, kp_ref, vc_ref, vp_ref, z_ref, o_ref):
    n = pl.program_id(1)
    hd = HEAD_DIM
    blk = q_ref.shape[0]
    n_kv = kc_ref.shape[1] // hd
    grp = q_ref.shape[1] // (n_kv * hd)
    qi = lax.broadcasted_iota(jnp.int32, (blk, 2 * blk), 0)
    kj = lax.broadcasted_iota(jnp.int32, (blk, 2 * blk), 1) - blk
    kpos = n * blk + kj
    diff = qi - kj
    valid = (kpos >= LEAD) & (diff >= 0) & (diff <= WINDOW)
    k_all = jnp.concatenate([kp_ref[...], kc_ref[...]], axis=0).astype(BF16)
    v_all = jnp.concatenate([vp_ref[...], vc_ref[...]], axis=0).astype(BF16)

    def scores(h):
        k_h = k_all[:, h * hd:(h + 1) * hd]
        return [_dot_nt(q_ref[:, (h * grp + gi) * hd:(h * grp + gi + 1) * hd], k_h) for gi in range(grp)]

    outs = []
    s_next = scores(0)
    for h in range(n_kv):
        s_cur = s_next
        if h + 1 < n_kv:
            s_next = scores(h + 1)
        v_h = v_all[:, h * hd:(h + 1) * hd]
        gs = range(grp)
        sks = [sink_ref[h * grp + gi] for gi in gs]
        ss = [jnp.where(valid, s_cur[gi], -jnp.inf) for gi in gs]
        ms = [jnp.maximum(jnp.max(ss[gi], axis=-1, keepdims=True), sks[gi]) for gi in gs]
        ps = [jnp.exp(ss[gi] - ms[gi]) for gi in gs]
        dens = [jnp.sum(ps[gi], axis=-1, keepdims=True) + jnp.exp(sks[gi] - ms[gi]) for gi in gs]
        outs += [_dot(ps[gi].astype(BF16), v_h) / dens[gi] for gi in gs]
    att = jnp.concatenate(outs, axis=1)
    o_ref[...] = (att * _silu(z_ref[...])).astype(o_ref.dtype)


def attn_prompt(sinks, q, k, v, z, n_batch):
    m, e = q.shape
    nb = m // (n_batch * BLOCK)
    kw = k.shape[1]
    cur = lambda i, n: (i * nb + n, 0)
    prv = lambda i, n: (i * nb + jnp.maximum(n - 1, 0), 0)
    return pl.pallas_call(
        _attn_prompt_kernel,
        grid=(n_batch, nb),
        in_specs=[pl.BlockSpec(memory_space=pltpu.SMEM),
                  pl.BlockSpec((BLOCK, e), cur),
                  pl.BlockSpec((BLOCK, kw), cur), pl.BlockSpec((BLOCK, kw), prv),
                  pl.BlockSpec((BLOCK, kw), cur), pl.BlockSpec((BLOCK, kw), prv),
                  pl.BlockSpec((BLOCK, e), cur)],
        out_specs=pl.BlockSpec((BLOCK, e), cur),
        out_shape=jax.ShapeDtypeStruct((m, e), BF16),
        compiler_params=_cparams(("arbitrary", "arbitrary")),
        name="attn_prompt",
    )(sinks, q, k, k, v, v, z)


def _attn_sample_kernel(sink_ref, q_ref, kc_ref, vc_ref, kn_ref, vn_ref, z_ref, o_ref, ko_ref, vo_ref):
    hd = HEAD_DIM
    win = kc_ref.shape[1]
    n_kv = kc_ref.shape[2] // hd
    nq = q_ref.shape[1]
    grp = nq // n_kv
    pad = 8
    kc = kc_ref[0]
    vc = vc_ref[0]
    kn = kn_ref[0]
    vn = vn_ref[0]
    first = lax.broadcasted_iota(jnp.int32, (pad, kc.shape[1]), 0) == 0
    k_all = jnp.concatenate([kc, jnp.where(first, kn, 0.0)], axis=0).astype(BF16)
    v_all = jnp.concatenate([vc, jnp.where(first, vn, 0.0)], axis=0).astype(BF16)
    col = lax.broadcasted_iota(jnp.int32, (grp, win + pad), 1)
    valid = (col <= win) & (win - col <= WINDOW)
    q = q_ref[0].astype(BF16)
    row_i = lax.broadcasted_iota(jnp.int32, (grp, 1), 0)
    hs = range(n_kv)
    sks = []
    for h in hs:
        sk = jnp.zeros((grp, 1), F32)
        for gi in range(grp):
            sk = jnp.where(row_i == gi, sink_ref[h * grp + gi], sk)
        sks.append(sk)
    ss = [jnp.where(valid, _dot_nt(q[h * grp:(h + 1) * grp, :], k_all[:, h * hd:(h + 1) * hd]), -jnp.inf)
          for h in hs]
    ms = [jnp.maximum(jnp.max(ss[h], axis=-1, keepdims=True), sks[h]) for h in hs]
    ps = [jnp.exp(ss[h] - ms[h]) for h in hs]
    dens = [jnp.sum(ps[h], axis=-1, keepdims=True) + jnp.exp(sks[h] - ms[h]) for h in hs]
    outs = [_dot(ps[h].astype(BF16), v_all[:, h * hd:(h + 1) * hd]) / dens[h] for h in hs]
    att = jnp.concatenate(outs, axis=0)
    o_ref[0] = (att * _silu(z_ref[0])).astype(o_ref.dtype)
    last = lax.broadcasted_iota(jnp.int32, kc.shape, 0) == win - 1
    ko_ref[0] = jnp.where(last, kn, pltpu.roll(kc, win - 1, axis=0))
    vo_ref[0] = jnp.where(last, vn, pltpu.roll(vc, win - 1, axis=0))


def attn_sample(sinks, q, cache_k, cache_v, k_new, v_new, z):
    m, win, kw = cache_k.shape
    nq = q.shape[1]
    hd = HEAD_DIM
    tok = pl.BlockSpec((1, nq, hd), lambda i: (i, 0, 0))
    cache = pl.BlockSpec((1, win, kw), lambda i: (i, 0, 0))
    new = pl.BlockSpec((1, 1, kw), lambda i: (i, 0, 0))
    return pl.pallas_call(
        _attn_sample_kernel,
        grid=(m,),
        in_specs=[pl.BlockSpec(memory_space=pltpu.SMEM), tok, cache, cache, new, new, tok],
        out_specs=[tok, cache, cache],
        out_shape=[jax.ShapeDtypeStruct((m, nq, hd), BF16),
                   jax.ShapeDtypeStruct(cache_k.shape, F32), jax.ShapeDtypeStruct(cache_v.shape, F32)],
        compiler_params=_cparams(("arbitrary",)),
        name="attn_sample",
    )(sinks, q, cache_k, cache_v, k_new, v_new, z)


def _pad_lora(w_down, w_up):
    r = w_down.shape[1]
    return (jnp.pad(w_down, ((0, 0), (0, LORA_PAD - r))).astype(BF16),
            jnp.pad(w_up, ((0, LORA_PAD - r), (0, 0))).astype(BF16))


def kernel(x_prompt, x_sample, state_wkv, state_shift, cache_k, cache_v, meta_tokens, a_norm, a_mu, a_w_rkvz,
           a_w0, a_w1, a_w2, a_a0, a_a1, a_a2, a_k_k, a_k_a, a_r_k, a_gn_g, a_gn_b, a_w_out, kv_norm, w_kv,
           b_norm, b_w_qz, b_sinks, b_w_o, final_norm):
    nb, seq, d = x_prompt.shape
    db, dseq, _ = x_sample.shape
    assert dseq == 1 and a_norm.shape[0] == 1 and b_norm.shape[0] == 1
    e = a_w_rkvz.shape[3]
    win = cache_k.shape[1]
    p_len = LEAD + N_META + seq
    assert p_len % BLOCK == 0 and (LEAD + N_META) == BLOCK
    m_p = nb * p_len
    kvw = N_KV_HEADS * HEAD_DIM

    w_rkvz = a_w_rkvz[0].astype(BF16)
    w1, w2 = _pad_lora(a_w1[0], a_w2[0])
    a1, a2 = _pad_lora(a_a1[0], a_a2[0])
    w_out = a_w_out[0].astype(BF16)
    w_kv_bf = w_kv.astype(BF16)
    w_q = b_w_qz[0][:, :e].astype(BF16)
    w_z = b_w_qz[0][:, e:].astype(BF16)
    w_o = b_w_o[0].astype(BF16)
    mu = a_mu[0]
    sinks = b_sinks[0]
    gains_b = jnp.stack([kv_norm, b_norm[0]])

    tm = p_len // 8

    hp = jnp.concatenate([jnp.zeros((nb, LEAD, d), F32),
                          jnp.broadcast_to(meta_tokens[None], (nb, N_META, d)), x_prompt], axis=1)
    xm, hw_p, ha_p, x_last = norm_shift_prompt(hp, a_norm[0], mu, w1, a1, tm)
    p_state_shift = x_last.reshape(1, nb, d)
    rkvz = matmul_groups(xm, w_rkvz, tm, F32)
    yg, p_state = wkv_prompt(rkvz, hw_p, ha_p, w2, a_w0[0], a2, a_a0[0], a_k_k[0], a_k_a[0],
                             a_r_k[0].reshape(-1), a_gn_g[0], a_gn_b[0], nb)
    hp, hn_kv, hn_b = matmul_residual_norm(yg, w_out, hp.reshape(m_p, d), gains_b, tm, BF16)

    pos_p = jnp.maximum(jnp.arange(p_len, dtype=jnp.int32) - LEAD, 0)
    tabs_p = tuple(jnp.tile(t, (nb, 1)) for t in rope_tables(pos_p))
    k_p, v_p = matmul_rope(hn_kv, w_kv_bf, tabs_p, tm, kvw, (F32, F32))
    q_p, = matmul_rope(hn_b, w_q, tabs_p, tm, e, (BF16,), scale=Q_SCALE)
    z_p = matmul_groups(hn_b[None], w_z[None], tm, F32)[0]
    att = attn_prompt(sinks, q_p, k_p, v_p, z_p, nb)
    y_pad, = matmul_residual_norm(att, w_o, hp, final_norm[None], tm, F32, emit_h=False)
    y_prompt = y_pad.reshape(nb, p_len, d)[:, LEAD + N_META:]
    p_cache_k = k_p.reshape(nb, p_len, N_KV_HEADS, HEAD_DIM)[:, -win:]
    p_cache_v = v_p.reshape(nb, p_len, N_KV_HEADS, HEAD_DIM)[:, -win:]

    hs = x_sample.reshape(db, d)
    xm_s, hw_s, ha_s, xn_s = norm_shift_sample(hs, state_shift[0], a_norm[0], mu, w1, a1)
    rkvz_s = matmul_groups(xm_s, w_rkvz, db, F32)
    wl_s, al_s = lora_up(hw_s, ha_s, w2, a_w0[0], a2, a_a0[0])
    yg_s, s_state = wkv_sample(rkvz_s, wl_s, al_s, a_k_k[0], a_k_a[0], a_r_k[0].reshape(-1), a_gn_g[0],
                               a_gn_b[0], state_wkv[0])
    hs, hn_kv_s, hn_b_s = matmul_residual_norm(yg_s.reshape(db, e), w_out, hs, gains_b, db, BF16)
    tabs_s = rope_tables(jnp.full((db,), PAST_LEN, jnp.int32))
    k_s, v_s = matmul_rope(hn_kv_s, w_kv_bf, tabs_s, db, kvw, (F32, F32))
    q_s, = matmul_rope(hn_b_s, w_q, tabs_s, db, e, (F32,), scale=Q_SCALE)
    z_s = matmul_groups(hn_b_s[None], w_z[None], db, F32)[0]
    nq = e // HEAD_DIM
    att_s, s_cache_k, s_cache_v = attn_sample(
        sinks, q_s.reshape(db, nq, HEAD_DIM), cache_k.reshape(db, win, kvw), cache_v.reshape(db, win, kvw),
        k_s.reshape(db, 1, kvw), v_s.reshape(db, 1, kvw), z_s.reshape(db, nq, HEAD_DIM))
    y_s, = matmul_residual_norm(att_s.reshape(db, e), w_o, hs, final_norm[None], db, F32, emit_h=False)
    y_sample = y_s.reshape(db, 1, d)

    return (y_prompt, y_sample, p_state[None], p_state_shift,
            p_cache_k, p_cache_v,
            s_state[None], xn_s[None],
            s_cache_k.reshape(cache_k.shape), s_cache_v.reshape(cache_v.shape))
```

```python
import functools
import math

import jax
import jax.numpy as jnp
from jax import lax
from jax.experimental import pallas as pl
from jax.experimental.pallas import tpu as pltpu

F32 = jnp.float32
BF16 = jnp.bfloat16

HEAD_DIM = 64
N_KV_HEADS = 8
WINDOW = 128
BLOCK = 128
ROPE_DIM = HEAD_DIM // 4
ROPE_THETA = 500000.0
N_META = 16
PAST_LEN = 16384
RMS_EPS = 1e-6
GN_EPS = 64e-5
LEAD = (-N_META) % BLOCK
CHUNK = 64
WKV_ROWS = 128
HEADS_PER_STREAM = 32
HEADS_PER_STEP = 32
LORA_PAD = 128
MXU_TILE = 256
ROPE_SLAB = 512
VMEM_LIMIT = 48 * 1024 * 1024
DECAY_SCALE = math.exp(-0.5)
SOFTMAX_BATCH = 4
Q_SCALE = HEAD_DIM ** -0.5


def _cparams(sem):
    return pltpu.CompilerParams(dimension_semantics=sem, vmem_limit_bytes=VMEM_LIMIT)


def _sigmoid(x):
    return 1.0 / (1.0 + jnp.exp(-x))


def _silu(x):
    return x * _sigmoid(x)


def _dot(a, b):
    return jnp.dot(a, b, preferred_element_type=F32)


def _dot_nt(a, b):
    return lax.dot_general(a, b, (((1,), (1,)), ((), ())), preferred_element_type=F32)


def _dot_tn(a, b):
    return lax.dot_general(a, b, (((0,), (0,)), ((), ())), preferred_element_type=F32)


def _split_hi_lo(x):
    hi = x.astype(BF16)
    lo = (x - hi.astype(F32)).astype(BF16)
    return hi, lo


def _mixes(xn, prev, mu_ref, w1_ref, a1_ref, xm_ref, hw_ref, ha_ref):
    xx = prev - xn
    n_proj = xm_ref.shape[0]
    for p in range(n_proj):
        xm_ref[p] = (xn + xx * mu_ref[p:p + 1, :]).astype(xm_ref.dtype)
    xw = (xn + xx * mu_ref[n_proj:n_proj + 1, :]).astype(BF16)
    xa = (xn + xx * mu_ref[n_proj + 1:n_proj + 2, :]).astype(BF16)
    hw_ref[...] = jnp.tanh(_dot(xw, w1_ref[...])).astype(hw_ref.dtype)
    ha_ref[...] = _dot(xa, a1_ref[...]).astype(ha_ref.dtype)


def _norm_shift_kernel(x_ref, g_ref, mu_ref, w1_ref, a1_ref, xm_ref, hw_ref, ha_ref, last_ref, carry_ref):
    @pl.when(pl.program_id(1) == 0)
    def _():
        carry_ref[...] = jnp.zeros_like(carry_ref)

    x = x_ref[0]
    tm = x.shape[0]
    xn = x * lax.rsqrt(jnp.mean(x * x, axis=-1, keepdims=True) + RMS_EPS) * g_ref[...]
    rolled = pltpu.roll(xn, 1, axis=0)
    row = lax.broadcasted_iota(jnp.int32, xn.shape, 0)
    prev = jnp.where(row == 0, carry_ref[0:1, :], rolled)
    _mixes(xn, prev, mu_ref, w1_ref, a1_ref, xm_ref, hw_ref, ha_ref)
    carry_ref[0:1, :] = xn[tm - 1:tm, :]
    last_ref[0] = xn[tm - 1:tm, :]


def norm_shift_prompt(x, g, mu, w1, a1, tm):
    b, p, d = x.shape
    n_mix = mu.shape[0]
    n_proj = n_mix - 2
    lr = w1.shape[1]
    nt = p // tm
    const = lambda shape: pl.BlockSpec(shape, lambda i, t: (0,) * len(shape))
    hid = pl.BlockSpec((tm, lr), lambda i, t: (i * nt + t, 0))
    return pl.pallas_call(
        _norm_shift_kernel,
        grid=(b, nt),
        in_specs=[pl.BlockSpec((1, tm, d), lambda i, t: (i, t, 0)),
                  const((1, d)), const((n_mix, d)), const((d, lr)), const((d, lr))],
        out_specs=[pl.BlockSpec((n_proj, tm, d), lambda i, t: (0, i * nt + t, 0)), hid, hid,
                   pl.BlockSpec((1, 1, d), lambda i, t: (i, 0, 0))],
        out_shape=[jax.ShapeDtypeStruct((n_proj, b * p, d), BF16),
                   jax.ShapeDtypeStruct((b * p, lr), BF16), jax.ShapeDtypeStruct((b * p, lr), BF16),
                   jax.ShapeDtypeStruct((b, 1, d), F32)],
        scratch_shapes=[pltpu.VMEM((8, d), F32)],
        compiler_params=_cparams(("arbitrary", "arbitrary")),
        name="norm_shift_prompt",
    )(x, g.reshape(1, d), mu, w1, a1)


def _norm_shift_sample_kernel(x_ref, prev_ref, g_ref, mu_ref, w1_ref, a1_ref, xm_ref, hw_ref, ha_ref, xn_ref):
    x = x_ref[...]
    xn = x * lax.rsqrt(jnp.mean(x * x, axis=-1, keepdims=True) + RMS_EPS) * g_ref[...]
    xn_ref[...] = xn
    _mixes(xn, prev_ref[...], mu_ref, w1_ref, a1_ref, xm_ref, hw_ref, ha_ref)


def norm_shift_sample(x, prev, g, mu, w1, a1):
    m, d = x.shape
    lr = w1.shape[1]
    return pl.pallas_call(
        _norm_shift_sample_kernel,
        out_shape=[jax.ShapeDtypeStruct((mu.shape[0] - 2, m, d), BF16),
                   jax.ShapeDtypeStruct((m, lr), BF16), jax.ShapeDtypeStruct((m, lr), BF16),
                   jax.ShapeDtypeStruct((m, d), F32)],
        name="norm_shift_sample",
    )(x, prev, g.reshape(1, d), mu, w1, a1)


def _rope(y, cos, sin_a, sin_b):
    half = ROPE_DIM // 2
    step = ROPE_SLAB
    rep = step // cos.shape[1]
    tile = lambda t: jnp.concatenate([t] * rep, axis=1)
    cos_t, sa_t, sb_t = tile(cos), tile(sin_a), tile(sin_b)
    outs = []
    for j in range(y.shape[1] // step):
        ys = y[:, j * step:(j + 1) * step]
        outs.append(ys * cos_t + pltpu.roll(ys, step - half, axis=1) * sa_t + pltpu.roll(ys, half, axis=1) * sb_t)
    return jnp.concatenate(outs, axis=1) if len(outs) > 1 else outs[0]


def _mm_group_kernel(x_ref, w_ref, o_ref):
    o_ref[0] = _dot(x_ref[0], w_ref[0]).astype(o_ref.dtype)


def matmul_groups(x, w, tm, out_dtype):
    g, _, n = w.shape
    _, m, kdim = x.shape
    return pl.pallas_call(
        _mm_group_kernel,
        grid=(g, m // tm),
        in_specs=[pl.BlockSpec((1, tm, kdim), lambda q, i: (q, i, 0)),
                  pl.BlockSpec((1, kdim, n), lambda q, i: (q, 0, 0))],
        out_specs=pl.BlockSpec((1, tm, n), lambda q, i: (q, i, 0)),
        out_shape=jax.ShapeDtypeStruct((g, m, n), out_dtype),
        compiler_params=_cparams(("arbitrary", "arbitrary")),
        name="matmul_groups",
    )(x, w)


def _mm_rope_kernel(x_ref, w_ref, cos_ref, sa_ref, sb_ref, *o_refs, n_rope, scale):
    y = _dot(x_ref[...], w_ref[...])
    rot = _rope(y[:, :n_rope], cos_ref[...], sa_ref[...], sb_ref[...])
    if scale != 1.0:
        rot = rot * scale
    o_refs[0][...] = rot.astype(o_refs[0].dtype)
    if len(o_refs) > 1:
        o_refs[1][...] = y[:, n_rope:].astype(o_refs[1].dtype)


def matmul_rope(x, w, tables, tm, n_rope, out_dtypes, scale=1.0):
    m, kdim = x.shape
    n = w.shape[1]
    lanes = tables[0].shape[1]
    widths = [n_rope] + ([n - n_rope] if n > n_rope else [])
    tab = pl.BlockSpec((tm, lanes), lambda i: (i, 0))
    outs = pl.pallas_call(
        functools.partial(_mm_rope_kernel, n_rope=n_rope, scale=scale),
        grid=(m // tm,),
        in_specs=[pl.BlockSpec((tm, kdim), lambda i: (i, 0)),
                  pl.BlockSpec((kdim, n), lambda i: (0, 0), pipeline_mode=pl.Buffered(1)),
                  tab, tab, tab],
        out_specs=[pl.BlockSpec((tm, wd), lambda i: (i, 0)) for wd in widths],
        out_shape=[jax.ShapeDtypeStruct((m, wd), dt) for wd, dt in zip(widths, out_dtypes)],
        compiler_params=_cparams(("arbitrary",)),
        name="matmul_rope",
    )(x, w, *tables)
    return outs


def _mm_res_norm_kernel(x_ref, w_ref, res_ref, g_ref, *out_refs, emit_h):
    h = res_ref[...] + _dot(x_ref[...], w_ref[...])
    hn_refs = out_refs
    if emit_h:
        out_refs[0][...] = h
        hn_refs = out_refs[1:]
    inv = lax.rsqrt(jnp.mean(h * h, axis=-1, keepdims=True) + RMS_EPS)
    for j, hn_ref in enumerate(hn_refs):
        hn_ref[...] = (h * inv * g_ref[j:j + 1, :]).astype(hn_ref.dtype)


def matmul_residual_norm(x, w, res, gains, tm, norm_dtype, emit_h=True):
    m, kdim = x.shape
    n = w.shape[1]
    ng = gains.shape[0]
    row = lambda width: pl.BlockSpec((tm, width), lambda i: (i, 0))
    return pl.pallas_call(
        functools.partial(_mm_res_norm_kernel, emit_h=emit_h),
        grid=(m // tm,),
        in_specs=[row(kdim),
                  pl.BlockSpec((kdim, n), lambda i: (0, 0), pipeline_mode=pl.Buffered(1)),
                  row(n),
                  pl.BlockSpec((ng, n), lambda i: (0, 0))],
        out_specs=[row(n)] * (int(emit_h) + ng),
        out_shape=[jax.ShapeDtypeStruct((m, n), F32)] * int(emit_h) + [jax.ShapeDtypeStruct((m, n), norm_dtype)] * ng,
        compiler_params=_cparams(("arbitrary",)),
        name="matmul_residual_norm",
    )(x, w, res, gains)


def _lora_up_kernel(hw_ref, ha_ref, w2_ref, w0_ref, a2_ref, a0_ref, wl_ref, al_ref):
    wl_ref[...] = w0_ref[...] + _dot(hw_ref[...], w2_ref[...])
    al_ref[...] = a0_ref[...] + _dot(ha_ref[...], a2_ref[...])


def lora_up(hw, ha, w2, w0, a2, a0):
    m = hw.shape[0]
    e = w2.shape[1]
    return pl.pallas_call(
        _lora_up_kernel,
        out_shape=[jax.ShapeDtypeStruct((m, e), F32), jax.ShapeDtypeStruct((m, e), F32)],
        name="lora_up",
    )(hw, ha, w2, w0.reshape(1, e), a2, a0.reshape(1, e))


def _seg_sum(x, ones_bd):
    hi = x.astype(BF16)
    outs = []
    for c in range(x.shape[1] // MXU_TILE):
        sl = slice(c * MXU_TILE, (c + 1) * MXU_TILE)
        outs.append(_dot(hi[:, sl], ones_bd))
    return jnp.concatenate(outs, axis=1) if len(outs) > 1 else outs[0]


def _wkv_prompt_kernel(r_ref, k_ref, v_ref, z_ref, hw_ref, ha_ref, w2_ref, w0_ref, a2_ref, a0_ref,
                       kk_ref, ka_ref, rk_ref, gg_ref, gb_ref, yg_ref, sout_ref, s_ref):
    t_idx = pl.program_id(2)
    c = CHUNK
    hd = HEAD_DIM

    @pl.when(t_idx == 0)
    def _():
        s_ref[...] = jnp.zeros_like(s_ref)

    tb = r_ref.shape[1]
    nh = HEADS_PER_STREAM
    hw = nh * hd
    nc = tb // c

    li = lax.broadcasted_iota(jnp.int32, (MXU_TILE, MXU_TILE), 0) // hd
    lj = lax.broadcasted_iota(jnp.int32, (MXU_TILE, MXU_TILE), 1) // hd
    ones_bd = jnp.where(li == lj, 1.0, 0.0).astype(BF16)
    bi_ = lax.broadcasted_iota(jnp.int32, (tb, tb), 0)
    bj_ = lax.broadcasted_iota(jnp.int32, (tb, tb), 1)
    tri_incl = jnp.where((bj_ <= bi_) & (bj_ // c == bi_ // c), 1.0, 0.0).astype(BF16)
    ti = lax.broadcasted_iota(jnp.int32, (c, c), 0)
    tj = lax.broadcasted_iota(jnp.int32, (c, c), 1)
    eye = jnp.where(ti == tj, 1.0, 0.0).astype(F32)
    ai = lax.broadcasted_iota(jnp.int32, (c, 2 * c), 0)
    aj = lax.broadcasted_iota(jnp.int32, (c, 2 * c), 1)
    aj_mod = jnp.where(aj >= c, aj - c, aj)
    mask_top_k = (aj >= c) & (aj_mod < ai)
    mask_bot = aj_mod <= ai

    for st in range(r_ref.shape[2] // hw):
        _wkv_stream(st, hw, nc, r_ref, k_ref, v_ref, z_ref, hw_ref, ha_ref, w2_ref, w0_ref, a2_ref, a0_ref,
                    kk_ref, ka_ref, rk_ref, gg_ref, gb_ref, yg_ref, s_ref,
                    ones_bd, tri_incl, eye, ti, tj, mask_top_k, mask_bot)

    @pl.when(t_idx == pl.num_programs(2) - 1)
    def _():
        sout_ref[0] = s_ref[...]


def _wkv_stream(st, hw, nc, r_ref, k_ref, v_ref, z_ref, hw_ref, ha_ref, w2_ref, w0_ref, a2_ref, a0_ref,
                kk_ref, ka_ref, rk_ref, gg_ref, gb_ref, yg_ref, s_ref,
                ones_bd, tri_incl, eye, ti, tj, mask_top_k, mask_bot):
    c = CHUNK
    hd = HEAD_DIM
    nh = hw // hd
    ls = slice(st * hw, (st + 1) * hw)
    h0 = st * nh
    r = r_ref[0, :, ls]
    k = k_ref[0, :, ls]
    v = v_ref[0, :, ls]
    wl = w0_ref[:, ls] + _dot(hw_ref[...], w2_ref[:, ls])
    al = a0_ref[:, ls] + _dot(ha_ref[...], a2_ref[:, ls])
    a = _sigmoid(al)
    lw = -DECAY_SCALE * _sigmoid(wl)
    kk = k * kk_ref[:, ls]
    n2 = _seg_sum(kk * kk, ones_bd)
    kk = kk / jnp.maximum(jnp.sqrt(n2), 1e-12)
    k2 = k * (1.0 + (a - 1.0) * ka_ref[:, ls])
    bb = kk * a

    lw_hi, lw_lo = _split_hi_lo(lw)
    g = _dot(tri_incl, lw_hi) + _dot(tri_incl, lw_lo)
    mid = lambda ci: g[ci * c + c // 2 - 1:ci * c + c // 2, :]
    gm = jnp.concatenate([jnp.broadcast_to(mid(ci), (c, hw)) for ci in range(nc)], axis=0)
    e_a = jnp.exp(g - gm)
    e_prev = jnp.exp(g - lw - gm)
    e_inv = jnp.exp(gm - g)
    e1 = [jnp.exp(mid(ci)) for ci in range(nc)]
    e2 = [jnp.exp(g[ci * c + c - 1:ci * c + c, :] - mid(ci)) for ci in range(nc)]

    kkd = (kk * e_prev).astype(BF16)
    rd = (r * e_a).astype(BF16)
    bi = (bb * e_inv).astype(BF16)
    ki = (k2 * e_inv).astype(BF16)
    v_bf = v.astype(BF16)
    zeros_cv = jnp.zeros((c, hd), BF16)

    pairs = [(ci, h) for ci in range(nc) for h in range(nh)]
    rows = lambda ci: slice(ci * c, (ci + 1) * c)
    cols = lambda h: slice(h * hd, (h + 1) * hd)
    xs = {(ci, h): jnp.concatenate([kkd[rows(ci), cols(h)], rd[rows(ci), cols(h)]], axis=0) for ci, h in pairs}
    r1s = {(ci, h): jnp.concatenate([bi[rows(ci), cols(h)], ki[rows(ci), cols(h)]], axis=0) for ci, h in pairs}
    vs = {(ci, h): v_bf[rows(ci), cols(h)] for ci, h in pairs}
    a_mats = {p: _dot_nt(xs[p], r1s[p]) for p in pairs}
    lk_vs = {p: _dot(jnp.where(mask_top_k, a_mats[p][:c, :], 0.0).astype(BF16),
                     jnp.concatenate([zeros_cv, vs[p]], axis=0)) for p in pairs}
    lps = {p: jnp.where(tj < ti, a_mats[p][:c, :c], 0.0) for p in pairs}
    ts = {p: eye - lps[p] for p in pairs}
    for _ in range(int(math.log2(c)) - 1):
        lpb = {p: lps[p].astype(BF16) for p in pairs}
        lps = {p: _dot(lpb[p], lpb[p]) for p in pairs}
        ts = {p: _dot(ts[p].astype(BF16), (eye + lps[p]).astype(BF16)) for p in pairs}
    a_bots = {p: jnp.where(mask_bot, a_mats[p][c:, :], 0.0).astype(BF16) for p in pairs}
    t_bf = {p: ts[p].astype(BF16) for p in pairs}

    state = [s_ref[h0 + h] for h in range(nh)]
    y_rows = []
    for ci in range(nc):
        hs = range(nh)
        sms = [state[h] * e1[ci][:, cols(h)] for h in hs]
        p_mats = [_dot_nt(xs[ci, h], sms[h].astype(BF16)) for h in hs]
        us = [-_dot(t_bf[ci, h], (p_mats[h][:c, :] + lk_vs[ci, h]).astype(BF16)) for h in hs]
        uvs = [jnp.concatenate([us[h].astype(BF16), vs[ci, h]], axis=0) for h in hs]
        ys = [p_mats[h][c:, :] + _dot(a_bots[ci, h], uvs[h]) for h in hs]
        state = [(sms[h] + _dot_tn(uvs[h], r1s[ci, h])) * e2[ci][:, cols(h)] for h in hs]
        y_rows.append(jnp.concatenate(ys, axis=1))
    for h in range(nh):
        s_ref[h0 + h] = state[h]
    y = jnp.concatenate(y_rows, axis=0) if nc > 1 else y_rows[0]

    inv_n = 1.0 / hd
    mean = _seg_sum(y, ones_bd) * inv_n
    yc = y - mean
    var = _seg_sum(yc * yc, ones_bd) * inv_n
    yn = yc * lax.rsqrt(var + GN_EPS) * gg_ref[:, ls] + gb_ref[:, ls]
    bonus = _seg_sum(r * k2 * rk_ref[:, ls], ones_bd)
    yg_ref[:, ls] = ((yn + bonus * v) * _silu(z_ref[0, :, ls])).astype(BF16)


def wkv_prompt(rkvz, hw_act, ha_act, w2, w0, a2, a0, k_k, k_a, r_k, gn_g, gn_b, n_batch):
    _, m, e = rkvz.shape
    p = m // n_batch
    tb = WKV_ROWS
    nt = p // tb
    hw = HEADS_PER_STEP * HEAD_DIM
    nh = e // HEAD_DIM
    row = lambda i, g, t: (i * nt + t, g)
    proj = lambda q: pl.BlockSpec((1, tb, hw), lambda i, g, t: (q, i * nt + t, g))
    par = pl.BlockSpec((1, hw), lambda i, g, t: (0, g))
    lr = w2.shape[0]
    hid = pl.BlockSpec((tb, lr), lambda i, g, t: (i * nt + t, 0))
    up = pl.BlockSpec((lr, hw), lambda i, g, t: (0, g))
    return pl.pallas_call(
        _wkv_prompt_kernel,
        grid=(n_batch, e // hw, nt),
        in_specs=[proj(0), proj(1), proj(2), proj(3), hid, hid, up, par, up, par,
                  par, par, par, par, par],
        out_specs=[pl.BlockSpec((tb, hw), row),
                   pl.BlockSpec((1, HEADS_PER_STEP, HEAD_DIM, HEAD_DIM), lambda i, g, t: (i, g, 0, 0))],
        out_shape=[jax.ShapeDtypeStruct((m, e), BF16),
                   jax.ShapeDtypeStruct((n_batch, nh, HEAD_DIM, HEAD_DIM), F32)],
        scratch_shapes=[pltpu.VMEM((HEADS_PER_STEP, HEAD_DIM, HEAD_DIM), F32)],
        compiler_params=_cparams(("arbitrary", "arbitrary", "arbitrary")),
        name="wkv_prompt",
    )(rkvz, rkvz, rkvz, rkvz, hw_act, ha_act, w2, w0.reshape(1, e), a2, a0.reshape(1, e),
      k_k.reshape(1, e), k_a.reshape(1, e), r_k.reshape(1, e), gn_g.reshape(1, e), gn_b.reshape(1, e))


def _wkv_sample_kernel(r_ref, k_ref, v_ref, z_ref, wl_ref, al_ref, kk_ref, ka_ref, rk_ref, gg_ref, gb_ref,
                       s_ref, yg_ref, sout_ref, y_scr):
    hd = HEAD_DIM
    r = r_ref[0]
    k = k_ref[0]
    v = v_ref[0]
    a = _sigmoid(al_ref[0])
    d = jnp.exp(-DECAY_SCALE * _sigmoid(wl_ref[0]))
    kk = k * kk_ref[...]
    kk = kk / jnp.maximum(jnp.sqrt(jnp.sum(kk * kk, axis=-1, keepdims=True)), 1e-12)
    k2 = k * (1.0 + (a - 1.0) * ka_ref[...])
    bb = kk * a
    nh = r.shape[0]
    ii = lax.broadcasted_iota(jnp.int32, (hd, hd), 0)
    jj = lax.broadcasted_iota(jnp.int32, (hd, hd), 1)
    eye = ii == jj

    row = lambda x, h: x[h:h + 1, :]
    group = 8
    for h0 in range(0, nh, group):
        hs = range(h0, h0 + group)
        s = {h: s_ref[0, h] for h in hs}
        sa = {h: jnp.sum(s[h] * row(kk, h), axis=-1, keepdims=True) for h in hs}
        v_col = {h: jnp.sum(jnp.where(eye, row(v, h), 0.0), axis=-1, keepdims=True) for h in hs}
        s_new = {h: s[h] * row(d, h) - sa[h] * row(bb, h) + v_col[h] * row(k2, h) for h in hs}
        y_col = {h: jnp.sum(s_new[h] * row(r, h), axis=-1, keepdims=True) for h in hs}
        for h in hs:
            sout_ref[0, h] = s_new[h]
            y_scr[h:h + 1, :] = jnp.sum(jnp.where(eye, y_col[h], 0.0), axis=0, keepdims=True)

    y = y_scr[...]
    mean = jnp.mean(y, axis=-1, keepdims=True)
    yc = y - mean
    var = jnp.mean(yc * yc, axis=-1, keepdims=True)
    yn = yc * lax.rsqrt(var + GN_EPS) * gg_ref[...] + gb_ref[...]
    bonus = jnp.sum(r * k2 * rk_ref[...], axis=-1, keepdims=True)
    yg_ref[0] = ((yn + bonus * v) * _silu(z_ref[0])).astype(BF16)


def wkv_sample(rkvz, wl, al, k_k, k_a, r_k, gn_g, gn_b, state):
    _, m, e = rkvz.shape
    nh = e // HEAD_DIM
    hd = HEAD_DIM
    rkvz4 = rkvz.reshape(4, m, nh, hd)
    proj = lambda q: pl.BlockSpec((None, 1, nh, hd), lambda i: (q, i, 0, 0))
    tok = pl.BlockSpec((1, nh, hd), lambda i: (i, 0, 0))
    par = pl.BlockSpec((nh, hd), lambda i: (0, 0))
    st = pl.BlockSpec((1, nh, hd, hd), lambda i: (i, 0, 0, 0))
    as_heads = lambda x: x.reshape(nh, hd)
    return pl.pallas_call(
        _wkv_sample_kernel,
        grid=(m,),
        in_specs=[proj(0), proj(1), proj(2), proj(3), tok, tok, par, par, par, par, par, st],
        out_specs=[tok, st],
        out_shape=[jax.ShapeDtypeStruct((m, nh, hd), BF16), jax.ShapeDtypeStruct(state.shape, F32)],
        scratch_shapes=[pltpu.VMEM((nh, hd), F32)],
        compiler_params=_cparams(("arbitrary",)),
        name="wkv_sample",
    )(rkvz4, rkvz4, rkvz4, rkvz4, wl.reshape(m, nh, hd), al.reshape(m, nh, hd),
      as_heads(k_k), as_heads(k_a), as_heads(r_k), as_heads(gn_g), as_heads(gn_b), state)


def rope_tables(pos):
    half = ROPE_DIM // 2
    inv_freq = ROPE_THETA ** (-jnp.arange(half, dtype=F32) * 2.0 / ROPE_DIM)
    ang = pos.astype(F32)[:, None] * inv_freq[None, :]
    cos = jnp.cos(ang)
    sin = jnp.sin(ang)
    rows = pos.shape[0]
    ones = jnp.ones((rows, HEAD_DIM - ROPE_DIM), F32)
    zeros_h = jnp.zeros((rows, half), F32)
    zeros_r = jnp.zeros((rows, HEAD_DIM - ROPE_DIM), F32)
    cos_h = jnp.concatenate([cos, cos, ones], axis=1)
    sa_h = jnp.concatenate([-sin, zeros_h, zeros_r], axis=1)
    sb_h = jnp.concatenate([zeros_h, sin, zeros_r], axis=1)
    two = lambda t: jnp.concatenate([t, t], axis=1)
    return two(cos_h), two(sa_h), two(sb_h)


def _attn_prompt_kernel(sink_ref, q_ref, kc_ref, kp_ref, vc_ref, vp_ref, z_ref, o_ref):
    n = pl.program_id(1)
    hd = HEAD_DIM
    blk = q_ref.shape[0]
    n_kv = kc_ref.shape[1] // hd
    grp = q_ref.shape[1] // (n_kv * hd)
    qi = lax.broadcasted_iota(jnp.int32, (blk, 2 * blk), 0)
    kj = lax.broadcasted_iota(jnp.int32, (blk, 2 * blk), 1) - blk
    kpos = n * blk + kj
    diff = qi - kj
    valid = (kpos >= LEAD) & (diff >= 0) & (diff <= WINDOW)
    k_all = jnp.concatenate([kp_ref[...], kc_ref[...]], axis=0).astype(BF16)
    v_all = jnp.concatenate([vp_ref[...], vc_ref[...]], axis=0).astype(BF16)

    def scores(h):
        k_h = k_all[:, h * hd:(h + 1) * hd]
        return [_dot_nt(q_ref[:, (h * grp + gi) * hd:(h * grp + gi + 1) * hd], k_h) for gi in range(grp)]

    outs = []
    s_next = scores(0)
    for h in range(n_kv):
        s_cur = s_next
        if h + 1 < n_kv:
            s_next = scores(h + 1)
        v_h = v_all[:, h * hd:(h + 1) * hd]
        for g0 in range(0, grp, SOFTMAX_BATCH):
            gs = range(g0, min(g0 + SOFTMAX_BATCH, grp))
            sks = {gi: sink_ref[h * grp + gi] for gi in gs}
            ss = {gi: jnp.where(valid, s_cur[gi], -jnp.inf) for gi in gs}
            ms = {gi: jnp.maximum(jnp.max(ss[gi], axis=-1, keepdims=True), sks[gi]) for gi in gs}
            ps = {gi: jnp.exp(ss[gi] - ms[gi]) for gi in gs}
            dens = {gi: jnp.sum(ps[gi], axis=-1, keepdims=True) + jnp.exp(sks[gi] - ms[gi]) for gi in gs}
            outs += [_dot(ps[gi].astype(BF16), v_h) / dens[gi] for gi in gs]
    att = jnp.concatenate(outs, axis=1)
    o_ref[...] = (att * _silu(z_ref[...])).astype(o_ref.dtype)


def attn_prompt(sinks, q, k, v, z, n_batch):
    m, e = q.shape
    nb = m // (n_batch * BLOCK)
    kw = k.shape[1]
    cur = lambda i, n: (i * nb + n, 0)
    prv = lambda i, n: (i * nb + jnp.maximum(n - 1, 0), 0)
    return pl.pallas_call(
        _attn_prompt_kernel,
        grid=(n_batch, nb),
        in_specs=[pl.BlockSpec(memory_space=pltpu.SMEM),
                  pl.BlockSpec((BLOCK, e), cur),
                  pl.BlockSpec((BLOCK, kw), cur), pl.BlockSpec((BLOCK, kw), prv),
                  pl.BlockSpec((BLOCK, kw), cur), pl.BlockSpec((BLOCK, kw), prv),
                  pl.BlockSpec((BLOCK, e), cur)],
        out_specs=pl.BlockSpec((BLOCK, e), cur),
        out_shape=jax.ShapeDtypeStruct((m, e), BF16),
        compiler_params=_cparams(("arbitrary", "arbitrary")),
        name="attn_prompt",
    )(sinks, q, k, k, v, v, z)


def _attn_sample_kernel(sink_ref, q_ref, kc_ref, vc_ref, kn_ref, vn_ref, z_ref, o_ref, ko_ref, vo_ref):
    hd = HEAD_DIM
    win = kc_ref.shape[1]
    n_kv = kc_ref.shape[2] // hd
    nq = q_ref.shape[1]
    grp = nq // n_kv
    pad = 8
    kc = kc_ref[0]
    vc = vc_ref[0]
    kn = kn_ref[0]
    vn = vn_ref[0]
    first = lax.broadcasted_iota(jnp.int32, (pad, kc.shape[1]), 0) == 0
    k_all = jnp.concatenate([kc, jnp.where(first, kn, 0.0)], axis=0).astype(BF16)
    v_all = jnp.concatenate([vc, jnp.where(first, vn, 0.0)], axis=0).astype(BF16)
    col = lax.broadcasted_iota(jnp.int32, (grp, win + pad), 1)
    valid = (col <= win) & (win - col <= WINDOW)
    q = q_ref[0].astype(BF16)
    row_i = lax.broadcasted_iota(jnp.int32, (grp, 1), 0)
    hs = range(n_kv)
    sks = []
    for h in hs:
        sk = jnp.zeros((grp, 1), F32)
        for gi in range(grp):
            sk = jnp.where(row_i == gi, sink_ref[h * grp + gi], sk)
        sks.append(sk)
    ss = [jnp.where(valid, _dot_nt(q[h * grp:(h + 1) * grp, :], k_all[:, h * hd:(h + 1) * hd]), -jnp.inf)
          for h in hs]
    ms = [jnp.maximum(jnp.max(ss[h], axis=-1, keepdims=True), sks[h]) for h in hs]
    ps = [jnp.exp(ss[h] - ms[h]) for h in hs]
    dens = [jnp.sum(ps[h], axis=-1, keepdims=True) + jnp.exp(sks[h] - ms[h]) for h in hs]
    outs = [_dot(ps[h].astype(BF16), v_all[:, h * hd:(h + 1) * hd]) / dens[h] for h in hs]
    att = jnp.concatenate(outs, axis=0)
    o_ref[0] = (att * _silu(z_ref[0])).astype(o_ref.dtype)
    last = lax.broadcasted_iota(jnp.int32, kc.shape, 0) == win - 1
    ko_ref[0] = jnp.where(last, kn, pltpu.roll(kc, win - 1, axis=0))
    vo_ref[0] = jnp.where(last, vn, pltpu.roll(vc, win - 1, axis=0))


def attn_sample(sinks, q, cache_k, cache_v, k_new, v_new, z):
    m, win, kw = cache_k.shape
    nq = q.shape[1]
    hd = HEAD_DIM
    tok = pl.BlockSpec((1, nq, hd), lambda i: (i, 0, 0))
    cache = pl.BlockSpec((1, win, kw), lambda i: (i, 0, 0))
    new = pl.BlockSpec((1, 1, kw), lambda i: (i, 0, 0))
    return pl.pallas_call(
        _attn_sample_kernel,
        grid=(m,),
        in_specs=[pl.BlockSpec(memory_space=pltpu.SMEM), tok, cache, cache, new, new, tok],
        out_specs=[tok, cache, cache],
        out_shape=[jax.ShapeDtypeStruct((m, nq, hd), BF16),
                   jax.ShapeDtypeStruct(cache_k.shape, F32), jax.ShapeDtypeStruct(cache_v.shape, F32)],
        compiler_params=_cparams(("arbitrary",)),
        name="attn_sample",
    )(sinks, q, cache_k, cache_v, k_new, v_new, z)


def _pad_lora(w_down, w_up):
    r = w_down.shape[1]
    return (jnp.pad(w_down, ((0, 0), (0, LORA_PAD - r))).astype(BF16),
            jnp.pad(w_up, ((0, LORA_PAD - r), (0, 0))).astype(BF16))


def kernel(x_prompt, x_sample, state_wkv, state_shift, cache_k, cache_v, meta_tokens, a_norm, a_mu, a_w_rkvz,
           a_w0, a_w1, a_w2, a_a0, a_a1, a_a2, a_k_k, a_k_a, a_r_k, a_gn_g, a_gn_b, a_w_out, kv_norm, w_kv,
           b_norm, b_w_qz, b_sinks, b_w_o, final_norm):
    nb, seq, d = x_prompt.shape
    db, dseq, _ = x_sample.shape
    assert dseq == 1 and a_norm.shape[0] == 1 and b_norm.shape[0] == 1
    e = a_w_rkvz.shape[3]
    win = cache_k.shape[1]
    p_len = LEAD + N_META + seq
    assert p_len % BLOCK == 0 and (LEAD + N_META) == BLOCK
    m_p = nb * p_len
    kvw = N_KV_HEADS * HEAD_DIM

    w_rkvz = a_w_rkvz[0].astype(BF16)
    w1, w2 = _pad_lora(a_w1[0], a_w2[0])
    a1, a2 = _pad_lora(a_a1[0], a_a2[0])
    w_out = a_w_out[0].astype(BF16)
    w_kv_bf = w_kv.astype(BF16)
    w_q = b_w_qz[0][:, :e].astype(BF16)
    w_z = b_w_qz[0][:, e:].astype(BF16)
    w_o = b_w_o[0].astype(BF16)
    mu = a_mu[0]
    sinks = b_sinks[0]
    gains_b = jnp.stack([kv_norm, b_norm[0]])

    tm = p_len // 8

    hp = jnp.concatenate([jnp.zeros((nb, LEAD, d), F32),
                          jnp.broadcast_to(meta_tokens[None], (nb, N_META, d)), x_prompt], axis=1)
    xm, hw_p, ha_p, x_last = norm_shift_prompt(hp, a_norm[0], mu, w1, a1, tm)
    p_state_shift = x_last.reshape(1, nb, d)
    rkvz = matmul_groups(xm, w_rkvz, tm, F32)
    yg, p_state = wkv_prompt(rkvz, hw_p, ha_p, w2, a_w0[0], a2, a_a0[0], a_k_k[0], a_k_a[0],
                             a_r_k[0].reshape(-1), a_gn_g[0], a_gn_b[0], nb)
    hp, hn_kv, hn_b = matmul_residual_norm(yg, w_out, hp.reshape(m_p, d), gains_b, tm, BF16)

    pos_p = jnp.maximum(jnp.arange(p_len, dtype=jnp.int32) - LEAD, 0)
    tabs_p = tuple(jnp.tile(t, (nb, 1)) for t in rope_tables(pos_p))
    k_p, v_p = matmul_rope(hn_kv, w_kv_bf, tabs_p, tm, kvw, (F32, F32))
    q_p, = matmul_rope(hn_b, w_q, tabs_p, tm, e, (BF16,), scale=Q_SCALE)
    z_p = matmul_groups(hn_b[None], w_z[None], tm, F32)[0]
    att = attn_prompt(sinks, q_p, k_p, v_p, z_p, nb)
    y_pad, = matmul_residual_norm(att, w_o, hp, final_norm[None], tm, F32, emit_h=False)
    y_prompt = y_pad.reshape(nb, p_len, d)[:, LEAD + N_META:]
    p_cache_k = k_p.reshape(nb, p_len, N_KV_HEADS, HEAD_DIM)[:, -win:]
    p_cache_v = v_p.reshape(nb, p_len, N_KV_HEADS, HEAD_DIM)[:, -win:]

    hs = x_sample.reshape(db, d)
    xm_s, hw_s, ha_s, xn_s = norm_shift_sample(hs, state_shift[0], a_norm[0], mu, w1, a1)
    rkvz_s = matmul_groups(xm_s, w_rkvz, db, F32)
    wl_s, al_s = lora_up(hw_s, ha_s, w2, a_w0[0], a2, a_a0[0])
    yg_s, s_state = wkv_sample(rkvz_s, wl_s, al_s, a_k_k[0], a_k_a[0], a_r_k[0].reshape(-1), a_gn_g[0],
                               a_gn_b[0], state_wkv[0])
    hs, hn_kv_s, hn_b_s = matmul_residual_norm(yg_s.reshape(db, e), w_out, hs, gains_b, db, BF16)
    tabs_s = rope_tables(jnp.full((db,), PAST_LEN, jnp.int32))
    k_s, v_s = matmul_rope(hn_kv_s, w_kv_bf, tabs_s, db, kvw, (F32, F32))
    q_s, = matmul_rope(hn_b_s, w_q, tabs_s, db, e, (F32,), scale=Q_SCALE)
    z_s = matmul_groups(hn_b_s[None], w_z[None], db, F32)[0]
    nq = e // HEAD_DIM
    att_s, s_cache_k, s_cache_v = attn_sample(
        sinks, q_s.reshape(db, nq, HEAD_DIM), cache_k.reshape(db, win, kvw), cache_v.reshape(db, win, kvw),
        k_s.reshape(db, 1, kvw), v_s.reshape(db, 1, kvw), z_s.reshape(db, nq, HEAD_DIM))
    y_s, = matmul_residual_norm(att_s.reshape(db, e), w_o, hs, final_norm[None], db, F32, emit_h=False)
    y_sample = y_s.reshape(db, 1, d)

    return (y_prompt, y_sample, p_state[None], p_state_shift,
            p_cache_k, p_cache_v,
            s_state[None], xn_s[None],
            s_cache_k.reshape(cache_k.shape), s_cache_v.reshape(cache_v.shape))
```

```python
import functools
import math

import jax
import jax.numpy as jnp
from jax import lax
from jax.experimental import pallas as pl
from jax.experimental.pallas import tpu as pltpu

F32 = jnp.float32
BF16 = jnp.bfloat16

HEAD_DIM = 64
N_KV_HEADS = 8
WINDOW = 128
BLOCK = 128
ROPE_DIM = HEAD_DIM // 4
ROPE_THETA = 500000.0
N_META = 16
PAST_LEN = 16384
RMS_EPS = 1e-6
GN_EPS = 64e-5
LEAD = (-N_META) % BLOCK
CHUNK = 64
WKV_ROWS = 128
HEADS_PER_STREAM = 32
HEADS_PER_STEP = 32
LORA_PAD = 128
MXU_TILE = 256
ROPE_SLAB = 512
VMEM_LIMIT = 48 * 1024 * 1024
DECAY_SCALE = math.exp(-0.5)
SOFTMAX_BATCH = 4
Q_SCALE = HEAD_DIM ** -0.5


def _cparams(sem):
    return pltpu.CompilerParams(dimension_semantics=sem, vmem_limit_bytes=VMEM_LIMIT)


def _sigmoid(x):
    return 1.0 / (1.0 + jnp.exp(-x))


def _silu(x):
    return x * _sigmoid(x)


def _dot(a, b):
    return jnp.dot(a, b, preferred_element_type=F32)


def _dot_nt(a, b):
    return lax.dot_general(a, b, (((1,), (1,)), ((), ())), preferred_element_type=F32)


def _dot_tn(a, b):
    return lax.dot_general(a, b, (((0,), (0,)), ((), ())), preferred_element_type=F32)


def _split_hi_lo(x):
    hi = x.astype(BF16)
    lo = (x - hi.astype(F32)).astype(BF16)
    return hi, lo


def _mixes(xn, prev, mu_ref, w1_ref, a1_ref, xm_ref, hw_ref, ha_ref):
    xx = prev - xn
    n_proj = xm_ref.shape[0]
    for p in range(n_proj):
        xm_ref[p] = (xn + xx * mu_ref[p:p + 1, :]).astype(xm_ref.dtype)
    xw = (xn + xx * mu_ref[n_proj:n_proj + 1, :]).astype(BF16)
    xa = (xn + xx * mu_ref[n_proj + 1:n_proj + 2, :]).astype(BF16)
    hw_ref[...] = jnp.tanh(_dot(xw, w1_ref[...])).astype(hw_ref.dtype)
    ha_ref[...] = _dot(xa, a1_ref[...]).astype(ha_ref.dtype)


def _norm_shift_kernel(x_ref, g_ref, mu_ref, w1_ref, a1_ref, xm_ref, hw_ref, ha_ref, last_ref, carry_ref):
    @pl.when(pl.program_id(1) == 0)
    def _():
        carry_ref[...] = jnp.zeros_like(carry_ref)

    x = x_ref[0]
    tm = x.shape[0]
    xn = x * lax.rsqrt(jnp.mean(x * x, axis=-1, keepdims=True) + RMS_EPS) * g_ref[...]
    rolled = pltpu.roll(xn, 1, axis=0)
    row = lax.broadcasted_iota(jnp.int32, xn.shape, 0)
    prev = jnp.where(row == 0, carry_ref[0:1, :], rolled)
    _mixes(xn, prev, mu_ref, w1_ref, a1_ref, xm_ref, hw_ref, ha_ref)
    carry_ref[0:1, :] = xn[tm - 1:tm, :]
    last_ref[0] = xn[tm - 1:tm, :]


def norm_shift_prompt(x, g, mu, w1, a1, tm):
    b, p, d = x.shape
    n_mix = mu.shape[0]
    n_proj = n_mix - 2
    lr = w1.shape[1]
    nt = p // tm
    const = lambda shape: pl.BlockSpec(shape, lambda i, t: (0,) * len(shape))
    hid = pl.BlockSpec((tm, lr), lambda i, t: (i * nt + t, 0))
    return pl.pallas_call(
        _norm_shift_kernel,
        grid=(b, nt),
        in_specs=[pl.BlockSpec((1, tm, d), lambda i, t: (i, t, 0)),
                  const((1, d)), const((n_mix, d)), const((d, lr)), const((d, lr))],
        out_specs=[pl.BlockSpec((n_proj, tm, d), lambda i, t: (0, i * nt + t, 0)), hid, hid,
                   pl.BlockSpec((1, 1, d), lambda i, t: (i, 0, 0))],
        out_shape=[jax.ShapeDtypeStruct((n_proj, b * p, d), BF16),
                   jax.ShapeDtypeStruct((b * p, lr), BF16), jax.ShapeDtypeStruct((b * p, lr), BF16),
                   jax.ShapeDtypeStruct((b, 1, d), F32)],
        scratch_shapes=[pltpu.VMEM((8, d), F32)],
        compiler_params=_cparams(("arbitrary", "arbitrary")),
        name="norm_shift_prompt",
    )(x, g.reshape(1, d), mu, w1, a1)


def _norm_shift_sample_kernel(x_ref, prev_ref, g_ref, mu_ref, w1_ref, a1_ref, xm_ref, hw_ref, ha_ref, xn_ref):
    x = x_ref[...]
    xn = x * lax.rsqrt(jnp.mean(x * x, axis=-1, keepdims=True) + RMS_EPS) * g_ref[...]
    xn_ref[...] = xn
    _mixes(xn, prev_ref[...], mu_ref, w1_ref, a1_ref, xm_ref, hw_ref, ha_ref)


def norm_shift_sample(x, prev, g, mu, w1, a1):
    m, d = x.shape
    lr = w1.shape[1]
    return pl.pallas_call(
        _norm_shift_sample_kernel,
        out_shape=[jax.ShapeDtypeStruct((mu.shape[0] - 2, m, d), BF16),
                   jax.ShapeDtypeStruct((m, lr), BF16), jax.ShapeDtypeStruct((m, lr), BF16),
                   jax.ShapeDtypeStruct((m, d), F32)],
        name="norm_shift_sample",
    )(x, prev, g.reshape(1, d), mu, w1, a1)


def _rope(y, cos, sin_a, sin_b):
    half = ROPE_DIM // 2
    step = ROPE_SLAB
    rep = step // cos.shape[1]
    tile = lambda t: jnp.concatenate([t] * rep, axis=1)
    cos_t, sa_t, sb_t = tile(cos), tile(sin_a), tile(sin_b)
    outs = []
    for j in range(y.shape[1] // step):
        ys = y[:, j * step:(j + 1) * step]
        outs.append(ys * cos_t + pltpu.roll(ys, step - half, axis=1) * sa_t + pltpu.roll(ys, half, axis=1) * sb_t)
    return jnp.concatenate(outs, axis=1) if len(outs) > 1 else outs[0]


def _mm_group_kernel(x_ref, w_ref, o_ref):
    o_ref[0] = _dot(x_ref[0], w_ref[0]).astype(o_ref.dtype)


def matmul_groups(x, w, tm, out_dtype):
    g, _, n = w.shape
    _, m, kdim = x.shape
    return pl.pallas_call(
        _mm_group_kernel,
        grid=(g, m // tm),
        in_specs=[pl.BlockSpec((1, tm, kdim), lambda q, i: (q, i, 0)),
                  pl.BlockSpec((1, kdim, n), lambda q, i: (q, 0, 0))],
        out_specs=pl.BlockSpec((1, tm, n), lambda q, i: (q, i, 0)),
        out_shape=jax.ShapeDtypeStruct((g, m, n), out_dtype),
        compiler_params=_cparams(("arbitrary", "arbitrary")),
        name="matmul_groups",
    )(x, w)


def _mm_rope_kernel(x_ref, w_ref, cos_ref, sa_ref, sb_ref, *o_refs, n_rope, scale):
    y = _dot(x_ref[...], w_ref[...])
    rot = _rope(y[:, :n_rope], cos_ref[...], sa_ref[...], sb_ref[...])
    if scale != 1.0:
        rot = rot * scale
    o_refs[0][...] = rot.astype(o_refs[0].dtype)
    if len(o_refs) > 1:
        o_refs[1][...] = y[:, n_rope:].astype(o_refs[1].dtype)


def matmul_rope(x, w, tables, tm, n_rope, out_dtypes, scale=1.0):
    m, kdim = x.shape
    n = w.shape[1]
    lanes = tables[0].shape[1]
    widths = [n_rope] + ([n - n_rope] if n > n_rope else [])
    tab = pl.BlockSpec((tm, lanes), lambda i: (i, 0))
    outs = pl.pallas_call(
        functools.partial(_mm_rope_kernel, n_rope=n_rope, scale=scale),
        grid=(m // tm,),
        in_specs=[pl.BlockSpec((tm, kdim), lambda i: (i, 0)),
                  pl.BlockSpec((kdim, n), lambda i: (0, 0), pipeline_mode=pl.Buffered(1)),
                  tab, tab, tab],
        out_specs=[pl.BlockSpec((tm, wd), lambda i: (i, 0)) for wd in widths],
        out_shape=[jax.ShapeDtypeStruct((m, wd), dt) for wd, dt in zip(widths, out_dtypes)],
        compiler_params=_cparams(("arbitrary",)),
        name="matmul_rope",
    )(x, w, *tables)
    return outs


def _mm_res_norm_kernel(x_ref, w_ref, res_ref, g_ref, *out_refs, emit_h):
    h = res_ref[...] + _dot(x_ref[...], w_ref[...])
    hn_refs = out_refs
    if emit_h:
        out_refs[0][...] = h
        hn_refs = out_refs[1:]
    inv = lax.rsqrt(jnp.mean(h * h, axis=-1, keepdims=True) + RMS_EPS)
    for j, hn_ref in enumerate(hn_refs):
        hn_ref[...] = (h * inv * g_ref[j:j + 1, :]).astype(hn_ref.dtype)


def matmul_residual_norm(x, w, res, gains, tm, norm_dtype, emit_h=True):
    m, kdim = x.shape
    n = w.shape[1]
    ng = gains.shape[0]
    row = lambda width: pl.BlockSpec((tm, width), lambda i: (i, 0))
    return pl.pallas_call(
        functools.partial(_mm_res_norm_kernel, emit_h=emit_h),
        grid=(m // tm,),
        in_specs=[row(kdim),
                  pl.BlockSpec((kdim, n), lambda i: (0, 0), pipeline_mode=pl.Buffered(1)),
                  row(n),
                  pl.BlockSpec((ng, n), lambda i: (0, 0))],
        out_specs=[row(n)] * (int(emit_h) + ng),
        out_shape=[jax.ShapeDtypeStruct((m, n), F32)] * int(emit_h) + [jax.ShapeDtypeStruct((m, n), norm_dtype)] * ng,
        compiler_params=_cparams(("arbitrary",)),
        name="matmul_residual_norm",
    )(x, w, res, gains)


def _lora_up_kernel(hw_ref, ha_ref, w2_ref, w0_ref, a2_ref, a0_ref, wl_ref, al_ref):
    wl_ref[...] = w0_ref[...] + _dot(hw_ref[...], w2_ref[...])
    al_ref[...] = a0_ref[...] + _dot(ha_ref[...], a2_ref[...])


def lora_up(hw, ha, w2, w0, a2, a0):
    m = hw.shape[0]
    e = w2.shape[1]
    return pl.pallas_call(
        _lora_up_kernel,
        out_shape=[jax.ShapeDtypeStruct((m, e), F32), jax.ShapeDtypeStruct((m, e), F32)],
        name="lora_up",
    )(hw, ha, w2, w0.reshape(1, e), a2, a0.reshape(1, e))


def _seg_sum(x, ones_bd):
    hi = x.astype(BF16)
    outs = []
    for c in range(x.shape[1] // MXU_TILE):
        sl = slice(c * MXU_TILE, (c + 1) * MXU_TILE)
        outs.append(_dot(hi[:, sl], ones_bd))
    return jnp.concatenate(outs, axis=1) if len(outs) > 1 else outs[0]


def _wkv_prompt_kernel(r_ref, k_ref, v_ref, z_ref, hw_ref, ha_ref, w2_ref, w0_ref, a2_ref, a0_ref,
                       kk_ref, ka_ref, rk_ref, gg_ref, gb_ref, yg_ref, sout_ref, s_ref):
    t_idx = pl.program_id(2)
    c = CHUNK
    hd = HEAD_DIM

    @pl.when(t_idx == 0)
    def _():
        s_ref[...] = jnp.zeros_like(s_ref)

    tb = r_ref.shape[1]
    nh = HEADS_PER_STREAM
    hw = nh * hd
    nc = tb // c

    li = lax.broadcasted_iota(jnp.int32, (MXU_TILE, MXU_TILE), 0) // hd
    lj = lax.broadcasted_iota(jnp.int32, (MXU_TILE, MXU_TILE), 1) // hd
    ones_bd = jnp.where(li == lj, 1.0, 0.0).astype(BF16)
    bi_ = lax.broadcasted_iota(jnp.int32, (tb, tb), 0)
    bj_ = lax.broadcasted_iota(jnp.int32, (tb, tb), 1)
    tri_incl = jnp.where((bj_ <= bi_) & (bj_ // c == bi_ // c), 1.0, 0.0).astype(BF16)
    ti = lax.broadcasted_iota(jnp.int32, (c, c), 0)
    tj = lax.broadcasted_iota(jnp.int32, (c, c), 1)
    ai = lax.broadcasted_iota(jnp.int32, (c, 2 * c), 0)
    aj = lax.broadcasted_iota(jnp.int32, (c, 2 * c), 1)
    upper = aj >= c
    aj_mod = jnp.where(upper, aj - c, aj)
    masks = dict(
        strict=tj < ti,
        upper=upper,
        top_b=aj < ai,
        top_k=upper & (aj_mod < ai),
        bot=aj_mod <= ai,
        eye_hi=jnp.where(aj - c == ai, 1.0, 0.0).astype(F32))

    for st in range(r_ref.shape[2] // hw):
        _wkv_stream(st, hw, nc, r_ref, k_ref, v_ref, z_ref, hw_ref, ha_ref, w2_ref, w0_ref, a2_ref, a0_ref,
                    kk_ref, ka_ref, rk_ref, gg_ref, gb_ref, yg_ref, s_ref, ones_bd, tri_incl, masks)

    @pl.when(t_idx == pl.num_programs(2) - 1)
    def _():
        sout_ref[0] = s_ref[...]


def _wkv_stream(st, hw, nc, r_ref, k_ref, v_ref, z_ref, hw_ref, ha_ref, w2_ref, w0_ref, a2_ref, a0_ref,
                kk_ref, ka_ref, rk_ref, gg_ref, gb_ref, yg_ref, s_ref, ones_bd, tri_incl, masks):
    c = CHUNK
    hd = HEAD_DIM
    nh = hw // hd
    ls = slice(st * hw, (st + 1) * hw)
    h0 = st * nh
    r = r_ref[0, :, ls]
    k = k_ref[0, :, ls]
    v = v_ref[0, :, ls]
    wl = w0_ref[:, ls] + _dot(hw_ref[...], w2_ref[:, ls])
    al = a0_ref[:, ls] + _dot(ha_ref[...], a2_ref[:, ls])
    a = _sigmoid(al)
    lw = -DECAY_SCALE * _sigmoid(wl)
    kk = k * kk_ref[:, ls]
    n2 = _seg_sum(kk * kk, ones_bd)
    kk = kk / jnp.maximum(jnp.sqrt(n2), 1e-12)
    k2 = k * (1.0 + (a - 1.0) * ka_ref[:, ls])
    bb = kk * a

    lw_hi, lw_lo = _split_hi_lo(lw)
    g = _dot(tri_incl, lw_hi) + _dot(tri_incl, lw_lo)
    mid = lambda ci: g[ci * c + c // 2 - 1:ci * c + c // 2, :]
    gm = jnp.concatenate([jnp.broadcast_to(mid(ci), (c, hw)) for ci in range(nc)], axis=0)
    e_a = jnp.exp(g - gm)
    e_prev = jnp.exp(g - lw - gm)
    e_inv = jnp.exp(gm - g)
    e1 = [jnp.exp(mid(ci)) for ci in range(nc)]
    e2 = [jnp.exp(g[ci * c + c - 1:ci * c + c, :] - mid(ci)) for ci in range(nc)]

    kkd = (kk * e_prev).astype(BF16)
    rd = (r * e_a).astype(BF16)
    bi = (bb * e_inv).astype(BF16)
    ki = (k2 * e_inv).astype(BF16)
    v_bf = v.astype(BF16)
    zeros_cv = jnp.zeros((c, hd), BF16)

    pairs = [(ci, h) for ci in range(nc) for h in range(nh)]
    rows = lambda ci: slice(ci * c, (ci + 1) * c)
    cols = lambda h: slice(h * hd, (h + 1) * hd)
    xs = {(ci, h): jnp.concatenate([kkd[rows(ci), cols(h)], rd[rows(ci), cols(h)]], axis=0) for ci, h in pairs}
    r1s = {(ci, h): jnp.concatenate([bi[rows(ci), cols(h)], ki[rows(ci), cols(h)]], axis=0) for ci, h in pairs}
    vs = {(ci, h): v_bf[rows(ci), cols(h)] for ci, h in pairs}
    a_mats = {p: _dot_nt(xs[p], r1s[p]) for p in pairs}
    lk_vs = {p: _dot(jnp.where(masks["top_k"], a_mats[p][:c, :], 0.0).astype(BF16),
                     jnp.concatenate([zeros_cv, vs[p]], axis=0)) for p in pairs}
    eye_hi = masks["eye_hi"]
    zs = {p: _dot(jnp.where(masks["strict"], a_mats[p][:c, :c], 0.0).astype(BF16),
                  (jnp.where(masks["top_b"], a_mats[p][:c, :], 0.0) - eye_hi).astype(BF16)) + eye_hi
          for p in pairs}
    for _ in range(int(math.log2(c)) - 1):
        outs = {p: _dot(zs[p][:, :c].astype(BF16), zs[p].astype(BF16)) for p in pairs}
        zs = {p: outs[p] + jnp.where(masks["upper"], zs[p], 0.0) for p in pairs}
    a_bots = {p: jnp.where(masks["bot"], a_mats[p][c:, :], 0.0).astype(BF16) for p in pairs}
    t_bf = {p: zs[p].astype(BF16) for p in pairs}

    state = [s_ref[h0 + h] for h in range(nh)]
    y_rows = []
    for ci in range(nc):
        hs = range(nh)
        sms = [state[h] * e1[ci][:, cols(h)] for h in hs]
        p_mats = [_dot_nt(xs[ci, h], sms[h].astype(BF16)) for h in hs]
        us = [-_dot(t_bf[ci, h], jnp.concatenate([zeros_cv, (p_mats[h][:c, :] + lk_vs[ci, h]).astype(BF16)], axis=0))
              for h in hs]
        uvs = [jnp.concatenate([us[h].astype(BF16), vs[ci, h]], axis=0) for h in hs]
        ys = [p_mats[h][c:, :] + _dot(a_bots[ci, h], uvs[h]) for h in hs]
        state = [(sms[h] + _dot_tn(uvs[h], r1s[ci, h])) * e2[ci][:, cols(h)] for h in hs]
        y_rows.append(jnp.concatenate(ys, axis=1))
    for h in range(nh):
        s_ref[h0 + h] = state[h]
    y = jnp.concatenate(y_rows, axis=0) if nc > 1 else y_rows[0]

    inv_n = 1.0 / hd
    mean = _seg_sum(y, ones_bd) * inv_n
    yc = y - mean
    var = _seg_sum(yc * yc, ones_bd) * inv_n
    yn = yc * lax.rsqrt(var + GN_EPS) * gg_ref[:, ls] + gb_ref[:, ls]
    bonus = _seg_sum(r * k2 * rk_ref[:, ls], ones_bd)
    yg_ref[:, ls] = ((yn + bonus * v) * _silu(z_ref[0, :, ls])).astype(BF16)


def wkv_prompt(rkvz, hw_act, ha_act, w2, w0, a2, a0, k_k, k_a, r_k, gn_g, gn_b, n_batch):
    _, m, e = rkvz.shape
    p = m // n_batch
    tb = WKV_ROWS
    nt = p // tb
    hw = HEADS_PER_STEP * HEAD_DIM
    nh = e // HEAD_DIM
    row = lambda i, g, t: (i * nt + t, g)
    proj = lambda q: pl.BlockSpec((1, tb, hw), lambda i, g, t: (q, i * nt + t, g))
    par = pl.BlockSpec((1, hw), lambda i, g, t: (0, g))
    lr = w2.shape[0]
    hid = pl.BlockSpec((tb, lr), lambda i, g, t: (i * nt + t, 0))
    up = pl.BlockSpec((lr, hw), lambda i, g, t: (0, g))
    return pl.pallas_call(
        _wkv_prompt_kernel,
        grid=(n_batch, e // hw, nt),
        in_specs=[proj(0), proj(1), proj(2), proj(3), hid, hid, up, par, up, par,
                  par, par, par, par, par],
        out_specs=[pl.BlockSpec((tb, hw), row),
                   pl.BlockSpec((1, HEADS_PER_STEP, HEAD_DIM, HEAD_DIM), lambda i, g, t: (i, g, 0, 0))],
        out_shape=[jax.ShapeDtypeStruct((m, e), BF16),
                   jax.ShapeDtypeStruct((n_batch, nh, HEAD_DIM, HEAD_DIM), F32)],
        scratch_shapes=[pltpu.VMEM((HEADS_PER_STEP, HEAD_DIM, HEAD_DIM), F32)],
        compiler_params=_cparams(("arbitrary", "arbitrary", "arbitrary")),
        name="wkv_prompt",
    )(rkvz, rkvz, rkvz, rkvz, hw_act, ha_act, w2, w0.reshape(1, e), a2, a0.reshape(1, e),
      k_k.reshape(1, e), k_a.reshape(1, e), r_k.reshape(1, e), gn_g.reshape(1, e), gn_b.reshape(1, e))


def _wkv_sample_kernel(r_ref, k_ref, v_ref, z_ref, wl_ref, al_ref, kk_ref, ka_ref, rk_ref, gg_ref, gb_ref,
                       s_ref, yg_ref, sout_ref, y_scr):
    hd = HEAD_DIM
    r = r_ref[0]
    k = k_ref[0]
    v = v_ref[0]
    a = _sigmoid(al_ref[0])
    d = jnp.exp(-DECAY_SCALE * _sigmoid(wl_ref[0]))
    kk = k * kk_ref[...]
    kk = kk / jnp.maximum(jnp.sqrt(jnp.sum(kk * kk, axis=-1, keepdims=True)), 1e-12)
    k2 = k * (1.0 + (a - 1.0) * ka_ref[...])
    bb = kk * a
    nh = r.shape[0]
    ii = lax.broadcasted_iota(jnp.int32, (hd, hd), 0)
    jj = lax.broadcasted_iota(jnp.int32, (hd, hd), 1)
    eye = ii == jj

    row = lambda x, h: x[h:h + 1, :]
    group = 8
    for h0 in range(0, nh, group):
        hs = range(h0, h0 + group)
        s = {h: s_ref[0, h] for h in hs}
        sa = {h: jnp.sum(s[h] * row(kk, h), axis=-1, keepdims=True) for h in hs}
        v_col = {h: jnp.sum(jnp.where(eye, row(v, h), 0.0), axis=-1, keepdims=True) for h in hs}
        s_new = {h: s[h] * row(d, h) - sa[h] * row(bb, h) + v_col[h] * row(k2, h) for h in hs}
        y_col = {h: jnp.sum(s_new[h] * row(r, h), axis=-1, keepdims=True) for h in hs}
        for h in hs:
            sout_ref[0, h] = s_new[h]
            y_scr[h:h + 1, :] = jnp.sum(jnp.where(eye, y_col[h], 0.0), axis=0, keepdims=True)

    y = y_scr[...]
    mean = jnp.mean(y, axis=-1, keepdims=True)
    yc = y - mean
    var = jnp.mean(yc * yc, axis=-1, keepdims=True)
    yn = yc * lax.rsqrt(var + GN_EPS) * gg_ref[...] + gb_ref[...]
    bonus = jnp.sum(r * k2 * rk_ref[...], axis=-1, keepdims=True)
    yg_ref[0] = ((yn + bonus * v) * _silu(z_ref[0])).astype(BF16)


def wkv_sample(rkvz, wl, al, k_k, k_a, r_k, gn_g, gn_b, state):
    _, m, e = rkvz.shape
    nh = e // HEAD_DIM
    hd = HEAD_DIM
    rkvz4 = rkvz.reshape(4, m, nh, hd)
    proj = lambda q: pl.BlockSpec((None, 1, nh, hd), lambda i: (q, i, 0, 0))
    tok = pl.BlockSpec((1, nh, hd), lambda i: (i, 0, 0))
    par = pl.BlockSpec((nh, hd), lambda i: (0, 0))
    st = pl.BlockSpec((1, nh, hd, hd), lambda i: (i, 0, 0, 0))
    as_heads = lambda x: x.reshape(nh, hd)
    return pl.pallas_call(
        _wkv_sample_kernel,
        grid=(m,),
        in_specs=[proj(0), proj(1), proj(2), proj(3), tok, tok, par, par, par, par, par, st],
        out_specs=[tok, st],
        out_shape=[jax.ShapeDtypeStruct((m, nh, hd), BF16), jax.ShapeDtypeStruct(state.shape, F32)],
        scratch_shapes=[pltpu.VMEM((nh, hd), F32)],
        compiler_params=_cparams(("arbitrary",)),
        name="wkv_sample",
    )(rkvz4, rkvz4, rkvz4, rkvz4, wl.reshape(m, nh, hd), al.reshape(m, nh, hd),
      as_heads(k_k), as_heads(k_a), as_heads(r_k), as_heads(gn_g), as_heads(gn_b), state)


def rope_tables(pos):
    half = ROPE_DIM // 2
    inv_freq = ROPE_THETA ** (-jnp.arange(half, dtype=F32) * 2.0 / ROPE_DIM)
    ang = pos.astype(F32)[:, None] * inv_freq[None, :]
    cos = jnp.cos(ang)
    sin = jnp.sin(ang)
    rows = pos.shape[0]
    ones = jnp.ones((rows, HEAD_DIM - ROPE_DIM), F32)
    zeros_h = jnp.zeros((rows, half), F32)
    zeros_r = jnp.zeros((rows, HEAD_DIM - ROPE_DIM), F32)
    cos_h = jnp.concatenate([cos, cos, ones], axis=1)
    sa_h = jnp.concatenate([-sin, zeros_h, zeros_r], axis=1)
    sb_h = jnp.concatenate([zeros_h, sin, zeros_r], axis=1)
    two = lambda t: jnp.concatenate([t, t], axis=1)
    return two(cos_h), two(sa_h), two(sb_h)


def _attn_prompt_kernel(sink_ref, q_ref, kc_ref, kp_ref, vc_ref, vp_ref, z_ref, o_ref):
    n = pl.program_id(1)
    hd = HEAD_DIM
    blk = q_ref.shape[0]
    n_kv = kc_ref.shape[1] // hd
    grp = q_ref.shape[1] // (n_kv * hd)
    qi = lax.broadcasted_iota(jnp.int32, (blk, 2 * blk), 0)
    kj = lax.broadcasted_iota(jnp.int32, (blk, 2 * blk), 1) - blk
    kpos = n * blk + kj
    diff = qi - kj
    valid = (kpos >= LEAD) & (diff >= 0) & (diff <= WINDOW)
    k_all = jnp.concatenate([kp_ref[...], kc_ref[...]], axis=0).astype(BF16)
    v_all = jnp.concatenate([vp_ref[...], vc_ref[...]], axis=0).astype(BF16)

    def scores(h):
        k_h = k_all[:, h * hd:(h + 1) * hd]
        return [_dot_nt(q_ref[:, (h * grp + gi) * hd:(h * grp + gi + 1) * hd], k_h) for gi in range(grp)]

    outs = []
    s_next = scores(0)
    for h in range(n_kv):
        s_cur = s_next
        if h + 1 < n_kv:
            s_next = scores(h + 1)
        v_h = v_all[:, h * hd:(h + 1) * hd]
        for g0 in range(0, grp, SOFTMAX_BATCH):
            gs = range(g0, min(g0 + SOFTMAX_BATCH, grp))
            sks = {gi: sink_ref[h * grp + gi] for gi in gs}
            ss = {gi: jnp.where(valid, s_cur[gi], -jnp.inf) for gi in gs}
            ms = {gi: jnp.maximum(jnp.max(ss[gi], axis=-1, keepdims=True), sks[gi]) for gi in gs}
            ps = {gi: jnp.exp(ss[gi] - ms[gi]) for gi in gs}
            dens = {gi: jnp.sum(ps[gi], axis=-1, keepdims=True) + jnp.exp(sks[gi] - ms[gi]) for gi in gs}
            outs += [_dot(ps[gi].astype(BF16), v_h) / dens[gi] for gi in gs]
    att = jnp.concatenate(outs, axis=1)
    o_ref[...] = (att * _silu(z_ref[...])).astype(o_ref.dtype)


def attn_prompt(sinks, q, k, v, z, n_batch):
    m, e = q.shape
    nb = m // (n_batch * BLOCK)
    kw = k.shape[1]
    cur = lambda i, n: (i * nb + n, 0)
    prv = lambda i, n: (i * nb + jnp.maximum(n - 1, 0), 0)
    return pl.pallas_call(
        _attn_prompt_kernel,
        grid=(n_batch, nb),
        in_specs=[pl.BlockSpec(memory_space=pltpu.SMEM),
                  pl.BlockSpec((BLOCK, e), cur),
                  pl.BlockSpec((BLOCK, kw), cur), pl.BlockSpec((BLOCK, kw), prv),
                  pl.BlockSpec((BLOCK, kw), cur), pl.BlockSpec((BLOCK, kw), prv),
                  pl.BlockSpec((BLOCK, e), cur)],
        out_specs=pl.BlockSpec((BLOCK, e), cur),
        out_shape=jax.ShapeDtypeStruct((m, e), BF16),
        compiler_params=_cparams(("arbitrary", "arbitrary")),
        name="attn_prompt",
    )(sinks, q, k, k, v, v, z)


def _attn_sample_kernel(sink_ref, q_ref, kc_ref, vc_ref, kn_ref, vn_ref, z_ref, o_ref, ko_ref, vo_ref):
    hd = HEAD_DIM
    win = kc_ref.shape[1]
    n_kv = kc_ref.shape[2] // hd
    nq = q_ref.shape[1]
    grp = nq // n_kv
    pad = 8
    kc = kc_ref[0]
    vc = vc_ref[0]
    kn = kn_ref[0]
    vn = vn_ref[0]
    first = lax.broadcasted_iota(jnp.int32, (pad, kc.shape[1]), 0) == 0
    k_all = jnp.concatenate([kc, jnp.where(first, kn, 0.0)], axis=0).astype(BF16)
    v_all = jnp.concatenate([vc, jnp.where(first, vn, 0.0)], axis=0).astype(BF16)
    col = lax.broadcasted_iota(jnp.int32, (grp, win + pad), 1)
    valid = (col <= win) & (win - col <= WINDOW)
    q = q_ref[0].astype(BF16)
    row_i = lax.broadcasted_iota(jnp.int32, (grp, 1), 0)
    hs = range(n_kv)
    sks = []
    for h in hs:
        sk = jnp.zeros((grp, 1), F32)
        for gi in range(grp):
            sk = jnp.where(row_i == gi, sink_ref[h * grp + gi], sk)
        sks.append(sk)
    ss = [jnp.where(valid, _dot_nt(q[h * grp:(h + 1) * grp, :], k_all[:, h * hd:(h + 1) * hd]), -jnp.inf)
          for h in hs]
    ms = [jnp.maximum(jnp.max(ss[h], axis=-1, keepdims=True), sks[h]) for h in hs]
    ps = [jnp.exp(ss[h] - ms[h]) for h in hs]
    dens = [jnp.sum(ps[h], axis=-1, keepdims=True) + jnp.exp(sks[h] - ms[h]) for h in hs]
    outs = [_dot(ps[h].astype(BF16), v_all[:, h * hd:(h + 1) * hd]) / dens[h] for h in hs]
    att = jnp.concatenate(outs, axis=0)
    o_ref[0] = (att * _silu(z_ref[0])).astype(o_ref.dtype)
    last = lax.broadcasted_iota(jnp.int32, kc.shape, 0) == win - 1
    ko_ref[0] = jnp.where(last, kn, pltpu.roll(kc, win - 1, axis=0))
    vo_ref[0] = jnp.where(last, vn, pltpu.roll(vc, win - 1, axis=0))


def attn_sample(sinks, q, cache_k, cache_v, k_new, v_new, z):
    m, win, kw = cache_k.shape
    nq = q.shape[1]
    hd = HEAD_DIM
    tok = pl.BlockSpec((1, nq, hd), lambda i: (i, 0, 0))
    cache = pl.BlockSpec((1, win, kw), lambda i: (i, 0, 0))
    new = pl.BlockSpec((1, 1, kw), lambda i: (i, 0, 0))
    return pl.pallas_call(
        _attn_sample_kernel,
        grid=(m,),
        in_specs=[pl.BlockSpec(memory_space=pltpu.SMEM), tok, cache, cache, new, new, tok],
        out_specs=[tok, cache, cache],
        out_shape=[jax.ShapeDtypeStruct((m, nq, hd), BF16),
                   jax.ShapeDtypeStruct(cache_k.shape, F32), jax.ShapeDtypeStruct(cache_v.shape, F32)],
        compiler_params=_cparams(("arbitrary",)),
        name="attn_sample",
    )(sinks, q, cache_k, cache_v, k_new, v_new, z)


def _pad_lora(w_down, w_up):
    r = w_down.shape[1]
    return (jnp.pad(w_down, ((0, 0), (0, LORA_PAD - r))).astype(BF16),
            jnp.pad(w_up, ((0, LORA_PAD - r), (0, 0))).astype(BF16))


def kernel(x_prompt, x_sample, state_wkv, state_shift, cache_k, cache_v, meta_tokens, a_norm, a_mu, a_w_rkvz,
           a_w0, a_w1, a_w2, a_a0, a_a1, a_a2, a_k_k, a_k_a, a_r_k, a_gn_g, a_gn_b, a_w_out, kv_norm, w_kv,
           b_norm, b_w_qz, b_sinks, b_w_o, final_norm):
    nb, seq, d = x_prompt.shape
    db, dseq, _ = x_sample.shape
    assert dseq == 1 and a_norm.shape[0] == 1 and b_norm.shape[0] == 1
    e = a_w_rkvz.shape[3]
    win = cache_k.shape[1]
    p_len = LEAD + N_META + seq
    assert p_len % BLOCK == 0 and (LEAD + N_META) == BLOCK
    m_p = nb * p_len
    kvw = N_KV_HEADS * HEAD_DIM

    w_rkvz = a_w_rkvz[0].astype(BF16)
    w1, w2 = _pad_lora(a_w1[0], a_w2[0])
    a1, a2 = _pad_lora(a_a1[0], a_a2[0])
    w_out = a_w_out[0].astype(BF16)
    w_kv_bf = w_kv.astype(BF16)
    w_q = b_w_qz[0][:, :e].astype(BF16)
    w_z = b_w_qz[0][:, e:].astype(BF16)
    w_o = b_w_o[0].astype(BF16)
    mu = a_mu[0]
    sinks = b_sinks[0]
    gains_b = jnp.stack([kv_norm, b_norm[0]])

    tm = p_len // 8

    hp = jnp.concatenate([jnp.zeros((nb, LEAD, d), F32),
                          jnp.broadcast_to(meta_tokens[None], (nb, N_META, d)), x_prompt], axis=1)
    xm, hw_p, ha_p, x_last = norm_shift_prompt(hp, a_norm[0], mu, w1, a1, tm)
    p_state_shift = x_last.reshape(1, nb, d)
    rkvz = matmul_groups(xm, w_rkvz, tm, F32)
    yg, p_state = wkv_prompt(rkvz, hw_p, ha_p, w2, a_w0[0], a2, a_a0[0], a_k_k[0], a_k_a[0],
                             a_r_k[0].reshape(-1), a_gn_g[0], a_gn_b[0], nb)
    hp, hn_kv, hn_b = matmul_residual_norm(yg, w_out, hp.reshape(m_p, d), gains_b, tm, BF16)

    pos_p = jnp.maximum(jnp.arange(p_len, dtype=jnp.int32) - LEAD, 0)
    tabs_p = tuple(jnp.tile(t, (nb, 1)) for t in rope_tables(pos_p))
    k_p, v_p = matmul_rope(hn_kv, w_kv_bf, tabs_p, tm, kvw, (F32, F32))
    q_p, = matmul_rope(hn_b, w_q, tabs_p, tm, e, (BF16,), scale=Q_SCALE)
    z_p = matmul_groups(hn_b[None], w_z[None], tm, F32)[0]
    att = attn_prompt(sinks, q_p, k_p, v_p, z_p, nb)
    y_pad, = matmul_residual_norm(att, w_o, hp, final_norm[None], tm, F32, emit_h=False)
    y_prompt = y_pad.reshape(nb, p_len, d)[:, LEAD + N_META:]
    p_cache_k = k_p.reshape(nb, p_len, N_KV_HEADS, HEAD_DIM)[:, -win:]
    p_cache_v = v_p.reshape(nb, p_len, N_KV_HEADS, HEAD_DIM)[:, -win:]

    hs = x_sample.reshape(db, d)
    xm_s, hw_s, ha_s, xn_s = norm_shift_sample(hs, state_shift[0], a_norm[0], mu, w1, a1)
    rkvz_s = matmul_groups(xm_s, w_rkvz, db, F32)
    wl_s, al_s = lora_up(hw_s, ha_s, w2, a_w0[0], a2, a_a0[0])
    yg_s, s_state = wkv_sample(rkvz_s, wl_s, al_s, a_k_k[0], a_k_a[0], a_r_k[0].reshape(-1), a_gn_g[0],
                               a_gn_b[0], state_wkv[0])
    hs, hn_kv_s, hn_b_s = matmul_residual_norm(yg_s.reshape(db, e), w_out, hs, gains_b, db, BF16)
    tabs_s = rope_tables(jnp.full((db,), PAST_LEN, jnp.int32))
    k_s, v_s = matmul_rope(hn_kv_s, w_kv_bf, tabs_s, db, kvw, (F32, F32))
    q_s, = matmul_rope(hn_b_s, w_q, tabs_s, db, e, (F32,), scale=Q_SCALE)
    z_s = matmul_groups(hn_b_s[None], w_z[None], db, F32)[0]
    nq = e // HEAD_DIM
    att_s, s_cache_k, s_cache_v = attn_sample(
        sinks, q_s.reshape(db, nq, HEAD_DIM), cache_k.reshape(db, win, kvw), cache_v.reshape(db, win, kvw),
        k_s.reshape(db, 1, kvw), v_s.reshape(db, 1, kvw), z_s.reshape(db, nq, HEAD_DIM))
    y_s, = matmul_residual_norm(att_s.reshape(db, e), w_o, hs, final_norm[None], db, F32, emit_h=False)
    y_sample = y_s.reshape(db, 1, d)

    return (y_prompt, y_sample, p_state[None], p_state_shift,
            p_cache_k, p_cache_v,
            s_state[None], xn_s[None],
            s_cache_k.reshape(cache_k.shape), s_cache_v.reshape(cache_v.shape))
```

```python
import functools
import math

import jax
import jax.numpy as jnp
from jax import lax
from jax.experimental import pallas as pl
from jax.experimental.pallas import tpu as pltpu

F32 = jnp.float32
BF16 = jnp.bfloat16

HEAD_DIM = 64
N_KV_HEADS = 8
WINDOW = 128
BLOCK = 128
ROPE_DIM = HEAD_DIM // 4
ROPE_THETA = 500000.0
N_META = 16
PAST_LEN = 16384
RMS_EPS = 1e-6
GN_EPS = 64e-5
LEAD = (-N_META) % BLOCK
CHUNK = 64
WKV_ROWS = 128
HEADS_PER_STREAM = 32
HEADS_PER_STEP = 32
LORA_PAD = 128
MXU_TILE = 256
ROPE_SLAB = 512
VMEM_LIMIT = 48 * 1024 * 1024
DECAY_SCALE = math.exp(-0.5)
SOFTMAX_BATCH = 4
Q_SCALE = HEAD_DIM ** -0.5


def _cparams(sem):
    return pltpu.CompilerParams(dimension_semantics=sem, vmem_limit_bytes=VMEM_LIMIT)


def _sigmoid(x):
    return 1.0 / (1.0 + jnp.exp(-x))


def _silu(x):
    return x * _sigmoid(x)


def _dot(a, b):
    return jnp.dot(a, b, preferred_element_type=F32)


def _dot_nt(a, b):
    return lax.dot_general(a, b, (((1,), (1,)), ((), ())), preferred_element_type=F32)


def _dot_tn(a, b):
    return lax.dot_general(a, b, (((0,), (0,)), ((), ())), preferred_element_type=F32)


def _split_hi_lo(x):
    hi = x.astype(BF16)
    lo = (x - hi.astype(F32)).astype(BF16)
    return hi, lo


def _mixes(xn, prev, mu_ref, w1_ref, a1_ref, xm_ref, hw_ref, ha_ref):
    xx = prev - xn
    n_proj = xm_ref.shape[0]
    for p in range(n_proj):
        xm_ref[p] = (xn + xx * mu_ref[p:p + 1, :]).astype(xm_ref.dtype)
    xw = (xn + xx * mu_ref[n_proj:n_proj + 1, :]).astype(BF16)
    xa = (xn + xx * mu_ref[n_proj + 1:n_proj + 2, :]).astype(BF16)
    hw_ref[...] = jnp.tanh(_dot(xw, w1_ref[...])).astype(hw_ref.dtype)
    ha_ref[...] = _dot(xa, a1_ref[...]).astype(ha_ref.dtype)


def _norm_shift_kernel(x_ref, g_ref, mu_ref, w1_ref, a1_ref, xm_ref, hw_ref, ha_ref, last_ref, carry_ref):
    @pl.when(pl.program_id(1) == 0)
    def _():
        carry_ref[...] = jnp.zeros_like(carry_ref)

    x = x_ref[0]
    tm = x.shape[0]
    xn = x * lax.rsqrt(jnp.mean(x * x, axis=-1, keepdims=True) + RMS_EPS) * g_ref[...]
    rolled = pltpu.roll(xn, 1, axis=0)
    row = lax.broadcasted_iota(jnp.int32, xn.shape, 0)
    prev = jnp.where(row == 0, carry_ref[0:1, :], rolled)
    _mixes(xn, prev, mu_ref, w1_ref, a1_ref, xm_ref, hw_ref, ha_ref)
    carry_ref[0:1, :] = xn[tm - 1:tm, :]
    last_ref[0] = xn[tm - 1:tm, :]


def norm_shift_prompt(x, g, mu, w1, a1, tm):
    b, p, d = x.shape
    n_mix = mu.shape[0]
    n_proj = n_mix - 2
    lr = w1.shape[1]
    nt = p // tm
    const = lambda shape: pl.BlockSpec(shape, lambda i, t: (0,) * len(shape))
    hid = pl.BlockSpec((tm, lr), lambda i, t: (i * nt + t, 0))
    return pl.pallas_call(
        _norm_shift_kernel,
        grid=(b, nt),
        in_specs=[pl.BlockSpec((1, tm, d), lambda i, t: (i, t, 0)),
                  const((1, d)), const((n_mix, d)), const((d, lr)), const((d, lr))],
        out_specs=[pl.BlockSpec((n_proj, tm, d), lambda i, t: (0, i * nt + t, 0)), hid, hid,
                   pl.BlockSpec((1, 1, d), lambda i, t: (i, 0, 0))],
        out_shape=[jax.ShapeDtypeStruct((n_proj, b * p, d), BF16),
                   jax.ShapeDtypeStruct((b * p, lr), BF16), jax.ShapeDtypeStruct((b * p, lr), BF16),
                   jax.ShapeDtypeStruct((b, 1, d), F32)],
        scratch_shapes=[pltpu.VMEM((8, d), F32)],
        compiler_params=_cparams(("arbitrary", "arbitrary")),
        name="norm_shift_prompt",
    )(x, g.reshape(1, d), mu, w1, a1)


def _norm_shift_sample_kernel(x_ref, prev_ref, g_ref, mu_ref, w1_ref, a1_ref, xm_ref, hw_ref, ha_ref, xn_ref):
    x = x_ref[...]
    xn = x * lax.rsqrt(jnp.mean(x * x, axis=-1, keepdims=True) + RMS_EPS) * g_ref[...]
    xn_ref[...] = xn
    _mixes(xn, prev_ref[...], mu_ref, w1_ref, a1_ref, xm_ref, hw_ref, ha_ref)


def norm_shift_sample(x, prev, g, mu, w1, a1):
    m, d = x.shape
    lr = w1.shape[1]
    return pl.pallas_call(
        _norm_shift_sample_kernel,
        out_shape=[jax.ShapeDtypeStruct((mu.shape[0] - 2, m, d), BF16),
                   jax.ShapeDtypeStruct((m, lr), BF16), jax.ShapeDtypeStruct((m, lr), BF16),
                   jax.ShapeDtypeStruct((m, d), F32)],
        name="norm_shift_sample",
    )(x, prev, g.reshape(1, d), mu, w1, a1)


def _rope(y, cos, sin_a, sin_b):
    half = ROPE_DIM // 2
    step = ROPE_SLAB
    rep = step // cos.shape[1]
    tile = lambda t: jnp.concatenate([t] * rep, axis=1)
    cos_t, sa_t, sb_t = tile(cos), tile(sin_a), tile(sin_b)
    outs = []
    for j in range(y.shape[1] // step):
        ys = y[:, j * step:(j + 1) * step]
        outs.append(ys * cos_t + pltpu.roll(ys, step - half, axis=1) * sa_t + pltpu.roll(ys, half, axis=1) * sb_t)
    return jnp.concatenate(outs, axis=1) if len(outs) > 1 else outs[0]


def _mm_group_kernel(x_ref, w_ref, o_ref):
    o_ref[0] = _dot(x_ref[0], w_ref[0]).astype(o_ref.dtype)


def matmul_groups(x, w, tm, out_dtype):
    g, _, n = w.shape
    _, m, kdim = x.shape
    return pl.pallas_call(
        _mm_group_kernel,
        grid=(g, m // tm),
        in_specs=[pl.BlockSpec((1, tm, kdim), lambda q, i: (q, i, 0)),
                  pl.BlockSpec((1, kdim, n), lambda q, i: (q, 0, 0))],
        out_specs=pl.BlockSpec((1, tm, n), lambda q, i: (q, i, 0)),
        out_shape=jax.ShapeDtypeStruct((g, m, n), out_dtype),
        compiler_params=_cparams(("arbitrary", "arbitrary")),
        name="matmul_groups",
    )(x, w)


def _mm_rope_kernel(x_ref, w_ref, cos_ref, sa_ref, sb_ref, *o_refs, n_rope, scale):
    y = _dot(x_ref[...], w_ref[...])
    rot = _rope(y[:, :n_rope], cos_ref[...], sa_ref[...], sb_ref[...])
    if scale != 1.0:
        rot = rot * scale
    o_refs[0][...] = rot.astype(o_refs[0].dtype)
    if len(o_refs) > 1:
        o_refs[1][...] = y[:, n_rope:].astype(o_refs[1].dtype)


def matmul_rope(x, w, tables, tm, n_rope, out_dtypes, scale=1.0):
    m, kdim = x.shape
    n = w.shape[1]
    lanes = tables[0].shape[1]
    widths = [n_rope] + ([n - n_rope] if n > n_rope else [])
    tab = pl.BlockSpec((tm, lanes), lambda i: (i, 0))
    outs = pl.pallas_call(
        functools.partial(_mm_rope_kernel, n_rope=n_rope, scale=scale),
        grid=(m // tm,),
        in_specs=[pl.BlockSpec((tm, kdim), lambda i: (i, 0)),
                  pl.BlockSpec((kdim, n), lambda i: (0, 0), pipeline_mode=pl.Buffered(1)),
                  tab, tab, tab],
        out_specs=[pl.BlockSpec((tm, wd), lambda i: (i, 0)) for wd in widths],
        out_shape=[jax.ShapeDtypeStruct((m, wd), dt) for wd, dt in zip(widths, out_dtypes)],
        compiler_params=_cparams(("arbitrary",)),
        name="matmul_rope",
    )(x, w, *tables)
    return outs


def _mm_res_norm_kernel(x_ref, w_ref, res_ref, g_ref, *out_refs, emit_h):
    h = res_ref[...] + _dot(x_ref[...], w_ref[...])
    hn_refs = out_refs
    if emit_h:
        out_refs[0][...] = h
        hn_refs = out_refs[1:]
    inv = lax.rsqrt(jnp.mean(h * h, axis=-1, keepdims=True) + RMS_EPS)
    for j, hn_ref in enumerate(hn_refs):
        hn_ref[...] = (h * inv * g_ref[j:j + 1, :]).astype(hn_ref.dtype)


def matmul_residual_norm(x, w, res, gains, tm, norm_dtype, emit_h=True):
    m, kdim = x.shape
    n = w.shape[1]
    ng = gains.shape[0]
    row = lambda width: pl.BlockSpec((tm, width), lambda i: (i, 0))
    return pl.pallas_call(
        functools.partial(_mm_res_norm_kernel, emit_h=emit_h),
        grid=(m // tm,),
        in_specs=[row(kdim),
                  pl.BlockSpec((kdim, n), lambda i: (0, 0), pipeline_mode=pl.Buffered(1)),
                  row(n),
                  pl.BlockSpec((ng, n), lambda i: (0, 0))],
        out_specs=[row(n)] * (int(emit_h) + ng),
        out_shape=[jax.ShapeDtypeStruct((m, n), F32)] * int(emit_h) + [jax.ShapeDtypeStruct((m, n), norm_dtype)] * ng,
        compiler_params=_cparams(("arbitrary",)),
        name="matmul_residual_norm",
    )(x, w, res, gains)


def _lora_up_kernel(hw_ref, ha_ref, w2_ref, w0_ref, a2_ref, a0_ref, wl_ref, al_ref):
    wl_ref[...] = w0_ref[...] + _dot(hw_ref[...], w2_ref[...])
    al_ref[...] = a0_ref[...] + _dot(ha_ref[...], a2_ref[...])


def lora_up(hw, ha, w2, w0, a2, a0):
    m = hw.shape[0]
    e = w2.shape[1]
    return pl.pallas_call(
        _lora_up_kernel,
        out_shape=[jax.ShapeDtypeStruct((m, e), F32), jax.ShapeDtypeStruct((m, e), F32)],
        name="lora_up",
    )(hw, ha, w2, w0.reshape(1, e), a2, a0.reshape(1, e))


def _seg_sum(x, ones_bd):
    hi = x.astype(BF16)
    outs = []
    for c in range(x.shape[1] // MXU_TILE):
        sl = slice(c * MXU_TILE, (c + 1) * MXU_TILE)
        outs.append(_dot(hi[:, sl], ones_bd))
    return jnp.concatenate(outs, axis=1) if len(outs) > 1 else outs[0]


def _wkv_prompt_kernel(r_ref, k_ref, v_ref, z_ref, hw_ref, ha_ref, w2_ref, w0_ref, a2_ref, a0_ref,
                       kk_ref, ka_ref, rk_ref, gg_ref, gb_ref, yg_ref, sout_ref, s_ref):
    t_idx = pl.program_id(2)
    c = CHUNK
    hd = HEAD_DIM

    @pl.when(t_idx == 0)
    def _():
        s_ref[...] = jnp.zeros_like(s_ref)

    tb = r_ref.shape[1]
    nh = HEADS_PER_STREAM
    hw = nh * hd
    nc = tb // c

    li = lax.broadcasted_iota(jnp.int32, (MXU_TILE, MXU_TILE), 0) // hd
    lj = lax.broadcasted_iota(jnp.int32, (MXU_TILE, MXU_TILE), 1) // hd
    ones_bd = jnp.where(li == lj, 1.0, 0.0).astype(BF16)
    bi_ = lax.broadcasted_iota(jnp.int32, (tb, tb), 0)
    bj_ = lax.broadcasted_iota(jnp.int32, (tb, tb), 1)
    tri_incl = jnp.where((bj_ <= bi_) & (bj_ // c == bi_ // c), 1.0, 0.0).astype(BF16)
    ti = lax.broadcasted_iota(jnp.int32, (c, c), 0)
    tj = lax.broadcasted_iota(jnp.int32, (c, c), 1)
    ai = lax.broadcasted_iota(jnp.int32, (c, 2 * c), 0)
    aj = lax.broadcasted_iota(jnp.int32, (c, 2 * c), 1)
    upper = aj >= c
    aj_mod = jnp.where(upper, aj - c, aj)
    masks = dict(
        strict=tj < ti,
        eye=jnp.where(ti == tj, 1.0, 0.0).astype(F32),
        top_k=upper & (aj_mod < ai),
        bot=aj_mod <= ai)

    for st in range(r_ref.shape[2] // hw):
        _wkv_stream(st, hw, nc, r_ref, k_ref, v_ref, z_ref, hw_ref, ha_ref, w2_ref, w0_ref, a2_ref, a0_ref,
                    kk_ref, ka_ref, rk_ref, gg_ref, gb_ref, yg_ref, s_ref, ones_bd, tri_incl, masks)

    @pl.when(t_idx == pl.num_programs(2) - 1)
    def _():
        sout_ref[0] = s_ref[...]


def _wkv_stream(st, hw, nc, r_ref, k_ref, v_ref, z_ref, hw_ref, ha_ref, w2_ref, w0_ref, a2_ref, a0_ref,
                kk_ref, ka_ref, rk_ref, gg_ref, gb_ref, yg_ref, s_ref, ones_bd, tri_incl, masks):
    c = CHUNK
    hd = HEAD_DIM
    nh = hw // hd
    ls = slice(st * hw, (st + 1) * hw)
    h0 = st * nh
    r = r_ref[0, :, ls]
    k = k_ref[0, :, ls]
    v = v_ref[0, :, ls]
    wl = w0_ref[:, ls] + _dot(hw_ref[...], w2_ref[:, ls])
    al = a0_ref[:, ls] + _dot(ha_ref[...], a2_ref[:, ls])
    a = _sigmoid(al)
    lw = -DECAY_SCALE * _sigmoid(wl)
    kk = k * kk_ref[:, ls]
    n2 = _seg_sum(kk * kk, ones_bd)
    kk = kk / jnp.maximum(jnp.sqrt(n2), 1e-12)
    k2 = k * (1.0 + (a - 1.0) * ka_ref[:, ls])
    bb = kk * a

    lw_hi, lw_lo = _split_hi_lo(lw)
    g = _dot(tri_incl, lw_hi) + _dot(tri_incl, lw_lo)
    mid = lambda ci: g[ci * c + c // 2 - 1:ci * c + c // 2, :]
    gm = jnp.concatenate([jnp.broadcast_to(mid(ci), (c, hw)) for ci in range(nc)], axis=0)
    e_a = jnp.exp(g - gm)
    e_prev = jnp.exp(g - lw - gm)
    e_inv = jnp.exp(gm - g)
    e1 = [jnp.exp(mid(ci)) for ci in range(nc)]
    e2 = [jnp.exp(g[ci * c + c - 1:ci * c + c, :] - mid(ci)) for ci in range(nc)]

    kkd = (kk * e_prev).astype(BF16)
    rd = (r * e_a).astype(BF16)
    bi = (bb * e_inv).astype(BF16)
    ki = (k2 * e_inv).astype(BF16)
    v_bf = v.astype(BF16)
    zeros_cv = jnp.zeros((c, hd), BF16)

    pairs = [(ci, h) for ci in range(nc) for h in range(nh)]
    rows = lambda ci: slice(ci * c, (ci + 1) * c)
    cols = lambda h: slice(h * hd, (h + 1) * hd)
    xs = {(ci, h): jnp.concatenate([kkd[rows(ci), cols(h)], rd[rows(ci), cols(h)]], axis=0) for ci, h in pairs}
    r1s = {(ci, h): jnp.concatenate([bi[rows(ci), cols(h)], ki[rows(ci), cols(h)]], axis=0) for ci, h in pairs}
    vs = {(ci, h): v_bf[rows(ci), cols(h)] for ci, h in pairs}
    a_mats = {p: _dot_nt(xs[p], r1s[p]) for p in pairs}
    lk_vs = {p: _dot(jnp.where(masks["top_k"], a_mats[p][:c, :], 0.0).astype(BF16),
                     jnp.concatenate([zeros_cv, vs[p]], axis=0)) for p in pairs}
    lps = {p: jnp.where(masks["strict"], a_mats[p][:c, :c], 0.0) for p in pairs}
    ts = {p: masks["eye"] - lps[p] for p in pairs}
    for _ in range(int(math.log2(c)) - 1):
        lpb = {p: lps[p].astype(BF16) for p in pairs}
        lps = {p: _dot(lpb[p], lpb[p]) for p in pairs}
        ts = {p: _dot(ts[p].astype(BF16), (masks["eye"] + lps[p]).astype(BF16)) for p in pairs}
    a_bots = {p: jnp.where(masks["bot"], a_mats[p][c:, :], 0.0).astype(BF16) for p in pairs}
    t_bf = {p: ts[p].astype(BF16) for p in pairs}

    state = [s_ref[h0 + h] for h in range(nh)]
    y_rows = []
    for ci in range(nc):
        hs = range(nh)
        sms = [state[h] * e1[ci][:, cols(h)] for h in hs]
        p_mats = [_dot_nt(xs[ci, h], sms[h].astype(BF16)) for h in hs]
        us = [-_dot(t_bf[ci, h], (p_mats[h][:c, :] + lk_vs[ci, h]).astype(BF16)) for h in hs]
        uvs = [jnp.concatenate([us[h].astype(BF16), vs[ci, h]], axis=0) for h in hs]
        ys = [p_mats[h][c:, :] + _dot(a_bots[ci, h], uvs[h]) for h in hs]
        state = [(sms[h] + _dot_tn(uvs[h], r1s[ci, h])) * e2[ci][:, cols(h)] for h in hs]
        y_rows.append(jnp.concatenate(ys, axis=1))
    for h in range(nh):
        s_ref[h0 + h] = state[h]
    y = jnp.concatenate(y_rows, axis=0) if nc > 1 else y_rows[0]

    inv_n = 1.0 / hd
    mean = _seg_sum(y, ones_bd) * inv_n
    yc = y - mean
    var = _seg_sum(yc * yc, ones_bd) * inv_n
    yn = yc * lax.rsqrt(var + GN_EPS) * gg_ref[:, ls] + gb_ref[:, ls]
    bonus = _seg_sum(r * k2 * rk_ref[:, ls], ones_bd)
    yg_ref[:, ls] = ((yn + bonus * v) * _silu(z_ref[0, :, ls])).astype(BF16)


def wkv_prompt(rkvz, hw_act, ha_act, w2, w0, a2, a0, k_k, k_a, r_k, gn_g, gn_b, n_batch):
    _, m, e = rkvz.shape
    p = m // n_batch
    tb = WKV_ROWS
    nt = p // tb
    hw = HEADS_PER_STEP * HEAD_DIM
    nh = e // HEAD_DIM
    row = lambda i, g, t: (i * nt + t, g)
    proj = lambda q: pl.BlockSpec((1, tb, hw), lambda i, g, t: (q, i * nt + t, g))
    par = pl.BlockSpec((1, hw), lambda i, g, t: (0, g))
    lr = w2.shape[0]
    hid = pl.BlockSpec((tb, lr), lambda i, g, t: (i * nt + t, 0))
    up = pl.BlockSpec((lr, hw), lambda i, g, t: (0, g))
    return pl.pallas_call(
        _wkv_prompt_kernel,
        grid=(n_batch, e // hw, nt),
        in_specs=[proj(0), proj(1), proj(2), proj(3), hid, hid, up, par, up, par,
                  par, par, par, par, par],
        out_specs=[pl.BlockSpec((tb, hw), row),
                   pl.BlockSpec((1, HEADS_PER_STEP, HEAD_DIM, HEAD_DIM), lambda i, g, t: (i, g, 0, 0))],
        out_shape=[jax.ShapeDtypeStruct((m, e), BF16),
                   jax.ShapeDtypeStruct((n_batch, nh, HEAD_DIM, HEAD_DIM), F32)],
        scratch_shapes=[pltpu.VMEM((HEADS_PER_STEP, HEAD_DIM, HEAD_DIM), F32)],
        compiler_params=_cparams(("arbitrary", "arbitrary", "arbitrary")),
        name="wkv_prompt",
    )(rkvz, rkvz, rkvz, rkvz, hw_act, ha_act, w2, w0.reshape(1, e), a2, a0.reshape(1, e),
      k_k.reshape(1, e), k_a.reshape(1, e), r_k.reshape(1, e), gn_g.reshape(1, e), gn_b.reshape(1, e))


def _wkv_sample_kernel(r_ref, k_ref, v_ref, z_ref, wl_ref, al_ref, kk_ref, ka_ref, rk_ref, gg_ref, gb_ref,
                       s_ref, yg_ref, sout_ref, y_scr):
    hd = HEAD_DIM
    r = r_ref[0]
    k = k_ref[0]
    v = v_ref[0]
    a = _sigmoid(al_ref[0])
    d = jnp.exp(-DECAY_SCALE * _sigmoid(wl_ref[0]))
    kk = k * kk_ref[...]
    kk = kk / jnp.maximum(jnp.sqrt(jnp.sum(kk * kk, axis=-1, keepdims=True)), 1e-12)
    k2 = k * (1.0 + (a - 1.0) * ka_ref[...])
    bb = kk * a
    nh = r.shape[0]
    ii = lax.broadcasted_iota(jnp.int32, (hd, hd), 0)
    jj = lax.broadcasted_iota(jnp.int32, (hd, hd), 1)
    eye = ii == jj

    row = lambda x, h: x[h:h + 1, :]
    group = 8
    for h0 in range(0, nh, group):
        hs = range(h0, h0 + group)
        s = {h: s_ref[0, h] for h in hs}
        sa = {h: jnp.sum(s[h] * row(kk, h), axis=-1, keepdims=True) for h in hs}
        v_col = {h: jnp.sum(jnp.where(eye, row(v, h), 0.0), axis=-1, keepdims=True) for h in hs}
        s_new = {h: s[h] * row(d, h) - sa[h] * row(bb, h) + v_col[h] * row(k2, h) for h in hs}
        y_col = {h: jnp.sum(s_new[h] * row(r, h), axis=-1, keepdims=True) for h in hs}
        for h in hs:
            sout_ref[0, h] = s_new[h]
            y_scr[h:h + 1, :] = jnp.sum(jnp.where(eye, y_col[h], 0.0), axis=0, keepdims=True)

    y = y_scr[...]
    mean = jnp.mean(y, axis=-1, keepdims=True)
    yc = y - mean
    var = jnp.mean(yc * yc, axis=-1, keepdims=True)
    yn = yc * lax.rsqrt(var + GN_EPS) * gg_ref[...] + gb_ref[...]
    bonus = jnp.sum(r * k2 * rk_ref[...], axis=-1, keepdims=True)
    yg_ref[0] = ((yn + bonus * v) * _silu(z_ref[0])).astype(BF16)


def wkv_sample(rkvz, wl, al, k_k, k_a, r_k, gn_g, gn_b, state):
    _, m, e = rkvz.shape
    nh = e // HEAD_DIM
    hd = HEAD_DIM
    rkvz4 = rkvz.reshape(4, m, nh, hd)
    proj = lambda q: pl.BlockSpec((None, 1, nh, hd), lambda i: (q, i, 0, 0))
    tok = pl.BlockSpec((1, nh, hd), lambda i: (i, 0, 0))
    par = pl.BlockSpec((nh, hd), lambda i: (0, 0))
    st = pl.BlockSpec((1, nh, hd, hd), lambda i: (i, 0, 0, 0))
    as_heads = lambda x: x.reshape(nh, hd)
    return pl.pallas_call(
        _wkv_sample_kernel,
        grid=(m,),
        in_specs=[proj(0), proj(1), proj(2), proj(3), tok, tok, par, par, par, par, par, st],
        out_specs=[tok, st],
        out_shape=[jax.ShapeDtypeStruct((m, nh, hd), BF16), jax.ShapeDtypeStruct(state.shape, F32)],
        scratch_shapes=[pltpu.VMEM((nh, hd), F32)],
        compiler_params=_cparams(("arbitrary",)),
        name="wkv_sample",
    )(rkvz4, rkvz4, rkvz4, rkvz4, wl.reshape(m, nh, hd), al.reshape(m, nh, hd),
      as_heads(k_k), as_heads(k_a), as_heads(r_k), as_heads(gn_g), as_heads(gn_b), state)


def rope_tables(pos):
    half = ROPE_DIM // 2
    inv_freq = ROPE_THETA ** (-jnp.arange(half, dtype=F32) * 2.0 / ROPE_DIM)
    ang = pos.astype(F32)[:, None] * inv_freq[None, :]
    cos = jnp.cos(ang)
    sin = jnp.sin(ang)
    rows = pos.shape[0]
    ones = jnp.ones((rows, HEAD_DIM - ROPE_DIM), F32)
    zeros_h = jnp.zeros((rows, half), F32)
    zeros_r = jnp.zeros((rows, HEAD_DIM - ROPE_DIM), F32)
    cos_h = jnp.concatenate([cos, cos, ones], axis=1)
    sa_h = jnp.concatenate([-sin, zeros_h, zeros_r], axis=1)
    sb_h = jnp.concatenate([zeros_h, sin, zeros_r], axis=1)
    two = lambda t: jnp.concatenate([t, t], axis=1)
    return two(cos_h), two(sa_h), two(sb_h)


def _attn_prompt_kernel(sink_ref, q_ref, kc_ref, kp_ref, vc_ref, vp_ref, z_ref, o_ref):
    n = pl.program_id(1)
    hd = HEAD_DIM
    blk = q_ref.shape[0]
    n_kv = kc_ref.shape[1] // hd
    grp = q_ref.shape[1] // (n_kv * hd)
    qi = lax.broadcasted_iota(jnp.int32, (blk, 2 * blk), 0)
    kj = lax.broadcasted_iota(jnp.int32, (blk, 2 * blk), 1) - blk
    kpos = n * blk + kj
    diff = qi - kj
    valid = (kpos >= LEAD) & (diff >= 0) & (diff <= WINDOW)
    k_all = jnp.concatenate([kp_ref[...], kc_ref[...]], axis=0).astype(BF16)
    v_all = jnp.concatenate([vp_ref[...], vc_ref[...]], axis=0).astype(BF16)

    def scores(h):
        k_h = k_all[:, h * hd:(h + 1) * hd]
        return [_dot_nt(q_ref[:, (h * grp + gi) * hd:(h * grp + gi + 1) * hd], k_h) for gi in range(grp)]

    outs = []
    s_next = scores(0)
    for h in range(n_kv):
        s_cur = s_next
        if h + 1 < n_kv:
            s_next = scores(h + 1)
        v_h = v_all[:, h * hd:(h + 1) * hd]
        for g0 in range(0, grp, SOFTMAX_BATCH):
            gs = range(g0, min(g0 + SOFTMAX_BATCH, grp))
            sks = {gi: sink_ref[h * grp + gi] for gi in gs}
            ss = {gi: jnp.where(valid, s_cur[gi], -jnp.inf) for gi in gs}
            ms = {gi: jnp.maximum(jnp.max(ss[gi], axis=-1, keepdims=True), sks[gi]) for gi in gs}
            ps = {gi: jnp.exp(ss[gi] - ms[gi]) for gi in gs}
            dens = {gi: jnp.sum(ps[gi], axis=-1, keepdims=True) + jnp.exp(sks[gi] - ms[gi]) for gi in gs}
            outs += [_dot(ps[gi].astype(BF16), v_h) / dens[gi] for gi in gs]
    att = jnp.concatenate(outs, axis=1)
    o_ref[...] = (att * _silu(z_ref[...])).astype(o_ref.dtype)


def attn_prompt(sinks, q, k, v, z, n_batch):
    m, e = q.shape
    nb = m // (n_batch * BLOCK)
    kw = k.shape[1]
    cur = lambda i, n: (i * nb + n, 0)
    prv = lambda i, n: (i * nb + jnp.maximum(n - 1, 0), 0)
    return pl.pallas_call(
        _attn_prompt_kernel,
        grid=(n_batch, nb),
        in_specs=[pl.BlockSpec(memory_space=pltpu.SMEM),
                  pl.BlockSpec((BLOCK, e), cur),
                  pl.BlockSpec((BLOCK, kw), cur), pl.BlockSpec((BLOCK, kw), prv),
                  pl.BlockSpec((BLOCK, kw), cur), pl.BlockSpec((BLOCK, kw), prv),
                  pl.BlockSpec((BLOCK, e), cur)],
        out_specs=pl.BlockSpec((BLOCK, e), cur),
        out_shape=jax.ShapeDtypeStruct((m, e), BF16),
        compiler_params=_cparams(("arbitrary", "arbitrary")),
        name="attn_prompt",
    )(sinks, q, k, k, v, v, z)


def _attn_sample_kernel(sink_ref, q_ref, kc_ref, vc_ref, kn_ref, vn_ref, z_ref, o_ref, ko_ref, vo_ref):
    hd = HEAD_DIM
    win, n_kv = kc_ref.shape[1], kc_ref.shape[2]
    nq = q_ref.shape[1]
    grp = nq // n_kv
    pad = 8
    first = lax.broadcasted_iota(jnp.int32, (pad, hd), 0) == 0
    last = lax.broadcasted_iota(jnp.int32, (win, hd), 0) == win - 1
    col = lax.broadcasted_iota(jnp.int32, (grp, win + pad), 1)
    valid = (col <= win) & (win - col <= WINDOW)
    q = q_ref[0].astype(BF16)
    row_i = lax.broadcasted_iota(jnp.int32, (grp, 1), 0)
    hs = range(n_kv)
    k_all, v_all, sks = [], [], []
    for h in hs:
        k_h, v_h = kc_ref[0, :, h, :], vc_ref[0, :, h, :]
        kn_h, vn_h = kn_ref[0, :, h, :], vn_ref[0, :, h, :]
        k_all.append(jnp.concatenate([k_h, jnp.where(first, kn_h, 0.0)], axis=0).astype(BF16))
        v_all.append(jnp.concatenate([v_h, jnp.where(first, vn_h, 0.0)], axis=0).astype(BF16))
        ko_ref[0, :, h, :] = jnp.where(last, kn_h, pltpu.roll(k_h, win - 1, axis=0))
        vo_ref[0, :, h, :] = jnp.where(last, vn_h, pltpu.roll(v_h, win - 1, axis=0))
        sk = jnp.zeros((grp, 1), F32)
        for gi in range(grp):
            sk = jnp.where(row_i == gi, sink_ref[h * grp + gi], sk)
        sks.append(sk)
    ss = [jnp.where(valid, _dot_nt(q[h * grp:(h + 1) * grp, :], k_all[h]), -jnp.inf) for h in hs]
    ms = [jnp.maximum(jnp.max(ss[h], axis=-1, keepdims=True), sks[h]) for h in hs]
    ps = [jnp.exp(ss[h] - ms[h]) for h in hs]
    dens = [jnp.sum(ps[h], axis=-1, keepdims=True) + jnp.exp(sks[h] - ms[h]) for h in hs]
    outs = [_dot(ps[h].astype(BF16), v_all[h]) / dens[h] for h in hs]
    att = jnp.concatenate(outs, axis=0)
    o_ref[0] = (att * _silu(z_ref[0])).astype(o_ref.dtype)


def attn_sample(sinks, q, cache_k, cache_v, k_new, v_new, z):
    m, win, n_kv, hd = cache_k.shape
    nq = q.shape[1]
    tok = pl.BlockSpec((1, nq, hd), lambda i: (i, 0, 0))
    cache = pl.BlockSpec((1, win, n_kv, hd), lambda i: (i, 0, 0, 0))
    new = pl.BlockSpec((1, 1, n_kv, hd), lambda i: (i, 0, 0, 0))
    return pl.pallas_call(
        _attn_sample_kernel,
        grid=(m,),
        in_specs=[pl.BlockSpec(memory_space=pltpu.SMEM), tok, cache, cache, new, new, tok],
        out_specs=[tok, cache, cache],
        out_shape=[jax.ShapeDtypeStruct((m, nq, hd), BF16),
                   jax.ShapeDtypeStruct(cache_k.shape, F32), jax.ShapeDtypeStruct(cache_v.shape, F32)],
        compiler_params=_cparams(("arbitrary",)),
        name="attn_sample",
    )(sinks, q, cache_k, cache_v, k_new, v_new, z)


def _pad_lora(w_down, w_up):
    r = w_down.shape[1]
    return (jnp.pad(w_down, ((0, 0), (0, LORA_PAD - r))).astype(BF16),
            jnp.pad(w_up, ((0, LORA_PAD - r), (0, 0))).astype(BF16))


def kernel(x_prompt, x_sample, state_wkv, state_shift, cache_k, cache_v, meta_tokens, a_norm, a_mu, a_w_rkvz,
           a_w0, a_w1, a_w2, a_a0, a_a1, a_a2, a_k_k, a_k_a, a_r_k, a_gn_g, a_gn_b, a_w_out, kv_norm, w_kv,
           b_norm, b_w_qz, b_sinks, b_w_o, final_norm):
    nb, seq, d = x_prompt.shape
    db, dseq, _ = x_sample.shape
    assert dseq == 1 and a_norm.shape[0] == 1 and b_norm.shape[0] == 1
    e = a_w_rkvz.shape[3]
    win = cache_k.shape[1]
    p_len = LEAD + N_META + seq
    assert p_len % BLOCK == 0 and (LEAD + N_META) == BLOCK
    m_p = nb * p_len
    kvw = N_KV_HEADS * HEAD_DIM

    w_rkvz = a_w_rkvz[0].astype(BF16)
    w1, w2 = _pad_lora(a_w1[0], a_w2[0])
    a1, a2 = _pad_lora(a_a1[0], a_a2[0])
    w_out = a_w_out[0].astype(BF16)
    w_kv_bf = w_kv.astype(BF16)
    w_q = b_w_qz[0][:, :e].astype(BF16)
    w_z = b_w_qz[0][:, e:].astype(BF16)
    w_o = b_w_o[0].astype(BF16)
    mu = a_mu[0]
    sinks = b_sinks[0]
    gains_b = jnp.stack([kv_norm, b_norm[0]])

    tm = p_len // 8

    hp = jnp.concatenate([jnp.zeros((nb, LEAD, d), F32),
                          jnp.broadcast_to(meta_tokens[None], (nb, N_META, d)), x_prompt], axis=1)
    xm, hw_p, ha_p, x_last = norm_shift_prompt(hp, a_norm[0], mu, w1, a1, tm)
    p_state_shift = x_last.reshape(1, nb, d)
    rkvz = matmul_groups(xm, w_rkvz, tm, F32)
    yg, p_state = wkv_prompt(rkvz, hw_p, ha_p, w2, a_w0[0], a2, a_a0[0], a_k_k[0], a_k_a[0],
                             a_r_k[0].reshape(-1), a_gn_g[0], a_gn_b[0], nb)
    hp, hn_kv, hn_b = matmul_residual_norm(yg, w_out, hp.reshape(m_p, d), gains_b, tm, BF16)

    pos_p = jnp.maximum(jnp.arange(p_len, dtype=jnp.int32) - LEAD, 0)
    tabs_p = tuple(jnp.tile(t, (nb, 1)) for t in rope_tables(pos_p))
    k_p, v_p = matmul_rope(hn_kv, w_kv_bf, tabs_p, tm, kvw, (F32, F32))
    q_p, = matmul_rope(hn_b, w_q, tabs_p, tm, e, (BF16,), scale=Q_SCALE)
    z_p = matmul_groups(hn_b[None], w_z[None], tm, F32)[0]
    att = attn_prompt(sinks, q_p, k_p, v_p, z_p, nb)
    y_pad, = matmul_residual_norm(att, w_o, hp, final_norm[None], tm, F32, emit_h=False)
    y_prompt = y_pad.reshape(nb, p_len, d)[:, LEAD + N_META:]
    p_cache_k = k_p.reshape(nb, p_len, N_KV_HEADS, HEAD_DIM)[:, -win:]
    p_cache_v = v_p.reshape(nb, p_len, N_KV_HEADS, HEAD_DIM)[:, -win:]

    hs = x_sample.reshape(db, d)
    xm_s, hw_s, ha_s, xn_s = norm_shift_sample(hs, state_shift[0], a_norm[0], mu, w1, a1)
    rkvz_s = matmul_groups(xm_s, w_rkvz, db, F32)
    wl_s, al_s = lora_up(hw_s, ha_s, w2, a_w0[0], a2, a_a0[0])
    yg_s, s_state = wkv_sample(rkvz_s, wl_s, al_s, a_k_k[0], a_k_a[0], a_r_k[0].reshape(-1), a_gn_g[0],
                               a_gn_b[0], state_wkv[0])
    hs, hn_kv_s, hn_b_s = matmul_residual_norm(yg_s.reshape(db, e), w_out, hs, gains_b, db, BF16)
    tabs_s = rope_tables(jnp.full((db,), PAST_LEN, jnp.int32))
    k_s, v_s = matmul_rope(hn_kv_s, w_kv_bf, tabs_s, db, kvw, (F32, F32))
    q_s, = matmul_rope(hn_b_s, w_q, tabs_s, db, e, (F32,), scale=Q_SCALE)
    z_s = matmul_groups(hn_b_s[None], w_z[None], db, F32)[0]
    nq = e // HEAD_DIM
    att_s, s_cache_k, s_cache_v = attn_sample(
        sinks, q_s.reshape(db, nq, HEAD_DIM), cache_k, cache_v,
        k_s.reshape(db, 1, N_KV_HEADS, HEAD_DIM), v_s.reshape(db, 1, N_KV_HEADS, HEAD_DIM),
        z_s.reshape(db, nq, HEAD_DIM))
    y_s, = matmul_residual_norm(att_s.reshape(db, e), w_o, hs, final_norm[None], db, F32, emit_h=False)
    y_sample = y_s.reshape(db, 1, d)

    return (y_prompt, y_sample, p_state[None], p_state_shift,
            p_cache_k, p_cache_v,
            s_state[None], xn_s[None], s_cache_k, s_cache_v)
```

```python
import functools
import math

import jax
import jax.numpy as jnp
from jax import lax
from jax.experimental import pallas as pl
from jax.experimental.pallas import tpu as pltpu

F32 = jnp.float32
BF16 = jnp.bfloat16

HEAD_DIM = 64
N_KV_HEADS = 8
WINDOW = 128
BLOCK = 128
ROPE_DIM = HEAD_DIM // 4
ROPE_THETA = 500000.0
N_META = 16
PAST_LEN = 16384
RMS_EPS = 1e-6
GN_EPS = 64e-5
LEAD = (-N_META) % BLOCK
CHUNK = 64
WKV_ROWS = 128
HEADS_PER_STREAM = 32
HEADS_PER_STEP = 32
LORA_PAD = 128
MXU_TILE = 256
ROPE_SLAB = 512
VMEM_LIMIT = 48 * 1024 * 1024
DECAY_SCALE = math.exp(-0.5)
SOFTMAX_BATCH = 4
Q_SCALE = HEAD_DIM ** -0.5


def _cparams(sem):
    return pltpu.CompilerParams(dimension_semantics=sem, vmem_limit_bytes=VMEM_LIMIT)


def _sigmoid(x):
    return 1.0 / (1.0 + jnp.exp(-x))


def _silu(x):
    return x * _sigmoid(x)


def _dot(a, b):
    return jnp.dot(a, b, preferred_element_type=F32)


def _dot_nt(a, b):
    return lax.dot_general(a, b, (((1,), (1,)), ((), ())), preferred_element_type=F32)


def _dot_tn(a, b):
    return lax.dot_general(a, b, (((0,), (0,)), ((), ())), preferred_element_type=F32)


def _split_hi_lo(x):
    hi = x.astype(BF16)
    lo = (x - hi.astype(F32)).astype(BF16)
    return hi, lo


def _mixes(xn, prev, mu_ref, w1_ref, a1_ref, xm_ref, hw_ref, ha_ref):
    xx = prev - xn
    n_proj = xm_ref.shape[0]
    for p in range(n_proj):
        xm_ref[p] = (xn + xx * mu_ref[p:p + 1, :]).astype(xm_ref.dtype)
    xw = (xn + xx * mu_ref[n_proj:n_proj + 1, :]).astype(BF16)
    xa = (xn + xx * mu_ref[n_proj + 1:n_proj + 2, :]).astype(BF16)
    hw_ref[...] = jnp.tanh(_dot(xw, w1_ref[...])).astype(hw_ref.dtype)
    ha_ref[...] = _dot(xa, a1_ref[...]).astype(ha_ref.dtype)


def _padded_rows(x_ref, head_ref, first_block):
    return jnp.where(pl.program_id(1) + first_block == 0, head_ref[...], x_ref[0])


def _norm_shift_kernel(x_ref, head_ref, g_ref, mu_ref, w1_ref, a1_ref, xm_ref, hw_ref, ha_ref, last_ref, carry_ref):
    @pl.when(pl.program_id(1) == 0)
    def _():
        carry_ref[...] = jnp.zeros_like(carry_ref)

    x = _padded_rows(x_ref, head_ref, 0)
    tm = x.shape[0]
    xn = x * lax.rsqrt(jnp.mean(x * x, axis=-1, keepdims=True) + RMS_EPS) * g_ref[...]
    rolled = pltpu.roll(xn, 1, axis=0)
    row = lax.broadcasted_iota(jnp.int32, xn.shape, 0)
    prev = jnp.where(row == 0, carry_ref[0:1, :], rolled)
    _mixes(xn, prev, mu_ref, w1_ref, a1_ref, xm_ref, hw_ref, ha_ref)
    carry_ref[0:1, :] = xn[tm - 1:tm, :]
    last_ref[0] = xn[tm - 1:tm, :]


def norm_shift_prompt(x, head, g, mu, w1, a1):
    b, seq, d = x.shape
    tm = BLOCK
    p = tm + seq
    n_mix = mu.shape[0]
    n_proj = n_mix - 2
    lr = w1.shape[1]
    nt = p // tm
    const = lambda shape: pl.BlockSpec(shape, lambda i, t: (0,) * len(shape))
    hid = pl.BlockSpec((tm, lr), lambda i, t: (i * nt + t, 0))
    return pl.pallas_call(
        _norm_shift_kernel,
        grid=(b, nt),
        in_specs=[pl.BlockSpec((1, tm, d), lambda i, t: (i, jnp.maximum(t - 1, 0), 0)),
                  const((tm, d)), const((1, d)), const((n_mix, d)), const((d, lr)), const((d, lr))],
        out_specs=[pl.BlockSpec((n_proj, tm, d), lambda i, t: (0, i * nt + t, 0)), hid, hid,
                   pl.BlockSpec((1, 1, d), lambda i, t: (i, 0, 0))],
        out_shape=[jax.ShapeDtypeStruct((n_proj, b * p, d), BF16),
                   jax.ShapeDtypeStruct((b * p, lr), BF16), jax.ShapeDtypeStruct((b * p, lr), BF16),
                   jax.ShapeDtypeStruct((b, 1, d), F32)],
        scratch_shapes=[pltpu.VMEM((8, d), F32)],
        compiler_params=_cparams(("arbitrary", "arbitrary")),
        name="norm_shift_prompt",
    )(x, head, g.reshape(1, d), mu, w1, a1)


def _norm_shift_sample_kernel(x_ref, prev_ref, g_ref, mu_ref, w1_ref, a1_ref, xm_ref, hw_ref, ha_ref, xn_ref):
    x = x_ref[...]
    xn = x * lax.rsqrt(jnp.mean(x * x, axis=-1, keepdims=True) + RMS_EPS) * g_ref[...]
    xn_ref[...] = xn
    _mixes(xn, prev_ref[...], mu_ref, w1_ref, a1_ref, xm_ref, hw_ref, ha_ref)


def norm_shift_sample(x, prev, g, mu, w1, a1):
    m, d = x.shape
    lr = w1.shape[1]
    return pl.pallas_call(
        _norm_shift_sample_kernel,
        out_shape=[jax.ShapeDtypeStruct((mu.shape[0] - 2, m, d), BF16),
                   jax.ShapeDtypeStruct((m, lr), BF16), jax.ShapeDtypeStruct((m, lr), BF16),
                   jax.ShapeDtypeStruct((m, d), F32)],
        name="norm_shift_sample",
    )(x, prev, g.reshape(1, d), mu, w1, a1)


def _rope(y, cos, sin_a, sin_b):
    half = ROPE_DIM // 2
    step = ROPE_SLAB
    rep = step // cos.shape[1]
    tile = lambda t: jnp.concatenate([t] * rep, axis=1)
    cos_t, sa_t, sb_t = tile(cos), tile(sin_a), tile(sin_b)
    outs = []
    for j in range(y.shape[1] // step):
        ys = y[:, j * step:(j + 1) * step]
        outs.append(ys * cos_t + pltpu.roll(ys, step - half, axis=1) * sa_t + pltpu.roll(ys, half, axis=1) * sb_t)
    return jnp.concatenate(outs, axis=1) if len(outs) > 1 else outs[0]


def _mm_group_kernel(x_ref, w_ref, o_ref):
    o_ref[0] = _dot(x_ref[0], w_ref[0]).astype(o_ref.dtype)


def matmul_groups(x, w, tm, out_dtype):
    g, _, n = w.shape
    _, m, kdim = x.shape
    return pl.pallas_call(
        _mm_group_kernel,
        grid=(g, m // tm),
        in_specs=[pl.BlockSpec((1, tm, kdim), lambda q, i: (q, i, 0)),
                  pl.BlockSpec((1, kdim, n), lambda q, i: (q, 0, 0))],
        out_specs=pl.BlockSpec((1, tm, n), lambda q, i: (q, i, 0)),
        out_shape=jax.ShapeDtypeStruct((g, m, n), out_dtype),
        compiler_params=_cparams(("arbitrary", "arbitrary")),
        name="matmul_groups",
    )(x, w)


def _mm_rope_kernel(x_ref, w_ref, cos_ref, sa_ref, sb_ref, *o_refs, n_rope, scale):
    y = _dot(x_ref[...], w_ref[...])
    rot = _rope(y[:, :n_rope], cos_ref[...], sa_ref[...], sb_ref[...])
    if scale != 1.0:
        rot = rot * scale
    o_refs[0][...] = rot.astype(o_refs[0].dtype)
    if len(o_refs) > 1:
        o_refs[1][...] = y[:, n_rope:].astype(o_refs[1].dtype)


def matmul_rope(x, w, tables, tm, n_rope, out_dtypes, scale=1.0):
    m, kdim = x.shape
    n = w.shape[1]
    lanes = tables[0].shape[1]
    widths = [n_rope] + ([n - n_rope] if n > n_rope else [])
    tab = pl.BlockSpec((tm, lanes), lambda i: (i, 0))
    outs = pl.pallas_call(
        functools.partial(_mm_rope_kernel, n_rope=n_rope, scale=scale),
        grid=(m // tm,),
        in_specs=[pl.BlockSpec((tm, kdim), lambda i: (i, 0)),
                  pl.BlockSpec((kdim, n), lambda i: (0, 0), pipeline_mode=pl.Buffered(1)),
                  tab, tab, tab],
        out_specs=[pl.BlockSpec((tm, wd), lambda i: (i, 0)) for wd in widths],
        out_shape=[jax.ShapeDtypeStruct((m, wd), dt) for wd, dt in zip(widths, out_dtypes)],
        compiler_params=_cparams(("arbitrary",)),
        name="matmul_rope",
    )(x, w, *tables)
    return outs


def _mm_res_norm_kernel(x_ref, w_ref, res_ref, g_ref, *out_refs, emit_h):
    h = res_ref[...] + _dot(x_ref[...], w_ref[...])
    hn_refs = out_refs
    if emit_h:
        out_refs[0][...] = h
        hn_refs = out_refs[1:]
    inv = lax.rsqrt(jnp.mean(h * h, axis=-1, keepdims=True) + RMS_EPS)
    for j, hn_ref in enumerate(hn_refs):
        hn_ref[...] = (h * inv * g_ref[j:j + 1, :]).astype(hn_ref.dtype)


def matmul_residual_norm(x, w, res, gains, tm, norm_dtype, emit_h=True):
    m, kdim = x.shape
    n = w.shape[1]
    ng = gains.shape[0]
    row = lambda width: pl.BlockSpec((tm, width), lambda i: (i, 0))
    return pl.pallas_call(
        functools.partial(_mm_res_norm_kernel, emit_h=emit_h),
        grid=(m // tm,),
        in_specs=[row(kdim),
                  pl.BlockSpec((kdim, n), lambda i: (0, 0), pipeline_mode=pl.Buffered(1)),
                  row(n),
                  pl.BlockSpec((ng, n), lambda i: (0, 0))],
        out_specs=[row(n)] * (int(emit_h) + ng),
        out_shape=[jax.ShapeDtypeStruct((m, n), F32)] * int(emit_h) + [jax.ShapeDtypeStruct((m, n), norm_dtype)] * ng,
        compiler_params=_cparams(("arbitrary",)),
        name="matmul_residual_norm",
    )(x, w, res, gains)


def _mm_res_norm_blocks_kernel(x_ref, w_ref, res_ref, *rest, emit_h, head, first_block):
    if head:
        head_ref, g_ref, *out_refs = rest
        res = _padded_rows(res_ref, head_ref, first_block)
    else:
        g_ref, *out_refs = rest
        res = res_ref[...]
    h = res + _dot(x_ref[...], w_ref[...])
    hn_refs = out_refs
    if emit_h:
        out_refs[0][...] = h
        hn_refs = out_refs[1:]
    inv = lax.rsqrt(jnp.mean(h * h, axis=-1, keepdims=True) + RMS_EPS)
    for j, hn_ref in enumerate(hn_refs):
        hn_ref[...] = (h * inv * g_ref[j:j + 1, :]).astype(hn_ref.dtype)


def matmul_residual_norm_blocks(x, w, res, gains, n_batch, norm_dtype, emit_h, head=None, first_block=0):
    m, kdim = x.shape
    n = w.shape[1]
    ng = gains.shape[0]
    nb = m // (n_batch * BLOCK)
    nb_out = nb - first_block
    padded = lambda width: pl.BlockSpec((BLOCK, width), lambda i, t: (i * nb + t + first_block, 0))
    compact = lambda width: pl.BlockSpec((BLOCK, width), lambda i, t: (i * nb_out + t, 0))
    const = lambda shape: pl.BlockSpec(shape, lambda i, t: (0,) * len(shape))
    if head is not None:
        res_specs = [pl.BlockSpec((1, BLOCK, n), lambda i, t: (i, jnp.maximum(t + first_block - 1, 0), 0)),
                     const((BLOCK, n))]
        res_args = [res, head]
    else:
        res_specs, res_args = [padded(n)], [res]
    norm_spec = compact(n) if first_block else padded(n)
    norm_rows = n_batch * nb_out * BLOCK
    return pl.pallas_call(
        functools.partial(_mm_res_norm_blocks_kernel, emit_h=emit_h, head=head is not None, first_block=first_block),
        grid=(n_batch, nb_out),
        in_specs=[padded(kdim), pl.BlockSpec((kdim, n), lambda i, t: (0, 0), pipeline_mode=pl.Buffered(1))]
        + res_specs + [const((ng, n))],
        out_specs=[padded(n)] * int(emit_h) + [norm_spec] * ng,
        out_shape=[jax.ShapeDtypeStruct((m, n), F32)] * int(emit_h)
        + [jax.ShapeDtypeStruct((norm_rows, n), norm_dtype)] * ng,
        compiler_params=_cparams(("arbitrary", "arbitrary")),
        name="matmul_residual_norm_blocks",
    )(x, w, *res_args, gains)


def _lora_up_kernel(hw_ref, ha_ref, w2_ref, w0_ref, a2_ref, a0_ref, wl_ref, al_ref):
    wl_ref[...] = w0_ref[...] + _dot(hw_ref[...], w2_ref[...])
    al_ref[...] = a0_ref[...] + _dot(ha_ref[...], a2_ref[...])


def lora_up(hw, ha, w2, w0, a2, a0):
    m = hw.shape[0]
    e = w2.shape[1]
    return pl.pallas_call(
        _lora_up_kernel,
        out_shape=[jax.ShapeDtypeStruct((m, e), F32), jax.ShapeDtypeStruct((m, e), F32)],
        name="lora_up",
    )(hw, ha, w2, w0.reshape(1, e), a2, a0.reshape(1, e))


def _seg_sum(x, ones_bd):
    hi = x.astype(BF16)
    outs = []
    for c in range(x.shape[1] // MXU_TILE):
        sl = slice(c * MXU_TILE, (c + 1) * MXU_TILE)
        outs.append(_dot(hi[:, sl], ones_bd))
    return jnp.concatenate(outs, axis=1) if len(outs) > 1 else outs[0]


def _wkv_prompt_kernel(r_ref, k_ref, v_ref, z_ref, hw_ref, ha_ref, w2_ref, w0_ref, a2_ref, a0_ref,
                       kk_ref, ka_ref, rk_ref, gg_ref, gb_ref, yg_ref, sout_ref, s_ref):
    t_idx = pl.program_id(2)
    c = CHUNK
    hd = HEAD_DIM

    @pl.when(t_idx == 0)
    def _():
        s_ref[...] = jnp.zeros_like(s_ref)

    tb = r_ref.shape[1]
    nh = HEADS_PER_STREAM
    hw = nh * hd
    nc = tb // c

    li = lax.broadcasted_iota(jnp.int32, (MXU_TILE, MXU_TILE), 0) // hd
    lj = lax.broadcasted_iota(jnp.int32, (MXU_TILE, MXU_TILE), 1) // hd
    ones_bd = jnp.where(li == lj, 1.0, 0.0).astype(BF16)
    bi_ = lax.broadcasted_iota(jnp.int32, (tb, tb), 0)
    bj_ = lax.broadcasted_iota(jnp.int32, (tb, tb), 1)
    tri_incl = jnp.where((bj_ <= bi_) & (bj_ // c == bi_ // c), 1.0, 0.0).astype(BF16)
    ti = lax.broadcasted_iota(jnp.int32, (c, c), 0)
    tj = lax.broadcasted_iota(jnp.int32, (c, c), 1)
    ai = lax.broadcasted_iota(jnp.int32, (c, 2 * c), 0)
    aj = lax.broadcasted_iota(jnp.int32, (c, 2 * c), 1)
    upper = aj >= c
    aj_mod = jnp.where(upper, aj - c, aj)
    masks = dict(
        strict=tj < ti,
        eye=jnp.where(ti == tj, 1.0, 0.0).astype(F32),
        top_k=upper & (aj_mod < ai),
        bot=aj_mod <= ai)

    for st in range(r_ref.shape[2] // hw):
        _wkv_stream(st, hw, nc, r_ref, k_ref, v_ref, z_ref, hw_ref, ha_ref, w2_ref, w0_ref, a2_ref, a0_ref,
                    kk_ref, ka_ref, rk_ref, gg_ref, gb_ref, yg_ref, s_ref, ones_bd, tri_incl, masks)

    @pl.when(t_idx == pl.num_programs(2) - 1)
    def _():
        sout_ref[0] = s_ref[...]


def _wkv_stream(st, hw, nc, r_ref, k_ref, v_ref, z_ref, hw_ref, ha_ref, w2_ref, w0_ref, a2_ref, a0_ref,
                kk_ref, ka_ref, rk_ref, gg_ref, gb_ref, yg_ref, s_ref, ones_bd, tri_incl, masks):
    c = CHUNK
    hd = HEAD_DIM
    nh = hw // hd
    ls = slice(st * hw, (st + 1) * hw)
    h0 = st * nh
    r = r_ref[0, :, ls]
    k = k_ref[0, :, ls]
    v = v_ref[0, :, ls]
    wl = w0_ref[:, ls] + _dot(hw_ref[...], w2_ref[:, ls])
    al = a0_ref[:, ls] + _dot(ha_ref[...], a2_ref[:, ls])
    a = _sigmoid(al)
    lw = -DECAY_SCALE * _sigmoid(wl)
    kk = k * kk_ref[:, ls]
    n2 = _seg_sum(kk * kk, ones_bd)
    kk = kk / jnp.maximum(jnp.sqrt(n2), 1e-12)
    k2 = k * (1.0 + (a - 1.0) * ka_ref[:, ls])
    bb = kk * a

    lw_hi, lw_lo = _split_hi_lo(lw)
    g = _dot(tri_incl, lw_hi) + _dot(tri_incl, lw_lo)
    mid = lambda ci: g[ci * c + c // 2 - 1:ci * c + c // 2, :]
    gm = jnp.concatenate([jnp.broadcast_to(mid(ci), (c, hw)) for ci in range(nc)], axis=0)
    e_a = jnp.exp(g - gm)
    e_prev = jnp.exp(g - lw - gm)
    e_inv = jnp.exp(gm - g)
    e1 = [jnp.exp(mid(ci)) for ci in range(nc)]
    e2 = [jnp.exp(g[ci * c + c - 1:ci * c + c, :] - mid(ci)) for ci in range(nc)]

    kkd = (kk * e_prev).astype(BF16)
    rd = (r * e_a).astype(BF16)
    bi = (bb * e_inv).astype(BF16)
    ki = (k2 * e_inv).astype(BF16)
    v_bf = v.astype(BF16)
    zeros_cv = jnp.zeros((c, hd), BF16)

    pairs = [(ci, h) for ci in range(nc) for h in range(nh)]
    rows = lambda ci: slice(ci * c, (ci + 1) * c)
    cols = lambda h: slice(h * hd, (h + 1) * hd)
    xs = {(ci, h): jnp.concatenate([kkd[rows(ci), cols(h)], rd[rows(ci), cols(h)]], axis=0) for ci, h in pairs}
    r1s = {(ci, h): jnp.concatenate([bi[rows(ci), cols(h)], ki[rows(ci), cols(h)]], axis=0) for ci, h in pairs}
    vs = {(ci, h): v_bf[rows(ci), cols(h)] for ci, h in pairs}
    a_mats = {p: _dot_nt(xs[p], r1s[p]) for p in pairs}
    lk_vs = {p: _dot(jnp.where(masks["top_k"], a_mats[p][:c, :], 0.0).astype(BF16),
                     jnp.concatenate([zeros_cv, vs[p]], axis=0)) for p in pairs}
    lps = {p: jnp.where(masks["strict"], a_mats[p][:c, :c], 0.0) for p in pairs}
    ts = {p: masks["eye"] - lps[p] for p in pairs}
    for _ in range(int(math.log2(c)) - 1):
        lpb = {p: lps[p].astype(BF16) for p in pairs}
        lps = {p: _dot(lpb[p], lpb[p]) for p in pairs}
        ts = {p: _dot(ts[p].astype(BF16), (masks["eye"] + lps[p]).astype(BF16)) for p in pairs}
    a_bots = {p: jnp.where(masks["bot"], a_mats[p][c:, :], 0.0).astype(BF16) for p in pairs}
    t_bf = {p: ts[p].astype(BF16) for p in pairs}

    state = [s_ref[h0 + h] for h in range(nh)]
    y_rows = []
    for ci in range(nc):
        hs = range(nh)
        sms = [state[h] * e1[ci][:, cols(h)] for h in hs]
        p_mats = [_dot_nt(xs[ci, h], sms[h].astype(BF16)) for h in hs]
        us = [-_dot(t_bf[ci, h], (p_mats[h][:c, :] + lk_vs[ci, h]).astype(BF16)) for h in hs]
        uvs = [jnp.concatenate([us[h].astype(BF16), vs[ci, h]], axis=0) for h in hs]
        ys = [p_mats[h][c:, :] + _dot(a_bots[ci, h], uvs[h]) for h in hs]
        state = [(sms[h] + _dot_tn(uvs[h], r1s[ci, h])) * e2[ci][:, cols(h)] for h in hs]
        y_rows.append(jnp.concatenate(ys, axis=1))
    for h in range(nh):
        s_ref[h0 + h] = state[h]
    y = jnp.concatenate(y_rows, axis=0) if nc > 1 else y_rows[0]

    inv_n = 1.0 / hd
    mean = _seg_sum(y, ones_bd) * inv_n
    yc = y - mean
    var = _seg_sum(yc * yc, ones_bd) * inv_n
    yn = yc * lax.rsqrt(var + GN_EPS) * gg_ref[:, ls] + gb_ref[:, ls]
    bonus = _seg_sum(r * k2 * rk_ref[:, ls], ones_bd)
    yg_ref[:, ls] = ((yn + bonus * v) * _silu(z_ref[0, :, ls])).astype(BF16)


def wkv_prompt(rkvz, hw_act, ha_act, w2, w0, a2, a0, k_k, k_a, r_k, gn_g, gn_b, n_batch):
    _, m, e = rkvz.shape
    p = m // n_batch
    tb = WKV_ROWS
    nt = p // tb
    hw = HEADS_PER_STEP * HEAD_DIM
    nh = e // HEAD_DIM
    row = lambda i, g, t: (i * nt + t, g)
    proj = lambda q: pl.BlockSpec((1, tb, hw), lambda i, g, t: (q, i * nt + t, g))
    par = pl.BlockSpec((1, hw), lambda i, g, t: (0, g))
    lr = w2.shape[0]
    hid = pl.BlockSpec((tb, lr), lambda i, g, t: (i * nt + t, 0))
    up = pl.BlockSpec((lr, hw), lambda i, g, t: (0, g))
    return pl.pallas_call(
        _wkv_prompt_kernel,
        grid=(n_batch, e // hw, nt),
        in_specs=[proj(0), proj(1), proj(2), proj(3), hid, hid, up, par, up, par,
                  par, par, par, par, par],
        out_specs=[pl.BlockSpec((tb, hw), row),
                   pl.BlockSpec((1, HEADS_PER_STEP, HEAD_DIM, HEAD_DIM), lambda i, g, t: (i, g, 0, 0))],
        out_shape=[jax.ShapeDtypeStruct((m, e), BF16),
                   jax.ShapeDtypeStruct((n_batch, nh, HEAD_DIM, HEAD_DIM), F32)],
        scratch_shapes=[pltpu.VMEM((HEADS_PER_STEP, HEAD_DIM, HEAD_DIM), F32)],
        compiler_params=_cparams(("arbitrary", "arbitrary", "arbitrary")),
        name="wkv_prompt",
    )(rkvz, rkvz, rkvz, rkvz, hw_act, ha_act, w2, w0.reshape(1, e), a2, a0.reshape(1, e),
      k_k.reshape(1, e), k_a.reshape(1, e), r_k.reshape(1, e), gn_g.reshape(1, e), gn_b.reshape(1, e))


def _wkv_sample_kernel(r_ref, k_ref, v_ref, z_ref, wl_ref, al_ref, kk_ref, ka_ref, rk_ref, gg_ref, gb_ref,
                       s_ref, yg_ref, sout_ref, y_scr):
    hd = HEAD_DIM
    r = r_ref[0]
    k = k_ref[0]
    v = v_ref[0]
    a = _sigmoid(al_ref[0])
    d = jnp.exp(-DECAY_SCALE * _sigmoid(wl_ref[0]))
    kk = k * kk_ref[...]
    kk = kk / jnp.maximum(jnp.sqrt(jnp.sum(kk * kk, axis=-1, keepdims=True)), 1e-12)
    k2 = k * (1.0 + (a - 1.0) * ka_ref[...])
    bb = kk * a
    nh = r.shape[0]
    ii = lax.broadcasted_iota(jnp.int32, (hd, hd), 0)
    jj = lax.broadcasted_iota(jnp.int32, (hd, hd), 1)
    eye = ii == jj

    row = lambda x, h: x[h:h + 1, :]
    group = 8
    for h0 in range(0, nh, group):
        hs = range(h0, h0 + group)
        s = {h: s_ref[0, h] for h in hs}
        sa = {h: jnp.sum(s[h] * row(kk, h), axis=-1, keepdims=True) for h in hs}
        v_col = {h: jnp.sum(jnp.where(eye, row(v, h), 0.0), axis=-1, keepdims=True) for h in hs}
        s_new = {h: s[h] * row(d, h) - sa[h] * row(bb, h) + v_col[h] * row(k2, h) for h in hs}
        y_col = {h: jnp.sum(s_new[h] * row(r, h), axis=-1, keepdims=True) for h in hs}
        for h in hs:
            sout_ref[0, h] = s_new[h]
            y_scr[h:h + 1, :] = jnp.sum(jnp.where(eye, y_col[h], 0.0), axis=0, keepdims=True)

    y = y_scr[...]
    mean = jnp.mean(y, axis=-1, keepdims=True)
    yc = y - mean
    var = jnp.mean(yc * yc, axis=-1, keepdims=True)
    yn = yc * lax.rsqrt(var + GN_EPS) * gg_ref[...] + gb_ref[...]
    bonus = jnp.sum(r * k2 * rk_ref[...], axis=-1, keepdims=True)
    yg_ref[0] = ((yn + bonus * v) * _silu(z_ref[0])).astype(BF16)


def wkv_sample(rkvz, wl, al, k_k, k_a, r_k, gn_g, gn_b, state):
    _, m, e = rkvz.shape
    nh = e // HEAD_DIM
    hd = HEAD_DIM
    rkvz4 = rkvz.reshape(4, m, nh, hd)
    proj = lambda q: pl.BlockSpec((None, 1, nh, hd), lambda i: (q, i, 0, 0))
    tok = pl.BlockSpec((1, nh, hd), lambda i: (i, 0, 0))
    par = pl.BlockSpec((nh, hd), lambda i: (0, 0))
    st = pl.BlockSpec((1, nh, hd, hd), lambda i: (i, 0, 0, 0))
    as_heads = lambda x: x.reshape(nh, hd)
    return pl.pallas_call(
        _wkv_sample_kernel,
        grid=(m,),
        in_specs=[proj(0), proj(1), proj(2), proj(3), tok, tok, par, par, par, par, par, st],
        out_specs=[tok, st],
        out_shape=[jax.ShapeDtypeStruct((m, nh, hd), BF16), jax.ShapeDtypeStruct(state.shape, F32)],
        scratch_shapes=[pltpu.VMEM((nh, hd), F32)],
        compiler_params=_cparams(("arbitrary",)),
        name="wkv_sample",
    )(rkvz4, rkvz4, rkvz4, rkvz4, wl.reshape(m, nh, hd), al.reshape(m, nh, hd),
      as_heads(k_k), as_heads(k_a), as_heads(r_k), as_heads(gn_g), as_heads(gn_b), state)


def rope_tables(pos):
    half = ROPE_DIM // 2
    inv_freq = ROPE_THETA ** (-jnp.arange(half, dtype=F32) * 2.0 / ROPE_DIM)
    ang = pos.astype(F32)[:, None] * inv_freq[None, :]
    cos = jnp.cos(ang)
    sin = jnp.sin(ang)
    rows = pos.shape[0]
    ones = jnp.ones((rows, HEAD_DIM - ROPE_DIM), F32)
    zeros_h = jnp.zeros((rows, half), F32)
    zeros_r = jnp.zeros((rows, HEAD_DIM - ROPE_DIM), F32)
    cos_h = jnp.concatenate([cos, cos, ones], axis=1)
    sa_h = jnp.concatenate([-sin, zeros_h, zeros_r], axis=1)
    sb_h = jnp.concatenate([zeros_h, sin, zeros_r], axis=1)
    two = lambda t: jnp.concatenate([t, t], axis=1)
    return two(cos_h), two(sa_h), two(sb_h)


def _attn_prompt_kernel(sink_ref, q_ref, kc_ref, kp_ref, vc_ref, vp_ref, z_ref, o_ref):
    n = pl.program_id(1)
    hd = HEAD_DIM
    blk = q_ref.shape[0]
    n_kv = kc_ref.shape[1] // hd
    grp = q_ref.shape[1] // (n_kv * hd)
    qi = lax.broadcasted_iota(jnp.int32, (blk, 2 * blk), 0)
    kj = lax.broadcasted_iota(jnp.int32, (blk, 2 * blk), 1) - blk
    kpos = n * blk + kj
    diff = qi - kj
    valid = (kpos >= LEAD) & (diff >= 0) & (diff <= WINDOW)
    k_all = jnp.concatenate([kp_ref[...], kc_ref[...]], axis=0).astype(BF16)
    v_all = jnp.concatenate([vp_ref[...], vc_ref[...]], axis=0).astype(BF16)

    def scores(h):
        k_h = k_all[:, h * hd:(h + 1) * hd]
        return [_dot_nt(q_ref[:, (h * grp + gi) * hd:(h * grp + gi + 1) * hd], k_h) for gi in range(grp)]

    outs = []
    s_next = scores(0)
    for h in range(n_kv):
        s_cur = s_next
        if h + 1 < n_kv:
            s_next = scores(h + 1)
        v_h = v_all[:, h * hd:(h + 1) * hd]
        for g0 in range(0, grp, SOFTMAX_BATCH):
            gs = range(g0, min(g0 + SOFTMAX_BATCH, grp))
            sks = {gi: sink_ref[h * grp + gi] for gi in gs}
            ss = {gi: jnp.where(valid, s_cur[gi], -jnp.inf) for gi in gs}
            ms = {gi: jnp.maximum(jnp.max(ss[gi], axis=-1, keepdims=True), sks[gi]) for gi in gs}
            ps = {gi: jnp.exp(ss[gi] - ms[gi]) for gi in gs}
            dens = {gi: jnp.sum(ps[gi], axis=-1, keepdims=True) + jnp.exp(sks[gi] - ms[gi]) for gi in gs}
            outs += [_dot(ps[gi].astype(BF16), v_h) / dens[gi] for gi in gs]
    att = jnp.concatenate(outs, axis=1)
    o_ref[...] = (att * _silu(z_ref[...])).astype(o_ref.dtype)


def attn_prompt(sinks, q, k, v, z, n_batch):
    m, e = q.shape
    nb = m // (n_batch * BLOCK)
    kw = k.shape[1]
    cur = lambda i, n: (i * nb + n, 0)
    prv = lambda i, n: (i * nb + jnp.maximum(n - 1, 0), 0)
    return pl.pallas_call(
        _attn_prompt_kernel,
        grid=(n_batch, nb),
        in_specs=[pl.BlockSpec(memory_space=pltpu.SMEM),
                  pl.BlockSpec((BLOCK, e), cur),
                  pl.BlockSpec((BLOCK, kw), cur), pl.BlockSpec((BLOCK, kw), prv),
                  pl.BlockSpec((BLOCK, kw), cur), pl.BlockSpec((BLOCK, kw), prv),
                  pl.BlockSpec((BLOCK, e), cur)],
        out_specs=pl.BlockSpec((BLOCK, e), cur),
        out_shape=jax.ShapeDtypeStruct((m, e), BF16),
        compiler_params=_cparams(("arbitrary", "arbitrary")),
        name="attn_prompt",
    )(sinks, q, k, k, v, v, z)


def _attn_sample_kernel(sink_ref, q_ref, kc_ref, vc_ref, kn_ref, vn_ref, z_ref, o_ref, ko_ref, vo_ref):
    hd = HEAD_DIM
    win = kc_ref.shape[1]
    n_kv = kc_ref.shape[2] // hd
    nq = q_ref.shape[1]
    grp = nq // n_kv
    pad = 8
    kc = kc_ref[0]
    vc = vc_ref[0]
    kn = kn_ref[0]
    vn = vn_ref[0]
    first = lax.broadcasted_iota(jnp.int32, (pad, kc.shape[1]), 0) == 0
    k_all = jnp.concatenate([kc, jnp.where(first, kn, 0.0)], axis=0).astype(BF16)
    v_all = jnp.concatenate([vc, jnp.where(first, vn, 0.0)], axis=0).astype(BF16)
    col = lax.broadcasted_iota(jnp.int32, (grp, win + pad), 1)
    valid = (col <= win) & (win - col <= WINDOW)
    q = q_ref[0].astype(BF16)
    row_i = lax.broadcasted_iota(jnp.int32, (grp, 1), 0)
    hs = range(n_kv)
    sks = []
    for h in hs:
        sk = jnp.zeros((grp, 1), F32)
        for gi in range(grp):
            sk = jnp.where(row_i == gi, sink_ref[h * grp + gi], sk)
        sks.append(sk)
    ss = [jnp.where(valid, _dot_nt(q[h * grp:(h + 1) * grp, :], k_all[:, h * hd:(h + 1) * hd]), -jnp.inf)
          for h in hs]
    ms = [jnp.maximum(jnp.max(ss[h], axis=-1, keepdims=True), sks[h]) for h in hs]
    ps = [jnp.exp(ss[h] - ms[h]) for h in hs]
    dens = [jnp.sum(ps[h], axis=-1, keepdims=True) + jnp.exp(sks[h] - ms[h]) for h in hs]
    outs = [_dot(ps[h].astype(BF16), v_all[:, h * hd:(h + 1) * hd]) / dens[h] for h in hs]
    att = jnp.concatenate(outs, axis=0)
    o_ref[0] = (att * _silu(z_ref[0])).astype(o_ref.dtype)
    last = lax.broadcasted_iota(jnp.int32, kc.shape, 0) == win - 1
    ko_ref[0] = jnp.where(last, kn, pltpu.roll(kc, win - 1, axis=0))
    vo_ref[0] = jnp.where(last, vn, pltpu.roll(vc, win - 1, axis=0))


def attn_sample(sinks, q, cache_k, cache_v, k_new, v_new, z):
    m, win, kw = cache_k.shape
    nq = q.shape[1]
    hd = HEAD_DIM
    tok = pl.BlockSpec((1, nq, hd), lambda i: (i, 0, 0))
    cache = pl.BlockSpec((1, win, kw), lambda i: (i, 0, 0))
    new = pl.BlockSpec((1, 1, kw), lambda i: (i, 0, 0))
    return pl.pallas_call(
        _attn_sample_kernel,
        grid=(m,),
        in_specs=[pl.BlockSpec(memory_space=pltpu.SMEM), tok, cache, cache, new, new, tok],
        out_specs=[tok, cache, cache],
        out_shape=[jax.ShapeDtypeStruct((m, nq, hd), BF16),
                   jax.ShapeDtypeStruct(cache_k.shape, F32), jax.ShapeDtypeStruct(cache_v.shape, F32)],
        compiler_params=_cparams(("arbitrary",)),
        name="attn_sample",
    )(sinks, q, cache_k, cache_v, k_new, v_new, z)


def _pad_lora(w_down, w_up):
    r = w_down.shape[1]
    return (jnp.pad(w_down, ((0, 0), (0, LORA_PAD - r))).astype(BF16),
            jnp.pad(w_up, ((0, LORA_PAD - r), (0, 0))).astype(BF16))


def kernel(x_prompt, x_sample, state_wkv, state_shift, cache_k, cache_v, meta_tokens, a_norm, a_mu, a_w_rkvz,
           a_w0, a_w1, a_w2, a_a0, a_a1, a_a2, a_k_k, a_k_a, a_r_k, a_gn_g, a_gn_b, a_w_out, kv_norm, w_kv,
           b_norm, b_w_qz, b_sinks, b_w_o, final_norm):
    nb, seq, d = x_prompt.shape
    db, dseq, _ = x_sample.shape
    assert dseq == 1 and a_norm.shape[0] == 1 and b_norm.shape[0] == 1
    e = a_w_rkvz.shape[3]
    win = cache_k.shape[1]
    p_len = LEAD + N_META + seq
    assert p_len % BLOCK == 0 and (LEAD + N_META) == BLOCK
    kvw = N_KV_HEADS * HEAD_DIM

    w_rkvz = a_w_rkvz[0].astype(BF16)
    w1, w2 = _pad_lora(a_w1[0], a_w2[0])
    a1, a2 = _pad_lora(a_a1[0], a_a2[0])
    w_out = a_w_out[0].astype(BF16)
    w_kv_bf = w_kv.astype(BF16)
    w_q = b_w_qz[0][:, :e].astype(BF16)
    w_z = b_w_qz[0][:, e:].astype(BF16)
    w_o = b_w_o[0].astype(BF16)
    mu = a_mu[0]
    sinks = b_sinks[0]
    gains_b = jnp.stack([kv_norm, b_norm[0]])

    tm = p_len // 8

    head = jnp.concatenate([jnp.zeros((LEAD, d), F32), meta_tokens], axis=0)
    xm, hw_p, ha_p, x_last = norm_shift_prompt(x_prompt, head, a_norm[0], mu, w1, a1)
    p_state_shift = x_last.reshape(1, nb, d)
    rkvz = matmul_groups(xm, w_rkvz, tm, F32)
    yg, p_state = wkv_prompt(rkvz, hw_p, ha_p, w2, a_w0[0], a2, a_a0[0], a_k_k[0], a_k_a[0],
                             a_r_k[0].reshape(-1), a_gn_g[0], a_gn_b[0], nb)
    hp, hn_kv, hn_b = matmul_residual_norm_blocks(yg, w_out, x_prompt, gains_b, nb, BF16, True, head=head)

    pos_p = jnp.maximum(jnp.arange(p_len, dtype=jnp.int32) - LEAD, 0)
    tabs_p = tuple(jnp.tile(t, (nb, 1)) for t in rope_tables(pos_p))
    k_p, v_p = matmul_rope(hn_kv, w_kv_bf, tabs_p, tm, kvw, (F32, F32))
    q_p, = matmul_rope(hn_b, w_q, tabs_p, tm, e, (BF16,), scale=Q_SCALE)
    z_p = matmul_groups(hn_b[None], w_z[None], tm, F32)[0]
    att = attn_prompt(sinks, q_p, k_p, v_p, z_p, nb)
    y_prompt, = matmul_residual_norm_blocks(att, w_o, hp, final_norm[None], nb, F32, False,
                                            first_block=(LEAD + N_META) // BLOCK)
    y_prompt = y_prompt.reshape(nb, seq, d)
    tail = lambda t: t.reshape(nb, p_len, kvw)[:, -win:].reshape(nb, win, N_KV_HEADS, HEAD_DIM)
    p_cache_k = tail(k_p)
    p_cache_v = tail(v_p)

    hs = x_sample.reshape(db, d)
    xm_s, hw_s, ha_s, xn_s = norm_shift_sample(hs, state_shift[0], a_norm[0], mu, w1, a1)
    rkvz_s = matmul_groups(xm_s, w_rkvz, db, F32)
    wl_s, al_s = lora_up(hw_s, ha_s, w2, a_w0[0], a2, a_a0[0])
    yg_s, s_state = wkv_sample(rkvz_s, wl_s, al_s, a_k_k[0], a_k_a[0], a_r_k[0].reshape(-1), a_gn_g[0],
                               a_gn_b[0], state_wkv[0])
    hs, hn_kv_s, hn_b_s = matmul_residual_norm(yg_s.reshape(db, e), w_out, hs, gains_b, db, BF16)
    tabs_s = rope_tables(jnp.full((db,), PAST_LEN, jnp.int32))
    k_s, v_s = matmul_rope(hn_kv_s, w_kv_bf, tabs_s, db, kvw, (F32, F32))
    q_s, = matmul_rope(hn_b_s, w_q, tabs_s, db, e, (F32,), scale=Q_SCALE)
    z_s = matmul_groups(hn_b_s[None], w_z[None], db, F32)[0]
    nq = e // HEAD_DIM
    att_s, s_cache_k, s_cache_v = attn_sample(
        sinks, q_s.reshape(db, nq, HEAD_DIM), cache_k.reshape(db, win, kvw), cache_v.reshape(db, win, kvw),
        k_s.reshape(db, 1, kvw), v_s.reshape(db, 1, kvw), z_s.reshape(db, nq, HEAD_DIM))
    y_s, = matmul_residual_norm(att_s.reshape(db, e), w_o, hs, final_norm[None], db, F32, emit_h=False)
    y_sample = y_s.reshape(db, 1, d)

    return (y_prompt, y_sample, p_state[None], p_state_shift,
            p_cache_k, p_cache_v,
            s_state[None], xn_s[None],
            s_cache_k.reshape(cache_k.shape), s_cache_v.reshape(cache_v.shape))
```

```python
import functools
import math

import jax
import jax.numpy as jnp
from jax import lax
from jax.experimental import pallas as pl
from jax.experimental.pallas import tpu as pltpu

F32 = jnp.float32
BF16 = jnp.bfloat16

HEAD_DIM = 64
N_KV_HEADS = 8
WINDOW = 128
BLOCK = 128
ROPE_DIM = HEAD_DIM // 4
ROPE_THETA = 500000.0
N_META = 16
PAST_LEN = 16384
RMS_EPS = 1e-6
GN_EPS = 64e-5
LEAD = (-N_META) % BLOCK
CHUNK = 64
WKV_ROWS = 128
HEADS_PER_STREAM = 32
HEADS_PER_STEP = 32
LORA_PAD = 128
MXU_TILE = 256
ROPE_SLAB = 512
VMEM_LIMIT = 48 * 1024 * 1024
DECAY_SCALE = math.exp(-0.5)
SOFTMAX_BATCH = 4
Q_SCALE = HEAD_DIM ** -0.5


def _cparams(sem):
    return pltpu.CompilerParams(dimension_semantics=sem, vmem_limit_bytes=VMEM_LIMIT)


def _sigmoid(x):
    return 1.0 / (1.0 + jnp.exp(-x))


def _silu(x):
    return x * _sigmoid(x)


def _dot(a, b):
    return jnp.dot(a, b, preferred_element_type=F32)


def _dot_nt(a, b):
    return lax.dot_general(a, b, (((1,), (1,)), ((), ())), preferred_element_type=F32)


def _dot_tn(a, b):
    return lax.dot_general(a, b, (((0,), (0,)), ((), ())), preferred_element_type=F32)


def _split_hi_lo(x):
    hi = x.astype(BF16)
    lo = (x - hi.astype(F32)).astype(BF16)
    return hi, lo


def _mixes(xn, prev, mu_ref, w1_ref, a1_ref, xm_ref, hw_ref, ha_ref):
    xx = prev - xn
    n_proj = xm_ref.shape[0]
    for p in range(n_proj):
        xm_ref[p] = (xn + xx * mu_ref[p:p + 1, :]).astype(xm_ref.dtype)
    xw = (xn + xx * mu_ref[n_proj:n_proj + 1, :]).astype(BF16)
    xa = (xn + xx * mu_ref[n_proj + 1:n_proj + 2, :]).astype(BF16)
    hw_ref[...] = jnp.tanh(_dot(xw, w1_ref[...])).astype(hw_ref.dtype)
    ha_ref[...] = _dot(xa, a1_ref[...]).astype(ha_ref.dtype)


def _padded_rows(x_ref, head_ref, first_block):
    return jnp.where(pl.program_id(1) + first_block == 0, head_ref[...], x_ref[0])


def _norm_shift_kernel(x_ref, head_ref, g_ref, mu_ref, w1_ref, a1_ref, xm_ref, hw_ref, ha_ref, last_ref, carry_ref):
    @pl.when(pl.program_id(1) == 0)
    def _():
        carry_ref[...] = jnp.zeros_like(carry_ref)

    x = _padded_rows(x_ref, head_ref, 0)
    tm = x.shape[0]
    xn = x * lax.rsqrt(jnp.mean(x * x, axis=-1, keepdims=True) + RMS_EPS) * g_ref[...]
    rolled = pltpu.roll(xn, 1, axis=0)
    row = lax.broadcasted_iota(jnp.int32, xn.shape, 0)
    prev = jnp.where(row == 0, carry_ref[0:1, :], rolled)
    _mixes(xn, prev, mu_ref, w1_ref, a1_ref, xm_ref, hw_ref, ha_ref)
    carry_ref[0:1, :] = xn[tm - 1:tm, :]
    last_ref[0] = xn[tm - 1:tm, :]


def norm_shift_prompt(x, head, g, mu, w1, a1):
    b, seq, d = x.shape
    tm = BLOCK
    p = tm + seq
    n_mix = mu.shape[0]
    n_proj = n_mix - 2
    lr = w1.shape[1]
    nt = p // tm
    const = lambda shape: pl.BlockSpec(shape, lambda i, t: (0,) * len(shape))
    hid = pl.BlockSpec((tm, lr), lambda i, t: (i * nt + t, 0))
    return pl.pallas_call(
        _norm_shift_kernel,
        grid=(b, nt),
        in_specs=[pl.BlockSpec((1, tm, d), lambda i, t: (i, jnp.maximum(t - 1, 0), 0)),
                  const((tm, d)), const((1, d)), const((n_mix, d)), const((d, lr)), const((d, lr))],
        out_specs=[pl.BlockSpec((n_proj, tm, d), lambda i, t: (0, i * nt + t, 0)), hid, hid,
                   pl.BlockSpec((1, 1, d), lambda i, t: (i, 0, 0))],
        out_shape=[jax.ShapeDtypeStruct((n_proj, b * p, d), BF16),
                   jax.ShapeDtypeStruct((b * p, lr), BF16), jax.ShapeDtypeStruct((b * p, lr), BF16),
                   jax.ShapeDtypeStruct((b, 1, d), F32)],
        scratch_shapes=[pltpu.VMEM((8, d), F32)],
        compiler_params=_cparams(("arbitrary", "arbitrary")),
        name="norm_shift_prompt",
    )(x, head, g.reshape(1, d), mu, w1, a1)


def _norm_shift_sample_kernel(x_ref, prev_ref, g_ref, mu_ref, w1_ref, a1_ref, xm_ref, hw_ref, ha_ref, xn_ref):
    x = x_ref[...]
    xn = x * lax.rsqrt(jnp.mean(x * x, axis=-1, keepdims=True) + RMS_EPS) * g_ref[...]
    xn_ref[...] = xn
    _mixes(xn, prev_ref[...], mu_ref, w1_ref, a1_ref, xm_ref, hw_ref, ha_ref)


def norm_shift_sample(x, prev, g, mu, w1, a1):
    m, d = x.shape
    lr = w1.shape[1]
    return pl.pallas_call(
        _norm_shift_sample_kernel,
        out_shape=[jax.ShapeDtypeStruct((mu.shape[0] - 2, m, d), BF16),
                   jax.ShapeDtypeStruct((m, lr), BF16), jax.ShapeDtypeStruct((m, lr), BF16),
                   jax.ShapeDtypeStruct((m, d), F32)],
        name="norm_shift_sample",
    )(x, prev, g.reshape(1, d), mu, w1, a1)


def _rope(y, cos, sin_a, sin_b):
    half = ROPE_DIM // 2
    step = ROPE_SLAB
    rep = step // cos.shape[1]
    tile = lambda t: jnp.concatenate([t] * rep, axis=1)
    cos_t, sa_t, sb_t = tile(cos), tile(sin_a), tile(sin_b)
    outs = []
    for j in range(y.shape[1] // step):
        ys = y[:, j * step:(j + 1) * step]
        outs.append(ys * cos_t + pltpu.roll(ys, step - half, axis=1) * sa_t + pltpu.roll(ys, half, axis=1) * sb_t)
    return jnp.concatenate(outs, axis=1) if len(outs) > 1 else outs[0]


def _mm_group_kernel(x_ref, w_ref, o_ref):
    o_ref[0] = _dot(x_ref[0], w_ref[0]).astype(o_ref.dtype)


def matmul_groups(x, w, tm, out_dtype, n=None, col=0):
    g = w.shape[0]
    n = w.shape[2] if n is None else n
    _, m, kdim = x.shape
    return pl.pallas_call(
        _mm_group_kernel,
        grid=(g, m // tm),
        in_specs=[pl.BlockSpec((1, tm, kdim), lambda q, i: (q, i, 0)),
                  pl.BlockSpec((1, kdim, n), lambda q, i: (q, 0, col))],
        out_specs=pl.BlockSpec((1, tm, n), lambda q, i: (q, i, 0)),
        out_shape=jax.ShapeDtypeStruct((g, m, n), out_dtype),
        compiler_params=_cparams(("arbitrary", "arbitrary")),
        name="matmul_groups",
    )(x, w)


def _mm_cast_kernel(x_ref, w_ref, *rest, rope, scale):
    if rope:
        cos_ref, sa_ref, sb_ref, o_ref, wb_ref = rest
    else:
        o_ref, wb_ref = rest

    @pl.when(pl.program_id(2) == 0)
    def _():
        wb_ref[0] = w_ref[0].astype(BF16)

    y = _dot(x_ref[0], wb_ref[0])
    if rope:
        y = _rope(y, cos_ref[...], sa_ref[...], sb_ref[...])
        if scale != 1.0:
            y = y * scale
    o_ref[0] = y.astype(o_ref.dtype)


def matmul_cast_weights(x, w, tm, tn, out_dtype, col0=0, n_cols=None, rope_tables=None, scale=1.0):
    g, kdim, n_total = w.shape
    n_cols = n_total // tn - col0 if n_cols is None else n_cols
    m = x.shape[1]
    in_specs = [pl.BlockSpec((1, tm, kdim), lambda q, c, i: (q, i, 0)),
                pl.BlockSpec((1, kdim, tn), lambda q, c, i: (q, 0, col0 + c))]
    args = [x, w]
    if rope_tables is not None:
        lanes = rope_tables[0].shape[1]
        in_specs += [pl.BlockSpec((tm, lanes), lambda q, c, i: (i, 0))] * 3
        args += list(rope_tables)
    return pl.pallas_call(
        functools.partial(_mm_cast_kernel, rope=rope_tables is not None, scale=scale),
        grid=(g, n_cols, m // tm),
        in_specs=in_specs,
        out_specs=[pl.BlockSpec((1, tm, tn), lambda q, c, i: (q, i, c)),
                   pl.BlockSpec((1, kdim, tn), lambda q, c, i: (q, 0, c))],
        out_shape=[jax.ShapeDtypeStruct((g, m, n_cols * tn), out_dtype),
                   jax.ShapeDtypeStruct((g, kdim, n_cols * tn), BF16)],
        compiler_params=_cparams(("arbitrary", "arbitrary", "arbitrary")),
        name="matmul_cast_weights",
    )(*args)


def _mm_rope_kernel(x_ref, w_ref, cos_ref, sa_ref, sb_ref, *o_refs, n_rope, scale):
    y = _dot(x_ref[...], w_ref[...])
    rot = _rope(y[:, :n_rope], cos_ref[...], sa_ref[...], sb_ref[...])
    if scale != 1.0:
        rot = rot * scale
    o_refs[0][...] = rot.astype(o_refs[0].dtype)
    if len(o_refs) > 1:
        o_refs[1][...] = y[:, n_rope:].astype(o_refs[1].dtype)


def matmul_rope(x, w, tables, tm, n_rope, out_dtypes, scale=1.0, n=None):
    m, kdim = x.shape
    n = w.shape[1] if n is None else n
    lanes = tables[0].shape[1]
    widths = [n_rope] + ([n - n_rope] if n > n_rope else [])
    tab = pl.BlockSpec((tm, lanes), lambda i: (i, 0))
    outs = pl.pallas_call(
        functools.partial(_mm_rope_kernel, n_rope=n_rope, scale=scale),
        grid=(m // tm,),
        in_specs=[pl.BlockSpec((tm, kdim), lambda i: (i, 0)),
                  pl.BlockSpec((kdim, n), lambda i: (0, 0), pipeline_mode=pl.Buffered(1)),
                  tab, tab, tab],
        out_specs=[pl.BlockSpec((tm, wd), lambda i: (i, 0)) for wd in widths],
        out_shape=[jax.ShapeDtypeStruct((m, wd), dt) for wd, dt in zip(widths, out_dtypes)],
        compiler_params=_cparams(("arbitrary",)),
        name="matmul_rope",
    )(x, w, *tables)
    return outs


def _mm_res_norm_kernel(x_ref, w_ref, res_ref, g_ref, *out_refs, emit_h):
    h = res_ref[...] + _dot(x_ref[...], w_ref[...])
    hn_refs = out_refs
    if emit_h:
        out_refs[0][...] = h
        hn_refs = out_refs[1:]
    inv = lax.rsqrt(jnp.mean(h * h, axis=-1, keepdims=True) + RMS_EPS)
    for j, hn_ref in enumerate(hn_refs):
        hn_ref[...] = (h * inv * g_ref[j:j + 1, :]).astype(hn_ref.dtype)


def matmul_residual_norm(x, w, res, gains, tm, norm_dtype, emit_h=True):
    m, kdim = x.shape
    n = w.shape[1]
    ng = gains.shape[0]
    row = lambda width: pl.BlockSpec((tm, width), lambda i: (i, 0))
    return pl.pallas_call(
        functools.partial(_mm_res_norm_kernel, emit_h=emit_h),
        grid=(m // tm,),
        in_specs=[row(kdim),
                  pl.BlockSpec((kdim, n), lambda i: (0, 0), pipeline_mode=pl.Buffered(1)),
                  row(n),
                  pl.BlockSpec((ng, n), lambda i: (0, 0))],
        out_specs=[row(n)] * (int(emit_h) + ng),
        out_shape=[jax.ShapeDtypeStruct((m, n), F32)] * int(emit_h) + [jax.ShapeDtypeStruct((m, n), norm_dtype)] * ng,
        compiler_params=_cparams(("arbitrary",)),
        name="matmul_residual_norm",
    )(x, w, res, gains)


def _mm_res_norm_blocks_kernel(x_ref, w_ref, res_ref, *rest, emit_h, head, first_block):
    if head:
        head_ref, g_ref, *out_refs = rest
        res = _padded_rows(res_ref, head_ref, first_block)
    else:
        g_ref, *out_refs = rest
        res = res_ref[...]
    h = res + _dot(x_ref[...], w_ref[...])
    hn_refs = out_refs
    if emit_h:
        out_refs[0][...] = h
        hn_refs = out_refs[1:]
    inv = lax.rsqrt(jnp.mean(h * h, axis=-1, keepdims=True) + RMS_EPS)
    for j, hn_ref in enumerate(hn_refs):
        hn_ref[...] = (h * inv * g_ref[j:j + 1, :]).astype(hn_ref.dtype)


def matmul_residual_norm_blocks(x, w, res, gains, n_batch, norm_dtype, emit_h, head=None, first_block=0):
    m, kdim = x.shape
    n = w.shape[1]
    ng = gains.shape[0]
    nb = m // (n_batch * BLOCK)
    nb_out = nb - first_block
    padded = lambda width: pl.BlockSpec((BLOCK, width), lambda i, t: (i * nb + t + first_block, 0))
    compact = lambda width: pl.BlockSpec((BLOCK, width), lambda i, t: (i * nb_out + t, 0))
    const = lambda shape: pl.BlockSpec(shape, lambda i, t: (0,) * len(shape))
    if head is not None:
        res_specs = [pl.BlockSpec((1, BLOCK, n), lambda i, t: (i, jnp.maximum(t + first_block - 1, 0), 0)),
                     const((BLOCK, n))]
        res_args = [res, head]
    else:
        res_specs, res_args = [padded(n)], [res]
    norm_spec = compact(n) if first_block else padded(n)
    norm_rows = n_batch * nb_out * BLOCK
    return pl.pallas_call(
        functools.partial(_mm_res_norm_blocks_kernel, emit_h=emit_h, head=head is not None, first_block=first_block),
        grid=(n_batch, nb_out),
        in_specs=[padded(kdim), pl.BlockSpec((kdim, n), lambda i, t: (0, 0), pipeline_mode=pl.Buffered(1))]
        + res_specs + [const((ng, n))],
        out_specs=[padded(n)] * int(emit_h) + [norm_spec] * ng,
        out_shape=[jax.ShapeDtypeStruct((m, n), F32)] * int(emit_h)
        + [jax.ShapeDtypeStruct((norm_rows, n), norm_dtype)] * ng,
        compiler_params=_cparams(("arbitrary", "arbitrary")),
        name="matmul_residual_norm_blocks",
    )(x, w, *res_args, gains)


def _lora_up_kernel(hw_ref, ha_ref, w2_ref, w0_ref, a2_ref, a0_ref, wl_ref, al_ref):
    wl_ref[...] = w0_ref[...] + _dot(hw_ref[...], w2_ref[...])
    al_ref[...] = a0_ref[...] + _dot(ha_ref[...], a2_ref[...])


def lora_up(hw, ha, w2, w0, a2, a0):
    m = hw.shape[0]
    e = w2.shape[1]
    return pl.pallas_call(
        _lora_up_kernel,
        out_shape=[jax.ShapeDtypeStruct((m, e), F32), jax.ShapeDtypeStruct((m, e), F32)],
        name="lora_up",
    )(hw, ha, w2, w0.reshape(1, e), a2, a0.reshape(1, e))


def _seg_sum(x, ones_bd):
    hi = x.astype(BF16)
    outs = []
    for c in range(x.shape[1] // MXU_TILE):
        sl = slice(c * MXU_TILE, (c + 1) * MXU_TILE)
        outs.append(_dot(hi[:, sl], ones_bd))
    return jnp.concatenate(outs, axis=1) if len(outs) > 1 else outs[0]


def _wkv_prompt_kernel(r_ref, k_ref, v_ref, z_ref, hw_ref, ha_ref, w2_ref, w0_ref, a2_ref, a0_ref,
                       kk_ref, ka_ref, rk_ref, gg_ref, gb_ref, yg_ref, sout_ref, s_ref):
    t_idx = pl.program_id(2)
    c = CHUNK
    hd = HEAD_DIM

    @pl.when(t_idx == 0)
    def _():
        s_ref[...] = jnp.zeros_like(s_ref)

    tb = r_ref.shape[1]
    nh = HEADS_PER_STREAM
    hw = nh * hd
    nc = tb // c

    li = lax.broadcasted_iota(jnp.int32, (MXU_TILE, MXU_TILE), 0) // hd
    lj = lax.broadcasted_iota(jnp.int32, (MXU_TILE, MXU_TILE), 1) // hd
    ones_bd = jnp.where(li == lj, 1.0, 0.0).astype(BF16)
    bi_ = lax.broadcasted_iota(jnp.int32, (tb, tb), 0)
    bj_ = lax.broadcasted_iota(jnp.int32, (tb, tb), 1)
    tri_incl = jnp.where((bj_ <= bi_) & (bj_ // c == bi_ // c), 1.0, 0.0).astype(BF16)
    ti = lax.broadcasted_iota(jnp.int32, (c, c), 0)
    tj = lax.broadcasted_iota(jnp.int32, (c, c), 1)
    ai = lax.broadcasted_iota(jnp.int32, (c, 2 * c), 0)
    aj = lax.broadcasted_iota(jnp.int32, (c, 2 * c), 1)
    upper = aj >= c
    aj_mod = jnp.where(upper, aj - c, aj)
    masks = dict(
        strict=tj < ti,
        eye=jnp.where(ti == tj, 1.0, 0.0).astype(F32),
        top_k=upper & (aj_mod < ai),
        bot=aj_mod <= ai)

    for st in range(r_ref.shape[2] // hw):
        _wkv_stream(st, hw, nc, r_ref, k_ref, v_ref, z_ref, hw_ref, ha_ref, w2_ref, w0_ref, a2_ref, a0_ref,
                    kk_ref, ka_ref, rk_ref, gg_ref, gb_ref, yg_ref, s_ref, ones_bd, tri_incl, masks)

    @pl.when(t_idx == pl.num_programs(2) - 1)
    def _():
        sout_ref[0] = s_ref[...]


def _wkv_stream(st, hw, nc, r_ref, k_ref, v_ref, z_ref, hw_ref, ha_ref, w2_ref, w0_ref, a2_ref, a0_ref,
                kk_ref, ka_ref, rk_ref, gg_ref, gb_ref, yg_ref, s_ref, ones_bd, tri_incl, masks):
    c = CHUNK
    hd = HEAD_DIM
    nh = hw // hd
    ls = slice(st * hw, (st + 1) * hw)
    h0 = st * nh
    r = r_ref[0, :, ls]
    k = k_ref[0, :, ls]
    v = v_ref[0, :, ls]
    wl = w0_ref[:, ls] + _dot(hw_ref[...], w2_ref[:, ls])
    al = a0_ref[:, ls] + _dot(ha_ref[...], a2_ref[:, ls])
    a = _sigmoid(al)
    lw = -DECAY_SCALE * _sigmoid(wl)
    kk = k * kk_ref[:, ls]
    n2 = _seg_sum(kk * kk, ones_bd)
    kk = kk / jnp.maximum(jnp.sqrt(n2), 1e-12)
    k2 = k * (1.0 + (a - 1.0) * ka_ref[:, ls])
    bb = kk * a

    lw_hi, lw_lo = _split_hi_lo(lw)
    g = _dot(tri_incl, lw_hi) + _dot(tri_incl, lw_lo)
    mid = lambda ci: g[ci * c + c // 2 - 1:ci * c + c // 2, :]
    gm = jnp.concatenate([jnp.broadcast_to(mid(ci), (c, hw)) for ci in range(nc)], axis=0)
    e_a = jnp.exp(g - gm)
    e_prev = jnp.exp(g - lw - gm)
    e_inv = jnp.exp(gm - g)
    e1 = [jnp.exp(mid(ci)) for ci in range(nc)]
    e2 = [jnp.exp(g[ci * c + c - 1:ci * c + c, :] - mid(ci)) for ci in range(nc)]

    kkd = (kk * e_prev).astype(BF16)
    rd = (r * e_a).astype(BF16)
    bi = (bb * e_inv).astype(BF16)
    ki = (k2 * e_inv).astype(BF16)
    v_bf = v.astype(BF16)
    zeros_cv = jnp.zeros((c, hd), BF16)

    pairs = [(ci, h) for ci in range(nc) for h in range(nh)]
    rows = lambda ci: slice(ci * c, (ci + 1) * c)
    cols = lambda h: slice(h * hd, (h + 1) * hd)
    xs = {(ci, h): jnp.concatenate([kkd[rows(ci), cols(h)], rd[rows(ci), cols(h)]], axis=0) for ci, h in pairs}
    r1s = {(ci, h): jnp.concatenate([bi[rows(ci), cols(h)], ki[rows(ci), cols(h)]], axis=0) for ci, h in pairs}
    vs = {(ci, h): v_bf[rows(ci), cols(h)] for ci, h in pairs}
    a_mats = {p: _dot_nt(xs[p], r1s[p]) for p in pairs}
    lk_vs = {p: _dot(jnp.where(masks["top_k"], a_mats[p][:c, :], 0.0).astype(BF16),
                     jnp.concatenate([zeros_cv, vs[p]], axis=0)) for p in pairs}
    lps = {p: jnp.where(masks["strict"], a_mats[p][:c, :c], 0.0) for p in pairs}
    ts = {p: masks["eye"] - lps[p] for p in pairs}
    for _ in range(int(math.log2(c)) - 1):
        lpb = {p: lps[p].astype(BF16) for p in pairs}
        lps = {p: _dot(lpb[p], lpb[p]) for p in pairs}
        ts = {p: _dot(ts[p].astype(BF16), (masks["eye"] + lps[p]).astype(BF16)) for p in pairs}
    a_bots = {p: jnp.where(masks["bot"], a_mats[p][c:, :], 0.0).astype(BF16) for p in pairs}
    t_bf = {p: ts[p].astype(BF16) for p in pairs}

    state = [s_ref[h0 + h] for h in range(nh)]
    y_rows = []
    for ci in range(nc):
        hs = range(nh)
        sms = [state[h] * e1[ci][:, cols(h)] for h in hs]
        p_mats = [_dot_nt(xs[ci, h], sms[h].astype(BF16)) for h in hs]
        us = [-_dot(t_bf[ci, h], (p_mats[h][:c, :] + lk_vs[ci, h]).astype(BF16)) for h in hs]
        uvs = [jnp.concatenate([us[h].astype(BF16), vs[ci, h]], axis=0) for h in hs]
        ys = [p_mats[h][c:, :] + _dot(a_bots[ci, h], uvs[h]) for h in hs]
        state = [(sms[h] + _dot_tn(uvs[h], r1s[ci, h])) * e2[ci][:, cols(h)] for h in hs]
        y_rows.append(jnp.concatenate(ys, axis=1))
    for h in range(nh):
        s_ref[h0 + h] = state[h]
    y = jnp.concatenate(y_rows, axis=0) if nc > 1 else y_rows[0]

    inv_n = 1.0 / hd
    mean = _seg_sum(y, ones_bd) * inv_n
    yc = y - mean
    var = _seg_sum(yc * yc, ones_bd) * inv_n
    yn = yc * lax.rsqrt(var + GN_EPS) * gg_ref[:, ls] + gb_ref[:, ls]
    bonus = _seg_sum(r * k2 * rk_ref[:, ls], ones_bd)
    yg_ref[:, ls] = ((yn + bonus * v) * _silu(z_ref[0, :, ls])).astype(BF16)


def wkv_prompt(rkvz, hw_act, ha_act, w2, w0, a2, a0, k_k, k_a, r_k, gn_g, gn_b, n_batch):
    _, m, e = rkvz.shape
    p = m // n_batch
    tb = WKV_ROWS
    nt = p // tb
    hw = HEADS_PER_STEP * HEAD_DIM
    nh = e // HEAD_DIM
    row = lambda i, g, t: (i * nt + t, g)
    proj = lambda q: pl.BlockSpec((1, tb, hw), lambda i, g, t: (q, i * nt + t, g))
    par = pl.BlockSpec((1, hw), lambda i, g, t: (0, g))
    lr = w2.shape[0]
    hid = pl.BlockSpec((tb, lr), lambda i, g, t: (i * nt + t, 0))
    up = pl.BlockSpec((lr, hw), lambda i, g, t: (0, g))
    return pl.pallas_call(
        _wkv_prompt_kernel,
        grid=(n_batch, e // hw, nt),
        in_specs=[proj(0), proj(1), proj(2), proj(3), hid, hid, up, par, up, par,
                  par, par, par, par, par],
        out_specs=[pl.BlockSpec((tb, hw), row),
                   pl.BlockSpec((1, HEADS_PER_STEP, HEAD_DIM, HEAD_DIM), lambda i, g, t: (i, g, 0, 0))],
        out_shape=[jax.ShapeDtypeStruct((m, e), BF16),
                   jax.ShapeDtypeStruct((n_batch, nh, HEAD_DIM, HEAD_DIM), F32)],
        scratch_shapes=[pltpu.VMEM((HEADS_PER_STEP, HEAD_DIM, HEAD_DIM), F32)],
        compiler_params=_cparams(("arbitrary", "arbitrary", "arbitrary")),
        name="wkv_prompt",
    )(rkvz, rkvz, rkvz, rkvz, hw_act, ha_act, w2, w0.reshape(1, e), a2, a0.reshape(1, e),
      k_k.reshape(1, e), k_a.reshape(1, e), r_k.reshape(1, e), gn_g.reshape(1, e), gn_b.reshape(1, e))


def _wkv_sample_kernel(r_ref, k_ref, v_ref, z_ref, wl_ref, al_ref, kk_ref, ka_ref, rk_ref, gg_ref, gb_ref,
                       s_ref, yg_ref, sout_ref, y_scr):
    hd = HEAD_DIM
    r = r_ref[0]
    k = k_ref[0]
    v = v_ref[0]
    a = _sigmoid(al_ref[0])
    d = jnp.exp(-DECAY_SCALE * _sigmoid(wl_ref[0]))
    kk = k * kk_ref[...]
    kk = kk / jnp.maximum(jnp.sqrt(jnp.sum(kk * kk, axis=-1, keepdims=True)), 1e-12)
    k2 = k * (1.0 + (a - 1.0) * ka_ref[...])
    bb = kk * a
    nh = r.shape[0]
    ii = lax.broadcasted_iota(jnp.int32, (hd, hd), 0)
    jj = lax.broadcasted_iota(jnp.int32, (hd, hd), 1)
    eye = ii == jj

    row = lambda x, h: x[h:h + 1, :]
    group = 8
    for h0 in range(0, nh, group):
        hs = range(h0, h0 + group)
        s = {h: s_ref[0, h] for h in hs}
        sa = {h: jnp.sum(s[h] * row(kk, h), axis=-1, keepdims=True) for h in hs}
        v_col = {h: jnp.sum(jnp.where(eye, row(v, h), 0.0), axis=-1, keepdims=True) for h in hs}
        s_new = {h: s[h] * row(d, h) - sa[h] * row(bb, h) + v_col[h] * row(k2, h) for h in hs}
        y_col = {h: jnp.sum(s_new[h] * row(r, h), axis=-1, keepdims=True) for h in hs}
        for h in hs:
            sout_ref[0, h] = s_new[h]
            y_scr[h:h + 1, :] = jnp.sum(jnp.where(eye, y_col[h], 0.0), axis=0, keepdims=True)

    y = y_scr[...]
    mean = jnp.mean(y, axis=-1, keepdims=True)
    yc = y - mean
    var = jnp.mean(yc * yc, axis=-1, keepdims=True)
    yn = yc * lax.rsqrt(var + GN_EPS) * gg_ref[...] + gb_ref[...]
    bonus = jnp.sum(r * k2 * rk_ref[...], axis=-1, keepdims=True)
    yg_ref[0] = ((yn + bonus * v) * _silu(z_ref[0])).astype(BF16)


def wkv_sample(rkvz, wl, al, k_k, k_a, r_k, gn_g, gn_b, state):
    _, m, e = rkvz.shape
    nh = e // HEAD_DIM
    hd = HEAD_DIM
    rkvz4 = rkvz.reshape(4, m, nh, hd)
    proj = lambda q: pl.BlockSpec((None, 1, nh, hd), lambda i: (q, i, 0, 0))
    tok = pl.BlockSpec((1, nh, hd), lambda i: (i, 0, 0))
    par = pl.BlockSpec((nh, hd), lambda i: (0, 0))
    st = pl.BlockSpec((1, nh, hd, hd), lambda i: (i, 0, 0, 0))
    as_heads = lambda x: x.reshape(nh, hd)
    return pl.pallas_call(
        _wkv_sample_kernel,
        grid=(m,),
        in_specs=[proj(0), proj(1), proj(2), proj(3), tok, tok, par, par, par, par, par, st],
        out_specs=[tok, st],
        out_shape=[jax.ShapeDtypeStruct((m, nh, hd), BF16), jax.ShapeDtypeStruct(state.shape, F32)],
        scratch_shapes=[pltpu.VMEM((nh, hd), F32)],
        compiler_params=_cparams(("arbitrary",)),
        name="wkv_sample",
    )(rkvz4, rkvz4, rkvz4, rkvz4, wl.reshape(m, nh, hd), al.reshape(m, nh, hd),
      as_heads(k_k), as_heads(k_a), as_heads(r_k), as_heads(gn_g), as_heads(gn_b), state)


def rope_tables(pos):
    half = ROPE_DIM // 2
    inv_freq = ROPE_THETA ** (-jnp.arange(half, dtype=F32) * 2.0 / ROPE_DIM)
    ang = pos.astype(F32)[:, None] * inv_freq[None, :]
    cos = jnp.cos(ang)
    sin = jnp.sin(ang)
    rows = pos.shape[0]
    ones = jnp.ones((rows, HEAD_DIM - ROPE_DIM), F32)
    zeros_h = jnp.zeros((rows, half), F32)
    zeros_r = jnp.zeros((rows, HEAD_DIM - ROPE_DIM), F32)
    cos_h = jnp.concatenate([cos, cos, ones], axis=1)
    sa_h = jnp.concatenate([-sin, zeros_h, zeros_r], axis=1)
    sb_h = jnp.concatenate([zeros_h, sin, zeros_r], axis=1)
    two = lambda t: jnp.concatenate([t, t], axis=1)
    return two(cos_h), two(sa_h), two(sb_h)


def _attn_prompt_kernel(sink_ref, q_ref, kc_ref, kp_ref, vc_ref, vp_ref, z_ref, o_ref):
    n = pl.program_id(1)
    hd = HEAD_DIM
    blk = q_ref.shape[0]
    n_kv = kc_ref.shape[1] // hd
    grp = q_ref.shape[1] // (n_kv * hd)
    qi = lax.broadcasted_iota(jnp.int32, (blk, 2 * blk), 0)
    kj = lax.broadcasted_iota(jnp.int32, (blk, 2 * blk), 1) - blk
    kpos = n * blk + kj
    diff = qi - kj
    valid = (kpos >= LEAD) & (diff >= 0) & (diff <= WINDOW)
    k_all = jnp.concatenate([kp_ref[...], kc_ref[...]], axis=0).astype(BF16)
    v_all = jnp.concatenate([vp_ref[...], vc_ref[...]], axis=0).astype(BF16)

    def scores(h):
        k_h = k_all[:, h * hd:(h + 1) * hd]
        return [_dot_nt(q_ref[:, (h * grp + gi) * hd:(h * grp + gi + 1) * hd], k_h) for gi in range(grp)]

    outs = []
    s_next = scores(0)
    for h in range(n_kv):
        s_cur = s_next
        if h + 1 < n_kv:
            s_next = scores(h + 1)
        v_h = v_all[:, h * hd:(h + 1) * hd]
        for g0 in range(0, grp, SOFTMAX_BATCH):
            gs = range(g0, min(g0 + SOFTMAX_BATCH, grp))
            sks = {gi: sink_ref[h * grp + gi] for gi in gs}
            ss = {gi: jnp.where(valid, s_cur[gi], -jnp.inf) for gi in gs}
            ms = {gi: jnp.maximum(jnp.max(ss[gi], axis=-1, keepdims=True), sks[gi]) for gi in gs}
            ps = {gi: jnp.exp(ss[gi] - ms[gi]) for gi in gs}
            dens = {gi: jnp.sum(ps[gi], axis=-1, keepdims=True) + jnp.exp(sks[gi] - ms[gi]) for gi in gs}
            outs += [_dot(ps[gi].astype(BF16), v_h) / dens[gi] for gi in gs]
    att = jnp.concatenate(outs, axis=1)
    o_ref[...] = (att * _silu(z_ref[...])).astype(o_ref.dtype)


def attn_prompt(sinks, q, k, v, z, n_batch):
    m, e = q.shape
    nb = m // (n_batch * BLOCK)
    kw = k.shape[1]
    cur = lambda i, n: (i * nb + n, 0)
    prv = lambda i, n: (i * nb + jnp.maximum(n - 1, 0), 0)
    return pl.pallas_call(
        _attn_prompt_kernel,
        grid=(n_batch, nb),
        in_specs=[pl.BlockSpec(memory_space=pltpu.SMEM),
                  pl.BlockSpec((BLOCK, e), cur),
                  pl.BlockSpec((BLOCK, kw), cur), pl.BlockSpec((BLOCK, kw), prv),
                  pl.BlockSpec((BLOCK, kw), cur), pl.BlockSpec((BLOCK, kw), prv),
                  pl.BlockSpec((BLOCK, e), cur)],
        out_specs=pl.BlockSpec((BLOCK, e), cur),
        out_shape=jax.ShapeDtypeStruct((m, e), BF16),
        compiler_params=_cparams(("arbitrary", "arbitrary")),
        name="attn_prompt",
    )(sinks, q, k, k, v, v, z)


def _attn_sample_kernel(sink_ref, q_ref, kc_ref, vc_ref, kn_ref, vn_ref, z_ref, o_ref, ko_ref, vo_ref):
    hd = HEAD_DIM
    win = kc_ref.shape[1]
    n_kv = kc_ref.shape[2] // hd
    nq = q_ref.shape[1]
    grp = nq // n_kv
    pad = 8
    kc = kc_ref[0]
    vc = vc_ref[0]
    kn = kn_ref[0]
    vn = vn_ref[0]
    first = lax.broadcasted_iota(jnp.int32, (pad, kc.shape[1]), 0) == 0
    k_all = jnp.concatenate([kc, jnp.where(first, kn, 0.0)], axis=0).astype(BF16)
    v_all = jnp.concatenate([vc, jnp.where(first, vn, 0.0)], axis=0).astype(BF16)
    col = lax.broadcasted_iota(jnp.int32, (grp, win + pad), 1)
    valid = (col <= win) & (win - col <= WINDOW)
    q = q_ref[0].astype(BF16)
    row_i = lax.broadcasted_iota(jnp.int32, (grp, 1), 0)
    hs = range(n_kv)
    sks = []
    for h in hs:
        sk = jnp.zeros((grp, 1), F32)
        for gi in range(grp):
            sk = jnp.where(row_i == gi, sink_ref[h * grp + gi], sk)
        sks.append(sk)
    ss = [jnp.where(valid, _dot_nt(q[h * grp:(h + 1) * grp, :], k_all[:, h * hd:(h + 1) * hd]), -jnp.inf)
          for h in hs]
    ms = [jnp.maximum(jnp.max(ss[h], axis=-1, keepdims=True), sks[h]) for h in hs]
    ps = [jnp.exp(ss[h] - ms[h]) for h in hs]
    dens = [jnp.sum(ps[h], axis=-1, keepdims=True) + jnp.exp(sks[h] - ms[h]) for h in hs]
    outs = [_dot(ps[h].astype(BF16), v_all[:, h * hd:(h + 1) * hd]) / dens[h] for h in hs]
    att = jnp.concatenate(outs, axis=0)
    o_ref[0] = (att * _silu(z_ref[0])).astype(o_ref.dtype)
    last = lax.broadcasted_iota(jnp.int32, kc.shape, 0) == win - 1
    ko_ref[0] = jnp.where(last, kn, pltpu.roll(kc, win - 1, axis=0))
    vo_ref[0] = jnp.where(last, vn, pltpu.roll(vc, win - 1, axis=0))


def attn_sample(sinks, q, cache_k, cache_v, k_new, v_new, z):
    m, win, kw = cache_k.shape
    nq = q.shape[1]
    hd = HEAD_DIM
    tok = pl.BlockSpec((1, nq, hd), lambda i: (i, 0, 0))
    cache = pl.BlockSpec((1, win, kw), lambda i: (i, 0, 0))
    new = pl.BlockSpec((1, 1, kw), lambda i: (i, 0, 0))
    return pl.pallas_call(
        _attn_sample_kernel,
        grid=(m,),
        in_specs=[pl.BlockSpec(memory_space=pltpu.SMEM), tok, cache, cache, new, new, tok],
        out_specs=[tok, cache, cache],
        out_shape=[jax.ShapeDtypeStruct((m, nq, hd), BF16),
                   jax.ShapeDtypeStruct(cache_k.shape, F32), jax.ShapeDtypeStruct(cache_v.shape, F32)],
        compiler_params=_cparams(("arbitrary",)),
        name="attn_sample",
    )(sinks, q, cache_k, cache_v, k_new, v_new, z)


def _pad_lora(w_down, w_up):
    r = w_down.shape[1]
    return (jnp.pad(w_down, ((0, 0), (0, LORA_PAD - r))).astype(BF16),
            jnp.pad(w_up, ((0, LORA_PAD - r), (0, 0))).astype(BF16))


def kernel(x_prompt, x_sample, state_wkv, state_shift, cache_k, cache_v, meta_tokens, a_norm, a_mu, a_w_rkvz,
           a_w0, a_w1, a_w2, a_a0, a_a1, a_a2, a_k_k, a_k_a, a_r_k, a_gn_g, a_gn_b, a_w_out, kv_norm, w_kv,
           b_norm, b_w_qz, b_sinks, b_w_o, final_norm):
    nb, seq, d = x_prompt.shape
    db, dseq, _ = x_sample.shape
    assert dseq == 1 and a_norm.shape[0] == 1 and b_norm.shape[0] == 1
    e = a_w_rkvz.shape[3]
    win = cache_k.shape[1]
    p_len = LEAD + N_META + seq
    assert p_len % BLOCK == 0 and (LEAD + N_META) == BLOCK
    kvw = N_KV_HEADS * HEAD_DIM

    w1, w2 = _pad_lora(a_w1[0], a_w2[0])
    a1, a2 = _pad_lora(a_a1[0], a_a2[0])
    w_out = a_w_out[0].astype(BF16)
    w_kv_bf = w_kv.astype(BF16)
    w_o = b_w_o[0].astype(BF16)
    mu = a_mu[0]
    sinks = b_sinks[0]
    gains_b = jnp.stack([kv_norm, b_norm[0]])

    tm = p_len // 8
    tm_wide, tn_wide = p_len // 2, 1024

    head = jnp.concatenate([jnp.zeros((LEAD, d), F32), meta_tokens], axis=0)
    xm, hw_p, ha_p, x_last = norm_shift_prompt(x_prompt, head, a_norm[0], mu, w1, a1)
    p_state_shift = x_last.reshape(1, nb, d)
    rkvz, w_rkvz = matmul_cast_weights(xm, a_w_rkvz[0], tm_wide, tn_wide, F32)
    yg, p_state = wkv_prompt(rkvz, hw_p, ha_p, w2, a_w0[0], a2, a_a0[0], a_k_k[0], a_k_a[0],
                             a_r_k[0].reshape(-1), a_gn_g[0], a_gn_b[0], nb)
    hp, hn_kv, hn_b = matmul_residual_norm_blocks(yg, w_out, x_prompt, gains_b, nb, BF16, True, head=head)

    pos_p = jnp.maximum(jnp.arange(p_len, dtype=jnp.int32) - LEAD, 0)
    tabs_p = tuple(jnp.tile(t, (nb, 1)) for t in rope_tables(pos_p))
    k_p, v_p = matmul_rope(hn_kv, w_kv_bf, tabs_p, tm, kvw, (F32, F32))
    qz_cols = e // tn_wide
    q_p, w_q = matmul_cast_weights(hn_b[None], b_w_qz, tm_wide, tn_wide, BF16, 0, qz_cols, tabs_p, Q_SCALE)
    z_p, w_z = matmul_cast_weights(hn_b[None], b_w_qz, tm_wide, tn_wide, F32, qz_cols, qz_cols)
    q_p, z_p = q_p[0], z_p[0]
    att = attn_prompt(sinks, q_p, k_p, v_p, z_p, nb)
    y_prompt, = matmul_residual_norm_blocks(att, w_o, hp, final_norm[None], nb, F32, False,
                                            first_block=(LEAD + N_META) // BLOCK)
    y_prompt = y_prompt.reshape(nb, seq, d)
    tail = lambda t: t.reshape(nb, p_len, kvw)[:, -win:].reshape(nb, win, N_KV_HEADS, HEAD_DIM)
    p_cache_k = tail(k_p)
    p_cache_v = tail(v_p)

    hs = x_sample.reshape(db, d)
    xm_s, hw_s, ha_s, xn_s = norm_shift_sample(hs, state_shift[0], a_norm[0], mu, w1, a1)
    rkvz_s = matmul_groups(xm_s, w_rkvz, db, F32)
    wl_s, al_s = lora_up(hw_s, ha_s, w2, a_w0[0], a2, a_a0[0])
    yg_s, s_state = wkv_sample(rkvz_s, wl_s, al_s, a_k_k[0], a_k_a[0], a_r_k[0].reshape(-1), a_gn_g[0],
                               a_gn_b[0], state_wkv[0])
    hs, hn_kv_s, hn_b_s = matmul_residual_norm(yg_s.reshape(db, e), w_out, hs, gains_b, db, BF16)
    tabs_s = rope_tables(jnp.full((db,), PAST_LEN, jnp.int32))
    k_s, v_s = matmul_rope(hn_kv_s, w_kv_bf, tabs_s, db, kvw, (F32, F32))
    q_s, = matmul_rope(hn_b_s, w_q[0], tabs_s, db, e, (F32,), scale=Q_SCALE)
    z_s = matmul_groups(hn_b_s[None], w_z, db, F32)[0]
    nq = e // HEAD_DIM
    att_s, s_cache_k, s_cache_v = attn_sample(
        sinks, q_s.reshape(db, nq, HEAD_DIM), cache_k.reshape(db, win, kvw), cache_v.reshape(db, win, kvw),
        k_s.reshape(db, 1, kvw), v_s.reshape(db, 1, kvw), z_s.reshape(db, nq, HEAD_DIM))
    y_s, = matmul_residual_norm(att_s.reshape(db, e), w_o, hs, final_norm[None], db, F32, emit_h=False)
    y_sample = y_s.reshape(db, 1, d)

    return (y_prompt, y_sample, p_state[None], p_state_shift,
            p_cache_k, p_cache_v,
            s_state[None], xn_s[None],
            s_cache_k.reshape(cache_k.shape), s_cache_v.reshape(cache_v.shape))
```

```python
import functools
import math

import jax
import jax.numpy as jnp
from jax import lax
from jax.experimental import pallas as pl
from jax.experimental.pallas import tpu as pltpu

F32 = jnp.float32
BF16 = jnp.bfloat16

HEAD_DIM = 64
N_KV_HEADS = 8
WINDOW = 128
BLOCK = 128
ROPE_DIM = HEAD_DIM // 4
ROPE_THETA = 500000.0
N_META = 16
PAST_LEN = 16384
RMS_EPS = 1e-6
GN_EPS = 64e-5
LEAD = (-N_META) % BLOCK
CHUNK = 64
WKV_ROWS = 128
HEADS_PER_STREAM = 32
HEADS_PER_STEP = 32
LORA_PAD = 128
MXU_TILE = 256
ROPE_SLAB = 512
VMEM_LIMIT = 48 * 1024 * 1024
DECAY_SCALE = math.exp(-0.5)
SOFTMAX_BATCH = 4
Q_SCALE = HEAD_DIM ** -0.5


def _cparams(sem):
    return pltpu.CompilerParams(dimension_semantics=sem, vmem_limit_bytes=VMEM_LIMIT)


def _sigmoid(x):
    return 1.0 / (1.0 + jnp.exp(-x))


def _silu(x):
    return x * _sigmoid(x)


def _dot(a, b):
    return jnp.dot(a, b, preferred_element_type=F32)


def _dot_nt(a, b):
    return lax.dot_general(a, b, (((1,), (1,)), ((), ())), preferred_element_type=F32)


def _dot_tn(a, b):
    return lax.dot_general(a, b, (((0,), (0,)), ((), ())), preferred_element_type=F32)


def _split_hi_lo(x):
    hi = x.astype(BF16)
    lo = (x - hi.astype(F32)).astype(BF16)
    return hi, lo


def _mixes(xn, prev, mu_ref, w1_ref, a1_ref, xm_ref, hw_ref, ha_ref):
    xx = prev - xn
    n_proj = xm_ref.shape[0]
    for p in range(n_proj):
        xm_ref[p] = (xn + xx * mu_ref[p:p + 1, :]).astype(xm_ref.dtype)
    xw = (xn + xx * mu_ref[n_proj:n_proj + 1, :]).astype(BF16)
    xa = (xn + xx * mu_ref[n_proj + 1:n_proj + 2, :]).astype(BF16)
    hw_ref[...] = jnp.tanh(_dot(xw, w1_ref[...])).astype(hw_ref.dtype)
    ha_ref[...] = _dot(xa, a1_ref[...]).astype(ha_ref.dtype)


def _padded_rows(x_ref, head_ref, first_block):
    return jnp.where(pl.program_id(1) + first_block == 0, head_ref[...], x_ref[0])


def _norm_shift_kernel(x_ref, head_ref, g_ref, mu_ref, w1_ref, a1_ref, xm_ref, hw_ref, ha_ref, last_ref, carry_ref):
    @pl.when(pl.program_id(1) == 0)
    def _():
        carry_ref[...] = jnp.zeros_like(carry_ref)

    x = _padded_rows(x_ref, head_ref, 0)
    tm = x.shape[0]
    xn = x * lax.rsqrt(jnp.mean(x * x, axis=-1, keepdims=True) + RMS_EPS) * g_ref[...]
    rolled = pltpu.roll(xn, 1, axis=0)
    row = lax.broadcasted_iota(jnp.int32, xn.shape, 0)
    prev = jnp.where(row == 0, carry_ref[0:1, :], rolled)
    _mixes(xn, prev, mu_ref, w1_ref, a1_ref, xm_ref, hw_ref, ha_ref)
    carry_ref[0:1, :] = xn[tm - 1:tm, :]
    last_ref[0] = xn[tm - 1:tm, :]


def norm_shift_prompt(x, head, g, mu, w1, a1):
    b, seq, d = x.shape
    tm = BLOCK
    p = tm + seq
    n_mix = mu.shape[0]
    n_proj = n_mix - 2
    lr = w1.shape[1]
    nt = p // tm
    const = lambda shape: pl.BlockSpec(shape, lambda i, t: (0,) * len(shape))
    hid = pl.BlockSpec((tm, lr), lambda i, t: (i * nt + t, 0))
    return pl.pallas_call(
        _norm_shift_kernel,
        grid=(b, nt),
        in_specs=[pl.BlockSpec((1, tm, d), lambda i, t: (i, jnp.maximum(t - 1, 0), 0)),
                  const((tm, d)), const((1, d)), const((n_mix, d)), const((d, lr)), const((d, lr))],
        out_specs=[pl.BlockSpec((n_proj, tm, d), lambda i, t: (0, i * nt + t, 0)), hid, hid,
                   pl.BlockSpec((1, 1, d), lambda i, t: (i, 0, 0))],
        out_shape=[jax.ShapeDtypeStruct((n_proj, b * p, d), BF16),
                   jax.ShapeDtypeStruct((b * p, lr), BF16), jax.ShapeDtypeStruct((b * p, lr), BF16),
                   jax.ShapeDtypeStruct((b, 1, d), F32)],
        scratch_shapes=[pltpu.VMEM((8, d), F32)],
        compiler_params=_cparams(("arbitrary", "arbitrary")),
        name="norm_shift_prompt",
    )(x, head, g.reshape(1, d), mu, w1, a1)


def _norm_shift_sample_kernel(x_ref, prev_ref, g_ref, mu_ref, w1_ref, a1_ref, xm_ref, hw_ref, ha_ref, xn_ref):
    x = x_ref[...]
    xn = x * lax.rsqrt(jnp.mean(x * x, axis=-1, keepdims=True) + RMS_EPS) * g_ref[...]
    xn_ref[...] = xn
    _mixes(xn, prev_ref[...], mu_ref, w1_ref, a1_ref, xm_ref, hw_ref, ha_ref)


def norm_shift_sample(x, prev, g, mu, w1, a1):
    m, d = x.shape
    lr = w1.shape[1]
    return pl.pallas_call(
        _norm_shift_sample_kernel,
        out_shape=[jax.ShapeDtypeStruct((mu.shape[0] - 2, m, d), BF16),
                   jax.ShapeDtypeStruct((m, lr), BF16), jax.ShapeDtypeStruct((m, lr), BF16),
                   jax.ShapeDtypeStruct((m, d), F32)],
        name="norm_shift_sample",
    )(x, prev, g.reshape(1, d), mu, w1, a1)


def _rope(y, cos, sin_a, sin_b):
    half = ROPE_DIM // 2
    step = ROPE_SLAB
    rep = step // cos.shape[1]
    tile = lambda t: jnp.concatenate([t] * rep, axis=1)
    cos_t, sa_t, sb_t = tile(cos), tile(sin_a), tile(sin_b)
    outs = []
    for j in range(y.shape[1] // step):
        ys = y[:, j * step:(j + 1) * step]
        outs.append(ys * cos_t + pltpu.roll(ys, step - half, axis=1) * sa_t + pltpu.roll(ys, half, axis=1) * sb_t)
    return jnp.concatenate(outs, axis=1) if len(outs) > 1 else outs[0]


def _mm_group_kernel(x_ref, w_ref, o_ref):
    o_ref[0] = _dot(x_ref[0], w_ref[0]).astype(o_ref.dtype)


def matmul_groups(x, w, tm, out_dtype, n=None, col=0):
    g = w.shape[0]
    n = w.shape[2] if n is None else n
    _, m, kdim = x.shape
    return pl.pallas_call(
        _mm_group_kernel,
        grid=(g, m // tm),
        in_specs=[pl.BlockSpec((1, tm, kdim), lambda q, i: (q, i, 0)),
                  pl.BlockSpec((1, kdim, n), lambda q, i: (q, 0, col))],
        out_specs=pl.BlockSpec((1, tm, n), lambda q, i: (q, i, 0)),
        out_shape=jax.ShapeDtypeStruct((g, m, n), out_dtype),
        compiler_params=_cparams(("arbitrary", "arbitrary")),
        name="matmul_groups",
    )(x, w)


def _mm_rope_kernel(x_ref, w_ref, cos_ref, sa_ref, sb_ref, *o_refs, n_rope, scale):
    y = _dot(x_ref[...], w_ref[...])
    rot = _rope(y[:, :n_rope], cos_ref[...], sa_ref[...], sb_ref[...])
    if scale != 1.0:
        rot = rot * scale
    o_refs[0][...] = rot.astype(o_refs[0].dtype)
    if len(o_refs) > 1:
        o_refs[1][...] = y[:, n_rope:].astype(o_refs[1].dtype)


def matmul_rope(x, w, tables, tm, n_rope, out_dtypes, scale=1.0, n=None):
    m, kdim = x.shape
    n = w.shape[1] if n is None else n
    lanes = tables[0].shape[1]
    widths = [n_rope] + ([n - n_rope] if n > n_rope else [])
    tab = pl.BlockSpec((tm, lanes), lambda i: (i, 0))
    outs = pl.pallas_call(
        functools.partial(_mm_rope_kernel, n_rope=n_rope, scale=scale),
        grid=(m // tm,),
        in_specs=[pl.BlockSpec((tm, kdim), lambda i: (i, 0)),
                  pl.BlockSpec((kdim, n), lambda i: (0, 0), pipeline_mode=pl.Buffered(1)),
                  tab, tab, tab],
        out_specs=[pl.BlockSpec((tm, wd), lambda i: (i, 0)) for wd in widths],
        out_shape=[jax.ShapeDtypeStruct((m, wd), dt) for wd, dt in zip(widths, out_dtypes)],
        compiler_params=_cparams(("arbitrary",)),
        name="matmul_rope",
    )(x, w, *tables)
    return outs


def _mm_res_norm_kernel(x_ref, w_ref, res_ref, g_ref, *out_refs, emit_h):
    h = res_ref[...] + _dot(x_ref[...], w_ref[...])
    hn_refs = out_refs
    if emit_h:
        out_refs[0][...] = h
        hn_refs = out_refs[1:]
    inv = lax.rsqrt(jnp.mean(h * h, axis=-1, keepdims=True) + RMS_EPS)
    for j, hn_ref in enumerate(hn_refs):
        hn_ref[...] = (h * inv * g_ref[j:j + 1, :]).astype(hn_ref.dtype)


def matmul_residual_norm(x, w, res, gains, tm, norm_dtype, emit_h=True):
    m, kdim = x.shape
    n = w.shape[1]
    ng = gains.shape[0]
    row = lambda width: pl.BlockSpec((tm, width), lambda i: (i, 0))
    return pl.pallas_call(
        functools.partial(_mm_res_norm_kernel, emit_h=emit_h),
        grid=(m // tm,),
        in_specs=[row(kdim),
                  pl.BlockSpec((kdim, n), lambda i: (0, 0), pipeline_mode=pl.Buffered(1)),
                  row(n),
                  pl.BlockSpec((ng, n), lambda i: (0, 0))],
        out_specs=[row(n)] * (int(emit_h) + ng),
        out_shape=[jax.ShapeDtypeStruct((m, n), F32)] * int(emit_h) + [jax.ShapeDtypeStruct((m, n), norm_dtype)] * ng,
        compiler_params=_cparams(("arbitrary",)),
        name="matmul_residual_norm",
    )(x, w, res, gains)


def _mm_res_norm_blocks_kernel(x_ref, w_ref, res_ref, *rest, emit_h, head, first_block):
    if head:
        head_ref, g_ref, *out_refs = rest
        res = _padded_rows(res_ref, head_ref, first_block)
    else:
        g_ref, *out_refs = rest
        res = res_ref[...]
    h = res + _dot(x_ref[...], w_ref[...])
    hn_refs = out_refs
    if emit_h:
        out_refs[0][...] = h
        hn_refs = out_refs[1:]
    inv = lax.rsqrt(jnp.mean(h * h, axis=-1, keepdims=True) + RMS_EPS)
    for j, hn_ref in enumerate(hn_refs):
        hn_ref[...] = (h * inv * g_ref[j:j + 1, :]).astype(hn_ref.dtype)


def matmul_residual_norm_blocks(x, w, res, gains, n_batch, norm_dtype, emit_h, head=None, first_block=0):
    m, kdim = x.shape
    n = w.shape[1]
    ng = gains.shape[0]
    nb = m // (n_batch * BLOCK)
    nb_out = nb - first_block
    padded = lambda width: pl.BlockSpec((BLOCK, width), lambda i, t: (i * nb + t + first_block, 0))
    compact = lambda width: pl.BlockSpec((BLOCK, width), lambda i, t: (i * nb_out + t, 0))
    const = lambda shape: pl.BlockSpec(shape, lambda i, t: (0,) * len(shape))
    if head is not None:
        res_specs = [pl.BlockSpec((1, BLOCK, n), lambda i, t: (i, jnp.maximum(t + first_block - 1, 0), 0)),
                     const((BLOCK, n))]
        res_args = [res, head]
    else:
        res_specs, res_args = [padded(n)], [res]
    norm_spec = compact(n) if first_block else padded(n)
    norm_rows = n_batch * nb_out * BLOCK
    return pl.pallas_call(
        functools.partial(_mm_res_norm_blocks_kernel, emit_h=emit_h, head=head is not None, first_block=first_block),
        grid=(n_batch, nb_out),
        in_specs=[padded(kdim), pl.BlockSpec((kdim, n), lambda i, t: (0, 0), pipeline_mode=pl.Buffered(1))]
        + res_specs + [const((ng, n))],
        out_specs=[padded(n)] * int(emit_h) + [norm_spec] * ng,
        out_shape=[jax.ShapeDtypeStruct((m, n), F32)] * int(emit_h)
        + [jax.ShapeDtypeStruct((norm_rows, n), norm_dtype)] * ng,
        compiler_params=_cparams(("arbitrary", "arbitrary")),
        name="matmul_residual_norm_blocks",
    )(x, w, *res_args, gains)


def _lora_up_kernel(hw_ref, ha_ref, w2_ref, w0_ref, a2_ref, a0_ref, wl_ref, al_ref):
    wl_ref[...] = w0_ref[...] + _dot(hw_ref[...], w2_ref[...])
    al_ref[...] = a0_ref[...] + _dot(ha_ref[...], a2_ref[...])


def lora_up(hw, ha, w2, w0, a2, a0):
    m = hw.shape[0]
    e = w2.shape[1]
    return pl.pallas_call(
        _lora_up_kernel,
        out_shape=[jax.ShapeDtypeStruct((m, e), F32), jax.ShapeDtypeStruct((m, e), F32)],
        name="lora_up",
    )(hw, ha, w2, w0.reshape(1, e), a2, a0.reshape(1, e))


def _seg_sum(x, ones_bd):
    hi = x.astype(BF16)
    outs = []
    for c in range(x.shape[1] // MXU_TILE):
        sl = slice(c * MXU_TILE, (c + 1) * MXU_TILE)
        outs.append(_dot(hi[:, sl], ones_bd))
    return jnp.concatenate(outs, axis=1) if len(outs) > 1 else outs[0]


def _wkv_prompt_kernel(r_ref, k_ref, v_ref, z_ref, hw_ref, ha_ref, w2_ref, w0_ref, a2_ref, a0_ref,
                       kk_ref, ka_ref, rk_ref, gg_ref, gb_ref, yg_ref, sout_ref, s_ref):
    t_idx = pl.program_id(2)
    c = CHUNK
    hd = HEAD_DIM

    @pl.when(t_idx == 0)
    def _():
        s_ref[...] = jnp.zeros_like(s_ref)

    tb = r_ref.shape[1]
    nh = HEADS_PER_STREAM
    hw = nh * hd
    nc = tb // c

    li = lax.broadcasted_iota(jnp.int32, (MXU_TILE, MXU_TILE), 0) // hd
    lj = lax.broadcasted_iota(jnp.int32, (MXU_TILE, MXU_TILE), 1) // hd
    ones_bd = jnp.where(li == lj, 1.0, 0.0).astype(BF16)
    bi_ = lax.broadcasted_iota(jnp.int32, (tb, tb), 0)
    bj_ = lax.broadcasted_iota(jnp.int32, (tb, tb), 1)
    tri_incl = jnp.where((bj_ <= bi_) & (bj_ // c == bi_ // c), 1.0, 0.0).astype(BF16)
    ti = lax.broadcasted_iota(jnp.int32, (c, c), 0)
    tj = lax.broadcasted_iota(jnp.int32, (c, c), 1)
    ai = lax.broadcasted_iota(jnp.int32, (c, 2 * c), 0)
    aj = lax.broadcasted_iota(jnp.int32, (c, 2 * c), 1)
    upper = aj >= c
    aj_mod = jnp.where(upper, aj - c, aj)
    masks = dict(
        strict=tj < ti,
        eye=jnp.where(ti == tj, 1.0, 0.0).astype(F32),
        top_k=upper & (aj_mod < ai),
        bot=aj_mod <= ai)

    for st in range(r_ref.shape[2] // hw):
        _wkv_stream(st, hw, nc, r_ref, k_ref, v_ref, z_ref, hw_ref, ha_ref, w2_ref, w0_ref, a2_ref, a0_ref,
                    kk_ref, ka_ref, rk_ref, gg_ref, gb_ref, yg_ref, s_ref, ones_bd, tri_incl, masks)

    @pl.when(t_idx == pl.num_programs(2) - 1)
    def _():
        sout_ref[0] = s_ref[...]


def _wkv_stream(st, hw, nc, r_ref, k_ref, v_ref, z_ref, hw_ref, ha_ref, w2_ref, w0_ref, a2_ref, a0_ref,
                kk_ref, ka_ref, rk_ref, gg_ref, gb_ref, yg_ref, s_ref, ones_bd, tri_incl, masks):
    c = CHUNK
    hd = HEAD_DIM
    nh = hw // hd
    ls = slice(st * hw, (st + 1) * hw)
    h0 = st * nh
    r = r_ref[0, :, ls]
    k = k_ref[0, :, ls]
    v = v_ref[0, :, ls]
    wl = w0_ref[:, ls] + _dot(hw_ref[...], w2_ref[:, ls])
    al = a0_ref[:, ls] + _dot(ha_ref[...], a2_ref[:, ls])
    a = _sigmoid(al)
    lw = -DECAY_SCALE * _sigmoid(wl)
    kk = k * kk_ref[:, ls]
    n2 = _seg_sum(kk * kk, ones_bd)
    kk = kk / jnp.maximum(jnp.sqrt(n2), 1e-12)
    k2 = k * (1.0 + (a - 1.0) * ka_ref[:, ls])
    bb = kk * a

    lw_hi, lw_lo = _split_hi_lo(lw)
    g = _dot(tri_incl, lw_hi) + _dot(tri_incl, lw_lo)
    mid = lambda ci: g[ci * c + c // 2 - 1:ci * c + c // 2, :]
    gm = jnp.concatenate([jnp.broadcast_to(mid(ci), (c, hw)) for ci in range(nc)], axis=0)
    e_a = jnp.exp(g - gm)
    e_prev = jnp.exp(g - lw - gm)
    e_inv = jnp.exp(gm - g)
    e1 = [jnp.exp(mid(ci)) for ci in range(nc)]
    e2 = [jnp.exp(g[ci * c + c - 1:ci * c + c, :] - mid(ci)) for ci in range(nc)]

    kkd = (kk * e_prev).astype(BF16)
    rd = (r * e_a).astype(BF16)
    bi = (bb * e_inv).astype(BF16)
    ki = (k2 * e_inv).astype(BF16)
    v_bf = v.astype(BF16)
    zeros_cv = jnp.zeros((c, hd), BF16)

    pairs = [(ci, h) for ci in range(nc) for h in range(nh)]
    rows = lambda ci: slice(ci * c, (ci + 1) * c)
    cols = lambda h: slice(h * hd, (h + 1) * hd)
    xs = {(ci, h): jnp.concatenate([kkd[rows(ci), cols(h)], rd[rows(ci), cols(h)]], axis=0) for ci, h in pairs}
    r1s = {(ci, h): jnp.concatenate([bi[rows(ci), cols(h)], ki[rows(ci), cols(h)]], axis=0) for ci, h in pairs}
    vs = {(ci, h): v_bf[rows(ci), cols(h)] for ci, h in pairs}
    a_mats = {p: _dot_nt(xs[p], r1s[p]) for p in pairs}
    lk_vs = {p: _dot(jnp.where(masks["top_k"], a_mats[p][:c, :], 0.0).astype(BF16),
                     jnp.concatenate([zeros_cv, vs[p]], axis=0)) for p in pairs}
    lps = {p: jnp.where(masks["strict"], a_mats[p][:c, :c], 0.0) for p in pairs}
    ts = {p: masks["eye"] - lps[p] for p in pairs}
    for _ in range(int(math.log2(c)) - 1):
        lpb = {p: lps[p].astype(BF16) for p in pairs}
        lps = {p: _dot(lpb[p], lpb[p]) for p in pairs}
        ts = {p: _dot(ts[p].astype(BF16), (masks["eye"] + lps[p]).astype(BF16)) for p in pairs}
    a_bots = {p: jnp.where(masks["bot"], a_mats[p][c:, :], 0.0).astype(BF16) for p in pairs}
    t_bf = {p: ts[p].astype(BF16) for p in pairs}

    state = [s_ref[h0 + h] for h in range(nh)]
    y_rows = []
    for ci in range(nc):
        hs = range(nh)
        sms = [state[h] * e1[ci][:, cols(h)] for h in hs]
        p_mats = [_dot_nt(xs[ci, h], sms[h].astype(BF16)) for h in hs]
        us = [-_dot(t_bf[ci, h], (p_mats[h][:c, :] + lk_vs[ci, h]).astype(BF16)) for h in hs]
        uvs = [jnp.concatenate([us[h].astype(BF16), vs[ci, h]], axis=0) for h in hs]
        ys = [p_mats[h][c:, :] + _dot(a_bots[ci, h], uvs[h]) for h in hs]
        state = [(sms[h] + _dot_tn(uvs[h], r1s[ci, h])) * e2[ci][:, cols(h)] for h in hs]
        y_rows.append(jnp.concatenate(ys, axis=1))
    for h in range(nh):
        s_ref[h0 + h] = state[h]
    y = jnp.concatenate(y_rows, axis=0) if nc > 1 else y_rows[0]

    inv_n = 1.0 / hd
    mean = _seg_sum(y, ones_bd) * inv_n
    yc = y - mean
    var = _seg_sum(yc * yc, ones_bd) * inv_n
    yn = yc * lax.rsqrt(var + GN_EPS) * gg_ref[:, ls] + gb_ref[:, ls]
    bonus = _seg_sum(r * k2 * rk_ref[:, ls], ones_bd)
    yg_ref[:, ls] = ((yn + bonus * v) * _silu(z_ref[0, :, ls])).astype(BF16)


def wkv_prompt(rkvz, hw_act, ha_act, w2, w0, a2, a0, k_k, k_a, r_k, gn_g, gn_b, n_batch):
    _, m, e = rkvz.shape
    p = m // n_batch
    tb = WKV_ROWS
    nt = p // tb
    hw = HEADS_PER_STEP * HEAD_DIM
    nh = e // HEAD_DIM
    row = lambda i, g, t: (i * nt + t, g)
    proj = lambda q: pl.BlockSpec((1, tb, hw), lambda i, g, t: (q, i * nt + t, g))
    par = pl.BlockSpec((1, hw), lambda i, g, t: (0, g))
    lr = w2.shape[0]
    hid = pl.BlockSpec((tb, lr), lambda i, g, t: (i * nt + t, 0))
    up = pl.BlockSpec((lr, hw), lambda i, g, t: (0, g))
    return pl.pallas_call(
        _wkv_prompt_kernel,
        grid=(n_batch, e // hw, nt),
        in_specs=[proj(0), proj(1), proj(2), proj(3), hid, hid, up, par, up, par,
                  par, par, par, par, par],
        out_specs=[pl.BlockSpec((tb, hw), row),
                   pl.BlockSpec((1, HEADS_PER_STEP, HEAD_DIM, HEAD_DIM), lambda i, g, t: (i, g, 0, 0))],
        out_shape=[jax.ShapeDtypeStruct((m, e), BF16),
                   jax.ShapeDtypeStruct((n_batch, nh, HEAD_DIM, HEAD_DIM), F32)],
        scratch_shapes=[pltpu.VMEM((HEADS_PER_STEP, HEAD_DIM, HEAD_DIM), F32)],
        compiler_params=_cparams(("arbitrary", "arbitrary", "arbitrary")),
        name="wkv_prompt",
    )(rkvz, rkvz, rkvz, rkvz, hw_act, ha_act, w2, w0.reshape(1, e), a2, a0.reshape(1, e),
      k_k.reshape(1, e), k_a.reshape(1, e), r_k.reshape(1, e), gn_g.reshape(1, e), gn_b.reshape(1, e))


def _wkv_sample_kernel(r_ref, k_ref, v_ref, z_ref, wl_ref, al_ref, kk_ref, ka_ref, rk_ref, gg_ref, gb_ref,
                       s_ref, yg_ref, sout_ref, y_scr):
    hd = HEAD_DIM
    r = r_ref[0]
    k = k_ref[0]
    v = v_ref[0]
    a = _sigmoid(al_ref[0])
    d = jnp.exp(-DECAY_SCALE * _sigmoid(wl_ref[0]))
    kk = k * kk_ref[...]
    kk = kk / jnp.maximum(jnp.sqrt(jnp.sum(kk * kk, axis=-1, keepdims=True)), 1e-12)
    k2 = k * (1.0 + (a - 1.0) * ka_ref[...])
    bb = kk * a
    nh = r.shape[0]
    ii = lax.broadcasted_iota(jnp.int32, (hd, hd), 0)
    jj = lax.broadcasted_iota(jnp.int32, (hd, hd), 1)
    eye = ii == jj

    row = lambda x, h: x[h:h + 1, :]
    group = 8
    for h0 in range(0, nh, group):
        hs = range(h0, h0 + group)
        s = {h: s_ref[0, h] for h in hs}
        sa = {h: jnp.sum(s[h] * row(kk, h), axis=-1, keepdims=True) for h in hs}
        v_col = {h: jnp.sum(jnp.where(eye, row(v, h), 0.0), axis=-1, keepdims=True) for h in hs}
        s_new = {h: s[h] * row(d, h) - sa[h] * row(bb, h) + v_col[h] * row(k2, h) for h in hs}
        y_col = {h: jnp.sum(s_new[h] * row(r, h), axis=-1, keepdims=True) for h in hs}
        for h in hs:
            sout_ref[0, h] = s_new[h]
            y_scr[h:h + 1, :] = jnp.sum(jnp.where(eye, y_col[h], 0.0), axis=0, keepdims=True)

    y = y_scr[...]
    mean = jnp.mean(y, axis=-1, keepdims=True)
    yc = y - mean
    var = jnp.mean(yc * yc, axis=-1, keepdims=True)
    yn = yc * lax.rsqrt(var + GN_EPS) * gg_ref[...] + gb_ref[...]
    bonus = jnp.sum(r * k2 * rk_ref[...], axis=-1, keepdims=True)
    yg_ref[0] = ((yn + bonus * v) * _silu(z_ref[0])).astype(BF16)


def wkv_sample(rkvz, wl, al, k_k, k_a, r_k, gn_g, gn_b, state):
    _, m, e = rkvz.shape
    nh = e // HEAD_DIM
    hd = HEAD_DIM
    rkvz4 = rkvz.reshape(4, m, nh, hd)
    proj = lambda q: pl.BlockSpec((None, 1, nh, hd), lambda i: (q, i, 0, 0))
    tok = pl.BlockSpec((1, nh, hd), lambda i: (i, 0, 0))
    par = pl.BlockSpec((nh, hd), lambda i: (0, 0))
    st = pl.BlockSpec((1, nh, hd, hd), lambda i: (i, 0, 0, 0))
    as_heads = lambda x: x.reshape(nh, hd)
    return pl.pallas_call(
        _wkv_sample_kernel,
        grid=(m,),
        in_specs=[proj(0), proj(1), proj(2), proj(3), tok, tok, par, par, par, par, par, st],
        out_specs=[tok, st],
        out_shape=[jax.ShapeDtypeStruct((m, nh, hd), BF16), jax.ShapeDtypeStruct(state.shape, F32)],
        scratch_shapes=[pltpu.VMEM((nh, hd), F32)],
        compiler_params=_cparams(("arbitrary",)),
        name="wkv_sample",
    )(rkvz4, rkvz4, rkvz4, rkvz4, wl.reshape(m, nh, hd), al.reshape(m, nh, hd),
      as_heads(k_k), as_heads(k_a), as_heads(r_k), as_heads(gn_g), as_heads(gn_b), state)


def rope_tables(pos):
    half = ROPE_DIM // 2
    inv_freq = ROPE_THETA ** (-jnp.arange(half, dtype=F32) * 2.0 / ROPE_DIM)
    ang = pos.astype(F32)[:, None] * inv_freq[None, :]
    cos = jnp.cos(ang)
    sin = jnp.sin(ang)
    rows = pos.shape[0]
    ones = jnp.ones((rows, HEAD_DIM - ROPE_DIM), F32)
    zeros_h = jnp.zeros((rows, half), F32)
    zeros_r = jnp.zeros((rows, HEAD_DIM - ROPE_DIM), F32)
    cos_h = jnp.concatenate([cos, cos, ones], axis=1)
    sa_h = jnp.concatenate([-sin, zeros_h, zeros_r], axis=1)
    sb_h = jnp.concatenate([zeros_h, sin, zeros_r], axis=1)
    two = lambda t: jnp.concatenate([t, t], axis=1)
    return two(cos_h), two(sa_h), two(sb_h)


def _attn_prompt_kernel(sink_ref, q_ref, kc_ref, kp_ref, vc_ref, vp_ref, z_ref, o_ref):
    n = pl.program_id(1)
    hd = HEAD_DIM
    blk = q_ref.shape[0]
    n_kv = kc_ref.shape[1] // hd
    grp = q_ref.shape[1] // (n_kv * hd)
    qi = lax.broadcasted_iota(jnp.int32, (blk, 2 * blk), 0)
    kj = lax.broadcasted_iota(jnp.int32, (blk, 2 * blk), 1) - blk
    kpos = n * blk + kj
    diff = qi - kj
    valid = (kpos >= LEAD) & (diff >= 0) & (diff <= WINDOW)
    k_all = jnp.concatenate([kp_ref[...], kc_ref[...]], axis=0).astype(BF16)
    v_all = jnp.concatenate([vp_ref[...], vc_ref[...]], axis=0).astype(BF16)

    def scores(h):
        k_h = k_all[:, h * hd:(h + 1) * hd]
        return [_dot_nt(q_ref[:, (h * grp + gi) * hd:(h * grp + gi + 1) * hd], k_h) for gi in range(grp)]

    outs = []
    s_next = scores(0)
    for h in range(n_kv):
        s_cur = s_next
        if h + 1 < n_kv:
            s_next = scores(h + 1)
        v_h = v_all[:, h * hd:(h + 1) * hd]
        for g0 in range(0, grp, SOFTMAX_BATCH):
            gs = range(g0, min(g0 + SOFTMAX_BATCH, grp))
            sks = {gi: sink_ref[h * grp + gi] for gi in gs}
            ss = {gi: jnp.where(valid, s_cur[gi], -jnp.inf) for gi in gs}
            ms = {gi: jnp.maximum(jnp.max(ss[gi], axis=-1, keepdims=True), sks[gi]) for gi in gs}
            ps = {gi: jnp.exp(ss[gi] - ms[gi]) for gi in gs}
            dens = {gi: jnp.sum(ps[gi], axis=-1, keepdims=True) + jnp.exp(sks[gi] - ms[gi]) for gi in gs}
            outs += [_dot(ps[gi].astype(BF16), v_h) / dens[gi] for gi in gs]
    att = jnp.concatenate(outs, axis=1)
    o_ref[...] = (att * _silu(z_ref[...])).astype(o_ref.dtype)


def attn_prompt(sinks, q, k, v, z, n_batch):
    m, e = q.shape
    nb = m // (n_batch * BLOCK)
    kw = k.shape[1]
    cur = lambda i, n: (i * nb + n, 0)
    prv = lambda i, n: (i * nb + jnp.maximum(n - 1, 0), 0)
    return pl.pallas_call(
        _attn_prompt_kernel,
        grid=(n_batch, nb),
        in_specs=[pl.BlockSpec(memory_space=pltpu.SMEM),
                  pl.BlockSpec((BLOCK, e), cur),
                  pl.BlockSpec((BLOCK, kw), cur), pl.BlockSpec((BLOCK, kw), prv),
                  pl.BlockSpec((BLOCK, kw), cur), pl.BlockSpec((BLOCK, kw), prv),
                  pl.BlockSpec((BLOCK, e), cur)],
        out_specs=pl.BlockSpec((BLOCK, e), cur),
        out_shape=jax.ShapeDtypeStruct((m, e), BF16),
        compiler_params=_cparams(("arbitrary", "arbitrary")),
        name="attn_prompt",
    )(sinks, q, k, k, v, v, z)


def _attn_sample_kernel(sink_ref, q_ref, kc_ref, vc_ref, kn_ref, vn_ref, z_ref, o_ref, ko_ref, vo_ref):
    hd = HEAD_DIM
    win = kc_ref.shape[1]
    n_kv = kc_ref.shape[2] // hd
    nq = q_ref.shape[1]
    grp = nq // n_kv
    pad = 8
    kc = kc_ref[0]
    vc = vc_ref[0]
    kn = kn_ref[0]
    vn = vn_ref[0]
    first = lax.broadcasted_iota(jnp.int32, (pad, kc.shape[1]), 0) == 0
    k_all = jnp.concatenate([kc, jnp.where(first, kn, 0.0)], axis=0).astype(BF16)
    v_all = jnp.concatenate([vc, jnp.where(first, vn, 0.0)], axis=0).astype(BF16)
    col = lax.broadcasted_iota(jnp.int32, (grp, win + pad), 1)
    valid = (col <= win) & (win - col <= WINDOW)
    q = q_ref[0].astype(BF16)
    row_i = lax.broadcasted_iota(jnp.int32, (grp, 1), 0)
    hs = range(n_kv)
    sks = []
    for h in hs:
        sk = jnp.zeros((grp, 1), F32)
        for gi in range(grp):
            sk = jnp.where(row_i == gi, sink_ref[h * grp + gi], sk)
        sks.append(sk)
    ss = [jnp.where(valid, _dot_nt(q[h * grp:(h + 1) * grp, :], k_all[:, h * hd:(h + 1) * hd]), -jnp.inf)
          for h in hs]
    ms = [jnp.maximum(jnp.max(ss[h], axis=-1, keepdims=True), sks[h]) for h in hs]
    ps = [jnp.exp(ss[h] - ms[h]) for h in hs]
    dens = [jnp.sum(ps[h], axis=-1, keepdims=True) + jnp.exp(sks[h] - ms[h]) for h in hs]
    outs = [_dot(ps[h].astype(BF16), v_all[:, h * hd:(h + 1) * hd]) / dens[h] for h in hs]
    att = jnp.concatenate(outs, axis=0)
    o_ref[0] = (att * _silu(z_ref[0])).astype(o_ref.dtype)
    last = lax.broadcasted_iota(jnp.int32, kc.shape, 0) == win - 1
    ko_ref[0] = jnp.where(last, kn, pltpu.roll(kc, win - 1, axis=0))
    vo_ref[0] = jnp.where(last, vn, pltpu.roll(vc, win - 1, axis=0))


def attn_sample(sinks, q, cache_k, cache_v, k_new, v_new, z):
    m, win, kw = cache_k.shape
    nq = q.shape[1]
    hd = HEAD_DIM
    tok = pl.BlockSpec((1, nq, hd), lambda i: (i, 0, 0))
    cache = pl.BlockSpec((1, win, kw), lambda i: (i, 0, 0))
    new = pl.BlockSpec((1, 1, kw), lambda i: (i, 0, 0))
    return pl.pallas_call(
        _attn_sample_kernel,
        grid=(m,),
        in_specs=[pl.BlockSpec(memory_space=pltpu.SMEM), tok, cache, cache, new, new, tok],
        out_specs=[tok, cache, cache],
        out_shape=[jax.ShapeDtypeStruct((m, nq, hd), BF16),
                   jax.ShapeDtypeStruct(cache_k.shape, F32), jax.ShapeDtypeStruct(cache_v.shape, F32)],
        compiler_params=_cparams(("arbitrary",)),
        name="attn_sample",
    )(sinks, q, cache_k, cache_v, k_new, v_new, z)


def _pad_lora(w_down, w_up):
    r = w_down.shape[1]
    return (jnp.pad(w_down, ((0, 0), (0, LORA_PAD - r))).astype(BF16),
            jnp.pad(w_up, ((0, LORA_PAD - r), (0, 0))).astype(BF16))


def kernel(x_prompt, x_sample, state_wkv, state_shift, cache_k, cache_v, meta_tokens, a_norm, a_mu, a_w_rkvz,
           a_w0, a_w1, a_w2, a_a0, a_a1, a_a2, a_k_k, a_k_a, a_r_k, a_gn_g, a_gn_b, a_w_out, kv_norm, w_kv,
           b_norm, b_w_qz, b_sinks, b_w_o, final_norm):
    nb, seq, d = x_prompt.shape
    db, dseq, _ = x_sample.shape
    assert dseq == 1 and a_norm.shape[0] == 1 and b_norm.shape[0] == 1
    e = a_w_rkvz.shape[3]
    win = cache_k.shape[1]
    p_len = LEAD + N_META + seq
    assert p_len % BLOCK == 0 and (LEAD + N_META) == BLOCK
    kvw = N_KV_HEADS * HEAD_DIM

    w_rkvz = a_w_rkvz[0].astype(BF16)
    w1, w2 = _pad_lora(a_w1[0], a_w2[0])
    a1, a2 = _pad_lora(a_a1[0], a_a2[0])
    w_out = a_w_out[0].astype(BF16)
    w_kv_bf = w_kv.astype(BF16)
    w_qz = b_w_qz[0].astype(BF16)
    w_o = b_w_o[0].astype(BF16)
    mu = a_mu[0]
    sinks = b_sinks[0]
    gains_b = jnp.stack([kv_norm, b_norm[0]])

    tm = p_len // 8

    head = jnp.concatenate([jnp.zeros((LEAD, d), F32), meta_tokens], axis=0)
    xm, hw_p, ha_p, x_last = norm_shift_prompt(x_prompt, head, a_norm[0], mu, w1, a1)
    p_state_shift = x_last.reshape(1, nb, d)
    rkvz = matmul_groups(xm, w_rkvz, tm, F32)
    yg, p_state = wkv_prompt(rkvz, hw_p, ha_p, w2, a_w0[0], a2, a_a0[0], a_k_k[0], a_k_a[0],
                             a_r_k[0].reshape(-1), a_gn_g[0], a_gn_b[0], nb)
    hp, hn_kv, hn_b = matmul_residual_norm_blocks(yg, w_out, x_prompt, gains_b, nb, BF16, True, head=head)

    pos_p = jnp.maximum(jnp.arange(p_len, dtype=jnp.int32) - LEAD, 0)
    tabs_p = tuple(jnp.tile(t, (nb, 1)) for t in rope_tables(pos_p))
    k_p, v_p = matmul_rope(hn_kv, w_kv_bf, tabs_p, tm, kvw, (F32, F32))
    q_p, = matmul_rope(hn_b, w_qz, tabs_p, tm, e, (BF16,), scale=Q_SCALE, n=e)
    z_p = matmul_groups(hn_b[None], w_qz[None], tm, F32, n=e, col=1)[0]
    att = attn_prompt(sinks, q_p, k_p, v_p, z_p, nb)
    y_prompt, = matmul_residual_norm_blocks(att, w_o, hp, final_norm[None], nb, F32, False,
                                            first_block=(LEAD + N_META) // BLOCK)
    y_prompt = y_prompt.reshape(nb, seq, d)
    tail = lambda t: t.reshape(nb, p_len, kvw)[:, -win:].reshape(nb, win, N_KV_HEADS, HEAD_DIM)
    p_cache_k = tail(k_p)
    p_cache_v = tail(v_p)

    hs = x_sample.reshape(db, d)
    xm_s, hw_s, ha_s, xn_s = norm_shift_sample(hs, state_shift[0], a_norm[0], mu, w1, a1)
    rkvz_s = matmul_groups(xm_s, w_rkvz, db, F32)
    wl_s, al_s = lora_up(hw_s, ha_s, w2, a_w0[0], a2, a_a0[0])
    yg_s, s_state = wkv_sample(rkvz_s, wl_s, al_s, a_k_k[0], a_k_a[0], a_r_k[0].reshape(-1), a_gn_g[0],
                               a_gn_b[0], state_wkv[0])
    hs, hn_kv_s, hn_b_s = matmul_residual_norm(yg_s.reshape(db, e), w_out, hs, gains_b, db, BF16)
    tabs_s = rope_tables(jnp.full((db,), PAST_LEN, jnp.int32))
    k_s, v_s = matmul_rope(hn_kv_s, w_kv_bf, tabs_s, db, kvw, (F32, F32))
    q_s, = matmul_rope(hn_b_s, w_qz, tabs_s, db, e, (F32,), scale=Q_SCALE, n=e)
    z_s = matmul_groups(hn_b_s[None], w_qz[None], db, F32, n=e, col=1)[0]
    nq = e // HEAD_DIM
    att_s, s_cache_k, s_cache_v = attn_sample(
        sinks, q_s.reshape(db, nq, HEAD_DIM), cache_k.reshape(db, win, kvw), cache_v.reshape(db, win, kvw),
        k_s.reshape(db, 1, kvw), v_s.reshape(db, 1, kvw), z_s.reshape(db, nq, HEAD_DIM))
    y_s, = matmul_residual_norm(att_s.reshape(db, e), w_o, hs, final_norm[None], db, F32, emit_h=False)
    y_sample = y_s.reshape(db, 1, d)

    return (y_prompt, y_sample, p_state[None], p_state_shift,
            p_cache_k, p_cache_v,
            s_state[None], xn_s[None],
            s_cache_k.reshape(cache_k.shape), s_cache_v.reshape(cache_v.shape))
```

```python
import functools
import math

import jax
import jax.numpy as jnp
from jax import lax
from jax.experimental import pallas as pl
from jax.experimental.pallas import tpu as pltpu

F32 = jnp.float32
BF16 = jnp.bfloat16

HEAD_DIM = 64
N_KV_HEADS = 8
WINDOW = 128
BLOCK = 128
ROPE_DIM = HEAD_DIM // 4
ROPE_THETA = 500000.0
N_META = 16
PAST_LEN = 16384
RMS_EPS = 1e-6
GN_EPS = 64e-5
LEAD = (-N_META) % BLOCK
CHUNK = 64
WKV_ROWS = 128
HEADS_PER_STREAM = 32
HEADS_PER_STEP = 32
LORA_PAD = 128
MXU_TILE = 256
ROPE_SLAB = 512
VMEM_LIMIT = 48 * 1024 * 1024
DECAY_SCALE = math.exp(-0.5)
SOFTMAX_BATCH = 4
Q_SCALE = HEAD_DIM ** -0.5


def _cparams(sem):
    return pltpu.CompilerParams(dimension_semantics=sem, vmem_limit_bytes=VMEM_LIMIT)


def _sigmoid(x):
    return 1.0 / (1.0 + jnp.exp(-x))


def _silu(x):
    return x * _sigmoid(x)


def _dot(a, b):
    return jnp.dot(a, b, preferred_element_type=F32)


def _dot_nt(a, b):
    return lax.dot_general(a, b, (((1,), (1,)), ((), ())), preferred_element_type=F32)


def _dot_tn(a, b):
    return lax.dot_general(a, b, (((0,), (0,)), ((), ())), preferred_element_type=F32)


def _split_hi_lo(x):
    hi = x.astype(BF16)
    lo = (x - hi.astype(F32)).astype(BF16)
    return hi, lo


def _mixes(xn, prev, mu_ref, w1_ref, a1_ref, xm_ref, hw_ref, ha_ref):
    xx = prev - xn
    n_proj = xm_ref.shape[0]
    for p in range(n_proj):
        xm_ref[p] = (xn + xx * mu_ref[p:p + 1, :]).astype(xm_ref.dtype)
    xw = (xn + xx * mu_ref[n_proj:n_proj + 1, :]).astype(BF16)
    xa = (xn + xx * mu_ref[n_proj + 1:n_proj + 2, :]).astype(BF16)
    hw_ref[...] = jnp.tanh(_dot(xw, w1_ref[...])).astype(hw_ref.dtype)
    ha_ref[...] = _dot(xa, a1_ref[...]).astype(ha_ref.dtype)


def _padded_rows(x_ref, head_ref, first_block):
    return jnp.where(pl.program_id(1) + first_block == 0, head_ref[...], x_ref[0])


def _norm_shift_kernel(x_ref, head_ref, g_ref, mu_ref, w1_ref, a1_ref, xm_ref, hw_ref, ha_ref, last_ref, carry_ref):
    @pl.when(pl.program_id(1) == 0)
    def _():
        carry_ref[...] = jnp.zeros_like(carry_ref)

    x = _padded_rows(x_ref, head_ref, 0)
    tm = x.shape[0]
    xn = x * lax.rsqrt(jnp.mean(x * x, axis=-1, keepdims=True) + RMS_EPS) * g_ref[...]
    rolled = pltpu.roll(xn, 1, axis=0)
    row = lax.broadcasted_iota(jnp.int32, xn.shape, 0)
    prev = jnp.where(row == 0, carry_ref[0:1, :], rolled)
    _mixes(xn, prev, mu_ref, w1_ref, a1_ref, xm_ref, hw_ref, ha_ref)
    carry_ref[0:1, :] = xn[tm - 1:tm, :]
    last_ref[0] = xn[tm - 1:tm, :]


def norm_shift_prompt(x, head, g, mu, w1, a1):
    b, seq, d = x.shape
    tm = BLOCK
    p = tm + seq
    n_mix = mu.shape[0]
    n_proj = n_mix - 2
    lr = w1.shape[1]
    nt = p // tm
    const = lambda shape: pl.BlockSpec(shape, lambda i, t: (0,) * len(shape))
    hid = pl.BlockSpec((tm, lr), lambda i, t: (i * nt + t, 0))
    return pl.pallas_call(
        _norm_shift_kernel,
        grid=(b, nt),
        in_specs=[pl.BlockSpec((1, tm, d), lambda i, t: (i, jnp.maximum(t - 1, 0), 0)),
                  const((tm, d)), const((1, d)), const((n_mix, d)), const((d, lr)), const((d, lr))],
        out_specs=[pl.BlockSpec((n_proj, tm, d), lambda i, t: (0, i * nt + t, 0)), hid, hid,
                   pl.BlockSpec((1, 1, d), lambda i, t: (i, 0, 0))],
        out_shape=[jax.ShapeDtypeStruct((n_proj, b * p, d), BF16),
                   jax.ShapeDtypeStruct((b * p, lr), BF16), jax.ShapeDtypeStruct((b * p, lr), BF16),
                   jax.ShapeDtypeStruct((b, 1, d), F32)],
        scratch_shapes=[pltpu.VMEM((8, d), F32)],
        compiler_params=_cparams(("arbitrary", "arbitrary")),
        name="norm_shift_prompt",
    )(x, head, g.reshape(1, d), mu, w1, a1)


def _norm_shift_sample_kernel(x_ref, prev_ref, g_ref, mu_ref, w1_ref, a1_ref, xm_ref, hw_ref, ha_ref, xn_ref):
    x = x_ref[...]
    xn = x * lax.rsqrt(jnp.mean(x * x, axis=-1, keepdims=True) + RMS_EPS) * g_ref[...]
    xn_ref[...] = xn
    _mixes(xn, prev_ref[...], mu_ref, w1_ref, a1_ref, xm_ref, hw_ref, ha_ref)


def norm_shift_sample(x, prev, g, mu, w1, a1):
    m, d = x.shape
    lr = w1.shape[1]
    return pl.pallas_call(
        _norm_shift_sample_kernel,
        out_shape=[jax.ShapeDtypeStruct((mu.shape[0] - 2, m, d), BF16),
                   jax.ShapeDtypeStruct((m, lr), BF16), jax.ShapeDtypeStruct((m, lr), BF16),
                   jax.ShapeDtypeStruct((m, d), F32)],
        name="norm_shift_sample",
    )(x, prev, g.reshape(1, d), mu, w1, a1)


def _rope(y, cos, sin_a, sin_b):
    half = ROPE_DIM // 2
    step = ROPE_SLAB
    rep = step // cos.shape[1]
    tile = lambda t: jnp.concatenate([t] * rep, axis=1)
    cos_t, sa_t, sb_t = tile(cos), tile(sin_a), tile(sin_b)
    outs = []
    for j in range(y.shape[1] // step):
        ys = y[:, j * step:(j + 1) * step]
        outs.append(ys * cos_t + pltpu.roll(ys, step - half, axis=1) * sa_t + pltpu.roll(ys, half, axis=1) * sb_t)
    return jnp.concatenate(outs, axis=1) if len(outs) > 1 else outs[0]


def _mm_group_kernel(x_ref, w_ref, o_ref):
    o_ref[0] = _dot(x_ref[0], w_ref[0]).astype(o_ref.dtype)


def matmul_groups(x, w, tm, out_dtype, n=None, col=0):
    g = w.shape[0]
    n = w.shape[2] if n is None else n
    _, m, kdim = x.shape
    return pl.pallas_call(
        _mm_group_kernel,
        grid=(g, m // tm),
        in_specs=[pl.BlockSpec((1, tm, kdim), lambda q, i: (q, i, 0)),
                  pl.BlockSpec((1, kdim, n), lambda q, i: (q, 0, col))],
        out_specs=pl.BlockSpec((1, tm, n), lambda q, i: (q, i, 0)),
        out_shape=jax.ShapeDtypeStruct((g, m, n), out_dtype),
        compiler_params=_cparams(("arbitrary", "arbitrary")),
        name="matmul_groups",
    )(x, w)


def _mm_rope_kernel(x_ref, w_ref, cos_ref, sa_ref, sb_ref, *o_refs, n_rope, scale):
    y = _dot(x_ref[...], w_ref[...])
    rot = _rope(y[:, :n_rope], cos_ref[...], sa_ref[...], sb_ref[...])
    if scale != 1.0:
        rot = rot * scale
    o_refs[0][...] = rot.astype(o_refs[0].dtype)
    if len(o_refs) > 1:
        o_refs[1][...] = y[:, n_rope:].astype(o_refs[1].dtype)


def matmul_rope(x, w, tables, tm, n_rope, out_dtypes, scale=1.0, n=None):
    m, kdim = x.shape
    n = w.shape[-1] if n is None else n
    w_block = (kdim, n) if w.ndim == 2 else (None, kdim, n)
    lanes = tables[0].shape[1]
    widths = [n_rope] + ([n - n_rope] if n > n_rope else [])
    tab = pl.BlockSpec((tm, lanes), lambda i: (i, 0))
    outs = pl.pallas_call(
        functools.partial(_mm_rope_kernel, n_rope=n_rope, scale=scale),
        grid=(m // tm,),
        in_specs=[pl.BlockSpec((tm, kdim), lambda i: (i, 0)),
                  pl.BlockSpec(w_block, lambda i: (0,) * w.ndim, pipeline_mode=pl.Buffered(1)),
                  tab, tab, tab],
        out_specs=[pl.BlockSpec((tm, wd), lambda i: (i, 0)) for wd in widths],
        out_shape=[jax.ShapeDtypeStruct((m, wd), dt) for wd, dt in zip(widths, out_dtypes)],
        compiler_params=_cparams(("arbitrary",)),
        name="matmul_rope",
    )(x, w, *tables)
    return outs


def _mm_res_norm_kernel(x_ref, w_ref, res_ref, g_ref, *out_refs, emit_h):
    h = res_ref[...] + _dot(x_ref[...], w_ref[...])
    hn_refs = out_refs
    if emit_h:
        out_refs[0][...] = h
        hn_refs = out_refs[1:]
    inv = lax.rsqrt(jnp.mean(h * h, axis=-1, keepdims=True) + RMS_EPS)
    for j, hn_ref in enumerate(hn_refs):
        hn_ref[...] = (h * inv * g_ref[j:j + 1, :]).astype(hn_ref.dtype)


def matmul_residual_norm(x, w, res, gains, tm, norm_dtype, emit_h=True):
    m, kdim = x.shape
    n = w.shape[1]
    ng = gains.shape[0]
    row = lambda width: pl.BlockSpec((tm, width), lambda i: (i, 0))
    return pl.pallas_call(
        functools.partial(_mm_res_norm_kernel, emit_h=emit_h),
        grid=(m // tm,),
        in_specs=[row(kdim),
                  pl.BlockSpec((kdim, n), lambda i: (0, 0), pipeline_mode=pl.Buffered(1)),
                  row(n),
                  pl.BlockSpec((ng, n), lambda i: (0, 0))],
        out_specs=[row(n)] * (int(emit_h) + ng),
        out_shape=[jax.ShapeDtypeStruct((m, n), F32)] * int(emit_h) + [jax.ShapeDtypeStruct((m, n), norm_dtype)] * ng,
        compiler_params=_cparams(("arbitrary",)),
        name="matmul_residual_norm",
    )(x, w, res, gains)


def _mm_res_norm_blocks_kernel(x_ref, w_ref, res_ref, *rest, emit_h, head, first_block):
    if head:
        head_ref, g_ref, *out_refs = rest
        res = _padded_rows(res_ref, head_ref, first_block)
    else:
        g_ref, *out_refs = rest
        res = res_ref[...]
    h = res + _dot(x_ref[...], w_ref[...])
    hn_refs = out_refs
    if emit_h:
        out_refs[0][...] = h
        hn_refs = out_refs[1:]
    inv = lax.rsqrt(jnp.mean(h * h, axis=-1, keepdims=True) + RMS_EPS)
    for j, hn_ref in enumerate(hn_refs):
        hn_ref[...] = (h * inv * g_ref[j:j + 1, :]).astype(hn_ref.dtype)


def matmul_residual_norm_blocks(x, w, res, gains, n_batch, norm_dtype, emit_h, head=None, first_block=0):
    m, kdim = x.shape
    n = w.shape[1]
    ng = gains.shape[0]
    nb = m // (n_batch * BLOCK)
    nb_out = nb - first_block
    padded = lambda width: pl.BlockSpec((BLOCK, width), lambda i, t: (i * nb + t + first_block, 0))
    compact = lambda width: pl.BlockSpec((BLOCK, width), lambda i, t: (i * nb_out + t, 0))
    const = lambda shape: pl.BlockSpec(shape, lambda i, t: (0,) * len(shape))
    if head is not None:
        res_specs = [pl.BlockSpec((1, BLOCK, n), lambda i, t: (i, jnp.maximum(t + first_block - 1, 0), 0)),
                     const((BLOCK, n))]
        res_args = [res, head]
    else:
        res_specs, res_args = [padded(n)], [res]
    norm_spec = compact(n) if first_block else padded(n)
    norm_rows = n_batch * nb_out * BLOCK
    return pl.pallas_call(
        functools.partial(_mm_res_norm_blocks_kernel, emit_h=emit_h, head=head is not None, first_block=first_block),
        grid=(n_batch, nb_out),
        in_specs=[padded(kdim), pl.BlockSpec((kdim, n), lambda i, t: (0, 0), pipeline_mode=pl.Buffered(1))]
        + res_specs + [const((ng, n))],
        out_specs=[padded(n)] * int(emit_h) + [norm_spec] * ng,
        out_shape=[jax.ShapeDtypeStruct((m, n), F32)] * int(emit_h)
        + [jax.ShapeDtypeStruct((norm_rows, n), norm_dtype)] * ng,
        compiler_params=_cparams(("arbitrary", "arbitrary")),
        name="matmul_residual_norm_blocks",
    )(x, w, *res_args, gains)


def _lora_up_kernel(hw_ref, ha_ref, w2_ref, w0_ref, a2_ref, a0_ref, wl_ref, al_ref):
    wl_ref[...] = w0_ref[...] + _dot(hw_ref[...], w2_ref[...])
    al_ref[...] = a0_ref[...] + _dot(ha_ref[...], a2_ref[...])


def lora_up(hw, ha, w2, w0, a2, a0):
    m = hw.shape[0]
    e = w2.shape[1]
    return pl.pallas_call(
        _lora_up_kernel,
        out_shape=[jax.ShapeDtypeStruct((m, e), F32), jax.ShapeDtypeStruct((m, e), F32)],
        name="lora_up",
    )(hw, ha, w2, w0.reshape(1, e), a2, a0.reshape(1, e))


def _seg_sum(x, ones_bd):
    hi = x.astype(BF16)
    outs = []
    for c in range(x.shape[1] // MXU_TILE):
        sl = slice(c * MXU_TILE, (c + 1) * MXU_TILE)
        outs.append(_dot(hi[:, sl], ones_bd))
    return jnp.concatenate(outs, axis=1) if len(outs) > 1 else outs[0]


def _wkv_prompt_kernel(r_ref, k_ref, v_ref, z_ref, hw_ref, ha_ref, w2_ref, w0_ref, a2_ref, a0_ref,
                       kk_ref, ka_ref, rk_ref, gg_ref, gb_ref, yg_ref, sout_ref, s_ref):
    t_idx = pl.program_id(2)
    c = CHUNK
    hd = HEAD_DIM

    @pl.when(t_idx == 0)
    def _():
        s_ref[...] = jnp.zeros_like(s_ref)

    tb = r_ref.shape[1]
    nh = HEADS_PER_STREAM
    hw = nh * hd
    nc = tb // c

    li = lax.broadcasted_iota(jnp.int32, (MXU_TILE, MXU_TILE), 0) // hd
    lj = lax.broadcasted_iota(jnp.int32, (MXU_TILE, MXU_TILE), 1) // hd
    ones_bd = jnp.where(li == lj, 1.0, 0.0).astype(BF16)
    bi_ = lax.broadcasted_iota(jnp.int32, (tb, tb), 0)
    bj_ = lax.broadcasted_iota(jnp.int32, (tb, tb), 1)
    tri_incl = jnp.where((bj_ <= bi_) & (bj_ // c == bi_ // c), 1.0, 0.0).astype(BF16)
    ti = lax.broadcasted_iota(jnp.int32, (c, c), 0)
    tj = lax.broadcasted_iota(jnp.int32, (c, c), 1)
    ai = lax.broadcasted_iota(jnp.int32, (c, 2 * c), 0)
    aj = lax.broadcasted_iota(jnp.int32, (c, 2 * c), 1)
    upper = aj >= c
    aj_mod = jnp.where(upper, aj - c, aj)
    masks = dict(
        strict=tj < ti,
        eye=jnp.where(ti == tj, 1.0, 0.0).astype(F32),
        top_k=upper & (aj_mod < ai),
        bot=aj_mod <= ai)

    for st in range(r_ref.shape[2] // hw):
        _wkv_stream(st, hw, nc, r_ref, k_ref, v_ref, z_ref, hw_ref, ha_ref, w2_ref, w0_ref, a2_ref, a0_ref,
                    kk_ref, ka_ref, rk_ref, gg_ref, gb_ref, yg_ref, s_ref, ones_bd, tri_incl, masks)

    @pl.when(t_idx == pl.num_programs(2) - 1)
    def _():
        sout_ref[0] = s_ref[...]


def _wkv_stream(st, hw, nc, r_ref, k_ref, v_ref, z_ref, hw_ref, ha_ref, w2_ref, w0_ref, a2_ref, a0_ref,
                kk_ref, ka_ref, rk_ref, gg_ref, gb_ref, yg_ref, s_ref, ones_bd, tri_incl, masks):
    c = CHUNK
    hd = HEAD_DIM
    nh = hw // hd
    ls = slice(st * hw, (st + 1) * hw)
    h0 = st * nh
    r = r_ref[0, :, ls]
    k = k_ref[0, :, ls]
    v = v_ref[0, :, ls]
    wl = w0_ref[:, ls] + _dot(hw_ref[...], w2_ref[:, ls])
    al = a0_ref[:, ls] + _dot(ha_ref[...], a2_ref[:, ls])
    a = _sigmoid(al)
    lw = -DECAY_SCALE * _sigmoid(wl)
    kk = k * kk_ref[:, ls]
    n2 = _seg_sum(kk * kk, ones_bd)
    kk = kk / jnp.maximum(jnp.sqrt(n2), 1e-12)
    k2 = k * (1.0 + (a - 1.0) * ka_ref[:, ls])
    bb = kk * a

    lw_hi, lw_lo = _split_hi_lo(lw)
    g = _dot(tri_incl, lw_hi) + _dot(tri_incl, lw_lo)
    mid = lambda ci: g[ci * c + c // 2 - 1:ci * c + c // 2, :]
    gm = jnp.concatenate([jnp.broadcast_to(mid(ci), (c, hw)) for ci in range(nc)], axis=0)
    e_a = jnp.exp(g - gm)
    e_prev = jnp.exp(g - lw - gm)
    e_inv = jnp.exp(gm - g)
    e1 = [jnp.exp(mid(ci)) for ci in range(nc)]
    e2 = [jnp.exp(g[ci * c + c - 1:ci * c + c, :] - mid(ci)) for ci in range(nc)]

    kkd = (kk * e_prev).astype(BF16)
    rd = (r * e_a).astype(BF16)
    bi = (bb * e_inv).astype(BF16)
    ki = (k2 * e_inv).astype(BF16)
    v_bf = v.astype(BF16)
    zeros_cv = jnp.zeros((c, hd), BF16)

    pairs = [(ci, h) for ci in range(nc) for h in range(nh)]
    rows = lambda ci: slice(ci * c, (ci + 1) * c)
    cols = lambda h: slice(h * hd, (h + 1) * hd)
    xs = {(ci, h): jnp.concatenate([kkd[rows(ci), cols(h)], rd[rows(ci), cols(h)]], axis=0) for ci, h in pairs}
    r1s = {(ci, h): jnp.concatenate([bi[rows(ci), cols(h)], ki[rows(ci), cols(h)]], axis=0) for ci, h in pairs}
    vs = {(ci, h): v_bf[rows(ci), cols(h)] for ci, h in pairs}
    a_mats = {p: _dot_nt(xs[p], r1s[p]) for p in pairs}
    lk_vs = {p: _dot(jnp.where(masks["top_k"], a_mats[p][:c, :], 0.0).astype(BF16),
                     jnp.concatenate([zeros_cv, vs[p]], axis=0)) for p in pairs}
    lps = {p: jnp.where(masks["strict"], a_mats[p][:c, :c], 0.0) for p in pairs}
    ts = {p: masks["eye"] - lps[p] for p in pairs}
    for _ in range(int(math.log2(c)) - 1):
        lpb = {p: lps[p].astype(BF16) for p in pairs}
        lps = {p: _dot(lpb[p], lpb[p]) for p in pairs}
        ts = {p: _dot(ts[p].astype(BF16), (masks["eye"] + lps[p]).astype(BF16)) for p in pairs}
    a_bots = {p: jnp.where(masks["bot"], a_mats[p][c:, :], 0.0).astype(BF16) for p in pairs}
    t_bf = {p: ts[p].astype(BF16) for p in pairs}

    state = [s_ref[h0 + h] for h in range(nh)]
    y_rows = []
    for ci in range(nc):
        hs = range(nh)
        sms = [state[h] * e1[ci][:, cols(h)] for h in hs]
        p_mats = [_dot_nt(xs[ci, h], sms[h].astype(BF16)) for h in hs]
        us = [-_dot(t_bf[ci, h], (p_mats[h][:c, :] + lk_vs[ci, h]).astype(BF16)) for h in hs]
        uvs = [jnp.concatenate([us[h].astype(BF16), vs[ci, h]], axis=0) for h in hs]
        ys = [p_mats[h][c:, :] + _dot(a_bots[ci, h], uvs[h]) for h in hs]
        state = [(sms[h] + _dot_tn(uvs[h], r1s[ci, h])) * e2[ci][:, cols(h)] for h in hs]
        y_rows.append(jnp.concatenate(ys, axis=1))
    for h in range(nh):
        s_ref[h0 + h] = state[h]
    y = jnp.concatenate(y_rows, axis=0) if nc > 1 else y_rows[0]

    inv_n = 1.0 / hd
    mean = _seg_sum(y, ones_bd) * inv_n
    yc = y - mean
    var = _seg_sum(yc * yc, ones_bd) * inv_n
    yn = yc * lax.rsqrt(var + GN_EPS) * gg_ref[:, ls] + gb_ref[:, ls]
    bonus = _seg_sum(r * k2 * rk_ref[:, ls], ones_bd)
    yg_ref[:, ls] = ((yn + bonus * v) * _silu(z_ref[0, :, ls])).astype(BF16)


def wkv_prompt(rkvz, hw_act, ha_act, w2, w0, a2, a0, k_k, k_a, r_k, gn_g, gn_b, n_batch):
    _, m, e = rkvz.shape
    p = m // n_batch
    tb = WKV_ROWS
    nt = p // tb
    hw = HEADS_PER_STEP * HEAD_DIM
    nh = e // HEAD_DIM
    row = lambda i, g, t: (i * nt + t, g)
    proj = lambda q: pl.BlockSpec((1, tb, hw), lambda i, g, t: (q, i * nt + t, g))
    par = pl.BlockSpec((1, hw), lambda i, g, t: (0, g))
    lr = w2.shape[0]
    hid = pl.BlockSpec((tb, lr), lambda i, g, t: (i * nt + t, 0))
    up = pl.BlockSpec((lr, hw), lambda i, g, t: (0, g))
    return pl.pallas_call(
        _wkv_prompt_kernel,
        grid=(n_batch, e // hw, nt),
        in_specs=[proj(0), proj(1), proj(2), proj(3), hid, hid, up, par, up, par,
                  par, par, par, par, par],
        out_specs=[pl.BlockSpec((tb, hw), row),
                   pl.BlockSpec((1, HEADS_PER_STEP, HEAD_DIM, HEAD_DIM), lambda i, g, t: (i, g, 0, 0))],
        out_shape=[jax.ShapeDtypeStruct((m, e), BF16),
                   jax.ShapeDtypeStruct((n_batch, nh, HEAD_DIM, HEAD_DIM), F32)],
        scratch_shapes=[pltpu.VMEM((HEADS_PER_STEP, HEAD_DIM, HEAD_DIM), F32)],
        compiler_params=_cparams(("arbitrary", "arbitrary", "arbitrary")),
        name="wkv_prompt",
    )(rkvz, rkvz, rkvz, rkvz, hw_act, ha_act, w2, w0.reshape(1, e), a2, a0.reshape(1, e),
      k_k.reshape(1, e), k_a.reshape(1, e), r_k.reshape(1, e), gn_g.reshape(1, e), gn_b.reshape(1, e))


def _wkv_sample_kernel(r_ref, k_ref, v_ref, z_ref, wl_ref, al_ref, kk_ref, ka_ref, rk_ref, gg_ref, gb_ref,
                       s_ref, yg_ref, sout_ref, y_scr):
    hd = HEAD_DIM
    r = r_ref[0]
    k = k_ref[0]
    v = v_ref[0]
    a = _sigmoid(al_ref[0])
    d = jnp.exp(-DECAY_SCALE * _sigmoid(wl_ref[0]))
    kk = k * kk_ref[...]
    kk = kk / jnp.maximum(jnp.sqrt(jnp.sum(kk * kk, axis=-1, keepdims=True)), 1e-12)
    k2 = k * (1.0 + (a - 1.0) * ka_ref[...])
    bb = kk * a
    nh = r.shape[0]
    ii = lax.broadcasted_iota(jnp.int32, (hd, hd), 0)
    jj = lax.broadcasted_iota(jnp.int32, (hd, hd), 1)
    eye = ii == jj

    row = lambda x, h: x[h:h + 1, :]
    group = 8
    for h0 in range(0, nh, group):
        hs = range(h0, h0 + group)
        s = {h: s_ref[0, h] for h in hs}
        sa = {h: jnp.sum(s[h] * row(kk, h), axis=-1, keepdims=True) for h in hs}
        v_col = {h: jnp.sum(jnp.where(eye, row(v, h), 0.0), axis=-1, keepdims=True) for h in hs}
        s_new = {h: s[h] * row(d, h) - sa[h] * row(bb, h) + v_col[h] * row(k2, h) for h in hs}
        y_col = {h: jnp.sum(s_new[h] * row(r, h), axis=-1, keepdims=True) for h in hs}
        for h in hs:
            sout_ref[0, h] = s_new[h]
            y_scr[h:h + 1, :] = jnp.sum(jnp.where(eye, y_col[h], 0.0), axis=0, keepdims=True)

    y = y_scr[...]
    mean = jnp.mean(y, axis=-1, keepdims=True)
    yc = y - mean
    var = jnp.mean(yc * yc, axis=-1, keepdims=True)
    yn = yc * lax.rsqrt(var + GN_EPS) * gg_ref[...] + gb_ref[...]
    bonus = jnp.sum(r * k2 * rk_ref[...], axis=-1, keepdims=True)
    yg_ref[0] = ((yn + bonus * v) * _silu(z_ref[0])).astype(BF16)


def wkv_sample(rkvz, wl, al, k_k, k_a, r_k, gn_g, gn_b, state):
    _, m, e = rkvz.shape
    nh = e // HEAD_DIM
    hd = HEAD_DIM
    rkvz4 = rkvz.reshape(4, m, nh, hd)
    proj = lambda q: pl.BlockSpec((None, 1, nh, hd), lambda i: (q, i, 0, 0))
    tok = pl.BlockSpec((1, nh, hd), lambda i: (i, 0, 0))
    par = pl.BlockSpec((nh, hd), lambda i: (0, 0))
    st = pl.BlockSpec((1, nh, hd, hd), lambda i: (i, 0, 0, 0))
    as_heads = lambda x: x.reshape(nh, hd)
    return pl.pallas_call(
        _wkv_sample_kernel,
        grid=(m,),
        in_specs=[proj(0), proj(1), proj(2), proj(3), tok, tok, par, par, par, par, par, st],
        out_specs=[tok, st],
        out_shape=[jax.ShapeDtypeStruct((m, nh, hd), BF16), jax.ShapeDtypeStruct(state.shape, F32)],
        scratch_shapes=[pltpu.VMEM((nh, hd), F32)],
        compiler_params=_cparams(("arbitrary",)),
        name="wkv_sample",
    )(rkvz4, rkvz4, rkvz4, rkvz4, wl.reshape(m, nh, hd), al.reshape(m, nh, hd),
      as_heads(k_k), as_heads(k_a), as_heads(r_k), as_heads(gn_g), as_heads(gn_b), state)


def rope_tables(pos):
    half = ROPE_DIM // 2
    inv_freq = ROPE_THETA ** (-jnp.arange(half, dtype=F32) * 2.0 / ROPE_DIM)
    ang = pos.astype(F32)[:, None] * inv_freq[None, :]
    cos = jnp.cos(ang)
    sin = jnp.sin(ang)
    rows = pos.shape[0]
    ones = jnp.ones((rows, HEAD_DIM - ROPE_DIM), F32)
    zeros_h = jnp.zeros((rows, half), F32)
    zeros_r = jnp.zeros((rows, HEAD_DIM - ROPE_DIM), F32)
    cos_h = jnp.concatenate([cos, cos, ones], axis=1)
    sa_h = jnp.concatenate([-sin, zeros_h, zeros_r], axis=1)
    sb_h = jnp.concatenate([zeros_h, sin, zeros_r], axis=1)
    two = lambda t: jnp.concatenate([t, t], axis=1)
    return two(cos_h), two(sa_h), two(sb_h)


def _attn_prompt_kernel(sink_ref, q_ref, kc_ref, kp_ref, vc_ref, vp_ref, z_ref, o_ref):
    n = pl.program_id(1)
    hd = HEAD_DIM
    blk = q_ref.shape[0]
    n_kv = kc_ref.shape[1] // hd
    grp = q_ref.shape[1] // (n_kv * hd)
    qi = lax.broadcasted_iota(jnp.int32, (blk, 2 * blk), 0)
    kj = lax.broadcasted_iota(jnp.int32, (blk, 2 * blk), 1) - blk
    kpos = n * blk + kj
    diff = qi - kj
    valid = (kpos >= LEAD) & (diff >= 0) & (diff <= WINDOW)
    k_all = jnp.concatenate([kp_ref[...], kc_ref[...]], axis=0).astype(BF16)
    v_all = jnp.concatenate([vp_ref[...], vc_ref[...]], axis=0).astype(BF16)

    def scores(h):
        k_h = k_all[:, h * hd:(h + 1) * hd]
        return [_dot_nt(q_ref[:, (h * grp + gi) * hd:(h * grp + gi + 1) * hd], k_h) for gi in range(grp)]

    outs = []
    s_next = scores(0)
    for h in range(n_kv):
        s_cur = s_next
        if h + 1 < n_kv:
            s_next = scores(h + 1)
        v_h = v_all[:, h * hd:(h + 1) * hd]
        for g0 in range(0, grp, SOFTMAX_BATCH):
            gs = range(g0, min(g0 + SOFTMAX_BATCH, grp))
            sks = {gi: sink_ref[h * grp + gi] for gi in gs}
            ss = {gi: jnp.where(valid, s_cur[gi], -jnp.inf) for gi in gs}
            ms = {gi: jnp.maximum(jnp.max(ss[gi], axis=-1, keepdims=True), sks[gi]) for gi in gs}
            ps = {gi: jnp.exp(ss[gi] - ms[gi]) for gi in gs}
            dens = {gi: jnp.sum(ps[gi], axis=-1, keepdims=True) + jnp.exp(sks[gi] - ms[gi]) for gi in gs}
            outs += [_dot(ps[gi].astype(BF16), v_h) / dens[gi] for gi in gs]
    att = jnp.concatenate(outs, axis=1)
    o_ref[...] = (att * _silu(z_ref[...])).astype(o_ref.dtype)


def attn_prompt(sinks, q, k, v, z, n_batch):
    m, e = q.shape
    nb = m // (n_batch * BLOCK)
    kw = k.shape[1]
    cur = lambda i, n: (i * nb + n, 0)
    prv = lambda i, n: (i * nb + jnp.maximum(n - 1, 0), 0)
    return pl.pallas_call(
        _attn_prompt_kernel,
        grid=(n_batch, nb),
        in_specs=[pl.BlockSpec(memory_space=pltpu.SMEM),
                  pl.BlockSpec((BLOCK, e), cur),
                  pl.BlockSpec((BLOCK, kw), cur), pl.BlockSpec((BLOCK, kw), prv),
                  pl.BlockSpec((BLOCK, kw), cur), pl.BlockSpec((BLOCK, kw), prv),
                  pl.BlockSpec((BLOCK, e), cur)],
        out_specs=pl.BlockSpec((BLOCK, e), cur),
        out_shape=jax.ShapeDtypeStruct((m, e), BF16),
        compiler_params=_cparams(("arbitrary", "arbitrary")),
        name="attn_prompt",
    )(sinks, q, k, k, v, v, z)


def _attn_sample_kernel(sink_ref, q_ref, kc_ref, vc_ref, kn_ref, vn_ref, z_ref, o_ref, ko_ref, vo_ref):
    hd = HEAD_DIM
    win = kc_ref.shape[1]
    n_kv = kc_ref.shape[2] // hd
    nq = q_ref.shape[1]
    grp = nq // n_kv
    pad = 8
    kc = kc_ref[0]
    vc = vc_ref[0]
    kn = kn_ref[0]
    vn = vn_ref[0]
    first = lax.broadcasted_iota(jnp.int32, (pad, kc.shape[1]), 0) == 0
    k_all = jnp.concatenate([kc, jnp.where(first, kn, 0.0)], axis=0).astype(BF16)
    v_all = jnp.concatenate([vc, jnp.where(first, vn, 0.0)], axis=0).astype(BF16)
    col = lax.broadcasted_iota(jnp.int32, (grp, win + pad), 1)
    valid = (col <= win) & (win - col <= WINDOW)
    q = q_ref[0].astype(BF16)
    row_i = lax.broadcasted_iota(jnp.int32, (grp, 1), 0)
    hs = range(n_kv)
    sks = []
    for h in hs:
        sk = jnp.zeros((grp, 1), F32)
        for gi in range(grp):
            sk = jnp.where(row_i == gi, sink_ref[h * grp + gi], sk)
        sks.append(sk)
    ss = [jnp.where(valid, _dot_nt(q[h * grp:(h + 1) * grp, :], k_all[:, h * hd:(h + 1) * hd]), -jnp.inf)
          for h in hs]
    ms = [jnp.maximum(jnp.max(ss[h], axis=-1, keepdims=True), sks[h]) for h in hs]
    ps = [jnp.exp(ss[h] - ms[h]) for h in hs]
    dens = [jnp.sum(ps[h], axis=-1, keepdims=True) + jnp.exp(sks[h] - ms[h]) for h in hs]
    outs = [_dot(ps[h].astype(BF16), v_all[:, h * hd:(h + 1) * hd]) / dens[h] for h in hs]
    att = jnp.concatenate(outs, axis=0)
    o_ref[0] = (att * _silu(z_ref[0])).astype(o_ref.dtype)
    last = lax.broadcasted_iota(jnp.int32, kc.shape, 0) == win - 1
    ko_ref[0] = jnp.where(last, kn, pltpu.roll(kc, win - 1, axis=0))
    vo_ref[0] = jnp.where(last, vn, pltpu.roll(vc, win - 1, axis=0))


def attn_sample(sinks, q, cache_k, cache_v, k_new, v_new, z):
    m, win, kw = cache_k.shape
    nq = q.shape[1]
    hd = HEAD_DIM
    tok = pl.BlockSpec((1, nq, hd), lambda i: (i, 0, 0))
    cache = pl.BlockSpec((1, win, kw), lambda i: (i, 0, 0))
    new = pl.BlockSpec((1, 1, kw), lambda i: (i, 0, 0))
    return pl.pallas_call(
        _attn_sample_kernel,
        grid=(m,),
        in_specs=[pl.BlockSpec(memory_space=pltpu.SMEM), tok, cache, cache, new, new, tok],
        out_specs=[tok, cache, cache],
        out_shape=[jax.ShapeDtypeStruct((m, nq, hd), BF16),
                   jax.ShapeDtypeStruct(cache_k.shape, F32), jax.ShapeDtypeStruct(cache_v.shape, F32)],
        compiler_params=_cparams(("arbitrary",)),
        name="attn_sample",
    )(sinks, q, cache_k, cache_v, k_new, v_new, z)


def _pad_lora(w_down, w_up):
    r = w_down.shape[1]
    return (jnp.pad(w_down, ((0, 0), (0, LORA_PAD - r))).astype(BF16),
            jnp.pad(w_up, ((0, LORA_PAD - r), (0, 0))).astype(BF16))


def kernel(x_prompt, x_sample, state_wkv, state_shift, cache_k, cache_v, meta_tokens, a_norm, a_mu, a_w_rkvz,
           a_w0, a_w1, a_w2, a_a0, a_a1, a_a2, a_k_k, a_k_a, a_r_k, a_gn_g, a_gn_b, a_w_out, kv_norm, w_kv,
           b_norm, b_w_qz, b_sinks, b_w_o, final_norm):
    nb, seq, d = x_prompt.shape
    db, dseq, _ = x_sample.shape
    assert dseq == 1 and a_norm.shape[0] == 1 and b_norm.shape[0] == 1
    e = a_w_rkvz.shape[3]
    win = cache_k.shape[1]
    p_len = LEAD + N_META + seq
    assert p_len % BLOCK == 0 and (LEAD + N_META) == BLOCK
    kvw = N_KV_HEADS * HEAD_DIM

    w_rkvz = a_w_rkvz[0].astype(BF16)
    w1, w2 = _pad_lora(a_w1[0], a_w2[0])
    a1, a2 = _pad_lora(a_a1[0], a_a2[0])
    w_out = a_w_out[0].astype(BF16)
    w_kv_bf = w_kv.astype(BF16)
    w_qz = b_w_qz.astype(BF16)
    w_o = b_w_o[0].astype(BF16)
    mu = a_mu[0]
    sinks = b_sinks[0]
    gains_b = jnp.stack([kv_norm, b_norm[0]])

    tm = p_len // 8

    head = jnp.concatenate([jnp.zeros((LEAD, d), F32), meta_tokens], axis=0)
    xm, hw_p, ha_p, x_last = norm_shift_prompt(x_prompt, head, a_norm[0], mu, w1, a1)
    p_state_shift = x_last.reshape(1, nb, d)
    rkvz = matmul_groups(xm, w_rkvz, tm, F32)
    yg, p_state = wkv_prompt(rkvz, hw_p, ha_p, w2, a_w0[0], a2, a_a0[0], a_k_k[0], a_k_a[0],
                             a_r_k[0].reshape(-1), a_gn_g[0], a_gn_b[0], nb)
    hp, hn_kv, hn_b = matmul_residual_norm_blocks(yg, w_out, x_prompt, gains_b, nb, BF16, True, head=head)

    pos_p = jnp.maximum(jnp.arange(p_len, dtype=jnp.int32) - LEAD, 0)
    tabs_p = tuple(jnp.tile(t, (nb, 1)) for t in rope_tables(pos_p))
    k_p, v_p = matmul_rope(hn_kv, w_kv_bf, tabs_p, tm, kvw, (F32, F32))
    q_p, = matmul_rope(hn_b, w_qz, tabs_p, tm, e, (BF16,), scale=Q_SCALE, n=e)
    z_p = matmul_groups(hn_b[None], w_qz, tm, F32, n=e, col=1)[0]
    att = attn_prompt(sinks, q_p, k_p, v_p, z_p, nb)
    y_prompt, = matmul_residual_norm_blocks(att, w_o, hp, final_norm[None], nb, F32, False,
                                            first_block=(LEAD + N_META) // BLOCK)
    y_prompt = y_prompt.reshape(nb, seq, d)
    tail = lambda t: t.reshape(nb, p_len, kvw)[:, -win:].reshape(nb, win, N_KV_HEADS, HEAD_DIM)
    p_cache_k = tail(k_p)
    p_cache_v = tail(v_p)

    hs = x_sample.reshape(db, d)
    xm_s, hw_s, ha_s, xn_s = norm_shift_sample(hs, state_shift[0], a_norm[0], mu, w1, a1)
    rkvz_s = matmul_groups(xm_s, w_rkvz, db, F32)
    wl_s, al_s = lora_up(hw_s, ha_s, w2, a_w0[0], a2, a_a0[0])
    yg_s, s_state = wkv_sample(rkvz_s, wl_s, al_s, a_k_k[0], a_k_a[0], a_r_k[0].reshape(-1), a_gn_g[0],
                               a_gn_b[0], state_wkv[0])
    hs, hn_kv_s, hn_b_s = matmul_residual_norm(yg_s.reshape(db, e), w_out, hs, gains_b, db, BF16)
    tabs_s = rope_tables(jnp.full((db,), PAST_LEN, jnp.int32))
    k_s, v_s = matmul_rope(hn_kv_s, w_kv_bf, tabs_s, db, kvw, (F32, F32))
    q_s, = matmul_rope(hn_b_s, w_qz, tabs_s, db, e, (F32,), scale=Q_SCALE, n=e)
    z_s = matmul_groups(hn_b_s[None], w_qz, db, F32, n=e, col=1)[0]
    nq = e // HEAD_DIM
    att_s, s_cache_k, s_cache_v = attn_sample(
        sinks, q_s.reshape(db, nq, HEAD_DIM), cache_k.reshape(db, win, kvw), cache_v.reshape(db, win, kvw),
        k_s.reshape(db, 1, kvw), v_s.reshape(db, 1, kvw), z_s.reshape(db, nq, HEAD_DIM))
    y_s, = matmul_residual_norm(att_s.reshape(db, e), w_o, hs, final_norm[None], db, F32, emit_h=False)
    y_sample = y_s.reshape(db, 1, d)

    return (y_prompt, y_sample, p_state[None], p_state_shift,
            p_cache_k, p_cache_v,
            s_state[None], xn_s[None],
            s_cache_k.reshape(cache_k.shape), s_cache_v.reshape(cache_v.shape))
```

```python
import functools
import math

import jax
import jax.numpy as jnp
from jax import lax
from jax.experimental import pallas as pl
from jax.experimental.pallas import tpu as pltpu

F32 = jnp.float32
BF16 = jnp.bfloat16

HEAD_DIM = 64
N_KV_HEADS = 8
WINDOW = 128
BLOCK = 128
ROPE_DIM = HEAD_DIM // 4
ROPE_THETA = 500000.0
N_META = 16
PAST_LEN = 16384
RMS_EPS = 1e-6
GN_EPS = 64e-5
LEAD = (-N_META) % BLOCK
CHUNK = 64
WKV_ROWS = 128
HEADS_PER_STREAM = 32
HEADS_PER_STEP = 32
LORA_PAD = 128
MXU_TILE = 256
ROPE_SLAB = 512
VMEM_LIMIT = 48 * 1024 * 1024
DECAY_SCALE = math.exp(-0.5)
SOFTMAX_BATCH = 4
Q_SCALE = HEAD_DIM ** -0.5


def _cparams(sem):
    return pltpu.CompilerParams(dimension_semantics=sem, vmem_limit_bytes=VMEM_LIMIT)


def _sigmoid(x):
    return 1.0 / (1.0 + jnp.exp(-x))


def _silu(x):
    return x * _sigmoid(x)


def _dot(a, b):
    return jnp.dot(a, b, preferred_element_type=F32)


def _dot_nt(a, b):
    return lax.dot_general(a, b, (((1,), (1,)), ((), ())), preferred_element_type=F32)


def _dot_tn(a, b):
    return lax.dot_general(a, b, (((0,), (0,)), ((), ())), preferred_element_type=F32)


def _split_hi_lo(x):
    hi = x.astype(BF16)
    lo = (x - hi.astype(F32)).astype(BF16)
    return hi, lo


def _mixes(xn, prev, mu_ref, w1_ref, a1_ref, xm_ref, hw_ref, ha_ref):
    xx = prev - xn
    n_proj = xm_ref.shape[0]
    for p in range(n_proj):
        xm_ref[p] = (xn + xx * mu_ref[p:p + 1, :]).astype(xm_ref.dtype)
    xw = (xn + xx * mu_ref[n_proj:n_proj + 1, :]).astype(BF16)
    xa = (xn + xx * mu_ref[n_proj + 1:n_proj + 2, :]).astype(BF16)
    hw_ref[...] = jnp.tanh(_dot(xw, w1_ref[...])).astype(hw_ref.dtype)
    ha_ref[...] = _dot(xa, a1_ref[...]).astype(ha_ref.dtype)


def _padded_rows(x_ref, head_ref, first_block):
    return jnp.where(pl.program_id(1) + first_block == 0, head_ref[...], x_ref[0])


def _norm_shift_kernel(x_ref, head_ref, g_ref, mu_ref, w1_ref, a1_ref, xm_ref, hw_ref, ha_ref, last_ref, carry_ref):
    @pl.when(pl.program_id(1) == 0)
    def _():
        carry_ref[...] = jnp.zeros_like(carry_ref)

    x = _padded_rows(x_ref, head_ref, 0)
    tm = x.shape[0]
    xn = x * lax.rsqrt(jnp.mean(x * x, axis=-1, keepdims=True) + RMS_EPS) * g_ref[...]
    rolled = pltpu.roll(xn, 1, axis=0)
    row = lax.broadcasted_iota(jnp.int32, xn.shape, 0)
    prev = jnp.where(row == 0, carry_ref[0:1, :], rolled)
    _mixes(xn, prev, mu_ref, w1_ref, a1_ref, xm_ref, hw_ref, ha_ref)
    carry_ref[0:1, :] = xn[tm - 1:tm, :]
    last_ref[0] = xn[tm - 1:tm, :]


def norm_shift_prompt(x, head, g, mu, w1, a1):
    b, seq, d = x.shape
    tm = BLOCK
    p = tm + seq
    n_mix = mu.shape[0]
    n_proj = n_mix - 2
    lr = w1.shape[1]
    nt = p // tm
    const = lambda shape: pl.BlockSpec(shape, lambda i, t: (0,) * len(shape))
    hid = pl.BlockSpec((tm, lr), lambda i, t: (i * nt + t, 0))
    return pl.pallas_call(
        _norm_shift_kernel,
        grid=(b, nt),
        in_specs=[pl.BlockSpec((1, tm, d), lambda i, t: (i, jnp.maximum(t - 1, 0), 0)),
                  const((tm, d)), const((1, d)), const((n_mix, d)), const((d, lr)), const((d, lr))],
        out_specs=[pl.BlockSpec((n_proj, tm, d), lambda i, t: (0, i * nt + t, 0)), hid, hid,
                   pl.BlockSpec((1, 1, d), lambda i, t: (i, 0, 0))],
        out_shape=[jax.ShapeDtypeStruct((n_proj, b * p, d), BF16),
                   jax.ShapeDtypeStruct((b * p, lr), BF16), jax.ShapeDtypeStruct((b * p, lr), BF16),
                   jax.ShapeDtypeStruct((b, 1, d), F32)],
        scratch_shapes=[pltpu.VMEM((8, d), F32)],
        compiler_params=_cparams(("arbitrary", "arbitrary")),
        name="norm_shift_prompt",
    )(x, head, g.reshape(1, d), mu, w1, a1)


def _norm_shift_sample_kernel(x_ref, prev_ref, g_ref, mu_ref, w1_ref, a1_ref, xm_ref, hw_ref, ha_ref, xn_ref):
    x = x_ref[...]
    xn = x * lax.rsqrt(jnp.mean(x * x, axis=-1, keepdims=True) + RMS_EPS) * g_ref[...]
    xn_ref[...] = xn
    _mixes(xn, prev_ref[...], mu_ref, w1_ref, a1_ref, xm_ref, hw_ref, ha_ref)


def norm_shift_sample(x, prev, g, mu, w1, a1):
    m, d = x.shape
    lr = w1.shape[1]
    return pl.pallas_call(
        _norm_shift_sample_kernel,
        out_shape=[jax.ShapeDtypeStruct((mu.shape[0] - 2, m, d), BF16),
                   jax.ShapeDtypeStruct((m, lr), BF16), jax.ShapeDtypeStruct((m, lr), BF16),
                   jax.ShapeDtypeStruct((m, d), F32)],
        name="norm_shift_sample",
    )(x, prev, g.reshape(1, d), mu, w1, a1)


def _rope(y, cos, sin_a, sin_b):
    half = ROPE_DIM // 2
    step = ROPE_SLAB
    rep = step // cos.shape[1]
    tile = lambda t: jnp.concatenate([t] * rep, axis=1)
    cos_t, sa_t, sb_t = tile(cos), tile(sin_a), tile(sin_b)
    outs = []
    for j in range(y.shape[1] // step):
        ys = y[:, j * step:(j + 1) * step]
        outs.append(ys * cos_t + pltpu.roll(ys, step - half, axis=1) * sa_t + pltpu.roll(ys, half, axis=1) * sb_t)
    return jnp.concatenate(outs, axis=1) if len(outs) > 1 else outs[0]


def _mm_group_kernel(x_ref, w_ref, o_ref):
    o_ref[0] = _dot(x_ref[0], w_ref[0]).astype(o_ref.dtype)


def matmul_groups(x, w, tm, out_dtype, n=None, col=0):
    g = w.shape[0]
    n = w.shape[2] if n is None else n
    _, m, kdim = x.shape
    return pl.pallas_call(
        _mm_group_kernel,
        grid=(g, m // tm),
        in_specs=[pl.BlockSpec((1, tm, kdim), lambda q, i: (q, i, 0)),
                  pl.BlockSpec((1, kdim, n), lambda q, i: (q, 0, col))],
        out_specs=pl.BlockSpec((1, tm, n), lambda q, i: (q, i, 0)),
        out_shape=jax.ShapeDtypeStruct((g, m, n), out_dtype),
        compiler_params=_cparams(("arbitrary", "arbitrary")),
        name="matmul_groups",
    )(x, w)


def _mm_rope_kernel(x_ref, w_ref, cos_ref, sa_ref, sb_ref, *o_refs, n_rope, scale):
    y = _dot(x_ref[...], w_ref[...])
    rot = _rope(y[:, :n_rope], cos_ref[...], sa_ref[...], sb_ref[...])
    if scale != 1.0:
        rot = rot * scale
    o_refs[0][...] = rot.astype(o_refs[0].dtype)
    if len(o_refs) > 1:
        o_refs[1][...] = y[:, n_rope:].astype(o_refs[1].dtype)


def matmul_rope(x, w, tables, tm, n_rope, out_dtypes, scale=1.0, n=None):
    m, kdim = x.shape
    n = w.shape[-1] if n is None else n
    w_block = (kdim, n) if w.ndim == 2 else (None, kdim, n)
    lanes = tables[0].shape[1]
    widths = [n_rope] + ([n - n_rope] if n > n_rope else [])
    tab = pl.BlockSpec((tm, lanes), lambda i: (i, 0))
    outs = pl.pallas_call(
        functools.partial(_mm_rope_kernel, n_rope=n_rope, scale=scale),
        grid=(m // tm,),
        in_specs=[pl.BlockSpec((tm, kdim), lambda i: (i, 0)),
                  pl.BlockSpec(w_block, lambda i: (0,) * w.ndim, pipeline_mode=pl.Buffered(1)),
                  tab, tab, tab],
        out_specs=[pl.BlockSpec((tm, wd), lambda i: (i, 0)) for wd in widths],
        out_shape=[jax.ShapeDtypeStruct((m, wd), dt) for wd, dt in zip(widths, out_dtypes)],
        compiler_params=_cparams(("arbitrary",)),
        name="matmul_rope",
    )(x, w, *tables)
    return outs


def _mm_res_norm_kernel(x_ref, w_ref, res_ref, g_ref, *out_refs, emit_h):
    h = res_ref[...] + _dot(x_ref[...], w_ref[...])
    hn_refs = out_refs
    if emit_h:
        out_refs[0][...] = h
        hn_refs = out_refs[1:]
    inv = lax.rsqrt(jnp.mean(h * h, axis=-1, keepdims=True) + RMS_EPS)
    for j, hn_ref in enumerate(hn_refs):
        hn_ref[...] = (h * inv * g_ref[j:j + 1, :]).astype(hn_ref.dtype)


def matmul_residual_norm(x, w, res, gains, tm, norm_dtype, emit_h=True):
    m, kdim = x.shape
    n = w.shape[1]
    ng = gains.shape[0]
    row = lambda width: pl.BlockSpec((tm, width), lambda i: (i, 0))
    return pl.pallas_call(
        functools.partial(_mm_res_norm_kernel, emit_h=emit_h),
        grid=(m // tm,),
        in_specs=[row(kdim),
                  pl.BlockSpec((kdim, n), lambda i: (0, 0), pipeline_mode=pl.Buffered(1)),
                  row(n),
                  pl.BlockSpec((ng, n), lambda i: (0, 0))],
        out_specs=[row(n)] * (int(emit_h) + ng),
        out_shape=[jax.ShapeDtypeStruct((m, n), F32)] * int(emit_h) + [jax.ShapeDtypeStruct((m, n), norm_dtype)] * ng,
        compiler_params=_cparams(("arbitrary",)),
        name="matmul_residual_norm",
    )(x, w, res, gains)


def _mm_res_norm_blocks_kernel(x_ref, w_ref, res_ref, *rest, emit_h, head, first_block):
    if head:
        head_ref, g_ref, *out_refs = rest
        res = _padded_rows(res_ref, head_ref, first_block)
    else:
        g_ref, *out_refs = rest
        res = res_ref[...]
    h = res + _dot(x_ref[...], w_ref[...])
    hn_refs = out_refs
    if emit_h:
        out_refs[0][...] = h
        hn_refs = out_refs[1:]
    inv = lax.rsqrt(jnp.mean(h * h, axis=-1, keepdims=True) + RMS_EPS)
    for j, hn_ref in enumerate(hn_refs):
        hn_ref[...] = (h * inv * g_ref[j:j + 1, :]).astype(hn_ref.dtype)


def matmul_residual_norm_blocks(x, w, res, gains, n_batch, norm_dtype, emit_h, head=None, first_block=0):
    kdim = x.shape[1]
    n = w.shape[1]
    ng = gains.shape[0]
    nb_out = x.shape[0] // (n_batch * BLOCK)
    nb = nb_out + first_block
    m = n_batch * nb * BLOCK
    padded = lambda width: pl.BlockSpec((BLOCK, width), lambda i, t: (i * nb + t + first_block, 0))
    compact = lambda width: pl.BlockSpec((BLOCK, width), lambda i, t: (i * nb_out + t, 0))
    const = lambda shape: pl.BlockSpec(shape, lambda i, t: (0,) * len(shape))
    if head is not None:
        res_specs = [pl.BlockSpec((1, BLOCK, n), lambda i, t: (i, jnp.maximum(t + first_block - 1, 0), 0)),
                     const((BLOCK, n))]
        res_args = [res, head]
    else:
        res_specs, res_args = [padded(n)], [res]
    norm_spec = compact(n) if first_block else padded(n)
    norm_rows = n_batch * nb_out * BLOCK
    return pl.pallas_call(
        functools.partial(_mm_res_norm_blocks_kernel, emit_h=emit_h, head=head is not None, first_block=first_block),
        grid=(n_batch, nb_out),
        in_specs=[compact(kdim), pl.BlockSpec((kdim, n), lambda i, t: (0, 0), pipeline_mode=pl.Buffered(1))]
        + res_specs + [const((ng, n))],
        out_specs=[padded(n)] * int(emit_h) + [norm_spec] * ng,
        out_shape=[jax.ShapeDtypeStruct((m, n), F32)] * int(emit_h)
        + [jax.ShapeDtypeStruct((norm_rows, n), norm_dtype)] * ng,
        compiler_params=_cparams(("arbitrary", "arbitrary")),
        name="matmul_residual_norm_blocks",
    )(x, w, *res_args, gains)


def _lora_up_kernel(hw_ref, ha_ref, w2_ref, w0_ref, a2_ref, a0_ref, wl_ref, al_ref):
    wl_ref[...] = w0_ref[...] + _dot(hw_ref[...], w2_ref[...])
    al_ref[...] = a0_ref[...] + _dot(ha_ref[...], a2_ref[...])


def lora_up(hw, ha, w2, w0, a2, a0):
    m = hw.shape[0]
    e = w2.shape[1]
    return pl.pallas_call(
        _lora_up_kernel,
        out_shape=[jax.ShapeDtypeStruct((m, e), F32), jax.ShapeDtypeStruct((m, e), F32)],
        name="lora_up",
    )(hw, ha, w2, w0.reshape(1, e), a2, a0.reshape(1, e))


def _seg_sum(x, ones_bd):
    hi = x.astype(BF16)
    outs = []
    for c in range(x.shape[1] // MXU_TILE):
        sl = slice(c * MXU_TILE, (c + 1) * MXU_TILE)
        outs.append(_dot(hi[:, sl], ones_bd))
    return jnp.concatenate(outs, axis=1) if len(outs) > 1 else outs[0]


def _wkv_prompt_kernel(r_ref, k_ref, v_ref, z_ref, hw_ref, ha_ref, w2_ref, w0_ref, a2_ref, a0_ref,
                       kk_ref, ka_ref, rk_ref, gg_ref, gb_ref, yg_ref, sout_ref, s_ref):
    t_idx = pl.program_id(2)
    c = CHUNK
    hd = HEAD_DIM

    @pl.when(t_idx == 0)
    def _():
        s_ref[...] = jnp.zeros_like(s_ref)

    tb = r_ref.shape[1]
    nh = HEADS_PER_STREAM
    hw = nh * hd
    nc = tb // c

    li = lax.broadcasted_iota(jnp.int32, (MXU_TILE, MXU_TILE), 0) // hd
    lj = lax.broadcasted_iota(jnp.int32, (MXU_TILE, MXU_TILE), 1) // hd
    ones_bd = jnp.where(li == lj, 1.0, 0.0).astype(BF16)
    bi_ = lax.broadcasted_iota(jnp.int32, (tb, tb), 0)
    bj_ = lax.broadcasted_iota(jnp.int32, (tb, tb), 1)
    tri_incl = jnp.where((bj_ <= bi_) & (bj_ // c == bi_ // c), 1.0, 0.0).astype(BF16)
    ti = lax.broadcasted_iota(jnp.int32, (c, c), 0)
    tj = lax.broadcasted_iota(jnp.int32, (c, c), 1)
    ai = lax.broadcasted_iota(jnp.int32, (c, 2 * c), 0)
    aj = lax.broadcasted_iota(jnp.int32, (c, 2 * c), 1)
    upper = aj >= c
    aj_mod = jnp.where(upper, aj - c, aj)
    masks = dict(
        strict=tj < ti,
        eye=jnp.where(ti == tj, 1.0, 0.0).astype(F32),
        top_k=upper & (aj_mod < ai),
        bot=aj_mod <= ai)

    for st in range(r_ref.shape[2] // hw):
        _wkv_stream(st, hw, nc, r_ref, k_ref, v_ref, z_ref, hw_ref, ha_ref, w2_ref, w0_ref, a2_ref, a0_ref,
                    kk_ref, ka_ref, rk_ref, gg_ref, gb_ref, yg_ref, s_ref, ones_bd, tri_incl, masks)

    @pl.when(t_idx == pl.num_programs(2) - 1)
    def _():
        sout_ref[0] = s_ref[...]


def _wkv_stream(st, hw, nc, r_ref, k_ref, v_ref, z_ref, hw_ref, ha_ref, w2_ref, w0_ref, a2_ref, a0_ref,
                kk_ref, ka_ref, rk_ref, gg_ref, gb_ref, yg_ref, s_ref, ones_bd, tri_incl, masks):
    c = CHUNK
    hd = HEAD_DIM
    nh = hw // hd
    ls = slice(st * hw, (st + 1) * hw)
    h0 = st * nh
    r = r_ref[0, :, ls]
    k = k_ref[0, :, ls]
    v = v_ref[0, :, ls]
    wl = w0_ref[:, ls] + _dot(hw_ref[...], w2_ref[:, ls])
    al = a0_ref[:, ls] + _dot(ha_ref[...], a2_ref[:, ls])
    a = _sigmoid(al)
    lw = -DECAY_SCALE * _sigmoid(wl)
    kk = k * kk_ref[:, ls]
    n2 = _seg_sum(kk * kk, ones_bd)
    kk = kk / jnp.maximum(jnp.sqrt(n2), 1e-12)
    k2 = k * (1.0 + (a - 1.0) * ka_ref[:, ls])
    bb = kk * a

    lw_hi, lw_lo = _split_hi_lo(lw)
    g = _dot(tri_incl, lw_hi) + _dot(tri_incl, lw_lo)
    mid = lambda ci: g[ci * c + c // 2 - 1:ci * c + c // 2, :]
    gm = jnp.concatenate([jnp.broadcast_to(mid(ci), (c, hw)) for ci in range(nc)], axis=0)
    e_a = jnp.exp(g - gm)
    e_prev = jnp.exp(g - lw - gm)
    e_inv = jnp.exp(gm - g)
    e1 = [jnp.exp(mid(ci)) for ci in range(nc)]
    e2 = [jnp.exp(g[ci * c + c - 1:ci * c + c, :] - mid(ci)) for ci in range(nc)]

    kkd = (kk * e_prev).astype(BF16)
    rd = (r * e_a).astype(BF16)
    bi = (bb * e_inv).astype(BF16)
    ki = (k2 * e_inv).astype(BF16)
    v_bf = v.astype(BF16)
    zeros_cv = jnp.zeros((c, hd), BF16)

    pairs = [(ci, h) for ci in range(nc) for h in range(nh)]
    rows = lambda ci: slice(ci * c, (ci + 1) * c)
    cols = lambda h: slice(h * hd, (h + 1) * hd)
    xs = {(ci, h): jnp.concatenate([kkd[rows(ci), cols(h)], rd[rows(ci), cols(h)]], axis=0) for ci, h in pairs}
    r1s = {(ci, h): jnp.concatenate([bi[rows(ci), cols(h)], ki[rows(ci), cols(h)]], axis=0) for ci, h in pairs}
    vs = {(ci, h): v_bf[rows(ci), cols(h)] for ci, h in pairs}
    a_mats = {p: _dot_nt(xs[p], r1s[p]) for p in pairs}
    lk_vs = {p: _dot(jnp.where(masks["top_k"], a_mats[p][:c, :], 0.0).astype(BF16),
                     jnp.concatenate([zeros_cv, vs[p]], axis=0)) for p in pairs}
    lps = {p: jnp.where(masks["strict"], a_mats[p][:c, :c], 0.0) for p in pairs}
    ts = {p: masks["eye"] - lps[p] for p in pairs}
    for _ in range(int(math.log2(c)) - 1):
        lpb = {p: lps[p].astype(BF16) for p in pairs}
        lps = {p: _dot(lpb[p], lpb[p]) for p in pairs}
        ts = {p: _dot(ts[p].astype(BF16), (masks["eye"] + lps[p]).astype(BF16)) for p in pairs}
    a_bots = {p: jnp.where(masks["bot"], a_mats[p][c:, :], 0.0).astype(BF16) for p in pairs}
    t_bf = {p: ts[p].astype(BF16) for p in pairs}

    state = [s_ref[h0 + h] for h in range(nh)]
    y_rows = []
    for ci in range(nc):
        hs = range(nh)
        sms = [state[h] * e1[ci][:, cols(h)] for h in hs]
        p_mats = [_dot_nt(xs[ci, h], sms[h].astype(BF16)) for h in hs]
        us = [-_dot(t_bf[ci, h], (p_mats[h][:c, :] + lk_vs[ci, h]).astype(BF16)) for h in hs]
        uvs = [jnp.concatenate([us[h].astype(BF16), vs[ci, h]], axis=0) for h in hs]
        ys = [p_mats[h][c:, :] + _dot(a_bots[ci, h], uvs[h]) for h in hs]
        state = [(sms[h] + _dot_tn(uvs[h], r1s[ci, h])) * e2[ci][:, cols(h)] for h in hs]
        y_rows.append(jnp.concatenate(ys, axis=1))
    for h in range(nh):
        s_ref[h0 + h] = state[h]
    y = jnp.concatenate(y_rows, axis=0) if nc > 1 else y_rows[0]

    inv_n = 1.0 / hd
    mean = _seg_sum(y, ones_bd) * inv_n
    yc = y - mean
    var = _seg_sum(yc * yc, ones_bd) * inv_n
    yn = yc * lax.rsqrt(var + GN_EPS) * gg_ref[:, ls] + gb_ref[:, ls]
    bonus = _seg_sum(r * k2 * rk_ref[:, ls], ones_bd)
    yg_ref[:, ls] = ((yn + bonus * v) * _silu(z_ref[0, :, ls])).astype(BF16)


def wkv_prompt(rkvz, hw_act, ha_act, w2, w0, a2, a0, k_k, k_a, r_k, gn_g, gn_b, n_batch):
    _, m, e = rkvz.shape
    p = m // n_batch
    tb = WKV_ROWS
    nt = p // tb
    hw = HEADS_PER_STEP * HEAD_DIM
    nh = e // HEAD_DIM
    row = lambda i, g, t: (i * nt + t, g)
    proj = lambda q: pl.BlockSpec((1, tb, hw), lambda i, g, t: (q, i * nt + t, g))
    par = pl.BlockSpec((1, hw), lambda i, g, t: (0, g))
    lr = w2.shape[0]
    hid = pl.BlockSpec((tb, lr), lambda i, g, t: (i * nt + t, 0))
    up = pl.BlockSpec((lr, hw), lambda i, g, t: (0, g))
    return pl.pallas_call(
        _wkv_prompt_kernel,
        grid=(n_batch, e // hw, nt),
        in_specs=[proj(0), proj(1), proj(2), proj(3), hid, hid, up, par, up, par,
                  par, par, par, par, par],
        out_specs=[pl.BlockSpec((tb, hw), row),
                   pl.BlockSpec((1, HEADS_PER_STEP, HEAD_DIM, HEAD_DIM), lambda i, g, t: (i, g, 0, 0))],
        out_shape=[jax.ShapeDtypeStruct((m, e), BF16),
                   jax.ShapeDtypeStruct((n_batch, nh, HEAD_DIM, HEAD_DIM), F32)],
        scratch_shapes=[pltpu.VMEM((HEADS_PER_STEP, HEAD_DIM, HEAD_DIM), F32)],
        compiler_params=_cparams(("arbitrary", "arbitrary", "arbitrary")),
        name="wkv_prompt",
    )(rkvz, rkvz, rkvz, rkvz, hw_act, ha_act, w2, w0.reshape(1, e), a2, a0.reshape(1, e),
      k_k.reshape(1, e), k_a.reshape(1, e), r_k.reshape(1, e), gn_g.reshape(1, e), gn_b.reshape(1, e))


def _wkv_sample_kernel(r_ref, k_ref, v_ref, z_ref, wl_ref, al_ref, kk_ref, ka_ref, rk_ref, gg_ref, gb_ref,
                       s_ref, yg_ref, sout_ref, y_scr):
    hd = HEAD_DIM
    r = r_ref[0]
    k = k_ref[0]
    v = v_ref[0]
    a = _sigmoid(al_ref[0])
    d = jnp.exp(-DECAY_SCALE * _sigmoid(wl_ref[0]))
    kk = k * kk_ref[...]
    kk = kk / jnp.maximum(jnp.sqrt(jnp.sum(kk * kk, axis=-1, keepdims=True)), 1e-12)
    k2 = k * (1.0 + (a - 1.0) * ka_ref[...])
    bb = kk * a
    nh = r.shape[0]
    ii = lax.broadcasted_iota(jnp.int32, (hd, hd), 0)
    jj = lax.broadcasted_iota(jnp.int32, (hd, hd), 1)
    eye = ii == jj

    row = lambda x, h: x[h:h + 1, :]
    group = 8
    for h0 in range(0, nh, group):
        hs = range(h0, h0 + group)
        s = {h: s_ref[0, h] for h in hs}
        sa = {h: jnp.sum(s[h] * row(kk, h), axis=-1, keepdims=True) for h in hs}
        v_col = {h: jnp.sum(jnp.where(eye, row(v, h), 0.0), axis=-1, keepdims=True) for h in hs}
        s_new = {h: s[h] * row(d, h) - sa[h] * row(bb, h) + v_col[h] * row(k2, h) for h in hs}
        y_col = {h: jnp.sum(s_new[h] * row(r, h), axis=-1, keepdims=True) for h in hs}
        for h in hs:
            sout_ref[0, h] = s_new[h]
            y_scr[h:h + 1, :] = jnp.sum(jnp.where(eye, y_col[h], 0.0), axis=0, keepdims=True)

    y = y_scr[...]
    mean = jnp.mean(y, axis=-1, keepdims=True)
    yc = y - mean
    var = jnp.mean(yc * yc, axis=-1, keepdims=True)
    yn = yc * lax.rsqrt(var + GN_EPS) * gg_ref[...] + gb_ref[...]
    bonus = jnp.sum(r * k2 * rk_ref[...], axis=-1, keepdims=True)
    yg_ref[0] = ((yn + bonus * v) * _silu(z_ref[0])).astype(BF16)


def wkv_sample(rkvz, wl, al, k_k, k_a, r_k, gn_g, gn_b, state):
    _, m, e = rkvz.shape
    nh = e // HEAD_DIM
    hd = HEAD_DIM
    rkvz4 = rkvz.reshape(4, m, nh, hd)
    proj = lambda q: pl.BlockSpec((None, 1, nh, hd), lambda i: (q, i, 0, 0))
    tok = pl.BlockSpec((1, nh, hd), lambda i: (i, 0, 0))
    par = pl.BlockSpec((nh, hd), lambda i: (0, 0))
    st = pl.BlockSpec((1, nh, hd, hd), lambda i: (i, 0, 0, 0))
    as_heads = lambda x: x.reshape(nh, hd)
    return pl.pallas_call(
        _wkv_sample_kernel,
        grid=(m,),
        in_specs=[proj(0), proj(1), proj(2), proj(3), tok, tok, par, par, par, par, par, st],
        out_specs=[tok, st],
        out_shape=[jax.ShapeDtypeStruct((m, nh, hd), BF16), jax.ShapeDtypeStruct(state.shape, F32)],
        scratch_shapes=[pltpu.VMEM((nh, hd), F32)],
        compiler_params=_cparams(("arbitrary",)),
        name="wkv_sample",
    )(rkvz4, rkvz4, rkvz4, rkvz4, wl.reshape(m, nh, hd), al.reshape(m, nh, hd),
      as_heads(k_k), as_heads(k_a), as_heads(r_k), as_heads(gn_g), as_heads(gn_b), state)


def rope_tables(pos):
    half = ROPE_DIM // 2
    inv_freq = ROPE_THETA ** (-jnp.arange(half, dtype=F32) * 2.0 / ROPE_DIM)
    ang = pos.astype(F32)[:, None] * inv_freq[None, :]
    cos = jnp.cos(ang)
    sin = jnp.sin(ang)
    rows = pos.shape[0]
    ones = jnp.ones((rows, HEAD_DIM - ROPE_DIM), F32)
    zeros_h = jnp.zeros((rows, half), F32)
    zeros_r = jnp.zeros((rows, HEAD_DIM - ROPE_DIM), F32)
    cos_h = jnp.concatenate([cos, cos, ones], axis=1)
    sa_h = jnp.concatenate([-sin, zeros_h, zeros_r], axis=1)
    sb_h = jnp.concatenate([zeros_h, sin, zeros_r], axis=1)
    two = lambda t: jnp.concatenate([t, t], axis=1)
    return two(cos_h), two(sa_h), two(sb_h)


def _attn_prompt_kernel(sink_ref, q_ref, kc_ref, kp_ref, vc_ref, vp_ref, z_ref, o_ref, *, first_block):
    n = pl.program_id(1) + first_block
    hd = HEAD_DIM
    blk = q_ref.shape[0]
    n_kv = kc_ref.shape[1] // hd
    grp = q_ref.shape[1] // (n_kv * hd)
    qi = lax.broadcasted_iota(jnp.int32, (blk, 2 * blk), 0)
    kj = lax.broadcasted_iota(jnp.int32, (blk, 2 * blk), 1) - blk
    kpos = n * blk + kj
    diff = qi - kj
    valid = (kpos >= LEAD) & (diff >= 0) & (diff <= WINDOW)
    k_all = jnp.concatenate([kp_ref[...], kc_ref[...]], axis=0).astype(BF16)
    v_all = jnp.concatenate([vp_ref[...], vc_ref[...]], axis=0).astype(BF16)

    def scores(h):
        k_h = k_all[:, h * hd:(h + 1) * hd]
        return [_dot_nt(q_ref[:, (h * grp + gi) * hd:(h * grp + gi + 1) * hd], k_h) for gi in range(grp)]

    outs = []
    s_next = scores(0)
    for h in range(n_kv):
        s_cur = s_next
        if h + 1 < n_kv:
            s_next = scores(h + 1)
        v_h = v_all[:, h * hd:(h + 1) * hd]
        for g0 in range(0, grp, SOFTMAX_BATCH):
            gs = range(g0, min(g0 + SOFTMAX_BATCH, grp))
            sks = {gi: sink_ref[h * grp + gi] for gi in gs}
            ss = {gi: jnp.where(valid, s_cur[gi], -jnp.inf) for gi in gs}
            ms = {gi: jnp.maximum(jnp.max(ss[gi], axis=-1, keepdims=True), sks[gi]) for gi in gs}
            ps = {gi: jnp.exp(ss[gi] - ms[gi]) for gi in gs}
            dens = {gi: jnp.sum(ps[gi], axis=-1, keepdims=True) + jnp.exp(sks[gi] - ms[gi]) for gi in gs}
            outs += [_dot(ps[gi].astype(BF16), v_h) / dens[gi] for gi in gs]
    att = jnp.concatenate(outs, axis=1)
    o_ref[...] = (att * _silu(z_ref[...])).astype(o_ref.dtype)


def attn_prompt(sinks, q, k, v, z, n_batch, first_block):
    m, e = q.shape
    nb = m // (n_batch * BLOCK)
    nb_out = nb - first_block
    kw = k.shape[1]
    cur = lambda i, n: (i * nb + n + first_block, 0)
    prv = lambda i, n: (i * nb + jnp.maximum(n + first_block - 1, 0), 0)
    return pl.pallas_call(
        functools.partial(_attn_prompt_kernel, first_block=first_block),
        grid=(n_batch, nb_out),
        in_specs=[pl.BlockSpec(memory_space=pltpu.SMEM),
                  pl.BlockSpec((BLOCK, e), cur),
                  pl.BlockSpec((BLOCK, kw), cur), pl.BlockSpec((BLOCK, kw), prv),
                  pl.BlockSpec((BLOCK, kw), cur), pl.BlockSpec((BLOCK, kw), prv),
                  pl.BlockSpec((BLOCK, e), cur)],
        out_specs=pl.BlockSpec((BLOCK, e), lambda i, n: (i * nb_out + n, 0)),
        out_shape=jax.ShapeDtypeStruct((n_batch * nb_out * BLOCK, e), BF16),
        compiler_params=_cparams(("arbitrary", "arbitrary")),
        name="attn_prompt",
    )(sinks, q, k, k, v, v, z)


def _attn_sample_kernel(sink_ref, q_ref, kc_ref, vc_ref, kn_ref, vn_ref, z_ref, o_ref, ko_ref, vo_ref):
    hd = HEAD_DIM
    win = kc_ref.shape[1]
    n_kv = kc_ref.shape[2] // hd
    nq = q_ref.shape[1]
    grp = nq // n_kv
    pad = 8
    kc = kc_ref[0]
    vc = vc_ref[0]
    kn = kn_ref[0]
    vn = vn_ref[0]
    first = lax.broadcasted_iota(jnp.int32, (pad, kc.shape[1]), 0) == 0
    k_all = jnp.concatenate([kc, jnp.where(first, kn, 0.0)], axis=0).astype(BF16)
    v_all = jnp.concatenate([vc, jnp.where(first, vn, 0.0)], axis=0).astype(BF16)
    col = lax.broadcasted_iota(jnp.int32, (grp, win + pad), 1)
    valid = (col <= win) & (win - col <= WINDOW)
    q = q_ref[0].astype(BF16)
    row_i = lax.broadcasted_iota(jnp.int32, (grp, 1), 0)
    hs = range(n_kv)
    sks = []
    for h in hs:
        sk = jnp.zeros((grp, 1), F32)
        for gi in range(grp):
            sk = jnp.where(row_i == gi, sink_ref[h * grp + gi], sk)
        sks.append(sk)
    ss = [jnp.where(valid, _dot_nt(q[h * grp:(h + 1) * grp, :], k_all[:, h * hd:(h + 1) * hd]), -jnp.inf)
          for h in hs]
    ms = [jnp.maximum(jnp.max(ss[h], axis=-1, keepdims=True), sks[h]) for h in hs]
    ps = [jnp.exp(ss[h] - ms[h]) for h in hs]
    dens = [jnp.sum(ps[h], axis=-1, keepdims=True) + jnp.exp(sks[h] - ms[h]) for h in hs]
    outs = [_dot(ps[h].astype(BF16), v_all[:, h * hd:(h + 1) * hd]) / dens[h] for h in hs]
    att = jnp.concatenate(outs, axis=0)
    o_ref[0] = (att * _silu(z_ref[0])).astype(o_ref.dtype)
    last = lax.broadcasted_iota(jnp.int32, kc.shape, 0) == win - 1
    ko_ref[0] = jnp.where(last, kn, pltpu.roll(kc, win - 1, axis=0))
    vo_ref[0] = jnp.where(last, vn, pltpu.roll(vc, win - 1, axis=0))


def attn_sample(sinks, q, cache_k, cache_v, k_new, v_new, z):
    m, win, kw = cache_k.shape
    nq = q.shape[1]
    hd = HEAD_DIM
    tok = pl.BlockSpec((1, nq, hd), lambda i: (i, 0, 0))
    cache = pl.BlockSpec((1, win, kw), lambda i: (i, 0, 0))
    new = pl.BlockSpec((1, 1, kw), lambda i: (i, 0, 0))
    return pl.pallas_call(
        _attn_sample_kernel,
        grid=(m,),
        in_specs=[pl.BlockSpec(memory_space=pltpu.SMEM), tok, cache, cache, new, new, tok],
        out_specs=[tok, cache, cache],
        out_shape=[jax.ShapeDtypeStruct((m, nq, hd), BF16),
                   jax.ShapeDtypeStruct(cache_k.shape, F32), jax.ShapeDtypeStruct(cache_v.shape, F32)],
        compiler_params=_cparams(("arbitrary",)),
        name="attn_sample",
    )(sinks, q, cache_k, cache_v, k_new, v_new, z)


def _pad_lora(w_down, w_up):
    r = w_down.shape[1]
    return (jnp.pad(w_down, ((0, 0), (0, LORA_PAD - r))).astype(BF16),
            jnp.pad(w_up, ((0, LORA_PAD - r), (0, 0))).astype(BF16))


def kernel(x_prompt, x_sample, state_wkv, state_shift, cache_k, cache_v, meta_tokens, a_norm, a_mu, a_w_rkvz,
           a_w0, a_w1, a_w2, a_a0, a_a1, a_a2, a_k_k, a_k_a, a_r_k, a_gn_g, a_gn_b, a_w_out, kv_norm, w_kv,
           b_norm, b_w_qz, b_sinks, b_w_o, final_norm):
    nb, seq, d = x_prompt.shape
    db, dseq, _ = x_sample.shape
    assert dseq == 1 and a_norm.shape[0] == 1 and b_norm.shape[0] == 1
    e = a_w_rkvz.shape[3]
    win = cache_k.shape[1]
    p_len = LEAD + N_META + seq
    assert p_len % BLOCK == 0 and (LEAD + N_META) == BLOCK
    kvw = N_KV_HEADS * HEAD_DIM

    w_rkvz = a_w_rkvz[0].astype(BF16)
    w1, w2 = _pad_lora(a_w1[0], a_w2[0])
    a1, a2 = _pad_lora(a_a1[0], a_a2[0])
    w_out = a_w_out[0].astype(BF16)
    w_kv_bf = w_kv.astype(BF16)
    w_qz = b_w_qz.astype(BF16)
    w_o = b_w_o[0].astype(BF16)
    mu = a_mu[0]
    sinks = b_sinks[0]
    gains_b = jnp.stack([kv_norm, b_norm[0]])

    tm = p_len // 8

    head = jnp.concatenate([jnp.zeros((LEAD, d), F32), meta_tokens], axis=0)
    xm, hw_p, ha_p, x_last = norm_shift_prompt(x_prompt, head, a_norm[0], mu, w1, a1)
    p_state_shift = x_last.reshape(1, nb, d)
    rkvz = matmul_groups(xm, w_rkvz, tm, F32)
    yg, p_state = wkv_prompt(rkvz, hw_p, ha_p, w2, a_w0[0], a2, a_a0[0], a_k_k[0], a_k_a[0],
                             a_r_k[0].reshape(-1), a_gn_g[0], a_gn_b[0], nb)
    hp, hn_kv, hn_b = matmul_residual_norm_blocks(yg, w_out, x_prompt, gains_b, nb, BF16, True, head=head)

    pos_p = jnp.maximum(jnp.arange(p_len, dtype=jnp.int32) - LEAD, 0)
    tabs_p = tuple(jnp.tile(t, (nb, 1)) for t in rope_tables(pos_p))
    k_p, v_p = matmul_rope(hn_kv, w_kv_bf, tabs_p, tm, kvw, (F32, F32))
    q_p, = matmul_rope(hn_b, w_qz, tabs_p, tm, e, (BF16,), scale=Q_SCALE, n=e)
    z_p = matmul_groups(hn_b[None], w_qz, tm, F32, n=e, col=1)[0]
    skip = (LEAD + N_META) // BLOCK
    att = attn_prompt(sinks, q_p, k_p, v_p, z_p, nb, skip)
    y_prompt, = matmul_residual_norm_blocks(att, w_o, hp, final_norm[None], nb, F32, False, first_block=skip)
    y_prompt = y_prompt.reshape(nb, seq, d)
    tail = lambda t: t.reshape(nb, p_len, kvw)[:, -win:].reshape(nb, win, N_KV_HEADS, HEAD_DIM)
    p_cache_k = tail(k_p)
    p_cache_v = tail(v_p)

    hs = x_sample.reshape(db, d)
    xm_s, hw_s, ha_s, xn_s = norm_shift_sample(hs, state_shift[0], a_norm[0], mu, w1, a1)
    rkvz_s = matmul_groups(xm_s, w_rkvz, db, F32)
    wl_s, al_s = lora_up(hw_s, ha_s, w2, a_w0[0], a2, a_a0[0])
    yg_s, s_state = wkv_sample(rkvz_s, wl_s, al_s, a_k_k[0], a_k_a[0], a_r_k[0].reshape(-1), a_gn_g[0],
                               a_gn_b[0], state_wkv[0])
    hs, hn_kv_s, hn_b_s = matmul_residual_norm(yg_s.reshape(db, e), w_out, hs, gains_b, db, BF16)
    tabs_s = rope_tables(jnp.full((db,), PAST_LEN, jnp.int32))
    k_s, v_s = matmul_rope(hn_kv_s, w_kv_bf, tabs_s, db, kvw, (F32, F32))
    q_s, = matmul_rope(hn_b_s, w_qz, tabs_s, db, e, (F32,), scale=Q_SCALE, n=e)
    z_s = matmul_groups(hn_b_s[None], w_qz, db, F32, n=e, col=1)[0]
    nq = e // HEAD_DIM
    att_s, s_cache_k, s_cache_v = attn_sample(
        sinks, q_s.reshape(db, nq, HEAD_DIM), cache_k.reshape(db, win, kvw), cache_v.reshape(db, win, kvw),
        k_s.reshape(db, 1, kvw), v_s.reshape(db, 1, kvw), z_s.reshape(db, nq, HEAD_DIM))
    y_s, = matmul_residual_norm(att_s.reshape(db, e), w_o, hs, final_norm[None], db, F32, emit_h=False)
    y_sample = y_s.reshape(db, 1, d)

    return (y_prompt, y_sample, p_state[None], p_state_shift,
            p_cache_k, p_cache_v,
            s_state[None], xn_s[None],
            s_cache_k.reshape(cache_k.shape), s_cache_v.reshape(cache_v.shape))
```

```python
import functools
import math

import jax
import jax.numpy as jnp
from jax import lax
from jax.experimental import pallas as pl
from jax.experimental.pallas import tpu as pltpu

F32 = jnp.float32
BF16 = jnp.bfloat16

HEAD_DIM = 64
N_KV_HEADS = 8
WINDOW = 128
BLOCK = 128
ROPE_DIM = HEAD_DIM // 4
ROPE_THETA = 500000.0
N_META = 16
PAST_LEN = 16384
RMS_EPS = 1e-6
GN_EPS = 64e-5
LEAD = (-N_META) % BLOCK
CHUNK = 64
WKV_ROWS = 128
HEADS_PER_STREAM = 32
HEADS_PER_STEP = 32
LORA_PAD = 128
MXU_TILE = 256
ROPE_SLAB = 512
VMEM_LIMIT = 48 * 1024 * 1024
LOG2E = 1.0 / math.log(2.0)
DECAY_SCALE = math.exp(-0.5)
SOFTMAX_BATCH = 4
Q_SCALE = HEAD_DIM ** -0.5 * LOG2E


def _cparams(sem):
    return pltpu.CompilerParams(dimension_semantics=sem, vmem_limit_bytes=VMEM_LIMIT)


def _sigmoid(x):
    return 1.0 / (1.0 + jnp.exp2(x * (-LOG2E)))


def _silu(x):
    return x * _sigmoid(x)


def _dot(a, b):
    return jnp.dot(a, b, preferred_element_type=F32)


def _dot_nt(a, b):
    return lax.dot_general(a, b, (((1,), (1,)), ((), ())), preferred_element_type=F32)


def _dot_tn(a, b):
    return lax.dot_general(a, b, (((0,), (0,)), ((), ())), preferred_element_type=F32)


def _split_hi_lo(x):
    hi = x.astype(BF16)
    lo = (x - hi.astype(F32)).astype(BF16)
    return hi, lo


def _mixes(xn, prev, mu_ref, w1_ref, a1_ref, xm_ref, hw_ref, ha_ref):
    xx = prev - xn
    n_proj = xm_ref.shape[0]
    for p in range(n_proj):
        xm_ref[p] = (xn + xx * mu_ref[p:p + 1, :]).astype(xm_ref.dtype)
    xw = (xn + xx * mu_ref[n_proj:n_proj + 1, :]).astype(BF16)
    xa = (xn + xx * mu_ref[n_proj + 1:n_proj + 2, :]).astype(BF16)
    hw_ref[...] = jnp.tanh(_dot(xw, w1_ref[...])).astype(hw_ref.dtype)
    ha_ref[...] = _dot(xa, a1_ref[...]).astype(ha_ref.dtype)


def _padded_rows(x_ref, head_ref, first_block):
    return jnp.where(pl.program_id(1) + first_block == 0, head_ref[...], x_ref[0])


def _norm_shift_kernel(x_ref, head_ref, g_ref, mu_ref, w1_ref, a1_ref, xm_ref, hw_ref, ha_ref, last_ref, carry_ref):
    @pl.when(pl.program_id(1) == 0)
    def _():
        carry_ref[...] = jnp.zeros_like(carry_ref)

    x = _padded_rows(x_ref, head_ref, 0)
    tm = x.shape[0]
    xn = x * lax.rsqrt(jnp.mean(x * x, axis=-1, keepdims=True) + RMS_EPS) * g_ref[...]
    rolled = pltpu.roll(xn, 1, axis=0)
    row = lax.broadcasted_iota(jnp.int32, xn.shape, 0)
    prev = jnp.where(row == 0, carry_ref[0:1, :], rolled)
    _mixes(xn, prev, mu_ref, w1_ref, a1_ref, xm_ref, hw_ref, ha_ref)
    carry_ref[0:1, :] = xn[tm - 1:tm, :]
    last_ref[0] = xn[tm - 1:tm, :]


def norm_shift_prompt(x, head, g, mu, w1, a1):
    b, seq, d = x.shape
    tm = BLOCK
    p = tm + seq
    n_mix = mu.shape[0]
    n_proj = n_mix - 2
    lr = w1.shape[1]
    nt = p // tm
    const = lambda shape: pl.BlockSpec(shape, lambda i, t: (0,) * len(shape))
    hid = pl.BlockSpec((tm, lr), lambda i, t: (i * nt + t, 0))
    return pl.pallas_call(
        _norm_shift_kernel,
        grid=(b, nt),
        in_specs=[pl.BlockSpec((1, tm, d), lambda i, t: (i, jnp.maximum(t - 1, 0), 0)),
                  const((tm, d)), const((1, d)), const((n_mix, d)), const((d, lr)), const((d, lr))],
        out_specs=[pl.BlockSpec((n_proj, tm, d), lambda i, t: (0, i * nt + t, 0)), hid, hid,
                   pl.BlockSpec((1, 1, d), lambda i, t: (i, 0, 0))],
        out_shape=[jax.ShapeDtypeStruct((n_proj, b * p, d), BF16),
                   jax.ShapeDtypeStruct((b * p, lr), BF16), jax.ShapeDtypeStruct((b * p, lr), BF16),
                   jax.ShapeDtypeStruct((b, 1, d), F32)],
        scratch_shapes=[pltpu.VMEM((8, d), F32)],
        compiler_params=_cparams(("arbitrary", "arbitrary")),
        name="norm_shift_prompt",
    )(x, head, g.reshape(1, d), mu, w1, a1)


def _norm_shift_sample_kernel(x_ref, prev_ref, g_ref, mu_ref, w1_ref, a1_ref, xm_ref, hw_ref, ha_ref, xn_ref):
    x = x_ref[...]
    xn = x * lax.rsqrt(jnp.mean(x * x, axis=-1, keepdims=True) + RMS_EPS) * g_ref[...]
    xn_ref[...] = xn
    _mixes(xn, prev_ref[...], mu_ref, w1_ref, a1_ref, xm_ref, hw_ref, ha_ref)


def norm_shift_sample(x, prev, g, mu, w1, a1):
    m, d = x.shape
    lr = w1.shape[1]
    return pl.pallas_call(
        _norm_shift_sample_kernel,
        out_shape=[jax.ShapeDtypeStruct((mu.shape[0] - 2, m, d), BF16),
                   jax.ShapeDtypeStruct((m, lr), BF16), jax.ShapeDtypeStruct((m, lr), BF16),
                   jax.ShapeDtypeStruct((m, d), F32)],
        name="norm_shift_sample",
    )(x, prev, g.reshape(1, d), mu, w1, a1)


def _rope(y, cos, sin_a, sin_b):
    half = ROPE_DIM // 2
    step = ROPE_SLAB
    rep = step // cos.shape[1]
    tile = lambda t: jnp.concatenate([t] * rep, axis=1)
    cos_t, sa_t, sb_t = tile(cos), tile(sin_a), tile(sin_b)
    outs = []
    for j in range(y.shape[1] // step):
        ys = y[:, j * step:(j + 1) * step]
        outs.append(ys * cos_t + pltpu.roll(ys, step - half, axis=1) * sa_t + pltpu.roll(ys, half, axis=1) * sb_t)
    return jnp.concatenate(outs, axis=1) if len(outs) > 1 else outs[0]


def _mm_group_kernel(x_ref, w_ref, o_ref):
    o_ref[0] = _dot(x_ref[0], w_ref[0]).astype(o_ref.dtype)


def matmul_groups(x, w, tm, out_dtype, n=None, col=0):
    g = w.shape[0]
    n = w.shape[2] if n is None else n
    _, m, kdim = x.shape
    return pl.pallas_call(
        _mm_group_kernel,
        grid=(g, m // tm),
        in_specs=[pl.BlockSpec((1, tm, kdim), lambda q, i: (q, i, 0)),
                  pl.BlockSpec((1, kdim, n), lambda q, i: (q, 0, col))],
        out_specs=pl.BlockSpec((1, tm, n), lambda q, i: (q, i, 0)),
        out_shape=jax.ShapeDtypeStruct((g, m, n), out_dtype),
        compiler_params=_cparams(("arbitrary", "arbitrary")),
        name="matmul_groups",
    )(x, w)


def _mm_rope_kernel(x_ref, w_ref, cos_ref, sa_ref, sb_ref, *o_refs, n_rope, scale):
    y = _dot(x_ref[...], w_ref[...])
    rot = _rope(y[:, :n_rope], cos_ref[...], sa_ref[...], sb_ref[...])
    if scale != 1.0:
        rot = rot * scale
    o_refs[0][...] = rot.astype(o_refs[0].dtype)
    if len(o_refs) > 1:
        o_refs[1][...] = y[:, n_rope:].astype(o_refs[1].dtype)


def matmul_rope(x, w, tables, tm, n_rope, out_dtypes, scale=1.0, n=None):
    m, kdim = x.shape
    n = w.shape[-1] if n is None else n
    w_block = (kdim, n) if w.ndim == 2 else (None, kdim, n)
    lanes = tables[0].shape[1]
    widths = [n_rope] + ([n - n_rope] if n > n_rope else [])
    tab = pl.BlockSpec((tm, lanes), lambda i: (i, 0))
    outs = pl.pallas_call(
        functools.partial(_mm_rope_kernel, n_rope=n_rope, scale=scale),
        grid=(m // tm,),
        in_specs=[pl.BlockSpec((tm, kdim), lambda i: (i, 0)),
                  pl.BlockSpec(w_block, lambda i: (0,) * w.ndim, pipeline_mode=pl.Buffered(1)),
                  tab, tab, tab],
        out_specs=[pl.BlockSpec((tm, wd), lambda i: (i, 0)) for wd in widths],
        out_shape=[jax.ShapeDtypeStruct((m, wd), dt) for wd, dt in zip(widths, out_dtypes)],
        compiler_params=_cparams(("arbitrary",)),
        name="matmul_rope",
    )(x, w, *tables)
    return outs


def _mm_res_norm_kernel(x_ref, w_ref, res_ref, g_ref, *out_refs, emit_h):
    h = res_ref[...] + _dot(x_ref[...], w_ref[...])
    hn_refs = out_refs
    if emit_h:
        out_refs[0][...] = h
        hn_refs = out_refs[1:]
    inv = lax.rsqrt(jnp.mean(h * h, axis=-1, keepdims=True) + RMS_EPS)
    for j, hn_ref in enumerate(hn_refs):
        hn_ref[...] = (h * inv * g_ref[j:j + 1, :]).astype(hn_ref.dtype)


def matmul_residual_norm(x, w, res, gains, tm, norm_dtype, emit_h=True):
    m, kdim = x.shape
    n = w.shape[1]
    ng = gains.shape[0]
    row = lambda width: pl.BlockSpec((tm, width), lambda i: (i, 0))
    return pl.pallas_call(
        functools.partial(_mm_res_norm_kernel, emit_h=emit_h),
        grid=(m // tm,),
        in_specs=[row(kdim),
                  pl.BlockSpec((kdim, n), lambda i: (0, 0), pipeline_mode=pl.Buffered(1)),
                  row(n),
                  pl.BlockSpec((ng, n), lambda i: (0, 0))],
        out_specs=[row(n)] * (int(emit_h) + ng),
        out_shape=[jax.ShapeDtypeStruct((m, n), F32)] * int(emit_h) + [jax.ShapeDtypeStruct((m, n), norm_dtype)] * ng,
        compiler_params=_cparams(("arbitrary",)),
        name="matmul_residual_norm",
    )(x, w, res, gains)


def _mm_res_norm_blocks_kernel(x_ref, w_ref, res_ref, *rest, emit_h, head, first_block):
    if head:
        head_ref, g_ref, *out_refs = rest
        res = _padded_rows(res_ref, head_ref, first_block)
    else:
        g_ref, *out_refs = rest
        res = res_ref[...]
    h = res + _dot(x_ref[...], w_ref[...])
    hn_refs = out_refs
    if emit_h:
        out_refs[0][...] = h
        hn_refs = out_refs[1:]
    inv = lax.rsqrt(jnp.mean(h * h, axis=-1, keepdims=True) + RMS_EPS)
    for j, hn_ref in enumerate(hn_refs):
        hn_ref[...] = (h * inv * g_ref[j:j + 1, :]).astype(hn_ref.dtype)


def matmul_residual_norm_blocks(x, w, res, gains, n_batch, norm_dtype, emit_h, head=None, first_block=0):
    kdim = x.shape[1]
    n = w.shape[1]
    ng = gains.shape[0]
    nb_out = x.shape[0] // (n_batch * BLOCK)
    nb = nb_out + first_block
    m = n_batch * nb * BLOCK
    padded = lambda width: pl.BlockSpec((BLOCK, width), lambda i, t: (i * nb + t + first_block, 0))
    compact = lambda width: pl.BlockSpec((BLOCK, width), lambda i, t: (i * nb_out + t, 0))
    const = lambda shape: pl.BlockSpec(shape, lambda i, t: (0,) * len(shape))
    if head is not None:
        res_specs = [pl.BlockSpec((1, BLOCK, n), lambda i, t: (i, jnp.maximum(t + first_block - 1, 0), 0)),
                     const((BLOCK, n))]
        res_args = [res, head]
    else:
        res_specs, res_args = [padded(n)], [res]
    norm_spec = compact(n) if first_block else padded(n)
    norm_rows = n_batch * nb_out * BLOCK
    return pl.pallas_call(
        functools.partial(_mm_res_norm_blocks_kernel, emit_h=emit_h, head=head is not None, first_block=first_block),
        grid=(n_batch, nb_out),
        in_specs=[compact(kdim), pl.BlockSpec((kdim, n), lambda i, t: (0, 0), pipeline_mode=pl.Buffered(1))]
        + res_specs + [const((ng, n))],
        out_specs=[padded(n)] * int(emit_h) + [norm_spec] * ng,
        out_shape=[jax.ShapeDtypeStruct((m, n), F32)] * int(emit_h)
        + [jax.ShapeDtypeStruct((norm_rows, n), norm_dtype)] * ng,
        compiler_params=_cparams(("arbitrary", "arbitrary")),
        name="matmul_residual_norm_blocks",
    )(x, w, *res_args, gains)


def _lora_up_kernel(hw_ref, ha_ref, w2_ref, w0_ref, a2_ref, a0_ref, wl_ref, al_ref):
    wl_ref[...] = w0_ref[...] + _dot(hw_ref[...], w2_ref[...])
    al_ref[...] = a0_ref[...] + _dot(ha_ref[...], a2_ref[...])


def lora_up(hw, ha, w2, w0, a2, a0):
    m = hw.shape[0]
    e = w2.shape[1]
    return pl.pallas_call(
        _lora_up_kernel,
        out_shape=[jax.ShapeDtypeStruct((m, e), F32), jax.ShapeDtypeStruct((m, e), F32)],
        name="lora_up",
    )(hw, ha, w2, w0.reshape(1, e), a2, a0.reshape(1, e))


def _seg_sum(x, ones_bd):
    hi = x.astype(BF16)
    outs = []
    for c in range(x.shape[1] // MXU_TILE):
        sl = slice(c * MXU_TILE, (c + 1) * MXU_TILE)
        outs.append(_dot(hi[:, sl], ones_bd))
    return jnp.concatenate(outs, axis=1) if len(outs) > 1 else outs[0]


def _wkv_prompt_kernel(r_ref, k_ref, v_ref, z_ref, hw_ref, ha_ref, w2_ref, w0_ref, a2_ref, a0_ref,
                       kk_ref, ka_ref, rk_ref, gg_ref, gb_ref, yg_ref, sout_ref, s_ref):
    t_idx = pl.program_id(2)
    c = CHUNK
    hd = HEAD_DIM

    @pl.when(t_idx == 0)
    def _():
        s_ref[...] = jnp.zeros_like(s_ref)

    tb = r_ref.shape[1]
    nh = HEADS_PER_STREAM
    hw = nh * hd
    nc = tb // c

    li = lax.broadcasted_iota(jnp.int32, (MXU_TILE, MXU_TILE), 0) // hd
    lj = lax.broadcasted_iota(jnp.int32, (MXU_TILE, MXU_TILE), 1) // hd
    ones_bd = jnp.where(li == lj, 1.0, 0.0).astype(BF16)
    bi_ = lax.broadcasted_iota(jnp.int32, (tb, tb), 0)
    bj_ = lax.broadcasted_iota(jnp.int32, (tb, tb), 1)
    tri_incl = jnp.where((bj_ <= bi_) & (bj_ // c == bi_ // c), 1.0, 0.0).astype(BF16)
    ti = lax.broadcasted_iota(jnp.int32, (c, c), 0)
    tj = lax.broadcasted_iota(jnp.int32, (c, c), 1)
    ai = lax.broadcasted_iota(jnp.int32, (c, 2 * c), 0)
    aj = lax.broadcasted_iota(jnp.int32, (c, 2 * c), 1)
    upper = aj >= c
    aj_mod = jnp.where(upper, aj - c, aj)
    masks = dict(
        strict=tj < ti,
        eye=jnp.where(ti == tj, 1.0, 0.0).astype(F32),
        top_k=upper & (aj_mod < ai),
        bot=aj_mod <= ai)

    for st in range(r_ref.shape[2] // hw):
        _wkv_stream(st, hw, nc, r_ref, k_ref, v_ref, z_ref, hw_ref, ha_ref, w2_ref, w0_ref, a2_ref, a0_ref,
                    kk_ref, ka_ref, rk_ref, gg_ref, gb_ref, yg_ref, s_ref, ones_bd, tri_incl, masks)

    @pl.when(t_idx == pl.num_programs(2) - 1)
    def _():
        sout_ref[0] = s_ref[...]


def _wkv_stream(st, hw, nc, r_ref, k_ref, v_ref, z_ref, hw_ref, ha_ref, w2_ref, w0_ref, a2_ref, a0_ref,
                kk_ref, ka_ref, rk_ref, gg_ref, gb_ref, yg_ref, s_ref, ones_bd, tri_incl, masks):
    c = CHUNK
    hd = HEAD_DIM
    nh = hw // hd
    ls = slice(st * hw, (st + 1) * hw)
    h0 = st * nh
    r = r_ref[0, :, ls]
    k = k_ref[0, :, ls]
    v = v_ref[0, :, ls]
    wl = w0_ref[:, ls] + _dot(hw_ref[...], w2_ref[:, ls])
    al = a0_ref[:, ls] + _dot(ha_ref[...], a2_ref[:, ls])
    a = _sigmoid(al)
    lw = (-DECAY_SCALE * LOG2E) * _sigmoid(wl)
    kk = k * kk_ref[:, ls]
    n2 = _seg_sum(kk * kk, ones_bd)
    kk = kk / jnp.maximum(jnp.sqrt(n2), 1e-12)
    k2 = k * (1.0 + (a - 1.0) * ka_ref[:, ls])
    bb = kk * a

    lw_hi, lw_lo = _split_hi_lo(lw)
    g = _dot(tri_incl, lw_hi) + _dot(tri_incl, lw_lo)
    mid = lambda ci: g[ci * c + c // 2 - 1:ci * c + c // 2, :]
    gm = jnp.concatenate([jnp.broadcast_to(mid(ci), (c, hw)) for ci in range(nc)], axis=0)
    e_a = jnp.exp2(g - gm)
    e_prev = jnp.exp2(g - lw - gm)
    e_inv = jnp.exp2(gm - g)
    e1 = [jnp.exp2(mid(ci)) for ci in range(nc)]
    e2 = [jnp.exp2(g[ci * c + c - 1:ci * c + c, :] - mid(ci)) for ci in range(nc)]

    kkd = (kk * e_prev).astype(BF16)
    rd = (r * e_a).astype(BF16)
    bi = (bb * e_inv).astype(BF16)
    ki = (k2 * e_inv).astype(BF16)
    v_bf = v.astype(BF16)
    zeros_cv = jnp.zeros((c, hd), BF16)

    pairs = [(ci, h) for ci in range(nc) for h in range(nh)]
    rows = lambda ci: slice(ci * c, (ci + 1) * c)
    cols = lambda h: slice(h * hd, (h + 1) * hd)
    xs = {(ci, h): jnp.concatenate([kkd[rows(ci), cols(h)], rd[rows(ci), cols(h)]], axis=0) for ci, h in pairs}
    r1s = {(ci, h): jnp.concatenate([bi[rows(ci), cols(h)], ki[rows(ci), cols(h)]], axis=0) for ci, h in pairs}
    vs = {(ci, h): v_bf[rows(ci), cols(h)] for ci, h in pairs}
    a_mats = {p: _dot_nt(xs[p], r1s[p]) for p in pairs}
    lk_vs = {p: _dot(jnp.where(masks["top_k"], a_mats[p][:c, :], 0.0).astype(BF16),
                     jnp.concatenate([zeros_cv, vs[p]], axis=0)) for p in pairs}
    lps = {p: jnp.where(masks["strict"], a_mats[p][:c, :c], 0.0) for p in pairs}
    ts = {p: masks["eye"] - lps[p] for p in pairs}
    for _ in range(int(math.log2(c)) - 1):
        lpb = {p: lps[p].astype(BF16) for p in pairs}
        lps = {p: _dot(lpb[p], lpb[p]) for p in pairs}
        ts = {p: _dot(ts[p].astype(BF16), (masks["eye"] + lps[p]).astype(BF16)) for p in pairs}
    a_bots = {p: jnp.where(masks["bot"], a_mats[p][c:, :], 0.0).astype(BF16) for p in pairs}
    t_bf = {p: ts[p].astype(BF16) for p in pairs}

    state = [s_ref[h0 + h] for h in range(nh)]
    y_rows = []
    for ci in range(nc):
        hs = range(nh)
        sms = [state[h] * e1[ci][:, cols(h)] for h in hs]
        p_mats = [_dot_nt(xs[ci, h], sms[h].astype(BF16)) for h in hs]
        us = [-_dot(t_bf[ci, h], (p_mats[h][:c, :] + lk_vs[ci, h]).astype(BF16)) for h in hs]
        uvs = [jnp.concatenate([us[h].astype(BF16), vs[ci, h]], axis=0) for h in hs]
        ys = [p_mats[h][c:, :] + _dot(a_bots[ci, h], uvs[h]) for h in hs]
        state = [(sms[h] + _dot_tn(uvs[h], r1s[ci, h])) * e2[ci][:, cols(h)] for h in hs]
        y_rows.append(jnp.concatenate(ys, axis=1))
    for h in range(nh):
        s_ref[h0 + h] = state[h]
    y = jnp.concatenate(y_rows, axis=0) if nc > 1 else y_rows[0]

    inv_n = 1.0 / hd
    mean = _seg_sum(y, ones_bd) * inv_n
    yc = y - mean
    var = _seg_sum(yc * yc, ones_bd) * inv_n
    yn = yc * lax.rsqrt(var + GN_EPS) * gg_ref[:, ls] + gb_ref[:, ls]
    bonus = _seg_sum(r * k2 * rk_ref[:, ls], ones_bd)
    yg_ref[:, ls] = ((yn + bonus * v) * _silu(z_ref[0, :, ls])).astype(BF16)


def wkv_prompt(rkvz, hw_act, ha_act, w2, w0, a2, a0, k_k, k_a, r_k, gn_g, gn_b, n_batch):
    _, m, e = rkvz.shape
    p = m // n_batch
    tb = WKV_ROWS
    nt = p // tb
    hw = HEADS_PER_STEP * HEAD_DIM
    nh = e // HEAD_DIM
    row = lambda i, g, t: (i * nt + t, g)
    proj = lambda q: pl.BlockSpec((1, tb, hw), lambda i, g, t: (q, i * nt + t, g))
    par = pl.BlockSpec((1, hw), lambda i, g, t: (0, g))
    lr = w2.shape[0]
    hid = pl.BlockSpec((tb, lr), lambda i, g, t: (i * nt + t, 0))
    up = pl.BlockSpec((lr, hw), lambda i, g, t: (0, g))
    return pl.pallas_call(
        _wkv_prompt_kernel,
        grid=(n_batch, e // hw, nt),
        in_specs=[proj(0), proj(1), proj(2), proj(3), hid, hid, up, par, up, par,
                  par, par, par, par, par],
        out_specs=[pl.BlockSpec((tb, hw), row),
                   pl.BlockSpec((1, HEADS_PER_STEP, HEAD_DIM, HEAD_DIM), lambda i, g, t: (i, g, 0, 0))],
        out_shape=[jax.ShapeDtypeStruct((m, e), BF16),
                   jax.ShapeDtypeStruct((n_batch, nh, HEAD_DIM, HEAD_DIM), F32)],
        scratch_shapes=[pltpu.VMEM((HEADS_PER_STEP, HEAD_DIM, HEAD_DIM), F32)],
        compiler_params=_cparams(("arbitrary", "arbitrary", "arbitrary")),
        name="wkv_prompt",
    )(rkvz, rkvz, rkvz, rkvz, hw_act, ha_act, w2, w0.reshape(1, e), a2, a0.reshape(1, e),
      k_k.reshape(1, e), k_a.reshape(1, e), r_k.reshape(1, e), gn_g.reshape(1, e), gn_b.reshape(1, e))


def _wkv_sample_kernel(r_ref, k_ref, v_ref, z_ref, wl_ref, al_ref, kk_ref, ka_ref, rk_ref, gg_ref, gb_ref,
                       s_ref, yg_ref, sout_ref, y_scr):
    hd = HEAD_DIM
    r = r_ref[0]
    k = k_ref[0]
    v = v_ref[0]
    a = _sigmoid(al_ref[0])
    d = jnp.exp2((-DECAY_SCALE * LOG2E) * _sigmoid(wl_ref[0]))
    kk = k * kk_ref[...]
    kk = kk / jnp.maximum(jnp.sqrt(jnp.sum(kk * kk, axis=-1, keepdims=True)), 1e-12)
    k2 = k * (1.0 + (a - 1.0) * ka_ref[...])
    bb = kk * a
    nh = r.shape[0]
    ii = lax.broadcasted_iota(jnp.int32, (hd, hd), 0)
    jj = lax.broadcasted_iota(jnp.int32, (hd, hd), 1)
    eye = ii == jj

    row = lambda x, h: x[h:h + 1, :]
    group = 8
    for h0 in range(0, nh, group):
        hs = range(h0, h0 + group)
        s = {h: s_ref[0, h] for h in hs}
        sa = {h: jnp.sum(s[h] * row(kk, h), axis=-1, keepdims=True) for h in hs}
        v_col = {h: jnp.sum(jnp.where(eye, row(v, h), 0.0), axis=-1, keepdims=True) for h in hs}
        s_new = {h: s[h] * row(d, h) - sa[h] * row(bb, h) + v_col[h] * row(k2, h) for h in hs}
        y_col = {h: jnp.sum(s_new[h] * row(r, h), axis=-1, keepdims=True) for h in hs}
        for h in hs:
            sout_ref[0, h] = s_new[h]
            y_scr[h:h + 1, :] = jnp.sum(jnp.where(eye, y_col[h], 0.0), axis=0, keepdims=True)

    y = y_scr[...]
    mean = jnp.mean(y, axis=-1, keepdims=True)
    yc = y - mean
    var = jnp.mean(yc * yc, axis=-1, keepdims=True)
    yn = yc * lax.rsqrt(var + GN_EPS) * gg_ref[...] + gb_ref[...]
    bonus = jnp.sum(r * k2 * rk_ref[...], axis=-1, keepdims=True)
    yg_ref[0] = ((yn + bonus * v) * _silu(z_ref[0])).astype(BF16)


def wkv_sample(rkvz, wl, al, k_k, k_a, r_k, gn_g, gn_b, state):
    _, m, e = rkvz.shape
    nh = e // HEAD_DIM
    hd = HEAD_DIM
    rkvz4 = rkvz.reshape(4, m, nh, hd)
    proj = lambda q: pl.BlockSpec((None, 1, nh, hd), lambda i: (q, i, 0, 0))
    tok = pl.BlockSpec((1, nh, hd), lambda i: (i, 0, 0))
    par = pl.BlockSpec((nh, hd), lambda i: (0, 0))
    st = pl.BlockSpec((1, nh, hd, hd), lambda i: (i, 0, 0, 0))
    as_heads = lambda x: x.reshape(nh, hd)
    return pl.pallas_call(
        _wkv_sample_kernel,
        grid=(m,),
        in_specs=[proj(0), proj(1), proj(2), proj(3), tok, tok, par, par, par, par, par, st],
        out_specs=[tok, st],
        out_shape=[jax.ShapeDtypeStruct((m, nh, hd), BF16), jax.ShapeDtypeStruct(state.shape, F32)],
        scratch_shapes=[pltpu.VMEM((nh, hd), F32)],
        compiler_params=_cparams(("arbitrary",)),
        name="wkv_sample",
    )(rkvz4, rkvz4, rkvz4, rkvz4, wl.reshape(m, nh, hd), al.reshape(m, nh, hd),
      as_heads(k_k), as_heads(k_a), as_heads(r_k), as_heads(gn_g), as_heads(gn_b), state)


def rope_tables(pos):
    half = ROPE_DIM // 2
    inv_freq = ROPE_THETA ** (-jnp.arange(half, dtype=F32) * 2.0 / ROPE_DIM)
    ang = pos.astype(F32)[:, None] * inv_freq[None, :]
    cos = jnp.cos(ang)
    sin = jnp.sin(ang)
    rows = pos.shape[0]
    ones = jnp.ones((rows, HEAD_DIM - ROPE_DIM), F32)
    zeros_h = jnp.zeros((rows, half), F32)
    zeros_r = jnp.zeros((rows, HEAD_DIM - ROPE_DIM), F32)
    cos_h = jnp.concatenate([cos, cos, ones], axis=1)
    sa_h = jnp.concatenate([-sin, zeros_h, zeros_r], axis=1)
    sb_h = jnp.concatenate([zeros_h, sin, zeros_r], axis=1)
    two = lambda t: jnp.concatenate([t, t], axis=1)
    return two(cos_h), two(sa_h), two(sb_h)


def _attn_prompt_kernel(sink_ref, q_ref, kc_ref, kp_ref, vc_ref, vp_ref, z_ref, o_ref, *, first_block):
    n = pl.program_id(1) + first_block
    hd = HEAD_DIM
    blk = q_ref.shape[0]
    n_kv = kc_ref.shape[1] // hd
    grp = q_ref.shape[1] // (n_kv * hd)
    qi = lax.broadcasted_iota(jnp.int32, (blk, 2 * blk), 0)
    kj = lax.broadcasted_iota(jnp.int32, (blk, 2 * blk), 1) - blk
    kpos = n * blk + kj
    diff = qi - kj
    valid = (kpos >= LEAD) & (diff >= 0) & (diff <= WINDOW)
    k_all = jnp.concatenate([kp_ref[...], kc_ref[...]], axis=0).astype(BF16)
    v_all = jnp.concatenate([vp_ref[...], vc_ref[...]], axis=0).astype(BF16)

    def scores(h):
        k_h = k_all[:, h * hd:(h + 1) * hd]
        return [_dot_nt(q_ref[:, (h * grp + gi) * hd:(h * grp + gi + 1) * hd], k_h) for gi in range(grp)]

    outs = []
    s_next = scores(0)
    for h in range(n_kv):
        s_cur = s_next
        if h + 1 < n_kv:
            s_next = scores(h + 1)
        v_h = v_all[:, h * hd:(h + 1) * hd]
        for g0 in range(0, grp, SOFTMAX_BATCH):
            gs = range(g0, min(g0 + SOFTMAX_BATCH, grp))
            sks = {gi: sink_ref[h * grp + gi] * LOG2E for gi in gs}
            ss = {gi: jnp.where(valid, s_cur[gi], -jnp.inf) for gi in gs}
            ms = {gi: jnp.maximum(jnp.max(ss[gi], axis=-1, keepdims=True), sks[gi]) for gi in gs}
            ps = {gi: jnp.exp2(ss[gi] - ms[gi]) for gi in gs}
            dens = {gi: jnp.sum(ps[gi], axis=-1, keepdims=True) + jnp.exp2(sks[gi] - ms[gi]) for gi in gs}
            outs += [_dot(ps[gi].astype(BF16), v_h) / dens[gi] for gi in gs]
    att = jnp.concatenate(outs, axis=1)
    o_ref[...] = (att * _silu(z_ref[...])).astype(o_ref.dtype)


def attn_prompt(sinks, q, k, v, z, n_batch, first_block):
    m, e = q.shape
    nb = m // (n_batch * BLOCK)
    nb_out = nb - first_block
    kw = k.shape[1]
    cur = lambda i, n: (i * nb + n + first_block, 0)
    prv = lambda i, n: (i * nb + jnp.maximum(n + first_block - 1, 0), 0)
    return pl.pallas_call(
        functools.partial(_attn_prompt_kernel, first_block=first_block),
        grid=(n_batch, nb_out),
        in_specs=[pl.BlockSpec(memory_space=pltpu.SMEM),
                  pl.BlockSpec((BLOCK, e), cur),
                  pl.BlockSpec((BLOCK, kw), cur), pl.BlockSpec((BLOCK, kw), prv),
                  pl.BlockSpec((BLOCK, kw), cur), pl.BlockSpec((BLOCK, kw), prv),
                  pl.BlockSpec((BLOCK, e), cur)],
        out_specs=pl.BlockSpec((BLOCK, e), lambda i, n: (i * nb_out + n, 0)),
        out_shape=jax.ShapeDtypeStruct((n_batch * nb_out * BLOCK, e), BF16),
        compiler_params=_cparams(("arbitrary", "arbitrary")),
        name="attn_prompt",
    )(sinks, q, k, k, v, v, z)


def _attn_sample_kernel(sink_ref, q_ref, kc_ref, vc_ref, kn_ref, vn_ref, z_ref, o_ref, ko_ref, vo_ref):
    hd = HEAD_DIM
    win = kc_ref.shape[1]
    n_kv = kc_ref.shape[2] // hd
    nq = q_ref.shape[1]
    grp = nq // n_kv
    pad = 8
    kc = kc_ref[0]
    vc = vc_ref[0]
    kn = kn_ref[0]
    vn = vn_ref[0]
    first = lax.broadcasted_iota(jnp.int32, (pad, kc.shape[1]), 0) == 0
    k_all = jnp.concatenate([kc, jnp.where(first, kn, 0.0)], axis=0).astype(BF16)
    v_all = jnp.concatenate([vc, jnp.where(first, vn, 0.0)], axis=0).astype(BF16)
    col = lax.broadcasted_iota(jnp.int32, (grp, win + pad), 1)
    valid = (col <= win) & (win - col <= WINDOW)
    q = q_ref[0].astype(BF16)
    row_i = lax.broadcasted_iota(jnp.int32, (grp, 1), 0)
    hs = range(n_kv)
    sks = []
    for h in hs:
        sk = jnp.zeros((grp, 1), F32)
        for gi in range(grp):
            sk = jnp.where(row_i == gi, sink_ref[h * grp + gi] * LOG2E, sk)
        sks.append(sk)
    ss = [jnp.where(valid, _dot_nt(q[h * grp:(h + 1) * grp, :], k_all[:, h * hd:(h + 1) * hd]), -jnp.inf)
          for h in hs]
    ms = [jnp.maximum(jnp.max(ss[h], axis=-1, keepdims=True), sks[h]) for h in hs]
    ps = [jnp.exp2(ss[h] - ms[h]) for h in hs]
    dens = [jnp.sum(ps[h], axis=-1, keepdims=True) + jnp.exp2(sks[h] - ms[h]) for h in hs]
    outs = [_dot(ps[h].astype(BF16), v_all[:, h * hd:(h + 1) * hd]) / dens[h] for h in hs]
    att = jnp.concatenate(outs, axis=0)
    o_ref[0] = (att * _silu(z_ref[0])).astype(o_ref.dtype)
    last = lax.broadcasted_iota(jnp.int32, kc.shape, 0) == win - 1
    ko_ref[0] = jnp.where(last, kn, pltpu.roll(kc, win - 1, axis=0))
    vo_ref[0] = jnp.where(last, vn, pltpu.roll(vc, win - 1, axis=0))


def attn_sample(sinks, q, cache_k, cache_v, k_new, v_new, z):
    m, win, kw = cache_k.shape
    nq = q.shape[1]
    hd = HEAD_DIM
    tok = pl.BlockSpec((1, nq, hd), lambda i: (i, 0, 0))
    cache = pl.BlockSpec((1, win, kw), lambda i: (i, 0, 0))
    new = pl.BlockSpec((1, 1, kw), lambda i: (i, 0, 0))
    return pl.pallas_call(
        _attn_sample_kernel,
        grid=(m,),
        in_specs=[pl.BlockSpec(memory_space=pltpu.SMEM), tok, cache, cache, new, new, tok],
        out_specs=[tok, cache, cache],
        out_shape=[jax.ShapeDtypeStruct((m, nq, hd), BF16),
                   jax.ShapeDtypeStruct(cache_k.shape, F32), jax.ShapeDtypeStruct(cache_v.shape, F32)],
        compiler_params=_cparams(("arbitrary",)),
        name="attn_sample",
    )(sinks, q, cache_k, cache_v, k_new, v_new, z)


def _pad_lora(w_down, w_up):
    r = w_down.shape[1]
    return (jnp.pad(w_down, ((0, 0), (0, LORA_PAD - r))).astype(BF16),
            jnp.pad(w_up, ((0, LORA_PAD - r), (0, 0))).astype(BF16))


def kernel(x_prompt, x_sample, state_wkv, state_shift, cache_k, cache_v, meta_tokens, a_norm, a_mu, a_w_rkvz,
           a_w0, a_w1, a_w2, a_a0, a_a1, a_a2, a_k_k, a_k_a, a_r_k, a_gn_g, a_gn_b, a_w_out, kv_norm, w_kv,
           b_norm, b_w_qz, b_sinks, b_w_o, final_norm):
    nb, seq, d = x_prompt.shape
    db, dseq, _ = x_sample.shape
    assert dseq == 1 and a_norm.shape[0] == 1 and b_norm.shape[0] == 1
    e = a_w_rkvz.shape[3]
    win = cache_k.shape[1]
    p_len = LEAD + N_META + seq
    assert p_len % BLOCK == 0 and (LEAD + N_META) == BLOCK
    kvw = N_KV_HEADS * HEAD_DIM

    w_rkvz = a_w_rkvz[0].astype(BF16)
    w1, w2 = _pad_lora(a_w1[0], a_w2[0])
    a1, a2 = _pad_lora(a_a1[0], a_a2[0])
    w_out = a_w_out[0].astype(BF16)
    w_kv_bf = w_kv.astype(BF16)
    w_qz = b_w_qz.astype(BF16)
    w_o = b_w_o[0].astype(BF16)
    mu = a_mu[0]
    sinks = b_sinks[0]
    gains_b = jnp.stack([kv_norm, b_norm[0]])

    tm = p_len // 8

    head = jnp.concatenate([jnp.zeros((LEAD, d), F32), meta_tokens], axis=0)
    xm, hw_p, ha_p, x_last = norm_shift_prompt(x_prompt, head, a_norm[0], mu, w1, a1)
    p_state_shift = x_last.reshape(1, nb, d)
    rkvz = matmul_groups(xm, w_rkvz, tm, F32)
    yg, p_state = wkv_prompt(rkvz, hw_p, ha_p, w2, a_w0[0], a2, a_a0[0], a_k_k[0], a_k_a[0],
                             a_r_k[0].reshape(-1), a_gn_g[0], a_gn_b[0], nb)
    hp, hn_kv, hn_b = matmul_residual_norm_blocks(yg, w_out, x_prompt, gains_b, nb, BF16, True, head=head)

    pos_p = jnp.maximum(jnp.arange(p_len, dtype=jnp.int32) - LEAD, 0)
    tabs_p = tuple(jnp.tile(t, (nb, 1)) for t in rope_tables(pos_p))
    k_p, v_p = matmul_rope(hn_kv, w_kv_bf, tabs_p, tm, kvw, (F32, F32))
    q_p, = matmul_rope(hn_b, w_qz, tabs_p, tm, e, (BF16,), scale=Q_SCALE, n=e)
    z_p = matmul_groups(hn_b[None], w_qz, tm, F32, n=e, col=1)[0]
    skip = (LEAD + N_META) // BLOCK
    att = attn_prompt(sinks, q_p, k_p, v_p, z_p, nb, skip)
    y_prompt, = matmul_residual_norm_blocks(att, w_o, hp, final_norm[None], nb, F32, False, first_block=skip)
    y_prompt = y_prompt.reshape(nb, seq, d)
    tail = lambda t: t.reshape(nb, p_len, kvw)[:, -win:].reshape(nb, win, N_KV_HEADS, HEAD_DIM)
    p_cache_k = tail(k_p)
    p_cache_v = tail(v_p)

    hs = x_sample.reshape(db, d)
    xm_s, hw_s, ha_s, xn_s = norm_shift_sample(hs, state_shift[0], a_norm[0], mu, w1, a1)
    rkvz_s = matmul_groups(xm_s, w_rkvz, db, F32)
    wl_s, al_s = lora_up(hw_s, ha_s, w2, a_w0[0], a2, a_a0[0])
    yg_s, s_state = wkv_sample(rkvz_s, wl_s, al_s, a_k_k[0], a_k_a[0], a_r_k[0].reshape(-1), a_gn_g[0],
                               a_gn_b[0], state_wkv[0])
    hs, hn_kv_s, hn_b_s = matmul_residual_norm(yg_s.reshape(db, e), w_out, hs, gains_b, db, BF16)
    tabs_s = rope_tables(jnp.full((db,), PAST_LEN, jnp.int32))
    k_s, v_s = matmul_rope(hn_kv_s, w_kv_bf, tabs_s, db, kvw, (F32, F32))
    q_s, = matmul_rope(hn_b_s, w_qz, tabs_s, db, e, (F32,), scale=Q_SCALE, n=e)
    z_s = matmul_groups(hn_b_s[None], w_qz, db, F32, n=e, col=1)[0]
    nq = e // HEAD_DIM
    att_s, s_cache_k, s_cache_v = attn_sample(
        sinks, q_s.reshape(db, nq, HEAD_DIM), cache_k.reshape(db, win, kvw), cache_v.reshape(db, win, kvw),
        k_s.reshape(db, 1, kvw), v_s.reshape(db, 1, kvw), z_s.reshape(db, nq, HEAD_DIM))
    y_s, = matmul_residual_norm(att_s.reshape(db, e), w_o, hs, final_norm[None], db, F32, emit_h=False)
    y_sample = y_s.reshape(db, 1, d)

    return (y_prompt, y_sample, p_state[None], p_state_shift,
            p_cache_k, p_cache_v,
            s_state[None], xn_s[None],
            s_cache_k.reshape(cache_k.shape), s_cache_v.reshape(cache_v.shape))
```

```python
import functools
import math

import jax
import jax.numpy as jnp
from jax import lax
from jax.experimental import pallas as pl
from jax.experimental.pallas import tpu as pltpu

F32 = jnp.float32
BF16 = jnp.bfloat16

HEAD_DIM = 64
N_KV_HEADS = 8
WINDOW = 128
BLOCK = 128
ROPE_DIM = HEAD_DIM // 4
ROPE_THETA = 500000.0
N_META = 16
PAST_LEN = 16384
RMS_EPS = 1e-6
GN_EPS = 64e-5
NORM_FLOOR_SQ = 1e-24
LEAD = (-N_META) % BLOCK
CHUNK = 64
WKV_ROWS = 128
HEADS_PER_STREAM = 32
HEADS_PER_STEP = 32
LORA_PAD = 128
MXU_TILE = 256
ROPE_SLAB = 512
VMEM_LIMIT = 48 * 1024 * 1024
LOG2E = 1.0 / math.log(2.0)
DECAY_SCALE = math.exp(-0.5)
SOFTMAX_BATCH = 4
Q_SCALE = HEAD_DIM ** -0.5 * LOG2E


def _cparams(sem):
    return pltpu.CompilerParams(dimension_semantics=sem, vmem_limit_bytes=VMEM_LIMIT)


def _sigmoid(x):
    return 1.0 / (1.0 + jnp.exp2(x * (-LOG2E)))


def _silu(x):
    return x * _sigmoid(x)


def _dot(a, b):
    return jnp.dot(a, b, preferred_element_type=F32)


def _dot_nt(a, b):
    return lax.dot_general(a, b, (((1,), (1,)), ((), ())), preferred_element_type=F32)


def _dot_tn(a, b):
    return lax.dot_general(a, b, (((0,), (0,)), ((), ())), preferred_element_type=F32)


def _split_hi_lo(x):
    hi = x.astype(BF16)
    lo = (x - hi.astype(F32)).astype(BF16)
    return hi, lo


def _mixes(xn, prev, mu_ref, w1_ref, a1_ref, xm_ref, hw_ref, ha_ref):
    xx = prev - xn
    n_proj = xm_ref.shape[0]
    for p in range(n_proj):
        xm_ref[p] = (xn + xx * mu_ref[p:p + 1, :]).astype(xm_ref.dtype)
    xw = (xn + xx * mu_ref[n_proj:n_proj + 1, :]).astype(BF16)
    xa = (xn + xx * mu_ref[n_proj + 1:n_proj + 2, :]).astype(BF16)
    hw_ref[...] = jnp.tanh(_dot(xw, w1_ref[...])).astype(hw_ref.dtype)
    ha_ref[...] = _dot(xa, a1_ref[...]).astype(ha_ref.dtype)


def _padded_rows(x_ref, head_ref, first_block):
    return jnp.where(pl.program_id(1) + first_block == 0, head_ref[...], x_ref[0])


def _norm_shift_kernel(x_ref, head_ref, g_ref, mu_ref, w1_ref, a1_ref, xm_ref, hw_ref, ha_ref, last_ref, carry_ref):
    @pl.when(pl.program_id(1) == 0)
    def _():
        carry_ref[...] = jnp.zeros_like(carry_ref)

    x = _padded_rows(x_ref, head_ref, 0)
    tm = x.shape[0]
    xn = x * lax.rsqrt(jnp.mean(x * x, axis=-1, keepdims=True) + RMS_EPS) * g_ref[...]
    rolled = pltpu.roll(xn, 1, axis=0)
    row = lax.broadcasted_iota(jnp.int32, xn.shape, 0)
    prev = jnp.where(row == 0, carry_ref[0:1, :], rolled)
    _mixes(xn, prev, mu_ref, w1_ref, a1_ref, xm_ref, hw_ref, ha_ref)
    carry_ref[0:1, :] = xn[tm - 1:tm, :]
    last_ref[0] = xn[tm - 1:tm, :]


def norm_shift_prompt(x, head, g, mu, w1, a1):
    b, seq, d = x.shape
    tm = BLOCK
    p = tm + seq
    n_mix = mu.shape[0]
    n_proj = n_mix - 2
    lr = w1.shape[1]
    nt = p // tm
    const = lambda shape: pl.BlockSpec(shape, lambda i, t: (0,) * len(shape))
    hid = pl.BlockSpec((tm, lr), lambda i, t: (i * nt + t, 0))
    return pl.pallas_call(
        _norm_shift_kernel,
        grid=(b, nt),
        in_specs=[pl.BlockSpec((1, tm, d), lambda i, t: (i, jnp.maximum(t - 1, 0), 0)),
                  const((tm, d)), const((1, d)), const((n_mix, d)), const((d, lr)), const((d, lr))],
        out_specs=[pl.BlockSpec((n_proj, tm, d), lambda i, t: (0, i * nt + t, 0)), hid, hid,
                   pl.BlockSpec((1, 1, d), lambda i, t: (i, 0, 0))],
        out_shape=[jax.ShapeDtypeStruct((n_proj, b * p, d), BF16),
                   jax.ShapeDtypeStruct((b * p, lr), BF16), jax.ShapeDtypeStruct((b * p, lr), BF16),
                   jax.ShapeDtypeStruct((b, 1, d), F32)],
        scratch_shapes=[pltpu.VMEM((8, d), F32)],
        compiler_params=_cparams(("arbitrary", "arbitrary")),
        name="norm_shift_prompt",
    )(x, head, g.reshape(1, d), mu, w1, a1)


def _norm_shift_sample_kernel(x_ref, prev_ref, g_ref, mu_ref, w1_ref, a1_ref, xm_ref, hw_ref, ha_ref, xn_ref):
    x = x_ref[...]
    xn = x * lax.rsqrt(jnp.mean(x * x, axis=-1, keepdims=True) + RMS_EPS) * g_ref[...]
    xn_ref[...] = xn
    _mixes(xn, prev_ref[...], mu_ref, w1_ref, a1_ref, xm_ref, hw_ref, ha_ref)


def norm_shift_sample(x, prev, g, mu, w1, a1):
    m, d = x.shape
    lr = w1.shape[1]
    return pl.pallas_call(
        _norm_shift_sample_kernel,
        out_shape=[jax.ShapeDtypeStruct((mu.shape[0] - 2, m, d), BF16),
                   jax.ShapeDtypeStruct((m, lr), BF16), jax.ShapeDtypeStruct((m, lr), BF16),
                   jax.ShapeDtypeStruct((m, d), F32)],
        name="norm_shift_sample",
    )(x, prev, g.reshape(1, d), mu, w1, a1)


def _rope(y, cos, sin_a, sin_b):
    half = ROPE_DIM // 2
    step = ROPE_SLAB
    rep = step // cos.shape[1]
    tile = lambda t: jnp.concatenate([t] * rep, axis=1)
    cos_t, sa_t, sb_t = tile(cos), tile(sin_a), tile(sin_b)
    outs = []
    for j in range(y.shape[1] // step):
        ys = y[:, j * step:(j + 1) * step]
        outs.append(ys * cos_t + pltpu.roll(ys, step - half, axis=1) * sa_t + pltpu.roll(ys, half, axis=1) * sb_t)
    return jnp.concatenate(outs, axis=1) if len(outs) > 1 else outs[0]


def _mm_group_kernel(x_ref, w_ref, o_ref):
    o_ref[0] = _dot(x_ref[0], w_ref[0]).astype(o_ref.dtype)


def matmul_groups(x, w, tm, out_dtype, n=None, col=0):
    g = w.shape[0]
    n = w.shape[2] if n is None else n
    _, m, kdim = x.shape
    return pl.pallas_call(
        _mm_group_kernel,
        grid=(g, m // tm),
        in_specs=[pl.BlockSpec((1, tm, kdim), lambda q, i: (q, i, 0)),
                  pl.BlockSpec((1, kdim, n), lambda q, i: (q, 0, col))],
        out_specs=pl.BlockSpec((1, tm, n), lambda q, i: (q, i, 0)),
        out_shape=jax.ShapeDtypeStruct((g, m, n), out_dtype),
        compiler_params=_cparams(("arbitrary", "arbitrary")),
        name="matmul_groups",
    )(x, w)


def _mm_rope_kernel(x_ref, w_ref, cos_ref, sa_ref, sb_ref, *o_refs, n_rope, scale):
    y = _dot(x_ref[...], w_ref[...])
    rot = _rope(y[:, :n_rope], cos_ref[...], sa_ref[...], sb_ref[...])
    if scale != 1.0:
        rot = rot * scale
    o_refs[0][...] = rot.astype(o_refs[0].dtype)
    if len(o_refs) > 1:
        o_refs[1][...] = y[:, n_rope:].astype(o_refs[1].dtype)


def matmul_rope(x, w, tables, tm, n_rope, out_dtypes, scale=1.0, n=None):
    m, kdim = x.shape
    n = w.shape[-1] if n is None else n
    w_block = (kdim, n) if w.ndim == 2 else (None, kdim, n)
    lanes = tables[0].shape[1]
    widths = [n_rope] + ([n - n_rope] if n > n_rope else [])
    tab = pl.BlockSpec((tm, lanes), lambda i: (i, 0))
    outs = pl.pallas_call(
        functools.partial(_mm_rope_kernel, n_rope=n_rope, scale=scale),
        grid=(m // tm,),
        in_specs=[pl.BlockSpec((tm, kdim), lambda i: (i, 0)),
                  pl.BlockSpec(w_block, lambda i: (0,) * w.ndim, pipeline_mode=pl.Buffered(1)),
                  tab, tab, tab],
        out_specs=[pl.BlockSpec((tm, wd), lambda i: (i, 0)) for wd in widths],
        out_shape=[jax.ShapeDtypeStruct((m, wd), dt) for wd, dt in zip(widths, out_dtypes)],
        compiler_params=_cparams(("arbitrary",)),
        name="matmul_rope",
    )(x, w, *tables)
    return outs


def _mm_res_norm_kernel(x_ref, w_ref, res_ref, g_ref, *out_refs, emit_h):
    h = res_ref[...] + _dot(x_ref[...], w_ref[...])
    hn_refs = out_refs
    if emit_h:
        out_refs[0][...] = h
        hn_refs = out_refs[1:]
    inv = lax.rsqrt(jnp.mean(h * h, axis=-1, keepdims=True) + RMS_EPS)
    for j, hn_ref in enumerate(hn_refs):
        hn_ref[...] = (h * inv * g_ref[j:j + 1, :]).astype(hn_ref.dtype)


def matmul_residual_norm(x, w, res, gains, tm, norm_dtype, emit_h=True):
    m, kdim = x.shape
    n = w.shape[1]
    ng = gains.shape[0]
    row = lambda width: pl.BlockSpec((tm, width), lambda i: (i, 0))
    return pl.pallas_call(
        functools.partial(_mm_res_norm_kernel, emit_h=emit_h),
        grid=(m // tm,),
        in_specs=[row(kdim),
                  pl.BlockSpec((kdim, n), lambda i: (0, 0), pipeline_mode=pl.Buffered(1)),
                  row(n),
                  pl.BlockSpec((ng, n), lambda i: (0, 0))],
        out_specs=[row(n)] * (int(emit_h) + ng),
        out_shape=[jax.ShapeDtypeStruct((m, n), F32)] * int(emit_h) + [jax.ShapeDtypeStruct((m, n), norm_dtype)] * ng,
        compiler_params=_cparams(("arbitrary",)),
        name="matmul_residual_norm",
    )(x, w, res, gains)


def _mm_res_norm_blocks_kernel(x_ref, w_ref, res_ref, *rest, emit_h, head, first_block):
    if head:
        head_ref, g_ref, *out_refs = rest
        res = _padded_rows(res_ref, head_ref, first_block)
    else:
        g_ref, *out_refs = rest
        res = res_ref[...]
    h = res + _dot(x_ref[...], w_ref[...])
    hn_refs = out_refs
    if emit_h:
        out_refs[0][...] = h
        hn_refs = out_refs[1:]
    inv = lax.rsqrt(jnp.mean(h * h, axis=-1, keepdims=True) + RMS_EPS)
    for j, hn_ref in enumerate(hn_refs):
        hn_ref[...] = (h * inv * g_ref[j:j + 1, :]).astype(hn_ref.dtype)


def matmul_residual_norm_blocks(x, w, res, gains, n_batch, norm_dtype, emit_h, head=None, first_block=0):
    kdim = x.shape[1]
    n = w.shape[1]
    ng = gains.shape[0]
    nb_out = x.shape[0] // (n_batch * BLOCK)
    nb = nb_out + first_block
    m = n_batch * nb * BLOCK
    padded = lambda width: pl.BlockSpec((BLOCK, width), lambda i, t: (i * nb + t + first_block, 0))
    compact = lambda width: pl.BlockSpec((BLOCK, width), lambda i, t: (i * nb_out + t, 0))
    const = lambda shape: pl.BlockSpec(shape, lambda i, t: (0,) * len(shape))
    if head is not None:
        res_specs = [pl.BlockSpec((1, BLOCK, n), lambda i, t: (i, jnp.maximum(t + first_block - 1, 0), 0)),
                     const((BLOCK, n))]
        res_args = [res, head]
    else:
        res_specs, res_args = [padded(n)], [res]
    norm_spec = compact(n) if first_block else padded(n)
    norm_rows = n_batch * nb_out * BLOCK
    return pl.pallas_call(
        functools.partial(_mm_res_norm_blocks_kernel, emit_h=emit_h, head=head is not None, first_block=first_block),
        grid=(n_batch, nb_out),
        in_specs=[compact(kdim), pl.BlockSpec((kdim, n), lambda i, t: (0, 0), pipeline_mode=pl.Buffered(1))]
        + res_specs + [const((ng, n))],
        out_specs=[padded(n)] * int(emit_h) + [norm_spec] * ng,
        out_shape=[jax.ShapeDtypeStruct((m, n), F32)] * int(emit_h)
        + [jax.ShapeDtypeStruct((norm_rows, n), norm_dtype)] * ng,
        compiler_params=_cparams(("arbitrary", "arbitrary")),
        name="matmul_residual_norm_blocks",
    )(x, w, *res_args, gains)


def _lora_up_kernel(hw_ref, ha_ref, w2_ref, w0_ref, a2_ref, a0_ref, wl_ref, al_ref):
    wl_ref[...] = w0_ref[...] + _dot(hw_ref[...], w2_ref[...])
    al_ref[...] = a0_ref[...] + _dot(ha_ref[...], a2_ref[...])


def lora_up(hw, ha, w2, w0, a2, a0):
    m = hw.shape[0]
    e = w2.shape[1]
    return pl.pallas_call(
        _lora_up_kernel,
        out_shape=[jax.ShapeDtypeStruct((m, e), F32), jax.ShapeDtypeStruct((m, e), F32)],
        name="lora_up",
    )(hw, ha, w2, w0.reshape(1, e), a2, a0.reshape(1, e))


def _seg_sum(x, ones_bd):
    hi = x.astype(BF16)
    outs = []
    for c in range(x.shape[1] // MXU_TILE):
        sl = slice(c * MXU_TILE, (c + 1) * MXU_TILE)
        outs.append(_dot(hi[:, sl], ones_bd))
    return jnp.concatenate(outs, axis=1) if len(outs) > 1 else outs[0]


def _wkv_prompt_kernel(r_ref, k_ref, v_ref, z_ref, hw_ref, ha_ref, w2_ref, w0_ref, a2_ref, a0_ref,
                       kk_ref, ka_ref, rk_ref, gg_ref, gb_ref, yg_ref, sout_ref, s_ref):
    t_idx = pl.program_id(2)
    c = CHUNK
    hd = HEAD_DIM

    @pl.when(t_idx == 0)
    def _():
        s_ref[...] = jnp.zeros_like(s_ref)

    tb = r_ref.shape[1]
    nh = HEADS_PER_STREAM
    hw = nh * hd
    nc = tb // c

    li = lax.broadcasted_iota(jnp.int32, (MXU_TILE, MXU_TILE), 0) // hd
    lj = lax.broadcasted_iota(jnp.int32, (MXU_TILE, MXU_TILE), 1) // hd
    ones_bd = jnp.where(li == lj, 1.0, 0.0).astype(BF16)
    bi_ = lax.broadcasted_iota(jnp.int32, (tb, tb), 0)
    bj_ = lax.broadcasted_iota(jnp.int32, (tb, tb), 1)
    tri_incl = jnp.where((bj_ <= bi_) & (bj_ // c == bi_ // c), 1.0, 0.0).astype(BF16)
    ti = lax.broadcasted_iota(jnp.int32, (c, c), 0)
    tj = lax.broadcasted_iota(jnp.int32, (c, c), 1)
    ai = lax.broadcasted_iota(jnp.int32, (c, 2 * c), 0)
    aj = lax.broadcasted_iota(jnp.int32, (c, 2 * c), 1)
    upper = aj >= c
    aj_mod = jnp.where(upper, aj - c, aj)
    masks = dict(
        strict=tj < ti,
        eye=jnp.where(ti == tj, 1.0, 0.0).astype(F32),
        top_k=upper & (aj_mod < ai),
        bot=aj_mod <= ai,
        mean_bd=jnp.where(li == lj, 1.0 / hd, 0.0).astype(BF16))

    for st in range(r_ref.shape[2] // hw):
        _wkv_stream(st, hw, nc, r_ref, k_ref, v_ref, z_ref, hw_ref, ha_ref, w2_ref, w0_ref, a2_ref, a0_ref,
                    kk_ref, ka_ref, rk_ref, gg_ref, gb_ref, yg_ref, s_ref, ones_bd, tri_incl, masks)

    @pl.when(t_idx == pl.num_programs(2) - 1)
    def _():
        sout_ref[0] = s_ref[...]


def _wkv_stream(st, hw, nc, r_ref, k_ref, v_ref, z_ref, hw_ref, ha_ref, w2_ref, w0_ref, a2_ref, a0_ref,
                kk_ref, ka_ref, rk_ref, gg_ref, gb_ref, yg_ref, s_ref, ones_bd, tri_incl, masks):
    c = CHUNK
    hd = HEAD_DIM
    nh = hw // hd
    ls = slice(st * hw, (st + 1) * hw)
    h0 = st * nh
    r = r_ref[0, :, ls]
    k = k_ref[0, :, ls]
    v = v_ref[0, :, ls]
    wl = w0_ref[:, ls] + _dot(hw_ref[...], w2_ref[:, ls])
    al = a0_ref[:, ls] + _dot(ha_ref[...], a2_ref[:, ls])
    a = _sigmoid(al)
    lw = (-DECAY_SCALE * LOG2E) * _sigmoid(wl)
    kk = k * kk_ref[:, ls]
    n2 = _seg_sum(kk * kk, ones_bd)
    kk = kk * lax.rsqrt(jnp.maximum(n2, NORM_FLOOR_SQ))
    ka = ka_ref[:, ls]
    k2 = k * (a * ka + (1.0 - ka))
    bb = kk * a

    lw_hi, lw_lo = _split_hi_lo(lw)
    g = _dot(tri_incl, lw_hi) + _dot(tri_incl, lw_lo)
    mid = lambda ci: g[ci * c + c // 2 - 1:ci * c + c // 2, :]
    gm = jnp.concatenate([jnp.broadcast_to(mid(ci), (c, hw)) for ci in range(nc)], axis=0)
    t = g - gm
    e_a = jnp.exp2(t)
    e_prev = jnp.exp2(t - lw)
    e_inv = jnp.exp2(-t)
    e1 = [jnp.exp2(mid(ci)) for ci in range(nc)]
    e2 = [jnp.exp2(g[ci * c + c - 1:ci * c + c, :] - mid(ci)) for ci in range(nc)]

    kkd = (kk * e_prev).astype(BF16)
    rd = (r * e_a).astype(BF16)
    bi = (bb * e_inv).astype(BF16)
    ki = (k2 * e_inv).astype(BF16)
    v_bf = v.astype(BF16)
    zeros_cv = jnp.zeros((c, hd), BF16)

    pairs = [(ci, h) for ci in range(nc) for h in range(nh)]
    rows = lambda ci: slice(ci * c, (ci + 1) * c)
    cols = lambda h: slice(h * hd, (h + 1) * hd)
    xs = {(ci, h): jnp.concatenate([kkd[rows(ci), cols(h)], rd[rows(ci), cols(h)]], axis=0) for ci, h in pairs}
    r1s = {(ci, h): jnp.concatenate([bi[rows(ci), cols(h)], ki[rows(ci), cols(h)]], axis=0) for ci, h in pairs}
    vs = {(ci, h): v_bf[rows(ci), cols(h)] for ci, h in pairs}
    a_mats = {p: _dot_nt(xs[p], r1s[p]) for p in pairs}
    lk_vs = {p: _dot(jnp.where(masks["top_k"], a_mats[p][:c, :], 0.0).astype(BF16),
                     jnp.concatenate([zeros_cv, vs[p]], axis=0)) for p in pairs}
    lps = {p: jnp.where(masks["strict"], a_mats[p][:c, :c], 0.0) for p in pairs}
    ts = {p: masks["eye"] - lps[p] for p in pairs}
    for _ in range(int(math.log2(c)) - 1):
        lpb = {p: lps[p].astype(BF16) for p in pairs}
        lps = {p: _dot(lpb[p], lpb[p]) for p in pairs}
        ts = {p: _dot(ts[p].astype(BF16), (masks["eye"] + lps[p]).astype(BF16)) for p in pairs}
    a_bots = {p: jnp.where(masks["bot"], a_mats[p][c:, :], 0.0).astype(BF16) for p in pairs}
    t_bf = {p: ts[p].astype(BF16) for p in pairs}

    state = [s_ref[h0 + h] for h in range(nh)]
    y_rows = []
    for ci in range(nc):
        hs = range(nh)
        sms = [state[h] * e1[ci][:, cols(h)] for h in hs]
        p_mats = [_dot_nt(xs[ci, h], sms[h].astype(BF16)) for h in hs]
        us = [-_dot(t_bf[ci, h], (p_mats[h][:c, :] + lk_vs[ci, h]).astype(BF16)) for h in hs]
        uvs = [jnp.concatenate([us[h].astype(BF16), vs[ci, h]], axis=0) for h in hs]
        ys = [p_mats[h][c:, :] + _dot(a_bots[ci, h], uvs[h]) for h in hs]
        state = [(sms[h] + _dot_tn(uvs[h], r1s[ci, h])) * e2[ci][:, cols(h)] for h in hs]
        y_rows.append(jnp.concatenate(ys, axis=1))
    for h in range(nh):
        s_ref[h0 + h] = state[h]
    y = jnp.concatenate(y_rows, axis=0) if nc > 1 else y_rows[0]

    mean = _seg_sum(y, masks["mean_bd"])
    yc = y - mean
    var = _seg_sum(yc * yc, masks["mean_bd"])
    yn = yc * lax.rsqrt(var + GN_EPS) * gg_ref[:, ls] + gb_ref[:, ls]
    bonus = _seg_sum(r * k2 * rk_ref[:, ls], ones_bd)
    yg_ref[:, ls] = ((yn + bonus * v) * _silu(z_ref[0, :, ls])).astype(BF16)


def wkv_prompt(rkvz, hw_act, ha_act, w2, w0, a2, a0, k_k, k_a, r_k, gn_g, gn_b, n_batch):
    _, m, e = rkvz.shape
    p = m // n_batch
    tb = WKV_ROWS
    nt = p // tb
    hw = HEADS_PER_STEP * HEAD_DIM
    nh = e // HEAD_DIM
    row = lambda i, g, t: (i * nt + t, g)
    proj = lambda q: pl.BlockSpec((1, tb, hw), lambda i, g, t: (q, i * nt + t, g))
    par = pl.BlockSpec((1, hw), lambda i, g, t: (0, g))
    lr = w2.shape[0]
    hid = pl.BlockSpec((tb, lr), lambda i, g, t: (i * nt + t, 0))
    up = pl.BlockSpec((lr, hw), lambda i, g, t: (0, g))
    return pl.pallas_call(
        _wkv_prompt_kernel,
        grid=(n_batch, e // hw, nt),
        in_specs=[proj(0), proj(1), proj(2), proj(3), hid, hid, up, par, up, par,
                  par, par, par, par, par],
        out_specs=[pl.BlockSpec((tb, hw), row),
                   pl.BlockSpec((1, HEADS_PER_STEP, HEAD_DIM, HEAD_DIM), lambda i, g, t: (i, g, 0, 0))],
        out_shape=[jax.ShapeDtypeStruct((m, e), BF16),
                   jax.ShapeDtypeStruct((n_batch, nh, HEAD_DIM, HEAD_DIM), F32)],
        scratch_shapes=[pltpu.VMEM((HEADS_PER_STEP, HEAD_DIM, HEAD_DIM), F32)],
        compiler_params=_cparams(("arbitrary", "arbitrary", "arbitrary")),
        name="wkv_prompt",
    )(rkvz, rkvz, rkvz, rkvz, hw_act, ha_act, w2, w0.reshape(1, e), a2, a0.reshape(1, e),
      k_k.reshape(1, e), k_a.reshape(1, e), r_k.reshape(1, e), gn_g.reshape(1, e), gn_b.reshape(1, e))


def _wkv_sample_kernel(r_ref, k_ref, v_ref, z_ref, wl_ref, al_ref, kk_ref, ka_ref, rk_ref, gg_ref, gb_ref,
                       s_ref, yg_ref, sout_ref, y_scr):
    hd = HEAD_DIM
    r = r_ref[0]
    k = k_ref[0]
    v = v_ref[0]
    a = _sigmoid(al_ref[0])
    d = jnp.exp2((-DECAY_SCALE * LOG2E) * _sigmoid(wl_ref[0]))
    kk = k * kk_ref[...]
    kk = kk / jnp.maximum(jnp.sqrt(jnp.sum(kk * kk, axis=-1, keepdims=True)), 1e-12)
    k2 = k * (1.0 + (a - 1.0) * ka_ref[...])
    bb = kk * a
    nh = r.shape[0]
    ii = lax.broadcasted_iota(jnp.int32, (hd, hd), 0)
    jj = lax.broadcasted_iota(jnp.int32, (hd, hd), 1)
    eye = ii == jj

    row = lambda x, h: x[h:h + 1, :]
    group = 8
    for h0 in range(0, nh, group):
        hs = range(h0, h0 + group)
        s = {h: s_ref[0, h] for h in hs}
        sa = {h: jnp.sum(s[h] * row(kk, h), axis=-1, keepdims=True) for h in hs}
        v_col = {h: jnp.sum(jnp.where(eye, row(v, h), 0.0), axis=-1, keepdims=True) for h in hs}
        s_new = {h: s[h] * row(d, h) - sa[h] * row(bb, h) + v_col[h] * row(k2, h) for h in hs}
        y_col = {h: jnp.sum(s_new[h] * row(r, h), axis=-1, keepdims=True) for h in hs}
        for h in hs:
            sout_ref[0, h] = s_new[h]
            y_scr[h:h + 1, :] = jnp.sum(jnp.where(eye, y_col[h], 0.0), axis=0, keepdims=True)

    y = y_scr[...]
    mean = jnp.mean(y, axis=-1, keepdims=True)
    yc = y - mean
    var = jnp.mean(yc * yc, axis=-1, keepdims=True)
    yn = yc * lax.rsqrt(var + GN_EPS) * gg_ref[...] + gb_ref[...]
    bonus = jnp.sum(r * k2 * rk_ref[...], axis=-1, keepdims=True)
    yg_ref[0] = ((yn + bonus * v) * _silu(z_ref[0])).astype(BF16)


def wkv_sample(rkvz, wl, al, k_k, k_a, r_k, gn_g, gn_b, state):
    _, m, e = rkvz.shape
    nh = e // HEAD_DIM
    hd = HEAD_DIM
    rkvz4 = rkvz.reshape(4, m, nh, hd)
    proj = lambda q: pl.BlockSpec((None, 1, nh, hd), lambda i: (q, i, 0, 0))
    tok = pl.BlockSpec((1, nh, hd), lambda i: (i, 0, 0))
    par = pl.BlockSpec((nh, hd), lambda i: (0, 0))
    st = pl.BlockSpec((1, nh, hd, hd), lambda i: (i, 0, 0, 0))
    as_heads = lambda x: x.reshape(nh, hd)
    return pl.pallas_call(
        _wkv_sample_kernel,
        grid=(m,),
        in_specs=[proj(0), proj(1), proj(2), proj(3), tok, tok, par, par, par, par, par, st],
        out_specs=[tok, st],
        out_shape=[jax.ShapeDtypeStruct((m, nh, hd), BF16), jax.ShapeDtypeStruct(state.shape, F32)],
        scratch_shapes=[pltpu.VMEM((nh, hd), F32)],
        compiler_params=_cparams(("arbitrary",)),
        name="wkv_sample",
    )(rkvz4, rkvz4, rkvz4, rkvz4, wl.reshape(m, nh, hd), al.reshape(m, nh, hd),
      as_heads(k_k), as_heads(k_a), as_heads(r_k), as_heads(gn_g), as_heads(gn_b), state)


def rope_tables(pos):
    half = ROPE_DIM // 2
    inv_freq = ROPE_THETA ** (-jnp.arange(half, dtype=F32) * 2.0 / ROPE_DIM)
    ang = pos.astype(F32)[:, None] * inv_freq[None, :]
    cos = jnp.cos(ang)
    sin = jnp.sin(ang)
    rows = pos.shape[0]
    ones = jnp.ones((rows, HEAD_DIM - ROPE_DIM), F32)
    zeros_h = jnp.zeros((rows, half), F32)
    zeros_r = jnp.zeros((rows, HEAD_DIM - ROPE_DIM), F32)
    cos_h = jnp.concatenate([cos, cos, ones], axis=1)
    sa_h = jnp.concatenate([-sin, zeros_h, zeros_r], axis=1)
    sb_h = jnp.concatenate([zeros_h, sin, zeros_r], axis=1)
    two = lambda t: jnp.concatenate([t, t], axis=1)
    return two(cos_h), two(sa_h), two(sb_h)


def _attn_prompt_kernel(sink_ref, q_ref, kc_ref, kp_ref, vc_ref, vp_ref, z_ref, o_ref, *, first_block):
    n = pl.program_id(1) + first_block
    hd = HEAD_DIM
    blk = q_ref.shape[0]
    n_kv = kc_ref.shape[1] // hd
    grp = q_ref.shape[1] // (n_kv * hd)
    qi = lax.broadcasted_iota(jnp.int32, (blk, 2 * blk), 0)
    kj = lax.broadcasted_iota(jnp.int32, (blk, 2 * blk), 1) - blk
    kpos = n * blk + kj
    diff = qi - kj
    valid = (kpos >= LEAD) & (diff >= 0) & (diff <= WINDOW)
    k_all = jnp.concatenate([kp_ref[...], kc_ref[...]], axis=0).astype(BF16)
    v_all = jnp.concatenate([vp_ref[...], vc_ref[...]], axis=0).astype(BF16)

    def scores(h):
        k_h = k_all[:, h * hd:(h + 1) * hd]
        return [_dot_nt(q_ref[:, (h * grp + gi) * hd:(h * grp + gi + 1) * hd], k_h) for gi in range(grp)]

    outs = []
    s_next = scores(0)
    for h in range(n_kv):
        s_cur = s_next
        if h + 1 < n_kv:
            s_next = scores(h + 1)
        v_h = v_all[:, h * hd:(h + 1) * hd]
        for g0 in range(0, grp, SOFTMAX_BATCH):
            gs = range(g0, min(g0 + SOFTMAX_BATCH, grp))
            sks = {gi: sink_ref[h * grp + gi] * LOG2E for gi in gs}
            ss = {gi: jnp.where(valid, s_cur[gi], -jnp.inf) for gi in gs}
            ms = {gi: jnp.maximum(jnp.max(ss[gi], axis=-1, keepdims=True), sks[gi]) for gi in gs}
            ps = {gi: jnp.exp2(ss[gi] - ms[gi]) for gi in gs}
            dens = {gi: jnp.sum(ps[gi], axis=-1, keepdims=True) + jnp.exp2(sks[gi] - ms[gi]) for gi in gs}
            outs += [_dot(ps[gi].astype(BF16), v_h) / dens[gi] for gi in gs]
    att = jnp.concatenate(outs, axis=1)
    o_ref[...] = (att * _silu(z_ref[...])).astype(o_ref.dtype)


def attn_prompt(sinks, q, k, v, z, n_batch, first_block):
    m, e = q.shape
    nb = m // (n_batch * BLOCK)
    nb_out = nb - first_block
    kw = k.shape[1]
    cur = lambda i, n: (i * nb + n + first_block, 0)
    prv = lambda i, n: (i * nb + jnp.maximum(n + first_block - 1, 0), 0)
    return pl.pallas_call(
        functools.partial(_attn_prompt_kernel, first_block=first_block),
        grid=(n_batch, nb_out),
        in_specs=[pl.BlockSpec(memory_space=pltpu.SMEM),
                  pl.BlockSpec((BLOCK, e), cur),
                  pl.BlockSpec((BLOCK, kw), cur), pl.BlockSpec((BLOCK, kw), prv),
                  pl.BlockSpec((BLOCK, kw), cur), pl.BlockSpec((BLOCK, kw), prv),
                  pl.BlockSpec((BLOCK, e), cur)],
        out_specs=pl.BlockSpec((BLOCK, e), lambda i, n: (i * nb_out + n, 0)),
        out_shape=jax.ShapeDtypeStruct((n_batch * nb_out * BLOCK, e), BF16),
        compiler_params=_cparams(("arbitrary", "arbitrary")),
        name="attn_prompt",
    )(sinks, q, k, k, v, v, z)


def _attn_sample_kernel(sink_ref, q_ref, kc_ref, vc_ref, kn_ref, vn_ref, z_ref, o_ref, ko_ref, vo_ref):
    hd = HEAD_DIM
    win = kc_ref.shape[1]
    n_kv = kc_ref.shape[2] // hd
    nq = q_ref.shape[1]
    grp = nq // n_kv
    pad = 8
    kc = kc_ref[0]
    vc = vc_ref[0]
    kn = kn_ref[0]
    vn = vn_ref[0]
    first = lax.broadcasted_iota(jnp.int32, (pad, kc.shape[1]), 0) == 0
    k_all = jnp.concatenate([kc, jnp.where(first, kn, 0.0)], axis=0).astype(BF16)
    v_all = jnp.concatenate([vc, jnp.where(first, vn, 0.0)], axis=0).astype(BF16)
    col = lax.broadcasted_iota(jnp.int32, (grp, win + pad), 1)
    valid = (col <= win) & (win - col <= WINDOW)
    q = q_ref[0].astype(BF16)
    row_i = lax.broadcasted_iota(jnp.int32, (grp, 1), 0)
    hs = range(n_kv)
    sks = []
    for h in hs:
        sk = jnp.zeros((grp, 1), F32)
        for gi in range(grp):
            sk = jnp.where(row_i == gi, sink_ref[h * grp + gi] * LOG2E, sk)
        sks.append(sk)
    ss = [jnp.where(valid, _dot_nt(q[h * grp:(h + 1) * grp, :], k_all[:, h * hd:(h + 1) * hd]), -jnp.inf)
          for h in hs]
    ms = [jnp.maximum(jnp.max(ss[h], axis=-1, keepdims=True), sks[h]) for h in hs]
    ps = [jnp.exp2(ss[h] - ms[h]) for h in hs]
    dens = [jnp.sum(ps[h], axis=-1, keepdims=True) + jnp.exp2(sks[h] - ms[h]) for h in hs]
    outs = [_dot(ps[h].astype(BF16), v_all[:, h * hd:(h + 1) * hd]) / dens[h] for h in hs]
    att = jnp.concatenate(outs, axis=0)
    o_ref[0] = (att * _silu(z_ref[0])).astype(o_ref.dtype)
    last = lax.broadcasted_iota(jnp.int32, kc.shape, 0) == win - 1
    ko_ref[0] = jnp.where(last, kn, pltpu.roll(kc, win - 1, axis=0))
    vo_ref[0] = jnp.where(last, vn, pltpu.roll(vc, win - 1, axis=0))


def attn_sample(sinks, q, cache_k, cache_v, k_new, v_new, z):
    m, win, kw = cache_k.shape
    nq = q.shape[1]
    hd = HEAD_DIM
    tok = pl.BlockSpec((1, nq, hd), lambda i: (i, 0, 0))
    cache = pl.BlockSpec((1, win, kw), lambda i: (i, 0, 0))
    new = pl.BlockSpec((1, 1, kw), lambda i: (i, 0, 0))
    return pl.pallas_call(
        _attn_sample_kernel,
        grid=(m,),
        in_specs=[pl.BlockSpec(memory_space=pltpu.SMEM), tok, cache, cache, new, new, tok],
        out_specs=[tok, cache, cache],
        out_shape=[jax.ShapeDtypeStruct((m, nq, hd), BF16),
                   jax.ShapeDtypeStruct(cache_k.shape, F32), jax.ShapeDtypeStruct(cache_v.shape, F32)],
        compiler_params=_cparams(("arbitrary",)),
        name="attn_sample",
    )(sinks, q, cache_k, cache_v, k_new, v_new, z)


def _pad_lora(w_down, w_up):
    r = w_down.shape[1]
    return (jnp.pad(w_down, ((0, 0), (0, LORA_PAD - r))).astype(BF16),
            jnp.pad(w_up, ((0, LORA_PAD - r), (0, 0))).astype(BF16))


def kernel(x_prompt, x_sample, state_wkv, state_shift, cache_k, cache_v, meta_tokens, a_norm, a_mu, a_w_rkvz,
           a_w0, a_w1, a_w2, a_a0, a_a1, a_a2, a_k_k, a_k_a, a_r_k, a_gn_g, a_gn_b, a_w_out, kv_norm, w_kv,
           b_norm, b_w_qz, b_sinks, b_w_o, final_norm):
    nb, seq, d = x_prompt.shape
    db, dseq, _ = x_sample.shape
    assert dseq == 1 and a_norm.shape[0] == 1 and b_norm.shape[0] == 1
    e = a_w_rkvz.shape[3]
    win = cache_k.shape[1]
    p_len = LEAD + N_META + seq
    assert p_len % BLOCK == 0 and (LEAD + N_META) == BLOCK
    kvw = N_KV_HEADS * HEAD_DIM

    w_rkvz = a_w_rkvz[0].astype(BF16)
    w1, w2 = _pad_lora(a_w1[0], a_w2[0])
    a1, a2 = _pad_lora(a_a1[0], a_a2[0])
    w_out = a_w_out[0].astype(BF16)
    w_kv_bf = w_kv.astype(BF16)
    w_qz = b_w_qz.astype(BF16)
    w_o = b_w_o[0].astype(BF16)
    mu = a_mu[0]
    sinks = b_sinks[0]
    gains_b = jnp.stack([kv_norm, b_norm[0]])

    tm = p_len // 8

    head = jnp.concatenate([jnp.zeros((LEAD, d), F32), meta_tokens], axis=0)
    xm, hw_p, ha_p, x_last = norm_shift_prompt(x_prompt, head, a_norm[0], mu, w1, a1)
    p_state_shift = x_last.reshape(1, nb, d)
    rkvz = matmul_groups(xm, w_rkvz, tm, F32)
    yg, p_state = wkv_prompt(rkvz, hw_p, ha_p, w2, a_w0[0], a2, a_a0[0], a_k_k[0], a_k_a[0],
                             a_r_k[0].reshape(-1), a_gn_g[0], a_gn_b[0], nb)
    hp, hn_kv, hn_b = matmul_residual_norm_blocks(yg, w_out, x_prompt, gains_b, nb, BF16, True, head=head)

    pos_p = jnp.maximum(jnp.arange(p_len, dtype=jnp.int32) - LEAD, 0)
    tabs_p = tuple(jnp.tile(t, (nb, 1)) for t in rope_tables(pos_p))
    k_p, v_p = matmul_rope(hn_kv, w_kv_bf, tabs_p, tm, kvw, (F32, F32))
    q_p, = matmul_rope(hn_b, w_qz, tabs_p, tm, e, (BF16,), scale=Q_SCALE, n=e)
    z_p = matmul_groups(hn_b[None], w_qz, tm, F32, n=e, col=1)[0]
    skip = (LEAD + N_META) // BLOCK
    att = attn_prompt(sinks, q_p, k_p, v_p, z_p, nb, skip)
    y_prompt, = matmul_residual_norm_blocks(att, w_o, hp, final_norm[None], nb, F32, False, first_block=skip)
    y_prompt = y_prompt.reshape(nb, seq, d)
    tail = lambda t: t.reshape(nb, p_len, kvw)[:, -win:].reshape(nb, win, N_KV_HEADS, HEAD_DIM)
    p_cache_k = tail(k_p)
    p_cache_v = tail(v_p)

    hs = x_sample.reshape(db, d)
    xm_s, hw_s, ha_s, xn_s = norm_shift_sample(hs, state_shift[0], a_norm[0], mu, w1, a1)
    rkvz_s = matmul_groups(xm_s, w_rkvz, db, F32)
    wl_s, al_s = lora_up(hw_s, ha_s, w2, a_w0[0], a2, a_a0[0])
    yg_s, s_state = wkv_sample(rkvz_s, wl_s, al_s, a_k_k[0], a_k_a[0], a_r_k[0].reshape(-1), a_gn_g[0],
                               a_gn_b[0], state_wkv[0])
    hs, hn_kv_s, hn_b_s = matmul_residual_norm(yg_s.reshape(db, e), w_out, hs, gains_b, db, BF16)
    tabs_s = rope_tables(jnp.full((db,), PAST_LEN, jnp.int32))
    k_s, v_s = matmul_rope(hn_kv_s, w_kv_bf, tabs_s, db, kvw, (F32, F32))
    q_s, = matmul_rope(hn_b_s, w_qz, tabs_s, db, e, (F32,), scale=Q_SCALE, n=e)
    z_s = matmul_groups(hn_b_s[None], w_qz, db, F32, n=e, col=1)[0]
    nq = e // HEAD_DIM
    att_s, s_cache_k, s_cache_v = attn_sample(
        sinks, q_s.reshape(db, nq, HEAD_DIM), cache_k.reshape(db, win, kvw), cache_v.reshape(db, win, kvw),
        k_s.reshape(db, 1, kvw), v_s.reshape(db, 1, kvw), z_s.reshape(db, nq, HEAD_DIM))
    y_s, = matmul_residual_norm(att_s.reshape(db, e), w_o, hs, final_norm[None], db, F32, emit_h=False)
    y_sample = y_s.reshape(db, 1, d)

    return (y_prompt, y_sample, p_state[None], p_state_shift,
            p_cache_k, p_cache_v,
            s_state[None], xn_s[None],
            s_cache_k.reshape(cache_k.shape), s_cache_v.reshape(cache_v.shape))
```

```python
import functools
import math

import jax
import jax.numpy as jnp
from jax import lax
from jax.experimental import pallas as pl
from jax.experimental.pallas import tpu as pltpu

F32 = jnp.float32
BF16 = jnp.bfloat16

HEAD_DIM = 64
N_KV_HEADS = 8
WINDOW = 128
BLOCK = 128
ROPE_DIM = HEAD_DIM // 4
ROPE_THETA = 500000.0
N_META = 16
PAST_LEN = 16384
RMS_EPS = 1e-6
GN_EPS = 64e-5
NORM_FLOOR_SQ = 1e-24
LEAD = (-N_META) % BLOCK
CHUNK = 64
WKV_ROWS = 128
SEQS_PER_STEP = 2
HEADS_PER_STREAM = 32
HEADS_PER_STEP = 32
LORA_PAD = 128
MXU_TILE = 256
ROPE_SLAB = 512
VMEM_LIMIT = 48 * 1024 * 1024
LOG2E = 1.0 / math.log(2.0)
DECAY_SCALE = math.exp(-0.5)
SOFTMAX_BATCH = 4
Q_SCALE = HEAD_DIM ** -0.5 * LOG2E


def _cparams(sem):
    return pltpu.CompilerParams(dimension_semantics=sem, vmem_limit_bytes=VMEM_LIMIT)


def _sigmoid(x):
    return 1.0 / (1.0 + jnp.exp2(x * (-LOG2E)))


def _silu(x):
    return x * _sigmoid(x)


def _dot(a, b):
    return jnp.dot(a, b, preferred_element_type=F32)


def _dot_nt(a, b):
    return lax.dot_general(a, b, (((1,), (1,)), ((), ())), preferred_element_type=F32)


def _dot_tn(a, b):
    return lax.dot_general(a, b, (((0,), (0,)), ((), ())), preferred_element_type=F32)


def _split_hi_lo(x):
    hi = x.astype(BF16)
    lo = (x - hi.astype(F32)).astype(BF16)
    return hi, lo


def _mixes(xn, prev, mu_ref, w1_ref, a1_ref, xm_ref, hw_ref, ha_ref):
    xx = prev - xn
    n_proj = xm_ref.shape[0]
    for p in range(n_proj):
        xm_ref[p] = (xn + xx * mu_ref[p:p + 1, :]).astype(xm_ref.dtype)
    xw = (xn + xx * mu_ref[n_proj:n_proj + 1, :]).astype(BF16)
    xa = (xn + xx * mu_ref[n_proj + 1:n_proj + 2, :]).astype(BF16)
    hw_ref[...] = jnp.tanh(_dot(xw, w1_ref[...])).astype(hw_ref.dtype)
    ha_ref[...] = _dot(xa, a1_ref[...]).astype(ha_ref.dtype)


def _padded_rows(x_ref, head_ref, first_block):
    return jnp.where(pl.program_id(1) + first_block == 0, head_ref[...], x_ref[0])


def _norm_shift_kernel(x_ref, head_ref, g_ref, mu_ref, w1_ref, a1_ref, xm_ref, hw_ref, ha_ref, last_ref, carry_ref):
    @pl.when(pl.program_id(1) == 0)
    def _():
        carry_ref[...] = jnp.zeros_like(carry_ref)

    x = _padded_rows(x_ref, head_ref, 0)
    tm = x.shape[0]
    xn = x * lax.rsqrt(jnp.mean(x * x, axis=-1, keepdims=True) + RMS_EPS) * g_ref[...]
    rolled = pltpu.roll(xn, 1, axis=0)
    row = lax.broadcasted_iota(jnp.int32, xn.shape, 0)
    prev = jnp.where(row == 0, carry_ref[0:1, :], rolled)
    _mixes(xn, prev, mu_ref, w1_ref, a1_ref, xm_ref, hw_ref, ha_ref)
    carry_ref[0:1, :] = xn[tm - 1:tm, :]
    last_ref[0] = xn[tm - 1:tm, :]


def norm_shift_prompt(x, head, g, mu, w1, a1):
    b, seq, d = x.shape
    tm = BLOCK
    p = tm + seq
    n_mix = mu.shape[0]
    n_proj = n_mix - 2
    lr = w1.shape[1]
    nt = p // tm
    const = lambda shape: pl.BlockSpec(shape, lambda i, t: (0,) * len(shape))
    hid = pl.BlockSpec((tm, lr), lambda i, t: (i * nt + t, 0))
    return pl.pallas_call(
        _norm_shift_kernel,
        grid=(b, nt),
        in_specs=[pl.BlockSpec((1, tm, d), lambda i, t: (i, jnp.maximum(t - 1, 0), 0)),
                  const((tm, d)), const((1, d)), const((n_mix, d)), const((d, lr)), const((d, lr))],
        out_specs=[pl.BlockSpec((n_proj, tm, d), lambda i, t: (0, i * nt + t, 0)), hid, hid,
                   pl.BlockSpec((1, 1, d), lambda i, t: (i, 0, 0))],
        out_shape=[jax.ShapeDtypeStruct((n_proj, b * p, d), BF16),
                   jax.ShapeDtypeStruct((b * p, lr), BF16), jax.ShapeDtypeStruct((b * p, lr), BF16),
                   jax.ShapeDtypeStruct((b, 1, d), F32)],
        scratch_shapes=[pltpu.VMEM((8, d), F32)],
        compiler_params=_cparams(("arbitrary", "arbitrary")),
        name="norm_shift_prompt",
    )(x, head, g.reshape(1, d), mu, w1, a1)


def _norm_shift_sample_kernel(x_ref, prev_ref, g_ref, mu_ref, w1_ref, a1_ref, xm_ref, hw_ref, ha_ref, xn_ref):
    x = x_ref[...]
    xn = x * lax.rsqrt(jnp.mean(x * x, axis=-1, keepdims=True) + RMS_EPS) * g_ref[...]
    xn_ref[...] = xn
    _mixes(xn, prev_ref[...], mu_ref, w1_ref, a1_ref, xm_ref, hw_ref, ha_ref)


def norm_shift_sample(x, prev, g, mu, w1, a1):
    m, d = x.shape
    lr = w1.shape[1]
    return pl.pallas_call(
        _norm_shift_sample_kernel,
        out_shape=[jax.ShapeDtypeStruct((mu.shape[0] - 2, m, d), BF16),
                   jax.ShapeDtypeStruct((m, lr), BF16), jax.ShapeDtypeStruct((m, lr), BF16),
                   jax.ShapeDtypeStruct((m, d), F32)],
        name="norm_shift_sample",
    )(x, prev, g.reshape(1, d), mu, w1, a1)


def _rope(y, cos, sin_a, sin_b):
    half = ROPE_DIM // 2
    step = ROPE_SLAB
    rep = step // cos.shape[1]
    tile = lambda t: jnp.concatenate([t] * rep, axis=1)
    cos_t, sa_t, sb_t = tile(cos), tile(sin_a), tile(sin_b)
    outs = []
    for j in range(y.shape[1] // step):
        ys = y[:, j * step:(j + 1) * step]
        outs.append(ys * cos_t + pltpu.roll(ys, step - half, axis=1) * sa_t + pltpu.roll(ys, half, axis=1) * sb_t)
    return jnp.concatenate(outs, axis=1) if len(outs) > 1 else outs[0]


def _mm_group_kernel(x_ref, w_ref, o_ref):
    o_ref[0] = _dot(x_ref[0], w_ref[0]).astype(o_ref.dtype)


def matmul_groups(x, w, tm, out_dtype, n=None, col=0):
    g = w.shape[0]
    n = w.shape[2] if n is None else n
    _, m, kdim = x.shape
    return pl.pallas_call(
        _mm_group_kernel,
        grid=(g, m // tm),
        in_specs=[pl.BlockSpec((1, tm, kdim), lambda q, i: (q, i, 0)),
                  pl.BlockSpec((1, kdim, n), lambda q, i: (q, 0, col))],
        out_specs=pl.BlockSpec((1, tm, n), lambda q, i: (q, i, 0)),
        out_shape=jax.ShapeDtypeStruct((g, m, n), out_dtype),
        compiler_params=_cparams(("arbitrary", "arbitrary")),
        name="matmul_groups",
    )(x, w)


def _mm_rope_kernel(x_ref, w_ref, cos_ref, sa_ref, sb_ref, *o_refs, n_rope, scale):
    y = _dot(x_ref[...], w_ref[...])
    rot = _rope(y[:, :n_rope], cos_ref[...], sa_ref[...], sb_ref[...])
    if scale != 1.0:
        rot = rot * scale
    o_refs[0][...] = rot.astype(o_refs[0].dtype)
    if len(o_refs) > 1:
        o_refs[1][...] = y[:, n_rope:].astype(o_refs[1].dtype)


def matmul_rope(x, w, tables, tm, n_rope, out_dtypes, scale=1.0, n=None):
    m, kdim = x.shape
    n = w.shape[-1] if n is None else n
    w_block = (kdim, n) if w.ndim == 2 else (None, kdim, n)
    lanes = tables[0].shape[1]
    widths = [n_rope] + ([n - n_rope] if n > n_rope else [])
    tab = pl.BlockSpec((tm, lanes), lambda i: (i, 0))
    outs = pl.pallas_call(
        functools.partial(_mm_rope_kernel, n_rope=n_rope, scale=scale),
        grid=(m // tm,),
        in_specs=[pl.BlockSpec((tm, kdim), lambda i: (i, 0)),
                  pl.BlockSpec(w_block, lambda i: (0,) * w.ndim, pipeline_mode=pl.Buffered(1)),
                  tab, tab, tab],
        out_specs=[pl.BlockSpec((tm, wd), lambda i: (i, 0)) for wd in widths],
        out_shape=[jax.ShapeDtypeStruct((m, wd), dt) for wd, dt in zip(widths, out_dtypes)],
        compiler_params=_cparams(("arbitrary",)),
        name="matmul_rope",
    )(x, w, *tables)
    return outs


def _mm_res_norm_kernel(x_ref, w_ref, res_ref, g_ref, *out_refs, emit_h):
    h = res_ref[...] + _dot(x_ref[...], w_ref[...])
    hn_refs = out_refs
    if emit_h:
        out_refs[0][...] = h
        hn_refs = out_refs[1:]
    inv = lax.rsqrt(jnp.mean(h * h, axis=-1, keepdims=True) + RMS_EPS)
    for j, hn_ref in enumerate(hn_refs):
        hn_ref[...] = (h * inv * g_ref[j:j + 1, :]).astype(hn_ref.dtype)


def matmul_residual_norm(x, w, res, gains, tm, norm_dtype, emit_h=True):
    m, kdim = x.shape
    n = w.shape[1]
    ng = gains.shape[0]
    row = lambda width: pl.BlockSpec((tm, width), lambda i: (i, 0))
    return pl.pallas_call(
        functools.partial(_mm_res_norm_kernel, emit_h=emit_h),
        grid=(m // tm,),
        in_specs=[row(kdim),
                  pl.BlockSpec((kdim, n), lambda i: (0, 0), pipeline_mode=pl.Buffered(1)),
                  row(n),
                  pl.BlockSpec((ng, n), lambda i: (0, 0))],
        out_specs=[row(n)] * (int(emit_h) + ng),
        out_shape=[jax.ShapeDtypeStruct((m, n), F32)] * int(emit_h) + [jax.ShapeDtypeStruct((m, n), norm_dtype)] * ng,
        compiler_params=_cparams(("arbitrary",)),
        name="matmul_residual_norm",
    )(x, w, res, gains)


def _mm_res_norm_blocks_kernel(x_ref, w_ref, res_ref, *rest, emit_h, head, first_block):
    if head:
        head_ref, g_ref, *out_refs = rest
        res = jnp.where(pl.program_id(1) + first_block == 0, head_ref[...][None], res_ref[...])
    else:
        g_ref, *out_refs = rest
        res = res_ref[...]
    seqs, blk, n = res.shape
    h = res.reshape(seqs * blk, n) + _dot(x_ref[...].reshape(seqs * blk, x_ref.shape[2]), w_ref[...])
    hn_refs = out_refs
    if emit_h:
        out_refs[0][...] = h.reshape(seqs, blk, n)
        hn_refs = out_refs[1:]
    inv = lax.rsqrt(jnp.mean(h * h, axis=-1, keepdims=True) + RMS_EPS)
    for j, hn_ref in enumerate(hn_refs):
        hn_ref[...] = (h * inv * g_ref[j:j + 1, :]).astype(hn_ref.dtype).reshape(seqs, blk, n)


def matmul_residual_norm_blocks(x, w, res, gains, n_batch, norm_dtype, emit_h, head=None, first_block=0):
    kdim = x.shape[1]
    n = w.shape[1]
    ng = gains.shape[0]
    nb_out = x.shape[0] // (n_batch * BLOCK)
    nb = nb_out + first_block
    sp = SEQS_PER_STEP
    blocks = lambda width, shift: pl.BlockSpec((sp, BLOCK, width), lambda i, t: (i, t + shift, 0))
    const = lambda shape: pl.BlockSpec(shape, lambda i, t: (0,) * len(shape))
    if head is not None:
        res_specs = [pl.BlockSpec((sp, BLOCK, n), lambda i, t: (i, jnp.maximum(t + first_block - 1, 0), 0)),
                     const((BLOCK, n))]
        res_args = [res, head]
    else:
        res_specs, res_args = [blocks(n, first_block)], [res.reshape(n_batch, nb * BLOCK, n)]
    outs = pl.pallas_call(
        functools.partial(_mm_res_norm_blocks_kernel, emit_h=emit_h, head=head is not None, first_block=first_block),
        grid=(n_batch // sp, nb_out),
        in_specs=[blocks(kdim, 0), pl.BlockSpec((kdim, n), lambda i, t: (0, 0), pipeline_mode=pl.Buffered(1))]
        + res_specs + [const((ng, n))],
        out_specs=[blocks(n, first_block)] * int(emit_h) + [blocks(n, 0)] * ng,
        out_shape=[jax.ShapeDtypeStruct((n_batch, nb * BLOCK, n), F32)] * int(emit_h)
        + [jax.ShapeDtypeStruct((n_batch, nb_out * BLOCK, n), norm_dtype)] * ng,
        compiler_params=_cparams(("arbitrary", "arbitrary")),
        name="matmul_residual_norm_blocks",
    )(x.reshape(n_batch, nb_out * BLOCK, kdim), w, *res_args, gains)
    return [o.reshape(-1, n) for o in outs]


def _lora_up_kernel(hw_ref, ha_ref, w2_ref, w0_ref, a2_ref, a0_ref, wl_ref, al_ref):
    wl_ref[...] = w0_ref[...] + _dot(hw_ref[...], w2_ref[...])
    al_ref[...] = a0_ref[...] + _dot(ha_ref[...], a2_ref[...])


def lora_up(hw, ha, w2, w0, a2, a0):
    m = hw.shape[0]
    e = w2.shape[1]
    return pl.pallas_call(
        _lora_up_kernel,
        out_shape=[jax.ShapeDtypeStruct((m, e), F32), jax.ShapeDtypeStruct((m, e), F32)],
        name="lora_up",
    )(hw, ha, w2, w0.reshape(1, e), a2, a0.reshape(1, e))


def _seg_sum(x, ones_bd):
    hi = x.astype(BF16)
    outs = []
    for c in range(x.shape[1] // MXU_TILE):
        sl = slice(c * MXU_TILE, (c + 1) * MXU_TILE)
        outs.append(_dot(hi[:, sl], ones_bd))
    return jnp.concatenate(outs, axis=1) if len(outs) > 1 else outs[0]


def _wkv_prompt_kernel(r_ref, k_ref, v_ref, z_ref, hw_ref, ha_ref, w2_ref, w0_ref, a2_ref, a0_ref,
                       kk_ref, ka_ref, rk_ref, gg_ref, gb_ref, yg_ref, sout_ref, s_ref):
    t_idx = pl.program_id(2)
    c = CHUNK
    hd = HEAD_DIM

    @pl.when(t_idx == 0)
    def _():
        s_ref[...] = jnp.zeros_like(s_ref)

    tb = r_ref.shape[1]
    nh = HEADS_PER_STREAM
    hw = nh * hd
    nc = tb // c

    li = lax.broadcasted_iota(jnp.int32, (MXU_TILE, MXU_TILE), 0) // hd
    lj = lax.broadcasted_iota(jnp.int32, (MXU_TILE, MXU_TILE), 1) // hd
    ones_bd = jnp.where(li == lj, 1.0, 0.0).astype(BF16)
    bi_ = lax.broadcasted_iota(jnp.int32, (tb, tb), 0)
    bj_ = lax.broadcasted_iota(jnp.int32, (tb, tb), 1)
    tri_incl = jnp.where((bj_ <= bi_) & (bj_ // c == bi_ // c), 1.0, 0.0).astype(BF16)
    ti = lax.broadcasted_iota(jnp.int32, (c, c), 0)
    tj = lax.broadcasted_iota(jnp.int32, (c, c), 1)
    ai = lax.broadcasted_iota(jnp.int32, (c, 2 * c), 0)
    aj = lax.broadcasted_iota(jnp.int32, (c, 2 * c), 1)
    upper = aj >= c
    aj_mod = jnp.where(upper, aj - c, aj)
    masks = dict(
        strict=tj < ti,
        eye=jnp.where(ti == tj, 1.0, 0.0).astype(F32),
        top_k=upper & (aj_mod < ai),
        bot=aj_mod <= ai,
        mean_bd=jnp.where(li == lj, 1.0 / hd, 0.0).astype(BF16))

    for st in range(r_ref.shape[2] // hw):
        _wkv_stream(st, hw, nc, r_ref, k_ref, v_ref, z_ref, hw_ref, ha_ref, w2_ref, w0_ref, a2_ref, a0_ref,
                    kk_ref, ka_ref, rk_ref, gg_ref, gb_ref, yg_ref, s_ref, ones_bd, tri_incl, masks)

    @pl.when(t_idx == pl.num_programs(2) - 1)
    def _():
        sout_ref[0] = s_ref[...]


def _wkv_stream(st, hw, nc, r_ref, k_ref, v_ref, z_ref, hw_ref, ha_ref, w2_ref, w0_ref, a2_ref, a0_ref,
                kk_ref, ka_ref, rk_ref, gg_ref, gb_ref, yg_ref, s_ref, ones_bd, tri_incl, masks):
    c = CHUNK
    hd = HEAD_DIM
    nh = hw // hd
    ls = slice(st * hw, (st + 1) * hw)
    h0 = st * nh
    r = r_ref[0, :, ls]
    k = k_ref[0, :, ls]
    v = v_ref[0, :, ls]
    wl = w0_ref[:, ls] + _dot(hw_ref[...], w2_ref[:, ls])
    al = a0_ref[:, ls] + _dot(ha_ref[...], a2_ref[:, ls])
    a = _sigmoid(al)
    lw = (-DECAY_SCALE * LOG2E) * _sigmoid(wl)
    kk = k * kk_ref[:, ls]
    n2 = _seg_sum(kk * kk, ones_bd)
    kk = kk * lax.rsqrt(jnp.maximum(n2, NORM_FLOOR_SQ))
    ka = ka_ref[:, ls]
    k2 = k * (a * ka + (1.0 - ka))
    bb = kk * a

    lw_hi, lw_lo = _split_hi_lo(lw)
    g = _dot(tri_incl, lw_hi) + _dot(tri_incl, lw_lo)
    mid = lambda ci: g[ci * c + c // 2 - 1:ci * c + c // 2, :]
    gm = jnp.concatenate([jnp.broadcast_to(mid(ci), (c, hw)) for ci in range(nc)], axis=0)
    t = g - gm
    e_a = jnp.exp2(t)
    e_prev = jnp.exp2(t - lw)
    e_inv = jnp.exp2(-t)
    e1 = [jnp.exp2(mid(ci)) for ci in range(nc)]
    e2 = [jnp.exp2(g[ci * c + c - 1:ci * c + c, :] - mid(ci)) for ci in range(nc)]

    kkd = (kk * e_prev).astype(BF16)
    rd = (r * e_a).astype(BF16)
    bi = (bb * e_inv).astype(BF16)
    ki = (k2 * e_inv).astype(BF16)
    v_bf = v.astype(BF16)
    zeros_cv = jnp.zeros((c, hd), BF16)

    pairs = [(ci, h) for ci in range(nc) for h in range(nh)]
    rows = lambda ci: slice(ci * c, (ci + 1) * c)
    cols = lambda h: slice(h * hd, (h + 1) * hd)
    xs = {(ci, h): jnp.concatenate([kkd[rows(ci), cols(h)], rd[rows(ci), cols(h)]], axis=0) for ci, h in pairs}
    r1s = {(ci, h): jnp.concatenate([bi[rows(ci), cols(h)], ki[rows(ci), cols(h)]], axis=0) for ci, h in pairs}
    vs = {(ci, h): v_bf[rows(ci), cols(h)] for ci, h in pairs}
    a_mats = {p: _dot_nt(xs[p], r1s[p]) for p in pairs}
    lk_vs = {p: _dot(jnp.where(masks["top_k"], a_mats[p][:c, :], 0.0).astype(BF16),
                     jnp.concatenate([zeros_cv, vs[p]], axis=0)) for p in pairs}
    lps = {p: jnp.where(masks["strict"], a_mats[p][:c, :c], 0.0) for p in pairs}
    ts = {p: masks["eye"] - lps[p] for p in pairs}
    for _ in range(int(math.log2(c)) - 1):
        lpb = {p: lps[p].astype(BF16) for p in pairs}
        lps = {p: _dot(lpb[p], lpb[p]) for p in pairs}
        ts = {p: _dot(ts[p].astype(BF16), (masks["eye"] + lps[p]).astype(BF16)) for p in pairs}
    a_bots = {p: jnp.where(masks["bot"], a_mats[p][c:, :], 0.0).astype(BF16) for p in pairs}
    t_bf = {p: ts[p].astype(BF16) for p in pairs}

    state = [s_ref[h0 + h] for h in range(nh)]
    y_rows = []
    for ci in range(nc):
        hs = range(nh)
        sms = [state[h] * e1[ci][:, cols(h)] for h in hs]
        p_mats = [_dot_nt(xs[ci, h], sms[h].astype(BF16)) for h in hs]
        us = [-_dot(t_bf[ci, h], (p_mats[h][:c, :] + lk_vs[ci, h]).astype(BF16)) for h in hs]
        uvs = [jnp.concatenate([us[h].astype(BF16), vs[ci, h]], axis=0) for h in hs]
        ys = [p_mats[h][c:, :] + _dot(a_bots[ci, h], uvs[h]) for h in hs]
        state = [(sms[h] + _dot_tn(uvs[h], r1s[ci, h])) * e2[ci][:, cols(h)] for h in hs]
        y_rows.append(jnp.concatenate(ys, axis=1))
    for h in range(nh):
        s_ref[h0 + h] = state[h]
    y = jnp.concatenate(y_rows, axis=0) if nc > 1 else y_rows[0]

    mean = _seg_sum(y, masks["mean_bd"])
    yc = y - mean
    var = _seg_sum(yc * yc, masks["mean_bd"])
    yn = yc * lax.rsqrt(var + GN_EPS) * gg_ref[:, ls] + gb_ref[:, ls]
    bonus = _seg_sum(r * k2 * rk_ref[:, ls], ones_bd)
    yg_ref[:, ls] = ((yn + bonus * v) * _silu(z_ref[0, :, ls])).astype(BF16)


def wkv_prompt(rkvz, hw_act, ha_act, w2, w0, a2, a0, k_k, k_a, r_k, gn_g, gn_b, n_batch):
    _, m, e = rkvz.shape
    p = m // n_batch
    tb = WKV_ROWS
    nt = p // tb
    hw = HEADS_PER_STEP * HEAD_DIM
    nh = e // HEAD_DIM
    row = lambda i, g, t: (i * nt + t, g)
    proj = lambda q: pl.BlockSpec((1, tb, hw), lambda i, g, t: (q, i * nt + t, g))
    par = pl.BlockSpec((1, hw), lambda i, g, t: (0, g))
    lr = w2.shape[0]
    hid = pl.BlockSpec((tb, lr), lambda i, g, t: (i * nt + t, 0))
    up = pl.BlockSpec((lr, hw), lambda i, g, t: (0, g))
    return pl.pallas_call(
        _wkv_prompt_kernel,
        grid=(n_batch, e // hw, nt),
        in_specs=[proj(0), proj(1), proj(2), proj(3), hid, hid, up, par, up, par,
                  par, par, par, par, par],
        out_specs=[pl.BlockSpec((tb, hw), row),
                   pl.BlockSpec((1, HEADS_PER_STEP, HEAD_DIM, HEAD_DIM), lambda i, g, t: (i, g, 0, 0))],
        out_shape=[jax.ShapeDtypeStruct((m, e), BF16),
                   jax.ShapeDtypeStruct((n_batch, nh, HEAD_DIM, HEAD_DIM), F32)],
        scratch_shapes=[pltpu.VMEM((HEADS_PER_STEP, HEAD_DIM, HEAD_DIM), F32)],
        compiler_params=_cparams(("arbitrary", "arbitrary", "arbitrary")),
        name="wkv_prompt",
    )(rkvz, rkvz, rkvz, rkvz, hw_act, ha_act, w2, w0.reshape(1, e), a2, a0.reshape(1, e),
      k_k.reshape(1, e), k_a.reshape(1, e), r_k.reshape(1, e), gn_g.reshape(1, e), gn_b.reshape(1, e))


def _wkv_sample_kernel(r_ref, k_ref, v_ref, z_ref, wl_ref, al_ref, kk_ref, ka_ref, rk_ref, gg_ref, gb_ref,
                       s_ref, yg_ref, sout_ref, y_scr):
    hd = HEAD_DIM
    r = r_ref[0]
    k = k_ref[0]
    v = v_ref[0]
    a = _sigmoid(al_ref[0])
    d = jnp.exp2((-DECAY_SCALE * LOG2E) * _sigmoid(wl_ref[0]))
    kk = k * kk_ref[...]
    kk = kk / jnp.maximum(jnp.sqrt(jnp.sum(kk * kk, axis=-1, keepdims=True)), 1e-12)
    k2 = k * (1.0 + (a - 1.0) * ka_ref[...])
    bb = kk * a
    nh = r.shape[0]
    ii = lax.broadcasted_iota(jnp.int32, (hd, hd), 0)
    jj = lax.broadcasted_iota(jnp.int32, (hd, hd), 1)
    eye = ii == jj

    row = lambda x, h: x[h:h + 1, :]
    group = 8
    for h0 in range(0, nh, group):
        hs = range(h0, h0 + group)
        s = {h: s_ref[0, h] for h in hs}
        sa = {h: jnp.sum(s[h] * row(kk, h), axis=-1, keepdims=True) for h in hs}
        v_col = {h: jnp.sum(jnp.where(eye, row(v, h), 0.0), axis=-1, keepdims=True) for h in hs}
        s_new = {h: s[h] * row(d, h) - sa[h] * row(bb, h) + v_col[h] * row(k2, h) for h in hs}
        y_col = {h: jnp.sum(s_new[h] * row(r, h), axis=-1, keepdims=True) for h in hs}
        for h in hs:
            sout_ref[0, h] = s_new[h]
            y_scr[h:h + 1, :] = jnp.sum(jnp.where(eye, y_col[h], 0.0), axis=0, keepdims=True)

    y = y_scr[...]
    mean = jnp.mean(y, axis=-1, keepdims=True)
    yc = y - mean
    var = jnp.mean(yc * yc, axis=-1, keepdims=True)
    yn = yc * lax.rsqrt(var + GN_EPS) * gg_ref[...] + gb_ref[...]
    bonus = jnp.sum(r * k2 * rk_ref[...], axis=-1, keepdims=True)
    yg_ref[0] = ((yn + bonus * v) * _silu(z_ref[0])).astype(BF16)


def wkv_sample(rkvz, wl, al, k_k, k_a, r_k, gn_g, gn_b, state):
    _, m, e = rkvz.shape
    nh = e // HEAD_DIM
    hd = HEAD_DIM
    rkvz4 = rkvz.reshape(4, m, nh, hd)
    proj = lambda q: pl.BlockSpec((None, 1, nh, hd), lambda i: (q, i, 0, 0))
    tok = pl.BlockSpec((1, nh, hd), lambda i: (i, 0, 0))
    par = pl.BlockSpec((nh, hd), lambda i: (0, 0))
    st = pl.BlockSpec((1, nh, hd, hd), lambda i: (i, 0, 0, 0))
    as_heads = lambda x: x.reshape(nh, hd)
    return pl.pallas_call(
        _wkv_sample_kernel,
        grid=(m,),
        in_specs=[proj(0), proj(1), proj(2), proj(3), tok, tok, par, par, par, par, par, st],
        out_specs=[tok, st],
        out_shape=[jax.ShapeDtypeStruct((m, nh, hd), BF16), jax.ShapeDtypeStruct(state.shape, F32)],
        scratch_shapes=[pltpu.VMEM((nh, hd), F32)],
        compiler_params=_cparams(("arbitrary",)),
        name="wkv_sample",
    )(rkvz4, rkvz4, rkvz4, rkvz4, wl.reshape(m, nh, hd), al.reshape(m, nh, hd),
      as_heads(k_k), as_heads(k_a), as_heads(r_k), as_heads(gn_g), as_heads(gn_b), state)


def rope_tables(pos):
    half = ROPE_DIM // 2
    inv_freq = ROPE_THETA ** (-jnp.arange(half, dtype=F32) * 2.0 / ROPE_DIM)
    ang = pos.astype(F32)[:, None] * inv_freq[None, :]
    cos = jnp.cos(ang)
    sin = jnp.sin(ang)
    rows = pos.shape[0]
    ones = jnp.ones((rows, HEAD_DIM - ROPE_DIM), F32)
    zeros_h = jnp.zeros((rows, half), F32)
    zeros_r = jnp.zeros((rows, HEAD_DIM - ROPE_DIM), F32)
    cos_h = jnp.concatenate([cos, cos, ones], axis=1)
    sa_h = jnp.concatenate([-sin, zeros_h, zeros_r], axis=1)
    sb_h = jnp.concatenate([zeros_h, sin, zeros_r], axis=1)
    two = lambda t: jnp.concatenate([t, t], axis=1)
    return two(cos_h), two(sa_h), two(sb_h)


def _attn_prompt_kernel(sink_ref, q_ref, kc_ref, kp_ref, vc_ref, vp_ref, z_ref, o_ref, *, first_block):
    n = pl.program_id(1) + first_block
    hd = HEAD_DIM
    blk = q_ref.shape[0]
    n_kv = kc_ref.shape[1] // hd
    grp = q_ref.shape[1] // (n_kv * hd)
    qi = lax.broadcasted_iota(jnp.int32, (blk, 2 * blk), 0)
    kj = lax.broadcasted_iota(jnp.int32, (blk, 2 * blk), 1) - blk
    kpos = n * blk + kj
    diff = qi - kj
    valid = (kpos >= LEAD) & (diff >= 0) & (diff <= WINDOW)
    k_all = jnp.concatenate([kp_ref[...], kc_ref[...]], axis=0).astype(BF16)
    v_all = jnp.concatenate([vp_ref[...], vc_ref[...]], axis=0).astype(BF16)

    def scores(h):
        k_h = k_all[:, h * hd:(h + 1) * hd]
        return [_dot_nt(q_ref[:, (h * grp + gi) * hd:(h * grp + gi + 1) * hd], k_h) for gi in range(grp)]

    outs = []
    s_next = scores(0)
    for h in range(n_kv):
        s_cur = s_next
        if h + 1 < n_kv:
            s_next = scores(h + 1)
        v_h = v_all[:, h * hd:(h + 1) * hd]
        for g0 in range(0, grp, SOFTMAX_BATCH):
            gs = range(g0, min(g0 + SOFTMAX_BATCH, grp))
            sks = {gi: sink_ref[h * grp + gi] * LOG2E for gi in gs}
            ss = {gi: jnp.where(valid, s_cur[gi], -jnp.inf) for gi in gs}
            ms = {gi: jnp.maximum(jnp.max(ss[gi], axis=-1, keepdims=True), sks[gi]) for gi in gs}
            ps = {gi: jnp.exp2(ss[gi] - ms[gi]) for gi in gs}
            dens = {gi: jnp.sum(ps[gi], axis=-1, keepdims=True) + jnp.exp2(sks[gi] - ms[gi]) for gi in gs}
            outs += [_dot(ps[gi].astype(BF16), v_h) / dens[gi] for gi in gs]
    att = jnp.concatenate(outs, axis=1)
    o_ref[...] = (att * _silu(z_ref[...])).astype(o_ref.dtype)


def attn_prompt(sinks, q, k, v, z, n_batch, first_block):
    m, e = q.shape
    nb = m // (n_batch * BLOCK)
    nb_out = nb - first_block
    kw = k.shape[1]
    cur = lambda i, n: (i * nb + n + first_block, 0)
    prv = lambda i, n: (i * nb + jnp.maximum(n + first_block - 1, 0), 0)
    return pl.pallas_call(
        functools.partial(_attn_prompt_kernel, first_block=first_block),
        grid=(n_batch, nb_out),
        in_specs=[pl.BlockSpec(memory_space=pltpu.SMEM),
                  pl.BlockSpec((BLOCK, e), cur),
                  pl.BlockSpec((BLOCK, kw), cur), pl.BlockSpec((BLOCK, kw), prv),
                  pl.BlockSpec((BLOCK, kw), cur), pl.BlockSpec((BLOCK, kw), prv),
                  pl.BlockSpec((BLOCK, e), cur)],
        out_specs=pl.BlockSpec((BLOCK, e), lambda i, n: (i * nb_out + n, 0)),
        out_shape=jax.ShapeDtypeStruct((n_batch * nb_out * BLOCK, e), BF16),
        compiler_params=_cparams(("arbitrary", "arbitrary")),
        name="attn_prompt",
    )(sinks, q, k, k, v, v, z)


def _attn_sample_kernel(sink_ref, q_ref, kc_ref, vc_ref, kn_ref, vn_ref, z_ref, o_ref, ko_ref, vo_ref):
    hd = HEAD_DIM
    win = kc_ref.shape[1]
    n_kv = kc_ref.shape[2] // hd
    nq = q_ref.shape[1]
    grp = nq // n_kv
    pad = 8
    kc = kc_ref[0]
    vc = vc_ref[0]
    kn = kn_ref[0]
    vn = vn_ref[0]
    first = lax.broadcasted_iota(jnp.int32, (pad, kc.shape[1]), 0) == 0
    k_all = jnp.concatenate([kc, jnp.where(first, kn, 0.0)], axis=0).astype(BF16)
    v_all = jnp.concatenate([vc, jnp.where(first, vn, 0.0)], axis=0).astype(BF16)
    col = lax.broadcasted_iota(jnp.int32, (grp, win + pad), 1)
    valid = (col <= win) & (win - col <= WINDOW)
    q = q_ref[0].astype(BF16)
    row_i = lax.broadcasted_iota(jnp.int32, (grp, 1), 0)
    hs = range(n_kv)
    sks = []
    for h in hs:
        sk = jnp.zeros((grp, 1), F32)
        for gi in range(grp):
            sk = jnp.where(row_i == gi, sink_ref[h * grp + gi] * LOG2E, sk)
        sks.append(sk)
    ss = [jnp.where(valid, _dot_nt(q[h * grp:(h + 1) * grp, :], k_all[:, h * hd:(h + 1) * hd]), -jnp.inf)
          for h in hs]
    ms = [jnp.maximum(jnp.max(ss[h], axis=-1, keepdims=True), sks[h]) for h in hs]
    ps = [jnp.exp2(ss[h] - ms[h]) for h in hs]
    dens = [jnp.sum(ps[h], axis=-1, keepdims=True) + jnp.exp2(sks[h] - ms[h]) for h in hs]
    outs = [_dot(ps[h].astype(BF16), v_all[:, h * hd:(h + 1) * hd]) / dens[h] for h in hs]
    att = jnp.concatenate(outs, axis=0)
    o_ref[0] = (att * _silu(z_ref[0])).astype(o_ref.dtype)
    last = lax.broadcasted_iota(jnp.int32, kc.shape, 0) == win - 1
    ko_ref[0] = jnp.where(last, kn, pltpu.roll(kc, win - 1, axis=0))
    vo_ref[0] = jnp.where(last, vn, pltpu.roll(vc, win - 1, axis=0))


def attn_sample(sinks, q, cache_k, cache_v, k_new, v_new, z):
    m, win, kw = cache_k.shape
    nq = q.shape[1]
    hd = HEAD_DIM
    tok = pl.BlockSpec((1, nq, hd), lambda i: (i, 0, 0))
    cache = pl.BlockSpec((1, win, kw), lambda i: (i, 0, 0))
    new = pl.BlockSpec((1, 1, kw), lambda i: (i, 0, 0))
    return pl.pallas_call(
        _attn_sample_kernel,
        grid=(m,),
        in_specs=[pl.BlockSpec(memory_space=pltpu.SMEM), tok, cache, cache, new, new, tok],
        out_specs=[tok, cache, cache],
        out_shape=[jax.ShapeDtypeStruct((m, nq, hd), BF16),
                   jax.ShapeDtypeStruct(cache_k.shape, F32), jax.ShapeDtypeStruct(cache_v.shape, F32)],
        compiler_params=_cparams(("arbitrary",)),
        name="attn_sample",
    )(sinks, q, cache_k, cache_v, k_new, v_new, z)


def _pad_lora(w_down, w_up):
    r = w_down.shape[1]
    return (jnp.pad(w_down, ((0, 0), (0, LORA_PAD - r))).astype(BF16),
            jnp.pad(w_up, ((0, LORA_PAD - r), (0, 0))).astype(BF16))


def kernel(x_prompt, x_sample, state_wkv, state_shift, cache_k, cache_v, meta_tokens, a_norm, a_mu, a_w_rkvz,
           a_w0, a_w1, a_w2, a_a0, a_a1, a_a2, a_k_k, a_k_a, a_r_k, a_gn_g, a_gn_b, a_w_out, kv_norm, w_kv,
           b_norm, b_w_qz, b_sinks, b_w_o, final_norm):
    nb, seq, d = x_prompt.shape
    db, dseq, _ = x_sample.shape
    assert dseq == 1 and a_norm.shape[0] == 1 and b_norm.shape[0] == 1
    e = a_w_rkvz.shape[3]
    win = cache_k.shape[1]
    p_len = LEAD + N_META + seq
    assert p_len % BLOCK == 0 and (LEAD + N_META) == BLOCK
    kvw = N_KV_HEADS * HEAD_DIM

    w_rkvz = a_w_rkvz[0].astype(BF16)
    w1, w2 = _pad_lora(a_w1[0], a_w2[0])
    a1, a2 = _pad_lora(a_a1[0], a_a2[0])
    w_out = a_w_out[0].astype(BF16)
    w_kv_bf = w_kv.astype(BF16)
    w_qz = b_w_qz.astype(BF16)
    w_o = b_w_o[0].astype(BF16)
    mu = a_mu[0]
    sinks = b_sinks[0]
    gains_b = jnp.stack([kv_norm, b_norm[0]])

    tm = p_len // 8

    head = jnp.concatenate([jnp.zeros((LEAD, d), F32), meta_tokens], axis=0)
    xm, hw_p, ha_p, x_last = norm_shift_prompt(x_prompt, head, a_norm[0], mu, w1, a1)
    p_state_shift = x_last.reshape(1, nb, d)
    rkvz = matmul_groups(xm, w_rkvz, tm, F32)
    yg, p_state = wkv_prompt(rkvz, hw_p, ha_p, w2, a_w0[0], a2, a_a0[0], a_k_k[0], a_k_a[0],
                             a_r_k[0].reshape(-1), a_gn_g[0], a_gn_b[0], nb)
    hp, hn_kv, hn_b = matmul_residual_norm_blocks(yg, w_out, x_prompt, gains_b, nb, BF16, True, head=head)

    pos_p = jnp.maximum(jnp.arange(p_len, dtype=jnp.int32) - LEAD, 0)
    tabs_p = tuple(jnp.tile(t, (nb, 1)) for t in rope_tables(pos_p))
    k_p, v_p = matmul_rope(hn_kv, w_kv_bf, tabs_p, tm, kvw, (F32, F32))
    q_p, = matmul_rope(hn_b, w_qz, tabs_p, tm, e, (BF16,), scale=Q_SCALE, n=e)
    z_p = matmul_groups(hn_b[None], w_qz, tm, F32, n=e, col=1)[0]
    skip = (LEAD + N_META) // BLOCK
    att = attn_prompt(sinks, q_p, k_p, v_p, z_p, nb, skip)
    y_prompt, = matmul_residual_norm_blocks(att, w_o, hp, final_norm[None], nb, F32, False, first_block=skip)
    y_prompt = y_prompt.reshape(nb, seq, d)
    tail = lambda t: t.reshape(nb, p_len, kvw)[:, -win:].reshape(nb, win, N_KV_HEADS, HEAD_DIM)
    p_cache_k = tail(k_p)
    p_cache_v = tail(v_p)

    hs = x_sample.reshape(db, d)
    xm_s, hw_s, ha_s, xn_s = norm_shift_sample(hs, state_shift[0], a_norm[0], mu, w1, a1)
    rkvz_s = matmul_groups(xm_s, w_rkvz, db, F32)
    wl_s, al_s = lora_up(hw_s, ha_s, w2, a_w0[0], a2, a_a0[0])
    yg_s, s_state = wkv_sample(rkvz_s, wl_s, al_s, a_k_k[0], a_k_a[0], a_r_k[0].reshape(-1), a_gn_g[0],
                               a_gn_b[0], state_wkv[0])
    hs, hn_kv_s, hn_b_s = matmul_residual_norm(yg_s.reshape(db, e), w_out, hs, gains_b, db, BF16)
    tabs_s = rope_tables(jnp.full((db,), PAST_LEN, jnp.int32))
    k_s, v_s = matmul_rope(hn_kv_s, w_kv_bf, tabs_s, db, kvw, (F32, F32))
    q_s, = matmul_rope(hn_b_s, w_qz, tabs_s, db, e, (F32,), scale=Q_SCALE, n=e)
    z_s = matmul_groups(hn_b_s[None], w_qz, db, F32, n=e, col=1)[0]
    nq = e // HEAD_DIM
    att_s, s_cache_k, s_cache_v = attn_sample(
        sinks, q_s.reshape(db, nq, HEAD_DIM), cache_k.reshape(db, win, kvw), cache_v.reshape(db, win, kvw),
        k_s.reshape(db, 1, kvw), v_s.reshape(db, 1, kvw), z_s.reshape(db, nq, HEAD_DIM))
    y_s, = matmul_residual_norm(att_s.reshape(db, e), w_o, hs, final_norm[None], db, F32, emit_h=False)
    y_sample = y_s.reshape(db, 1, d)

    return (y_prompt, y_sample, p_state[None], p_state_shift,
            p_cache_k, p_cache_v,
            s_state[None], xn_s[None],
            s_cache_k.reshape(cache_k.shape), s_cache_v.reshape(cache_v.shape))
```

```python
import functools
import math

import jax
import jax.numpy as jnp
from jax import lax
from jax.experimental import pallas as pl
from jax.experimental.pallas import tpu as pltpu

F32 = jnp.float32
BF16 = jnp.bfloat16

HEAD_DIM = 64
N_KV_HEADS = 8
WINDOW = 128
BLOCK = 128
ROPE_DIM = HEAD_DIM // 4
ROPE_THETA = 500000.0
N_META = 16
PAST_LEN = 16384
RMS_EPS = 1e-6
GN_EPS = 64e-5
NORM_FLOOR_SQ = 1e-24
LEAD = (-N_META) % BLOCK
CHUNK = 64
WKV_ROWS = 128
SEQS_PER_STEP = 2
HEADS_PER_STREAM = 32
HEADS_PER_STEP = 32
LORA_PAD = 128
MXU_TILE = 256
ROPE_SLAB = 512
VMEM_LIMIT = 48 * 1024 * 1024
LOG2E = 1.0 / math.log(2.0)
DECAY_SCALE = math.exp(-0.5)
SOFTMAX_BATCH = 4
Q_SCALE = HEAD_DIM ** -0.5 * LOG2E


def _cparams(sem):
    return pltpu.CompilerParams(dimension_semantics=sem, vmem_limit_bytes=VMEM_LIMIT)


def _sigmoid(x):
    return 1.0 / (1.0 + jnp.exp2(x * (-LOG2E)))


def _silu(x):
    return x * _sigmoid(x)


def _dot(a, b):
    return jnp.dot(a, b, preferred_element_type=F32)


def _dot_nt(a, b):
    return lax.dot_general(a, b, (((1,), (1,)), ((), ())), preferred_element_type=F32)


def _dot_tn(a, b):
    return lax.dot_general(a, b, (((0,), (0,)), ((), ())), preferred_element_type=F32)


def _split_hi_lo(x):
    hi = x.astype(BF16)
    lo = (x - hi.astype(F32)).astype(BF16)
    return hi, lo


def _mixes(xn, prev, mu_ref, w1_ref, a1_ref, xm_ref, hw_ref, ha_ref):
    xx = prev - xn
    n_proj = xm_ref.shape[0]
    for p in range(n_proj):
        xm_ref[p] = (xn + xx * mu_ref[p:p + 1, :]).astype(xm_ref.dtype)
    xw = (xn + xx * mu_ref[n_proj:n_proj + 1, :]).astype(BF16)
    xa = (xn + xx * mu_ref[n_proj + 1:n_proj + 2, :]).astype(BF16)
    hw_ref[...] = jnp.tanh(_dot(xw, w1_ref[...])).astype(hw_ref.dtype)
    ha_ref[...] = _dot(xa, a1_ref[...]).astype(ha_ref.dtype)


def _norm_shift_kernel(x_ref, head_ref, g_ref, mu_ref, w1_ref, a1_ref, xm_ref, hw_ref, ha_ref, last_ref, carry_ref):
    @pl.when(pl.program_id(1) == 0)
    def _():
        carry_ref[...] = jnp.zeros_like(carry_ref)

    is_head = pl.program_id(1) == 0
    for sq in range(x_ref.shape[0]):
        x = jnp.where(is_head, head_ref[...], x_ref[sq])
        tm = x.shape[0]
        xn = x * lax.rsqrt(jnp.mean(x * x, axis=-1, keepdims=True) + RMS_EPS) * g_ref[...]
        rolled = pltpu.roll(xn, 1, axis=0)
        row = lax.broadcasted_iota(jnp.int32, xn.shape, 0)
        prev = jnp.where(row == 0, carry_ref[sq, 0:1, :], rolled)
        _mixes(xn, prev, mu_ref, w1_ref, a1_ref, xm_ref.at[:, sq], hw_ref.at[sq], ha_ref.at[sq])
        carry_ref[sq, 0:1, :] = xn[tm - 1:tm, :]
        last_ref[sq] = xn[tm - 1:tm, :]


def norm_shift_prompt(x, head, g, mu, w1, a1):
    b, seq, d = x.shape
    tm = BLOCK
    p = tm + seq
    n_mix = mu.shape[0]
    n_proj = n_mix - 2
    lr = w1.shape[1]
    nt = p // tm
    sp = SEQS_PER_STEP
    const = lambda shape: pl.BlockSpec(shape, lambda i, t: (0,) * len(shape))
    hid = pl.BlockSpec((sp, tm, lr), lambda i, t: (i, t, 0))
    xm, hw_act, ha_act, last = pl.pallas_call(
        _norm_shift_kernel,
        grid=(b // sp, nt),
        in_specs=[pl.BlockSpec((sp, tm, d), lambda i, t: (i, jnp.maximum(t - 1, 0), 0)),
                  const((tm, d)), const((1, d)), const((n_mix, d)), const((d, lr)), const((d, lr))],
        out_specs=[pl.BlockSpec((n_proj, sp, tm, d), lambda i, t: (0, i, t, 0)), hid, hid,
                   pl.BlockSpec((sp, 1, d), lambda i, t: (i, 0, 0))],
        out_shape=[jax.ShapeDtypeStruct((n_proj, b, p, d), BF16),
                   jax.ShapeDtypeStruct((b, p, lr), BF16), jax.ShapeDtypeStruct((b, p, lr), BF16),
                   jax.ShapeDtypeStruct((b, 1, d), F32)],
        scratch_shapes=[pltpu.VMEM((sp, 8, d), F32)],
        compiler_params=_cparams(("arbitrary", "arbitrary")),
        name="norm_shift_prompt",
    )(x, head, g.reshape(1, d), mu, w1, a1)
    return xm.reshape(n_proj, b * p, d), hw_act.reshape(b * p, lr), ha_act.reshape(b * p, lr), last


def _norm_shift_sample_kernel(x_ref, prev_ref, g_ref, mu_ref, w1_ref, a1_ref, xm_ref, hw_ref, ha_ref, xn_ref):
    x = x_ref[...]
    xn = x * lax.rsqrt(jnp.mean(x * x, axis=-1, keepdims=True) + RMS_EPS) * g_ref[...]
    xn_ref[...] = xn
    _mixes(xn, prev_ref[...], mu_ref, w1_ref, a1_ref, xm_ref, hw_ref, ha_ref)


def norm_shift_sample(x, prev, g, mu, w1, a1):
    m, d = x.shape
    lr = w1.shape[1]
    return pl.pallas_call(
        _norm_shift_sample_kernel,
        out_shape=[jax.ShapeDtypeStruct((mu.shape[0] - 2, m, d), BF16),
                   jax.ShapeDtypeStruct((m, lr), BF16), jax.ShapeDtypeStruct((m, lr), BF16),
                   jax.ShapeDtypeStruct((m, d), F32)],
        name="norm_shift_sample",
    )(x, prev, g.reshape(1, d), mu, w1, a1)


def _rope(y, cos, sin_a, sin_b):
    half = ROPE_DIM // 2
    step = ROPE_SLAB
    rep = step // cos.shape[1]
    tile = lambda t: jnp.concatenate([t] * rep, axis=1)
    cos_t, sa_t, sb_t = tile(cos), tile(sin_a), tile(sin_b)
    outs = []
    for j in range(y.shape[1] // step):
        ys = y[:, j * step:(j + 1) * step]
        outs.append(ys * cos_t + pltpu.roll(ys, step - half, axis=1) * sa_t + pltpu.roll(ys, half, axis=1) * sb_t)
    return jnp.concatenate(outs, axis=1) if len(outs) > 1 else outs[0]


def _mm_group_kernel(x_ref, w_ref, o_ref):
    o_ref[0] = _dot(x_ref[0], w_ref[0]).astype(o_ref.dtype)


def matmul_groups(x, w, tm, out_dtype, n=None, col=0):
    g = w.shape[0]
    n = w.shape[2] if n is None else n
    _, m, kdim = x.shape
    return pl.pallas_call(
        _mm_group_kernel,
        grid=(g, m // tm),
        in_specs=[pl.BlockSpec((1, tm, kdim), lambda q, i: (q, i, 0)),
                  pl.BlockSpec((1, kdim, n), lambda q, i: (q, 0, col))],
        out_specs=pl.BlockSpec((1, tm, n), lambda q, i: (q, i, 0)),
        out_shape=jax.ShapeDtypeStruct((g, m, n), out_dtype),
        compiler_params=_cparams(("arbitrary", "arbitrary")),
        name="matmul_groups",
    )(x, w)


def _mm_rope_kernel(x_ref, w_ref, cos_ref, sa_ref, sb_ref, *o_refs, n_rope, scale):
    y = _dot(x_ref[...], w_ref[...])
    rot = _rope(y[:, :n_rope], cos_ref[...], sa_ref[...], sb_ref[...])
    if scale != 1.0:
        rot = rot * scale
    o_refs[0][...] = rot.astype(o_refs[0].dtype)
    if len(o_refs) > 1:
        o_refs[1][...] = y[:, n_rope:].astype(o_refs[1].dtype)


def matmul_rope(x, w, tables, tm, n_rope, out_dtypes, scale=1.0, n=None):
    m, kdim = x.shape
    n = w.shape[-1] if n is None else n
    w_block = (kdim, n) if w.ndim == 2 else (None, kdim, n)
    lanes = tables[0].shape[1]
    widths = [n_rope] + ([n - n_rope] if n > n_rope else [])
    tab = pl.BlockSpec((tm, lanes), lambda i: (i, 0))
    outs = pl.pallas_call(
        functools.partial(_mm_rope_kernel, n_rope=n_rope, scale=scale),
        grid=(m // tm,),
        in_specs=[pl.BlockSpec((tm, kdim), lambda i: (i, 0)),
                  pl.BlockSpec(w_block, lambda i: (0,) * w.ndim, pipeline_mode=pl.Buffered(1)),
                  tab, tab, tab],
        out_specs=[pl.BlockSpec((tm, wd), lambda i: (i, 0)) for wd in widths],
        out_shape=[jax.ShapeDtypeStruct((m, wd), dt) for wd, dt in zip(widths, out_dtypes)],
        compiler_params=_cparams(("arbitrary",)),
        name="matmul_rope",
    )(x, w, *tables)
    return outs


def _mm_res_norm_kernel(x_ref, w_ref, res_ref, g_ref, *out_refs, emit_h):
    h = res_ref[...] + _dot(x_ref[...], w_ref[...])
    hn_refs = out_refs
    if emit_h:
        out_refs[0][...] = h
        hn_refs = out_refs[1:]
    inv = lax.rsqrt(jnp.mean(h * h, axis=-1, keepdims=True) + RMS_EPS)
    for j, hn_ref in enumerate(hn_refs):
        hn_ref[...] = (h * inv * g_ref[j:j + 1, :]).astype(hn_ref.dtype)


def matmul_residual_norm(x, w, res, gains, tm, norm_dtype, emit_h=True):
    m, kdim = x.shape
    n = w.shape[1]
    ng = gains.shape[0]
    row = lambda width: pl.BlockSpec((tm, width), lambda i: (i, 0))
    return pl.pallas_call(
        functools.partial(_mm_res_norm_kernel, emit_h=emit_h),
        grid=(m // tm,),
        in_specs=[row(kdim),
                  pl.BlockSpec((kdim, n), lambda i: (0, 0), pipeline_mode=pl.Buffered(1)),
                  row(n),
                  pl.BlockSpec((ng, n), lambda i: (0, 0))],
        out_specs=[row(n)] * (int(emit_h) + ng),
        out_shape=[jax.ShapeDtypeStruct((m, n), F32)] * int(emit_h) + [jax.ShapeDtypeStruct((m, n), norm_dtype)] * ng,
        compiler_params=_cparams(("arbitrary",)),
        name="matmul_residual_norm",
    )(x, w, res, gains)


def _mm_res_norm_blocks_kernel(x_ref, w_ref, res_ref, *rest, emit_h, head, first_block):
    if head:
        head_ref, g_ref, *out_refs = rest
        res = jnp.where(pl.program_id(1) + first_block == 0, head_ref[...][None], res_ref[...])
    else:
        g_ref, *out_refs = rest
        res = res_ref[...]
    seqs, blk, n = res.shape
    h = res.reshape(seqs * blk, n) + _dot(x_ref[...].reshape(seqs * blk, x_ref.shape[2]), w_ref[...])
    hn_refs = out_refs
    if emit_h:
        out_refs[0][...] = h.reshape(seqs, blk, n)
        hn_refs = out_refs[1:]
    inv = lax.rsqrt(jnp.mean(h * h, axis=-1, keepdims=True) + RMS_EPS)
    for j, hn_ref in enumerate(hn_refs):
        hn_ref[...] = (h * inv * g_ref[j:j + 1, :]).astype(hn_ref.dtype).reshape(seqs, blk, n)


def matmul_residual_norm_blocks(x, w, res, gains, n_batch, norm_dtype, emit_h, head=None, first_block=0):
    kdim = x.shape[1]
    n = w.shape[1]
    ng = gains.shape[0]
    nb_out = x.shape[0] // (n_batch * BLOCK)
    nb = nb_out + first_block
    sp = SEQS_PER_STEP
    blocks = lambda width, shift: pl.BlockSpec((sp, BLOCK, width), lambda i, t: (i, t + shift, 0))
    const = lambda shape: pl.BlockSpec(shape, lambda i, t: (0,) * len(shape))
    if head is not None:
        res_specs = [pl.BlockSpec((sp, BLOCK, n), lambda i, t: (i, jnp.maximum(t + first_block - 1, 0), 0)),
                     const((BLOCK, n))]
        res_args = [res, head]
    else:
        res_specs, res_args = [blocks(n, first_block)], [res.reshape(n_batch, nb * BLOCK, n)]
    outs = pl.pallas_call(
        functools.partial(_mm_res_norm_blocks_kernel, emit_h=emit_h, head=head is not None, first_block=first_block),
        grid=(n_batch // sp, nb_out),
        in_specs=[blocks(kdim, 0), pl.BlockSpec((kdim, n), lambda i, t: (0, 0), pipeline_mode=pl.Buffered(1))]
        + res_specs + [const((ng, n))],
        out_specs=[blocks(n, first_block)] * int(emit_h) + [blocks(n, 0)] * ng,
        out_shape=[jax.ShapeDtypeStruct((n_batch, nb * BLOCK, n), F32)] * int(emit_h)
        + [jax.ShapeDtypeStruct((n_batch, nb_out * BLOCK, n), norm_dtype)] * ng,
        compiler_params=_cparams(("arbitrary", "arbitrary")),
        name="matmul_residual_norm_blocks",
    )(x.reshape(n_batch, nb_out * BLOCK, kdim), w, *res_args, gains)
    return [o.reshape(-1, n) for o in outs]


def _lora_up_kernel(hw_ref, ha_ref, w2_ref, w0_ref, a2_ref, a0_ref, wl_ref, al_ref):
    wl_ref[...] = w0_ref[...] + _dot(hw_ref[...], w2_ref[...])
    al_ref[...] = a0_ref[...] + _dot(ha_ref[...], a2_ref[...])


def lora_up(hw, ha, w2, w0, a2, a0):
    m = hw.shape[0]
    e = w2.shape[1]
    return pl.pallas_call(
        _lora_up_kernel,
        out_shape=[jax.ShapeDtypeStruct((m, e), F32), jax.ShapeDtypeStruct((m, e), F32)],
        name="lora_up",
    )(hw, ha, w2, w0.reshape(1, e), a2, a0.reshape(1, e))


def _seg_sum(x, ones_bd):
    hi = x.astype(BF16)
    outs = []
    for c in range(x.shape[1] // MXU_TILE):
        sl = slice(c * MXU_TILE, (c + 1) * MXU_TILE)
        outs.append(_dot(hi[:, sl], ones_bd))
    return jnp.concatenate(outs, axis=1) if len(outs) > 1 else outs[0]


def _wkv_prompt_kernel(r_ref, k_ref, v_ref, z_ref, hw_ref, ha_ref, w2_ref, w0_ref, a2_ref, a0_ref,
                       kk_ref, ka_ref, rk_ref, gg_ref, gb_ref, yg_ref, sout_ref, s_ref):
    t_idx = pl.program_id(2)
    c = CHUNK
    hd = HEAD_DIM

    @pl.when(t_idx == 0)
    def _():
        s_ref[...] = jnp.zeros_like(s_ref)

    tb = r_ref.shape[1]
    nh = HEADS_PER_STREAM
    hw = nh * hd
    nc = tb // c

    li = lax.broadcasted_iota(jnp.int32, (MXU_TILE, MXU_TILE), 0) // hd
    lj = lax.broadcasted_iota(jnp.int32, (MXU_TILE, MXU_TILE), 1) // hd
    ones_bd = jnp.where(li == lj, 1.0, 0.0).astype(BF16)
    bi_ = lax.broadcasted_iota(jnp.int32, (tb, tb), 0)
    bj_ = lax.broadcasted_iota(jnp.int32, (tb, tb), 1)
    tri_incl = jnp.where((bj_ <= bi_) & (bj_ // c == bi_ // c), 1.0, 0.0).astype(BF16)
    ti = lax.broadcasted_iota(jnp.int32, (c, c), 0)
    tj = lax.broadcasted_iota(jnp.int32, (c, c), 1)
    ai = lax.broadcasted_iota(jnp.int32, (c, 2 * c), 0)
    aj = lax.broadcasted_iota(jnp.int32, (c, 2 * c), 1)
    upper = aj >= c
    aj_mod = jnp.where(upper, aj - c, aj)
    masks = dict(
        strict=tj < ti,
        eye=jnp.where(ti == tj, 1.0, 0.0).astype(F32),
        top_k=upper & (aj_mod < ai),
        bot=aj_mod <= ai,
        mean_bd=jnp.where(li == lj, 1.0 / hd, 0.0).astype(BF16))

    for st in range(r_ref.shape[2] // hw):
        _wkv_stream(st, hw, nc, r_ref, k_ref, v_ref, z_ref, hw_ref, ha_ref, w2_ref, w0_ref, a2_ref, a0_ref,
                    kk_ref, ka_ref, rk_ref, gg_ref, gb_ref, yg_ref, s_ref, ones_bd, tri_incl, masks)

    @pl.when(t_idx == pl.num_programs(2) - 1)
    def _():
        sout_ref[0] = s_ref[...]


def _wkv_stream(st, hw, nc, r_ref, k_ref, v_ref, z_ref, hw_ref, ha_ref, w2_ref, w0_ref, a2_ref, a0_ref,
                kk_ref, ka_ref, rk_ref, gg_ref, gb_ref, yg_ref, s_ref, ones_bd, tri_incl, masks):
    c = CHUNK
    hd = HEAD_DIM
    nh = hw // hd
    ls = slice(st * hw, (st + 1) * hw)
    h0 = st * nh
    r = r_ref[0, :, ls]
    k = k_ref[0, :, ls]
    v = v_ref[0, :, ls]
    wl = w0_ref[:, ls] + _dot(hw_ref[...], w2_ref[:, ls])
    al = a0_ref[:, ls] + _dot(ha_ref[...], a2_ref[:, ls])
    a = _sigmoid(al)
    lw = (-DECAY_SCALE * LOG2E) * _sigmoid(wl)
    kk = k * kk_ref[:, ls]
    n2 = _seg_sum(kk * kk, ones_bd)
    kk = kk * lax.rsqrt(jnp.maximum(n2, NORM_FLOOR_SQ))
    ka = ka_ref[:, ls]
    k2 = k * (a * ka + (1.0 - ka))
    bb = kk * a

    lw_hi, lw_lo = _split_hi_lo(lw)
    g = _dot(tri_incl, lw_hi) + _dot(tri_incl, lw_lo)
    mid = lambda ci: g[ci * c + c // 2 - 1:ci * c + c // 2, :]
    gm = jnp.concatenate([jnp.broadcast_to(mid(ci), (c, hw)) for ci in range(nc)], axis=0)
    t = g - gm
    e_a = jnp.exp2(t)
    e_prev = jnp.exp2(t - lw)
    e_inv = jnp.exp2(-t)
    e1 = [jnp.exp2(mid(ci)) for ci in range(nc)]
    e2 = [jnp.exp2(g[ci * c + c - 1:ci * c + c, :] - mid(ci)) for ci in range(nc)]

    kkd = (kk * e_prev).astype(BF16)
    rd = (r * e_a).astype(BF16)
    bi = (bb * e_inv).astype(BF16)
    ki = (k2 * e_inv).astype(BF16)
    v_bf = v.astype(BF16)
    zeros_cv = jnp.zeros((c, hd), BF16)

    pairs = [(ci, h) for ci in range(nc) for h in range(nh)]
    rows = lambda ci: slice(ci * c, (ci + 1) * c)
    cols = lambda h: slice(h * hd, (h + 1) * hd)
    xs = {(ci, h): jnp.concatenate([kkd[rows(ci), cols(h)], rd[rows(ci), cols(h)]], axis=0) for ci, h in pairs}
    r1s = {(ci, h): jnp.concatenate([bi[rows(ci), cols(h)], ki[rows(ci), cols(h)]], axis=0) for ci, h in pairs}
    vs = {(ci, h): v_bf[rows(ci), cols(h)] for ci, h in pairs}
    a_mats = {p: _dot_nt(xs[p], r1s[p]) for p in pairs}
    lk_vs = {p: _dot(jnp.where(masks["top_k"], a_mats[p][:c, :], 0.0).astype(BF16),
                     jnp.concatenate([zeros_cv, vs[p]], axis=0)) for p in pairs}
    lps = {p: jnp.where(masks["strict"], a_mats[p][:c, :c], 0.0) for p in pairs}
    ts = {p: masks["eye"] - lps[p] for p in pairs}
    for _ in range(int(math.log2(c)) - 1):
        lpb = {p: lps[p].astype(BF16) for p in pairs}
        lps = {p: _dot(lpb[p], lpb[p]) for p in pairs}
        ts = {p: _dot(ts[p].astype(BF16), (masks["eye"] + lps[p]).astype(BF16)) for p in pairs}
    a_bots = {p: jnp.where(masks["bot"], a_mats[p][c:, :], 0.0).astype(BF16) for p in pairs}
    t_bf = {p: ts[p].astype(BF16) for p in pairs}

    state = [s_ref[h0 + h] for h in range(nh)]
    y_rows = []
    for ci in range(nc):
        hs = range(nh)
        sms = [state[h] * e1[ci][:, cols(h)] for h in hs]
        p_mats = [_dot_nt(xs[ci, h], sms[h].astype(BF16)) for h in hs]
        us = [-_dot(t_bf[ci, h], (p_mats[h][:c, :] + lk_vs[ci, h]).astype(BF16)) for h in hs]
        uvs = [jnp.concatenate([us[h].astype(BF16), vs[ci, h]], axis=0) for h in hs]
        ys = [p_mats[h][c:, :] + _dot(a_bots[ci, h], uvs[h]) for h in hs]
        state = [(sms[h] + _dot_tn(uvs[h], r1s[ci, h])) * e2[ci][:, cols(h)] for h in hs]
        y_rows.append(jnp.concatenate(ys, axis=1))
    for h in range(nh):
        s_ref[h0 + h] = state[h]
    y = jnp.concatenate(y_rows, axis=0) if nc > 1 else y_rows[0]

    mean = _seg_sum(y, masks["mean_bd"])
    yc = y - mean
    var = _seg_sum(yc * yc, masks["mean_bd"])
    yn = yc * lax.rsqrt(var + GN_EPS) * gg_ref[:, ls] + gb_ref[:, ls]
    bonus = _seg_sum(r * k2 * rk_ref[:, ls], ones_bd)
    yg_ref[:, ls] = ((yn + bonus * v) * _silu(z_ref[0, :, ls])).astype(BF16)


def wkv_prompt(rkvz, hw_act, ha_act, w2, w0, a2, a0, k_k, k_a, r_k, gn_g, gn_b, n_batch):
    _, m, e = rkvz.shape
    p = m // n_batch
    tb = WKV_ROWS
    nt = p // tb
    hw = HEADS_PER_STEP * HEAD_DIM
    nh = e // HEAD_DIM
    row = lambda i, g, t: (i * nt + t, g)
    proj = lambda q: pl.BlockSpec((1, tb, hw), lambda i, g, t: (q, i * nt + t, g))
    par = pl.BlockSpec((1, hw), lambda i, g, t: (0, g))
    lr = w2.shape[0]
    hid = pl.BlockSpec((tb, lr), lambda i, g, t: (i * nt + t, 0))
    up = pl.BlockSpec((lr, hw), lambda i, g, t: (0, g))
    return pl.pallas_call(
        _wkv_prompt_kernel,
        grid=(n_batch, e // hw, nt),
        in_specs=[proj(0), proj(1), proj(2), proj(3), hid, hid, up, par, up, par,
                  par, par, par, par, par],
        out_specs=[pl.BlockSpec((tb, hw), row),
                   pl.BlockSpec((1, HEADS_PER_STEP, HEAD_DIM, HEAD_DIM), lambda i, g, t: (i, g, 0, 0))],
        out_shape=[jax.ShapeDtypeStruct((m, e), BF16),
                   jax.ShapeDtypeStruct((n_batch, nh, HEAD_DIM, HEAD_DIM), F32)],
        scratch_shapes=[pltpu.VMEM((HEADS_PER_STEP, HEAD_DIM, HEAD_DIM), F32)],
        compiler_params=_cparams(("arbitrary", "arbitrary", "arbitrary")),
        name="wkv_prompt",
    )(rkvz, rkvz, rkvz, rkvz, hw_act, ha_act, w2, w0.reshape(1, e), a2, a0.reshape(1, e),
      k_k.reshape(1, e), k_a.reshape(1, e), r_k.reshape(1, e), gn_g.reshape(1, e), gn_b.reshape(1, e))


def _wkv_sample_kernel(r_ref, k_ref, v_ref, z_ref, wl_ref, al_ref, kk_ref, ka_ref, rk_ref, gg_ref, gb_ref,
                       s_ref, yg_ref, sout_ref, y_scr):
    hd = HEAD_DIM
    r = r_ref[0]
    k = k_ref[0]
    v = v_ref[0]
    a = _sigmoid(al_ref[0])
    d = jnp.exp2((-DECAY_SCALE * LOG2E) * _sigmoid(wl_ref[0]))
    kk = k * kk_ref[...]
    kk = kk / jnp.maximum(jnp.sqrt(jnp.sum(kk * kk, axis=-1, keepdims=True)), 1e-12)
    k2 = k * (1.0 + (a - 1.0) * ka_ref[...])
    bb = kk * a
    nh = r.shape[0]
    ii = lax.broadcasted_iota(jnp.int32, (hd, hd), 0)
    jj = lax.broadcasted_iota(jnp.int32, (hd, hd), 1)
    eye = ii == jj

    row = lambda x, h: x[h:h + 1, :]
    group = 8
    for h0 in range(0, nh, group):
        hs = range(h0, h0 + group)
        s = {h: s_ref[0, h] for h in hs}
        sa = {h: jnp.sum(s[h] * row(kk, h), axis=-1, keepdims=True) for h in hs}
        v_col = {h: jnp.sum(jnp.where(eye, row(v, h), 0.0), axis=-1, keepdims=True) for h in hs}
        s_new = {h: s[h] * row(d, h) - sa[h] * row(bb, h) + v_col[h] * row(k2, h) for h in hs}
        y_col = {h: jnp.sum(s_new[h] * row(r, h), axis=-1, keepdims=True) for h in hs}
        for h in hs:
            sout_ref[0, h] = s_new[h]
            y_scr[h:h + 1, :] = jnp.sum(jnp.where(eye, y_col[h], 0.0), axis=0, keepdims=True)

    y = y_scr[...]
    mean = jnp.mean(y, axis=-1, keepdims=True)
    yc = y - mean
    var = jnp.mean(yc * yc, axis=-1, keepdims=True)
    yn = yc * lax.rsqrt(var + GN_EPS) * gg_ref[...] + gb_ref[...]
    bonus = jnp.sum(r * k2 * rk_ref[...], axis=-1, keepdims=True)
    yg_ref[0] = ((yn + bonus * v) * _silu(z_ref[0])).astype(BF16)


def wkv_sample(rkvz, wl, al, k_k, k_a, r_k, gn_g, gn_b, state):
    _, m, e = rkvz.shape
    nh = e // HEAD_DIM
    hd = HEAD_DIM
    rkvz4 = rkvz.reshape(4, m, nh, hd)
    proj = lambda q: pl.BlockSpec((None, 1, nh, hd), lambda i: (q, i, 0, 0))
    tok = pl.BlockSpec((1, nh, hd), lambda i: (i, 0, 0))
    par = pl.BlockSpec((nh, hd), lambda i: (0, 0))
    st = pl.BlockSpec((1, nh, hd, hd), lambda i: (i, 0, 0, 0))
    as_heads = lambda x: x.reshape(nh, hd)
    return pl.pallas_call(
        _wkv_sample_kernel,
        grid=(m,),
        in_specs=[proj(0), proj(1), proj(2), proj(3), tok, tok, par, par, par, par, par, st],
        out_specs=[tok, st],
        out_shape=[jax.ShapeDtypeStruct((m, nh, hd), BF16), jax.ShapeDtypeStruct(state.shape, F32)],
        scratch_shapes=[pltpu.VMEM((nh, hd), F32)],
        compiler_params=_cparams(("arbitrary",)),
        name="wkv_sample",
    )(rkvz4, rkvz4, rkvz4, rkvz4, wl.reshape(m, nh, hd), al.reshape(m, nh, hd),
      as_heads(k_k), as_heads(k_a), as_heads(r_k), as_heads(gn_g), as_heads(gn_b), state)


def rope_tables(pos):
    half = ROPE_DIM // 2
    inv_freq = ROPE_THETA ** (-jnp.arange(half, dtype=F32) * 2.0 / ROPE_DIM)
    ang = pos.astype(F32)[:, None] * inv_freq[None, :]
    cos = jnp.cos(ang)
    sin = jnp.sin(ang)
    rows = pos.shape[0]
    ones = jnp.ones((rows, HEAD_DIM - ROPE_DIM), F32)
    zeros_h = jnp.zeros((rows, half), F32)
    zeros_r = jnp.zeros((rows, HEAD_DIM - ROPE_DIM), F32)
    cos_h = jnp.concatenate([cos, cos, ones], axis=1)
    sa_h = jnp.concatenate([-sin, zeros_h, zeros_r], axis=1)
    sb_h = jnp.concatenate([zeros_h, sin, zeros_r], axis=1)
    two = lambda t: jnp.concatenate([t, t], axis=1)
    return two(cos_h), two(sa_h), two(sb_h)


def _attn_prompt_kernel(sink_ref, q_ref, kc_ref, kp_ref, vc_ref, vp_ref, z_ref, o_ref, *, first_block):
    n = pl.program_id(1) + first_block
    hd = HEAD_DIM
    blk = q_ref.shape[0]
    n_kv = kc_ref.shape[1] // hd
    grp = q_ref.shape[1] // (n_kv * hd)
    qi = lax.broadcasted_iota(jnp.int32, (blk, 2 * blk), 0)
    kj = lax.broadcasted_iota(jnp.int32, (blk, 2 * blk), 1) - blk
    kpos = n * blk + kj
    diff = qi - kj
    valid = (kpos >= LEAD) & (diff >= 0) & (diff <= WINDOW)
    k_all = jnp.concatenate([kp_ref[...], kc_ref[...]], axis=0).astype(BF16)
    v_all = jnp.concatenate([vp_ref[...], vc_ref[...]], axis=0).astype(BF16)

    def scores(h):
        k_h = k_all[:, h * hd:(h + 1) * hd]
        return [_dot_nt(q_ref[:, (h * grp + gi) * hd:(h * grp + gi + 1) * hd], k_h) for gi in range(grp)]

    outs = []
    s_next = scores(0)
    for h in range(n_kv):
        s_cur = s_next
        if h + 1 < n_kv:
            s_next = scores(h + 1)
        v_h = v_all[:, h * hd:(h + 1) * hd]
        for g0 in range(0, grp, SOFTMAX_BATCH):
            gs = range(g0, min(g0 + SOFTMAX_BATCH, grp))
            sks = {gi: sink_ref[h * grp + gi] * LOG2E for gi in gs}
            ss = {gi: jnp.where(valid, s_cur[gi], -jnp.inf) for gi in gs}
            ms = {gi: jnp.maximum(jnp.max(ss[gi], axis=-1, keepdims=True), sks[gi]) for gi in gs}
            ps = {gi: jnp.exp2(ss[gi] - ms[gi]) for gi in gs}
            dens = {gi: jnp.sum(ps[gi], axis=-1, keepdims=True) + jnp.exp2(sks[gi] - ms[gi]) for gi in gs}
            outs += [_dot(ps[gi].astype(BF16), v_h) / dens[gi] for gi in gs]
    att = jnp.concatenate(outs, axis=1)
    o_ref[...] = (att * _silu(z_ref[...])).astype(o_ref.dtype)


def attn_prompt(sinks, q, k, v, z, n_batch, first_block):
    m, e = q.shape
    nb = m // (n_batch * BLOCK)
    nb_out = nb - first_block
    kw = k.shape[1]
    cur = lambda i, n: (i * nb + n + first_block, 0)
    prv = lambda i, n: (i * nb + jnp.maximum(n + first_block - 1, 0), 0)
    return pl.pallas_call(
        functools.partial(_attn_prompt_kernel, first_block=first_block),
        grid=(n_batch, nb_out),
        in_specs=[pl.BlockSpec(memory_space=pltpu.SMEM),
                  pl.BlockSpec((BLOCK, e), cur),
                  pl.BlockSpec((BLOCK, kw), cur), pl.BlockSpec((BLOCK, kw), prv),
                  pl.BlockSpec((BLOCK, kw), cur), pl.BlockSpec((BLOCK, kw), prv),
                  pl.BlockSpec((BLOCK, e), cur)],
        out_specs=pl.BlockSpec((BLOCK, e), lambda i, n: (i * nb_out + n, 0)),
        out_shape=jax.ShapeDtypeStruct((n_batch * nb_out * BLOCK, e), BF16),
        compiler_params=_cparams(("arbitrary", "arbitrary")),
        name="attn_prompt",
    )(sinks, q, k, k, v, v, z)


def _attn_sample_kernel(sink_ref, q_ref, kc_ref, vc_ref, kn_ref, vn_ref, z_ref, o_ref, ko_ref, vo_ref):
    hd = HEAD_DIM
    win = kc_ref.shape[1]
    n_kv = kc_ref.shape[2] // hd
    nq = q_ref.shape[1]
    grp = nq // n_kv
    pad = 8
    kc = kc_ref[0]
    vc = vc_ref[0]
    kn = kn_ref[0]
    vn = vn_ref[0]
    first = lax.broadcasted_iota(jnp.int32, (pad, kc.shape[1]), 0) == 0
    k_all = jnp.concatenate([kc, jnp.where(first, kn, 0.0)], axis=0).astype(BF16)
    v_all = jnp.concatenate([vc, jnp.where(first, vn, 0.0)], axis=0).astype(BF16)
    col = lax.broadcasted_iota(jnp.int32, (grp, win + pad), 1)
    valid = (col <= win) & (win - col <= WINDOW)
    q = q_ref[0].astype(BF16)
    row_i = lax.broadcasted_iota(jnp.int32, (grp, 1), 0)
    hs = range(n_kv)
    sks = []
    for h in hs:
        sk = jnp.zeros((grp, 1), F32)
        for gi in range(grp):
            sk = jnp.where(row_i == gi, sink_ref[h * grp + gi] * LOG2E, sk)
        sks.append(sk)
    ss = [jnp.where(valid, _dot_nt(q[h * grp:(h + 1) * grp, :], k_all[:, h * hd:(h + 1) * hd]), -jnp.inf)
          for h in hs]
    ms = [jnp.maximum(jnp.max(ss[h], axis=-1, keepdims=True), sks[h]) for h in hs]
    ps = [jnp.exp2(ss[h] - ms[h]) for h in hs]
    dens = [jnp.sum(ps[h], axis=-1, keepdims=True) + jnp.exp2(sks[h] - ms[h]) for h in hs]
    outs = [_dot(ps[h].astype(BF16), v_all[:, h * hd:(h + 1) * hd]) / dens[h] for h in hs]
    att = jnp.concatenate(outs, axis=0)
    o_ref[0] = (att * _silu(z_ref[0])).astype(o_ref.dtype)
    last = lax.broadcasted_iota(jnp.int32, kc.shape, 0) == win - 1
    ko_ref[0] = jnp.where(last, kn, pltpu.roll(kc, win - 1, axis=0))
    vo_ref[0] = jnp.where(last, vn, pltpu.roll(vc, win - 1, axis=0))


def attn_sample(sinks, q, cache_k, cache_v, k_new, v_new, z):
    m, win, kw = cache_k.shape
    nq = q.shape[1]
    hd = HEAD_DIM
    tok = pl.BlockSpec((1, nq, hd), lambda i: (i, 0, 0))
    cache = pl.BlockSpec((1, win, kw), lambda i: (i, 0, 0))
    new = pl.BlockSpec((1, 1, kw), lambda i: (i, 0, 0))
    return pl.pallas_call(
        _attn_sample_kernel,
        grid=(m,),
        in_specs=[pl.BlockSpec(memory_space=pltpu.SMEM), tok, cache, cache, new, new, tok],
        out_specs=[tok, cache, cache],
        out_shape=[jax.ShapeDtypeStruct((m, nq, hd), BF16),
                   jax.ShapeDtypeStruct(cache_k.shape, F32), jax.ShapeDtypeStruct(cache_v.shape, F32)],
        compiler_params=_cparams(("arbitrary",)),
        name="attn_sample",
    )(sinks, q, cache_k, cache_v, k_new, v_new, z)


def _pad_lora(w_down, w_up):
    r = w_down.shape[1]
    return (jnp.pad(w_down, ((0, 0), (0, LORA_PAD - r))).astype(BF16),
            jnp.pad(w_up, ((0, LORA_PAD - r), (0, 0))).astype(BF16))


def kernel(x_prompt, x_sample, state_wkv, state_shift, cache_k, cache_v, meta_tokens, a_norm, a_mu, a_w_rkvz,
           a_w0, a_w1, a_w2, a_a0, a_a1, a_a2, a_k_k, a_k_a, a_r_k, a_gn_g, a_gn_b, a_w_out, kv_norm, w_kv,
           b_norm, b_w_qz, b_sinks, b_w_o, final_norm):
    nb, seq, d = x_prompt.shape
    db, dseq, _ = x_sample.shape
    assert dseq == 1 and a_norm.shape[0] == 1 and b_norm.shape[0] == 1
    e = a_w_rkvz.shape[3]
    win = cache_k.shape[1]
    p_len = LEAD + N_META + seq
    assert p_len % BLOCK == 0 and (LEAD + N_META) == BLOCK
    kvw = N_KV_HEADS * HEAD_DIM

    w_rkvz = a_w_rkvz[0].astype(BF16)
    w1, w2 = _pad_lora(a_w1[0], a_w2[0])
    a1, a2 = _pad_lora(a_a1[0], a_a2[0])
    w_out = a_w_out[0].astype(BF16)
    w_kv_bf = w_kv.astype(BF16)
    w_qz = b_w_qz.astype(BF16)
    w_o = b_w_o[0].astype(BF16)
    mu = a_mu[0]
    sinks = b_sinks[0]
    gains_b = jnp.stack([kv_norm, b_norm[0]])

    tm = p_len // 8

    head = jnp.concatenate([jnp.zeros((LEAD, d), F32), meta_tokens], axis=0)
    xm, hw_p, ha_p, x_last = norm_shift_prompt(x_prompt, head, a_norm[0], mu, w1, a1)
    p_state_shift = x_last.reshape(1, nb, d)
    rkvz = matmul_groups(xm, w_rkvz, tm, F32)
    yg, p_state = wkv_prompt(rkvz, hw_p, ha_p, w2, a_w0[0], a2, a_a0[0], a_k_k[0], a_k_a[0],
                             a_r_k[0].reshape(-1), a_gn_g[0], a_gn_b[0], nb)
    hp, hn_kv, hn_b = matmul_residual_norm_blocks(yg, w_out, x_prompt, gains_b, nb, BF16, True, head=head)

    pos_p = jnp.maximum(jnp.arange(p_len, dtype=jnp.int32) - LEAD, 0)
    tabs_p = tuple(jnp.tile(t, (nb, 1)) for t in rope_tables(pos_p))
    k_p, v_p = matmul_rope(hn_kv, w_kv_bf, tabs_p, tm, kvw, (F32, F32))
    q_p, = matmul_rope(hn_b, w_qz, tabs_p, tm, e, (BF16,), scale=Q_SCALE, n=e)
    z_p = matmul_groups(hn_b[None], w_qz, tm, F32, n=e, col=1)[0]
    skip = (LEAD + N_META) // BLOCK
    att = attn_prompt(sinks, q_p, k_p, v_p, z_p, nb, skip)
    y_prompt, = matmul_residual_norm_blocks(att, w_o, hp, final_norm[None], nb, F32, False, first_block=skip)
    y_prompt = y_prompt.reshape(nb, seq, d)
    tail = lambda t: t.reshape(nb, p_len, kvw)[:, -win:].reshape(nb, win, N_KV_HEADS, HEAD_DIM)
    p_cache_k = tail(k_p)
    p_cache_v = tail(v_p)

    hs = x_sample.reshape(db, d)
    xm_s, hw_s, ha_s, xn_s = norm_shift_sample(hs, state_shift[0], a_norm[0], mu, w1, a1)
    rkvz_s = matmul_groups(xm_s, w_rkvz, db, F32)
    wl_s, al_s = lora_up(hw_s, ha_s, w2, a_w0[0], a2, a_a0[0])
    yg_s, s_state = wkv_sample(rkvz_s, wl_s, al_s, a_k_k[0], a_k_a[0], a_r_k[0].reshape(-1), a_gn_g[0],
                               a_gn_b[0], state_wkv[0])
    hs, hn_kv_s, hn_b_s = matmul_residual_norm(yg_s.reshape(db, e), w_out, hs, gains_b, db, BF16)
    tabs_s = rope_tables(jnp.full((db,), PAST_LEN, jnp.int32))
    k_s, v_s = matmul_rope(hn_kv_s, w_kv_bf, tabs_s, db, kvw, (F32, F32))
    q_s, = matmul_rope(hn_b_s, w_qz, tabs_s, db, e, (F32,), scale=Q_SCALE, n=e)
    z_s = matmul_groups(hn_b_s[None], w_qz, db, F32, n=e, col=1)[0]
    nq = e // HEAD_DIM
    att_s, s_cache_k, s_cache_v = attn_sample(
        sinks, q_s.reshape(db, nq, HEAD_DIM), cache_k.reshape(db, win, kvw), cache_v.reshape(db, win, kvw),
        k_s.reshape(db, 1, kvw), v_s.reshape(db, 1, kvw), z_s.reshape(db, nq, HEAD_DIM))
    y_s, = matmul_residual_norm(att_s.reshape(db, e), w_o, hs, final_norm[None], db, F32, emit_h=False)
    y_sample = y_s.reshape(db, 1, d)

    return (y_prompt, y_sample, p_state[None], p_state_shift,
            p_cache_k, p_cache_v,
            s_state[None], xn_s[None],
            s_cache_k.reshape(cache_k.shape), s_cache_v.reshape(cache_v.shape))
```

```python
import functools
import math

import jax
import jax.numpy as jnp
from jax import lax
from jax.experimental import pallas as pl
from jax.experimental.pallas import tpu as pltpu

F32 = jnp.float32
BF16 = jnp.bfloat16

HEAD_DIM = 64
N_KV_HEADS = 8
WINDOW = 128
BLOCK = 128
ROPE_DIM = HEAD_DIM // 4
ROPE_THETA = 500000.0
N_META = 16
PAST_LEN = 16384
RMS_EPS = 1e-6
GN_EPS = 64e-5
NORM_FLOOR_SQ = 1e-24
LEAD = (-N_META) % BLOCK
CHUNK = 64
WKV_ROWS = 128
CAST_CHUNKS = 32
SEQS_PER_STEP = 2
HEADS_PER_STREAM = 32
HEADS_PER_STEP = 32
LORA_PAD = 128
MXU_TILE = 256
ROPE_SLAB = 512
VMEM_LIMIT = 48 * 1024 * 1024
LOG2E = 1.0 / math.log(2.0)
DECAY_SCALE = math.exp(-0.5)
SOFTMAX_BATCH = 4
Q_SCALE = HEAD_DIM ** -0.5 * LOG2E


def _cparams(sem):
    return pltpu.CompilerParams(dimension_semantics=sem, vmem_limit_bytes=VMEM_LIMIT)


def _sigmoid(x):
    return 1.0 / (1.0 + jnp.exp2(x * (-LOG2E)))


def _silu(x):
    return x * _sigmoid(x)


def _dot(a, b):
    return jnp.dot(a, b, preferred_element_type=F32)


def _dot_nt(a, b):
    return lax.dot_general(a, b, (((1,), (1,)), ((), ())), preferred_element_type=F32)


def _dot_tn(a, b):
    return lax.dot_general(a, b, (((0,), (0,)), ((), ())), preferred_element_type=F32)


def _split_hi_lo(x):
    hi = x.astype(BF16)
    lo = (x - hi.astype(F32)).astype(BF16)
    return hi, lo


def _mixes(xn, prev, mu_ref, w1_ref, a1_ref, xm_ref, hw_ref, ha_ref):
    xx = prev - xn
    n_proj = xm_ref.shape[0]
    for p in range(n_proj):
        xm_ref[p] = (xn + xx * mu_ref[p:p + 1, :]).astype(xm_ref.dtype)
    xw = (xn + xx * mu_ref[n_proj:n_proj + 1, :]).astype(BF16)
    xa = (xn + xx * mu_ref[n_proj + 1:n_proj + 2, :]).astype(BF16)
    hw_ref[...] = jnp.tanh(_dot(xw, w1_ref[...])).astype(hw_ref.dtype)
    ha_ref[...] = _dot(xa, a1_ref[...]).astype(ha_ref.dtype)


def _norm_shift_kernel(x_ref, head_ref, g_ref, mu_ref, w1_ref, a1_ref, xm_ref, hw_ref, ha_ref, last_ref, carry_ref):
    @pl.when(pl.program_id(1) == 0)
    def _():
        carry_ref[...] = jnp.zeros_like(carry_ref)

    is_head = pl.program_id(1) == 0
    for sq in range(x_ref.shape[0]):
        x = jnp.where(is_head, head_ref[...], x_ref[sq])
        tm = x.shape[0]
        xn = x * lax.rsqrt(jnp.mean(x * x, axis=-1, keepdims=True) + RMS_EPS) * g_ref[...]
        rolled = pltpu.roll(xn, 1, axis=0)
        row = lax.broadcasted_iota(jnp.int32, xn.shape, 0)
        prev = jnp.where(row == 0, carry_ref[sq, 0:1, :], rolled)
        _mixes(xn, prev, mu_ref, w1_ref, a1_ref, xm_ref.at[:, sq], hw_ref.at[sq], ha_ref.at[sq])
        carry_ref[sq, 0:1, :] = xn[tm - 1:tm, :]
        last_ref[sq] = xn[tm - 1:tm, :]


def norm_shift_prompt(x, head, g, mu, w1, a1):
    b, seq, d = x.shape
    tm = BLOCK
    p = tm + seq
    n_mix = mu.shape[0]
    n_proj = n_mix - 2
    lr = w1.shape[1]
    nt = p // tm
    sp = SEQS_PER_STEP
    const = lambda shape: pl.BlockSpec(shape, lambda i, t: (0,) * len(shape))
    hid = pl.BlockSpec((sp, tm, lr), lambda i, t: (i, t, 0))
    xm, hw_act, ha_act, last = pl.pallas_call(
        _norm_shift_kernel,
        grid=(b // sp, nt),
        in_specs=[pl.BlockSpec((sp, tm, d), lambda i, t: (i, jnp.maximum(t - 1, 0), 0)),
                  const((tm, d)), const((1, d)), const((n_mix, d)), const((d, lr)), const((d, lr))],
        out_specs=[pl.BlockSpec((n_proj, sp, tm, d), lambda i, t: (0, i, t, 0)), hid, hid,
                   pl.BlockSpec((sp, 1, d), lambda i, t: (i, 0, 0))],
        out_shape=[jax.ShapeDtypeStruct((n_proj, b, p, d), BF16),
                   jax.ShapeDtypeStruct((b, p, lr), BF16), jax.ShapeDtypeStruct((b, p, lr), BF16),
                   jax.ShapeDtypeStruct((b, 1, d), F32)],
        scratch_shapes=[pltpu.VMEM((sp, 8, d), F32)],
        compiler_params=_cparams(("arbitrary", "arbitrary")),
        name="norm_shift_prompt",
    )(x, head, g.reshape(1, d), mu, w1, a1)
    return xm.reshape(n_proj, b * p, d), hw_act.reshape(b * p, lr), ha_act.reshape(b * p, lr), last


def _norm_shift_sample_kernel(x_ref, prev_ref, g_ref, mu_ref, w1_ref, a1_ref, xm_ref, hw_ref, ha_ref, xn_ref):
    x = x_ref[...]
    xn = x * lax.rsqrt(jnp.mean(x * x, axis=-1, keepdims=True) + RMS_EPS) * g_ref[...]
    xn_ref[...] = xn
    _mixes(xn, prev_ref[...], mu_ref, w1_ref, a1_ref, xm_ref, hw_ref, ha_ref)


def norm_shift_sample(x, prev, g, mu, w1, a1):
    m, d = x.shape
    lr = w1.shape[1]
    return pl.pallas_call(
        _norm_shift_sample_kernel,
        out_shape=[jax.ShapeDtypeStruct((mu.shape[0] - 2, m, d), BF16),
                   jax.ShapeDtypeStruct((m, lr), BF16), jax.ShapeDtypeStruct((m, lr), BF16),
                   jax.ShapeDtypeStruct((m, d), F32)],
        name="norm_shift_sample",
    )(x, prev, g.reshape(1, d), mu, w1, a1)


def _rope(y, cos, sin_a, sin_b):
    half = ROPE_DIM // 2
    step = ROPE_SLAB
    rep = step // cos.shape[1]
    tile = lambda t: jnp.concatenate([t] * rep, axis=1)
    cos_t, sa_t, sb_t = tile(cos), tile(sin_a), tile(sin_b)
    outs = []
    for j in range(y.shape[1] // step):
        ys = y[:, j * step:(j + 1) * step]
        outs.append(ys * cos_t + pltpu.roll(ys, step - half, axis=1) * sa_t + pltpu.roll(ys, half, axis=1) * sb_t)
    return jnp.concatenate(outs, axis=1) if len(outs) > 1 else outs[0]


def _mm_group_kernel(x_ref, w_ref, o_ref):
    o_ref[0] = _dot(x_ref[0], w_ref[0]).astype(o_ref.dtype)


def matmul_groups(x, w, tm, out_dtype, n=None, col=0):
    g = w.shape[0]
    n = w.shape[2] if n is None else n
    _, m, kdim = x.shape
    return pl.pallas_call(
        _mm_group_kernel,
        grid=(g, m // tm),
        in_specs=[pl.BlockSpec((1, tm, kdim), lambda q, i: (q, i, 0)),
                  pl.BlockSpec((1, kdim, n), lambda q, i: (q, 0, col))],
        out_specs=pl.BlockSpec((1, tm, n), lambda q, i: (q, i, 0)),
        out_shape=jax.ShapeDtypeStruct((g, m, n), out_dtype),
        compiler_params=_cparams(("arbitrary", "arbitrary")),
        name="matmul_groups",
    )(x, w)


def _mm_rope_kernel(x_ref, w_ref, cos_ref, sa_ref, sb_ref, *o_refs, n_rope, scale):
    y = _dot(x_ref[...], w_ref[...])
    rot = _rope(y[:, :n_rope], cos_ref[...], sa_ref[...], sb_ref[...])
    if scale != 1.0:
        rot = rot * scale
    o_refs[0][...] = rot.astype(o_refs[0].dtype)
    if len(o_refs) > 1:
        o_refs[1][...] = y[:, n_rope:].astype(o_refs[1].dtype)


def matmul_rope(x, w, tables, tm, n_rope, out_dtypes, scale=1.0, n=None):
    m, kdim = x.shape
    n = w.shape[-1] if n is None else n
    w_block = (kdim, n) if w.ndim == 2 else (None, kdim, n)
    lanes = tables[0].shape[1]
    widths = [n_rope] + ([n - n_rope] if n > n_rope else [])
    tab = pl.BlockSpec((tm, lanes), lambda i: (i, 0))
    outs = pl.pallas_call(
        functools.partial(_mm_rope_kernel, n_rope=n_rope, scale=scale),
        grid=(m // tm,),
        in_specs=[pl.BlockSpec((tm, kdim), lambda i: (i, 0)),
                  pl.BlockSpec(w_block, lambda i: (0,) * w.ndim, pipeline_mode=pl.Buffered(1)),
                  tab, tab, tab],
        out_specs=[pl.BlockSpec((tm, wd), lambda i: (i, 0)) for wd in widths],
        out_shape=[jax.ShapeDtypeStruct((m, wd), dt) for wd, dt in zip(widths, out_dtypes)],
        compiler_params=_cparams(("arbitrary",)),
        name="matmul_rope",
    )(x, w, *tables)
    return outs


def _mm_res_norm_kernel(x_ref, w_ref, res_ref, g_ref, *out_refs, emit_h):
    h = res_ref[...] + _dot(x_ref[...], w_ref[...])
    hn_refs = out_refs
    if emit_h:
        out_refs[0][...] = h
        hn_refs = out_refs[1:]
    inv = lax.rsqrt(jnp.mean(h * h, axis=-1, keepdims=True) + RMS_EPS)
    for j, hn_ref in enumerate(hn_refs):
        hn_ref[...] = (h * inv * g_ref[j:j + 1, :]).astype(hn_ref.dtype)


def matmul_residual_norm(x, w, res, gains, tm, norm_dtype, emit_h=True):
    m, kdim = x.shape
    n = w.shape[1]
    ng = gains.shape[0]
    row = lambda width: pl.BlockSpec((tm, width), lambda i: (i, 0))
    return pl.pallas_call(
        functools.partial(_mm_res_norm_kernel, emit_h=emit_h),
        grid=(m // tm,),
        in_specs=[row(kdim),
                  pl.BlockSpec((kdim, n), lambda i: (0, 0), pipeline_mode=pl.Buffered(1)),
                  row(n),
                  pl.BlockSpec((ng, n), lambda i: (0, 0))],
        out_specs=[row(n)] * (int(emit_h) + ng),
        out_shape=[jax.ShapeDtypeStruct((m, n), F32)] * int(emit_h) + [jax.ShapeDtypeStruct((m, n), norm_dtype)] * ng,
        compiler_params=_cparams(("arbitrary",)),
        name="matmul_residual_norm",
    )(x, w, res, gains)


def _mm_res_norm_blocks_kernel(x_ref, w_ref, res_ref, *rest, emit_h, head, first_block):
    if head:
        head_ref, g_ref, *out_refs = rest
        res = jnp.where(pl.program_id(1) + first_block == 0, head_ref[...][None], res_ref[...])
    else:
        g_ref, *out_refs = rest
        res = res_ref[...]
    seqs, blk, n = res.shape
    h = res.reshape(seqs * blk, n) + _dot(x_ref[...].reshape(seqs * blk, x_ref.shape[2]), w_ref[...])
    hn_refs = out_refs
    if emit_h:
        out_refs[0][...] = h.reshape(seqs, blk, n)
        hn_refs = out_refs[1:]
    inv = lax.rsqrt(jnp.mean(h * h, axis=-1, keepdims=True) + RMS_EPS)
    for j, hn_ref in enumerate(hn_refs):
        hn_ref[...] = (h * inv * g_ref[j:j + 1, :]).astype(hn_ref.dtype).reshape(seqs, blk, n)


def matmul_residual_norm_blocks(x, w, res, gains, n_batch, norm_dtype, emit_h, head=None, first_block=0):
    kdim = x.shape[1]
    n = w.shape[1]
    ng = gains.shape[0]
    nb_out = x.shape[0] // (n_batch * BLOCK)
    nb = nb_out + first_block
    sp = SEQS_PER_STEP
    blocks = lambda width, shift: pl.BlockSpec((sp, BLOCK, width), lambda i, t: (i, t + shift, 0))
    const = lambda shape: pl.BlockSpec(shape, lambda i, t: (0,) * len(shape))
    if head is not None:
        res_specs = [pl.BlockSpec((sp, BLOCK, n), lambda i, t: (i, jnp.maximum(t + first_block - 1, 0), 0)),
                     const((BLOCK, n))]
        res_args = [res, head]
    else:
        res_specs, res_args = [blocks(n, first_block)], [res.reshape(n_batch, nb * BLOCK, n)]
    outs = pl.pallas_call(
        functools.partial(_mm_res_norm_blocks_kernel, emit_h=emit_h, head=head is not None, first_block=first_block),
        grid=(n_batch // sp, nb_out),
        in_specs=[blocks(kdim, 0), pl.BlockSpec((kdim, n), lambda i, t: (0, 0), pipeline_mode=pl.Buffered(1))]
        + res_specs + [const((ng, n))],
        out_specs=[blocks(n, first_block)] * int(emit_h) + [blocks(n, 0)] * ng,
        out_shape=[jax.ShapeDtypeStruct((n_batch, nb * BLOCK, n), F32)] * int(emit_h)
        + [jax.ShapeDtypeStruct((n_batch, nb_out * BLOCK, n), norm_dtype)] * ng,
        compiler_params=_cparams(("arbitrary", "arbitrary")),
        name="matmul_residual_norm_blocks",
    )(x.reshape(n_batch, nb_out * BLOCK, kdim), w, *res_args, gains)
    return [o.reshape(-1, n) for o in outs]


def _lora_up_kernel(hw_ref, ha_ref, w2_ref, w0_ref, a2_ref, a0_ref, wl_ref, al_ref):
    wl_ref[...] = w0_ref[...] + _dot(hw_ref[...], w2_ref[...])
    al_ref[...] = a0_ref[...] + _dot(ha_ref[...], a2_ref[...])


def lora_up(hw, ha, w2, w0, a2, a0):
    m = hw.shape[0]
    e = w2.shape[1]
    return pl.pallas_call(
        _lora_up_kernel,
        out_shape=[jax.ShapeDtypeStruct((m, e), F32), jax.ShapeDtypeStruct((m, e), F32)],
        name="lora_up",
    )(hw, ha, w2, w0.reshape(1, e), a2, a0.reshape(1, e))


def _seg_sum(x, ones_bd):
    hi = x.astype(BF16)
    outs = []
    for c in range(x.shape[1] // MXU_TILE):
        sl = slice(c * MXU_TILE, (c + 1) * MXU_TILE)
        outs.append(_dot(hi[:, sl], ones_bd))
    return jnp.concatenate(outs, axis=1) if len(outs) > 1 else outs[0]


def _wkv_prompt_kernel(r_ref, k_ref, v_ref, z_ref, hw_ref, ha_ref, w2_ref, w0_ref, a2_ref, a0_ref,
                       kk_ref, ka_ref, rk_ref, gg_ref, gb_ref, *rest, n_cast, cast_every):
    cast_in = rest[:n_cast]
    yg_ref, sout_ref = rest[n_cast:n_cast + 2]
    cast_out = rest[n_cast + 2:2 * n_cast + 2]
    s_ref = rest[-1]
    t_idx = pl.program_id(2)
    c = CHUNK
    hd = HEAD_DIM

    @pl.when(t_idx == 0)
    def _():
        s_ref[...] = jnp.zeros_like(s_ref)

    step = (pl.program_id(0) * pl.num_programs(1) + pl.program_id(1)) * pl.num_programs(2) + t_idx

    @pl.when(step % cast_every == 0)
    def _():
        for src, dst in zip(cast_in, cast_out):
            dst[...] = src[...].astype(dst.dtype)

    tb = r_ref.shape[1]
    nh = HEADS_PER_STREAM
    hw = nh * hd
    nc = tb // c

    li = lax.broadcasted_iota(jnp.int32, (MXU_TILE, MXU_TILE), 0) // hd
    lj = lax.broadcasted_iota(jnp.int32, (MXU_TILE, MXU_TILE), 1) // hd
    ones_bd = jnp.where(li == lj, 1.0, 0.0).astype(BF16)
    bi_ = lax.broadcasted_iota(jnp.int32, (tb, tb), 0)
    bj_ = lax.broadcasted_iota(jnp.int32, (tb, tb), 1)
    tri_incl = jnp.where((bj_ <= bi_) & (bj_ // c == bi_ // c), 1.0, 0.0).astype(BF16)
    ti = lax.broadcasted_iota(jnp.int32, (c, c), 0)
    tj = lax.broadcasted_iota(jnp.int32, (c, c), 1)
    ai = lax.broadcasted_iota(jnp.int32, (c, 2 * c), 0)
    aj = lax.broadcasted_iota(jnp.int32, (c, 2 * c), 1)
    upper = aj >= c
    aj_mod = jnp.where(upper, aj - c, aj)
    masks = dict(
        strict=tj < ti,
        eye=jnp.where(ti == tj, 1.0, 0.0).astype(F32),
        top_k=upper & (aj_mod < ai),
        bot=aj_mod <= ai,
        mean_bd=jnp.where(li == lj, 1.0 / hd, 0.0).astype(BF16))

    for st in range(r_ref.shape[2] // hw):
        _wkv_stream(st, hw, nc, r_ref, k_ref, v_ref, z_ref, hw_ref, ha_ref, w2_ref, w0_ref, a2_ref, a0_ref,
                    kk_ref, ka_ref, rk_ref, gg_ref, gb_ref, yg_ref, s_ref, ones_bd, tri_incl, masks)

    @pl.when(t_idx == pl.num_programs(2) - 1)
    def _():
        sout_ref[0] = s_ref[...]


def _wkv_stream(st, hw, nc, r_ref, k_ref, v_ref, z_ref, hw_ref, ha_ref, w2_ref, w0_ref, a2_ref, a0_ref,
                kk_ref, ka_ref, rk_ref, gg_ref, gb_ref, yg_ref, s_ref, ones_bd, tri_incl, masks):
    c = CHUNK
    hd = HEAD_DIM
    nh = hw // hd
    ls = slice(st * hw, (st + 1) * hw)
    h0 = st * nh
    r = r_ref[0, :, ls]
    k = k_ref[0, :, ls]
    v = v_ref[0, :, ls]
    wl = w0_ref[:, ls] + _dot(hw_ref[...], w2_ref[:, ls])
    al = a0_ref[:, ls] + _dot(ha_ref[...], a2_ref[:, ls])
    a = _sigmoid(al)
    lw = (-DECAY_SCALE * LOG2E) * _sigmoid(wl)
    kk = k * kk_ref[:, ls]
    n2 = _seg_sum(kk * kk, ones_bd)
    kk = kk * lax.rsqrt(jnp.maximum(n2, NORM_FLOOR_SQ))
    ka = ka_ref[:, ls]
    k2 = k * (a * ka + (1.0 - ka))
    bb = kk * a

    lw_hi, lw_lo = _split_hi_lo(lw)
    g = _dot(tri_incl, lw_hi) + _dot(tri_incl, lw_lo)
    mid = lambda ci: g[ci * c + c // 2 - 1:ci * c + c // 2, :]
    gm = jnp.concatenate([jnp.broadcast_to(mid(ci), (c, hw)) for ci in range(nc)], axis=0)
    t = g - gm
    e_a = jnp.exp2(t)
    e_prev = jnp.exp2(t - lw)
    e_inv = jnp.exp2(-t)
    e1 = [jnp.exp2(mid(ci)) for ci in range(nc)]
    e2 = [jnp.exp2(g[ci * c + c - 1:ci * c + c, :] - mid(ci)) for ci in range(nc)]

    kkd = (kk * e_prev).astype(BF16)
    rd = (r * e_a).astype(BF16)
    bi = (bb * e_inv).astype(BF16)
    ki = (k2 * e_inv).astype(BF16)
    v_bf = v.astype(BF16)
    zeros_cv = jnp.zeros((c, hd), BF16)

    pairs = [(ci, h) for ci in range(nc) for h in range(nh)]
    rows = lambda ci: slice(ci * c, (ci + 1) * c)
    cols = lambda h: slice(h * hd, (h + 1) * hd)
    xs = {(ci, h): jnp.concatenate([kkd[rows(ci), cols(h)], rd[rows(ci), cols(h)]], axis=0) for ci, h in pairs}
    r1s = {(ci, h): jnp.concatenate([bi[rows(ci), cols(h)], ki[rows(ci), cols(h)]], axis=0) for ci, h in pairs}
    vs = {(ci, h): v_bf[rows(ci), cols(h)] for ci, h in pairs}
    a_mats = {p: _dot_nt(xs[p], r1s[p]) for p in pairs}
    lk_vs = {p: _dot(jnp.where(masks["top_k"], a_mats[p][:c, :], 0.0).astype(BF16),
                     jnp.concatenate([zeros_cv, vs[p]], axis=0)) for p in pairs}
    lps = {p: jnp.where(masks["strict"], a_mats[p][:c, :c], 0.0) for p in pairs}
    ts = {p: masks["eye"] - lps[p] for p in pairs}
    for _ in range(int(math.log2(c)) - 1):
        lpb = {p: lps[p].astype(BF16) for p in pairs}
        lps = {p: _dot(lpb[p], lpb[p]) for p in pairs}
        ts = {p: _dot(ts[p].astype(BF16), (masks["eye"] + lps[p]).astype(BF16)) for p in pairs}
    a_bots = {p: jnp.where(masks["bot"], a_mats[p][c:, :], 0.0).astype(BF16) for p in pairs}
    t_bf = {p: ts[p].astype(BF16) for p in pairs}

    state = [s_ref[h0 + h] for h in range(nh)]
    y_rows = []
    for ci in range(nc):
        hs = range(nh)
        sms = [state[h] * e1[ci][:, cols(h)] for h in hs]
        p_mats = [_dot_nt(xs[ci, h], sms[h].astype(BF16)) for h in hs]
        us = [-_dot(t_bf[ci, h], (p_mats[h][:c, :] + lk_vs[ci, h]).astype(BF16)) for h in hs]
        uvs = [jnp.concatenate([us[h].astype(BF16), vs[ci, h]], axis=0) for h in hs]
        ys = [p_mats[h][c:, :] + _dot(a_bots[ci, h], uvs[h]) for h in hs]
        state = [(sms[h] + _dot_tn(uvs[h], r1s[ci, h])) * e2[ci][:, cols(h)] for h in hs]
        y_rows.append(jnp.concatenate(ys, axis=1))
    for h in range(nh):
        s_ref[h0 + h] = state[h]
    y = jnp.concatenate(y_rows, axis=0) if nc > 1 else y_rows[0]

    mean = _seg_sum(y, masks["mean_bd"])
    yc = y - mean
    var = _seg_sum(yc * yc, masks["mean_bd"])
    yn = yc * lax.rsqrt(var + GN_EPS) * gg_ref[:, ls] + gb_ref[:, ls]
    bonus = _seg_sum(r * k2 * rk_ref[:, ls], ones_bd)
    yg_ref[:, ls] = ((yn + bonus * v) * _silu(z_ref[0, :, ls])).astype(BF16)


def wkv_prompt(rkvz, hw_act, ha_act, w2, w0, a2, a0, k_k, k_a, r_k, gn_g, gn_b, n_batch, cast_weights=()):
    _, m, e = rkvz.shape
    p = m // n_batch
    tb = WKV_ROWS
    nt = p // tb
    hw = HEADS_PER_STEP * HEAD_DIM
    nh = e // HEAD_DIM
    row = lambda i, g, t: (i * nt + t, g)
    proj = lambda q: pl.BlockSpec((1, tb, hw), lambda i, g, t: (q, i * nt + t, g))
    par = pl.BlockSpec((1, hw), lambda i, g, t: (0, g))
    lr = w2.shape[0]
    hid = pl.BlockSpec((tb, lr), lambda i, g, t: (i * nt + t, 0))
    up = pl.BlockSpec((lr, hw), lambda i, g, t: (0, g))
    n_groups = e // hw
    n_steps = n_batch * n_groups * nt
    n_chunks = min(CAST_CHUNKS, n_steps)
    every = n_steps // n_chunks
    chunk = lambda i, g, t: (jnp.minimum(((i * n_groups + g) * nt + t) // every, n_chunks - 1), 0)
    cast_specs = [pl.BlockSpec((w.shape[0] // n_chunks, w.shape[1]), chunk) for w in cast_weights]
    assert all(w.shape[0] % (n_chunks * 16) == 0 for w in cast_weights)
    return pl.pallas_call(
        functools.partial(_wkv_prompt_kernel, n_cast=len(cast_weights), cast_every=every),
        grid=(n_batch, n_groups, nt),
        in_specs=[proj(0), proj(1), proj(2), proj(3), hid, hid, up, par, up, par,
                  par, par, par, par, par] + cast_specs,
        out_specs=[pl.BlockSpec((tb, hw), row),
                   pl.BlockSpec((1, HEADS_PER_STEP, HEAD_DIM, HEAD_DIM), lambda i, g, t: (i, g, 0, 0))] + cast_specs,
        out_shape=[jax.ShapeDtypeStruct((m, e), BF16),
                   jax.ShapeDtypeStruct((n_batch, nh, HEAD_DIM, HEAD_DIM), F32)]
        + [jax.ShapeDtypeStruct(w.shape, BF16) for w in cast_weights],
        scratch_shapes=[pltpu.VMEM((HEADS_PER_STEP, HEAD_DIM, HEAD_DIM), F32)],
        compiler_params=_cparams(("arbitrary", "arbitrary", "arbitrary")),
        name="wkv_prompt",
    )(rkvz, rkvz, rkvz, rkvz, hw_act, ha_act, w2, w0.reshape(1, e), a2, a0.reshape(1, e),
      k_k.reshape(1, e), k_a.reshape(1, e), r_k.reshape(1, e), gn_g.reshape(1, e), gn_b.reshape(1, e),
      *cast_weights)


def _wkv_sample_kernel(r_ref, k_ref, v_ref, z_ref, wl_ref, al_ref, kk_ref, ka_ref, rk_ref, gg_ref, gb_ref,
                       s_ref, yg_ref, sout_ref, y_scr):
    hd = HEAD_DIM
    r = r_ref[0]
    k = k_ref[0]
    v = v_ref[0]
    a = _sigmoid(al_ref[0])
    d = jnp.exp2((-DECAY_SCALE * LOG2E) * _sigmoid(wl_ref[0]))
    kk = k * kk_ref[...]
    kk = kk / jnp.maximum(jnp.sqrt(jnp.sum(kk * kk, axis=-1, keepdims=True)), 1e-12)
    k2 = k * (1.0 + (a - 1.0) * ka_ref[...])
    bb = kk * a
    nh = r.shape[0]
    ii = lax.broadcasted_iota(jnp.int32, (hd, hd), 0)
    jj = lax.broadcasted_iota(jnp.int32, (hd, hd), 1)
    eye = ii == jj

    row = lambda x, h: x[h:h + 1, :]
    group = 8
    for h0 in range(0, nh, group):
        hs = range(h0, h0 + group)
        s = {h: s_ref[0, h] for h in hs}
        sa = {h: jnp.sum(s[h] * row(kk, h), axis=-1, keepdims=True) for h in hs}
        v_col = {h: jnp.sum(jnp.where(eye, row(v, h), 0.0), axis=-1, keepdims=True) for h in hs}
        s_new = {h: s[h] * row(d, h) - sa[h] * row(bb, h) + v_col[h] * row(k2, h) for h in hs}
        y_col = {h: jnp.sum(s_new[h] * row(r, h), axis=-1, keepdims=True) for h in hs}
        for h in hs:
            sout_ref[0, h] = s_new[h]
            y_scr[h:h + 1, :] = jnp.sum(jnp.where(eye, y_col[h], 0.0), axis=0, keepdims=True)

    y = y_scr[...]
    mean = jnp.mean(y, axis=-1, keepdims=True)
    yc = y - mean
    var = jnp.mean(yc * yc, axis=-1, keepdims=True)
    yn = yc * lax.rsqrt(var + GN_EPS) * gg_ref[...] + gb_ref[...]
    bonus = jnp.sum(r * k2 * rk_ref[...], axis=-1, keepdims=True)
    yg_ref[0] = ((yn + bonus * v) * _silu(z_ref[0])).astype(BF16)


def wkv_sample(rkvz, wl, al, k_k, k_a, r_k, gn_g, gn_b, state):
    _, m, e = rkvz.shape
    nh = e // HEAD_DIM
    hd = HEAD_DIM
    rkvz4 = rkvz.reshape(4, m, nh, hd)
    proj = lambda q: pl.BlockSpec((None, 1, nh, hd), lambda i: (q, i, 0, 0))
    tok = pl.BlockSpec((1, nh, hd), lambda i: (i, 0, 0))
    par = pl.BlockSpec((nh, hd), lambda i: (0, 0))
    st = pl.BlockSpec((1, nh, hd, hd), lambda i: (i, 0, 0, 0))
    as_heads = lambda x: x.reshape(nh, hd)
    return pl.pallas_call(
        _wkv_sample_kernel,
        grid=(m,),
        in_specs=[proj(0), proj(1), proj(2), proj(3), tok, tok, par, par, par, par, par, st],
        out_specs=[tok, st],
        out_shape=[jax.ShapeDtypeStruct((m, nh, hd), BF16), jax.ShapeDtypeStruct(state.shape, F32)],
        scratch_shapes=[pltpu.VMEM((nh, hd), F32)],
        compiler_params=_cparams(("arbitrary",)),
        name="wkv_sample",
    )(rkvz4, rkvz4, rkvz4, rkvz4, wl.reshape(m, nh, hd), al.reshape(m, nh, hd),
      as_heads(k_k), as_heads(k_a), as_heads(r_k), as_heads(gn_g), as_heads(gn_b), state)


def rope_tables(pos):
    half = ROPE_DIM // 2
    inv_freq = ROPE_THETA ** (-jnp.arange(half, dtype=F32) * 2.0 / ROPE_DIM)
    ang = pos.astype(F32)[:, None] * inv_freq[None, :]
    cos = jnp.cos(ang)
    sin = jnp.sin(ang)
    rows = pos.shape[0]
    ones = jnp.ones((rows, HEAD_DIM - ROPE_DIM), F32)
    zeros_h = jnp.zeros((rows, half), F32)
    zeros_r = jnp.zeros((rows, HEAD_DIM - ROPE_DIM), F32)
    cos_h = jnp.concatenate([cos, cos, ones], axis=1)
    sa_h = jnp.concatenate([-sin, zeros_h, zeros_r], axis=1)
    sb_h = jnp.concatenate([zeros_h, sin, zeros_r], axis=1)
    two = lambda t: jnp.concatenate([t, t], axis=1)
    return two(cos_h), two(sa_h), two(sb_h)


def _attn_prompt_kernel(sink_ref, q_ref, kc_ref, kp_ref, vc_ref, vp_ref, z_ref, o_ref, *, first_block):
    n = pl.program_id(1) + first_block
    hd = HEAD_DIM
    blk = q_ref.shape[0]
    n_kv = kc_ref.shape[1] // hd
    grp = q_ref.shape[1] // (n_kv * hd)
    qi = lax.broadcasted_iota(jnp.int32, (blk, 2 * blk), 0)
    kj = lax.broadcasted_iota(jnp.int32, (blk, 2 * blk), 1) - blk
    kpos = n * blk + kj
    diff = qi - kj
    valid = (kpos >= LEAD) & (diff >= 0) & (diff <= WINDOW)
    k_all = jnp.concatenate([kp_ref[...], kc_ref[...]], axis=0).astype(BF16)
    v_all = jnp.concatenate([vp_ref[...], vc_ref[...]], axis=0).astype(BF16)

    def scores(h):
        k_h = k_all[:, h * hd:(h + 1) * hd]
        return [_dot_nt(q_ref[:, (h * grp + gi) * hd:(h * grp + gi + 1) * hd], k_h) for gi in range(grp)]

    outs = []
    s_next = scores(0)
    for h in range(n_kv):
        s_cur = s_next
        if h + 1 < n_kv:
            s_next = scores(h + 1)
        v_h = v_all[:, h * hd:(h + 1) * hd]
        for g0 in range(0, grp, SOFTMAX_BATCH):
            gs = range(g0, min(g0 + SOFTMAX_BATCH, grp))
            sks = {gi: sink_ref[h * grp + gi] * LOG2E for gi in gs}
            ss = {gi: jnp.where(valid, s_cur[gi], -jnp.inf) for gi in gs}
            ms = {gi: jnp.maximum(jnp.max(ss[gi], axis=-1, keepdims=True), sks[gi]) for gi in gs}
            ps = {gi: jnp.exp2(ss[gi] - ms[gi]) for gi in gs}
            dens = {gi: jnp.sum(ps[gi], axis=-1, keepdims=True) + jnp.exp2(sks[gi] - ms[gi]) for gi in gs}
            outs += [_dot(ps[gi].astype(BF16), v_h) / dens[gi] for gi in gs]
    att = jnp.concatenate(outs, axis=1)
    o_ref[...] = (att * _silu(z_ref[...])).astype(o_ref.dtype)


def attn_prompt(sinks, q, k, v, z, n_batch, first_block):
    m, e = q.shape
    nb = m // (n_batch * BLOCK)
    nb_out = nb - first_block
    kw = k.shape[1]
    cur = lambda i, n: (i * nb + n + first_block, 0)
    prv = lambda i, n: (i * nb + jnp.maximum(n + first_block - 1, 0), 0)
    return pl.pallas_call(
        functools.partial(_attn_prompt_kernel, first_block=first_block),
        grid=(n_batch, nb_out),
        in_specs=[pl.BlockSpec(memory_space=pltpu.SMEM),
                  pl.BlockSpec((BLOCK, e), cur),
                  pl.BlockSpec((BLOCK, kw), cur), pl.BlockSpec((BLOCK, kw), prv),
                  pl.BlockSpec((BLOCK, kw), cur), pl.BlockSpec((BLOCK, kw), prv),
                  pl.BlockSpec((BLOCK, e), cur)],
        out_specs=pl.BlockSpec((BLOCK, e), lambda i, n: (i * nb_out + n, 0)),
        out_shape=jax.ShapeDtypeStruct((n_batch * nb_out * BLOCK, e), BF16),
        compiler_params=_cparams(("arbitrary", "arbitrary")),
        name="attn_prompt",
    )(sinks, q, k, k, v, v, z)


def _attn_sample_kernel(sink_ref, q_ref, kc_ref, vc_ref, kn_ref, vn_ref, z_ref, o_ref, ko_ref, vo_ref):
    hd = HEAD_DIM
    win = kc_ref.shape[1]
    n_kv = kc_ref.shape[2] // hd
    nq = q_ref.shape[1]
    grp = nq // n_kv
    pad = 8
    kc = kc_ref[0]
    vc = vc_ref[0]
    kn = kn_ref[0]
    vn = vn_ref[0]
    first = lax.broadcasted_iota(jnp.int32, (pad, kc.shape[1]), 0) == 0
    k_all = jnp.concatenate([kc, jnp.where(first, kn, 0.0)], axis=0).astype(BF16)
    v_all = jnp.concatenate([vc, jnp.where(first, vn, 0.0)], axis=0).astype(BF16)
    col = lax.broadcasted_iota(jnp.int32, (grp, win + pad), 1)
    valid = (col <= win) & (win - col <= WINDOW)
    q = q_ref[0].astype(BF16)
    row_i = lax.broadcasted_iota(jnp.int32, (grp, 1), 0)
    hs = range(n_kv)
    sks = []
    for h in hs:
        sk = jnp.zeros((grp, 1), F32)
        for gi in range(grp):
            sk = jnp.where(row_i == gi, sink_ref[h * grp + gi] * LOG2E, sk)
        sks.append(sk)
    ss = [jnp.where(valid, _dot_nt(q[h * grp:(h + 1) * grp, :], k_all[:, h * hd:(h + 1) * hd]), -jnp.inf)
          for h in hs]
    ms = [jnp.maximum(jnp.max(ss[h], axis=-1, keepdims=True), sks[h]) for h in hs]
    ps = [jnp.exp2(ss[h] - ms[h]) for h in hs]
    dens = [jnp.sum(ps[h], axis=-1, keepdims=True) + jnp.exp2(sks[h] - ms[h]) for h in hs]
    outs = [_dot(ps[h].astype(BF16), v_all[:, h * hd:(h + 1) * hd]) / dens[h] for h in hs]
    att = jnp.concatenate(outs, axis=0)
    o_ref[0] = (att * _silu(z_ref[0])).astype(o_ref.dtype)
    last = lax.broadcasted_iota(jnp.int32, kc.shape, 0) == win - 1
    ko_ref[0] = jnp.where(last, kn, pltpu.roll(kc, win - 1, axis=0))
    vo_ref[0] = jnp.where(last, vn, pltpu.roll(vc, win - 1, axis=0))


def attn_sample(sinks, q, cache_k, cache_v, k_new, v_new, z):
    m, win, kw = cache_k.shape
    nq = q.shape[1]
    hd = HEAD_DIM
    tok = pl.BlockSpec((1, nq, hd), lambda i: (i, 0, 0))
    cache = pl.BlockSpec((1, win, kw), lambda i: (i, 0, 0))
    new = pl.BlockSpec((1, 1, kw), lambda i: (i, 0, 0))
    return pl.pallas_call(
        _attn_sample_kernel,
        grid=(m,),
        in_specs=[pl.BlockSpec(memory_space=pltpu.SMEM), tok, cache, cache, new, new, tok],
        out_specs=[tok, cache, cache],
        out_shape=[jax.ShapeDtypeStruct((m, nq, hd), BF16),
                   jax.ShapeDtypeStruct(cache_k.shape, F32), jax.ShapeDtypeStruct(cache_v.shape, F32)],
        compiler_params=_cparams(("arbitrary",)),
        name="attn_sample",
    )(sinks, q, cache_k, cache_v, k_new, v_new, z)


def _pad_lora(w_down, w_up):
    r = w_down.shape[1]
    return (jnp.pad(w_down, ((0, 0), (0, LORA_PAD - r))).astype(BF16),
            jnp.pad(w_up, ((0, LORA_PAD - r), (0, 0))).astype(BF16))


def kernel(x_prompt, x_sample, state_wkv, state_shift, cache_k, cache_v, meta_tokens, a_norm, a_mu, a_w_rkvz,
           a_w0, a_w1, a_w2, a_a0, a_a1, a_a2, a_k_k, a_k_a, a_r_k, a_gn_g, a_gn_b, a_w_out, kv_norm, w_kv,
           b_norm, b_w_qz, b_sinks, b_w_o, final_norm):
    nb, seq, d = x_prompt.shape
    db, dseq, _ = x_sample.shape
    assert dseq == 1 and a_norm.shape[0] == 1 and b_norm.shape[0] == 1
    e = a_w_rkvz.shape[3]
    win = cache_k.shape[1]
    p_len = LEAD + N_META + seq
    assert p_len % BLOCK == 0 and (LEAD + N_META) == BLOCK
    kvw = N_KV_HEADS * HEAD_DIM

    w_rkvz = a_w_rkvz[0].astype(BF16)
    w1, w2 = _pad_lora(a_w1[0], a_w2[0])
    a1, a2 = _pad_lora(a_a1[0], a_a2[0])
    mu = a_mu[0]
    sinks = b_sinks[0]
    gains_b = jnp.stack([kv_norm, b_norm[0]])

    tm = p_len // 8

    head = jnp.concatenate([jnp.zeros((LEAD, d), F32), meta_tokens], axis=0)
    xm, hw_p, ha_p, x_last = norm_shift_prompt(x_prompt, head, a_norm[0], mu, w1, a1)
    p_state_shift = x_last.reshape(1, nb, d)
    rkvz = matmul_groups(xm, w_rkvz, tm, F32)
    yg, p_state, w_out, w_kv_bf, w_qz, w_o = wkv_prompt(
        rkvz, hw_p, ha_p, w2, a_w0[0], a2, a_a0[0], a_k_k[0], a_k_a[0], a_r_k[0].reshape(-1), a_gn_g[0],
        a_gn_b[0], nb, cast_weights=(a_w_out[0], w_kv, b_w_qz[0], b_w_o[0]))
    w_qz = w_qz[None]
    hp, hn_kv, hn_b = matmul_residual_norm_blocks(yg, w_out, x_prompt, gains_b, nb, BF16, True, head=head)

    pos_p = jnp.maximum(jnp.arange(p_len, dtype=jnp.int32) - LEAD, 0)
    tabs_p = tuple(jnp.tile(t, (nb, 1)) for t in rope_tables(pos_p))
    k_p, v_p = matmul_rope(hn_kv, w_kv_bf, tabs_p, tm, kvw, (F32, F32))
    q_p, = matmul_rope(hn_b, w_qz, tabs_p, tm, e, (BF16,), scale=Q_SCALE, n=e)
    z_p = matmul_groups(hn_b[None], w_qz, tm, F32, n=e, col=1)[0]
    skip = (LEAD + N_META) // BLOCK
    att = attn_prompt(sinks, q_p, k_p, v_p, z_p, nb, skip)
    y_prompt, = matmul_residual_norm_blocks(att, w_o, hp, final_norm[None], nb, F32, False, first_block=skip)
    y_prompt = y_prompt.reshape(nb, seq, d)
    tail = lambda t: t.reshape(nb, p_len, kvw)[:, -win:].reshape(nb, win, N_KV_HEADS, HEAD_DIM)
    p_cache_k = tail(k_p)
    p_cache_v = tail(v_p)

    hs = x_sample.reshape(db, d)
    xm_s, hw_s, ha_s, xn_s = norm_shift_sample(hs, state_shift[0], a_norm[0], mu, w1, a1)
    rkvz_s = matmul_groups(xm_s, w_rkvz, db, F32)
    wl_s, al_s = lora_up(hw_s, ha_s, w2, a_w0[0], a2, a_a0[0])
    yg_s, s_state = wkv_sample(rkvz_s, wl_s, al_s, a_k_k[0], a_k_a[0], a_r_k[0].reshape(-1), a_gn_g[0],
                               a_gn_b[0], state_wkv[0])
    hs, hn_kv_s, hn_b_s = matmul_residual_norm(yg_s.reshape(db, e), w_out, hs, gains_b, db, BF16)
    tabs_s = rope_tables(jnp.full((db,), PAST_LEN, jnp.int32))
    k_s, v_s = matmul_rope(hn_kv_s, w_kv_bf, tabs_s, db, kvw, (F32, F32))
    q_s, = matmul_rope(hn_b_s, w_qz, tabs_s, db, e, (F32,), scale=Q_SCALE, n=e)
    z_s = matmul_groups(hn_b_s[None], w_qz, db, F32, n=e, col=1)[0]
    nq = e // HEAD_DIM
    att_s, s_cache_k, s_cache_v = attn_sample(
        sinks, q_s.reshape(db, nq, HEAD_DIM), cache_k.reshape(db, win, kvw), cache_v.reshape(db, win, kvw),
        k_s.reshape(db, 1, kvw), v_s.reshape(db, 1, kvw), z_s.reshape(db, nq, HEAD_DIM))
    y_s, = matmul_residual_norm(att_s.reshape(db, e), w_o, hs, final_norm[None], db, F32, emit_h=False)
    y_sample = y_s.reshape(db, 1, d)

    return (y_prompt, y_sample, p_state[None], p_state_shift,
            p_cache_k, p_cache_v,
            s_state[None], xn_s[None],
            s_cache_k.reshape(cache_k.shape), s_cache_v.reshape(cache_v.shape))
```

```python
import functools
import math

import jax
import jax.numpy as jnp
from jax import lax
from jax.experimental import pallas as pl
from jax.experimental.pallas import tpu as pltpu

F32 = jnp.float32
BF16 = jnp.bfloat16

HEAD_DIM = 64
N_KV_HEADS = 8
WINDOW = 128
BLOCK = 128
ROPE_DIM = HEAD_DIM // 4
ROPE_THETA = 500000.0
N_META = 16
PAST_LEN = 16384
RMS_EPS = 1e-6
GN_EPS = 64e-5
NORM_FLOOR_SQ = 1e-24
LEAD = (-N_META) % BLOCK
CHUNK = 64
WKV_ROWS = 128
CAST_CHUNKS = 32
SEQS_PER_STEP = 2
HEADS_PER_STREAM = 32
HEADS_PER_STEP = 32
LORA_PAD = 128
MXU_TILE = 256
ROPE_SLAB = 512
VMEM_LIMIT = 48 * 1024 * 1024
LOG2E = 1.0 / math.log(2.0)
DECAY_SCALE = math.exp(-0.5)
SOFTMAX_BATCH = 4
Q_SCALE = HEAD_DIM ** -0.5 * LOG2E


def _cparams(sem):
    return pltpu.CompilerParams(dimension_semantics=sem, vmem_limit_bytes=VMEM_LIMIT)


def _sigmoid(x):
    return 1.0 / (1.0 + jnp.exp2(x * (-LOG2E)))


def _silu(x):
    return x * _sigmoid(x)


def _dot(a, b):
    return jnp.dot(a, b, preferred_element_type=F32)


def _dot_nt(a, b):
    return lax.dot_general(a, b, (((1,), (1,)), ((), ())), preferred_element_type=F32)


def _dot_tn(a, b):
    return lax.dot_general(a, b, (((0,), (0,)), ((), ())), preferred_element_type=F32)


def _split_hi_lo(x):
    hi = x.astype(BF16)
    lo = (x - hi.astype(F32)).astype(BF16)
    return hi, lo


def _cast_job(weights, n_steps, step_of):
    n_chunks = min(CAST_CHUNKS, n_steps)
    every = n_steps // n_chunks
    assert all(w.shape[0] % (n_chunks * 16) == 0 for w in weights)
    chunk = lambda *idx: (jnp.minimum(step_of(*idx) // every, n_chunks - 1), 0)
    return [pl.BlockSpec((w.shape[0] // n_chunks, w.shape[1]), chunk) for w in weights], every


def _cast_chunks(step, every, srcs, dsts):
    @pl.when(step % every == 0)
    def _():
        for src, dst in zip(srcs, dsts):
            dst[...] = src[...].astype(dst.dtype)


def _mixes(xn, prev, mu_ref, w1_ref, a1_ref, xm_ref, hw_ref, ha_ref):
    xx = prev - xn
    n_proj = xm_ref.shape[0]
    for p in range(n_proj):
        xm_ref[p] = (xn + xx * mu_ref[p:p + 1, :]).astype(xm_ref.dtype)
    xw = (xn + xx * mu_ref[n_proj:n_proj + 1, :]).astype(BF16)
    xa = (xn + xx * mu_ref[n_proj + 1:n_proj + 2, :]).astype(BF16)
    hw_ref[...] = jnp.tanh(_dot(xw, w1_ref[...])).astype(hw_ref.dtype)
    ha_ref[...] = _dot(xa, a1_ref[...]).astype(ha_ref.dtype)


def _norm_shift_kernel(x_ref, head_ref, g_ref, mu_ref, w1_ref, a1_ref, *rest, n_cast, cast_every):
    cast_in = rest[:n_cast]
    xm_ref, hw_ref, ha_ref, last_ref = rest[n_cast:n_cast + 4]
    cast_out = rest[n_cast + 4:2 * n_cast + 4]
    carry_ref = rest[-1]

    @pl.when(pl.program_id(1) == 0)
    def _():
        carry_ref[...] = jnp.zeros_like(carry_ref)

    _cast_chunks(pl.program_id(0) * pl.num_programs(1) + pl.program_id(1), cast_every, cast_in, cast_out)

    is_head = pl.program_id(1) == 0
    for sq in range(x_ref.shape[0]):
        x = jnp.where(is_head, head_ref[...], x_ref[sq])
        tm = x.shape[0]
        xn = x * lax.rsqrt(jnp.mean(x * x, axis=-1, keepdims=True) + RMS_EPS) * g_ref[...]
        rolled = pltpu.roll(xn, 1, axis=0)
        row = lax.broadcasted_iota(jnp.int32, xn.shape, 0)
        prev = jnp.where(row == 0, carry_ref[sq, 0:1, :], rolled)
        _mixes(xn, prev, mu_ref, w1_ref, a1_ref, xm_ref.at[:, sq], hw_ref.at[sq], ha_ref.at[sq])
        carry_ref[sq, 0:1, :] = xn[tm - 1:tm, :]
        last_ref[sq] = xn[tm - 1:tm, :]


def norm_shift_prompt(x, head, g, mu, w1, a1, cast_weights=()):
    b, seq, d = x.shape
    tm = BLOCK
    p = tm + seq
    n_mix = mu.shape[0]
    n_proj = n_mix - 2
    lr = w1.shape[1]
    nt = p // tm
    sp = SEQS_PER_STEP
    const = lambda shape: pl.BlockSpec(shape, lambda i, t: (0,) * len(shape))
    hid = pl.BlockSpec((sp, tm, lr), lambda i, t: (i, t, 0))
    cast_specs, every = _cast_job(cast_weights, (b // sp) * nt, lambda i, t: i * nt + t)
    xm, hw_act, ha_act, last, *casts = pl.pallas_call(
        functools.partial(_norm_shift_kernel, n_cast=len(cast_weights), cast_every=every),
        grid=(b // sp, nt),
        in_specs=[pl.BlockSpec((sp, tm, d), lambda i, t: (i, jnp.maximum(t - 1, 0), 0)),
                  const((tm, d)), const((1, d)), const((n_mix, d)), const((d, lr)), const((d, lr))] + cast_specs,
        out_specs=[pl.BlockSpec((n_proj, sp, tm, d), lambda i, t: (0, i, t, 0)), hid, hid,
                   pl.BlockSpec((sp, 1, d), lambda i, t: (i, 0, 0))] + cast_specs,
        out_shape=[jax.ShapeDtypeStruct((n_proj, b, p, d), BF16),
                   jax.ShapeDtypeStruct((b, p, lr), BF16), jax.ShapeDtypeStruct((b, p, lr), BF16),
                   jax.ShapeDtypeStruct((b, 1, d), F32)] + [jax.ShapeDtypeStruct(w.shape, BF16) for w in cast_weights],
        scratch_shapes=[pltpu.VMEM((sp, 8, d), F32)],
        compiler_params=_cparams(("arbitrary", "arbitrary")),
        name="norm_shift_prompt",
    )(x, head, g.reshape(1, d), mu, w1, a1, *cast_weights)
    return (xm.reshape(n_proj, b * p, d), hw_act.reshape(b * p, lr), ha_act.reshape(b * p, lr), last, *casts)


def _norm_shift_sample_kernel(x_ref, prev_ref, g_ref, mu_ref, w1_ref, a1_ref, xm_ref, hw_ref, ha_ref, xn_ref):
    x = x_ref[...]
    xn = x * lax.rsqrt(jnp.mean(x * x, axis=-1, keepdims=True) + RMS_EPS) * g_ref[...]
    xn_ref[...] = xn
    _mixes(xn, prev_ref[...], mu_ref, w1_ref, a1_ref, xm_ref, hw_ref, ha_ref)


def norm_shift_sample(x, prev, g, mu, w1, a1):
    m, d = x.shape
    lr = w1.shape[1]
    return pl.pallas_call(
        _norm_shift_sample_kernel,
        out_shape=[jax.ShapeDtypeStruct((mu.shape[0] - 2, m, d), BF16),
                   jax.ShapeDtypeStruct((m, lr), BF16), jax.ShapeDtypeStruct((m, lr), BF16),
                   jax.ShapeDtypeStruct((m, d), F32)],
        name="norm_shift_sample",
    )(x, prev, g.reshape(1, d), mu, w1, a1)


def _rope(y, cos, sin_a, sin_b):
    half = ROPE_DIM // 2
    step = ROPE_SLAB
    rep = step // cos.shape[1]
    tile = lambda t: jnp.concatenate([t] * rep, axis=1)
    cos_t, sa_t, sb_t = tile(cos), tile(sin_a), tile(sin_b)
    outs = []
    for j in range(y.shape[1] // step):
        ys = y[:, j * step:(j + 1) * step]
        outs.append(ys * cos_t + pltpu.roll(ys, step - half, axis=1) * sa_t + pltpu.roll(ys, half, axis=1) * sb_t)
    return jnp.concatenate(outs, axis=1) if len(outs) > 1 else outs[0]


def _mm_group_kernel(x_ref, w_ref, o_ref):
    o_ref[0] = _dot(x_ref[0], w_ref[0]).astype(o_ref.dtype)


def matmul_groups(x, w, tm, out_dtype, n=None, col=0):
    g = w.shape[0]
    n = w.shape[2] if n is None else n
    _, m, kdim = x.shape
    return pl.pallas_call(
        _mm_group_kernel,
        grid=(g, m // tm),
        in_specs=[pl.BlockSpec((1, tm, kdim), lambda q, i: (q, i, 0)),
                  pl.BlockSpec((1, kdim, n), lambda q, i: (q, 0, col))],
        out_specs=pl.BlockSpec((1, tm, n), lambda q, i: (q, i, 0)),
        out_shape=jax.ShapeDtypeStruct((g, m, n), out_dtype),
        compiler_params=_cparams(("arbitrary", "arbitrary")),
        name="matmul_groups",
    )(x, w)


def _mm_rope_kernel(x_ref, w_ref, cos_ref, sa_ref, sb_ref, *o_refs, n_rope, scale):
    y = _dot(x_ref[...], w_ref[...])
    rot = _rope(y[:, :n_rope], cos_ref[...], sa_ref[...], sb_ref[...])
    if scale != 1.0:
        rot = rot * scale
    o_refs[0][...] = rot.astype(o_refs[0].dtype)
    if len(o_refs) > 1:
        o_refs[1][...] = y[:, n_rope:].astype(o_refs[1].dtype)


def matmul_rope(x, w, tables, tm, n_rope, out_dtypes, scale=1.0, n=None):
    m, kdim = x.shape
    n = w.shape[-1] if n is None else n
    w_block = (kdim, n) if w.ndim == 2 else (None, kdim, n)
    lanes = tables[0].shape[1]
    widths = [n_rope] + ([n - n_rope] if n > n_rope else [])
    tab = pl.BlockSpec((tm, lanes), lambda i: (i, 0))
    outs = pl.pallas_call(
        functools.partial(_mm_rope_kernel, n_rope=n_rope, scale=scale),
        grid=(m // tm,),
        in_specs=[pl.BlockSpec((tm, kdim), lambda i: (i, 0)),
                  pl.BlockSpec(w_block, lambda i: (0,) * w.ndim, pipeline_mode=pl.Buffered(1)),
                  tab, tab, tab],
        out_specs=[pl.BlockSpec((tm, wd), lambda i: (i, 0)) for wd in widths],
        out_shape=[jax.ShapeDtypeStruct((m, wd), dt) for wd, dt in zip(widths, out_dtypes)],
        compiler_params=_cparams(("arbitrary",)),
        name="matmul_rope",
    )(x, w, *tables)
    return outs


def _mm_res_norm_kernel(x_ref, w_ref, res_ref, g_ref, *out_refs, emit_h):
    h = res_ref[...] + _dot(x_ref[...], w_ref[...])
    hn_refs = out_refs
    if emit_h:
        out_refs[0][...] = h
        hn_refs = out_refs[1:]
    inv = lax.rsqrt(jnp.mean(h * h, axis=-1, keepdims=True) + RMS_EPS)
    for j, hn_ref in enumerate(hn_refs):
        hn_ref[...] = (h * inv * g_ref[j:j + 1, :]).astype(hn_ref.dtype)


def matmul_residual_norm(x, w, res, gains, tm, norm_dtype, emit_h=True):
    m, kdim = x.shape
    n = w.shape[1]
    ng = gains.shape[0]
    row = lambda width: pl.BlockSpec((tm, width), lambda i: (i, 0))
    return pl.pallas_call(
        functools.partial(_mm_res_norm_kernel, emit_h=emit_h),
        grid=(m // tm,),
        in_specs=[row(kdim),
                  pl.BlockSpec((kdim, n), lambda i: (0, 0), pipeline_mode=pl.Buffered(1)),
                  row(n),
                  pl.BlockSpec((ng, n), lambda i: (0, 0))],
        out_specs=[row(n)] * (int(emit_h) + ng),
        out_shape=[jax.ShapeDtypeStruct((m, n), F32)] * int(emit_h) + [jax.ShapeDtypeStruct((m, n), norm_dtype)] * ng,
        compiler_params=_cparams(("arbitrary",)),
        name="matmul_residual_norm",
    )(x, w, res, gains)


def _mm_res_norm_blocks_kernel(x_ref, w_ref, res_ref, *rest, emit_h, head, first_block):
    if head:
        head_ref, g_ref, *out_refs = rest
        res = jnp.where(pl.program_id(1) + first_block == 0, head_ref[...][None], res_ref[...])
    else:
        g_ref, *out_refs = rest
        res = res_ref[...]
    seqs, blk, n = res.shape
    h = res.reshape(seqs * blk, n) + _dot(x_ref[...].reshape(seqs * blk, x_ref.shape[2]), w_ref[...])
    hn_refs = out_refs
    if emit_h:
        out_refs[0][...] = h.reshape(seqs, blk, n)
        hn_refs = out_refs[1:]
    inv = lax.rsqrt(jnp.mean(h * h, axis=-1, keepdims=True) + RMS_EPS)
    for j, hn_ref in enumerate(hn_refs):
        hn_ref[...] = (h * inv * g_ref[j:j + 1, :]).astype(hn_ref.dtype).reshape(seqs, blk, n)


def matmul_residual_norm_blocks(x, w, res, gains, n_batch, norm_dtype, emit_h, head=None, first_block=0):
    kdim = x.shape[1]
    n = w.shape[1]
    ng = gains.shape[0]
    nb_out = x.shape[0] // (n_batch * BLOCK)
    nb = nb_out + first_block
    sp = SEQS_PER_STEP
    blocks = lambda width, shift: pl.BlockSpec((sp, BLOCK, width), lambda i, t: (i, t + shift, 0))
    const = lambda shape: pl.BlockSpec(shape, lambda i, t: (0,) * len(shape))
    if head is not None:
        res_specs = [pl.BlockSpec((sp, BLOCK, n), lambda i, t: (i, jnp.maximum(t + first_block - 1, 0), 0)),
                     const((BLOCK, n))]
        res_args = [res, head]
    else:
        res_specs, res_args = [blocks(n, first_block)], [res.reshape(n_batch, nb * BLOCK, n)]
    outs = pl.pallas_call(
        functools.partial(_mm_res_norm_blocks_kernel, emit_h=emit_h, head=head is not None, first_block=first_block),
        grid=(n_batch // sp, nb_out),
        in_specs=[blocks(kdim, 0), pl.BlockSpec((kdim, n), lambda i, t: (0, 0), pipeline_mode=pl.Buffered(1))]
        + res_specs + [const((ng, n))],
        out_specs=[blocks(n, first_block)] * int(emit_h) + [blocks(n, 0)] * ng,
        out_shape=[jax.ShapeDtypeStruct((n_batch, nb * BLOCK, n), F32)] * int(emit_h)
        + [jax.ShapeDtypeStruct((n_batch, nb_out * BLOCK, n), norm_dtype)] * ng,
        compiler_params=_cparams(("arbitrary", "arbitrary")),
        name="matmul_residual_norm_blocks",
    )(x.reshape(n_batch, nb_out * BLOCK, kdim), w, *res_args, gains)
    return [o.reshape(-1, n) for o in outs]


def _lora_up_kernel(hw_ref, ha_ref, w2_ref, w0_ref, a2_ref, a0_ref, wl_ref, al_ref):
    wl_ref[...] = w0_ref[...] + _dot(hw_ref[...], w2_ref[...])
    al_ref[...] = a0_ref[...] + _dot(ha_ref[...], a2_ref[...])


def lora_up(hw, ha, w2, w0, a2, a0):
    m = hw.shape[0]
    e = w2.shape[1]
    return pl.pallas_call(
        _lora_up_kernel,
        out_shape=[jax.ShapeDtypeStruct((m, e), F32), jax.ShapeDtypeStruct((m, e), F32)],
        name="lora_up",
    )(hw, ha, w2, w0.reshape(1, e), a2, a0.reshape(1, e))


def _seg_sum(x, ones_bd):
    hi = x.astype(BF16)
    outs = []
    for c in range(x.shape[1] // MXU_TILE):
        sl = slice(c * MXU_TILE, (c + 1) * MXU_TILE)
        outs.append(_dot(hi[:, sl], ones_bd))
    return jnp.concatenate(outs, axis=1) if len(outs) > 1 else outs[0]


def _wkv_prompt_kernel(r_ref, k_ref, v_ref, z_ref, hw_ref, ha_ref, w2_ref, w0_ref, a2_ref, a0_ref,
                       kk_ref, ka_ref, rk_ref, gg_ref, gb_ref, *rest, n_cast, cast_every):
    cast_in = rest[:n_cast]
    yg_ref, sout_ref = rest[n_cast:n_cast + 2]
    cast_out = rest[n_cast + 2:2 * n_cast + 2]
    s_ref = rest[-1]
    t_idx = pl.program_id(2)
    c = CHUNK
    hd = HEAD_DIM

    @pl.when(t_idx == 0)
    def _():
        s_ref[...] = jnp.zeros_like(s_ref)

    step = (pl.program_id(0) * pl.num_programs(1) + pl.program_id(1)) * pl.num_programs(2) + t_idx
    _cast_chunks(step, cast_every, cast_in, cast_out)

    tb = r_ref.shape[1]
    nh = HEADS_PER_STREAM
    hw = nh * hd
    nc = tb // c

    li = lax.broadcasted_iota(jnp.int32, (MXU_TILE, MXU_TILE), 0) // hd
    lj = lax.broadcasted_iota(jnp.int32, (MXU_TILE, MXU_TILE), 1) // hd
    ones_bd = jnp.where(li == lj, 1.0, 0.0).astype(BF16)
    bi_ = lax.broadcasted_iota(jnp.int32, (tb, tb), 0)
    bj_ = lax.broadcasted_iota(jnp.int32, (tb, tb), 1)
    tri_incl = jnp.where((bj_ <= bi_) & (bj_ // c == bi_ // c), 1.0, 0.0).astype(BF16)
    ti = lax.broadcasted_iota(jnp.int32, (c, c), 0)
    tj = lax.broadcasted_iota(jnp.int32, (c, c), 1)
    ai = lax.broadcasted_iota(jnp.int32, (c, 2 * c), 0)
    aj = lax.broadcasted_iota(jnp.int32, (c, 2 * c), 1)
    upper = aj >= c
    aj_mod = jnp.where(upper, aj - c, aj)
    masks = dict(
        strict=tj < ti,
        eye=jnp.where(ti == tj, 1.0, 0.0).astype(F32),
        top_k=upper & (aj_mod < ai),
        bot=aj_mod <= ai,
        mean_bd=jnp.where(li == lj, 1.0 / hd, 0.0).astype(BF16))

    for st in range(r_ref.shape[2] // hw):
        _wkv_stream(st, hw, nc, r_ref, k_ref, v_ref, z_ref, hw_ref, ha_ref, w2_ref, w0_ref, a2_ref, a0_ref,
                    kk_ref, ka_ref, rk_ref, gg_ref, gb_ref, yg_ref, s_ref, ones_bd, tri_incl, masks)

    @pl.when(t_idx == pl.num_programs(2) - 1)
    def _():
        sout_ref[0] = s_ref[...]


def _wkv_stream(st, hw, nc, r_ref, k_ref, v_ref, z_ref, hw_ref, ha_ref, w2_ref, w0_ref, a2_ref, a0_ref,
                kk_ref, ka_ref, rk_ref, gg_ref, gb_ref, yg_ref, s_ref, ones_bd, tri_incl, masks):
    c = CHUNK
    hd = HEAD_DIM
    nh = hw // hd
    ls = slice(st * hw, (st + 1) * hw)
    h0 = st * nh
    r = r_ref[0, :, ls]
    k = k_ref[0, :, ls]
    v = v_ref[0, :, ls]
    wl = w0_ref[:, ls] + _dot(hw_ref[...], w2_ref[:, ls])
    al = a0_ref[:, ls] + _dot(ha_ref[...], a2_ref[:, ls])
    a = _sigmoid(al)
    lw = (-DECAY_SCALE * LOG2E) * _sigmoid(wl)
    kk = k * kk_ref[:, ls]
    n2 = _seg_sum(kk * kk, ones_bd)
    kk = kk * lax.rsqrt(jnp.maximum(n2, NORM_FLOOR_SQ))
    ka = ka_ref[:, ls]
    k2 = k * (a * ka + (1.0 - ka))
    bb = kk * a

    lw_hi, lw_lo = _split_hi_lo(lw)
    g = _dot(tri_incl, lw_hi) + _dot(tri_incl, lw_lo)
    mid = lambda ci: g[ci * c + c // 2 - 1:ci * c + c // 2, :]
    gm = jnp.concatenate([jnp.broadcast_to(mid(ci), (c, hw)) for ci in range(nc)], axis=0)
    t = g - gm
    e_a = jnp.exp2(t)
    e_prev = jnp.exp2(t - lw)
    e_inv = jnp.exp2(-t)
    e1 = [jnp.exp2(mid(ci)) for ci in range(nc)]
    e2 = [jnp.exp2(g[ci * c + c - 1:ci * c + c, :] - mid(ci)) for ci in range(nc)]

    kkd = (kk * e_prev).astype(BF16)
    rd = (r * e_a).astype(BF16)
    bi = (bb * e_inv).astype(BF16)
    ki = (k2 * e_inv).astype(BF16)
    v_bf = v.astype(BF16)
    zeros_cv = jnp.zeros((c, hd), BF16)

    pairs = [(ci, h) for ci in range(nc) for h in range(nh)]
    rows = lambda ci: slice(ci * c, (ci + 1) * c)
    cols = lambda h: slice(h * hd, (h + 1) * hd)
    xs = {(ci, h): jnp.concatenate([kkd[rows(ci), cols(h)], rd[rows(ci), cols(h)]], axis=0) for ci, h in pairs}
    r1s = {(ci, h): jnp.concatenate([bi[rows(ci), cols(h)], ki[rows(ci), cols(h)]], axis=0) for ci, h in pairs}
    vs = {(ci, h): v_bf[rows(ci), cols(h)] for ci, h in pairs}
    a_mats = {p: _dot_nt(xs[p], r1s[p]) for p in pairs}
    lk_vs = {p: _dot(jnp.where(masks["top_k"], a_mats[p][:c, :], 0.0).astype(BF16),
                     jnp.concatenate([zeros_cv, vs[p]], axis=0)) for p in pairs}
    lps = {p: jnp.where(masks["strict"], a_mats[p][:c, :c], 0.0) for p in pairs}
    ts = {p: masks["eye"] - lps[p] for p in pairs}
    for _ in range(int(math.log2(c)) - 1):
        lpb = {p: lps[p].astype(BF16) for p in pairs}
        lps = {p: _dot(lpb[p], lpb[p]) for p in pairs}
        ts = {p: _dot(ts[p].astype(BF16), (masks["eye"] + lps[p]).astype(BF16)) for p in pairs}
    a_bots = {p: jnp.where(masks["bot"], a_mats[p][c:, :], 0.0).astype(BF16) for p in pairs}
    t_bf = {p: ts[p].astype(BF16) for p in pairs}

    state = [s_ref[h0 + h] for h in range(nh)]
    y_rows = []
    for ci in range(nc):
        hs = range(nh)
        sms = [state[h] * e1[ci][:, cols(h)] for h in hs]
        p_mats = [_dot_nt(xs[ci, h], sms[h].astype(BF16)) for h in hs]
        us = [-_dot(t_bf[ci, h], (p_mats[h][:c, :] + lk_vs[ci, h]).astype(BF16)) for h in hs]
        uvs = [jnp.concatenate([us[h].astype(BF16), vs[ci, h]], axis=0) for h in hs]
        ys = [p_mats[h][c:, :] + _dot(a_bots[ci, h], uvs[h]) for h in hs]
        state = [(sms[h] + _dot_tn(uvs[h], r1s[ci, h])) * e2[ci][:, cols(h)] for h in hs]
        y_rows.append(jnp.concatenate(ys, axis=1))
    for h in range(nh):
        s_ref[h0 + h] = state[h]
    y = jnp.concatenate(y_rows, axis=0) if nc > 1 else y_rows[0]

    mean = _seg_sum(y, masks["mean_bd"])
    yc = y - mean
    var = _seg_sum(yc * yc, masks["mean_bd"])
    yn = yc * lax.rsqrt(var + GN_EPS) * gg_ref[:, ls] + gb_ref[:, ls]
    bonus = _seg_sum(r * k2 * rk_ref[:, ls], ones_bd)
    yg_ref[:, ls] = ((yn + bonus * v) * _silu(z_ref[0, :, ls])).astype(BF16)


def wkv_prompt(rkvz, hw_act, ha_act, w2, w0, a2, a0, k_k, k_a, r_k, gn_g, gn_b, n_batch, cast_weights=()):
    _, m, e = rkvz.shape
    p = m // n_batch
    tb = WKV_ROWS
    nt = p // tb
    hw = HEADS_PER_STEP * HEAD_DIM
    nh = e // HEAD_DIM
    row = lambda i, g, t: (i * nt + t, g)
    proj = lambda q: pl.BlockSpec((1, tb, hw), lambda i, g, t: (q, i * nt + t, g))
    par = pl.BlockSpec((1, hw), lambda i, g, t: (0, g))
    lr = w2.shape[0]
    hid = pl.BlockSpec((tb, lr), lambda i, g, t: (i * nt + t, 0))
    up = pl.BlockSpec((lr, hw), lambda i, g, t: (0, g))
    n_groups = e // hw
    cast_specs, every = _cast_job(cast_weights, n_batch * n_groups * nt, lambda i, g, t: (i * n_groups + g) * nt + t)
    return pl.pallas_call(
        functools.partial(_wkv_prompt_kernel, n_cast=len(cast_weights), cast_every=every),
        grid=(n_batch, n_groups, nt),
        in_specs=[proj(0), proj(1), proj(2), proj(3), hid, hid, up, par, up, par,
                  par, par, par, par, par] + cast_specs,
        out_specs=[pl.BlockSpec((tb, hw), row),
                   pl.BlockSpec((1, HEADS_PER_STEP, HEAD_DIM, HEAD_DIM), lambda i, g, t: (i, g, 0, 0))] + cast_specs,
        out_shape=[jax.ShapeDtypeStruct((m, e), BF16),
                   jax.ShapeDtypeStruct((n_batch, nh, HEAD_DIM, HEAD_DIM), F32)]
        + [jax.ShapeDtypeStruct(w.shape, BF16) for w in cast_weights],
        scratch_shapes=[pltpu.VMEM((HEADS_PER_STEP, HEAD_DIM, HEAD_DIM), F32)],
        compiler_params=_cparams(("arbitrary", "arbitrary", "arbitrary")),
        name="wkv_prompt",
    )(rkvz, rkvz, rkvz, rkvz, hw_act, ha_act, w2, w0.reshape(1, e), a2, a0.reshape(1, e),
      k_k.reshape(1, e), k_a.reshape(1, e), r_k.reshape(1, e), gn_g.reshape(1, e), gn_b.reshape(1, e),
      *cast_weights)


def _wkv_sample_kernel(r_ref, k_ref, v_ref, z_ref, wl_ref, al_ref, kk_ref, ka_ref, rk_ref, gg_ref, gb_ref,
                       s_ref, yg_ref, sout_ref, y_scr):
    hd = HEAD_DIM
    r = r_ref[0]
    k = k_ref[0]
    v = v_ref[0]
    a = _sigmoid(al_ref[0])
    d = jnp.exp2((-DECAY_SCALE * LOG2E) * _sigmoid(wl_ref[0]))
    kk = k * kk_ref[...]
    kk = kk / jnp.maximum(jnp.sqrt(jnp.sum(kk * kk, axis=-1, keepdims=True)), 1e-12)
    k2 = k * (1.0 + (a - 1.0) * ka_ref[...])
    bb = kk * a
    nh = r.shape[0]
    ii = lax.broadcasted_iota(jnp.int32, (hd, hd), 0)
    jj = lax.broadcasted_iota(jnp.int32, (hd, hd), 1)
    eye = ii == jj

    row = lambda x, h: x[h:h + 1, :]
    group = 8
    for h0 in range(0, nh, group):
        hs = range(h0, h0 + group)
        s = {h: s_ref[0, h] for h in hs}
        sa = {h: jnp.sum(s[h] * row(kk, h), axis=-1, keepdims=True) for h in hs}
        v_col = {h: jnp.sum(jnp.where(eye, row(v, h), 0.0), axis=-1, keepdims=True) for h in hs}
        s_new = {h: s[h] * row(d, h) - sa[h] * row(bb, h) + v_col[h] * row(k2, h) for h in hs}
        y_col = {h: jnp.sum(s_new[h] * row(r, h), axis=-1, keepdims=True) for h in hs}
        for h in hs:
            sout_ref[0, h] = s_new[h]
            y_scr[h:h + 1, :] = jnp.sum(jnp.where(eye, y_col[h], 0.0), axis=0, keepdims=True)

    y = y_scr[...]
    mean = jnp.mean(y, axis=-1, keepdims=True)
    yc = y - mean
    var = jnp.mean(yc * yc, axis=-1, keepdims=True)
    yn = yc * lax.rsqrt(var + GN_EPS) * gg_ref[...] + gb_ref[...]
    bonus = jnp.sum(r * k2 * rk_ref[...], axis=-1, keepdims=True)
    yg_ref[0] = ((yn + bonus * v) * _silu(z_ref[0])).astype(BF16)


def wkv_sample(rkvz, wl, al, k_k, k_a, r_k, gn_g, gn_b, state):
    _, m, e = rkvz.shape
    nh = e // HEAD_DIM
    hd = HEAD_DIM
    rkvz4 = rkvz.reshape(4, m, nh, hd)
    proj = lambda q: pl.BlockSpec((None, 1, nh, hd), lambda i: (q, i, 0, 0))
    tok = pl.BlockSpec((1, nh, hd), lambda i: (i, 0, 0))
    par = pl.BlockSpec((nh, hd), lambda i: (0, 0))
    st = pl.BlockSpec((1, nh, hd, hd), lambda i: (i, 0, 0, 0))
    as_heads = lambda x: x.reshape(nh, hd)
    return pl.pallas_call(
        _wkv_sample_kernel,
        grid=(m,),
        in_specs=[proj(0), proj(1), proj(2), proj(3), tok, tok, par, par, par, par, par, st],
        out_specs=[tok, st],
        out_shape=[jax.ShapeDtypeStruct((m, nh, hd), BF16), jax.ShapeDtypeStruct(state.shape, F32)],
        scratch_shapes=[pltpu.VMEM((nh, hd), F32)],
        compiler_params=_cparams(("arbitrary",)),
        name="wkv_sample",
    )(rkvz4, rkvz4, rkvz4, rkvz4, wl.reshape(m, nh, hd), al.reshape(m, nh, hd),
      as_heads(k_k), as_heads(k_a), as_heads(r_k), as_heads(gn_g), as_heads(gn_b), state)


def rope_tables(pos):
    half = ROPE_DIM // 2
    inv_freq = ROPE_THETA ** (-jnp.arange(half, dtype=F32) * 2.0 / ROPE_DIM)
    ang = pos.astype(F32)[:, None] * inv_freq[None, :]
    cos = jnp.cos(ang)
    sin = jnp.sin(ang)
    rows = pos.shape[0]
    ones = jnp.ones((rows, HEAD_DIM - ROPE_DIM), F32)
    zeros_h = jnp.zeros((rows, half), F32)
    zeros_r = jnp.zeros((rows, HEAD_DIM - ROPE_DIM), F32)
    cos_h = jnp.concatenate([cos, cos, ones], axis=1)
    sa_h = jnp.concatenate([-sin, zeros_h, zeros_r], axis=1)
    sb_h = jnp.concatenate([zeros_h, sin, zeros_r], axis=1)
    two = lambda t: jnp.concatenate([t, t], axis=1)
    return two(cos_h), two(sa_h), two(sb_h)


def _attn_prompt_kernel(sink_ref, q_ref, kc_ref, kp_ref, vc_ref, vp_ref, z_ref, o_ref, *, first_block):
    n = pl.program_id(1) + first_block
    hd = HEAD_DIM
    blk = q_ref.shape[0]
    n_kv = kc_ref.shape[1] // hd
    grp = q_ref.shape[1] // (n_kv * hd)
    qi = lax.broadcasted_iota(jnp.int32, (blk, 2 * blk), 0)
    kj = lax.broadcasted_iota(jnp.int32, (blk, 2 * blk), 1) - blk
    kpos = n * blk + kj
    diff = qi - kj
    valid = (kpos >= LEAD) & (diff >= 0) & (diff <= WINDOW)
    k_all = jnp.concatenate([kp_ref[...], kc_ref[...]], axis=0).astype(BF16)
    v_all = jnp.concatenate([vp_ref[...], vc_ref[...]], axis=0).astype(BF16)

    def scores(h):
        k_h = k_all[:, h * hd:(h + 1) * hd]
        return [_dot_nt(q_ref[:, (h * grp + gi) * hd:(h * grp + gi + 1) * hd], k_h) for gi in range(grp)]

    outs = []
    s_next = scores(0)
    for h in range(n_kv):
        s_cur = s_next
        if h + 1 < n_kv:
            s_next = scores(h + 1)
        v_h = v_all[:, h * hd:(h + 1) * hd]
        for g0 in range(0, grp, SOFTMAX_BATCH):
            gs = range(g0, min(g0 + SOFTMAX_BATCH, grp))
            sks = {gi: sink_ref[h * grp + gi] * LOG2E for gi in gs}
            ss = {gi: jnp.where(valid, s_cur[gi], -jnp.inf) for gi in gs}
            ms = {gi: jnp.maximum(jnp.max(ss[gi], axis=-1, keepdims=True), sks[gi]) for gi in gs}
            ps = {gi: jnp.exp2(ss[gi] - ms[gi]) for gi in gs}
            dens = {gi: jnp.sum(ps[gi], axis=-1, keepdims=True) + jnp.exp2(sks[gi] - ms[gi]) for gi in gs}
            outs += [_dot(ps[gi].astype(BF16), v_h) / dens[gi] for gi in gs]
    att = jnp.concatenate(outs, axis=1)
    o_ref[...] = (att * _silu(z_ref[...])).astype(o_ref.dtype)


def attn_prompt(sinks, q, k, v, z, n_batch, first_block):
    m, e = q.shape
    nb = m // (n_batch * BLOCK)
    nb_out = nb - first_block
    kw = k.shape[1]
    cur = lambda i, n: (i * nb + n + first_block, 0)
    prv = lambda i, n: (i * nb + jnp.maximum(n + first_block - 1, 0), 0)
    return pl.pallas_call(
        functools.partial(_attn_prompt_kernel, first_block=first_block),
        grid=(n_batch, nb_out),
        in_specs=[pl.BlockSpec(memory_space=pltpu.SMEM),
                  pl.BlockSpec((BLOCK, e), cur),
                  pl.BlockSpec((BLOCK, kw), cur), pl.BlockSpec((BLOCK, kw), prv),
                  pl.BlockSpec((BLOCK, kw), cur), pl.BlockSpec((BLOCK, kw), prv),
                  pl.BlockSpec((BLOCK, e), cur)],
        out_specs=pl.BlockSpec((BLOCK, e), lambda i, n: (i * nb_out + n, 0)),
        out_shape=jax.ShapeDtypeStruct((n_batch * nb_out * BLOCK, e), BF16),
        compiler_params=_cparams(("arbitrary", "arbitrary")),
        name="attn_prompt",
    )(sinks, q, k, k, v, v, z)


def _attn_sample_kernel(sink_ref, q_ref, kc_ref, vc_ref, kn_ref, vn_ref, z_ref, o_ref, ko_ref, vo_ref):
    hd = HEAD_DIM
    win = kc_ref.shape[1]
    n_kv = kc_ref.shape[2] // hd
    nq = q_ref.shape[1]
    grp = nq // n_kv
    pad = 8
    kc = kc_ref[0]
    vc = vc_ref[0]
    kn = kn_ref[0]
    vn = vn_ref[0]
    first = lax.broadcasted_iota(jnp.int32, (pad, kc.shape[1]), 0) == 0
    k_all = jnp.concatenate([kc, jnp.where(first, kn, 0.0)], axis=0).astype(BF16)
    v_all = jnp.concatenate([vc, jnp.where(first, vn, 0.0)], axis=0).astype(BF16)
    col = lax.broadcasted_iota(jnp.int32, (grp, win + pad), 1)
    valid = (col <= win) & (win - col <= WINDOW)
    q = q_ref[0].astype(BF16)
    row_i = lax.broadcasted_iota(jnp.int32, (grp, 1), 0)
    hs = range(n_kv)
    sks = []
    for h in hs:
        sk = jnp.zeros((grp, 1), F32)
        for gi in range(grp):
            sk = jnp.where(row_i == gi, sink_ref[h * grp + gi] * LOG2E, sk)
        sks.append(sk)
    ss = [jnp.where(valid, _dot_nt(q[h * grp:(h + 1) * grp, :], k_all[:, h * hd:(h + 1) * hd]), -jnp.inf)
          for h in hs]
    ms = [jnp.maximum(jnp.max(ss[h], axis=-1, keepdims=True), sks[h]) for h in hs]
    ps = [jnp.exp2(ss[h] - ms[h]) for h in hs]
    dens = [jnp.sum(ps[h], axis=-1, keepdims=True) + jnp.exp2(sks[h] - ms[h]) for h in hs]
    outs = [_dot(ps[h].astype(BF16), v_all[:, h * hd:(h + 1) * hd]) / dens[h] for h in hs]
    att = jnp.concatenate(outs, axis=0)
    o_ref[0] = (att * _silu(z_ref[0])).astype(o_ref.dtype)
    last = lax.broadcasted_iota(jnp.int32, kc.shape, 0) == win - 1
    ko_ref[0] = jnp.where(last, kn, pltpu.roll(kc, win - 1, axis=0))
    vo_ref[0] = jnp.where(last, vn, pltpu.roll(vc, win - 1, axis=0))


def attn_sample(sinks, q, cache_k, cache_v, k_new, v_new, z):
    m, win, kw = cache_k.shape
    nq = q.shape[1]
    hd = HEAD_DIM
    tok = pl.BlockSpec((1, nq, hd), lambda i: (i, 0, 0))
    cache = pl.BlockSpec((1, win, kw), lambda i: (i, 0, 0))
    new = pl.BlockSpec((1, 1, kw), lambda i: (i, 0, 0))
    return pl.pallas_call(
        _attn_sample_kernel,
        grid=(m,),
        in_specs=[pl.BlockSpec(memory_space=pltpu.SMEM), tok, cache, cache, new, new, tok],
        out_specs=[tok, cache, cache],
        out_shape=[jax.ShapeDtypeStruct((m, nq, hd), BF16),
                   jax.ShapeDtypeStruct(cache_k.shape, F32), jax.ShapeDtypeStruct(cache_v.shape, F32)],
        compiler_params=_cparams(("arbitrary",)),
        name="attn_sample",
    )(sinks, q, cache_k, cache_v, k_new, v_new, z)


def _pad_lora(w_down, w_up):
    r = w_down.shape[1]
    return (jnp.pad(w_down, ((0, 0), (0, LORA_PAD - r))).astype(BF16),
            jnp.pad(w_up, ((0, LORA_PAD - r), (0, 0))).astype(BF16))


def kernel(x_prompt, x_sample, state_wkv, state_shift, cache_k, cache_v, meta_tokens, a_norm, a_mu, a_w_rkvz,
           a_w0, a_w1, a_w2, a_a0, a_a1, a_a2, a_k_k, a_k_a, a_r_k, a_gn_g, a_gn_b, a_w_out, kv_norm, w_kv,
           b_norm, b_w_qz, b_sinks, b_w_o, final_norm):
    nb, seq, d = x_prompt.shape
    db, dseq, _ = x_sample.shape
    assert dseq == 1 and a_norm.shape[0] == 1 and b_norm.shape[0] == 1
    e = a_w_rkvz.shape[3]
    win = cache_k.shape[1]
    p_len = LEAD + N_META + seq
    assert p_len % BLOCK == 0 and (LEAD + N_META) == BLOCK
    kvw = N_KV_HEADS * HEAD_DIM

    w1, w2 = _pad_lora(a_w1[0], a_w2[0])
    a1, a2 = _pad_lora(a_a1[0], a_a2[0])
    mu = a_mu[0]
    sinks = b_sinks[0]
    gains_b = jnp.stack([kv_norm, b_norm[0]])

    tm = p_len // 8

    head = jnp.concatenate([jnp.zeros((LEAD, d), F32), meta_tokens], axis=0)
    w4 = a_w_rkvz[0]
    xm, hw_p, ha_p, x_last, w_rkvz = norm_shift_prompt(x_prompt, head, a_norm[0], mu, w1, a1,
                                                      cast_weights=(w4.reshape(-1, w4.shape[2]),))
    w_rkvz = w_rkvz.reshape(w4.shape)
    p_state_shift = x_last.reshape(1, nb, d)
    rkvz = matmul_groups(xm, w_rkvz, tm, F32)
    yg, p_state, w_out, w_kv_bf, w_qz, w_o = wkv_prompt(
        rkvz, hw_p, ha_p, w2, a_w0[0], a2, a_a0[0], a_k_k[0], a_k_a[0], a_r_k[0].reshape(-1), a_gn_g[0],
        a_gn_b[0], nb, cast_weights=(a_w_out[0], w_kv, b_w_qz[0], b_w_o[0]))
    w_qz = w_qz[None]
    hp, hn_kv, hn_b = matmul_residual_norm_blocks(yg, w_out, x_prompt, gains_b, nb, BF16, True, head=head)

    pos_p = jnp.maximum(jnp.arange(p_len, dtype=jnp.int32) - LEAD, 0)
    tabs_p = tuple(jnp.tile(t, (nb, 1)) for t in rope_tables(pos_p))
    k_p, v_p = matmul_rope(hn_kv, w_kv_bf, tabs_p, tm, kvw, (F32, F32))
    q_p, = matmul_rope(hn_b, w_qz, tabs_p, tm, e, (BF16,), scale=Q_SCALE, n=e)
    z_p = matmul_groups(hn_b[None], w_qz, tm, F32, n=e, col=1)[0]
    skip = (LEAD + N_META) // BLOCK
    att = attn_prompt(sinks, q_p, k_p, v_p, z_p, nb, skip)
    y_prompt, = matmul_residual_norm_blocks(att, w_o, hp, final_norm[None], nb, F32, False, first_block=skip)
    y_prompt = y_prompt.reshape(nb, seq, d)
    tail = lambda t: t.reshape(nb, p_len, kvw)[:, -win:].reshape(nb, win, N_KV_HEADS, HEAD_DIM)
    p_cache_k = tail(k_p)
    p_cache_v = tail(v_p)

    hs = x_sample.reshape(db, d)
    xm_s, hw_s, ha_s, xn_s = norm_shift_sample(hs, state_shift[0], a_norm[0], mu, w1, a1)
    rkvz_s = matmul_groups(xm_s, w_rkvz, db, F32)
    wl_s, al_s = lora_up(hw_s, ha_s, w2, a_w0[0], a2, a_a0[0])
    yg_s, s_state = wkv_sample(rkvz_s, wl_s, al_s, a_k_k[0], a_k_a[0], a_r_k[0].reshape(-1), a_gn_g[0],
                               a_gn_b[0], state_wkv[0])
    hs, hn_kv_s, hn_b_s = matmul_residual_norm(yg_s.reshape(db, e), w_out, hs, gains_b, db, BF16)
    tabs_s = rope_tables(jnp.full((db,), PAST_LEN, jnp.int32))
    k_s, v_s = matmul_rope(hn_kv_s, w_kv_bf, tabs_s, db, kvw, (F32, F32))
    q_s, = matmul_rope(hn_b_s, w_qz, tabs_s, db, e, (F32,), scale=Q_SCALE, n=e)
    z_s = matmul_groups(hn_b_s[None], w_qz, db, F32, n=e, col=1)[0]
    nq = e // HEAD_DIM
    att_s, s_cache_k, s_cache_v = attn_sample(
        sinks, q_s.reshape(db, nq, HEAD_DIM), cache_k.reshape(db, win, kvw), cache_v.reshape(db, win, kvw),
        k_s.reshape(db, 1, kvw), v_s.reshape(db, 1, kvw), z_s.reshape(db, nq, HEAD_DIM))
    y_s, = matmul_residual_norm(att_s.reshape(db, e), w_o, hs, final_norm[None], db, F32, emit_h=False)
    y_sample = y_s.reshape(db, 1, d)

    return (y_prompt, y_sample, p_state[None], p_state_shift,
            p_cache_k, p_cache_v,
            s_state[None], xn_s[None],
            s_cache_k.reshape(cache_k.shape), s_cache_v.reshape(cache_v.shape))
```

```python
import functools
import math

import jax
import jax.numpy as jnp
from jax import lax
from jax.experimental import pallas as pl
from jax.experimental.pallas import tpu as pltpu

F32 = jnp.float32
BF16 = jnp.bfloat16

HEAD_DIM = 64
N_KV_HEADS = 8
WINDOW = 128
BLOCK = 128
ROPE_DIM = HEAD_DIM // 4
ROPE_THETA = 500000.0
N_META = 16
PAST_LEN = 16384
RMS_EPS = 1e-6
GN_EPS = 64e-5
NORM_FLOOR_SQ = 1e-24
LEAD = (-N_META) % BLOCK
CHUNK = 64
WKV_ROWS = 128
CAST_CHUNKS = 32
SEQS_PER_STEP = 2
HEADS_PER_STREAM = 32
HEADS_PER_STEP = 32
LORA_PAD = 128
MXU_TILE = 256
ROPE_SLAB = 512
VMEM_LIMIT = 48 * 1024 * 1024
LOG2E = 1.0 / math.log(2.0)
DECAY_SCALE = math.exp(-0.5)
SOFTMAX_BATCH = 4
Q_SCALE = HEAD_DIM ** -0.5 * LOG2E


def _cparams(sem):
    return pltpu.CompilerParams(dimension_semantics=sem, vmem_limit_bytes=VMEM_LIMIT)


def _sigmoid(x):
    return 1.0 / (1.0 + jnp.exp2(x * (-LOG2E)))


def _silu(x):
    return x * _sigmoid(x)


def _dot(a, b):
    return jnp.dot(a, b, preferred_element_type=F32)


def _dot_nt(a, b):
    return lax.dot_general(a, b, (((1,), (1,)), ((), ())), preferred_element_type=F32)


def _dot_tn(a, b):
    return lax.dot_general(a, b, (((0,), (0,)), ((), ())), preferred_element_type=F32)


def _split_hi_lo(x):
    hi = x.astype(BF16)
    lo = (x - hi.astype(F32)).astype(BF16)
    return hi, lo


def _cast_job(weights, n_steps, step_of):
    n_chunks = min(CAST_CHUNKS, n_steps)
    every = n_steps // n_chunks
    assert all(w.shape[0] % (n_chunks * 16) == 0 for w in weights)
    chunk = lambda *idx: (jnp.minimum(step_of(*idx) // every, n_chunks - 1), 0)
    return [pl.BlockSpec((w.shape[0] // n_chunks, w.shape[1]), chunk) for w in weights], every


def _cast_chunks(step, every, srcs, dsts):
    @pl.when(step % every == 0)
    def _():
        for src, dst in zip(srcs, dsts):
            dst[...] = src[...].astype(dst.dtype)


def _mixes(xn, prev, mu_ref, w1_ref, a1_ref, xm_ref, hw_ref, ha_ref):
    xx = prev - xn
    n_proj = xm_ref.shape[0]
    for p in range(n_proj):
        xm_ref[p] = (xn + xx * mu_ref[p:p + 1, :]).astype(xm_ref.dtype)
    xw = (xn + xx * mu_ref[n_proj:n_proj + 1, :]).astype(BF16)
    xa = (xn + xx * mu_ref[n_proj + 1:n_proj + 2, :]).astype(BF16)
    hw_ref[...] = jnp.tanh(_dot(xw, w1_ref[...])).astype(hw_ref.dtype)
    ha_ref[...] = _dot(xa, a1_ref[...]).astype(ha_ref.dtype)


def _norm_shift_kernel(x_ref, head_ref, g_ref, mu_ref, w1_ref, a1_ref, *rest, n_cast, cast_every):
    cast_in = rest[:n_cast]
    xm_ref, hw_ref, ha_ref, last_ref = rest[n_cast:n_cast + 4]
    cast_out = rest[n_cast + 4:2 * n_cast + 4]
    carry_ref = rest[-1]

    @pl.when(pl.program_id(1) == 0)
    def _():
        carry_ref[...] = jnp.zeros_like(carry_ref)

    _cast_chunks(pl.program_id(0) * pl.num_programs(1) + pl.program_id(1), cast_every, cast_in, cast_out)

    is_head = pl.program_id(1) == 0
    for sq in range(x_ref.shape[0]):
        x = jnp.where(is_head, head_ref[...], x_ref[sq])
        tm = x.shape[0]
        xn = x * lax.rsqrt(jnp.mean(x * x, axis=-1, keepdims=True) + RMS_EPS) * g_ref[...]
        rolled = pltpu.roll(xn, 1, axis=0)
        row = lax.broadcasted_iota(jnp.int32, xn.shape, 0)
        prev = jnp.where(row == 0, carry_ref[sq, 0:1, :], rolled)
        _mixes(xn, prev, mu_ref, w1_ref, a1_ref, xm_ref.at[:, sq], hw_ref.at[sq], ha_ref.at[sq])
        carry_ref[sq, 0:1, :] = xn[tm - 1:tm, :]
        last_ref[sq] = xn[tm - 1:tm, :]


def norm_shift_prompt(x, head, g, mu, w1, a1, cast_weights=()):
    b, seq, d = x.shape
    tm = BLOCK
    p = tm + seq
    n_mix = mu.shape[0]
    n_proj = n_mix - 2
    lr = w1.shape[1]
    nt = p // tm
    sp = SEQS_PER_STEP
    const = lambda shape: pl.BlockSpec(shape, lambda i, t: (0,) * len(shape))
    hid = pl.BlockSpec((sp, tm, lr), lambda i, t: (i, t, 0))
    cast_specs, every = _cast_job(cast_weights, (b // sp) * nt, lambda i, t: i * nt + t)
    xm, hw_act, ha_act, last, *casts = pl.pallas_call(
        functools.partial(_norm_shift_kernel, n_cast=len(cast_weights), cast_every=every),
        grid=(b // sp, nt),
        in_specs=[pl.BlockSpec((sp, tm, d), lambda i, t: (i, jnp.maximum(t - 1, 0), 0)),
                  const((tm, d)), const((1, d)), const((n_mix, d)), const((d, lr)), const((d, lr))] + cast_specs,
        out_specs=[pl.BlockSpec((n_proj, sp, tm, d), lambda i, t: (0, i, t, 0)), hid, hid,
                   pl.BlockSpec((sp, 1, d), lambda i, t: (i, 0, 0))] + cast_specs,
        out_shape=[jax.ShapeDtypeStruct((n_proj, b, p, d), BF16),
                   jax.ShapeDtypeStruct((b, p, lr), BF16), jax.ShapeDtypeStruct((b, p, lr), BF16),
                   jax.ShapeDtypeStruct((b, 1, d), F32)] + [jax.ShapeDtypeStruct(w.shape, BF16) for w in cast_weights],
        scratch_shapes=[pltpu.VMEM((sp, 8, d), F32)],
        compiler_params=_cparams(("arbitrary", "arbitrary")),
        name="norm_shift_prompt",
    )(x, head, g.reshape(1, d), mu, w1, a1, *cast_weights)
    return (xm.reshape(n_proj, b * p, d), hw_act.reshape(b * p, lr), ha_act.reshape(b * p, lr), last, *casts)


def _norm_shift_sample_kernel(x_ref, prev_ref, g_ref, mu_ref, w1_ref, a1_ref, xm_ref, hw_ref, ha_ref, xn_ref):
    x = x_ref[...]
    xn = x * lax.rsqrt(jnp.mean(x * x, axis=-1, keepdims=True) + RMS_EPS) * g_ref[...]
    xn_ref[...] = xn
    _mixes(xn, prev_ref[...], mu_ref, w1_ref, a1_ref, xm_ref, hw_ref, ha_ref)


def norm_shift_sample(x, prev, g, mu, w1, a1):
    m, d = x.shape
    lr = w1.shape[1]
    return pl.pallas_call(
        _norm_shift_sample_kernel,
        out_shape=[jax.ShapeDtypeStruct((mu.shape[0] - 2, m, d), BF16),
                   jax.ShapeDtypeStruct((m, lr), BF16), jax.ShapeDtypeStruct((m, lr), BF16),
                   jax.ShapeDtypeStruct((m, d), F32)],
        name="norm_shift_sample",
    )(x, prev, g.reshape(1, d), mu, w1, a1)


def _rope(y, cos, sin_a, sin_b):
    half = ROPE_DIM // 2
    step = ROPE_SLAB
    rep = step // cos.shape[1]
    tile = lambda t: jnp.concatenate([t] * rep, axis=1)
    cos_t, sa_t, sb_t = tile(cos), tile(sin_a), tile(sin_b)
    outs = []
    for j in range(y.shape[1] // step):
        ys = y[:, j * step:(j + 1) * step]
        outs.append(ys * cos_t + pltpu.roll(ys, step - half, axis=1) * sa_t + pltpu.roll(ys, half, axis=1) * sb_t)
    return jnp.concatenate(outs, axis=1) if len(outs) > 1 else outs[0]


def _mm_group_kernel(x_ref, w_ref, o_ref):
    o_ref[0] = _dot(x_ref[0], w_ref[0]).astype(o_ref.dtype)


def matmul_groups(x, w, tm, out_dtype, n=None, col=0):
    g = w.shape[0]
    n = w.shape[2] if n is None else n
    _, m, kdim = x.shape
    return pl.pallas_call(
        _mm_group_kernel,
        grid=(g, m // tm),
        in_specs=[pl.BlockSpec((1, tm, kdim), lambda q, i: (q, i, 0)),
                  pl.BlockSpec((1, kdim, n), lambda q, i: (q, 0, col))],
        out_specs=pl.BlockSpec((1, tm, n), lambda q, i: (q, i, 0)),
        out_shape=jax.ShapeDtypeStruct((g, m, n), out_dtype),
        compiler_params=_cparams(("arbitrary", "arbitrary")),
        name="matmul_groups",
    )(x, w)


def _mm_rope_kernel(x_ref, w_ref, cos_ref, sa_ref, sb_ref, *o_refs, n_rope, scale):
    y = _dot(x_ref[...], w_ref[...])
    rot = _rope(y[:, :n_rope], cos_ref[...], sa_ref[...], sb_ref[...])
    if scale != 1.0:
        rot = rot * scale
    o_refs[0][...] = rot.astype(o_refs[0].dtype)
    if len(o_refs) > 1:
        o_refs[1][...] = y[:, n_rope:].astype(o_refs[1].dtype)


def matmul_rope(x, w, tables, tm, n_rope, out_dtypes, scale=1.0, n=None):
    m, kdim = x.shape
    n = w.shape[-1] if n is None else n
    w_block = (kdim, n) if w.ndim == 2 else (None, kdim, n)
    lanes = tables[0].shape[1]
    widths = [n_rope] + ([n - n_rope] if n > n_rope else [])
    tab_blocks = tables[0].shape[0] // tm
    tab = pl.BlockSpec((tm, lanes), lambda i: (i % tab_blocks, 0))
    outs = pl.pallas_call(
        functools.partial(_mm_rope_kernel, n_rope=n_rope, scale=scale),
        grid=(m // tm,),
        in_specs=[pl.BlockSpec((tm, kdim), lambda i: (i, 0)),
                  pl.BlockSpec(w_block, lambda i: (0,) * w.ndim, pipeline_mode=pl.Buffered(1)),
                  tab, tab, tab],
        out_specs=[pl.BlockSpec((tm, wd), lambda i: (i, 0)) for wd in widths],
        out_shape=[jax.ShapeDtypeStruct((m, wd), dt) for wd, dt in zip(widths, out_dtypes)],
        compiler_params=_cparams(("arbitrary",)),
        name="matmul_rope",
    )(x, w, *tables)
    return outs


def _mm_res_norm_kernel(x_ref, w_ref, res_ref, g_ref, *out_refs, emit_h):
    h = res_ref[...] + _dot(x_ref[...], w_ref[...])
    hn_refs = out_refs
    if emit_h:
        out_refs[0][...] = h
        hn_refs = out_refs[1:]
    inv = lax.rsqrt(jnp.mean(h * h, axis=-1, keepdims=True) + RMS_EPS)
    for j, hn_ref in enumerate(hn_refs):
        hn_ref[...] = (h * inv * g_ref[j:j + 1, :]).astype(hn_ref.dtype)


def matmul_residual_norm(x, w, res, gains, tm, norm_dtype, emit_h=True):
    m, kdim = x.shape
    n = w.shape[1]
    ng = gains.shape[0]
    row = lambda width: pl.BlockSpec((tm, width), lambda i: (i, 0))
    return pl.pallas_call(
        functools.partial(_mm_res_norm_kernel, emit_h=emit_h),
        grid=(m // tm,),
        in_specs=[row(kdim),
                  pl.BlockSpec((kdim, n), lambda i: (0, 0), pipeline_mode=pl.Buffered(1)),
                  row(n),
                  pl.BlockSpec((ng, n), lambda i: (0, 0))],
        out_specs=[row(n)] * (int(emit_h) + ng),
        out_shape=[jax.ShapeDtypeStruct((m, n), F32)] * int(emit_h) + [jax.ShapeDtypeStruct((m, n), norm_dtype)] * ng,
        compiler_params=_cparams(("arbitrary",)),
        name="matmul_residual_norm",
    )(x, w, res, gains)


def _mm_res_norm_blocks_kernel(x_ref, w_ref, res_ref, *rest, emit_h, head, first_block):
    if head:
        head_ref, g_ref, *out_refs = rest
        res = jnp.where(pl.program_id(1) + first_block == 0, head_ref[...][None], res_ref[...])
    else:
        g_ref, *out_refs = rest
        res = res_ref[...]
    seqs, blk, n = res.shape
    h = res.reshape(seqs * blk, n) + _dot(x_ref[...].reshape(seqs * blk, x_ref.shape[2]), w_ref[...])
    hn_refs = out_refs
    if emit_h:
        out_refs[0][...] = h.reshape(seqs, blk, n)
        hn_refs = out_refs[1:]
    inv = lax.rsqrt(jnp.mean(h * h, axis=-1, keepdims=True) + RMS_EPS)
    for j, hn_ref in enumerate(hn_refs):
        hn_ref[...] = (h * inv * g_ref[j:j + 1, :]).astype(hn_ref.dtype).reshape(seqs, blk, n)


def matmul_residual_norm_blocks(x, w, res, gains, n_batch, norm_dtype, emit_h, head=None, first_block=0):
    kdim = x.shape[1]
    n = w.shape[1]
    ng = gains.shape[0]
    nb_out = x.shape[0] // (n_batch * BLOCK)
    nb = nb_out + first_block
    sp = SEQS_PER_STEP
    blocks = lambda width, shift: pl.BlockSpec((sp, BLOCK, width), lambda i, t: (i, t + shift, 0))
    const = lambda shape: pl.BlockSpec(shape, lambda i, t: (0,) * len(shape))
    if head is not None:
        res_specs = [pl.BlockSpec((sp, BLOCK, n), lambda i, t: (i, jnp.maximum(t + first_block - 1, 0), 0)),
                     const((BLOCK, n))]
        res_args = [res, head]
    else:
        res_specs, res_args = [blocks(n, first_block)], [res.reshape(n_batch, nb * BLOCK, n)]
    outs = pl.pallas_call(
        functools.partial(_mm_res_norm_blocks_kernel, emit_h=emit_h, head=head is not None, first_block=first_block),
        grid=(n_batch // sp, nb_out),
        in_specs=[blocks(kdim, 0), pl.BlockSpec((kdim, n), lambda i, t: (0, 0), pipeline_mode=pl.Buffered(1))]
        + res_specs + [const((ng, n))],
        out_specs=[blocks(n, first_block)] * int(emit_h) + [blocks(n, 0)] * ng,
        out_shape=[jax.ShapeDtypeStruct((n_batch, nb * BLOCK, n), F32)] * int(emit_h)
        + [jax.ShapeDtypeStruct((n_batch, nb_out * BLOCK, n), norm_dtype)] * ng,
        compiler_params=_cparams(("arbitrary", "arbitrary")),
        name="matmul_residual_norm_blocks",
    )(x.reshape(n_batch, nb_out * BLOCK, kdim), w, *res_args, gains)
    return [o.reshape(-1, n) for o in outs]


def _lora_up_kernel(hw_ref, ha_ref, w2_ref, w0_ref, a2_ref, a0_ref, wl_ref, al_ref):
    wl_ref[...] = w0_ref[...] + _dot(hw_ref[...], w2_ref[...])
    al_ref[...] = a0_ref[...] + _dot(ha_ref[...], a2_ref[...])


def lora_up(hw, ha, w2, w0, a2, a0):
    m = hw.shape[0]
    e = w2.shape[1]
    return pl.pallas_call(
        _lora_up_kernel,
        out_shape=[jax.ShapeDtypeStruct((m, e), F32), jax.ShapeDtypeStruct((m, e), F32)],
        name="lora_up",
    )(hw, ha, w2, w0.reshape(1, e), a2, a0.reshape(1, e))


def _seg_sum(x, ones_bd):
    hi = x.astype(BF16)
    outs = []
    for c in range(x.shape[1] // MXU_TILE):
        sl = slice(c * MXU_TILE, (c + 1) * MXU_TILE)
        outs.append(_dot(hi[:, sl], ones_bd))
    return jnp.concatenate(outs, axis=1) if len(outs) > 1 else outs[0]


def _wkv_prompt_kernel(r_ref, k_ref, v_ref, z_ref, hw_ref, ha_ref, w2_ref, w0_ref, a2_ref, a0_ref,
                       kk_ref, ka_ref, rk_ref, gg_ref, gb_ref, *rest, n_cast, cast_every):
    cast_in = rest[:n_cast]
    yg_ref, sout_ref = rest[n_cast:n_cast + 2]
    cast_out = rest[n_cast + 2:2 * n_cast + 2]
    s_ref = rest[-1]
    t_idx = pl.program_id(2)
    c = CHUNK
    hd = HEAD_DIM

    @pl.when(t_idx == 0)
    def _():
        s_ref[...] = jnp.zeros_like(s_ref)

    step = (pl.program_id(0) * pl.num_programs(1) + pl.program_id(1)) * pl.num_programs(2) + t_idx
    _cast_chunks(step, cast_every, cast_in, cast_out)

    tb = r_ref.shape[1]
    nh = HEADS_PER_STREAM
    hw = nh * hd
    nc = tb // c

    li = lax.broadcasted_iota(jnp.int32, (MXU_TILE, MXU_TILE), 0) // hd
    lj = lax.broadcasted_iota(jnp.int32, (MXU_TILE, MXU_TILE), 1) // hd
    ones_bd = jnp.where(li == lj, 1.0, 0.0).astype(BF16)
    bi_ = lax.broadcasted_iota(jnp.int32, (tb, tb), 0)
    bj_ = lax.broadcasted_iota(jnp.int32, (tb, tb), 1)
    tri_incl = jnp.where((bj_ <= bi_) & (bj_ // c == bi_ // c), 1.0, 0.0).astype(BF16)
    ti = lax.broadcasted_iota(jnp.int32, (c, c), 0)
    tj = lax.broadcasted_iota(jnp.int32, (c, c), 1)
    ai = lax.broadcasted_iota(jnp.int32, (c, 2 * c), 0)
    aj = lax.broadcasted_iota(jnp.int32, (c, 2 * c), 1)
    upper = aj >= c
    aj_mod = jnp.where(upper, aj - c, aj)
    masks = dict(
        strict=tj < ti,
        eye=jnp.where(ti == tj, 1.0, 0.0).astype(F32),
        top_k=upper & (aj_mod < ai),
        bot=aj_mod <= ai,
        mean_bd=jnp.where(li == lj, 1.0 / hd, 0.0).astype(BF16))

    for st in range(r_ref.shape[2] // hw):
        _wkv_stream(st, hw, nc, r_ref, k_ref, v_ref, z_ref, hw_ref, ha_ref, w2_ref, w0_ref, a2_ref, a0_ref,
                    kk_ref, ka_ref, rk_ref, gg_ref, gb_ref, yg_ref, s_ref, ones_bd, tri_incl, masks)

    @pl.when(t_idx == pl.num_programs(2) - 1)
    def _():
        sout_ref[0] = s_ref[...]


def _wkv_stream(st, hw, nc, r_ref, k_ref, v_ref, z_ref, hw_ref, ha_ref, w2_ref, w0_ref, a2_ref, a0_ref,
                kk_ref, ka_ref, rk_ref, gg_ref, gb_ref, yg_ref, s_ref, ones_bd, tri_incl, masks):
    c = CHUNK
    hd = HEAD_DIM
    nh = hw // hd
    ls = slice(st * hw, (st + 1) * hw)
    h0 = st * nh
    r = r_ref[0, :, ls]
    k = k_ref[0, :, ls]
    v = v_ref[0, :, ls]
    wl = w0_ref[:, ls] + _dot(hw_ref[...], w2_ref[:, ls])
    al = a0_ref[:, ls] + _dot(ha_ref[...], a2_ref[:, ls])
    a = _sigmoid(al)
    lw = (-DECAY_SCALE * LOG2E) * _sigmoid(wl)
    kk = k * kk_ref[:, ls]
    n2 = _seg_sum(kk * kk, ones_bd)
    kk = kk * lax.rsqrt(jnp.maximum(n2, NORM_FLOOR_SQ))
    ka = ka_ref[:, ls]
    k2 = k * (a * ka + (1.0 - ka))
    bb = kk * a

    lw_hi, lw_lo = _split_hi_lo(lw)
    g = _dot(tri_incl, lw_hi) + _dot(tri_incl, lw_lo)
    mid = lambda ci: g[ci * c + c // 2 - 1:ci * c + c // 2, :]
    gm = jnp.concatenate([jnp.broadcast_to(mid(ci), (c, hw)) for ci in range(nc)], axis=0)
    t = g - gm
    e_a = jnp.exp2(t)
    e_prev = jnp.exp2(t - lw)
    e_inv = jnp.exp2(-t)
    e1 = [jnp.exp2(mid(ci)) for ci in range(nc)]
    e2 = [jnp.exp2(g[ci * c + c - 1:ci * c + c, :] - mid(ci)) for ci in range(nc)]

    kkd = (kk * e_prev).astype(BF16)
    rd = (r * e_a).astype(BF16)
    bi = (bb * e_inv).astype(BF16)
    ki = (k2 * e_inv).astype(BF16)
    v_bf = v.astype(BF16)
    zeros_cv = jnp.zeros((c, hd), BF16)

    pairs = [(ci, h) for ci in range(nc) for h in range(nh)]
    rows = lambda ci: slice(ci * c, (ci + 1) * c)
    cols = lambda h: slice(h * hd, (h + 1) * hd)
    xs = {(ci, h): jnp.concatenate([kkd[rows(ci), cols(h)], rd[rows(ci), cols(h)]], axis=0) for ci, h in pairs}
    r1s = {(ci, h): jnp.concatenate([bi[rows(ci), cols(h)], ki[rows(ci), cols(h)]], axis=0) for ci, h in pairs}
    vs = {(ci, h): v_bf[rows(ci), cols(h)] for ci, h in pairs}
    a_mats = {p: _dot_nt(xs[p], r1s[p]) for p in pairs}
    lk_vs = {p: _dot(jnp.where(masks["top_k"], a_mats[p][:c, :], 0.0).astype(BF16),
                     jnp.concatenate([zeros_cv, vs[p]], axis=0)) for p in pairs}
    lps = {p: jnp.where(masks["strict"], a_mats[p][:c, :c], 0.0) for p in pairs}
    ts = {p: masks["eye"] - lps[p] for p in pairs}
    for _ in range(int(math.log2(c)) - 1):
        lpb = {p: lps[p].astype(BF16) for p in pairs}
        lps = {p: _dot(lpb[p], lpb[p]) for p in pairs}
        ts = {p: _dot(ts[p].astype(BF16), (masks["eye"] + lps[p]).astype(BF16)) for p in pairs}
    a_bots = {p: jnp.where(masks["bot"], a_mats[p][c:, :], 0.0).astype(BF16) for p in pairs}
    t_bf = {p: ts[p].astype(BF16) for p in pairs}

    state = [s_ref[h0 + h] for h in range(nh)]
    y_rows = []
    for ci in range(nc):
        hs = range(nh)
        sms = [state[h] * e1[ci][:, cols(h)] for h in hs]
        p_mats = [_dot_nt(xs[ci, h], sms[h].astype(BF16)) for h in hs]
        us = [-_dot(t_bf[ci, h], (p_mats[h][:c, :] + lk_vs[ci, h]).astype(BF16)) for h in hs]
        uvs = [jnp.concatenate([us[h].astype(BF16), vs[ci, h]], axis=0) for h in hs]
        ys = [p_mats[h][c:, :] + _dot(a_bots[ci, h], uvs[h]) for h in hs]
        state = [(sms[h] + _dot_tn(uvs[h], r1s[ci, h])) * e2[ci][:, cols(h)] for h in hs]
        y_rows.append(jnp.concatenate(ys, axis=1))
    for h in range(nh):
        s_ref[h0 + h] = state[h]
    y = jnp.concatenate(y_rows, axis=0) if nc > 1 else y_rows[0]

    mean = _seg_sum(y, masks["mean_bd"])
    yc = y - mean
    var = _seg_sum(yc * yc, masks["mean_bd"])
    yn = yc * lax.rsqrt(var + GN_EPS) * gg_ref[:, ls] + gb_ref[:, ls]
    bonus = _seg_sum(r * k2 * rk_ref[:, ls], ones_bd)
    yg_ref[:, ls] = ((yn + bonus * v) * _silu(z_ref[0, :, ls])).astype(BF16)


def wkv_prompt(rkvz, hw_act, ha_act, w2, w0, a2, a0, k_k, k_a, r_k, gn_g, gn_b, n_batch, cast_weights=()):
    _, m, e = rkvz.shape
    p = m // n_batch
    tb = WKV_ROWS
    nt = p // tb
    hw = HEADS_PER_STEP * HEAD_DIM
    nh = e // HEAD_DIM
    row = lambda i, g, t: (i * nt + t, g)
    proj = lambda q: pl.BlockSpec((1, tb, hw), lambda i, g, t: (q, i * nt + t, g))
    par = pl.BlockSpec((1, hw), lambda i, g, t: (0, g))
    lr = w2.shape[0]
    hid = pl.BlockSpec((tb, lr), lambda i, g, t: (i * nt + t, 0))
    up = pl.BlockSpec((lr, hw), lambda i, g, t: (0, g))
    n_groups = e // hw
    cast_specs, every = _cast_job(cast_weights, n_batch * n_groups * nt, lambda i, g, t: (i * n_groups + g) * nt + t)
    return pl.pallas_call(
        functools.partial(_wkv_prompt_kernel, n_cast=len(cast_weights), cast_every=every),
        grid=(n_batch, n_groups, nt),
        in_specs=[proj(0), proj(1), proj(2), proj(3), hid, hid, up, par, up, par,
                  par, par, par, par, par] + cast_specs,
        out_specs=[pl.BlockSpec((tb, hw), row),
                   pl.BlockSpec((1, HEADS_PER_STEP, HEAD_DIM, HEAD_DIM), lambda i, g, t: (i, g, 0, 0))] + cast_specs,
        out_shape=[jax.ShapeDtypeStruct((m, e), BF16),
                   jax.ShapeDtypeStruct((n_batch, nh, HEAD_DIM, HEAD_DIM), F32)]
        + [jax.ShapeDtypeStruct(w.shape, BF16) for w in cast_weights],
        scratch_shapes=[pltpu.VMEM((HEADS_PER_STEP, HEAD_DIM, HEAD_DIM), F32)],
        compiler_params=_cparams(("arbitrary", "arbitrary", "arbitrary")),
        name="wkv_prompt",
    )(rkvz, rkvz, rkvz, rkvz, hw_act, ha_act, w2, w0.reshape(1, e), a2, a0.reshape(1, e),
      k_k.reshape(1, e), k_a.reshape(1, e), r_k.reshape(1, e), gn_g.reshape(1, e), gn_b.reshape(1, e),
      *cast_weights)


def _wkv_sample_kernel(r_ref, k_ref, v_ref, z_ref, wl_ref, al_ref, kk_ref, ka_ref, rk_ref, gg_ref, gb_ref,
                       s_ref, yg_ref, sout_ref, y_scr):
    hd = HEAD_DIM
    r = r_ref[0]
    k = k_ref[0]
    v = v_ref[0]
    a = _sigmoid(al_ref[0])
    d = jnp.exp2((-DECAY_SCALE * LOG2E) * _sigmoid(wl_ref[0]))
    kk = k * kk_ref[...]
    kk = kk / jnp.maximum(jnp.sqrt(jnp.sum(kk * kk, axis=-1, keepdims=True)), 1e-12)
    k2 = k * (1.0 + (a - 1.0) * ka_ref[...])
    bb = kk * a
    nh = r.shape[0]
    ii = lax.broadcasted_iota(jnp.int32, (hd, hd), 0)
    jj = lax.broadcasted_iota(jnp.int32, (hd, hd), 1)
    eye = ii == jj

    row = lambda x, h: x[h:h + 1, :]
    group = 8
    for h0 in range(0, nh, group):
        hs = range(h0, h0 + group)
        s = {h: s_ref[0, h] for h in hs}
        sa = {h: jnp.sum(s[h] * row(kk, h), axis=-1, keepdims=True) for h in hs}
        v_col = {h: jnp.sum(jnp.where(eye, row(v, h), 0.0), axis=-1, keepdims=True) for h in hs}
        s_new = {h: s[h] * row(d, h) - sa[h] * row(bb, h) + v_col[h] * row(k2, h) for h in hs}
        y_col = {h: jnp.sum(s_new[h] * row(r, h), axis=-1, keepdims=True) for h in hs}
        for h in hs:
            sout_ref[0, h] = s_new[h]
            y_scr[h:h + 1, :] = jnp.sum(jnp.where(eye, y_col[h], 0.0), axis=0, keepdims=True)

    y = y_scr[...]
    mean = jnp.mean(y, axis=-1, keepdims=True)
    yc = y - mean
    var = jnp.mean(yc * yc, axis=-1, keepdims=True)
    yn = yc * lax.rsqrt(var + GN_EPS) * gg_ref[...] + gb_ref[...]
    bonus = jnp.sum(r * k2 * rk_ref[...], axis=-1, keepdims=True)
    yg_ref[0] = ((yn + bonus * v) * _silu(z_ref[0])).astype(BF16)


def wkv_sample(rkvz, wl, al, k_k, k_a, r_k, gn_g, gn_b, state):
    _, m, e = rkvz.shape
    nh = e // HEAD_DIM
    hd = HEAD_DIM
    rkvz4 = rkvz.reshape(4, m, nh, hd)
    proj = lambda q: pl.BlockSpec((None, 1, nh, hd), lambda i: (q, i, 0, 0))
    tok = pl.BlockSpec((1, nh, hd), lambda i: (i, 0, 0))
    par = pl.BlockSpec((nh, hd), lambda i: (0, 0))
    st = pl.BlockSpec((1, nh, hd, hd), lambda i: (i, 0, 0, 0))
    as_heads = lambda x: x.reshape(nh, hd)
    return pl.pallas_call(
        _wkv_sample_kernel,
        grid=(m,),
        in_specs=[proj(0), proj(1), proj(2), proj(3), tok, tok, par, par, par, par, par, st],
        out_specs=[tok, st],
        out_shape=[jax.ShapeDtypeStruct((m, nh, hd), BF16), jax.ShapeDtypeStruct(state.shape, F32)],
        scratch_shapes=[pltpu.VMEM((nh, hd), F32)],
        compiler_params=_cparams(("arbitrary",)),
        name="wkv_sample",
    )(rkvz4, rkvz4, rkvz4, rkvz4, wl.reshape(m, nh, hd), al.reshape(m, nh, hd),
      as_heads(k_k), as_heads(k_a), as_heads(r_k), as_heads(gn_g), as_heads(gn_b), state)


def rope_tables(pos):
    half = ROPE_DIM // 2
    inv_freq = ROPE_THETA ** (-jnp.arange(half, dtype=F32) * 2.0 / ROPE_DIM)
    ang = pos.astype(F32)[:, None] * inv_freq[None, :]
    cos = jnp.cos(ang)
    sin = jnp.sin(ang)
    rows = pos.shape[0]
    ones = jnp.ones((rows, HEAD_DIM - ROPE_DIM), F32)
    zeros_h = jnp.zeros((rows, half), F32)
    zeros_r = jnp.zeros((rows, HEAD_DIM - ROPE_DIM), F32)
    cos_h = jnp.concatenate([cos, cos, ones], axis=1)
    sa_h = jnp.concatenate([-sin, zeros_h, zeros_r], axis=1)
    sb_h = jnp.concatenate([zeros_h, sin, zeros_r], axis=1)
    two = lambda t: jnp.concatenate([t, t], axis=1)
    return two(cos_h), two(sa_h), two(sb_h)


def _attn_prompt_kernel(sink_ref, q_ref, kc_ref, kp_ref, vc_ref, vp_ref, z_ref, o_ref, *, first_block):
    n = pl.program_id(1) + first_block
    hd = HEAD_DIM
    blk = q_ref.shape[0]
    n_kv = kc_ref.shape[1] // hd
    grp = q_ref.shape[1] // (n_kv * hd)
    qi = lax.broadcasted_iota(jnp.int32, (blk, 2 * blk), 0)
    kj = lax.broadcasted_iota(jnp.int32, (blk, 2 * blk), 1) - blk
    kpos = n * blk + kj
    diff = qi - kj
    valid = (kpos >= LEAD) & (diff >= 0) & (diff <= WINDOW)
    k_all = jnp.concatenate([kp_ref[...], kc_ref[...]], axis=0).astype(BF16)
    v_all = jnp.concatenate([vp_ref[...], vc_ref[...]], axis=0).astype(BF16)

    def scores(h):
        k_h = k_all[:, h * hd:(h + 1) * hd]
        return [_dot_nt(q_ref[:, (h * grp + gi) * hd:(h * grp + gi + 1) * hd], k_h) for gi in range(grp)]

    outs = []
    s_next = scores(0)
    for h in range(n_kv):
        s_cur = s_next
        if h + 1 < n_kv:
            s_next = scores(h + 1)
        v_h = v_all[:, h * hd:(h + 1) * hd]
        for g0 in range(0, grp, SOFTMAX_BATCH):
            gs = range(g0, min(g0 + SOFTMAX_BATCH, grp))
            sks = {gi: sink_ref[h * grp + gi] * LOG2E for gi in gs}
            ss = {gi: jnp.where(valid, s_cur[gi], -jnp.inf) for gi in gs}
            ms = {gi: jnp.maximum(jnp.max(ss[gi], axis=-1, keepdims=True), sks[gi]) for gi in gs}
            ps = {gi: jnp.exp2(ss[gi] - ms[gi]) for gi in gs}
            dens = {gi: jnp.sum(ps[gi], axis=-1, keepdims=True) + jnp.exp2(sks[gi] - ms[gi]) for gi in gs}
            outs += [_dot(ps[gi].astype(BF16), v_h) / dens[gi] for gi in gs]
    att = jnp.concatenate(outs, axis=1)
    o_ref[...] = (att * _silu(z_ref[...])).astype(o_ref.dtype)


def attn_prompt(sinks, q, k, v, z, n_batch, first_block):
    m, e = q.shape
    nb = m // (n_batch * BLOCK)
    nb_out = nb - first_block
    kw = k.shape[1]
    cur = lambda i, n: (i * nb + n + first_block, 0)
    prv = lambda i, n: (i * nb + jnp.maximum(n + first_block - 1, 0), 0)
    return pl.pallas_call(
        functools.partial(_attn_prompt_kernel, first_block=first_block),
        grid=(n_batch, nb_out),
        in_specs=[pl.BlockSpec(memory_space=pltpu.SMEM),
                  pl.BlockSpec((BLOCK, e), cur),
                  pl.BlockSpec((BLOCK, kw), cur), pl.BlockSpec((BLOCK, kw), prv),
                  pl.BlockSpec((BLOCK, kw), cur), pl.BlockSpec((BLOCK, kw), prv),
                  pl.BlockSpec((BLOCK, e), cur)],
        out_specs=pl.BlockSpec((BLOCK, e), lambda i, n: (i * nb_out + n, 0)),
        out_shape=jax.ShapeDtypeStruct((n_batch * nb_out * BLOCK, e), BF16),
        compiler_params=_cparams(("arbitrary", "arbitrary")),
        name="attn_prompt",
    )(sinks, q, k, k, v, v, z)


def _attn_sample_kernel(sink_ref, q_ref, kc_ref, vc_ref, kn_ref, vn_ref, z_ref, o_ref, ko_ref, vo_ref):
    hd = HEAD_DIM
    win = kc_ref.shape[1]
    n_kv = kc_ref.shape[2] // hd
    nq = q_ref.shape[1]
    grp = nq // n_kv
    pad = 8
    kc = kc_ref[0]
    vc = vc_ref[0]
    kn = kn_ref[0]
    vn = vn_ref[0]
    first = lax.broadcasted_iota(jnp.int32, (pad, kc.shape[1]), 0) == 0
    k_all = jnp.concatenate([kc, jnp.where(first, kn, 0.0)], axis=0).astype(BF16)
    v_all = jnp.concatenate([vc, jnp.where(first, vn, 0.0)], axis=0).astype(BF16)
    col = lax.broadcasted_iota(jnp.int32, (grp, win + pad), 1)
    valid = (col <= win) & (win - col <= WINDOW)
    q = q_ref[0].astype(BF16)
    row_i = lax.broadcasted_iota(jnp.int32, (grp, 1), 0)
    hs = range(n_kv)
    sks = []
    for h in hs:
        sk = jnp.zeros((grp, 1), F32)
        for gi in range(grp):
            sk = jnp.where(row_i == gi, sink_ref[h * grp + gi] * LOG2E, sk)
        sks.append(sk)
    ss = [jnp.where(valid, _dot_nt(q[h * grp:(h + 1) * grp, :], k_all[:, h * hd:(h + 1) * hd]), -jnp.inf)
          for h in hs]
    ms = [jnp.maximum(jnp.max(ss[h], axis=-1, keepdims=True), sks[h]) for h in hs]
    ps = [jnp.exp2(ss[h] - ms[h]) for h in hs]
    dens = [jnp.sum(ps[h], axis=-1, keepdims=True) + jnp.exp2(sks[h] - ms[h]) for h in hs]
    outs = [_dot(ps[h].astype(BF16), v_all[:, h * hd:(h + 1) * hd]) / dens[h] for h in hs]
    att = jnp.concatenate(outs, axis=0)
    o_ref[0] = (att * _silu(z_ref[0])).astype(o_ref.dtype)
    last = lax.broadcasted_iota(jnp.int32, kc.shape, 0) == win - 1
    ko_ref[0] = jnp.where(last, kn, pltpu.roll(kc, win - 1, axis=0))
    vo_ref[0] = jnp.where(last, vn, pltpu.roll(vc, win - 1, axis=0))


def attn_sample(sinks, q, cache_k, cache_v, k_new, v_new, z):
    m, win, kw = cache_k.shape
    nq = q.shape[1]
    hd = HEAD_DIM
    tok = pl.BlockSpec((1, nq, hd), lambda i: (i, 0, 0))
    cache = pl.BlockSpec((1, win, kw), lambda i: (i, 0, 0))
    new = pl.BlockSpec((1, 1, kw), lambda i: (i, 0, 0))
    return pl.pallas_call(
        _attn_sample_kernel,
        grid=(m,),
        in_specs=[pl.BlockSpec(memory_space=pltpu.SMEM), tok, cache, cache, new, new, tok],
        out_specs=[tok, cache, cache],
        out_shape=[jax.ShapeDtypeStruct((m, nq, hd), BF16),
                   jax.ShapeDtypeStruct(cache_k.shape, F32), jax.ShapeDtypeStruct(cache_v.shape, F32)],
        compiler_params=_cparams(("arbitrary",)),
        name="attn_sample",
    )(sinks, q, cache_k, cache_v, k_new, v_new, z)


def _pad_lora(w_down, w_up):
    r = w_down.shape[1]
    return (jnp.pad(w_down, ((0, 0), (0, LORA_PAD - r))).astype(BF16),
            jnp.pad(w_up, ((0, LORA_PAD - r), (0, 0))).astype(BF16))


def kernel(x_prompt, x_sample, state_wkv, state_shift, cache_k, cache_v, meta_tokens, a_norm, a_mu, a_w_rkvz,
           a_w0, a_w1, a_w2, a_a0, a_a1, a_a2, a_k_k, a_k_a, a_r_k, a_gn_g, a_gn_b, a_w_out, kv_norm, w_kv,
           b_norm, b_w_qz, b_sinks, b_w_o, final_norm):
    nb, seq, d = x_prompt.shape
    db, dseq, _ = x_sample.shape
    assert dseq == 1 and a_norm.shape[0] == 1 and b_norm.shape[0] == 1
    e = a_w_rkvz.shape[3]
    win = cache_k.shape[1]
    p_len = LEAD + N_META + seq
    assert p_len % BLOCK == 0 and (LEAD + N_META) == BLOCK
    kvw = N_KV_HEADS * HEAD_DIM

    w1, w2 = _pad_lora(a_w1[0], a_w2[0])
    a1, a2 = _pad_lora(a_a1[0], a_a2[0])
    mu = a_mu[0]
    sinks = b_sinks[0]
    gains_b = jnp.stack([kv_norm, b_norm[0]])

    tm = p_len // 8

    head = jnp.concatenate([jnp.zeros((LEAD, d), F32), meta_tokens], axis=0)
    w4 = a_w_rkvz[0]
    xm, hw_p, ha_p, x_last, w_rkvz = norm_shift_prompt(x_prompt, head, a_norm[0], mu, w1, a1,
                                                      cast_weights=(w4.reshape(-1, w4.shape[2]),))
    w_rkvz = w_rkvz.reshape(w4.shape)
    p_state_shift = x_last.reshape(1, nb, d)
    rkvz = matmul_groups(xm, w_rkvz, tm, F32)
    yg, p_state, w_out, w_kv_bf, w_qz, w_o = wkv_prompt(
        rkvz, hw_p, ha_p, w2, a_w0[0], a2, a_a0[0], a_k_k[0], a_k_a[0], a_r_k[0].reshape(-1), a_gn_g[0],
        a_gn_b[0], nb, cast_weights=(a_w_out[0], w_kv, b_w_qz[0], b_w_o[0]))
    w_qz = w_qz[None]
    hp, hn_kv, hn_b = matmul_residual_norm_blocks(yg, w_out, x_prompt, gains_b, nb, BF16, True, head=head)

    pos_p = jnp.maximum(jnp.arange(p_len, dtype=jnp.int32) - LEAD, 0)
    tabs_p = rope_tables(pos_p)
    k_p, v_p = matmul_rope(hn_kv, w_kv_bf, tabs_p, tm, kvw, (F32, F32))
    q_p, = matmul_rope(hn_b, w_qz, tabs_p, tm, e, (BF16,), scale=Q_SCALE, n=e)
    z_p = matmul_groups(hn_b[None], w_qz, tm, F32, n=e, col=1)[0]
    skip = (LEAD + N_META) // BLOCK
    att = attn_prompt(sinks, q_p, k_p, v_p, z_p, nb, skip)
    y_prompt, = matmul_residual_norm_blocks(att, w_o, hp, final_norm[None], nb, F32, False, first_block=skip)
    y_prompt = y_prompt.reshape(nb, seq, d)
    tail = lambda t: t.reshape(nb, p_len, kvw)[:, -win:].reshape(nb, win, N_KV_HEADS, HEAD_DIM)
    p_cache_k = tail(k_p)
    p_cache_v = tail(v_p)

    hs = x_sample.reshape(db, d)
    xm_s, hw_s, ha_s, xn_s = norm_shift_sample(hs, state_shift[0], a_norm[0], mu, w1, a1)
    rkvz_s = matmul_groups(xm_s, w_rkvz, db, F32)
    wl_s, al_s = lora_up(hw_s, ha_s, w2, a_w0[0], a2, a_a0[0])
    yg_s, s_state = wkv_sample(rkvz_s, wl_s, al_s, a_k_k[0], a_k_a[0], a_r_k[0].reshape(-1), a_gn_g[0],
                               a_gn_b[0], state_wkv[0])
    hs, hn_kv_s, hn_b_s = matmul_residual_norm(yg_s.reshape(db, e), w_out, hs, gains_b, db, BF16)
    tabs_s = rope_tables(jnp.full((db,), PAST_LEN, jnp.int32))
    k_s, v_s = matmul_rope(hn_kv_s, w_kv_bf, tabs_s, db, kvw, (F32, F32))
    q_s, = matmul_rope(hn_b_s, w_qz, tabs_s, db, e, (F32,), scale=Q_SCALE, n=e)
    z_s = matmul_groups(hn_b_s[None], w_qz, db, F32, n=e, col=1)[0]
    nq = e // HEAD_DIM
    att_s, s_cache_k, s_cache_v = attn_sample(
        sinks, q_s.reshape(db, nq, HEAD_DIM), cache_k.reshape(db, win, kvw), cache_v.reshape(db, win, kvw),
        k_s.reshape(db, 1, kvw), v_s.reshape(db, 1, kvw), z_s.reshape(db, nq, HEAD_DIM))
    y_s, = matmul_residual_norm(att_s.reshape(db, e), w_o, hs, final_norm[None], db, F32, emit_h=False)
    y_sample = y_s.reshape(db, 1, d)

    return (y_prompt, y_sample, p_state[None], p_state_shift,
            p_cache_k, p_cache_v,
            s_state[None], xn_s[None],
            s_cache_k.reshape(cache_k.shape), s_cache_v.reshape(cache_v.shape))
```

```python
import functools
import math

import jax
import jax.numpy as jnp
from jax import lax
from jax.experimental import pallas as pl
from jax.experimental.pallas import tpu as pltpu

F32 = jnp.float32
BF16 = jnp.bfloat16

HEAD_DIM = 64
N_KV_HEADS = 8
WINDOW = 128
BLOCK = 128
ROPE_DIM = HEAD_DIM // 4
ROPE_THETA = 500000.0
N_META = 16
PAST_LEN = 16384
RMS_EPS = 1e-6
GN_EPS = 64e-5
NORM_FLOOR_SQ = 1e-24
LEAD = (-N_META) % BLOCK
CHUNK = 64
WKV_ROWS = 128
CAST_CHUNKS = 32
SEQS_PER_STEP = 2
HEADS_PER_STREAM = 32
HEADS_PER_STEP = 32
LORA_PAD = 128
MXU_TILE = 256
ROPE_SLAB = 512
VMEM_LIMIT = 48 * 1024 * 1024
LOG2E = 1.0 / math.log(2.0)
DECAY_SCALE = math.exp(-0.5)
SOFTMAX_BATCH = 4
Q_SCALE = HEAD_DIM ** -0.5 * LOG2E


def _cparams(sem):
    return pltpu.CompilerParams(dimension_semantics=sem, vmem_limit_bytes=VMEM_LIMIT)


def _sigmoid(x):
    return 1.0 / (1.0 + jnp.exp2(x * (-LOG2E)))


def _silu(x):
    return x * _sigmoid(x)


def _dot(a, b):
    return jnp.dot(a, b, preferred_element_type=F32)


def _dot_nt(a, b):
    return lax.dot_general(a, b, (((1,), (1,)), ((), ())), preferred_element_type=F32)


def _dot_tn(a, b):
    return lax.dot_general(a, b, (((0,), (0,)), ((), ())), preferred_element_type=F32)


def _split_hi_lo(x):
    hi = x.astype(BF16)
    lo = (x - hi.astype(F32)).astype(BF16)
    return hi, lo


def _cast_job(weights, n_steps, step_of):
    n_chunks = min(CAST_CHUNKS, n_steps)
    every = n_steps // n_chunks
    assert all(w.shape[0] % (n_chunks * 16) == 0 for w in weights)
    chunk = lambda *idx: (jnp.minimum(step_of(*idx) // every, n_chunks - 1), 0)
    return [pl.BlockSpec((w.shape[0] // n_chunks, w.shape[1]), chunk) for w in weights], every


def _cast_chunks(step, every, srcs, dsts):
    @pl.when(step % every == 0)
    def _():
        for src, dst in zip(srcs, dsts):
            dst[...] = src[...].astype(dst.dtype)


def _mixes(xn, prev, mu_ref, w1_ref, a1_ref, xm_ref, hw_ref, ha_ref):
    xx = prev - xn
    n_proj = xm_ref.shape[0]
    for p in range(n_proj):
        xm_ref[p] = (xn + xx * mu_ref[p:p + 1, :]).astype(xm_ref.dtype)
    xw = (xn + xx * mu_ref[n_proj:n_proj + 1, :]).astype(BF16)
    xa = (xn + xx * mu_ref[n_proj + 1:n_proj + 2, :]).astype(BF16)
    hw_ref[...] = jnp.tanh(_dot(xw, w1_ref[...])).astype(hw_ref.dtype)
    ha_ref[...] = _dot(xa, a1_ref[...]).astype(ha_ref.dtype)


def _norm_shift_kernel(x_ref, head_ref, g_ref, mu_ref, w1_ref, a1_ref, *rest, n_cast, cast_every):
    cast_in = rest[:n_cast]
    xm_ref, hw_ref, ha_ref, last_ref = rest[n_cast:n_cast + 4]
    cast_out = rest[n_cast + 4:2 * n_cast + 4]
    carry_ref = rest[-1]

    @pl.when(pl.program_id(1) == 0)
    def _():
        carry_ref[...] = jnp.zeros_like(carry_ref)

    _cast_chunks(pl.program_id(0) * pl.num_programs(1) + pl.program_id(1), cast_every, cast_in, cast_out)

    is_head = pl.program_id(1) == 0
    for sq in range(x_ref.shape[0]):
        x = jnp.where(is_head, head_ref[...], x_ref[sq])
        tm = x.shape[0]
        xn = x * lax.rsqrt(jnp.mean(x * x, axis=-1, keepdims=True) + RMS_EPS) * g_ref[...]
        rolled = pltpu.roll(xn, 1, axis=0)
        row = lax.broadcasted_iota(jnp.int32, xn.shape, 0)
        prev = jnp.where(row == 0, carry_ref[sq, 0:1, :], rolled)
        _mixes(xn, prev, mu_ref, w1_ref, a1_ref, xm_ref.at[:, sq], hw_ref.at[sq], ha_ref.at[sq])
        carry_ref[sq, 0:1, :] = xn[tm - 1:tm, :]
        last_ref[sq] = xn[tm - 1:tm, :]


def norm_shift_prompt(x, head, g, mu, w1, a1, cast_weights=()):
    b, seq, d = x.shape
    tm = BLOCK
    p = tm + seq
    n_mix = mu.shape[0]
    n_proj = n_mix - 2
    lr = w1.shape[1]
    nt = p // tm
    sp = SEQS_PER_STEP
    const = lambda shape: pl.BlockSpec(shape, lambda i, t: (0,) * len(shape))
    hid = pl.BlockSpec((sp, tm, lr), lambda i, t: (i, t, 0))
    cast_specs, every = _cast_job(cast_weights, (b // sp) * nt, lambda i, t: i * nt + t)
    xm, hw_act, ha_act, last, *casts = pl.pallas_call(
        functools.partial(_norm_shift_kernel, n_cast=len(cast_weights), cast_every=every),
        grid=(b // sp, nt),
        in_specs=[pl.BlockSpec((sp, tm, d), lambda i, t: (i, jnp.maximum(t - 1, 0), 0)),
                  const((tm, d)), const((1, d)), const((n_mix, d)), const((d, lr)), const((d, lr))] + cast_specs,
        out_specs=[pl.BlockSpec((n_proj, sp, tm, d), lambda i, t: (0, i, t, 0)), hid, hid,
                   pl.BlockSpec((sp, 1, d), lambda i, t: (i, 0, 0))] + cast_specs,
        out_shape=[jax.ShapeDtypeStruct((n_proj, b, p, d), BF16),
                   jax.ShapeDtypeStruct((b, p, lr), BF16), jax.ShapeDtypeStruct((b, p, lr), BF16),
                   jax.ShapeDtypeStruct((b, 1, d), F32)] + [jax.ShapeDtypeStruct(w.shape, BF16) for w in cast_weights],
        scratch_shapes=[pltpu.VMEM((sp, 8, d), F32)],
        compiler_params=_cparams(("arbitrary", "arbitrary")),
        name="norm_shift_prompt",
    )(x, head, g.reshape(1, d), mu, w1, a1, *cast_weights)
    return (xm.reshape(n_proj, b * p, d), hw_act.reshape(b * p, lr), ha_act.reshape(b * p, lr), last, *casts)


def _norm_shift_sample_kernel(x_ref, prev_ref, g_ref, mu_ref, w1_ref, a1_ref, xm_ref, hw_ref, ha_ref, xn_ref):
    x = x_ref[...]
    xn = x * lax.rsqrt(jnp.mean(x * x, axis=-1, keepdims=True) + RMS_EPS) * g_ref[...]
    xn_ref[...] = xn
    _mixes(xn, prev_ref[...], mu_ref, w1_ref, a1_ref, xm_ref, hw_ref, ha_ref)


def norm_shift_sample(x, prev, g, mu, w1, a1):
    m, d = x.shape
    lr = w1.shape[1]
    return pl.pallas_call(
        _norm_shift_sample_kernel,
        out_shape=[jax.ShapeDtypeStruct((mu.shape[0] - 2, m, d), BF16),
                   jax.ShapeDtypeStruct((m, lr), BF16), jax.ShapeDtypeStruct((m, lr), BF16),
                   jax.ShapeDtypeStruct((m, d), F32)],
        name="norm_shift_sample",
    )(x, prev, g.reshape(1, d), mu, w1, a1)


def _rope(y, cos, sin_a, sin_b):
    half = ROPE_DIM // 2
    step = ROPE_SLAB
    rep = step // cos.shape[1]
    tile = lambda t: jnp.concatenate([t] * rep, axis=1)
    cos_t, sa_t, sb_t = tile(cos), tile(sin_a), tile(sin_b)
    outs = []
    for j in range(y.shape[1] // step):
        ys = y[:, j * step:(j + 1) * step]
        outs.append(ys * cos_t + pltpu.roll(ys, step - half, axis=1) * sa_t + pltpu.roll(ys, half, axis=1) * sb_t)
    return jnp.concatenate(outs, axis=1) if len(outs) > 1 else outs[0]


def _mm_group_kernel(x_ref, w_ref, o_ref):
    o_ref[0] = _dot(x_ref[0], w_ref[0]).astype(o_ref.dtype)


def matmul_groups(x, w, tm, out_dtype, n=None, col=0, tn=None):
    g = w.shape[0]
    n = w.shape[2] if n is None else n
    tn = n if tn is None else tn
    nt = n // tn
    _, m, kdim = x.shape
    return pl.pallas_call(
        _mm_group_kernel,
        grid=(g, nt, m // tm),
        in_specs=[pl.BlockSpec((1, tm, kdim), lambda q, j, i: (q, i, 0)),
                  pl.BlockSpec((1, kdim, tn), lambda q, j, i: (q, 0, col * nt + j))],
        out_specs=pl.BlockSpec((1, tm, tn), lambda q, j, i: (q, i, j)),
        out_shape=jax.ShapeDtypeStruct((g, m, n), out_dtype),
        compiler_params=_cparams(("arbitrary", "arbitrary", "arbitrary")),
        name="matmul_groups",
    )(x, w)


def _mm_rope_kernel(x_ref, w_ref, cos_ref, sa_ref, sb_ref, *o_refs, n_rope, scale):
    y = _dot(x_ref[...], w_ref[...])
    rot = _rope(y[:, :n_rope], cos_ref[...], sa_ref[...], sb_ref[...])
    if scale != 1.0:
        rot = rot * scale
    o_refs[0][...] = rot.astype(o_refs[0].dtype)
    if len(o_refs) > 1:
        o_refs[1][...] = y[:, n_rope:].astype(o_refs[1].dtype)


def matmul_rope(x, w, tables, tm, n_rope, out_dtypes, scale=1.0, n=None):
    m, kdim = x.shape
    n = w.shape[-1] if n is None else n
    w_block = (kdim, n) if w.ndim == 2 else (None, kdim, n)
    lanes = tables[0].shape[1]
    widths = [n_rope] + ([n - n_rope] if n > n_rope else [])
    tab_blocks = tables[0].shape[0] // tm
    tab = pl.BlockSpec((tm, lanes), lambda i: (i % tab_blocks, 0))
    outs = pl.pallas_call(
        functools.partial(_mm_rope_kernel, n_rope=n_rope, scale=scale),
        grid=(m // tm,),
        in_specs=[pl.BlockSpec((tm, kdim), lambda i: (i, 0)),
                  pl.BlockSpec(w_block, lambda i: (0,) * w.ndim, pipeline_mode=pl.Buffered(1)),
                  tab, tab, tab],
        out_specs=[pl.BlockSpec((tm, wd), lambda i: (i, 0)) for wd in widths],
        out_shape=[jax.ShapeDtypeStruct((m, wd), dt) for wd, dt in zip(widths, out_dtypes)],
        compiler_params=_cparams(("arbitrary",)),
        name="matmul_rope",
    )(x, w, *tables)
    return outs


def _mm_res_norm_kernel(x_ref, w_ref, res_ref, g_ref, *out_refs, emit_h):
    h = res_ref[...] + _dot(x_ref[...], w_ref[...])
    hn_refs = out_refs
    if emit_h:
        out_refs[0][...] = h
        hn_refs = out_refs[1:]
    inv = lax.rsqrt(jnp.mean(h * h, axis=-1, keepdims=True) + RMS_EPS)
    for j, hn_ref in enumerate(hn_refs):
        hn_ref[...] = (h * inv * g_ref[j:j + 1, :]).astype(hn_ref.dtype)


def matmul_residual_norm(x, w, res, gains, tm, norm_dtype, emit_h=True):
    m, kdim = x.shape
    n = w.shape[1]
    ng = gains.shape[0]
    row = lambda width: pl.BlockSpec((tm, width), lambda i: (i, 0))
    return pl.pallas_call(
        functools.partial(_mm_res_norm_kernel, emit_h=emit_h),
        grid=(m // tm,),
        in_specs=[row(kdim),
                  pl.BlockSpec((kdim, n), lambda i: (0, 0), pipeline_mode=pl.Buffered(1)),
                  row(n),
                  pl.BlockSpec((ng, n), lambda i: (0, 0))],
        out_specs=[row(n)] * (int(emit_h) + ng),
        out_shape=[jax.ShapeDtypeStruct((m, n), F32)] * int(emit_h) + [jax.ShapeDtypeStruct((m, n), norm_dtype)] * ng,
        compiler_params=_cparams(("arbitrary",)),
        name="matmul_residual_norm",
    )(x, w, res, gains)


def _mm_res_norm_blocks_kernel(x_ref, w_ref, res_ref, *rest, emit_h, head, first_block):
    if head:
        head_ref, g_ref, *out_refs = rest
        res = jnp.where(pl.program_id(1) + first_block == 0, head_ref[...][None], res_ref[...])
    else:
        g_ref, *out_refs = rest
        res = res_ref[...]
    seqs, blk, n = res.shape
    h = res.reshape(seqs * blk, n) + _dot(x_ref[...].reshape(seqs * blk, x_ref.shape[2]), w_ref[...])
    hn_refs = out_refs
    if emit_h:
        out_refs[0][...] = h.reshape(seqs, blk, n)
        hn_refs = out_refs[1:]
    inv = lax.rsqrt(jnp.mean(h * h, axis=-1, keepdims=True) + RMS_EPS)
    for j, hn_ref in enumerate(hn_refs):
        hn_ref[...] = (h * inv * g_ref[j:j + 1, :]).astype(hn_ref.dtype).reshape(seqs, blk, n)


def matmul_residual_norm_blocks(x, w, res, gains, n_batch, norm_dtype, emit_h, head=None, first_block=0):
    kdim = x.shape[1]
    n = w.shape[1]
    ng = gains.shape[0]
    nb_out = x.shape[0] // (n_batch * BLOCK)
    nb = nb_out + first_block
    sp = SEQS_PER_STEP
    blocks = lambda width, shift: pl.BlockSpec((sp, BLOCK, width), lambda i, t: (i, t + shift, 0))
    const = lambda shape: pl.BlockSpec(shape, lambda i, t: (0,) * len(shape))
    if head is not None:
        res_specs = [pl.BlockSpec((sp, BLOCK, n), lambda i, t: (i, jnp.maximum(t + first_block - 1, 0), 0)),
                     const((BLOCK, n))]
        res_args = [res, head]
    else:
        res_specs, res_args = [blocks(n, first_block)], [res.reshape(n_batch, nb * BLOCK, n)]
    outs = pl.pallas_call(
        functools.partial(_mm_res_norm_blocks_kernel, emit_h=emit_h, head=head is not None, first_block=first_block),
        grid=(n_batch // sp, nb_out),
        in_specs=[blocks(kdim, 0), pl.BlockSpec((kdim, n), lambda i, t: (0, 0), pipeline_mode=pl.Buffered(1))]
        + res_specs + [const((ng, n))],
        out_specs=[blocks(n, first_block)] * int(emit_h) + [blocks(n, 0)] * ng,
        out_shape=[jax.ShapeDtypeStruct((n_batch, nb * BLOCK, n), F32)] * int(emit_h)
        + [jax.ShapeDtypeStruct((n_batch, nb_out * BLOCK, n), norm_dtype)] * ng,
        compiler_params=_cparams(("arbitrary", "arbitrary")),
        name="matmul_residual_norm_blocks",
    )(x.reshape(n_batch, nb_out * BLOCK, kdim), w, *res_args, gains)
    return [o.reshape(-1, n) for o in outs]


def _lora_up_kernel(hw_ref, ha_ref, w2_ref, w0_ref, a2_ref, a0_ref, wl_ref, al_ref):
    wl_ref[...] = w0_ref[...] + _dot(hw_ref[...], w2_ref[...])
    al_ref[...] = a0_ref[...] + _dot(ha_ref[...], a2_ref[...])


def lora_up(hw, ha, w2, w0, a2, a0):
    m = hw.shape[0]
    e = w2.shape[1]
    return pl.pallas_call(
        _lora_up_kernel,
        out_shape=[jax.ShapeDtypeStruct((m, e), F32), jax.ShapeDtypeStruct((m, e), F32)],
        name="lora_up",
    )(hw, ha, w2, w0.reshape(1, e), a2, a0.reshape(1, e))


def _seg_sum(x, ones_bd):
    hi = x.astype(BF16)
    outs = []
    for c in range(x.shape[1] // MXU_TILE):
        sl = slice(c * MXU_TILE, (c + 1) * MXU_TILE)
        outs.append(_dot(hi[:, sl], ones_bd))
    return jnp.concatenate(outs, axis=1) if len(outs) > 1 else outs[0]


def _wkv_prompt_kernel(r_ref, k_ref, v_ref, z_ref, hw_ref, ha_ref, w2_ref, w0_ref, a2_ref, a0_ref,
                       kk_ref, ka_ref, rk_ref, gg_ref, gb_ref, *rest, n_cast, cast_every):
    cast_in = rest[:n_cast]
    yg_ref, sout_ref = rest[n_cast:n_cast + 2]
    cast_out = rest[n_cast + 2:2 * n_cast + 2]
    s_ref = rest[-1]
    t_idx = pl.program_id(2)
    c = CHUNK
    hd = HEAD_DIM

    @pl.when(t_idx == 0)
    def _():
        s_ref[...] = jnp.zeros_like(s_ref)

    step = (pl.program_id(0) * pl.num_programs(1) + pl.program_id(1)) * pl.num_programs(2) + t_idx
    _cast_chunks(step, cast_every, cast_in, cast_out)

    tb = r_ref.shape[1]
    nh = HEADS_PER_STREAM
    hw = nh * hd
    nc = tb // c

    li = lax.broadcasted_iota(jnp.int32, (MXU_TILE, MXU_TILE), 0) // hd
    lj = lax.broadcasted_iota(jnp.int32, (MXU_TILE, MXU_TILE), 1) // hd
    ones_bd = jnp.where(li == lj, 1.0, 0.0).astype(BF16)
    bi_ = lax.broadcasted_iota(jnp.int32, (tb, tb), 0)
    bj_ = lax.broadcasted_iota(jnp.int32, (tb, tb), 1)
    tri_incl = jnp.where((bj_ <= bi_) & (bj_ // c == bi_ // c), 1.0, 0.0).astype(BF16)
    ti = lax.broadcasted_iota(jnp.int32, (c, c), 0)
    tj = lax.broadcasted_iota(jnp.int32, (c, c), 1)
    ai = lax.broadcasted_iota(jnp.int32, (c, 2 * c), 0)
    aj = lax.broadcasted_iota(jnp.int32, (c, 2 * c), 1)
    upper = aj >= c
    aj_mod = jnp.where(upper, aj - c, aj)
    masks = dict(
        strict=tj < ti,
        eye=jnp.where(ti == tj, 1.0, 0.0).astype(F32),
        top_k=upper & (aj_mod < ai),
        bot=aj_mod <= ai,
        mean_bd=jnp.where(li == lj, 1.0 / hd, 0.0).astype(BF16))

    for st in range(r_ref.shape[2] // hw):
        _wkv_stream(st, hw, nc, r_ref, k_ref, v_ref, z_ref, hw_ref, ha_ref, w2_ref, w0_ref, a2_ref, a0_ref,
                    kk_ref, ka_ref, rk_ref, gg_ref, gb_ref, yg_ref, s_ref, ones_bd, tri_incl, masks)

    @pl.when(t_idx == pl.num_programs(2) - 1)
    def _():
        sout_ref[0] = s_ref[...]


def _wkv_stream(st, hw, nc, r_ref, k_ref, v_ref, z_ref, hw_ref, ha_ref, w2_ref, w0_ref, a2_ref, a0_ref,
                kk_ref, ka_ref, rk_ref, gg_ref, gb_ref, yg_ref, s_ref, ones_bd, tri_incl, masks):
    c = CHUNK
    hd = HEAD_DIM
    nh = hw // hd
    ls = slice(st * hw, (st + 1) * hw)
    h0 = st * nh
    r = r_ref[0, :, ls]
    k = k_ref[0, :, ls]
    v = v_ref[0, :, ls]
    wl = w0_ref[:, ls] + _dot(hw_ref[...], w2_ref[:, ls])
    al = a0_ref[:, ls] + _dot(ha_ref[...], a2_ref[:, ls])
    a = _sigmoid(al)
    lw = (-DECAY_SCALE * LOG2E) * _sigmoid(wl)
    kk = k * kk_ref[:, ls]
    n2 = _seg_sum(kk * kk, ones_bd)
    kk = kk * lax.rsqrt(jnp.maximum(n2, NORM_FLOOR_SQ))
    ka = ka_ref[:, ls]
    k2 = k * (a * ka + (1.0 - ka))
    bb = kk * a

    lw_hi, lw_lo = _split_hi_lo(lw)
    g = _dot(tri_incl, lw_hi) + _dot(tri_incl, lw_lo)
    mid = lambda ci: g[ci * c + c // 2 - 1:ci * c + c // 2, :]
    gm = jnp.concatenate([jnp.broadcast_to(mid(ci), (c, hw)) for ci in range(nc)], axis=0)
    t = g - gm
    e_a = jnp.exp2(t)
    e_prev = jnp.exp2(t - lw)
    e_inv = jnp.exp2(-t)
    e1 = [jnp.exp2(mid(ci)) for ci in range(nc)]
    e2 = [jnp.exp2(g[ci * c + c - 1:ci * c + c, :] - mid(ci)) for ci in range(nc)]

    kkd = (kk * e_prev).astype(BF16)
    rd = (r * e_a).astype(BF16)
    bi = (bb * e_inv).astype(BF16)
    ki = (k2 * e_inv).astype(BF16)
    v_bf = v.astype(BF16)
    zeros_cv = jnp.zeros((c, hd), BF16)

    pairs = [(ci, h) for ci in range(nc) for h in range(nh)]
    rows = lambda ci: slice(ci * c, (ci + 1) * c)
    cols = lambda h: slice(h * hd, (h + 1) * hd)
    xs = {(ci, h): jnp.concatenate([kkd[rows(ci), cols(h)], rd[rows(ci), cols(h)]], axis=0) for ci, h in pairs}
    r1s = {(ci, h): jnp.concatenate([bi[rows(ci), cols(h)], ki[rows(ci), cols(h)]], axis=0) for ci, h in pairs}
    vs = {(ci, h): v_bf[rows(ci), cols(h)] for ci, h in pairs}
    a_mats = {p: _dot_nt(xs[p], r1s[p]) for p in pairs}
    lk_vs = {p: _dot(jnp.where(masks["top_k"], a_mats[p][:c, :], 0.0).astype(BF16),
                     jnp.concatenate([zeros_cv, vs[p]], axis=0)) for p in pairs}
    lps = {p: jnp.where(masks["strict"], a_mats[p][:c, :c], 0.0) for p in pairs}
    ts = {p: masks["eye"] - lps[p] for p in pairs}
    for _ in range(int(math.log2(c)) - 1):
        lpb = {p: lps[p].astype(BF16) for p in pairs}
        lps = {p: _dot(lpb[p], lpb[p]) for p in pairs}
        ts = {p: _dot(ts[p].astype(BF16), (masks["eye"] + lps[p]).astype(BF16)) for p in pairs}
    a_bots = {p: jnp.where(masks["bot"], a_mats[p][c:, :], 0.0).astype(BF16) for p in pairs}
    t_bf = {p: ts[p].astype(BF16) for p in pairs}

    state = [s_ref[h0 + h] for h in range(nh)]
    y_rows = []
    for ci in range(nc):
        hs = range(nh)
        sms = [state[h] * e1[ci][:, cols(h)] for h in hs]
        p_mats = [_dot_nt(xs[ci, h], sms[h].astype(BF16)) for h in hs]
        us = [-_dot(t_bf[ci, h], (p_mats[h][:c, :] + lk_vs[ci, h]).astype(BF16)) for h in hs]
        uvs = [jnp.concatenate([us[h].astype(BF16), vs[ci, h]], axis=0) for h in hs]
        ys = [p_mats[h][c:, :] + _dot(a_bots[ci, h], uvs[h]) for h in hs]
        state = [(sms[h] + _dot_tn(uvs[h], r1s[ci, h])) * e2[ci][:, cols(h)] for h in hs]
        y_rows.append(jnp.concatenate(ys, axis=1))
    for h in range(nh):
        s_ref[h0 + h] = state[h]
    y = jnp.concatenate(y_rows, axis=0) if nc > 1 else y_rows[0]

    mean = _seg_sum(y, masks["mean_bd"])
    yc = y - mean
    var = _seg_sum(yc * yc, masks["mean_bd"])
    yn = yc * lax.rsqrt(var + GN_EPS) * gg_ref[:, ls] + gb_ref[:, ls]
    bonus = _seg_sum(r * k2 * rk_ref[:, ls], ones_bd)
    yg_ref[:, ls] = ((yn + bonus * v) * _silu(z_ref[0, :, ls])).astype(BF16)


def wkv_prompt(rkvz, hw_act, ha_act, w2, w0, a2, a0, k_k, k_a, r_k, gn_g, gn_b, n_batch, cast_weights=()):
    _, m, e = rkvz.shape
    p = m // n_batch
    tb = WKV_ROWS
    nt = p // tb
    hw = HEADS_PER_STEP * HEAD_DIM
    nh = e // HEAD_DIM
    row = lambda i, g, t: (i * nt + t, g)
    proj = lambda q: pl.BlockSpec((1, tb, hw), lambda i, g, t: (q, i * nt + t, g))
    par = pl.BlockSpec((1, hw), lambda i, g, t: (0, g))
    lr = w2.shape[0]
    hid = pl.BlockSpec((tb, lr), lambda i, g, t: (i * nt + t, 0))
    up = pl.BlockSpec((lr, hw), lambda i, g, t: (0, g))
    n_groups = e // hw
    cast_specs, every = _cast_job(cast_weights, n_batch * n_groups * nt, lambda i, g, t: (i * n_groups + g) * nt + t)
    return pl.pallas_call(
        functools.partial(_wkv_prompt_kernel, n_cast=len(cast_weights), cast_every=every),
        grid=(n_batch, n_groups, nt),
        in_specs=[proj(0), proj(1), proj(2), proj(3), hid, hid, up, par, up, par,
                  par, par, par, par, par] + cast_specs,
        out_specs=[pl.BlockSpec((tb, hw), row),
                   pl.BlockSpec((1, HEADS_PER_STEP, HEAD_DIM, HEAD_DIM), lambda i, g, t: (i, g, 0, 0))] + cast_specs,
        out_shape=[jax.ShapeDtypeStruct((m, e), BF16),
                   jax.ShapeDtypeStruct((n_batch, nh, HEAD_DIM, HEAD_DIM), F32)]
        + [jax.ShapeDtypeStruct(w.shape, BF16) for w in cast_weights],
        scratch_shapes=[pltpu.VMEM((HEADS_PER_STEP, HEAD_DIM, HEAD_DIM), F32)],
        compiler_params=_cparams(("arbitrary", "arbitrary", "arbitrary")),
        name="wkv_prompt",
    )(rkvz, rkvz, rkvz, rkvz, hw_act, ha_act, w2, w0.reshape(1, e), a2, a0.reshape(1, e),
      k_k.reshape(1, e), k_a.reshape(1, e), r_k.reshape(1, e), gn_g.reshape(1, e), gn_b.reshape(1, e),
      *cast_weights)


def _wkv_sample_kernel(r_ref, k_ref, v_ref, z_ref, wl_ref, al_ref, kk_ref, ka_ref, rk_ref, gg_ref, gb_ref,
                       s_ref, yg_ref, sout_ref, y_scr):
    hd = HEAD_DIM
    r = r_ref[0]
    k = k_ref[0]
    v = v_ref[0]
    a = _sigmoid(al_ref[0])
    d = jnp.exp2((-DECAY_SCALE * LOG2E) * _sigmoid(wl_ref[0]))
    kk = k * kk_ref[...]
    kk = kk / jnp.maximum(jnp.sqrt(jnp.sum(kk * kk, axis=-1, keepdims=True)), 1e-12)
    k2 = k * (1.0 + (a - 1.0) * ka_ref[...])
    bb = kk * a
    nh = r.shape[0]
    ii = lax.broadcasted_iota(jnp.int32, (hd, hd), 0)
    jj = lax.broadcasted_iota(jnp.int32, (hd, hd), 1)
    eye = ii == jj

    row = lambda x, h: x[h:h + 1, :]
    group = 8
    for h0 in range(0, nh, group):
        hs = range(h0, h0 + group)
        s = {h: s_ref[0, h] for h in hs}
        sa = {h: jnp.sum(s[h] * row(kk, h), axis=-1, keepdims=True) for h in hs}
        v_col = {h: jnp.sum(jnp.where(eye, row(v, h), 0.0), axis=-1, keepdims=True) for h in hs}
        s_new = {h: s[h] * row(d, h) - sa[h] * row(bb, h) + v_col[h] * row(k2, h) for h in hs}
        y_col = {h: jnp.sum(s_new[h] * row(r, h), axis=-1, keepdims=True) for h in hs}
        for h in hs:
            sout_ref[0, h] = s_new[h]
            y_scr[h:h + 1, :] = jnp.sum(jnp.where(eye, y_col[h], 0.0), axis=0, keepdims=True)

    y = y_scr[...]
    mean = jnp.mean(y, axis=-1, keepdims=True)
    yc = y - mean
    var = jnp.mean(yc * yc, axis=-1, keepdims=True)
    yn = yc * lax.rsqrt(var + GN_EPS) * gg_ref[...] + gb_ref[...]
    bonus = jnp.sum(r * k2 * rk_ref[...], axis=-1, keepdims=True)
    yg_ref[0] = ((yn + bonus * v) * _silu(z_ref[0])).astype(BF16)


def wkv_sample(rkvz, wl, al, k_k, k_a, r_k, gn_g, gn_b, state):
    _, m, e = rkvz.shape
    nh = e // HEAD_DIM
    hd = HEAD_DIM
    rkvz4 = rkvz.reshape(4, m, nh, hd)
    proj = lambda q: pl.BlockSpec((None, 1, nh, hd), lambda i: (q, i, 0, 0))
    tok = pl.BlockSpec((1, nh, hd), lambda i: (i, 0, 0))
    par = pl.BlockSpec((nh, hd), lambda i: (0, 0))
    st = pl.BlockSpec((1, nh, hd, hd), lambda i: (i, 0, 0, 0))
    as_heads = lambda x: x.reshape(nh, hd)
    return pl.pallas_call(
        _wkv_sample_kernel,
        grid=(m,),
        in_specs=[proj(0), proj(1), proj(2), proj(3), tok, tok, par, par, par, par, par, st],
        out_specs=[tok, st],
        out_shape=[jax.ShapeDtypeStruct((m, nh, hd), BF16), jax.ShapeDtypeStruct(state.shape, F32)],
        scratch_shapes=[pltpu.VMEM((nh, hd), F32)],
        compiler_params=_cparams(("arbitrary",)),
        name="wkv_sample",
    )(rkvz4, rkvz4, rkvz4, rkvz4, wl.reshape(m, nh, hd), al.reshape(m, nh, hd),
      as_heads(k_k), as_heads(k_a), as_heads(r_k), as_heads(gn_g), as_heads(gn_b), state)


def rope_tables(pos):
    half = ROPE_DIM // 2
    inv_freq = ROPE_THETA ** (-jnp.arange(half, dtype=F32) * 2.0 / ROPE_DIM)
    ang = pos.astype(F32)[:, None] * inv_freq[None, :]
    cos = jnp.cos(ang)
    sin = jnp.sin(ang)
    rows = pos.shape[0]
    ones = jnp.ones((rows, HEAD_DIM - ROPE_DIM), F32)
    zeros_h = jnp.zeros((rows, half), F32)
    zeros_r = jnp.zeros((rows, HEAD_DIM - ROPE_DIM), F32)
    cos_h = jnp.concatenate([cos, cos, ones], axis=1)
    sa_h = jnp.concatenate([-sin, zeros_h, zeros_r], axis=1)
    sb_h = jnp.concatenate([zeros_h, sin, zeros_r], axis=1)
    two = lambda t: jnp.concatenate([t, t], axis=1)
    return two(cos_h), two(sa_h), two(sb_h)


def _attn_prompt_kernel(sink_ref, q_ref, kc_ref, kp_ref, vc_ref, vp_ref, z_ref, o_ref, *, first_block):
    n = pl.program_id(1) + first_block
    hd = HEAD_DIM
    blk = q_ref.shape[0]
    n_kv = kc_ref.shape[1] // hd
    grp = q_ref.shape[1] // (n_kv * hd)
    qi = lax.broadcasted_iota(jnp.int32, (blk, 2 * blk), 0)
    kj = lax.broadcasted_iota(jnp.int32, (blk, 2 * blk), 1) - blk
    kpos = n * blk + kj
    diff = qi - kj
    valid = (kpos >= LEAD) & (diff >= 0) & (diff <= WINDOW)
    k_all = jnp.concatenate([kp_ref[...], kc_ref[...]], axis=0).astype(BF16)
    v_all = jnp.concatenate([vp_ref[...], vc_ref[...]], axis=0).astype(BF16)

    def scores(h):
        k_h = k_all[:, h * hd:(h + 1) * hd]
        return [_dot_nt(q_ref[:, (h * grp + gi) * hd:(h * grp + gi + 1) * hd], k_h) for gi in range(grp)]

    outs = []
    s_next = scores(0)
    for h in range(n_kv):
        s_cur = s_next
        if h + 1 < n_kv:
            s_next = scores(h + 1)
        v_h = v_all[:, h * hd:(h + 1) * hd]
        for g0 in range(0, grp, SOFTMAX_BATCH):
            gs = range(g0, min(g0 + SOFTMAX_BATCH, grp))
            sks = {gi: sink_ref[h * grp + gi] * LOG2E for gi in gs}
            ss = {gi: jnp.where(valid, s_cur[gi], -jnp.inf) for gi in gs}
            ms = {gi: jnp.maximum(jnp.max(ss[gi], axis=-1, keepdims=True), sks[gi]) for gi in gs}
            ps = {gi: jnp.exp2(ss[gi] - ms[gi]) for gi in gs}
            dens = {gi: jnp.sum(ps[gi], axis=-1, keepdims=True) + jnp.exp2(sks[gi] - ms[gi]) for gi in gs}
            outs += [_dot(ps[gi].astype(BF16), v_h) / dens[gi] for gi in gs]
    att = jnp.concatenate(outs, axis=1)
    o_ref[...] = (att * _silu(z_ref[...])).astype(o_ref.dtype)


def attn_prompt(sinks, q, k, v, z, n_batch, first_block):
    m, e = q.shape
    nb = m // (n_batch * BLOCK)
    nb_out = nb - first_block
    kw = k.shape[1]
    cur = lambda i, n: (i * nb + n + first_block, 0)
    prv = lambda i, n: (i * nb + jnp.maximum(n + first_block - 1, 0), 0)
    return pl.pallas_call(
        functools.partial(_attn_prompt_kernel, first_block=first_block),
        grid=(n_batch, nb_out),
        in_specs=[pl.BlockSpec(memory_space=pltpu.SMEM),
                  pl.BlockSpec((BLOCK, e), cur),
                  pl.BlockSpec((BLOCK, kw), cur), pl.BlockSpec((BLOCK, kw), prv),
                  pl.BlockSpec((BLOCK, kw), cur), pl.BlockSpec((BLOCK, kw), prv),
                  pl.BlockSpec((BLOCK, e), cur)],
        out_specs=pl.BlockSpec((BLOCK, e), lambda i, n: (i * nb_out + n, 0)),
        out_shape=jax.ShapeDtypeStruct((n_batch * nb_out * BLOCK, e), BF16),
        compiler_params=_cparams(("arbitrary", "arbitrary")),
        name="attn_prompt",
    )(sinks, q, k, k, v, v, z)


def _attn_sample_kernel(sink_ref, q_ref, kc_ref, vc_ref, kn_ref, vn_ref, z_ref, o_ref, ko_ref, vo_ref):
    hd = HEAD_DIM
    win = kc_ref.shape[1]
    n_kv = kc_ref.shape[2] // hd
    nq = q_ref.shape[1]
    grp = nq // n_kv
    pad = 8
    kc = kc_ref[0]
    vc = vc_ref[0]
    kn = kn_ref[0]
    vn = vn_ref[0]
    first = lax.broadcasted_iota(jnp.int32, (pad, kc.shape[1]), 0) == 0
    k_all = jnp.concatenate([kc, jnp.where(first, kn, 0.0)], axis=0).astype(BF16)
    v_all = jnp.concatenate([vc, jnp.where(first, vn, 0.0)], axis=0).astype(BF16)
    col = lax.broadcasted_iota(jnp.int32, (grp, win + pad), 1)
    valid = (col <= win) & (win - col <= WINDOW)
    q = q_ref[0].astype(BF16)
    row_i = lax.broadcasted_iota(jnp.int32, (grp, 1), 0)
    hs = range(n_kv)
    sks = []
    for h in hs:
        sk = jnp.zeros((grp, 1), F32)
        for gi in range(grp):
            sk = jnp.where(row_i == gi, sink_ref[h * grp + gi] * LOG2E, sk)
        sks.append(sk)
    ss = [jnp.where(valid, _dot_nt(q[h * grp:(h + 1) * grp, :], k_all[:, h * hd:(h + 1) * hd]), -jnp.inf)
          for h in hs]
    ms = [jnp.maximum(jnp.max(ss[h], axis=-1, keepdims=True), sks[h]) for h in hs]
    ps = [jnp.exp2(ss[h] - ms[h]) for h in hs]
    dens = [jnp.sum(ps[h], axis=-1, keepdims=True) + jnp.exp2(sks[h] - ms[h]) for h in hs]
    outs = [_dot(ps[h].astype(BF16), v_all[:, h * hd:(h + 1) * hd]) / dens[h] for h in hs]
    att = jnp.concatenate(outs, axis=0)
    o_ref[0] = (att * _silu(z_ref[0])).astype(o_ref.dtype)
    last = lax.broadcasted_iota(jnp.int32, kc.shape, 0) == win - 1
    ko_ref[0] = jnp.where(last, kn, pltpu.roll(kc, win - 1, axis=0))
    vo_ref[0] = jnp.where(last, vn, pltpu.roll(vc, win - 1, axis=0))


def attn_sample(sinks, q, cache_k, cache_v, k_new, v_new, z):
    m, win, kw = cache_k.shape
    nq = q.shape[1]
    hd = HEAD_DIM
    tok = pl.BlockSpec((1, nq, hd), lambda i: (i, 0, 0))
    cache = pl.BlockSpec((1, win, kw), lambda i: (i, 0, 0))
    new = pl.BlockSpec((1, 1, kw), lambda i: (i, 0, 0))
    return pl.pallas_call(
        _attn_sample_kernel,
        grid=(m,),
        in_specs=[pl.BlockSpec(memory_space=pltpu.SMEM), tok, cache, cache, new, new, tok],
        out_specs=[tok, cache, cache],
        out_shape=[jax.ShapeDtypeStruct((m, nq, hd), BF16),
                   jax.ShapeDtypeStruct(cache_k.shape, F32), jax.ShapeDtypeStruct(cache_v.shape, F32)],
        compiler_params=_cparams(("arbitrary",)),
        name="attn_sample",
    )(sinks, q, cache_k, cache_v, k_new, v_new, z)


def _pad_lora(w_down, w_up):
    r = w_down.shape[1]
    return (jnp.pad(w_down, ((0, 0), (0, LORA_PAD - r))).astype(BF16),
            jnp.pad(w_up, ((0, LORA_PAD - r), (0, 0))).astype(BF16))


def kernel(x_prompt, x_sample, state_wkv, state_shift, cache_k, cache_v, meta_tokens, a_norm, a_mu, a_w_rkvz,
           a_w0, a_w1, a_w2, a_a0, a_a1, a_a2, a_k_k, a_k_a, a_r_k, a_gn_g, a_gn_b, a_w_out, kv_norm, w_kv,
           b_norm, b_w_qz, b_sinks, b_w_o, final_norm):
    nb, seq, d = x_prompt.shape
    db, dseq, _ = x_sample.shape
    assert dseq == 1 and a_norm.shape[0] == 1 and b_norm.shape[0] == 1
    e = a_w_rkvz.shape[3]
    win = cache_k.shape[1]
    p_len = LEAD + N_META + seq
    assert p_len % BLOCK == 0 and (LEAD + N_META) == BLOCK
    kvw = N_KV_HEADS * HEAD_DIM

    w1, w2 = _pad_lora(a_w1[0], a_w2[0])
    a1, a2 = _pad_lora(a_a1[0], a_a2[0])
    mu = a_mu[0]
    sinks = b_sinks[0]
    gains_b = jnp.stack([kv_norm, b_norm[0]])

    tm = p_len // 8

    head = jnp.concatenate([jnp.zeros((LEAD, d), F32), meta_tokens], axis=0)
    w4 = a_w_rkvz[0]
    xm, hw_p, ha_p, x_last, w_rkvz = norm_shift_prompt(x_prompt, head, a_norm[0], mu, w1, a1,
                                                      cast_weights=(w4.reshape(-1, w4.shape[2]),))
    w_rkvz = w_rkvz.reshape(w4.shape)
    p_state_shift = x_last.reshape(1, nb, d)
    rkvz = matmul_groups(xm, w_rkvz, 4 * tm, F32, tn=e // 2)
    yg, p_state, w_out, w_kv_bf, w_qz, w_o = wkv_prompt(
        rkvz, hw_p, ha_p, w2, a_w0[0], a2, a_a0[0], a_k_k[0], a_k_a[0], a_r_k[0].reshape(-1), a_gn_g[0],
        a_gn_b[0], nb, cast_weights=(a_w_out[0], w_kv, b_w_qz[0], b_w_o[0]))
    w_qz = w_qz[None]
    hp, hn_kv, hn_b = matmul_residual_norm_blocks(yg, w_out, x_prompt, gains_b, nb, BF16, True, head=head)

    pos_p = jnp.maximum(jnp.arange(p_len, dtype=jnp.int32) - LEAD, 0)
    tabs_p = rope_tables(pos_p)
    k_p, v_p = matmul_rope(hn_kv, w_kv_bf, tabs_p, tm, kvw, (F32, F32))
    q_p, = matmul_rope(hn_b, w_qz, tabs_p, tm, e, (BF16,), scale=Q_SCALE, n=e)
    z_p = matmul_groups(hn_b[None], w_qz, 4 * tm, F32, n=e, col=1, tn=e // 2)[0]
    skip = (LEAD + N_META) // BLOCK
    att = attn_prompt(sinks, q_p, k_p, v_p, z_p, nb, skip)
    y_prompt, = matmul_residual_norm_blocks(att, w_o, hp, final_norm[None], nb, F32, False, first_block=skip)
    y_prompt = y_prompt.reshape(nb, seq, d)
    tail = lambda t: t.reshape(nb, p_len, kvw)[:, -win:].reshape(nb, win, N_KV_HEADS, HEAD_DIM)
    p_cache_k = tail(k_p)
    p_cache_v = tail(v_p)

    hs = x_sample.reshape(db, d)
    xm_s, hw_s, ha_s, xn_s = norm_shift_sample(hs, state_shift[0], a_norm[0], mu, w1, a1)
    rkvz_s = matmul_groups(xm_s, w_rkvz, db, F32)
    wl_s, al_s = lora_up(hw_s, ha_s, w2, a_w0[0], a2, a_a0[0])
    yg_s, s_state = wkv_sample(rkvz_s, wl_s, al_s, a_k_k[0], a_k_a[0], a_r_k[0].reshape(-1), a_gn_g[0],
                               a_gn_b[0], state_wkv[0])
    hs, hn_kv_s, hn_b_s = matmul_residual_norm(yg_s.reshape(db, e), w_out, hs, gains_b, db, BF16)
    tabs_s = rope_tables(jnp.full((db,), PAST_LEN, jnp.int32))
    k_s, v_s = matmul_rope(hn_kv_s, w_kv_bf, tabs_s, db, kvw, (F32, F32))
    q_s, = matmul_rope(hn_b_s, w_qz, tabs_s, db, e, (F32,), scale=Q_SCALE, n=e)
    z_s = matmul_groups(hn_b_s[None], w_qz, db, F32, n=e, col=1)[0]
    nq = e // HEAD_DIM
    att_s, s_cache_k, s_cache_v = attn_sample(
        sinks, q_s.reshape(db, nq, HEAD_DIM), cache_k.reshape(db, win, kvw), cache_v.reshape(db, win, kvw),
        k_s.reshape(db, 1, kvw), v_s.reshape(db, 1, kvw), z_s.reshape(db, nq, HEAD_DIM))
    y_s, = matmul_residual_norm(att_s.reshape(db, e), w_o, hs, final_norm[None], db, F32, emit_h=False)
    y_sample = y_s.reshape(db, 1, d)

    return (y_prompt, y_sample, p_state[None], p_state_shift,
            p_cache_k, p_cache_v,
            s_state[None], xn_s[None],
            s_cache_k.reshape(cache_k.shape), s_cache_v.reshape(cache_v.shape))
```

```python
import functools
import math

import jax
import jax.numpy as jnp
from jax import lax
from jax.experimental import pallas as pl
from jax.experimental.pallas import tpu as pltpu

F32 = jnp.float32
BF16 = jnp.bfloat16

HEAD_DIM = 64
N_KV_HEADS = 8
WINDOW = 128
BLOCK = 128
ROPE_DIM = HEAD_DIM // 4
ROPE_THETA = 500000.0
N_META = 16
PAST_LEN = 16384
RMS_EPS = 1e-6
GN_EPS = 64e-5
NORM_FLOOR_SQ = 1e-24
LEAD = (-N_META) % BLOCK
CHUNK = 64
WKV_ROWS = 128
CAST_CHUNKS = 32
SEQS_PER_STEP = 2
RESIDUAL_SEQS_PER_STEP = 4
HEADS_PER_STREAM = 32
HEADS_PER_STEP = 32
LORA_PAD = 128
MXU_TILE = 256
ROPE_SLAB = 512
VMEM_LIMIT = 56 * 1024 * 1024
LOG2E = 1.0 / math.log(2.0)
DECAY_SCALE = math.exp(-0.5)
SOFTMAX_BATCH = 4
Q_SCALE = HEAD_DIM ** -0.5 * LOG2E


def _cparams(sem):
    return pltpu.CompilerParams(dimension_semantics=sem, vmem_limit_bytes=VMEM_LIMIT)


def _sigmoid(x):
    return 1.0 / (1.0 + jnp.exp2(x * (-LOG2E)))


def _silu(x):
    return x * _sigmoid(x)


def _dot(a, b):
    return jnp.dot(a, b, preferred_element_type=F32)


def _dot_nt(a, b):
    return lax.dot_general(a, b, (((1,), (1,)), ((), ())), preferred_element_type=F32)


def _dot_tn(a, b):
    return lax.dot_general(a, b, (((0,), (0,)), ((), ())), preferred_element_type=F32)


def _split_hi_lo(x):
    hi = x.astype(BF16)
    lo = (x - hi.astype(F32)).astype(BF16)
    return hi, lo


def _cast_job(weights, n_steps, step_of):
    n_chunks = min(CAST_CHUNKS, n_steps)
    every = n_steps // n_chunks
    assert all(w.shape[0] % (n_chunks * 16) == 0 for w in weights)
    chunk = lambda *idx: (jnp.minimum(step_of(*idx) // every, n_chunks - 1), 0)
    return [pl.BlockSpec((w.shape[0] // n_chunks, w.shape[1]), chunk) for w in weights], every


def _cast_chunks(step, every, srcs, dsts):
    @pl.when(step % every == 0)
    def _():
        for src, dst in zip(srcs, dsts):
            dst[...] = src[...].astype(dst.dtype)


def _mixes(xn, prev, mu_ref, w1_ref, a1_ref, xm_ref, hw_ref, ha_ref):
    xx = prev - xn
    n_proj = xm_ref.shape[0]
    for p in range(n_proj):
        xm_ref[p] = (xn + xx * mu_ref[p:p + 1, :]).astype(xm_ref.dtype)
    xw = (xn + xx * mu_ref[n_proj:n_proj + 1, :]).astype(BF16)
    xa = (xn + xx * mu_ref[n_proj + 1:n_proj + 2, :]).astype(BF16)
    hw_ref[...] = jnp.tanh(_dot(xw, w1_ref[...])).astype(hw_ref.dtype)
    ha_ref[...] = _dot(xa, a1_ref[...]).astype(ha_ref.dtype)


def _norm_shift_kernel(x_ref, head_ref, g_ref, mu_ref, w1_ref, a1_ref, *rest, n_cast, cast_every):
    cast_in = rest[:n_cast]
    xm_ref, hw_ref, ha_ref, last_ref = rest[n_cast:n_cast + 4]
    cast_out = rest[n_cast + 4:2 * n_cast + 4]
    carry_ref = rest[-1]

    @pl.when(pl.program_id(1) == 0)
    def _():
        carry_ref[...] = jnp.zeros_like(carry_ref)

    _cast_chunks(pl.program_id(0) * pl.num_programs(1) + pl.program_id(1), cast_every, cast_in, cast_out)

    is_head = pl.program_id(1) == 0
    for sq in range(x_ref.shape[0]):
        x = jnp.where(is_head, head_ref[...], x_ref[sq])
        tm = x.shape[0]
        xn = x * lax.rsqrt(jnp.mean(x * x, axis=-1, keepdims=True) + RMS_EPS) * g_ref[...]
        rolled = pltpu.roll(xn, 1, axis=0)
        row = lax.broadcasted_iota(jnp.int32, xn.shape, 0)
        prev = jnp.where(row == 0, carry_ref[sq, 0:1, :], rolled)
        _mixes(xn, prev, mu_ref, w1_ref, a1_ref, xm_ref.at[:, sq], hw_ref.at[sq], ha_ref.at[sq])
        carry_ref[sq, 0:1, :] = xn[tm - 1:tm, :]
        last_ref[sq] = xn[tm - 1:tm, :]


def norm_shift_prompt(x, head, g, mu, w1, a1, cast_weights=()):
    b, seq, d = x.shape
    tm = BLOCK
    p = tm + seq
    n_mix = mu.shape[0]
    n_proj = n_mix - 2
    lr = w1.shape[1]
    nt = p // tm
    sp = SEQS_PER_STEP
    const = lambda shape: pl.BlockSpec(shape, lambda i, t: (0,) * len(shape))
    hid = pl.BlockSpec((sp, tm, lr), lambda i, t: (i, t, 0))
    cast_specs, every = _cast_job(cast_weights, (b // sp) * nt, lambda i, t: i * nt + t)
    xm, hw_act, ha_act, last, *casts = pl.pallas_call(
        functools.partial(_norm_shift_kernel, n_cast=len(cast_weights), cast_every=every),
        grid=(b // sp, nt),
        in_specs=[pl.BlockSpec((sp, tm, d), lambda i, t: (i, jnp.maximum(t - 1, 0), 0)),
                  const((tm, d)), const((1, d)), const((n_mix, d)), const((d, lr)), const((d, lr))] + cast_specs,
        out_specs=[pl.BlockSpec((n_proj, sp, tm, d), lambda i, t: (0, i, t, 0)), hid, hid,
                   pl.BlockSpec((sp, 1, d), lambda i, t: (i, 0, 0))] + cast_specs,
        out_shape=[jax.ShapeDtypeStruct((n_proj, b, p, d), BF16),
                   jax.ShapeDtypeStruct((b, p, lr), BF16), jax.ShapeDtypeStruct((b, p, lr), BF16),
                   jax.ShapeDtypeStruct((b, 1, d), F32)] + [jax.ShapeDtypeStruct(w.shape, BF16) for w in cast_weights],
        scratch_shapes=[pltpu.VMEM((sp, 8, d), F32)],
        compiler_params=_cparams(("arbitrary", "arbitrary")),
        name="norm_shift_prompt",
    )(x, head, g.reshape(1, d), mu, w1, a1, *cast_weights)
    return (xm.reshape(n_proj, b * p, d), hw_act.reshape(b * p, lr), ha_act.reshape(b * p, lr), last, *casts)


def _norm_shift_sample_kernel(x_ref, prev_ref, g_ref, mu_ref, w1_ref, a1_ref, xm_ref, hw_ref, ha_ref, xn_ref):
    x = x_ref[...]
    xn = x * lax.rsqrt(jnp.mean(x * x, axis=-1, keepdims=True) + RMS_EPS) * g_ref[...]
    xn_ref[...] = xn
    _mixes(xn, prev_ref[...], mu_ref, w1_ref, a1_ref, xm_ref, hw_ref, ha_ref)


def norm_shift_sample(x, prev, g, mu, w1, a1):
    m, d = x.shape
    lr = w1.shape[1]
    return pl.pallas_call(
        _norm_shift_sample_kernel,
        out_shape=[jax.ShapeDtypeStruct((mu.shape[0] - 2, m, d), BF16),
                   jax.ShapeDtypeStruct((m, lr), BF16), jax.ShapeDtypeStruct((m, lr), BF16),
                   jax.ShapeDtypeStruct((m, d), F32)],
        name="norm_shift_sample",
    )(x, prev, g.reshape(1, d), mu, w1, a1)


def _rope(y, cos, sin_a, sin_b):
    half = ROPE_DIM // 2
    step = ROPE_SLAB
    rep = step // cos.shape[1]
    tile = lambda t: jnp.concatenate([t] * rep, axis=1)
    cos_t, sa_t, sb_t = tile(cos), tile(sin_a), tile(sin_b)
    outs = []
    for j in range(y.shape[1] // step):
        ys = y[:, j * step:(j + 1) * step]
        outs.append(ys * cos_t + pltpu.roll(ys, step - half, axis=1) * sa_t + pltpu.roll(ys, half, axis=1) * sb_t)
    return jnp.concatenate(outs, axis=1) if len(outs) > 1 else outs[0]


def _mm_group_kernel(x_ref, w_ref, o_ref):
    o_ref[0] = _dot(x_ref[0], w_ref[0]).astype(o_ref.dtype)


def matmul_groups(x, w, tm, out_dtype, n=None, col=0, tn=None):
    g = w.shape[0]
    n = w.shape[2] if n is None else n
    tn = n if tn is None else tn
    nt = n // tn
    _, m, kdim = x.shape
    return pl.pallas_call(
        _mm_group_kernel,
        grid=(g, nt, m // tm),
        in_specs=[pl.BlockSpec((1, tm, kdim), lambda q, j, i: (q, i, 0)),
                  pl.BlockSpec((1, kdim, tn), lambda q, j, i: (q, 0, col * nt + j))],
        out_specs=pl.BlockSpec((1, tm, tn), lambda q, j, i: (q, i, j)),
        out_shape=jax.ShapeDtypeStruct((g, m, n), out_dtype),
        compiler_params=_cparams(("arbitrary", "arbitrary", "arbitrary")),
        name="matmul_groups",
    )(x, w)


def _mm_rope_kernel(x_ref, w_ref, cos_ref, sa_ref, sb_ref, *o_refs, n_rope, scale):
    y = _dot(x_ref[...], w_ref[...])
    rot = _rope(y[:, :n_rope], cos_ref[...], sa_ref[...], sb_ref[...])
    if scale != 1.0:
        rot = rot * scale
    o_refs[0][...] = rot.astype(o_refs[0].dtype)
    if len(o_refs) > 1:
        o_refs[1][...] = y[:, n_rope:].astype(o_refs[1].dtype)


def matmul_rope(x, w, tables, tm, n_rope, out_dtypes, scale=1.0, n=None):
    m, kdim = x.shape
    n = w.shape[-1] if n is None else n
    w_block = (kdim, n) if w.ndim == 2 else (None, kdim, n)
    lanes = tables[0].shape[1]
    widths = [n_rope] + ([n - n_rope] if n > n_rope else [])
    tab_blocks = tables[0].shape[0] // tm
    tab = pl.BlockSpec((tm, lanes), lambda i: (i % tab_blocks, 0))
    outs = pl.pallas_call(
        functools.partial(_mm_rope_kernel, n_rope=n_rope, scale=scale),
        grid=(m // tm,),
        in_specs=[pl.BlockSpec((tm, kdim), lambda i: (i, 0)),
                  pl.BlockSpec(w_block, lambda i: (0,) * w.ndim, pipeline_mode=pl.Buffered(1)),
                  tab, tab, tab],
        out_specs=[pl.BlockSpec((tm, wd), lambda i: (i, 0)) for wd in widths],
        out_shape=[jax.ShapeDtypeStruct((m, wd), dt) for wd, dt in zip(widths, out_dtypes)],
        compiler_params=_cparams(("arbitrary",)),
        name="matmul_rope",
    )(x, w, *tables)
    return outs


def _mm_res_norm_kernel(x_ref, w_ref, res_ref, g_ref, *out_refs, emit_h):
    h = res_ref[...] + _dot(x_ref[...], w_ref[...])
    hn_refs = out_refs
    if emit_h:
        out_refs[0][...] = h
        hn_refs = out_refs[1:]
    inv = lax.rsqrt(jnp.mean(h * h, axis=-1, keepdims=True) + RMS_EPS)
    for j, hn_ref in enumerate(hn_refs):
        hn_ref[...] = (h * inv * g_ref[j:j + 1, :]).astype(hn_ref.dtype)


def matmul_residual_norm(x, w, res, gains, tm, norm_dtype, emit_h=True):
    m, kdim = x.shape
    n = w.shape[1]
    ng = gains.shape[0]
    row = lambda width: pl.BlockSpec((tm, width), lambda i: (i, 0))
    return pl.pallas_call(
        functools.partial(_mm_res_norm_kernel, emit_h=emit_h),
        grid=(m // tm,),
        in_specs=[row(kdim),
                  pl.BlockSpec((kdim, n), lambda i: (0, 0), pipeline_mode=pl.Buffered(1)),
                  row(n),
                  pl.BlockSpec((ng, n), lambda i: (0, 0))],
        out_specs=[row(n)] * (int(emit_h) + ng),
        out_shape=[jax.ShapeDtypeStruct((m, n), F32)] * int(emit_h) + [jax.ShapeDtypeStruct((m, n), norm_dtype)] * ng,
        compiler_params=_cparams(("arbitrary",)),
        name="matmul_residual_norm",
    )(x, w, res, gains)


def _mm_res_norm_blocks_kernel(x_ref, w_ref, res_ref, *rest, emit_h, head, first_block):
    if head:
        head_ref, g_ref, *out_refs = rest
        res = jnp.where(pl.program_id(1) + first_block == 0, head_ref[...][None], res_ref[...])
    else:
        g_ref, *out_refs = rest
        res = res_ref[...]
    seqs, blk, n = res.shape
    h = res.reshape(seqs * blk, n) + _dot(x_ref[...].reshape(seqs * blk, x_ref.shape[2]), w_ref[...])
    hn_refs = out_refs
    if emit_h:
        out_refs[0][...] = h.reshape(seqs, blk, n)
        hn_refs = out_refs[1:]
    inv = lax.rsqrt(jnp.mean(h * h, axis=-1, keepdims=True) + RMS_EPS)
    for j, hn_ref in enumerate(hn_refs):
        hn_ref[...] = (h * inv * g_ref[j:j + 1, :]).astype(hn_ref.dtype).reshape(seqs, blk, n)


def matmul_residual_norm_blocks(x, w, res, gains, n_batch, norm_dtype, emit_h, head=None, first_block=0):
    kdim = x.shape[1]
    n = w.shape[1]
    ng = gains.shape[0]
    nb_out = x.shape[0] // (n_batch * BLOCK)
    nb = nb_out + first_block
    sp = RESIDUAL_SEQS_PER_STEP
    blocks = lambda width, shift: pl.BlockSpec((sp, BLOCK, width), lambda i, t: (i, t + shift, 0))
    const = lambda shape: pl.BlockSpec(shape, lambda i, t: (0,) * len(shape))
    if head is not None:
        res_specs = [pl.BlockSpec((sp, BLOCK, n), lambda i, t: (i, jnp.maximum(t + first_block - 1, 0), 0)),
                     const((BLOCK, n))]
        res_args = [res, head]
    else:
        res_specs, res_args = [blocks(n, first_block)], [res.reshape(n_batch, nb * BLOCK, n)]
    outs = pl.pallas_call(
        functools.partial(_mm_res_norm_blocks_kernel, emit_h=emit_h, head=head is not None, first_block=first_block),
        grid=(n_batch // sp, nb_out),
        in_specs=[blocks(kdim, 0), pl.BlockSpec((kdim, n), lambda i, t: (0, 0), pipeline_mode=pl.Buffered(1))]
        + res_specs + [const((ng, n))],
        out_specs=[blocks(n, first_block)] * int(emit_h) + [blocks(n, 0)] * ng,
        out_shape=[jax.ShapeDtypeStruct((n_batch, nb * BLOCK, n), F32)] * int(emit_h)
        + [jax.ShapeDtypeStruct((n_batch, nb_out * BLOCK, n), norm_dtype)] * ng,
        compiler_params=_cparams(("arbitrary", "arbitrary")),
        name="matmul_residual_norm_blocks",
    )(x.reshape(n_batch, nb_out * BLOCK, kdim), w, *res_args, gains)
    return [o.reshape(-1, n) for o in outs]


def _lora_up_kernel(hw_ref, ha_ref, w2_ref, w0_ref, a2_ref, a0_ref, wl_ref, al_ref):
    wl_ref[...] = w0_ref[...] + _dot(hw_ref[...], w2_ref[...])
    al_ref[...] = a0_ref[...] + _dot(ha_ref[...], a2_ref[...])


def lora_up(hw, ha, w2, w0, a2, a0):
    m = hw.shape[0]
    e = w2.shape[1]
    return pl.pallas_call(
        _lora_up_kernel,
        out_shape=[jax.ShapeDtypeStruct((m, e), F32), jax.ShapeDtypeStruct((m, e), F32)],
        name="lora_up",
    )(hw, ha, w2, w0.reshape(1, e), a2, a0.reshape(1, e))


def _seg_sum(x, ones_bd):
    hi = x.astype(BF16)
    outs = []
    for c in range(x.shape[1] // MXU_TILE):
        sl = slice(c * MXU_TILE, (c + 1) * MXU_TILE)
        outs.append(_dot(hi[:, sl], ones_bd))
    return jnp.concatenate(outs, axis=1) if len(outs) > 1 else outs[0]


def _wkv_prompt_kernel(r_ref, k_ref, v_ref, z_ref, hw_ref, ha_ref, w2_ref, w0_ref, a2_ref, a0_ref,
                       kk_ref, ka_ref, rk_ref, gg_ref, gb_ref, *rest, n_cast, cast_every):
    cast_in = rest[:n_cast]
    yg_ref, sout_ref = rest[n_cast:n_cast + 2]
    cast_out = rest[n_cast + 2:2 * n_cast + 2]
    s_ref = rest[-1]
    t_idx = pl.program_id(2)
    c = CHUNK
    hd = HEAD_DIM

    @pl.when(t_idx == 0)
    def _():
        s_ref[...] = jnp.zeros_like(s_ref)

    step = (pl.program_id(0) * pl.num_programs(1) + pl.program_id(1)) * pl.num_programs(2) + t_idx
    _cast_chunks(step, cast_every, cast_in, cast_out)

    tb = r_ref.shape[1]
    nh = HEADS_PER_STREAM
    hw = nh * hd
    nc = tb // c

    li = lax.broadcasted_iota(jnp.int32, (MXU_TILE, MXU_TILE), 0) // hd
    lj = lax.broadcasted_iota(jnp.int32, (MXU_TILE, MXU_TILE), 1) // hd
    ones_bd = jnp.where(li == lj, 1.0, 0.0).astype(BF16)
    bi_ = lax.broadcasted_iota(jnp.int32, (tb, tb), 0)
    bj_ = lax.broadcasted_iota(jnp.int32, (tb, tb), 1)
    tri_incl = jnp.where((bj_ <= bi_) & (bj_ // c == bi_ // c), 1.0, 0.0).astype(BF16)
    ti = lax.broadcasted_iota(jnp.int32, (c, c), 0)
    tj = lax.broadcasted_iota(jnp.int32, (c, c), 1)
    ai = lax.broadcasted_iota(jnp.int32, (c, 2 * c), 0)
    aj = lax.broadcasted_iota(jnp.int32, (c, 2 * c), 1)
    upper = aj >= c
    aj_mod = jnp.where(upper, aj - c, aj)
    masks = dict(
        strict=tj < ti,
        eye=jnp.where(ti == tj, 1.0, 0.0).astype(F32),
        top_k=upper & (aj_mod < ai),
        bot=aj_mod <= ai,
        mean_bd=jnp.where(li == lj, 1.0 / hd, 0.0).astype(BF16))

    for st in range(r_ref.shape[2] // hw):
        _wkv_stream(st, hw, nc, r_ref, k_ref, v_ref, z_ref, hw_ref, ha_ref, w2_ref, w0_ref, a2_ref, a0_ref,
                    kk_ref, ka_ref, rk_ref, gg_ref, gb_ref, yg_ref, s_ref, ones_bd, tri_incl, masks)

    @pl.when(t_idx == pl.num_programs(2) - 1)
    def _():
        sout_ref[0] = s_ref[...]


def _wkv_stream(st, hw, nc, r_ref, k_ref, v_ref, z_ref, hw_ref, ha_ref, w2_ref, w0_ref, a2_ref, a0_ref,
                kk_ref, ka_ref, rk_ref, gg_ref, gb_ref, yg_ref, s_ref, ones_bd, tri_incl, masks):
    c = CHUNK
    hd = HEAD_DIM
    nh = hw // hd
    ls = slice(st * hw, (st + 1) * hw)
    h0 = st * nh
    r = r_ref[0, :, ls]
    k = k_ref[0, :, ls]
    v = v_ref[0, :, ls]
    wl = w0_ref[:, ls] + _dot(hw_ref[...], w2_ref[:, ls])
    al = a0_ref[:, ls] + _dot(ha_ref[...], a2_ref[:, ls])
    a = _sigmoid(al)
    lw = (-DECAY_SCALE * LOG2E) * _sigmoid(wl)
    kk = k * kk_ref[:, ls]
    n2 = _seg_sum(kk * kk, ones_bd)
    kk = kk * lax.rsqrt(jnp.maximum(n2, NORM_FLOOR_SQ))
    ka = ka_ref[:, ls]
    k2 = k * (a * ka + (1.0 - ka))
    bb = kk * a

    lw_hi, lw_lo = _split_hi_lo(lw)
    g = _dot(tri_incl, lw_hi) + _dot(tri_incl, lw_lo)
    mid = lambda ci: g[ci * c + c // 2 - 1:ci * c + c // 2, :]
    gm = jnp.concatenate([jnp.broadcast_to(mid(ci), (c, hw)) for ci in range(nc)], axis=0)
    t = g - gm
    e_a = jnp.exp2(t)
    e_prev = jnp.exp2(t - lw)
    e_inv = jnp.exp2(-t)
    e1 = [jnp.exp2(mid(ci)) for ci in range(nc)]
    e2 = [jnp.exp2(g[ci * c + c - 1:ci * c + c, :] - mid(ci)) for ci in range(nc)]

    kkd = (kk * e_prev).astype(BF16)
    rd = (r * e_a).astype(BF16)
    bi = (bb * e_inv).astype(BF16)
    ki = (k2 * e_inv).astype(BF16)
    v_bf = v.astype(BF16)
    zeros_cv = jnp.zeros((c, hd), BF16)

    pairs = [(ci, h) for ci in range(nc) for h in range(nh)]
    rows = lambda ci: slice(ci * c, (ci + 1) * c)
    cols = lambda h: slice(h * hd, (h + 1) * hd)
    xs = {(ci, h): jnp.concatenate([kkd[rows(ci), cols(h)], rd[rows(ci), cols(h)]], axis=0) for ci, h in pairs}
    r1s = {(ci, h): jnp.concatenate([bi[rows(ci), cols(h)], ki[rows(ci), cols(h)]], axis=0) for ci, h in pairs}
    vs = {(ci, h): v_bf[rows(ci), cols(h)] for ci, h in pairs}
    a_mats = {p: _dot_nt(xs[p], r1s[p]) for p in pairs}
    lk_vs = {p: _dot(jnp.where(masks["top_k"], a_mats[p][:c, :], 0.0).astype(BF16),
                     jnp.concatenate([zeros_cv, vs[p]], axis=0)) for p in pairs}
    lps = {p: jnp.where(masks["strict"], a_mats[p][:c, :c], 0.0) for p in pairs}
    ts = {p: masks["eye"] - lps[p] for p in pairs}
    for _ in range(int(math.log2(c)) - 1):
        lpb = {p: lps[p].astype(BF16) for p in pairs}
        lps = {p: _dot(lpb[p], lpb[p]) for p in pairs}
        ts = {p: _dot(ts[p].astype(BF16), (masks["eye"] + lps[p]).astype(BF16)) for p in pairs}
    a_bots = {p: jnp.where(masks["bot"], a_mats[p][c:, :], 0.0).astype(BF16) for p in pairs}
    t_bf = {p: ts[p].astype(BF16) for p in pairs}

    state = [s_ref[h0 + h] for h in range(nh)]
    y_rows = []
    for ci in range(nc):
        hs = range(nh)
        sms = [state[h] * e1[ci][:, cols(h)] for h in hs]
        p_mats = [_dot_nt(xs[ci, h], sms[h].astype(BF16)) for h in hs]
        us = [-_dot(t_bf[ci, h], (p_mats[h][:c, :] + lk_vs[ci, h]).astype(BF16)) for h in hs]
        uvs = [jnp.concatenate([us[h].astype(BF16), vs[ci, h]], axis=0) for h in hs]
        ys = [p_mats[h][c:, :] + _dot(a_bots[ci, h], uvs[h]) for h in hs]
        state = [(sms[h] + _dot_tn(uvs[h], r1s[ci, h])) * e2[ci][:, cols(h)] for h in hs]
        y_rows.append(jnp.concatenate(ys, axis=1))
    for h in range(nh):
        s_ref[h0 + h] = state[h]
    y = jnp.concatenate(y_rows, axis=0) if nc > 1 else y_rows[0]

    mean = _seg_sum(y, masks["mean_bd"])
    yc = y - mean
    var = _seg_sum(yc * yc, masks["mean_bd"])
    yn = yc * lax.rsqrt(var + GN_EPS) * gg_ref[:, ls] + gb_ref[:, ls]
    bonus = _seg_sum(r * k2 * rk_ref[:, ls], ones_bd)
    yg_ref[:, ls] = ((yn + bonus * v) * _silu(z_ref[0, :, ls])).astype(BF16)


def wkv_prompt(rkvz, hw_act, ha_act, w2, w0, a2, a0, k_k, k_a, r_k, gn_g, gn_b, n_batch, cast_weights=()):
    _, m, e = rkvz.shape
    p = m // n_batch
    tb = WKV_ROWS
    nt = p // tb
    hw = HEADS_PER_STEP * HEAD_DIM
    nh = e // HEAD_DIM
    row = lambda i, g, t: (i * nt + t, g)
    proj = lambda q: pl.BlockSpec((1, tb, hw), lambda i, g, t: (q, i * nt + t, g))
    par = pl.BlockSpec((1, hw), lambda i, g, t: (0, g))
    lr = w2.shape[0]
    hid = pl.BlockSpec((tb, lr), lambda i, g, t: (i * nt + t, 0))
    up = pl.BlockSpec((lr, hw), lambda i, g, t: (0, g))
    n_groups = e // hw
    cast_specs, every = _cast_job(cast_weights, n_batch * n_groups * nt, lambda i, g, t: (i * n_groups + g) * nt + t)
    return pl.pallas_call(
        functools.partial(_wkv_prompt_kernel, n_cast=len(cast_weights), cast_every=every),
        grid=(n_batch, n_groups, nt),
        in_specs=[proj(0), proj(1), proj(2), proj(3), hid, hid, up, par, up, par,
                  par, par, par, par, par] + cast_specs,
        out_specs=[pl.BlockSpec((tb, hw), row),
                   pl.BlockSpec((1, HEADS_PER_STEP, HEAD_DIM, HEAD_DIM), lambda i, g, t: (i, g, 0, 0))] + cast_specs,
        out_shape=[jax.ShapeDtypeStruct((m, e), BF16),
                   jax.ShapeDtypeStruct((n_batch, nh, HEAD_DIM, HEAD_DIM), F32)]
        + [jax.ShapeDtypeStruct(w.shape, BF16) for w in cast_weights],
        scratch_shapes=[pltpu.VMEM((HEADS_PER_STEP, HEAD_DIM, HEAD_DIM), F32)],
        compiler_params=_cparams(("arbitrary", "arbitrary", "arbitrary")),
        name="wkv_prompt",
    )(rkvz, rkvz, rkvz, rkvz, hw_act, ha_act, w2, w0.reshape(1, e), a2, a0.reshape(1, e),
      k_k.reshape(1, e), k_a.reshape(1, e), r_k.reshape(1, e), gn_g.reshape(1, e), gn_b.reshape(1, e),
      *cast_weights)


def _wkv_sample_kernel(r_ref, k_ref, v_ref, z_ref, wl_ref, al_ref, kk_ref, ka_ref, rk_ref, gg_ref, gb_ref,
                       s_ref, yg_ref, sout_ref, y_scr):
    hd = HEAD_DIM
    r = r_ref[0]
    k = k_ref[0]
    v = v_ref[0]
    a = _sigmoid(al_ref[0])
    d = jnp.exp2((-DECAY_SCALE * LOG2E) * _sigmoid(wl_ref[0]))
    kk = k * kk_ref[...]
    kk = kk / jnp.maximum(jnp.sqrt(jnp.sum(kk * kk, axis=-1, keepdims=True)), 1e-12)
    k2 = k * (1.0 + (a - 1.0) * ka_ref[...])
    bb = kk * a
    nh = r.shape[0]
    ii = lax.broadcasted_iota(jnp.int32, (hd, hd), 0)
    jj = lax.broadcasted_iota(jnp.int32, (hd, hd), 1)
    eye = ii == jj

    row = lambda x, h: x[h:h + 1, :]
    group = 8
    for h0 in range(0, nh, group):
        hs = range(h0, h0 + group)
        s = {h: s_ref[0, h] for h in hs}
        sa = {h: jnp.sum(s[h] * row(kk, h), axis=-1, keepdims=True) for h in hs}
        v_col = {h: jnp.sum(jnp.where(eye, row(v, h), 0.0), axis=-1, keepdims=True) for h in hs}
        s_new = {h: s[h] * row(d, h) - sa[h] * row(bb, h) + v_col[h] * row(k2, h) for h in hs}
        y_col = {h: jnp.sum(s_new[h] * row(r, h), axis=-1, keepdims=True) for h in hs}
        for h in hs:
            sout_ref[0, h] = s_new[h]
            y_scr[h:h + 1, :] = jnp.sum(jnp.where(eye, y_col[h], 0.0), axis=0, keepdims=True)

    y = y_scr[...]
    mean = jnp.mean(y, axis=-1, keepdims=True)
    yc = y - mean
    var = jnp.mean(yc * yc, axis=-1, keepdims=True)
    yn = yc * lax.rsqrt(var + GN_EPS) * gg_ref[...] + gb_ref[...]
    bonus = jnp.sum(r * k2 * rk_ref[...], axis=-1, keepdims=True)
    yg_ref[0] = ((yn + bonus * v) * _silu(z_ref[0])).astype(BF16)


def wkv_sample(rkvz, wl, al, k_k, k_a, r_k, gn_g, gn_b, state):
    _, m, e = rkvz.shape
    nh = e // HEAD_DIM
    hd = HEAD_DIM
    rkvz4 = rkvz.reshape(4, m, nh, hd)
    proj = lambda q: pl.BlockSpec((None, 1, nh, hd), lambda i: (q, i, 0, 0))
    tok = pl.BlockSpec((1, nh, hd), lambda i: (i, 0, 0))
    par = pl.BlockSpec((nh, hd), lambda i: (0, 0))
    st = pl.BlockSpec((1, nh, hd, hd), lambda i: (i, 0, 0, 0))
    as_heads = lambda x: x.reshape(nh, hd)
    return pl.pallas_call(
        _wkv_sample_kernel,
        grid=(m,),
        in_specs=[proj(0), proj(1), proj(2), proj(3), tok, tok, par, par, par, par, par, st],
        out_specs=[tok, st],
        out_shape=[jax.ShapeDtypeStruct((m, nh, hd), BF16), jax.ShapeDtypeStruct(state.shape, F32)],
        scratch_shapes=[pltpu.VMEM((nh, hd), F32)],
        compiler_params=_cparams(("arbitrary",)),
        name="wkv_sample",
    )(rkvz4, rkvz4, rkvz4, rkvz4, wl.reshape(m, nh, hd), al.reshape(m, nh, hd),
      as_heads(k_k), as_heads(k_a), as_heads(r_k), as_heads(gn_g), as_heads(gn_b), state)


def rope_tables(pos):
    half = ROPE_DIM // 2
    inv_freq = ROPE_THETA ** (-jnp.arange(half, dtype=F32) * 2.0 / ROPE_DIM)
    ang = pos.astype(F32)[:, None] * inv_freq[None, :]
    cos = jnp.cos(ang)
    sin = jnp.sin(ang)
    rows = pos.shape[0]
    ones = jnp.ones((rows, HEAD_DIM - ROPE_DIM), F32)
    zeros_h = jnp.zeros((rows, half), F32)
    zeros_r = jnp.zeros((rows, HEAD_DIM - ROPE_DIM), F32)
    cos_h = jnp.concatenate([cos, cos, ones], axis=1)
    sa_h = jnp.concatenate([-sin, zeros_h, zeros_r], axis=1)
    sb_h = jnp.concatenate([zeros_h, sin, zeros_r], axis=1)
    two = lambda t: jnp.concatenate([t, t], axis=1)
    return two(cos_h), two(sa_h), two(sb_h)


def _attn_prompt_kernel(sink_ref, q_ref, kc_ref, kp_ref, vc_ref, vp_ref, z_ref, o_ref, *, first_block):
    n = pl.program_id(1) + first_block
    hd = HEAD_DIM
    blk = q_ref.shape[0]
    n_kv = kc_ref.shape[1] // hd
    grp = q_ref.shape[1] // (n_kv * hd)
    qi = lax.broadcasted_iota(jnp.int32, (blk, 2 * blk), 0)
    kj = lax.broadcasted_iota(jnp.int32, (blk, 2 * blk), 1) - blk
    kpos = n * blk + kj
    diff = qi - kj
    valid = (kpos >= LEAD) & (diff >= 0) & (diff <= WINDOW)
    k_all = jnp.concatenate([kp_ref[...], kc_ref[...]], axis=0).astype(BF16)
    v_all = jnp.concatenate([vp_ref[...], vc_ref[...]], axis=0).astype(BF16)

    def scores(h):
        k_h = k_all[:, h * hd:(h + 1) * hd]
        return [_dot_nt(q_ref[:, (h * grp + gi) * hd:(h * grp + gi + 1) * hd], k_h) for gi in range(grp)]

    outs = []
    s_next = scores(0)
    for h in range(n_kv):
        s_cur = s_next
        if h + 1 < n_kv:
            s_next = scores(h + 1)
        v_h = v_all[:, h * hd:(h + 1) * hd]
        for g0 in range(0, grp, SOFTMAX_BATCH):
            gs = range(g0, min(g0 + SOFTMAX_BATCH, grp))
            sks = {gi: sink_ref[h * grp + gi] * LOG2E for gi in gs}
            ss = {gi: jnp.where(valid, s_cur[gi], -jnp.inf) for gi in gs}
            ms = {gi: jnp.maximum(jnp.max(ss[gi], axis=-1, keepdims=True), sks[gi]) for gi in gs}
            ps = {gi: jnp.exp2(ss[gi] - ms[gi]) for gi in gs}
            dens = {gi: jnp.sum(ps[gi], axis=-1, keepdims=True) + jnp.exp2(sks[gi] - ms[gi]) for gi in gs}
            outs += [_dot(ps[gi].astype(BF16), v_h) / dens[gi] for gi in gs]
    att = jnp.concatenate(outs, axis=1)
    o_ref[...] = (att * _silu(z_ref[...])).astype(o_ref.dtype)


def attn_prompt(sinks, q, k, v, z, n_batch, first_block):
    m, e = q.shape
    nb = m // (n_batch * BLOCK)
    nb_out = nb - first_block
    kw = k.shape[1]
    cur = lambda i, n: (i * nb + n + first_block, 0)
    prv = lambda i, n: (i * nb + jnp.maximum(n + first_block - 1, 0), 0)
    return pl.pallas_call(
        functools.partial(_attn_prompt_kernel, first_block=first_block),
        grid=(n_batch, nb_out),
        in_specs=[pl.BlockSpec(memory_space=pltpu.SMEM),
                  pl.BlockSpec((BLOCK, e), cur),
                  pl.BlockSpec((BLOCK, kw), cur), pl.BlockSpec((BLOCK, kw), prv),
                  pl.BlockSpec((BLOCK, kw), cur), pl.BlockSpec((BLOCK, kw), prv),
                  pl.BlockSpec((BLOCK, e), cur)],
        out_specs=pl.BlockSpec((BLOCK, e), lambda i, n: (i * nb_out + n, 0)),
        out_shape=jax.ShapeDtypeStruct((n_batch * nb_out * BLOCK, e), BF16),
        compiler_params=_cparams(("arbitrary", "arbitrary")),
        name="attn_prompt",
    )(sinks, q, k, k, v, v, z)


def _attn_sample_kernel(sink_ref, q_ref, kc_ref, vc_ref, kn_ref, vn_ref, z_ref, o_ref, ko_ref, vo_ref):
    hd = HEAD_DIM
    win = kc_ref.shape[1]
    n_kv = kc_ref.shape[2] // hd
    nq = q_ref.shape[1]
    grp = nq // n_kv
    pad = 8
    kc = kc_ref[0]
    vc = vc_ref[0]
    kn = kn_ref[0]
    vn = vn_ref[0]
    first = lax.broadcasted_iota(jnp.int32, (pad, kc.shape[1]), 0) == 0
    k_all = jnp.concatenate([kc, jnp.where(first, kn, 0.0)], axis=0).astype(BF16)
    v_all = jnp.concatenate([vc, jnp.where(first, vn, 0.0)], axis=0).astype(BF16)
    col = lax.broadcasted_iota(jnp.int32, (grp, win + pad), 1)
    valid = (col <= win) & (win - col <= WINDOW)
    q = q_ref[0].astype(BF16)
    row_i = lax.broadcasted_iota(jnp.int32, (grp, 1), 0)
    hs = range(n_kv)
    sks = []
    for h in hs:
        sk = jnp.zeros((grp, 1), F32)
        for gi in range(grp):
            sk = jnp.where(row_i == gi, sink_ref[h * grp + gi] * LOG2E, sk)
        sks.append(sk)
    ss = [jnp.where(valid, _dot_nt(q[h * grp:(h + 1) * grp, :], k_all[:, h * hd:(h + 1) * hd]), -jnp.inf)
          for h in hs]
    ms = [jnp.maximum(jnp.max(ss[h], axis=-1, keepdims=True), sks[h]) for h in hs]
    ps = [jnp.exp2(ss[h] - ms[h]) for h in hs]
    dens = [jnp.sum(ps[h], axis=-1, keepdims=True) + jnp.exp2(sks[h] - ms[h]) for h in hs]
    outs = [_dot(ps[h].astype(BF16), v_all[:, h * hd:(h + 1) * hd]) / dens[h] for h in hs]
    att = jnp.concatenate(outs, axis=0)
    o_ref[0] = (att * _silu(z_ref[0])).astype(o_ref.dtype)
    last = lax.broadcasted_iota(jnp.int32, kc.shape, 0) == win - 1
    ko_ref[0] = jnp.where(last, kn, pltpu.roll(kc, win - 1, axis=0))
    vo_ref[0] = jnp.where(last, vn, pltpu.roll(vc, win - 1, axis=0))


def attn_sample(sinks, q, cache_k, cache_v, k_new, v_new, z):
    m, win, kw = cache_k.shape
    nq = q.shape[1]
    hd = HEAD_DIM
    tok = pl.BlockSpec((1, nq, hd), lambda i: (i, 0, 0))
    cache = pl.BlockSpec((1, win, kw), lambda i: (i, 0, 0))
    new = pl.BlockSpec((1, 1, kw), lambda i: (i, 0, 0))
    return pl.pallas_call(
        _attn_sample_kernel,
        grid=(m,),
        in_specs=[pl.BlockSpec(memory_space=pltpu.SMEM), tok, cache, cache, new, new, tok],
        out_specs=[tok, cache, cache],
        out_shape=[jax.ShapeDtypeStruct((m, nq, hd), BF16),
                   jax.ShapeDtypeStruct(cache_k.shape, F32), jax.ShapeDtypeStruct(cache_v.shape, F32)],
        compiler_params=_cparams(("arbitrary",)),
        name="attn_sample",
    )(sinks, q, cache_k, cache_v, k_new, v_new, z)


def _pad_lora(w_down, w_up):
    r = w_down.shape[1]
    return (jnp.pad(w_down, ((0, 0), (0, LORA_PAD - r))).astype(BF16),
            jnp.pad(w_up, ((0, LORA_PAD - r), (0, 0))).astype(BF16))


def kernel(x_prompt, x_sample, state_wkv, state_shift, cache_k, cache_v, meta_tokens, a_norm, a_mu, a_w_rkvz,
           a_w0, a_w1, a_w2, a_a0, a_a1, a_a2, a_k_k, a_k_a, a_r_k, a_gn_g, a_gn_b, a_w_out, kv_norm, w_kv,
           b_norm, b_w_qz, b_sinks, b_w_o, final_norm):
    nb, seq, d = x_prompt.shape
    db, dseq, _ = x_sample.shape
    assert dseq == 1 and a_norm.shape[0] == 1 and b_norm.shape[0] == 1
    e = a_w_rkvz.shape[3]
    win = cache_k.shape[1]
    p_len = LEAD + N_META + seq
    assert p_len % BLOCK == 0 and (LEAD + N_META) == BLOCK
    kvw = N_KV_HEADS * HEAD_DIM

    w1, w2 = _pad_lora(a_w1[0], a_w2[0])
    a1, a2 = _pad_lora(a_a1[0], a_a2[0])
    mu = a_mu[0]
    sinks = b_sinks[0]
    gains_b = jnp.stack([kv_norm, b_norm[0]])

    tm = p_len // 8

    head = jnp.concatenate([jnp.zeros((LEAD, d), F32), meta_tokens], axis=0)
    w4 = a_w_rkvz[0]
    xm, hw_p, ha_p, x_last, w_rkvz = norm_shift_prompt(x_prompt, head, a_norm[0], mu, w1, a1,
                                                      cast_weights=(w4.reshape(-1, w4.shape[2]),))
    w_rkvz = w_rkvz.reshape(w4.shape)
    p_state_shift = x_last.reshape(1, nb, d)
    rkvz = matmul_groups(xm, w_rkvz, 4 * tm, F32, tn=e // 2)
    yg, p_state, w_out, w_kv_bf, w_qz, w_o = wkv_prompt(
        rkvz, hw_p, ha_p, w2, a_w0[0], a2, a_a0[0], a_k_k[0], a_k_a[0], a_r_k[0].reshape(-1), a_gn_g[0],
        a_gn_b[0], nb, cast_weights=(a_w_out[0], w_kv, b_w_qz[0], b_w_o[0]))
    w_qz = w_qz[None]
    hp, hn_kv, hn_b = matmul_residual_norm_blocks(yg, w_out, x_prompt, gains_b, nb, BF16, True, head=head)

    pos_p = jnp.maximum(jnp.arange(p_len, dtype=jnp.int32) - LEAD, 0)
    tabs_p = rope_tables(pos_p)
    k_p, v_p = matmul_rope(hn_kv, w_kv_bf, tabs_p, 4 * tm, kvw, (F32, F32))
    q_p, = matmul_rope(hn_b, w_qz, tabs_p, 2 * tm, e, (BF16,), scale=Q_SCALE, n=e)
    z_p = matmul_groups(hn_b[None], w_qz, 4 * tm, F32, n=e, col=1, tn=e // 2)[0]
    skip = (LEAD + N_META) // BLOCK
    att = attn_prompt(sinks, q_p, k_p, v_p, z_p, nb, skip)
    y_prompt, = matmul_residual_norm_blocks(att, w_o, hp, final_norm[None], nb, F32, False, first_block=skip)
    y_prompt = y_prompt.reshape(nb, seq, d)
    tail = lambda t: t.reshape(nb, p_len, kvw)[:, -win:].reshape(nb, win, N_KV_HEADS, HEAD_DIM)
    p_cache_k = tail(k_p)
    p_cache_v = tail(v_p)

    hs = x_sample.reshape(db, d)
    xm_s, hw_s, ha_s, xn_s = norm_shift_sample(hs, state_shift[0], a_norm[0], mu, w1, a1)
    rkvz_s = matmul_groups(xm_s, w_rkvz, db, F32)
    wl_s, al_s = lora_up(hw_s, ha_s, w2, a_w0[0], a2, a_a0[0])
    yg_s, s_state = wkv_sample(rkvz_s, wl_s, al_s, a_k_k[0], a_k_a[0], a_r_k[0].reshape(-1), a_gn_g[0],
                               a_gn_b[0], state_wkv[0])
    hs, hn_kv_s, hn_b_s = matmul_residual_norm(yg_s.reshape(db, e), w_out, hs, gains_b, db, BF16)
    tabs_s = rope_tables(jnp.full((db,), PAST_LEN, jnp.int32))
    k_s, v_s = matmul_rope(hn_kv_s, w_kv_bf, tabs_s, db, kvw, (F32, F32))
    q_s, = matmul_rope(hn_b_s, w_qz, tabs_s, db, e, (F32,), scale=Q_SCALE, n=e)
    z_s = matmul_groups(hn_b_s[None], w_qz, db, F32, n=e, col=1)[0]
    nq = e // HEAD_DIM
    att_s, s_cache_k, s_cache_v = attn_sample(
        sinks, q_s.reshape(db, nq, HEAD_DIM), cache_k.reshape(db, win, kvw), cache_v.reshape(db, win, kvw),
        k_s.reshape(db, 1, kvw), v_s.reshape(db, 1, kvw), z_s.reshape(db, nq, HEAD_DIM))
    y_s, = matmul_residual_norm(att_s.reshape(db, e), w_o, hs, final_norm[None], db, F32, emit_h=False)
    y_sample = y_s.reshape(db, 1, d)

    return (y_prompt, y_sample, p_state[None], p_state_shift,
            p_cache_k, p_cache_v,
            s_state[None], xn_s[None],
            s_cache_k.reshape(cache_k.shape), s_cache_v.reshape(cache_v.shape))
```

```python
import functools
import math

import jax
import jax.numpy as jnp
from jax import lax
from jax.experimental import pallas as pl
from jax.experimental.pallas import tpu as pltpu

F32 = jnp.float32
BF16 = jnp.bfloat16

HEAD_DIM = 64
N_KV_HEADS = 8
WINDOW = 128
BLOCK = 128
ROPE_DIM = HEAD_DIM // 4
ROPE_THETA = 500000.0
N_META = 16
PAST_LEN = 16384
RMS_EPS = 1e-6
GN_EPS = 64e-5
NORM_FLOOR_SQ = 1e-24
LEAD = (-N_META) % BLOCK
CHUNK = 64
WKV_ROWS = 128
CAST_CHUNKS = 32
SEQS_PER_STEP = 2
RESIDUAL_SEQS_PER_STEP = 4
HEADS_PER_STREAM = 32
HEADS_PER_STEP = 32
LORA_PAD = 128
MXU_TILE = 256
ROPE_SLAB = 512
VMEM_LIMIT = 48 * 1024 * 1024
RESIDUAL_VMEM_LIMIT = 56 * 1024 * 1024
LOG2E = 1.0 / math.log(2.0)
DECAY_SCALE = math.exp(-0.5)
SOFTMAX_BATCH = 4
Q_SCALE = HEAD_DIM ** -0.5 * LOG2E


def _cparams(sem, vmem_limit=VMEM_LIMIT):
    return pltpu.CompilerParams(dimension_semantics=sem, vmem_limit_bytes=vmem_limit)


def _sigmoid(x):
    return 1.0 / (1.0 + jnp.exp2(x * (-LOG2E)))


def _silu(x):
    return x * _sigmoid(x)


def _dot(a, b):
    return jnp.dot(a, b, preferred_element_type=F32)


def _dot_nt(a, b):
    return lax.dot_general(a, b, (((1,), (1,)), ((), ())), preferred_element_type=F32)


def _dot_tn(a, b):
    return lax.dot_general(a, b, (((0,), (0,)), ((), ())), preferred_element_type=F32)


def _split_hi_lo(x):
    hi = x.astype(BF16)
    lo = (x - hi.astype(F32)).astype(BF16)
    return hi, lo


def _cast_job(weights, n_steps, step_of):
    n_chunks = min(CAST_CHUNKS, n_steps)
    every = n_steps // n_chunks
    assert all(w.shape[0] % (n_chunks * 16) == 0 for w in weights)
    chunk = lambda *idx: (jnp.minimum(step_of(*idx) // every, n_chunks - 1), 0)
    return [pl.BlockSpec((w.shape[0] // n_chunks, w.shape[1]), chunk) for w in weights], every


def _cast_chunks(step, every, srcs, dsts):
    @pl.when(step % every == 0)
    def _():
        for src, dst in zip(srcs, dsts):
            dst[...] = src[...].astype(dst.dtype)


def _mixes(xn, prev, mu_ref, w1_ref, a1_ref, xm_ref, hw_ref, ha_ref):
    xx = prev - xn
    n_proj = xm_ref.shape[0]
    for p in range(n_proj):
        xm_ref[p] = (xn + xx * mu_ref[p:p + 1, :]).astype(xm_ref.dtype)
    xw = (xn + xx * mu_ref[n_proj:n_proj + 1, :]).astype(BF16)
    xa = (xn + xx * mu_ref[n_proj + 1:n_proj + 2, :]).astype(BF16)
    hw_ref[...] = jnp.tanh(_dot(xw, w1_ref[...])).astype(hw_ref.dtype)
    ha_ref[...] = _dot(xa, a1_ref[...]).astype(ha_ref.dtype)


def _norm_shift_kernel(x_ref, head_ref, g_ref, mu_ref, w1_ref, a1_ref, *rest, n_cast, cast_every):
    cast_in = rest[:n_cast]
    xm_ref, hw_ref, ha_ref, last_ref = rest[n_cast:n_cast + 4]
    cast_out = rest[n_cast + 4:2 * n_cast + 4]
    carry_ref = rest[-1]

    @pl.when(pl.program_id(1) == 0)
    def _():
        carry_ref[...] = jnp.zeros_like(carry_ref)

    _cast_chunks(pl.program_id(0) * pl.num_programs(1) + pl.program_id(1), cast_every, cast_in, cast_out)

    is_head = pl.program_id(1) == 0
    for sq in range(x_ref.shape[0]):
        x = jnp.where(is_head, head_ref[...], x_ref[sq])
        tm = x.shape[0]
        xn = x * lax.rsqrt(jnp.mean(x * x, axis=-1, keepdims=True) + RMS_EPS) * g_ref[...]
        rolled = pltpu.roll(xn, 1, axis=0)
        row = lax.broadcasted_iota(jnp.int32, xn.shape, 0)
        prev = jnp.where(row == 0, carry_ref[sq, 0:1, :], rolled)
        _mixes(xn, prev, mu_ref, w1_ref, a1_ref, xm_ref.at[:, sq], hw_ref.at[sq], ha_ref.at[sq])
        carry_ref[sq, 0:1, :] = xn[tm - 1:tm, :]
        last_ref[sq] = xn[tm - 1:tm, :]


def norm_shift_prompt(x, head, g, mu, w1, a1, cast_weights=()):
    b, seq, d = x.shape
    tm = BLOCK
    p = tm + seq
    n_mix = mu.shape[0]
    n_proj = n_mix - 2
    lr = w1.shape[1]
    nt = p // tm
    sp = SEQS_PER_STEP
    const = lambda shape: pl.BlockSpec(shape, lambda i, t: (0,) * len(shape))
    hid = pl.BlockSpec((sp, tm, lr), lambda i, t: (i, t, 0))
    cast_specs, every = _cast_job(cast_weights, (b // sp) * nt, lambda i, t: i * nt + t)
    xm, hw_act, ha_act, last, *casts = pl.pallas_call(
        functools.partial(_norm_shift_kernel, n_cast=len(cast_weights), cast_every=every),
        grid=(b // sp, nt),
        in_specs=[pl.BlockSpec((sp, tm, d), lambda i, t: (i, jnp.maximum(t - 1, 0), 0)),
                  const((tm, d)), const((1, d)), const((n_mix, d)), const((d, lr)), const((d, lr))] + cast_specs,
        out_specs=[pl.BlockSpec((n_proj, sp, tm, d), lambda i, t: (0, i, t, 0)), hid, hid,
                   pl.BlockSpec((sp, 1, d), lambda i, t: (i, 0, 0))] + cast_specs,
        out_shape=[jax.ShapeDtypeStruct((n_proj, b, p, d), BF16),
                   jax.ShapeDtypeStruct((b, p, lr), BF16), jax.ShapeDtypeStruct((b, p, lr), BF16),
                   jax.ShapeDtypeStruct((b, 1, d), F32)] + [jax.ShapeDtypeStruct(w.shape, BF16) for w in cast_weights],
        scratch_shapes=[pltpu.VMEM((sp, 8, d), F32)],
        compiler_params=_cparams(("arbitrary", "arbitrary")),
        name="norm_shift_prompt",
    )(x, head, g.reshape(1, d), mu, w1, a1, *cast_weights)
    return (xm.reshape(n_proj, b * p, d), hw_act.reshape(b * p, lr), ha_act.reshape(b * p, lr), last, *casts)


def _norm_shift_sample_kernel(x_ref, prev_ref, g_ref, mu_ref, w1_ref, a1_ref, xm_ref, hw_ref, ha_ref, xn_ref):
    x = x_ref[...]
    xn = x * lax.rsqrt(jnp.mean(x * x, axis=-1, keepdims=True) + RMS_EPS) * g_ref[...]
    xn_ref[...] = xn
    _mixes(xn, prev_ref[...], mu_ref, w1_ref, a1_ref, xm_ref, hw_ref, ha_ref)


def norm_shift_sample(x, prev, g, mu, w1, a1):
    m, d = x.shape
    lr = w1.shape[1]
    return pl.pallas_call(
        _norm_shift_sample_kernel,
        out_shape=[jax.ShapeDtypeStruct((mu.shape[0] - 2, m, d), BF16),
                   jax.ShapeDtypeStruct((m, lr), BF16), jax.ShapeDtypeStruct((m, lr), BF16),
                   jax.ShapeDtypeStruct((m, d), F32)],
        name="norm_shift_sample",
    )(x, prev, g.reshape(1, d), mu, w1, a1)


def _rope(y, cos, sin_a, sin_b):
    half = ROPE_DIM // 2
    step = ROPE_SLAB
    rep = step // cos.shape[1]
    tile = lambda t: jnp.concatenate([t] * rep, axis=1)
    cos_t, sa_t, sb_t = tile(cos), tile(sin_a), tile(sin_b)
    outs = []
    for j in range(y.shape[1] // step):
        ys = y[:, j * step:(j + 1) * step]
        outs.append(ys * cos_t + pltpu.roll(ys, step - half, axis=1) * sa_t + pltpu.roll(ys, half, axis=1) * sb_t)
    return jnp.concatenate(outs, axis=1) if len(outs) > 1 else outs[0]


def _mm_group_kernel(x_ref, w_ref, o_ref):
    o_ref[0] = _dot(x_ref[0], w_ref[0]).astype(o_ref.dtype)


def matmul_groups(x, w, tm, out_dtype, n=None, col=0, tn=None):
    g = w.shape[0]
    n = w.shape[2] if n is None else n
    tn = n if tn is None else tn
    nt = n // tn
    _, m, kdim = x.shape
    return pl.pallas_call(
        _mm_group_kernel,
        grid=(g, nt, m // tm),
        in_specs=[pl.BlockSpec((1, tm, kdim), lambda q, j, i: (q, i, 0)),
                  pl.BlockSpec((1, kdim, tn), lambda q, j, i: (q, 0, col * nt + j))],
        out_specs=pl.BlockSpec((1, tm, tn), lambda q, j, i: (q, i, j)),
        out_shape=jax.ShapeDtypeStruct((g, m, n), out_dtype),
        compiler_params=_cparams(("arbitrary", "arbitrary", "arbitrary")),
        name="matmul_groups",
    )(x, w)


def _mm_rope_kernel(x_ref, w_ref, cos_ref, sa_ref, sb_ref, *o_refs, n_rope, scale):
    y = _dot(x_ref[...], w_ref[...])
    rot = _rope(y[:, :n_rope], cos_ref[...], sa_ref[...], sb_ref[...])
    if scale != 1.0:
        rot = rot * scale
    o_refs[0][...] = rot.astype(o_refs[0].dtype)
    if len(o_refs) > 1:
        o_refs[1][...] = y[:, n_rope:].astype(o_refs[1].dtype)


def matmul_rope(x, w, tables, tm, n_rope, out_dtypes, scale=1.0, n=None):
    m, kdim = x.shape
    n = w.shape[-1] if n is None else n
    w_block = (kdim, n) if w.ndim == 2 else (None, kdim, n)
    lanes = tables[0].shape[1]
    widths = [n_rope] + ([n - n_rope] if n > n_rope else [])
    tab_blocks = tables[0].shape[0] // tm
    tab = pl.BlockSpec((tm, lanes), lambda i: (i % tab_blocks, 0))
    outs = pl.pallas_call(
        functools.partial(_mm_rope_kernel, n_rope=n_rope, scale=scale),
        grid=(m // tm,),
        in_specs=[pl.BlockSpec((tm, kdim), lambda i: (i, 0)),
                  pl.BlockSpec(w_block, lambda i: (0,) * w.ndim, pipeline_mode=pl.Buffered(1)),
                  tab, tab, tab],
        out_specs=[pl.BlockSpec((tm, wd), lambda i: (i, 0)) for wd in widths],
        out_shape=[jax.ShapeDtypeStruct((m, wd), dt) for wd, dt in zip(widths, out_dtypes)],
        compiler_params=_cparams(("arbitrary",)),
        name="matmul_rope",
    )(x, w, *tables)
    return outs


def _mm_res_norm_kernel(x_ref, w_ref, res_ref, g_ref, *out_refs, emit_h):
    h = res_ref[...] + _dot(x_ref[...], w_ref[...])
    hn_refs = out_refs
    if emit_h:
        out_refs[0][...] = h
        hn_refs = out_refs[1:]
    inv = lax.rsqrt(jnp.mean(h * h, axis=-1, keepdims=True) + RMS_EPS)
    for j, hn_ref in enumerate(hn_refs):
        hn_ref[...] = (h * inv * g_ref[j:j + 1, :]).astype(hn_ref.dtype)


def matmul_residual_norm(x, w, res, gains, tm, norm_dtype, emit_h=True):
    m, kdim = x.shape
    n = w.shape[1]
    ng = gains.shape[0]
    row = lambda width: pl.BlockSpec((tm, width), lambda i: (i, 0))
    return pl.pallas_call(
        functools.partial(_mm_res_norm_kernel, emit_h=emit_h),
        grid=(m // tm,),
        in_specs=[row(kdim),
                  pl.BlockSpec((kdim, n), lambda i: (0, 0), pipeline_mode=pl.Buffered(1)),
                  row(n),
                  pl.BlockSpec((ng, n), lambda i: (0, 0))],
        out_specs=[row(n)] * (int(emit_h) + ng),
        out_shape=[jax.ShapeDtypeStruct((m, n), F32)] * int(emit_h) + [jax.ShapeDtypeStruct((m, n), norm_dtype)] * ng,
        compiler_params=_cparams(("arbitrary",)),
        name="matmul_residual_norm",
    )(x, w, res, gains)


def _mm_res_norm_blocks_kernel(x_ref, w_ref, res_ref, *rest, emit_h, head, first_block):
    if head:
        head_ref, g_ref, *out_refs = rest
        res = jnp.where(pl.program_id(1) + first_block == 0, head_ref[...][None], res_ref[...])
    else:
        g_ref, *out_refs = rest
        res = res_ref[...]
    seqs, blk, n = res.shape
    h = res.reshape(seqs * blk, n) + _dot(x_ref[...].reshape(seqs * blk, x_ref.shape[2]), w_ref[...])
    hn_refs = out_refs
    if emit_h:
        out_refs[0][...] = h.reshape(seqs, blk, n)
        hn_refs = out_refs[1:]
    inv = lax.rsqrt(jnp.mean(h * h, axis=-1, keepdims=True) + RMS_EPS)
    for j, hn_ref in enumerate(hn_refs):
        hn_ref[...] = (h * inv * g_ref[j:j + 1, :]).astype(hn_ref.dtype).reshape(seqs, blk, n)


def matmul_residual_norm_blocks(x, w, res, gains, n_batch, norm_dtype, emit_h, head=None, first_block=0):
    kdim = x.shape[1]
    n = w.shape[1]
    ng = gains.shape[0]
    nb_out = x.shape[0] // (n_batch * BLOCK)
    nb = nb_out + first_block
    sp = RESIDUAL_SEQS_PER_STEP
    blocks = lambda width, shift: pl.BlockSpec((sp, BLOCK, width), lambda i, t: (i, t + shift, 0))
    const = lambda shape: pl.BlockSpec(shape, lambda i, t: (0,) * len(shape))
    if head is not None:
        res_specs = [pl.BlockSpec((sp, BLOCK, n), lambda i, t: (i, jnp.maximum(t + first_block - 1, 0), 0)),
                     const((BLOCK, n))]
        res_args = [res, head]
    else:
        res_specs, res_args = [blocks(n, first_block)], [res.reshape(n_batch, nb * BLOCK, n)]
    outs = pl.pallas_call(
        functools.partial(_mm_res_norm_blocks_kernel, emit_h=emit_h, head=head is not None, first_block=first_block),
        grid=(n_batch // sp, nb_out),
        in_specs=[blocks(kdim, 0), pl.BlockSpec((kdim, n), lambda i, t: (0, 0), pipeline_mode=pl.Buffered(1))]
        + res_specs + [const((ng, n))],
        out_specs=[blocks(n, first_block)] * int(emit_h) + [blocks(n, 0)] * ng,
        out_shape=[jax.ShapeDtypeStruct((n_batch, nb * BLOCK, n), F32)] * int(emit_h)
        + [jax.ShapeDtypeStruct((n_batch, nb_out * BLOCK, n), norm_dtype)] * ng,
        compiler_params=_cparams(("arbitrary", "arbitrary"), RESIDUAL_VMEM_LIMIT),
        name="matmul_residual_norm_blocks",
    )(x.reshape(n_batch, nb_out * BLOCK, kdim), w, *res_args, gains)
    return [o.reshape(-1, n) for o in outs]


def _lora_up_kernel(hw_ref, ha_ref, w2_ref, w0_ref, a2_ref, a0_ref, wl_ref, al_ref):
    wl_ref[...] = w0_ref[...] + _dot(hw_ref[...], w2_ref[...])
    al_ref[...] = a0_ref[...] + _dot(ha_ref[...], a2_ref[...])


def lora_up(hw, ha, w2, w0, a2, a0):
    m = hw.shape[0]
    e = w2.shape[1]
    return pl.pallas_call(
        _lora_up_kernel,
        out_shape=[jax.ShapeDtypeStruct((m, e), F32), jax.ShapeDtypeStruct((m, e), F32)],
        name="lora_up",
    )(hw, ha, w2, w0.reshape(1, e), a2, a0.reshape(1, e))


def _seg_sum(x, ones_bd):
    hi = x.astype(BF16)
    outs = []
    for c in range(x.shape[1] // MXU_TILE):
        sl = slice(c * MXU_TILE, (c + 1) * MXU_TILE)
        outs.append(_dot(hi[:, sl], ones_bd))
    return jnp.concatenate(outs, axis=1) if len(outs) > 1 else outs[0]


def _wkv_prompt_kernel(r_ref, k_ref, v_ref, z_ref, hw_ref, ha_ref, w2_ref, w0_ref, a2_ref, a0_ref,
                       kk_ref, ka_ref, rk_ref, gg_ref, gb_ref, *rest, n_cast, cast_every):
    cast_in = rest[:n_cast]
    yg_ref, sout_ref = rest[n_cast:n_cast + 2]
    cast_out = rest[n_cast + 2:2 * n_cast + 2]
    s_ref = rest[-1]
    t_idx = pl.program_id(2)
    c = CHUNK
    hd = HEAD_DIM

    @pl.when(t_idx == 0)
    def _():
        s_ref[...] = jnp.zeros_like(s_ref)

    step = (pl.program_id(0) * pl.num_programs(1) + pl.program_id(1)) * pl.num_programs(2) + t_idx
    _cast_chunks(step, cast_every, cast_in, cast_out)

    tb = r_ref.shape[1]
    nh = HEADS_PER_STREAM
    hw = nh * hd
    nc = tb // c

    li = lax.broadcasted_iota(jnp.int32, (MXU_TILE, MXU_TILE), 0) // hd
    lj = lax.broadcasted_iota(jnp.int32, (MXU_TILE, MXU_TILE), 1) // hd
    ones_bd = jnp.where(li == lj, 1.0, 0.0).astype(BF16)
    bi_ = lax.broadcasted_iota(jnp.int32, (tb, tb), 0)
    bj_ = lax.broadcasted_iota(jnp.int32, (tb, tb), 1)
    tri_incl = jnp.where((bj_ <= bi_) & (bj_ // c == bi_ // c), 1.0, 0.0).astype(BF16)
    ti = lax.broadcasted_iota(jnp.int32, (c, c), 0)
    tj = lax.broadcasted_iota(jnp.int32, (c, c), 1)
    ai = lax.broadcasted_iota(jnp.int32, (c, 2 * c), 0)
    aj = lax.broadcasted_iota(jnp.int32, (c, 2 * c), 1)
    upper = aj >= c
    aj_mod = jnp.where(upper, aj - c, aj)
    masks = dict(
        strict=tj < ti,
        eye=jnp.where(ti == tj, 1.0, 0.0).astype(F32),
        top_k=upper & (aj_mod < ai),
        bot=aj_mod <= ai,
        mean_bd=jnp.where(li == lj, 1.0 / hd, 0.0).astype(BF16))

    for st in range(r_ref.shape[2] // hw):
        _wkv_stream(st, hw, nc, r_ref, k_ref, v_ref, z_ref, hw_ref, ha_ref, w2_ref, w0_ref, a2_ref, a0_ref,
                    kk_ref, ka_ref, rk_ref, gg_ref, gb_ref, yg_ref, s_ref, ones_bd, tri_incl, masks)

    @pl.when(t_idx == pl.num_programs(2) - 1)
    def _():
        sout_ref[0] = s_ref[...]


def _wkv_stream(st, hw, nc, r_ref, k_ref, v_ref, z_ref, hw_ref, ha_ref, w2_ref, w0_ref, a2_ref, a0_ref,
                kk_ref, ka_ref, rk_ref, gg_ref, gb_ref, yg_ref, s_ref, ones_bd, tri_incl, masks):
    c = CHUNK
    hd = HEAD_DIM
    nh = hw // hd
    ls = slice(st * hw, (st + 1) * hw)
    h0 = st * nh
    r = r_ref[0, :, ls]
    k = k_ref[0, :, ls]
    v = v_ref[0, :, ls]
    wl = w0_ref[:, ls] + _dot(hw_ref[...], w2_ref[:, ls])
    al = a0_ref[:, ls] + _dot(ha_ref[...], a2_ref[:, ls])
    a = _sigmoid(al)
    lw = (-DECAY_SCALE * LOG2E) * _sigmoid(wl)
    kk = k * kk_ref[:, ls]
    n2 = _seg_sum(kk * kk, ones_bd)
    kk = kk * lax.rsqrt(jnp.maximum(n2, NORM_FLOOR_SQ))
    ka = ka_ref[:, ls]
    k2 = k * (a * ka + (1.0 - ka))
    bb = kk * a

    lw_hi, lw_lo = _split_hi_lo(lw)
    g = _dot(tri_incl, lw_hi) + _dot(tri_incl, lw_lo)
    mid = lambda ci: g[ci * c + c // 2 - 1:ci * c + c // 2, :]
    gm = jnp.concatenate([jnp.broadcast_to(mid(ci), (c, hw)) for ci in range(nc)], axis=0)
    t = g - gm
    e_a = jnp.exp2(t)
    e_prev = jnp.exp2(t - lw)
    e_inv = jnp.exp2(-t)
    e1 = [jnp.exp2(mid(ci)) for ci in range(nc)]
    e2 = [jnp.exp2(g[ci * c + c - 1:ci * c + c, :] - mid(ci)) for ci in range(nc)]

    kkd = (kk * e_prev).astype(BF16)
    rd = (r * e_a).astype(BF16)
    bi = (bb * e_inv).astype(BF16)
    ki = (k2 * e_inv).astype(BF16)
    v_bf = v.astype(BF16)
    zeros_cv = jnp.zeros((c, hd), BF16)

    pairs = [(ci, h) for ci in range(nc) for h in range(nh)]
    rows = lambda ci: slice(ci * c, (ci + 1) * c)
    cols = lambda h: slice(h * hd, (h + 1) * hd)
    xs = {(ci, h): jnp.concatenate([kkd[rows(ci), cols(h)], rd[rows(ci), cols(h)]], axis=0) for ci, h in pairs}
    r1s = {(ci, h): jnp.concatenate([bi[rows(ci), cols(h)], ki[rows(ci), cols(h)]], axis=0) for ci, h in pairs}
    vs = {(ci, h): v_bf[rows(ci), cols(h)] for ci, h in pairs}
    a_mats = {p: _dot_nt(xs[p], r1s[p]) for p in pairs}
    lk_vs = {p: _dot(jnp.where(masks["top_k"], a_mats[p][:c, :], 0.0).astype(BF16),
                     jnp.concatenate([zeros_cv, vs[p]], axis=0)) for p in pairs}
    lps = {p: jnp.where(masks["strict"], a_mats[p][:c, :c], 0.0) for p in pairs}
    ts = {p: masks["eye"] - lps[p] for p in pairs}
    for _ in range(int(math.log2(c)) - 1):
        lpb = {p: lps[p].astype(BF16) for p in pairs}
        lps = {p: _dot(lpb[p], lpb[p]) for p in pairs}
        ts = {p: _dot(ts[p].astype(BF16), (masks["eye"] + lps[p]).astype(BF16)) for p in pairs}
    a_bots = {p: jnp.where(masks["bot"], a_mats[p][c:, :], 0.0).astype(BF16) for p in pairs}
    t_bf = {p: ts[p].astype(BF16) for p in pairs}

    state = [s_ref[h0 + h] for h in range(nh)]
    y_rows = []
    for ci in range(nc):
        hs = range(nh)
        sms = [state[h] * e1[ci][:, cols(h)] for h in hs]
        p_mats = [_dot_nt(xs[ci, h], sms[h].astype(BF16)) for h in hs]
        us = [-_dot(t_bf[ci, h], (p_mats[h][:c, :] + lk_vs[ci, h]).astype(BF16)) for h in hs]
        uvs = [jnp.concatenate([us[h].astype(BF16), vs[ci, h]], axis=0) for h in hs]
        ys = [p_mats[h][c:, :] + _dot(a_bots[ci, h], uvs[h]) for h in hs]
        state = [(sms[h] + _dot_tn(uvs[h], r1s[ci, h])) * e2[ci][:, cols(h)] for h in hs]
        y_rows.append(jnp.concatenate(ys, axis=1))
    for h in range(nh):
        s_ref[h0 + h] = state[h]
    y = jnp.concatenate(y_rows, axis=0) if nc > 1 else y_rows[0]

    mean = _seg_sum(y, masks["mean_bd"])
    yc = y - mean
    var = _seg_sum(yc * yc, masks["mean_bd"])
    yn = yc * lax.rsqrt(var + GN_EPS) * gg_ref[:, ls] + gb_ref[:, ls]
    bonus = _seg_sum(r * k2 * rk_ref[:, ls], ones_bd)
    yg_ref[:, ls] = ((yn + bonus * v) * _silu(z_ref[0, :, ls])).astype(BF16)


def wkv_prompt(rkvz, hw_act, ha_act, w2, w0, a2, a0, k_k, k_a, r_k, gn_g, gn_b, n_batch, cast_weights=()):
    _, m, e = rkvz.shape
    p = m // n_batch
    tb = WKV_ROWS
    nt = p // tb
    hw = HEADS_PER_STEP * HEAD_DIM
    nh = e // HEAD_DIM
    row = lambda i, g, t: (i * nt + t, g)
    proj = lambda q: pl.BlockSpec((1, tb, hw), lambda i, g, t: (q, i * nt + t, g))
    par = pl.BlockSpec((1, hw), lambda i, g, t: (0, g))
    lr = w2.shape[0]
    hid = pl.BlockSpec((tb, lr), lambda i, g, t: (i * nt + t, 0))
    up = pl.BlockSpec((lr, hw), lambda i, g, t: (0, g))
    n_groups = e // hw
    cast_specs, every = _cast_job(cast_weights, n_batch * n_groups * nt, lambda i, g, t: (i * n_groups + g) * nt + t)
    return pl.pallas_call(
        functools.partial(_wkv_prompt_kernel, n_cast=len(cast_weights), cast_every=every),
        grid=(n_batch, n_groups, nt),
        in_specs=[proj(0), proj(1), proj(2), proj(3), hid, hid, up, par, up, par,
                  par, par, par, par, par] + cast_specs,
        out_specs=[pl.BlockSpec((tb, hw), row),
                   pl.BlockSpec((1, HEADS_PER_STEP, HEAD_DIM, HEAD_DIM), lambda i, g, t: (i, g, 0, 0))] + cast_specs,
        out_shape=[jax.ShapeDtypeStruct((m, e), BF16),
                   jax.ShapeDtypeStruct((n_batch, nh, HEAD_DIM, HEAD_DIM), F32)]
        + [jax.ShapeDtypeStruct(w.shape, BF16) for w in cast_weights],
        scratch_shapes=[pltpu.VMEM((HEADS_PER_STEP, HEAD_DIM, HEAD_DIM), F32)],
        compiler_params=_cparams(("arbitrary", "arbitrary", "arbitrary")),
        name="wkv_prompt",
    )(rkvz, rkvz, rkvz, rkvz, hw_act, ha_act, w2, w0.reshape(1, e), a2, a0.reshape(1, e),
      k_k.reshape(1, e), k_a.reshape(1, e), r_k.reshape(1, e), gn_g.reshape(1, e), gn_b.reshape(1, e),
      *cast_weights)


def _wkv_sample_kernel(r_ref, k_ref, v_ref, z_ref, wl_ref, al_ref, kk_ref, ka_ref, rk_ref, gg_ref, gb_ref,
                       s_ref, yg_ref, sout_ref, y_scr):
    hd = HEAD_DIM
    r = r_ref[0]
    k = k_ref[0]
    v = v_ref[0]
    a = _sigmoid(al_ref[0])
    d = jnp.exp2((-DECAY_SCALE * LOG2E) * _sigmoid(wl_ref[0]))
    kk = k * kk_ref[...]
    kk = kk / jnp.maximum(jnp.sqrt(jnp.sum(kk * kk, axis=-1, keepdims=True)), 1e-12)
    k2 = k * (1.0 + (a - 1.0) * ka_ref[...])
    bb = kk * a
    nh = r.shape[0]
    ii = lax.broadcasted_iota(jnp.int32, (hd, hd), 0)
    jj = lax.broadcasted_iota(jnp.int32, (hd, hd), 1)
    eye = ii == jj

    row = lambda x, h: x[h:h + 1, :]
    group = 8
    for h0 in range(0, nh, group):
        hs = range(h0, h0 + group)
        s = {h: s_ref[0, h] for h in hs}
        sa = {h: jnp.sum(s[h] * row(kk, h), axis=-1, keepdims=True) for h in hs}
        v_col = {h: jnp.sum(jnp.where(eye, row(v, h), 0.0), axis=-1, keepdims=True) for h in hs}
        s_new = {h: s[h] * row(d, h) - sa[h] * row(bb, h) + v_col[h] * row(k2, h) for h in hs}
        y_col = {h: jnp.sum(s_new[h] * row(r, h), axis=-1, keepdims=True) for h in hs}
        for h in hs:
            sout_ref[0, h] = s_new[h]
            y_scr[h:h + 1, :] = jnp.sum(jnp.where(eye, y_col[h], 0.0), axis=0, keepdims=True)

    y = y_scr[...]
    mean = jnp.mean(y, axis=-1, keepdims=True)
    yc = y - mean
    var = jnp.mean(yc * yc, axis=-1, keepdims=True)
    yn = yc * lax.rsqrt(var + GN_EPS) * gg_ref[...] + gb_ref[...]
    bonus = jnp.sum(r * k2 * rk_ref[...], axis=-1, keepdims=True)
    yg_ref[0] = ((yn + bonus * v) * _silu(z_ref[0])).astype(BF16)


def wkv_sample(rkvz, wl, al, k_k, k_a, r_k, gn_g, gn_b, state):
    _, m, e = rkvz.shape
    nh = e // HEAD_DIM
    hd = HEAD_DIM
    rkvz4 = rkvz.reshape(4, m, nh, hd)
    proj = lambda q: pl.BlockSpec((None, 1, nh, hd), lambda i: (q, i, 0, 0))
    tok = pl.BlockSpec((1, nh, hd), lambda i: (i, 0, 0))
    par = pl.BlockSpec((nh, hd), lambda i: (0, 0))
    st = pl.BlockSpec((1, nh, hd, hd), lambda i: (i, 0, 0, 0))
    as_heads = lambda x: x.reshape(nh, hd)
    return pl.pallas_call(
        _wkv_sample_kernel,
        grid=(m,),
        in_specs=[proj(0), proj(1), proj(2), proj(3), tok, tok, par, par, par, par, par, st],
        out_specs=[tok, st],
        out_shape=[jax.ShapeDtypeStruct((m, nh, hd), BF16), jax.ShapeDtypeStruct(state.shape, F32)],
        scratch_shapes=[pltpu.VMEM((nh, hd), F32)],
        compiler_params=_cparams(("arbitrary",)),
        name="wkv_sample",
    )(rkvz4, rkvz4, rkvz4, rkvz4, wl.reshape(m, nh, hd), al.reshape(m, nh, hd),
      as_heads(k_k), as_heads(k_a), as_heads(r_k), as_heads(gn_g), as_heads(gn_b), state)


def rope_tables(pos):
    half = ROPE_DIM // 2
    inv_freq = ROPE_THETA ** (-jnp.arange(half, dtype=F32) * 2.0 / ROPE_DIM)
    ang = pos.astype(F32)[:, None] * inv_freq[None, :]
    cos = jnp.cos(ang)
    sin = jnp.sin(ang)
    rows = pos.shape[0]
    ones = jnp.ones((rows, HEAD_DIM - ROPE_DIM), F32)
    zeros_h = jnp.zeros((rows, half), F32)
    zeros_r = jnp.zeros((rows, HEAD_DIM - ROPE_DIM), F32)
    cos_h = jnp.concatenate([cos, cos, ones], axis=1)
    sa_h = jnp.concatenate([-sin, zeros_h, zeros_r], axis=1)
    sb_h = jnp.concatenate([zeros_h, sin, zeros_r], axis=1)
    two = lambda t: jnp.concatenate([t, t], axis=1)
    return two(cos_h), two(sa_h), two(sb_h)


def _attn_prompt_kernel(sink_ref, q_ref, kc_ref, kp_ref, vc_ref, vp_ref, z_ref, o_ref, *, first_block):
    n = pl.program_id(1) + first_block
    hd = HEAD_DIM
    blk = q_ref.shape[0]
    n_kv = kc_ref.shape[1] // hd
    grp = q_ref.shape[1] // (n_kv * hd)
    qi = lax.broadcasted_iota(jnp.int32, (blk, 2 * blk), 0)
    kj = lax.broadcasted_iota(jnp.int32, (blk, 2 * blk), 1) - blk
    kpos = n * blk + kj
    diff = qi - kj
    valid = (kpos >= LEAD) & (diff >= 0) & (diff <= WINDOW)
    k_all = jnp.concatenate([kp_ref[...], kc_ref[...]], axis=0).astype(BF16)
    v_all = jnp.concatenate([vp_ref[...], vc_ref[...]], axis=0).astype(BF16)

    def scores(h):
        k_h = k_all[:, h * hd:(h + 1) * hd]
        return [_dot_nt(q_ref[:, (h * grp + gi) * hd:(h * grp + gi + 1) * hd], k_h) for gi in range(grp)]

    outs = []
    s_next = scores(0)
    for h in range(n_kv):
        s_cur = s_next
        if h + 1 < n_kv:
            s_next = scores(h + 1)
        v_h = v_all[:, h * hd:(h + 1) * hd]
        for g0 in range(0, grp, SOFTMAX_BATCH):
            gs = range(g0, min(g0 + SOFTMAX_BATCH, grp))
            sks = {gi: sink_ref[h * grp + gi] * LOG2E for gi in gs}
            ss = {gi: jnp.where(valid, s_cur[gi], -jnp.inf) for gi in gs}
            ms = {gi: jnp.maximum(jnp.max(ss[gi], axis=-1, keepdims=True), sks[gi]) for gi in gs}
            ps = {gi: jnp.exp2(ss[gi] - ms[gi]) for gi in gs}
            dens = {gi: jnp.sum(ps[gi], axis=-1, keepdims=True) + jnp.exp2(sks[gi] - ms[gi]) for gi in gs}
            outs += [_dot(ps[gi].astype(BF16), v_h) / dens[gi] for gi in gs]
    att = jnp.concatenate(outs, axis=1)
    o_ref[...] = (att * _silu(z_ref[...])).astype(o_ref.dtype)


def attn_prompt(sinks, q, k, v, z, n_batch, first_block):
    m, e = q.shape
    nb = m // (n_batch * BLOCK)
    nb_out = nb - first_block
    kw = k.shape[1]
    cur = lambda i, n: (i * nb + n + first_block, 0)
    prv = lambda i, n: (i * nb + jnp.maximum(n + first_block - 1, 0), 0)
    return pl.pallas_call(
        functools.partial(_attn_prompt_kernel, first_block=first_block),
        grid=(n_batch, nb_out),
        in_specs=[pl.BlockSpec(memory_space=pltpu.SMEM),
                  pl.BlockSpec((BLOCK, e), cur),
                  pl.BlockSpec((BLOCK, kw), cur), pl.BlockSpec((BLOCK, kw), prv),
                  pl.BlockSpec((BLOCK, kw), cur), pl.BlockSpec((BLOCK, kw), prv),
                  pl.BlockSpec((BLOCK, e), cur)],
        out_specs=pl.BlockSpec((BLOCK, e), lambda i, n: (i * nb_out + n, 0)),
        out_shape=jax.ShapeDtypeStruct((n_batch * nb_out * BLOCK, e), BF16),
        compiler_params=_cparams(("arbitrary", "arbitrary")),
        name="attn_prompt",
    )(sinks, q, k, k, v, v, z)


def _attn_sample_kernel(sink_ref, q_ref, kc_ref, vc_ref, kn_ref, vn_ref, z_ref, o_ref, ko_ref, vo_ref):
    hd = HEAD_DIM
    win = kc_ref.shape[1]
    n_kv = kc_ref.shape[2] // hd
    nq = q_ref.shape[1]
    grp = nq // n_kv
    pad = 8
    kc = kc_ref[0]
    vc = vc_ref[0]
    kn = kn_ref[0]
    vn = vn_ref[0]
    first = lax.broadcasted_iota(jnp.int32, (pad, kc.shape[1]), 0) == 0
    k_all = jnp.concatenate([kc, jnp.where(first, kn, 0.0)], axis=0).astype(BF16)
    v_all = jnp.concatenate([vc, jnp.where(first, vn, 0.0)], axis=0).astype(BF16)
    col = lax.broadcasted_iota(jnp.int32, (grp, win + pad), 1)
    valid = (col <= win) & (win - col <= WINDOW)
    q = q_ref[0].astype(BF16)
    row_i = lax.broadcasted_iota(jnp.int32, (grp, 1), 0)
    hs = range(n_kv)
    sks = []
    for h in hs:
        sk = jnp.zeros((grp, 1), F32)
        for gi in range(grp):
            sk = jnp.where(row_i == gi, sink_ref[h * grp + gi] * LOG2E, sk)
        sks.append(sk)
    ss = [jnp.where(valid, _dot_nt(q[h * grp:(h + 1) * grp, :], k_all[:, h * hd:(h + 1) * hd]), -jnp.inf)
          for h in hs]
    ms = [jnp.maximum(jnp.max(ss[h], axis=-1, keepdims=True), sks[h]) for h in hs]
    ps = [jnp.exp2(ss[h] - ms[h]) for h in hs]
    dens = [jnp.sum(ps[h], axis=-1, keepdims=True) + jnp.exp2(sks[h] - ms[h]) for h in hs]
    outs = [_dot(ps[h].astype(BF16), v_all[:, h * hd:(h + 1) * hd]) / dens[h] for h in hs]
    att = jnp.concatenate(outs, axis=0)
    o_ref[0] = (att * _silu(z_ref[0])).astype(o_ref.dtype)
    last = lax.broadcasted_iota(jnp.int32, kc.shape, 0) == win - 1
    ko_ref[0] = jnp.where(last, kn, pltpu.roll(kc, win - 1, axis=0))
    vo_ref[0] = jnp.where(last, vn, pltpu.roll(vc, win - 1, axis=0))


def attn_sample(sinks, q, cache_k, cache_v, k_new, v_new, z):
    m, win, kw = cache_k.shape
    nq = q.shape[1]
    hd = HEAD_DIM
    tok = pl.BlockSpec((1, nq, hd), lambda i: (i, 0, 0))
    cache = pl.BlockSpec((1, win, kw), lambda i: (i, 0, 0))
    new = pl.BlockSpec((1, 1, kw), lambda i: (i, 0, 0))
    return pl.pallas_call(
        _attn_sample_kernel,
        grid=(m,),
        in_specs=[pl.BlockSpec(memory_space=pltpu.SMEM), tok, cache, cache, new, new, tok],
        out_specs=[tok, cache, cache],
        out_shape=[jax.ShapeDtypeStruct((m, nq, hd), BF16),
                   jax.ShapeDtypeStruct(cache_k.shape, F32), jax.ShapeDtypeStruct(cache_v.shape, F32)],
        compiler_params=_cparams(("arbitrary",)),
        name="attn_sample",
    )(sinks, q, cache_k, cache_v, k_new, v_new, z)


def _pad_lora(w_down, w_up):
    r = w_down.shape[1]
    return (jnp.pad(w_down, ((0, 0), (0, LORA_PAD - r))).astype(BF16),
            jnp.pad(w_up, ((0, LORA_PAD - r), (0, 0))).astype(BF16))


def kernel(x_prompt, x_sample, state_wkv, state_shift, cache_k, cache_v, meta_tokens, a_norm, a_mu, a_w_rkvz,
           a_w0, a_w1, a_w2, a_a0, a_a1, a_a2, a_k_k, a_k_a, a_r_k, a_gn_g, a_gn_b, a_w_out, kv_norm, w_kv,
           b_norm, b_w_qz, b_sinks, b_w_o, final_norm):
    nb, seq, d = x_prompt.shape
    db, dseq, _ = x_sample.shape
    assert dseq == 1 and a_norm.shape[0] == 1 and b_norm.shape[0] == 1
    e = a_w_rkvz.shape[3]
    win = cache_k.shape[1]
    p_len = LEAD + N_META + seq
    assert p_len % BLOCK == 0 and (LEAD + N_META) == BLOCK
    kvw = N_KV_HEADS * HEAD_DIM

    w1, w2 = _pad_lora(a_w1[0], a_w2[0])
    a1, a2 = _pad_lora(a_a1[0], a_a2[0])
    mu = a_mu[0]
    sinks = b_sinks[0]
    gains_b = jnp.stack([kv_norm, b_norm[0]])

    tm = p_len // 8

    head = jnp.concatenate([jnp.zeros((LEAD, d), F32), meta_tokens], axis=0)
    w4 = a_w_rkvz[0]
    xm, hw_p, ha_p, x_last, w_rkvz = norm_shift_prompt(x_prompt, head, a_norm[0], mu, w1, a1,
                                                      cast_weights=(w4.reshape(-1, w4.shape[2]),))
    w_rkvz = w_rkvz.reshape(w4.shape)
    p_state_shift = x_last.reshape(1, nb, d)
    rkvz = matmul_groups(xm, w_rkvz, 4 * tm, F32, tn=e // 2)
    yg, p_state, w_out, w_kv_bf, w_qz, w_o = wkv_prompt(
        rkvz, hw_p, ha_p, w2, a_w0[0], a2, a_a0[0], a_k_k[0], a_k_a[0], a_r_k[0].reshape(-1), a_gn_g[0],
        a_gn_b[0], nb, cast_weights=(a_w_out[0], w_kv, b_w_qz[0], b_w_o[0]))
    w_qz = w_qz[None]
    hp, hn_kv, hn_b = matmul_residual_norm_blocks(yg, w_out, x_prompt, gains_b, nb, BF16, True, head=head)

    pos_p = jnp.maximum(jnp.arange(p_len, dtype=jnp.int32) - LEAD, 0)
    tabs_p = rope_tables(pos_p)
    k_p, v_p = matmul_rope(hn_kv, w_kv_bf, tabs_p, 4 * tm, kvw, (F32, F32))
    q_p, = matmul_rope(hn_b, w_qz, tabs_p, 2 * tm, e, (BF16,), scale=Q_SCALE, n=e)
    z_p = matmul_groups(hn_b[None], w_qz, 4 * tm, F32, n=e, col=1, tn=e // 2)[0]
    skip = (LEAD + N_META) // BLOCK
    att = attn_prompt(sinks, q_p, k_p, v_p, z_p, nb, skip)
    y_prompt, = matmul_residual_norm_blocks(att, w_o, hp, final_norm[None], nb, F32, False, first_block=skip)
    y_prompt = y_prompt.reshape(nb, seq, d)
    tail = lambda t: t.reshape(nb, p_len, kvw)[:, -win:].reshape(nb, win, N_KV_HEADS, HEAD_DIM)
    p_cache_k = tail(k_p)
    p_cache_v = tail(v_p)

    hs = x_sample.reshape(db, d)
    xm_s, hw_s, ha_s, xn_s = norm_shift_sample(hs, state_shift[0], a_norm[0], mu, w1, a1)
    rkvz_s = matmul_groups(xm_s, w_rkvz, db, F32)
    wl_s, al_s = lora_up(hw_s, ha_s, w2, a_w0[0], a2, a_a0[0])
    yg_s, s_state = wkv_sample(rkvz_s, wl_s, al_s, a_k_k[0], a_k_a[0], a_r_k[0].reshape(-1), a_gn_g[0],
                               a_gn_b[0], state_wkv[0])
    hs, hn_kv_s, hn_b_s = matmul_residual_norm(yg_s.reshape(db, e), w_out, hs, gains_b, db, BF16)
    tabs_s = rope_tables(jnp.full((db,), PAST_LEN, jnp.int32))
    k_s, v_s = matmul_rope(hn_kv_s, w_kv_bf, tabs_s, db, kvw, (F32, F32))
    q_s, = matmul_rope(hn_b_s, w_qz, tabs_s, db, e, (F32,), scale=Q_SCALE, n=e)
    z_s = matmul_groups(hn_b_s[None], w_qz, db, F32, n=e, col=1)[0]
    nq = e // HEAD_DIM
    att_s, s_cache_k, s_cache_v = attn_sample(
        sinks, q_s.reshape(db, nq, HEAD_DIM), cache_k.reshape(db, win, kvw), cache_v.reshape(db, win, kvw),
        k_s.reshape(db, 1, kvw), v_s.reshape(db, 1, kvw), z_s.reshape(db, nq, HEAD_DIM))
    y_s, = matmul_residual_norm(att_s.reshape(db, e), w_o, hs, final_norm[None], db, F32, emit_h=False)
    y_sample = y_s.reshape(db, 1, d)

    return (y_prompt, y_sample, p_state[None], p_state_shift,
            p_cache_k, p_cache_v,
            s_state[None], xn_s[None],
            s_cache_k.reshape(cache_k.shape), s_cache_v.reshape(cache_v.shape))
```

```python
import functools
import math

import jax
import jax.numpy as jnp
from jax import lax
from jax.experimental import pallas as pl
from jax.experimental.pallas import tpu as pltpu

F32 = jnp.float32
BF16 = jnp.bfloat16

HEAD_DIM = 64
N_KV_HEADS = 8
WINDOW = 128
BLOCK = 128
ROPE_DIM = HEAD_DIM // 4
ROPE_THETA = 500000.0
N_META = 16
PAST_LEN = 16384
RMS_EPS = 1e-6
GN_EPS = 64e-5
NORM_FLOOR_SQ = 1e-24
LEAD = (-N_META) % BLOCK
CHUNK = 64
WKV_ROWS = 128
CAST_CHUNKS = 32
SEQS_PER_STEP = 2
RESIDUAL_SEQS_PER_STEP = 4
HEADS_PER_STREAM = 32
HEADS_PER_STEP = 32
LORA_PAD = 128
MXU_TILE = 256
ROPE_SLAB = 512
VMEM_LIMIT = 48 * 1024 * 1024
RESIDUAL_VMEM_LIMIT = 56 * 1024 * 1024
LOG2E = 1.0 / math.log(2.0)
DECAY_SCALE = math.exp(-0.5)
SOFTMAX_BATCH = 4
Q_SCALE = HEAD_DIM ** -0.5 * LOG2E


def _cparams(sem, vmem_limit=VMEM_LIMIT):
    return pltpu.CompilerParams(dimension_semantics=sem, vmem_limit_bytes=vmem_limit)


def _sigmoid(x):
    return 1.0 / (1.0 + jnp.exp2(x * (-LOG2E)))


def _silu(x):
    return x * _sigmoid(x)


def _dot(a, b):
    return jnp.dot(a, b, preferred_element_type=F32)


def _dot_nt(a, b):
    return lax.dot_general(a, b, (((1,), (1,)), ((), ())), preferred_element_type=F32)


def _dot_tn(a, b):
    return lax.dot_general(a, b, (((0,), (0,)), ((), ())), preferred_element_type=F32)


def _split_hi_lo(x):
    hi = x.astype(BF16)
    lo = (x - hi.astype(F32)).astype(BF16)
    return hi, lo


def _cast_job(weights, n_steps, step_of):
    n_chunks = min(CAST_CHUNKS, n_steps)
    every = n_steps // n_chunks
    assert all(w.shape[0] % (n_chunks * 16) == 0 for w in weights)
    chunk = lambda *idx: (jnp.minimum(step_of(*idx) // every, n_chunks - 1), 0)
    return [pl.BlockSpec((w.shape[0] // n_chunks, w.shape[1]), chunk) for w in weights], every


def _cast_chunks(step, every, srcs, dsts):
    @pl.when(step % every == 0)
    def _():
        for src, dst in zip(srcs, dsts):
            dst[...] = src[...].astype(dst.dtype)


def _mixes(xn, prev, mu_ref, w1_ref, a1_ref, xm_ref, hw_ref, ha_ref):
    xx = prev - xn
    n_proj = xm_ref.shape[0]
    for p in range(n_proj):
        xm_ref[p] = (xn + xx * mu_ref[p:p + 1, :]).astype(xm_ref.dtype)
    xw = (xn + xx * mu_ref[n_proj:n_proj + 1, :]).astype(BF16)
    xa = (xn + xx * mu_ref[n_proj + 1:n_proj + 2, :]).astype(BF16)
    hw_ref[...] = jnp.tanh(_dot(xw, w1_ref[...])).astype(hw_ref.dtype)
    ha_ref[...] = _dot(xa, a1_ref[...]).astype(ha_ref.dtype)


def _norm_shift_kernel(x_ref, head_ref, g_ref, mu_ref, w1_ref, a1_ref, *rest, n_cast, cast_every):
    cast_in = rest[:n_cast]
    xm_ref, hw_ref, ha_ref, last_ref = rest[n_cast:n_cast + 4]
    cast_out = rest[n_cast + 4:2 * n_cast + 4]
    carry_ref = rest[-1]

    @pl.when(pl.program_id(1) == 0)
    def _():
        carry_ref[...] = jnp.zeros_like(carry_ref)

    _cast_chunks(pl.program_id(0) * pl.num_programs(1) + pl.program_id(1), cast_every, cast_in, cast_out)

    is_head = pl.program_id(1) == 0
    for sq in range(x_ref.shape[0]):
        x = jnp.where(is_head, head_ref[...], x_ref[sq])
        tm = x.shape[0]
        xn = x * lax.rsqrt(jnp.mean(x * x, axis=-1, keepdims=True) + RMS_EPS) * g_ref[...]
        rolled = pltpu.roll(xn, 1, axis=0)
        row = lax.broadcasted_iota(jnp.int32, xn.shape, 0)
        prev = jnp.where(row == 0, carry_ref[sq, 0:1, :], rolled)
        _mixes(xn, prev, mu_ref, w1_ref, a1_ref, xm_ref.at[:, sq], hw_ref.at[sq], ha_ref.at[sq])
        carry_ref[sq, 0:1, :] = xn[tm - 1:tm, :]
        last_ref[sq] = xn[tm - 1:tm, :]


def norm_shift_prompt(x, head, g, mu, w1, a1, cast_weights=()):
    b, seq, d = x.shape
    tm = BLOCK
    p = tm + seq
    n_mix = mu.shape[0]
    n_proj = n_mix - 2
    lr = w1.shape[1]
    nt = p // tm
    sp = SEQS_PER_STEP
    const = lambda shape: pl.BlockSpec(shape, lambda i, t: (0,) * len(shape))
    hid = pl.BlockSpec((sp, tm, lr), lambda i, t: (i, t, 0))
    cast_specs, every = _cast_job(cast_weights, (b // sp) * nt, lambda i, t: i * nt + t)
    xm, hw_act, ha_act, last, *casts = pl.pallas_call(
        functools.partial(_norm_shift_kernel, n_cast=len(cast_weights), cast_every=every),
        grid=(b // sp, nt),
        in_specs=[pl.BlockSpec((sp, tm, d), lambda i, t: (i, jnp.maximum(t - 1, 0), 0)),
                  const((tm, d)), const((1, d)), const((n_mix, d)), const((d, lr)), const((d, lr))] + cast_specs,
        out_specs=[pl.BlockSpec((n_proj, sp, tm, d), lambda i, t: (0, i, t, 0)), hid, hid,
                   pl.BlockSpec((sp, 1, d), lambda i, t: (i, 0, 0))] + cast_specs,
        out_shape=[jax.ShapeDtypeStruct((n_proj, b, p, d), BF16),
                   jax.ShapeDtypeStruct((b, p, lr), BF16), jax.ShapeDtypeStruct((b, p, lr), BF16),
                   jax.ShapeDtypeStruct((b, 1, d), F32)] + [jax.ShapeDtypeStruct(w.shape, BF16) for w in cast_weights],
        scratch_shapes=[pltpu.VMEM((sp, 8, d), F32)],
        compiler_params=_cparams(("arbitrary", "arbitrary")),
        name="norm_shift_prompt",
    )(x, head, g.reshape(1, d), mu, w1, a1, *cast_weights)
    return (xm.reshape(n_proj, b * p, d), hw_act.reshape(b * p, lr), ha_act.reshape(b * p, lr), last, *casts)


def _norm_shift_sample_kernel(x_ref, prev_ref, g_ref, mu_ref, w1_ref, a1_ref, xm_ref, hw_ref, ha_ref, xn_ref):
    x = x_ref[...]
    xn = x * lax.rsqrt(jnp.mean(x * x, axis=-1, keepdims=True) + RMS_EPS) * g_ref[...]
    xn_ref[...] = xn
    _mixes(xn, prev_ref[...], mu_ref, w1_ref, a1_ref, xm_ref, hw_ref, ha_ref)


def norm_shift_sample(x, prev, g, mu, w1, a1):
    m, d = x.shape
    lr = w1.shape[1]
    return pl.pallas_call(
        _norm_shift_sample_kernel,
        out_shape=[jax.ShapeDtypeStruct((mu.shape[0] - 2, m, d), BF16),
                   jax.ShapeDtypeStruct((m, lr), BF16), jax.ShapeDtypeStruct((m, lr), BF16),
                   jax.ShapeDtypeStruct((m, d), F32)],
        name="norm_shift_sample",
    )(x, prev, g.reshape(1, d), mu, w1, a1)


def _rope(y, cos, sin_a, sin_b):
    half = ROPE_DIM // 2
    step = ROPE_SLAB
    rep = step // cos.shape[1]
    tile = lambda t: jnp.concatenate([t] * rep, axis=1)
    cos_t, sa_t, sb_t = tile(cos), tile(sin_a), tile(sin_b)
    outs = []
    for j in range(y.shape[1] // step):
        ys = y[:, j * step:(j + 1) * step]
        outs.append(ys * cos_t + pltpu.roll(ys, step - half, axis=1) * sa_t + pltpu.roll(ys, half, axis=1) * sb_t)
    return jnp.concatenate(outs, axis=1) if len(outs) > 1 else outs[0]


def _mm_group_kernel(x_ref, w_ref, o_ref):
    o_ref[0] = _dot(x_ref[0], w_ref[0]).astype(o_ref.dtype)


def matmul_groups(x, w, tm, out_dtype, n=None, col=0, tn=None):
    g = w.shape[0]
    n = w.shape[2] if n is None else n
    tn = n if tn is None else tn
    nt = n // tn
    _, m, kdim = x.shape
    return pl.pallas_call(
        _mm_group_kernel,
        grid=(g, nt, m // tm),
        in_specs=[pl.BlockSpec((1, tm, kdim), lambda q, j, i: (q, i, 0)),
                  pl.BlockSpec((1, kdim, tn), lambda q, j, i: (q, 0, col * nt + j))],
        out_specs=pl.BlockSpec((1, tm, tn), lambda q, j, i: (q, i, j)),
        out_shape=jax.ShapeDtypeStruct((g, m, n), out_dtype),
        compiler_params=_cparams(("arbitrary", "arbitrary", "arbitrary")),
        name="matmul_groups",
    )(x, w)


def _mm_rope_kernel(x_ref, w_ref, cos_ref, sa_ref, sb_ref, *o_refs, n_rope, scale):
    y = _dot(x_ref[...], w_ref[...])
    rot = _rope(y[:, :n_rope], cos_ref[...], sa_ref[...], sb_ref[...])
    if scale != 1.0:
        rot = rot * scale
    o_refs[0][...] = rot.astype(o_refs[0].dtype)
    if len(o_refs) > 1:
        o_refs[1][...] = y[:, n_rope:].astype(o_refs[1].dtype)


def matmul_rope(x, w, tables, tm, n_rope, out_dtypes, scale=1.0, n=None):
    m, kdim = x.shape
    n = w.shape[-1] if n is None else n
    w_block = (kdim, n) if w.ndim == 2 else (None, kdim, n)
    lanes = tables[0].shape[1]
    widths = [n_rope] + ([n - n_rope] if n > n_rope else [])
    tab_blocks = tables[0].shape[0] // tm
    tab = pl.BlockSpec((tm, lanes), lambda i: (i % tab_blocks, 0))
    outs = pl.pallas_call(
        functools.partial(_mm_rope_kernel, n_rope=n_rope, scale=scale),
        grid=(m // tm,),
        in_specs=[pl.BlockSpec((tm, kdim), lambda i: (i, 0)),
                  pl.BlockSpec(w_block, lambda i: (0,) * w.ndim, pipeline_mode=pl.Buffered(1)),
                  tab, tab, tab],
        out_specs=[pl.BlockSpec((tm, wd), lambda i: (i, 0)) for wd in widths],
        out_shape=[jax.ShapeDtypeStruct((m, wd), dt) for wd, dt in zip(widths, out_dtypes)],
        compiler_params=_cparams(("arbitrary",)),
        name="matmul_rope",
    )(x, w, *tables)
    return outs


def _mm_res_norm_kernel(x_ref, w_ref, res_ref, g_ref, *out_refs, emit_h):
    h = res_ref[...] + _dot(x_ref[...], w_ref[...])
    hn_refs = out_refs
    if emit_h:
        out_refs[0][...] = h
        hn_refs = out_refs[1:]
    inv = lax.rsqrt(jnp.mean(h * h, axis=-1, keepdims=True) + RMS_EPS)
    for j, hn_ref in enumerate(hn_refs):
        hn_ref[...] = (h * inv * g_ref[j:j + 1, :]).astype(hn_ref.dtype)


def matmul_residual_norm(x, w, res, gains, tm, norm_dtype, emit_h=True):
    m, kdim = x.shape
    n = w.shape[1]
    ng = gains.shape[0]
    row = lambda width: pl.BlockSpec((tm, width), lambda i: (i, 0))
    return pl.pallas_call(
        functools.partial(_mm_res_norm_kernel, emit_h=emit_h),
        grid=(m // tm,),
        in_specs=[row(kdim),
                  pl.BlockSpec((kdim, n), lambda i: (0, 0), pipeline_mode=pl.Buffered(1)),
                  row(n),
                  pl.BlockSpec((ng, n), lambda i: (0, 0))],
        out_specs=[row(n)] * (int(emit_h) + ng),
        out_shape=[jax.ShapeDtypeStruct((m, n), F32)] * int(emit_h) + [jax.ShapeDtypeStruct((m, n), norm_dtype)] * ng,
        compiler_params=_cparams(("arbitrary",)),
        name="matmul_residual_norm",
    )(x, w, res, gains)


def _mm_res_norm_blocks_kernel(x_ref, w_ref, res_ref, *rest, emit_h, head, first_block):
    if head:
        head_ref, g_ref, *out_refs = rest
        res = jnp.where(pl.program_id(1) + first_block == 0, head_ref[...][None], res_ref[...])
    else:
        g_ref, *out_refs = rest
        res = res_ref[...]
    seqs, blk, n = res.shape
    h = res.reshape(seqs * blk, n) + _dot(x_ref[...].reshape(seqs * blk, x_ref.shape[2]), w_ref[...])
    hn_refs = out_refs
    if emit_h:
        out_refs[0][...] = h.reshape(seqs, blk, n)
        hn_refs = out_refs[1:]
    inv = lax.rsqrt(jnp.mean(h * h, axis=-1, keepdims=True) + RMS_EPS)
    for j, hn_ref in enumerate(hn_refs):
        hn_ref[...] = (h * inv * g_ref[j:j + 1, :]).astype(hn_ref.dtype).reshape(seqs, blk, n)


def matmul_residual_norm_blocks(x, w, res, gains, n_batch, norm_dtype, emit_h, head=None, first_block=0):
    kdim = x.shape[1]
    n = w.shape[1]
    ng = gains.shape[0]
    nb_out = x.shape[0] // (n_batch * BLOCK)
    nb = nb_out + first_block
    sp = RESIDUAL_SEQS_PER_STEP
    blocks = lambda width, shift: pl.BlockSpec((sp, BLOCK, width), lambda i, t: (i, t + shift, 0))
    const = lambda shape: pl.BlockSpec(shape, lambda i, t: (0,) * len(shape))
    if head is not None:
        res_specs = [pl.BlockSpec((sp, BLOCK, n), lambda i, t: (i, jnp.maximum(t + first_block - 1, 0), 0)),
                     const((BLOCK, n))]
        res_args = [res, head]
    else:
        res_specs, res_args = [blocks(n, first_block)], [res.reshape(n_batch, nb * BLOCK, n)]
    outs = pl.pallas_call(
        functools.partial(_mm_res_norm_blocks_kernel, emit_h=emit_h, head=head is not None, first_block=first_block),
        grid=(n_batch // sp, nb_out),
        in_specs=[blocks(kdim, 0), pl.BlockSpec((kdim, n), lambda i, t: (0, 0), pipeline_mode=pl.Buffered(1))]
        + res_specs + [const((ng, n))],
        out_specs=[blocks(n, first_block)] * int(emit_h) + [blocks(n, 0)] * ng,
        out_shape=[jax.ShapeDtypeStruct((n_batch, nb * BLOCK, n), F32)] * int(emit_h)
        + [jax.ShapeDtypeStruct((n_batch, nb_out * BLOCK, n), norm_dtype)] * ng,
        compiler_params=_cparams(("arbitrary", "arbitrary"), RESIDUAL_VMEM_LIMIT),
        name="matmul_residual_norm_blocks",
    )(x.reshape(n_batch, nb_out * BLOCK, kdim), w, *res_args, gains)
    return [o.reshape(-1, n) for o in outs]


def _lora_up_kernel(hw_ref, ha_ref, w2_ref, w0_ref, a2_ref, a0_ref, wl_ref, al_ref):
    wl_ref[...] = w0_ref[...] + _dot(hw_ref[...], w2_ref[...])
    al_ref[...] = a0_ref[...] + _dot(ha_ref[...], a2_ref[...])


def lora_up(hw, ha, w2, w0, a2, a0):
    m = hw.shape[0]
    e = w2.shape[1]
    return pl.pallas_call(
        _lora_up_kernel,
        out_shape=[jax.ShapeDtypeStruct((m, e), F32), jax.ShapeDtypeStruct((m, e), F32)],
        name="lora_up",
    )(hw, ha, w2, w0.reshape(1, e), a2, a0.reshape(1, e))


def _seg_sum(x, ones_bd):
    hi = x.astype(BF16)
    outs = []
    for c in range(x.shape[1] // MXU_TILE):
        sl = slice(c * MXU_TILE, (c + 1) * MXU_TILE)
        outs.append(_dot(hi[:, sl], ones_bd))
    return jnp.concatenate(outs, axis=1) if len(outs) > 1 else outs[0]


def _wkv_prompt_kernel(r_ref, k_ref, v_ref, z_ref, hw_ref, ha_ref, w2_ref, w0_ref, a2_ref, a0_ref,
                       kk_ref, ka_ref, rk_ref, gg_ref, gb_ref, *rest, n_cast, cast_every):
    cast_in = rest[:n_cast]
    yg_ref, sout_ref = rest[n_cast:n_cast + 2]
    cast_out = rest[n_cast + 2:2 * n_cast + 2]
    s_ref = rest[-1]
    t_idx = pl.program_id(2)
    c = CHUNK
    hd = HEAD_DIM

    @pl.when(t_idx == 0)
    def _():
        s_ref[...] = jnp.zeros_like(s_ref)

    step = (pl.program_id(0) * pl.num_programs(1) + pl.program_id(1)) * pl.num_programs(2) + t_idx
    _cast_chunks(step, cast_every, cast_in, cast_out)

    tb = r_ref.shape[1]
    nh = HEADS_PER_STREAM
    hw = nh * hd
    nc = tb // c

    li = lax.broadcasted_iota(jnp.int32, (MXU_TILE, MXU_TILE), 0) // hd
    lj = lax.broadcasted_iota(jnp.int32, (MXU_TILE, MXU_TILE), 1) // hd
    ones_bd = jnp.where(li == lj, 1.0, 0.0).astype(BF16)
    bi_ = lax.broadcasted_iota(jnp.int32, (tb, tb), 0)
    bj_ = lax.broadcasted_iota(jnp.int32, (tb, tb), 1)
    tri_incl = jnp.where((bj_ <= bi_) & (bj_ // c == bi_ // c), 1.0, 0.0).astype(BF16)
    ti = lax.broadcasted_iota(jnp.int32, (c, c), 0)
    tj = lax.broadcasted_iota(jnp.int32, (c, c), 1)
    ai = lax.broadcasted_iota(jnp.int32, (c, 2 * c), 0)
    aj = lax.broadcasted_iota(jnp.int32, (c, 2 * c), 1)
    upper = aj >= c
    aj_mod = jnp.where(upper, aj - c, aj)
    masks = dict(
        strict=tj < ti,
        eye=jnp.where(ti == tj, 1.0, 0.0).astype(F32),
        top_k=upper & (aj_mod < ai),
        bot=aj_mod <= ai,
        mean_bd=jnp.where(li == lj, 1.0 / hd, 0.0).astype(BF16))

    for st in range(r_ref.shape[2] // hw):
        _wkv_stream(st, hw, nc, r_ref, k_ref, v_ref, z_ref, hw_ref, ha_ref, w2_ref, w0_ref, a2_ref, a0_ref,
                    kk_ref, ka_ref, rk_ref, gg_ref, gb_ref, yg_ref, s_ref, ones_bd, tri_incl, masks)

    @pl.when(t_idx == pl.num_programs(2) - 1)
    def _():
        sout_ref[0] = s_ref[...]


def _wkv_stream(st, hw, nc, r_ref, k_ref, v_ref, z_ref, hw_ref, ha_ref, w2_ref, w0_ref, a2_ref, a0_ref,
                kk_ref, ka_ref, rk_ref, gg_ref, gb_ref, yg_ref, s_ref, ones_bd, tri_incl, masks):
    c = CHUNK
    hd = HEAD_DIM
    nh = hw // hd
    ls = slice(st * hw, (st + 1) * hw)
    h0 = st * nh
    r = r_ref[0, :, ls]
    k = k_ref[0, :, ls]
    v = v_ref[0, :, ls]
    wl = w0_ref[:, ls] + _dot(hw_ref[...], w2_ref[:, ls])
    al = a0_ref[:, ls] + _dot(ha_ref[...], a2_ref[:, ls])
    a = _sigmoid(al)
    lw = (-DECAY_SCALE * LOG2E) * _sigmoid(wl)
    kk = k * kk_ref[:, ls]
    n2 = _seg_sum(kk * kk, ones_bd)
    kk = kk * lax.rsqrt(jnp.maximum(n2, NORM_FLOOR_SQ))
    ka = ka_ref[:, ls]
    k2 = k * (a * ka + (1.0 - ka))
    bb = kk * a

    lw_hi, lw_lo = _split_hi_lo(lw)
    g = _dot(tri_incl, lw_hi) + _dot(tri_incl, lw_lo)
    mid = lambda ci: g[ci * c + c // 2 - 1:ci * c + c // 2, :]
    gm = jnp.concatenate([jnp.broadcast_to(mid(ci), (c, hw)) for ci in range(nc)], axis=0)
    t = g - gm
    e_a = jnp.exp2(t)
    e_prev = jnp.exp2(t - lw)
    e_inv = jnp.exp2(-t)
    e1 = [jnp.exp2(mid(ci)) for ci in range(nc)]
    e2 = [jnp.exp2(g[ci * c + c - 1:ci * c + c, :] - mid(ci)) for ci in range(nc)]

    kkd = (kk * e_prev).astype(BF16)
    rd = (r * e_a).astype(BF16)
    bi = (bb * e_inv).astype(BF16)
    ki = (k2 * e_inv).astype(BF16)
    v_bf = v.astype(BF16)
    zeros_cv = jnp.zeros((c, hd), BF16)

    pairs = [(ci, h) for ci in range(nc) for h in range(nh)]
    rows = lambda ci: slice(ci * c, (ci + 1) * c)
    cols = lambda h: slice(h * hd, (h + 1) * hd)
    xs = {(ci, h): jnp.concatenate([kkd[rows(ci), cols(h)], rd[rows(ci), cols(h)]], axis=0) for ci, h in pairs}
    r1s = {(ci, h): jnp.concatenate([bi[rows(ci), cols(h)], ki[rows(ci), cols(h)]], axis=0) for ci, h in pairs}
    vs = {(ci, h): v_bf[rows(ci), cols(h)] for ci, h in pairs}
    a_mats = {p: _dot_nt(xs[p], r1s[p]) for p in pairs}
    lk_vs = {p: _dot(jnp.where(masks["top_k"], a_mats[p][:c, :], 0.0).astype(BF16),
                     jnp.concatenate([zeros_cv, vs[p]], axis=0)) for p in pairs}
    lps = {p: jnp.where(masks["strict"], a_mats[p][:c, :c], 0.0) for p in pairs}
    ts = {p: masks["eye"] - lps[p] for p in pairs}
    for _ in range(int(math.log2(c)) - 1):
        lpb = {p: lps[p].astype(BF16) for p in pairs}
        lps = {p: _dot(lpb[p], lpb[p]) for p in pairs}
        ts = {p: _dot(ts[p].astype(BF16), (masks["eye"] + lps[p]).astype(BF16)) for p in pairs}
    a_bots = {p: jnp.where(masks["bot"], a_mats[p][c:, :], 0.0).astype(BF16) for p in pairs}
    t_bf = {p: ts[p].astype(BF16) for p in pairs}

    state = [s_ref[h0 + h] for h in range(nh)]
    y_rows = []
    for ci in range(nc):
        hs = range(nh)
        sms = [state[h] * e1[ci][:, cols(h)] for h in hs]
        p_mats = [_dot_nt(xs[ci, h], sms[h].astype(BF16)) for h in hs]
        us = [-_dot(t_bf[ci, h], (p_mats[h][:c, :] + lk_vs[ci, h]).astype(BF16)) for h in hs]
        uvs = [jnp.concatenate([us[h].astype(BF16), vs[ci, h]], axis=0) for h in hs]
        ys = [p_mats[h][c:, :] + _dot(a_bots[ci, h], uvs[h]) for h in hs]
        state = [(sms[h] + _dot_tn(uvs[h], r1s[ci, h])) * e2[ci][:, cols(h)] for h in hs]
        y_rows.append(jnp.concatenate(ys, axis=1))
    for h in range(nh):
        s_ref[h0 + h] = state[h]
    y = jnp.concatenate(y_rows, axis=0) if nc > 1 else y_rows[0]

    mean = _seg_sum(y, masks["mean_bd"])
    yc = y - mean
    var = _seg_sum(yc * yc, masks["mean_bd"])
    yn = yc * lax.rsqrt(var + GN_EPS) * gg_ref[:, ls] + gb_ref[:, ls]
    bonus = _seg_sum(r * k2 * rk_ref[:, ls], ones_bd)
    yg_ref[:, ls] = ((yn + bonus * v) * _silu(z_ref[0, :, ls])).astype(BF16)


def wkv_prompt(rkvz, hw_act, ha_act, w2, w0, a2, a0, k_k, k_a, r_k, gn_g, gn_b, n_batch, cast_weights=()):
    _, m, e = rkvz.shape
    p = m // n_batch
    tb = WKV_ROWS
    nt = p // tb
    hw = HEADS_PER_STEP * HEAD_DIM
    nh = e // HEAD_DIM
    row = lambda i, g, t: (i * nt + t, g)
    proj = lambda q: pl.BlockSpec((1, tb, hw), lambda i, g, t: (q, i * nt + t, g))
    par = pl.BlockSpec((1, hw), lambda i, g, t: (0, g))
    lr = w2.shape[0]
    hid = pl.BlockSpec((tb, lr), lambda i, g, t: (i * nt + t, 0))
    up = pl.BlockSpec((lr, hw), lambda i, g, t: (0, g))
    n_groups = e // hw
    cast_specs, every = _cast_job(cast_weights, n_batch * n_groups * nt, lambda i, g, t: (i * n_groups + g) * nt + t)
    return pl.pallas_call(
        functools.partial(_wkv_prompt_kernel, n_cast=len(cast_weights), cast_every=every),
        grid=(n_batch, n_groups, nt),
        in_specs=[proj(0), proj(1), proj(2), proj(3), hid, hid, up, par, up, par,
                  par, par, par, par, par] + cast_specs,
        out_specs=[pl.BlockSpec((tb, hw), row),
                   pl.BlockSpec((1, HEADS_PER_STEP, HEAD_DIM, HEAD_DIM), lambda i, g, t: (i, g, 0, 0))] + cast_specs,
        out_shape=[jax.ShapeDtypeStruct((m, e), BF16),
                   jax.ShapeDtypeStruct((n_batch, nh, HEAD_DIM, HEAD_DIM), F32)]
        + [jax.ShapeDtypeStruct(w.shape, BF16) for w in cast_weights],
        scratch_shapes=[pltpu.VMEM((HEADS_PER_STEP, HEAD_DIM, HEAD_DIM), F32)],
        compiler_params=_cparams(("arbitrary", "arbitrary", "arbitrary")),
        name="wkv_prompt",
    )(rkvz, rkvz, rkvz, rkvz, hw_act, ha_act, w2, w0.reshape(1, e), a2, a0.reshape(1, e),
      k_k.reshape(1, e), k_a.reshape(1, e), r_k.reshape(1, e), gn_g.reshape(1, e), gn_b.reshape(1, e),
      *cast_weights)


def _wkv_sample_kernel(r_ref, k_ref, v_ref, z_ref, wl_ref, al_ref, kk_ref, ka_ref, rk_ref, gg_ref, gb_ref,
                       s_ref, yg_ref, sout_ref, y_scr):
    hd = HEAD_DIM
    r = r_ref[0]
    k = k_ref[0]
    v = v_ref[0]
    a = _sigmoid(al_ref[0])
    d = jnp.exp2((-DECAY_SCALE * LOG2E) * _sigmoid(wl_ref[0]))
    kk = k * kk_ref[...]
    kk = kk / jnp.maximum(jnp.sqrt(jnp.sum(kk * kk, axis=-1, keepdims=True)), 1e-12)
    k2 = k * (1.0 + (a - 1.0) * ka_ref[...])
    bb = kk * a
    nh = r.shape[0]
    ii = lax.broadcasted_iota(jnp.int32, (hd, hd), 0)
    jj = lax.broadcasted_iota(jnp.int32, (hd, hd), 1)
    eye = ii == jj

    row = lambda x, h: x[h:h + 1, :]
    group = 8
    for h0 in range(0, nh, group):
        hs = range(h0, h0 + group)
        s = {h: s_ref[0, h] for h in hs}
        sa = {h: jnp.sum(s[h] * row(kk, h), axis=-1, keepdims=True) for h in hs}
        v_col = {h: jnp.sum(jnp.where(eye, row(v, h), 0.0), axis=-1, keepdims=True) for h in hs}
        s_new = {h: s[h] * row(d, h) - sa[h] * row(bb, h) + v_col[h] * row(k2, h) for h in hs}
        y_col = {h: jnp.sum(s_new[h] * row(r, h), axis=-1, keepdims=True) for h in hs}
        for h in hs:
            sout_ref[0, h] = s_new[h]
            y_scr[h:h + 1, :] = jnp.sum(jnp.where(eye, y_col[h], 0.0), axis=0, keepdims=True)

    y = y_scr[...]
    mean = jnp.mean(y, axis=-1, keepdims=True)
    yc = y - mean
    var = jnp.mean(yc * yc, axis=-1, keepdims=True)
    yn = yc * lax.rsqrt(var + GN_EPS) * gg_ref[...] + gb_ref[...]
    bonus = jnp.sum(r * k2 * rk_ref[...], axis=-1, keepdims=True)
    yg_ref[0] = ((yn + bonus * v) * _silu(z_ref[0])).astype(BF16)


def wkv_sample(rkvz, wl, al, k_k, k_a, r_k, gn_g, gn_b, state):
    _, m, e = rkvz.shape
    nh = e // HEAD_DIM
    hd = HEAD_DIM
    rkvz4 = rkvz.reshape(4, m, nh, hd)
    proj = lambda q: pl.BlockSpec((None, 1, nh, hd), lambda i: (q, i, 0, 0))
    tok = pl.BlockSpec((1, nh, hd), lambda i: (i, 0, 0))
    par = pl.BlockSpec((nh, hd), lambda i: (0, 0))
    st = pl.BlockSpec((1, nh, hd, hd), lambda i: (i, 0, 0, 0))
    as_heads = lambda x: x.reshape(nh, hd)
    return pl.pallas_call(
        _wkv_sample_kernel,
        grid=(m,),
        in_specs=[proj(0), proj(1), proj(2), proj(3), tok, tok, par, par, par, par, par, st],
        out_specs=[tok, st],
        out_shape=[jax.ShapeDtypeStruct((m, nh, hd), BF16), jax.ShapeDtypeStruct(state.shape, F32)],
        scratch_shapes=[pltpu.VMEM((nh, hd), F32)],
        compiler_params=_cparams(("arbitrary",)),
        name="wkv_sample",
    )(rkvz4, rkvz4, rkvz4, rkvz4, wl.reshape(m, nh, hd), al.reshape(m, nh, hd),
      as_heads(k_k), as_heads(k_a), as_heads(r_k), as_heads(gn_g), as_heads(gn_b), state)


def rope_tables(pos):
    half = ROPE_DIM // 2
    inv_freq = ROPE_THETA ** (-jnp.arange(half, dtype=F32) * 2.0 / ROPE_DIM)
    ang = pos.astype(F32)[:, None] * inv_freq[None, :]
    cos = jnp.cos(ang)
    sin = jnp.sin(ang)
    rows = pos.shape[0]
    ones = jnp.ones((rows, HEAD_DIM - ROPE_DIM), F32)
    zeros_h = jnp.zeros((rows, half), F32)
    zeros_r = jnp.zeros((rows, HEAD_DIM - ROPE_DIM), F32)
    cos_h = jnp.concatenate([cos, cos, ones], axis=1)
    sa_h = jnp.concatenate([-sin, zeros_h, zeros_r], axis=1)
    sb_h = jnp.concatenate([zeros_h, sin, zeros_r], axis=1)
    two = lambda t: jnp.concatenate([t, t], axis=1)
    return two(cos_h), two(sa_h), two(sb_h)


def _attn_prompt_kernel(sink_ref, q_ref, kc_ref, kp_ref, vc_ref, vp_ref, z_ref, o_ref, *, first_block):
    n = pl.program_id(1) + first_block
    hd = HEAD_DIM
    seqs, blk, _ = q_ref.shape
    n_kv = kc_ref.shape[2] // hd
    grp = q_ref.shape[2] // (n_kv * hd)
    qi = lax.broadcasted_iota(jnp.int32, (blk, 2 * blk), 0)
    kj = lax.broadcasted_iota(jnp.int32, (blk, 2 * blk), 1) - blk
    kpos = n * blk + kj
    diff = qi - kj
    valid = (kpos >= LEAD) & (diff >= 0) & (diff <= WINDOW)
    for sq in range(seqs):
        k_all = jnp.concatenate([kp_ref[sq], kc_ref[sq]], axis=0).astype(BF16)
        v_all = jnp.concatenate([vp_ref[sq], vc_ref[sq]], axis=0).astype(BF16)

        def scores(h):
            k_h = k_all[:, h * hd:(h + 1) * hd]
            return [_dot_nt(q_ref[sq, :, (h * grp + gi) * hd:(h * grp + gi + 1) * hd], k_h) for gi in range(grp)]

        outs = []
        s_next = scores(0)
        for h in range(n_kv):
            s_cur = s_next
            if h + 1 < n_kv:
                s_next = scores(h + 1)
            v_h = v_all[:, h * hd:(h + 1) * hd]
            for g0 in range(0, grp, SOFTMAX_BATCH):
                gs = range(g0, min(g0 + SOFTMAX_BATCH, grp))
                sks = {gi: sink_ref[h * grp + gi] * LOG2E for gi in gs}
                ss = {gi: jnp.where(valid, s_cur[gi], -jnp.inf) for gi in gs}
                ms = {gi: jnp.maximum(jnp.max(ss[gi], axis=-1, keepdims=True), sks[gi]) for gi in gs}
                ps = {gi: jnp.exp2(ss[gi] - ms[gi]) for gi in gs}
                dens = {gi: jnp.sum(ps[gi], axis=-1, keepdims=True) + jnp.exp2(sks[gi] - ms[gi]) for gi in gs}
                outs += [_dot(ps[gi].astype(BF16), v_h) / dens[gi] for gi in gs]
        att = jnp.concatenate(outs, axis=1)
        o_ref[sq] = (att * _silu(z_ref[sq])).astype(o_ref.dtype)


def attn_prompt(sinks, q, k, v, z, n_batch, first_block):
    m, e = q.shape
    nb = m // (n_batch * BLOCK)
    nb_out = nb - first_block
    kw = k.shape[1]
    sp = SEQS_PER_STEP
    cur = lambda i, n: (i, n + first_block, 0)
    prv = lambda i, n: (i, jnp.maximum(n + first_block - 1, 0), 0)
    seq_rows = lambda t: t.reshape(n_batch, nb * BLOCK, t.shape[1])
    q, k, v, z = seq_rows(q), seq_rows(k), seq_rows(v), seq_rows(z)
    return pl.pallas_call(
        functools.partial(_attn_prompt_kernel, first_block=first_block),
        grid=(n_batch // sp, nb_out),
        in_specs=[pl.BlockSpec(memory_space=pltpu.SMEM),
                  pl.BlockSpec((sp, BLOCK, e), cur),
                  pl.BlockSpec((sp, BLOCK, kw), cur), pl.BlockSpec((sp, BLOCK, kw), prv),
                  pl.BlockSpec((sp, BLOCK, kw), cur), pl.BlockSpec((sp, BLOCK, kw), prv),
                  pl.BlockSpec((sp, BLOCK, e), cur)],
        out_specs=pl.BlockSpec((sp, BLOCK, e), lambda i, n: (i, n, 0)),
        out_shape=jax.ShapeDtypeStruct((n_batch, nb_out * BLOCK, e), BF16),
        compiler_params=_cparams(("arbitrary", "arbitrary")),
        name="attn_prompt",
    )(sinks, q, k, k, v, v, z).reshape(n_batch * nb_out * BLOCK, e)


def _attn_sample_kernel(sink_ref, q_ref, kc_ref, vc_ref, kn_ref, vn_ref, z_ref, o_ref, ko_ref, vo_ref):
    hd = HEAD_DIM
    win = kc_ref.shape[1]
    n_kv = kc_ref.shape[2] // hd
    nq = q_ref.shape[1]
    grp = nq // n_kv
    pad = 8
    kc = kc_ref[0]
    vc = vc_ref[0]
    kn = kn_ref[0]
    vn = vn_ref[0]
    first = lax.broadcasted_iota(jnp.int32, (pad, kc.shape[1]), 0) == 0
    k_all = jnp.concatenate([kc, jnp.where(first, kn, 0.0)], axis=0).astype(BF16)
    v_all = jnp.concatenate([vc, jnp.where(first, vn, 0.0)], axis=0).astype(BF16)
    col = lax.broadcasted_iota(jnp.int32, (grp, win + pad), 1)
    valid = (col <= win) & (win - col <= WINDOW)
    q = q_ref[0].astype(BF16)
    row_i = lax.broadcasted_iota(jnp.int32, (grp, 1), 0)
    hs = range(n_kv)
    sks = []
    for h in hs:
        sk = jnp.zeros((grp, 1), F32)
        for gi in range(grp):
            sk = jnp.where(row_i == gi, sink_ref[h * grp + gi] * LOG2E, sk)
        sks.append(sk)
    ss = [jnp.where(valid, _dot_nt(q[h * grp:(h + 1) * grp, :], k_all[:, h * hd:(h + 1) * hd]), -jnp.inf)
          for h in hs]
    ms = [jnp.maximum(jnp.max(ss[h], axis=-1, keepdims=True), sks[h]) for h in hs]
    ps = [jnp.exp2(ss[h] - ms[h]) for h in hs]
    dens = [jnp.sum(ps[h], axis=-1, keepdims=True) + jnp.exp2(sks[h] - ms[h]) for h in hs]
    outs = [_dot(ps[h].astype(BF16), v_all[:, h * hd:(h + 1) * hd]) / dens[h] for h in hs]
    att = jnp.concatenate(outs, axis=0)
    o_ref[0] = (att * _silu(z_ref[0])).astype(o_ref.dtype)
    last = lax.broadcasted_iota(jnp.int32, kc.shape, 0) == win - 1
    ko_ref[0] = jnp.where(last, kn, pltpu.roll(kc, win - 1, axis=0))
    vo_ref[0] = jnp.where(last, vn, pltpu.roll(vc, win - 1, axis=0))


def attn_sample(sinks, q, cache_k, cache_v, k_new, v_new, z):
    m, win, kw = cache_k.shape
    nq = q.shape[1]
    hd = HEAD_DIM
    tok = pl.BlockSpec((1, nq, hd), lambda i: (i, 0, 0))
    cache = pl.BlockSpec((1, win, kw), lambda i: (i, 0, 0))
    new = pl.BlockSpec((1, 1, kw), lambda i: (i, 0, 0))
    return pl.pallas_call(
        _attn_sample_kernel,
        grid=(m,),
        in_specs=[pl.BlockSpec(memory_space=pltpu.SMEM), tok, cache, cache, new, new, tok],
        out_specs=[tok, cache, cache],
        out_shape=[jax.ShapeDtypeStruct((m, nq, hd), BF16),
                   jax.ShapeDtypeStruct(cache_k.shape, F32), jax.ShapeDtypeStruct(cache_v.shape, F32)],
        compiler_params=_cparams(("arbitrary",)),
        name="attn_sample",
    )(sinks, q, cache_k, cache_v, k_new, v_new, z)


def _pad_lora(w_down, w_up):
    r = w_down.shape[1]
    return (jnp.pad(w_down, ((0, 0), (0, LORA_PAD - r))).astype(BF16),
            jnp.pad(w_up, ((0, LORA_PAD - r), (0, 0))).astype(BF16))


def kernel(x_prompt, x_sample, state_wkv, state_shift, cache_k, cache_v, meta_tokens, a_norm, a_mu, a_w_rkvz,
           a_w0, a_w1, a_w2, a_a0, a_a1, a_a2, a_k_k, a_k_a, a_r_k, a_gn_g, a_gn_b, a_w_out, kv_norm, w_kv,
           b_norm, b_w_qz, b_sinks, b_w_o, final_norm):
    nb, seq, d = x_prompt.shape
    db, dseq, _ = x_sample.shape
    assert dseq == 1 and a_norm.shape[0] == 1 and b_norm.shape[0] == 1
    e = a_w_rkvz.shape[3]
    win = cache_k.shape[1]
    p_len = LEAD + N_META + seq
    assert p_len % BLOCK == 0 and (LEAD + N_META) == BLOCK
    kvw = N_KV_HEADS * HEAD_DIM

    w1, w2 = _pad_lora(a_w1[0], a_w2[0])
    a1, a2 = _pad_lora(a_a1[0], a_a2[0])
    mu = a_mu[0]
    sinks = b_sinks[0]
    gains_b = jnp.stack([kv_norm, b_norm[0]])

    tm = p_len // 8

    head = jnp.concatenate([jnp.zeros((LEAD, d), F32), meta_tokens], axis=0)
    w4 = a_w_rkvz[0]
    xm, hw_p, ha_p, x_last, w_rkvz = norm_shift_prompt(x_prompt, head, a_norm[0], mu, w1, a1,
                                                      cast_weights=(w4.reshape(-1, w4.shape[2]),))
    w_rkvz = w_rkvz.reshape(w4.shape)
    p_state_shift = x_last.reshape(1, nb, d)
    rkvz = matmul_groups(xm, w_rkvz, 4 * tm, F32, tn=e // 2)
    yg, p_state, w_out, w_kv_bf, w_qz, w_o = wkv_prompt(
        rkvz, hw_p, ha_p, w2, a_w0[0], a2, a_a0[0], a_k_k[0], a_k_a[0], a_r_k[0].reshape(-1), a_gn_g[0],
        a_gn_b[0], nb, cast_weights=(a_w_out[0], w_kv, b_w_qz[0], b_w_o[0]))
    w_qz = w_qz[None]
    hp, hn_kv, hn_b = matmul_residual_norm_blocks(yg, w_out, x_prompt, gains_b, nb, BF16, True, head=head)

    pos_p = jnp.maximum(jnp.arange(p_len, dtype=jnp.int32) - LEAD, 0)
    tabs_p = rope_tables(pos_p)
    k_p, v_p = matmul_rope(hn_kv, w_kv_bf, tabs_p, 4 * tm, kvw, (F32, F32))
    q_p, = matmul_rope(hn_b, w_qz, tabs_p, 2 * tm, e, (BF16,), scale=Q_SCALE, n=e)
    z_p = matmul_groups(hn_b[None], w_qz, 4 * tm, F32, n=e, col=1, tn=e // 2)[0]
    skip = (LEAD + N_META) // BLOCK
    att = attn_prompt(sinks, q_p, k_p, v_p, z_p, nb, skip)
    y_prompt, = matmul_residual_norm_blocks(att, w_o, hp, final_norm[None], nb, F32, False, first_block=skip)
    y_prompt = y_prompt.reshape(nb, seq, d)
    tail = lambda t: t.reshape(nb, p_len, kvw)[:, -win:].reshape(nb, win, N_KV_HEADS, HEAD_DIM)
    p_cache_k = tail(k_p)
    p_cache_v = tail(v_p)

    hs = x_sample.reshape(db, d)
    xm_s, hw_s, ha_s, xn_s = norm_shift_sample(hs, state_shift[0], a_norm[0], mu, w1, a1)
    rkvz_s = matmul_groups(xm_s, w_rkvz, db, F32)
    wl_s, al_s = lora_up(hw_s, ha_s, w2, a_w0[0], a2, a_a0[0])
    yg_s, s_state = wkv_sample(rkvz_s, wl_s, al_s, a_k_k[0], a_k_a[0], a_r_k[0].reshape(-1), a_gn_g[0],
                               a_gn_b[0], state_wkv[0])
    hs, hn_kv_s, hn_b_s = matmul_residual_norm(yg_s.reshape(db, e), w_out, hs, gains_b, db, BF16)
    tabs_s = rope_tables(jnp.full((db,), PAST_LEN, jnp.int32))
    k_s, v_s = matmul_rope(hn_kv_s, w_kv_bf, tabs_s, db, kvw, (F32, F32))
    q_s, = matmul_rope(hn_b_s, w_qz, tabs_s, db, e, (F32,), scale=Q_SCALE, n=e)
    z_s = matmul_groups(hn_b_s[None], w_qz, db, F32, n=e, col=1)[0]
    nq = e // HEAD_DIM
    att_s, s_cache_k, s_cache_v = attn_sample(
        sinks, q_s.reshape(db, nq, HEAD_DIM), cache_k.reshape(db, win, kvw), cache_v.reshape(db, win, kvw),
        k_s.reshape(db, 1, kvw), v_s.reshape(db, 1, kvw), z_s.reshape(db, nq, HEAD_DIM))
    y_s, = matmul_residual_norm(att_s.reshape(db, e), w_o, hs, final_norm[None], db, F32, emit_h=False)
    y_sample = y_s.reshape(db, 1, d)

    return (y_prompt, y_sample, p_state[None], p_state_shift,
            p_cache_k, p_cache_v,
            s_state[None], xn_s[None],
            s_cache_k.reshape(cache_k.shape), s_cache_v.reshape(cache_v.shape))
```

```python
import functools
import math

import jax
import jax.numpy as jnp
from jax import lax
from jax.experimental import pallas as pl
from jax.experimental.pallas import tpu as pltpu

F32 = jnp.float32
BF16 = jnp.bfloat16

HEAD_DIM = 64
N_KV_HEADS = 8
WINDOW = 128
BLOCK = 128
ROPE_DIM = HEAD_DIM // 4
ROPE_THETA = 500000.0
N_META = 16
PAST_LEN = 16384
RMS_EPS = 1e-6
GN_EPS = 64e-5
NORM_FLOOR_SQ = 1e-24
LEAD = (-N_META) % BLOCK
CHUNK = 64
WKV_ROWS = 128
CAST_CHUNKS = 32
SEQS_PER_STEP = 2
RESIDUAL_SEQS_PER_STEP = 4
HEADS_PER_STREAM = 32
HEADS_PER_STEP = 64
LORA_PAD = 128
MXU_TILE = 256
ROPE_SLAB = 512
VMEM_LIMIT = 48 * 1024 * 1024
LARGE_VMEM_LIMIT = 56 * 1024 * 1024
LOG2E = 1.0 / math.log(2.0)
DECAY_SCALE = math.exp(-0.5)
SOFTMAX_BATCH = 4
Q_SCALE = HEAD_DIM ** -0.5 * LOG2E


def _cparams(sem, vmem_limit=VMEM_LIMIT):
    return pltpu.CompilerParams(dimension_semantics=sem, vmem_limit_bytes=vmem_limit)


def _sigmoid(x):
    return 1.0 / (1.0 + jnp.exp2(x * (-LOG2E)))


def _silu(x):
    return x * _sigmoid(x)


def _dot(a, b):
    return jnp.dot(a, b, preferred_element_type=F32)


def _dot_nt(a, b):
    return lax.dot_general(a, b, (((1,), (1,)), ((), ())), preferred_element_type=F32)


def _dot_tn(a, b):
    return lax.dot_general(a, b, (((0,), (0,)), ((), ())), preferred_element_type=F32)


def _split_hi_lo(x):
    hi = x.astype(BF16)
    lo = (x - hi.astype(F32)).astype(BF16)
    return hi, lo


def _cast_job(weights, n_steps, step_of):
    n_chunks = min(CAST_CHUNKS, n_steps)
    every = n_steps // n_chunks
    assert all(w.shape[0] % (n_chunks * 16) == 0 for w in weights)
    chunk = lambda *idx: (jnp.minimum(step_of(*idx) // every, n_chunks - 1), 0)
    return [pl.BlockSpec((w.shape[0] // n_chunks, w.shape[1]), chunk) for w in weights], every


def _cast_chunks(step, every, srcs, dsts):
    @pl.when(step % every == 0)
    def _():
        for src, dst in zip(srcs, dsts):
            dst[...] = src[...].astype(dst.dtype)


def _mixes(xn, prev, mu_ref, w1_ref, a1_ref, xm_ref, hw_ref, ha_ref):
    xx = prev - xn
    n_proj = xm_ref.shape[0]
    for p in range(n_proj):
        xm_ref[p] = (xn + xx * mu_ref[p:p + 1, :]).astype(xm_ref.dtype)
    xw = (xn + xx * mu_ref[n_proj:n_proj + 1, :]).astype(BF16)
    xa = (xn + xx * mu_ref[n_proj + 1:n_proj + 2, :]).astype(BF16)
    hw_ref[...] = jnp.tanh(_dot(xw, w1_ref[...])).astype(hw_ref.dtype)
    ha_ref[...] = _dot(xa, a1_ref[...]).astype(ha_ref.dtype)


def _norm_shift_kernel(x_ref, head_ref, g_ref, mu_ref, w1_ref, a1_ref, *rest, n_cast, cast_every):
    cast_in = rest[:n_cast]
    xm_ref, hw_ref, ha_ref, last_ref = rest[n_cast:n_cast + 4]
    cast_out = rest[n_cast + 4:2 * n_cast + 4]
    carry_ref = rest[-1]

    @pl.when(pl.program_id(1) == 0)
    def _():
        carry_ref[...] = jnp.zeros_like(carry_ref)

    _cast_chunks(pl.program_id(0) * pl.num_programs(1) + pl.program_id(1), cast_every, cast_in, cast_out)

    is_head = pl.program_id(1) == 0
    for sq in range(x_ref.shape[0]):
        x = jnp.where(is_head, head_ref[...], x_ref[sq])
        tm = x.shape[0]
        xn = x * lax.rsqrt(jnp.mean(x * x, axis=-1, keepdims=True) + RMS_EPS) * g_ref[...]
        rolled = pltpu.roll(xn, 1, axis=0)
        row = lax.broadcasted_iota(jnp.int32, xn.shape, 0)
        prev = jnp.where(row == 0, carry_ref[sq, 0:1, :], rolled)
        _mixes(xn, prev, mu_ref, w1_ref, a1_ref, xm_ref.at[:, sq], hw_ref.at[sq], ha_ref.at[sq])
        carry_ref[sq, 0:1, :] = xn[tm - 1:tm, :]
        last_ref[sq] = xn[tm - 1:tm, :]


def norm_shift_prompt(x, head, g, mu, w1, a1, cast_weights=()):
    b, seq, d = x.shape
    tm = BLOCK
    p = tm + seq
    n_mix = mu.shape[0]
    n_proj = n_mix - 2
    lr = w1.shape[1]
    nt = p // tm
    sp = SEQS_PER_STEP
    const = lambda shape: pl.BlockSpec(shape, lambda i, t: (0,) * len(shape))
    hid = pl.BlockSpec((sp, tm, lr), lambda i, t: (i, t, 0))
    cast_specs, every = _cast_job(cast_weights, (b // sp) * nt, lambda i, t: i * nt + t)
    xm, hw_act, ha_act, last, *casts = pl.pallas_call(
        functools.partial(_norm_shift_kernel, n_cast=len(cast_weights), cast_every=every),
        grid=(b // sp, nt),
        in_specs=[pl.BlockSpec((sp, tm, d), lambda i, t: (i, jnp.maximum(t - 1, 0), 0)),
                  const((tm, d)), const((1, d)), const((n_mix, d)), const((d, lr)), const((d, lr))] + cast_specs,
        out_specs=[pl.BlockSpec((n_proj, sp, tm, d), lambda i, t: (0, i, t, 0)), hid, hid,
                   pl.BlockSpec((sp, 1, d), lambda i, t: (i, 0, 0))] + cast_specs,
        out_shape=[jax.ShapeDtypeStruct((n_proj, b, p, d), BF16),
                   jax.ShapeDtypeStruct((b, p, lr), BF16), jax.ShapeDtypeStruct((b, p, lr), BF16),
                   jax.ShapeDtypeStruct((b, 1, d), F32)] + [jax.ShapeDtypeStruct(w.shape, BF16) for w in cast_weights],
        scratch_shapes=[pltpu.VMEM((sp, 8, d), F32)],
        compiler_params=_cparams(("arbitrary", "arbitrary")),
        name="norm_shift_prompt",
    )(x, head, g.reshape(1, d), mu, w1, a1, *cast_weights)
    return (xm.reshape(n_proj, b * p, d), hw_act.reshape(b * p, lr), ha_act.reshape(b * p, lr), last, *casts)


def _norm_shift_sample_kernel(x_ref, prev_ref, g_ref, mu_ref, w1_ref, a1_ref, xm_ref, hw_ref, ha_ref, xn_ref):
    x = x_ref[...]
    xn = x * lax.rsqrt(jnp.mean(x * x, axis=-1, keepdims=True) + RMS_EPS) * g_ref[...]
    xn_ref[...] = xn
    _mixes(xn, prev_ref[...], mu_ref, w1_ref, a1_ref, xm_ref, hw_ref, ha_ref)


def norm_shift_sample(x, prev, g, mu, w1, a1):
    m, d = x.shape
    lr = w1.shape[1]
    return pl.pallas_call(
        _norm_shift_sample_kernel,
        out_shape=[jax.ShapeDtypeStruct((mu.shape[0] - 2, m, d), BF16),
                   jax.ShapeDtypeStruct((m, lr), BF16), jax.ShapeDtypeStruct((m, lr), BF16),
                   jax.ShapeDtypeStruct((m, d), F32)],
        name="norm_shift_sample",
    )(x, prev, g.reshape(1, d), mu, w1, a1)


def _rope(y, cos, sin_a, sin_b):
    half = ROPE_DIM // 2
    step = ROPE_SLAB
    rep = step // cos.shape[1]
    tile = lambda t: jnp.concatenate([t] * rep, axis=1)
    cos_t, sa_t, sb_t = tile(cos), tile(sin_a), tile(sin_b)
    outs = []
    for j in range(y.shape[1] // step):
        ys = y[:, j * step:(j + 1) * step]
        outs.append(ys * cos_t + pltpu.roll(ys, step - half, axis=1) * sa_t + pltpu.roll(ys, half, axis=1) * sb_t)
    return jnp.concatenate(outs, axis=1) if len(outs) > 1 else outs[0]


def _mm_group_kernel(x_ref, w_ref, o_ref):
    o_ref[0] = _dot(x_ref[0], w_ref[0]).astype(o_ref.dtype)


def matmul_groups(x, w, tm, out_dtype, n=None, col=0, tn=None):
    g = w.shape[0]
    n = w.shape[2] if n is None else n
    tn = n if tn is None else tn
    nt = n // tn
    _, m, kdim = x.shape
    return pl.pallas_call(
        _mm_group_kernel,
        grid=(g, nt, m // tm),
        in_specs=[pl.BlockSpec((1, tm, kdim), lambda q, j, i: (q, i, 0)),
                  pl.BlockSpec((1, kdim, tn), lambda q, j, i: (q, 0, col * nt + j))],
        out_specs=pl.BlockSpec((1, tm, tn), lambda q, j, i: (q, i, j)),
        out_shape=jax.ShapeDtypeStruct((g, m, n), out_dtype),
        compiler_params=_cparams(("arbitrary", "arbitrary", "arbitrary")),
        name="matmul_groups",
    )(x, w)


def _mm_rope_kernel(x_ref, w_ref, cos_ref, sa_ref, sb_ref, *o_refs, n_rope, scale):
    y = _dot(x_ref[...], w_ref[...])
    rot = _rope(y[:, :n_rope], cos_ref[...], sa_ref[...], sb_ref[...])
    if scale != 1.0:
        rot = rot * scale
    o_refs[0][...] = rot.astype(o_refs[0].dtype)
    if len(o_refs) > 1:
        o_refs[1][...] = y[:, n_rope:].astype(o_refs[1].dtype)


def matmul_rope(x, w, tables, tm, n_rope, out_dtypes, scale=1.0, n=None):
    m, kdim = x.shape
    n = w.shape[-1] if n is None else n
    w_block = (kdim, n) if w.ndim == 2 else (None, kdim, n)
    lanes = tables[0].shape[1]
    widths = [n_rope] + ([n - n_rope] if n > n_rope else [])
    tab_blocks = tables[0].shape[0] // tm
    tab = pl.BlockSpec((tm, lanes), lambda i: (i % tab_blocks, 0))
    outs = pl.pallas_call(
        functools.partial(_mm_rope_kernel, n_rope=n_rope, scale=scale),
        grid=(m // tm,),
        in_specs=[pl.BlockSpec((tm, kdim), lambda i: (i, 0)),
                  pl.BlockSpec(w_block, lambda i: (0,) * w.ndim, pipeline_mode=pl.Buffered(1)),
                  tab, tab, tab],
        out_specs=[pl.BlockSpec((tm, wd), lambda i: (i, 0)) for wd in widths],
        out_shape=[jax.ShapeDtypeStruct((m, wd), dt) for wd, dt in zip(widths, out_dtypes)],
        compiler_params=_cparams(("arbitrary",)),
        name="matmul_rope",
    )(x, w, *tables)
    return outs


def _mm_res_norm_kernel(x_ref, w_ref, res_ref, g_ref, *out_refs, emit_h):
    h = res_ref[...] + _dot(x_ref[...], w_ref[...])
    hn_refs = out_refs
    if emit_h:
        out_refs[0][...] = h
        hn_refs = out_refs[1:]
    inv = lax.rsqrt(jnp.mean(h * h, axis=-1, keepdims=True) + RMS_EPS)
    for j, hn_ref in enumerate(hn_refs):
        hn_ref[...] = (h * inv * g_ref[j:j + 1, :]).astype(hn_ref.dtype)


def matmul_residual_norm(x, w, res, gains, tm, norm_dtype, emit_h=True):
    m, kdim = x.shape
    n = w.shape[1]
    ng = gains.shape[0]
    row = lambda width: pl.BlockSpec((tm, width), lambda i: (i, 0))
    return pl.pallas_call(
        functools.partial(_mm_res_norm_kernel, emit_h=emit_h),
        grid=(m // tm,),
        in_specs=[row(kdim),
                  pl.BlockSpec((kdim, n), lambda i: (0, 0), pipeline_mode=pl.Buffered(1)),
                  row(n),
                  pl.BlockSpec((ng, n), lambda i: (0, 0))],
        out_specs=[row(n)] * (int(emit_h) + ng),
        out_shape=[jax.ShapeDtypeStruct((m, n), F32)] * int(emit_h) + [jax.ShapeDtypeStruct((m, n), norm_dtype)] * ng,
        compiler_params=_cparams(("arbitrary",)),
        name="matmul_residual_norm",
    )(x, w, res, gains)


def _mm_res_norm_blocks_kernel(x_ref, w_ref, res_ref, *rest, emit_h, head, first_block):
    if head:
        head_ref, g_ref, *out_refs = rest
        res = jnp.where(pl.program_id(1) + first_block == 0, head_ref[...][None], res_ref[...])
    else:
        g_ref, *out_refs = rest
        res = res_ref[...]
    seqs, blk, n = res.shape
    h = res.reshape(seqs * blk, n) + _dot(x_ref[...].reshape(seqs * blk, x_ref.shape[2]), w_ref[...])
    hn_refs = out_refs
    if emit_h:
        out_refs[0][...] = h.reshape(seqs, blk, n)
        hn_refs = out_refs[1:]
    inv = lax.rsqrt(jnp.mean(h * h, axis=-1, keepdims=True) + RMS_EPS)
    for j, hn_ref in enumerate(hn_refs):
        hn_ref[...] = (h * inv * g_ref[j:j + 1, :]).astype(hn_ref.dtype).reshape(seqs, blk, n)


def matmul_residual_norm_blocks(x, w, res, gains, n_batch, norm_dtype, emit_h, head=None, first_block=0):
    kdim = x.shape[1]
    n = w.shape[1]
    ng = gains.shape[0]
    nb_out = x.shape[0] // (n_batch * BLOCK)
    nb = nb_out + first_block
    sp = RESIDUAL_SEQS_PER_STEP
    blocks = lambda width, shift: pl.BlockSpec((sp, BLOCK, width), lambda i, t: (i, t + shift, 0))
    const = lambda shape: pl.BlockSpec(shape, lambda i, t: (0,) * len(shape))
    if head is not None:
        res_specs = [pl.BlockSpec((sp, BLOCK, n), lambda i, t: (i, jnp.maximum(t + first_block - 1, 0), 0)),
                     const((BLOCK, n))]
        res_args = [res, head]
    else:
        res_specs, res_args = [blocks(n, first_block)], [res.reshape(n_batch, nb * BLOCK, n)]
    outs = pl.pallas_call(
        functools.partial(_mm_res_norm_blocks_kernel, emit_h=emit_h, head=head is not None, first_block=first_block),
        grid=(n_batch // sp, nb_out),
        in_specs=[blocks(kdim, 0), pl.BlockSpec((kdim, n), lambda i, t: (0, 0), pipeline_mode=pl.Buffered(1))]
        + res_specs + [const((ng, n))],
        out_specs=[blocks(n, first_block)] * int(emit_h) + [blocks(n, 0)] * ng,
        out_shape=[jax.ShapeDtypeStruct((n_batch, nb * BLOCK, n), F32)] * int(emit_h)
        + [jax.ShapeDtypeStruct((n_batch, nb_out * BLOCK, n), norm_dtype)] * ng,
        compiler_params=_cparams(("arbitrary", "arbitrary"), LARGE_VMEM_LIMIT),
        name="matmul_residual_norm_blocks",
    )(x.reshape(n_batch, nb_out * BLOCK, kdim), w, *res_args, gains)
    return [o.reshape(-1, n) for o in outs]


def _lora_up_kernel(hw_ref, ha_ref, w2_ref, w0_ref, a2_ref, a0_ref, wl_ref, al_ref):
    wl_ref[...] = w0_ref[...] + _dot(hw_ref[...], w2_ref[...])
    al_ref[...] = a0_ref[...] + _dot(ha_ref[...], a2_ref[...])


def lora_up(hw, ha, w2, w0, a2, a0):
    m = hw.shape[0]
    e = w2.shape[1]
    return pl.pallas_call(
        _lora_up_kernel,
        out_shape=[jax.ShapeDtypeStruct((m, e), F32), jax.ShapeDtypeStruct((m, e), F32)],
        name="lora_up",
    )(hw, ha, w2, w0.reshape(1, e), a2, a0.reshape(1, e))


def _seg_sum(x, ones_bd):
    hi = x.astype(BF16)
    outs = []
    for c in range(x.shape[1] // MXU_TILE):
        sl = slice(c * MXU_TILE, (c + 1) * MXU_TILE)
        outs.append(_dot(hi[:, sl], ones_bd))
    return jnp.concatenate(outs, axis=1) if len(outs) > 1 else outs[0]


def _wkv_prompt_kernel(r_ref, k_ref, v_ref, z_ref, hw_ref, ha_ref, w2_ref, w0_ref, a2_ref, a0_ref,
                       kk_ref, ka_ref, rk_ref, gg_ref, gb_ref, *rest, n_cast, cast_every):
    cast_in = rest[:n_cast]
    yg_ref, sout_ref = rest[n_cast:n_cast + 2]
    cast_out = rest[n_cast + 2:2 * n_cast + 2]
    s_ref = rest[-1]
    t_idx = pl.program_id(2)
    c = CHUNK
    hd = HEAD_DIM

    @pl.when(t_idx == 0)
    def _():
        s_ref[...] = jnp.zeros_like(s_ref)

    step = (pl.program_id(0) * pl.num_programs(1) + pl.program_id(1)) * pl.num_programs(2) + t_idx
    _cast_chunks(step, cast_every, cast_in, cast_out)

    tb = r_ref.shape[1]
    nh = HEADS_PER_STREAM
    hw = nh * hd
    nc = tb // c

    li = lax.broadcasted_iota(jnp.int32, (MXU_TILE, MXU_TILE), 0) // hd
    lj = lax.broadcasted_iota(jnp.int32, (MXU_TILE, MXU_TILE), 1) // hd
    ones_bd = jnp.where(li == lj, 1.0, 0.0).astype(BF16)
    bi_ = lax.broadcasted_iota(jnp.int32, (tb, tb), 0)
    bj_ = lax.broadcasted_iota(jnp.int32, (tb, tb), 1)
    tri_incl = jnp.where((bj_ <= bi_) & (bj_ // c == bi_ // c), 1.0, 0.0).astype(BF16)
    ti = lax.broadcasted_iota(jnp.int32, (c, c), 0)
    tj = lax.broadcasted_iota(jnp.int32, (c, c), 1)
    ai = lax.broadcasted_iota(jnp.int32, (c, 2 * c), 0)
    aj = lax.broadcasted_iota(jnp.int32, (c, 2 * c), 1)
    upper = aj >= c
    aj_mod = jnp.where(upper, aj - c, aj)
    masks = dict(
        strict=tj < ti,
        eye=jnp.where(ti == tj, 1.0, 0.0).astype(F32),
        top_k=upper & (aj_mod < ai),
        bot=aj_mod <= ai,
        mean_bd=jnp.where(li == lj, 1.0 / hd, 0.0).astype(BF16))

    for st in range(r_ref.shape[2] // hw):
        _wkv_stream(st, hw, nc, r_ref, k_ref, v_ref, z_ref, hw_ref, ha_ref, w2_ref, w0_ref, a2_ref, a0_ref,
                    kk_ref, ka_ref, rk_ref, gg_ref, gb_ref, yg_ref, s_ref, ones_bd, tri_incl, masks)

    @pl.when(t_idx == pl.num_programs(2) - 1)
    def _():
        sout_ref[0] = s_ref[...]


def _wkv_stream(st, hw, nc, r_ref, k_ref, v_ref, z_ref, hw_ref, ha_ref, w2_ref, w0_ref, a2_ref, a0_ref,
                kk_ref, ka_ref, rk_ref, gg_ref, gb_ref, yg_ref, s_ref, ones_bd, tri_incl, masks):
    c = CHUNK
    hd = HEAD_DIM
    nh = hw // hd
    ls = slice(st * hw, (st + 1) * hw)
    h0 = st * nh
    r = r_ref[0, :, ls]
    k = k_ref[0, :, ls]
    v = v_ref[0, :, ls]
    wl = w0_ref[:, ls] + _dot(hw_ref[...], w2_ref[:, ls])
    al = a0_ref[:, ls] + _dot(ha_ref[...], a2_ref[:, ls])
    a = _sigmoid(al)
    lw = (-DECAY_SCALE * LOG2E) * _sigmoid(wl)
    kk = k * kk_ref[:, ls]
    n2 = _seg_sum(kk * kk, ones_bd)
    kk = kk * lax.rsqrt(jnp.maximum(n2, NORM_FLOOR_SQ))
    ka = ka_ref[:, ls]
    k2 = k * (a * ka + (1.0 - ka))
    bb = kk * a

    lw_hi, lw_lo = _split_hi_lo(lw)
    g = _dot(tri_incl, lw_hi) + _dot(tri_incl, lw_lo)
    mid = lambda ci: g[ci * c + c // 2 - 1:ci * c + c // 2, :]
    gm = jnp.concatenate([jnp.broadcast_to(mid(ci), (c, hw)) for ci in range(nc)], axis=0)
    t = g - gm
    e_a = jnp.exp2(t)
    e_prev = jnp.exp2(t - lw)
    e_inv = jnp.exp2(-t)
    e1 = [jnp.exp2(mid(ci)) for ci in range(nc)]
    e2 = [jnp.exp2(g[ci * c + c - 1:ci * c + c, :] - mid(ci)) for ci in range(nc)]

    kkd = (kk * e_prev).astype(BF16)
    rd = (r * e_a).astype(BF16)
    bi = (bb * e_inv).astype(BF16)
    ki = (k2 * e_inv).astype(BF16)
    v_bf = v.astype(BF16)
    zeros_cv = jnp.zeros((c, hd), BF16)

    pairs = [(ci, h) for ci in range(nc) for h in range(nh)]
    rows = lambda ci: slice(ci * c, (ci + 1) * c)
    cols = lambda h: slice(h * hd, (h + 1) * hd)
    xs = {(ci, h): jnp.concatenate([kkd[rows(ci), cols(h)], rd[rows(ci), cols(h)]], axis=0) for ci, h in pairs}
    r1s = {(ci, h): jnp.concatenate([bi[rows(ci), cols(h)], ki[rows(ci), cols(h)]], axis=0) for ci, h in pairs}
    vs = {(ci, h): v_bf[rows(ci), cols(h)] for ci, h in pairs}
    a_mats = {p: _dot_nt(xs[p], r1s[p]) for p in pairs}
    lk_vs = {p: _dot(jnp.where(masks["top_k"], a_mats[p][:c, :], 0.0).astype(BF16),
                     jnp.concatenate([zeros_cv, vs[p]], axis=0)) for p in pairs}
    lps = {p: jnp.where(masks["strict"], a_mats[p][:c, :c], 0.0) for p in pairs}
    ts = {p: masks["eye"] - lps[p] for p in pairs}
    for _ in range(int(math.log2(c)) - 1):
        lpb = {p: lps[p].astype(BF16) for p in pairs}
        lps = {p: _dot(lpb[p], lpb[p]) for p in pairs}
        ts = {p: _dot(ts[p].astype(BF16), (masks["eye"] + lps[p]).astype(BF16)) for p in pairs}
    a_bots = {p: jnp.where(masks["bot"], a_mats[p][c:, :], 0.0).astype(BF16) for p in pairs}
    t_bf = {p: ts[p].astype(BF16) for p in pairs}

    state = [s_ref[h0 + h] for h in range(nh)]
    y_rows = []
    for ci in range(nc):
        hs = range(nh)
        sms = [state[h] * e1[ci][:, cols(h)] for h in hs]
        p_mats = [_dot_nt(xs[ci, h], sms[h].astype(BF16)) for h in hs]
        us = [-_dot(t_bf[ci, h], (p_mats[h][:c, :] + lk_vs[ci, h]).astype(BF16)) for h in hs]
        uvs = [jnp.concatenate([us[h].astype(BF16), vs[ci, h]], axis=0) for h in hs]
        ys = [p_mats[h][c:, :] + _dot(a_bots[ci, h], uvs[h]) for h in hs]
        state = [(sms[h] + _dot_tn(uvs[h], r1s[ci, h])) * e2[ci][:, cols(h)] for h in hs]
        y_rows.append(jnp.concatenate(ys, axis=1))
    for h in range(nh):
        s_ref[h0 + h] = state[h]
    y = jnp.concatenate(y_rows, axis=0) if nc > 1 else y_rows[0]

    mean = _seg_sum(y, masks["mean_bd"])
    yc = y - mean
    var = _seg_sum(yc * yc, masks["mean_bd"])
    yn = yc * lax.rsqrt(var + GN_EPS) * gg_ref[:, ls] + gb_ref[:, ls]
    bonus = _seg_sum(r * k2 * rk_ref[:, ls], ones_bd)
    yg_ref[:, ls] = ((yn + bonus * v) * _silu(z_ref[0, :, ls])).astype(BF16)


def wkv_prompt(rkvz, hw_act, ha_act, w2, w0, a2, a0, k_k, k_a, r_k, gn_g, gn_b, n_batch, cast_weights=()):
    _, m, e = rkvz.shape
    p = m // n_batch
    tb = WKV_ROWS
    nt = p // tb
    hw = HEADS_PER_STEP * HEAD_DIM
    nh = e // HEAD_DIM
    row = lambda i, g, t: (i * nt + t, g)
    proj = lambda q: pl.BlockSpec((1, tb, hw), lambda i, g, t: (q, i * nt + t, g))
    par = pl.BlockSpec((1, hw), lambda i, g, t: (0, g))
    lr = w2.shape[0]
    hid = pl.BlockSpec((tb, lr), lambda i, g, t: (i * nt + t, 0))
    up = pl.BlockSpec((lr, hw), lambda i, g, t: (0, g))
    n_groups = e // hw
    cast_specs, every = _cast_job(cast_weights, n_batch * n_groups * nt, lambda i, g, t: (i * n_groups + g) * nt + t)
    return pl.pallas_call(
        functools.partial(_wkv_prompt_kernel, n_cast=len(cast_weights), cast_every=every),
        grid=(n_batch, n_groups, nt),
        in_specs=[proj(0), proj(1), proj(2), proj(3), hid, hid, up, par, up, par,
                  par, par, par, par, par] + cast_specs,
        out_specs=[pl.BlockSpec((tb, hw), row),
                   pl.BlockSpec((1, HEADS_PER_STEP, HEAD_DIM, HEAD_DIM), lambda i, g, t: (i, g, 0, 0))] + cast_specs,
        out_shape=[jax.ShapeDtypeStruct((m, e), BF16),
                   jax.ShapeDtypeStruct((n_batch, nh, HEAD_DIM, HEAD_DIM), F32)]
        + [jax.ShapeDtypeStruct(w.shape, BF16) for w in cast_weights],
        scratch_shapes=[pltpu.VMEM((HEADS_PER_STEP, HEAD_DIM, HEAD_DIM), F32)],
        compiler_params=_cparams(("arbitrary", "arbitrary", "arbitrary"), LARGE_VMEM_LIMIT),
        name="wkv_prompt",
    )(rkvz, rkvz, rkvz, rkvz, hw_act, ha_act, w2, w0.reshape(1, e), a2, a0.reshape(1, e),
      k_k.reshape(1, e), k_a.reshape(1, e), r_k.reshape(1, e), gn_g.reshape(1, e), gn_b.reshape(1, e),
      *cast_weights)


def _wkv_sample_kernel(r_ref, k_ref, v_ref, z_ref, wl_ref, al_ref, kk_ref, ka_ref, rk_ref, gg_ref, gb_ref,
                       s_ref, yg_ref, sout_ref, y_scr):
    hd = HEAD_DIM
    r = r_ref[0]
    k = k_ref[0]
    v = v_ref[0]
    a = _sigmoid(al_ref[0])
    d = jnp.exp2((-DECAY_SCALE * LOG2E) * _sigmoid(wl_ref[0]))
    kk = k * kk_ref[...]
    kk = kk / jnp.maximum(jnp.sqrt(jnp.sum(kk * kk, axis=-1, keepdims=True)), 1e-12)
    k2 = k * (1.0 + (a - 1.0) * ka_ref[...])
    bb = kk * a
    nh = r.shape[0]
    ii = lax.broadcasted_iota(jnp.int32, (hd, hd), 0)
    jj = lax.broadcasted_iota(jnp.int32, (hd, hd), 1)
    eye = ii == jj

    row = lambda x, h: x[h:h + 1, :]
    group = 8
    for h0 in range(0, nh, group):
        hs = range(h0, h0 + group)
        s = {h: s_ref[0, h] for h in hs}
        sa = {h: jnp.sum(s[h] * row(kk, h), axis=-1, keepdims=True) for h in hs}
        v_col = {h: jnp.sum(jnp.where(eye, row(v, h), 0.0), axis=-1, keepdims=True) for h in hs}
        s_new = {h: s[h] * row(d, h) - sa[h] * row(bb, h) + v_col[h] * row(k2, h) for h in hs}
        y_col = {h: jnp.sum(s_new[h] * row(r, h), axis=-1, keepdims=True) for h in hs}
        for h in hs:
            sout_ref[0, h] = s_new[h]
            y_scr[h:h + 1, :] = jnp.sum(jnp.where(eye, y_col[h], 0.0), axis=0, keepdims=True)

    y = y_scr[...]
    mean = jnp.mean(y, axis=-1, keepdims=True)
    yc = y - mean
    var = jnp.mean(yc * yc, axis=-1, keepdims=True)
    yn = yc * lax.rsqrt(var + GN_EPS) * gg_ref[...] + gb_ref[...]
    bonus = jnp.sum(r * k2 * rk_ref[...], axis=-1, keepdims=True)
    yg_ref[0] = ((yn + bonus * v) * _silu(z_ref[0])).astype(BF16)


def wkv_sample(rkvz, wl, al, k_k, k_a, r_k, gn_g, gn_b, state):
    _, m, e = rkvz.shape
    nh = e // HEAD_DIM
    hd = HEAD_DIM
    rkvz4 = rkvz.reshape(4, m, nh, hd)
    proj = lambda q: pl.BlockSpec((None, 1, nh, hd), lambda i: (q, i, 0, 0))
    tok = pl.BlockSpec((1, nh, hd), lambda i: (i, 0, 0))
    par = pl.BlockSpec((nh, hd), lambda i: (0, 0))
    st = pl.BlockSpec((1, nh, hd, hd), lambda i: (i, 0, 0, 0))
    as_heads = lambda x: x.reshape(nh, hd)
    return pl.pallas_call(
        _wkv_sample_kernel,
        grid=(m,),
        in_specs=[proj(0), proj(1), proj(2), proj(3), tok, tok, par, par, par, par, par, st],
        out_specs=[tok, st],
        out_shape=[jax.ShapeDtypeStruct((m, nh, hd), BF16), jax.ShapeDtypeStruct(state.shape, F32)],
        scratch_shapes=[pltpu.VMEM((nh, hd), F32)],
        compiler_params=_cparams(("arbitrary",)),
        name="wkv_sample",
    )(rkvz4, rkvz4, rkvz4, rkvz4, wl.reshape(m, nh, hd), al.reshape(m, nh, hd),
      as_heads(k_k), as_heads(k_a), as_heads(r_k), as_heads(gn_g), as_heads(gn_b), state)


def rope_tables(pos):
    half = ROPE_DIM // 2
    inv_freq = ROPE_THETA ** (-jnp.arange(half, dtype=F32) * 2.0 / ROPE_DIM)
    ang = pos.astype(F32)[:, None] * inv_freq[None, :]
    cos = jnp.cos(ang)
    sin = jnp.sin(ang)
    rows = pos.shape[0]
    ones = jnp.ones((rows, HEAD_DIM - ROPE_DIM), F32)
    zeros_h = jnp.zeros((rows, half), F32)
    zeros_r = jnp.zeros((rows, HEAD_DIM - ROPE_DIM), F32)
    cos_h = jnp.concatenate([cos, cos, ones], axis=1)
    sa_h = jnp.concatenate([-sin, zeros_h, zeros_r], axis=1)
    sb_h = jnp.concatenate([zeros_h, sin, zeros_r], axis=1)
    two = lambda t: jnp.concatenate([t, t], axis=1)
    return two(cos_h), two(sa_h), two(sb_h)


def _attn_prompt_kernel(sink_ref, q_ref, kc_ref, kp_ref, vc_ref, vp_ref, z_ref, o_ref, *, first_block):
    n = pl.program_id(1) + first_block
    hd = HEAD_DIM
    seqs, blk, _ = q_ref.shape
    n_kv = kc_ref.shape[2] // hd
    grp = q_ref.shape[2] // (n_kv * hd)
    qi = lax.broadcasted_iota(jnp.int32, (blk, 2 * blk), 0)
    kj = lax.broadcasted_iota(jnp.int32, (blk, 2 * blk), 1) - blk
    kpos = n * blk + kj
    diff = qi - kj
    valid = (kpos >= LEAD) & (diff >= 0) & (diff <= WINDOW)
    for sq in range(seqs):
        k_all = jnp.concatenate([kp_ref[sq], kc_ref[sq]], axis=0).astype(BF16)
        v_all = jnp.concatenate([vp_ref[sq], vc_ref[sq]], axis=0).astype(BF16)

        def scores(h):
            k_h = k_all[:, h * hd:(h + 1) * hd]
            return [_dot_nt(q_ref[sq, :, (h * grp + gi) * hd:(h * grp + gi + 1) * hd], k_h) for gi in range(grp)]

        outs = []
        s_next = scores(0)
        for h in range(n_kv):
            s_cur = s_next
            if h + 1 < n_kv:
                s_next = scores(h + 1)
            v_h = v_all[:, h * hd:(h + 1) * hd]
            for g0 in range(0, grp, SOFTMAX_BATCH):
                gs = range(g0, min(g0 + SOFTMAX_BATCH, grp))
                sks = {gi: sink_ref[h * grp + gi] * LOG2E for gi in gs}
                ss = {gi: jnp.where(valid, s_cur[gi], -jnp.inf) for gi in gs}
                ms = {gi: jnp.maximum(jnp.max(ss[gi], axis=-1, keepdims=True), sks[gi]) for gi in gs}
                ps = {gi: jnp.exp2(ss[gi] - ms[gi]) for gi in gs}
                dens = {gi: jnp.sum(ps[gi], axis=-1, keepdims=True) + jnp.exp2(sks[gi] - ms[gi]) for gi in gs}
                outs += [_dot(ps[gi].astype(BF16), v_h) / dens[gi] for gi in gs]
        att = jnp.concatenate(outs, axis=1)
        o_ref[sq] = (att * _silu(z_ref[sq])).astype(o_ref.dtype)


def attn_prompt(sinks, q, k, v, z, n_batch, first_block):
    m, e = q.shape
    nb = m // (n_batch * BLOCK)
    nb_out = nb - first_block
    kw = k.shape[1]
    sp = SEQS_PER_STEP
    cur = lambda i, n: (i, n + first_block, 0)
    prv = lambda i, n: (i, jnp.maximum(n + first_block - 1, 0), 0)
    seq_rows = lambda t: t.reshape(n_batch, nb * BLOCK, t.shape[1])
    q, k, v, z = seq_rows(q), seq_rows(k), seq_rows(v), seq_rows(z)
    return pl.pallas_call(
        functools.partial(_attn_prompt_kernel, first_block=first_block),
        grid=(n_batch // sp, nb_out),
        in_specs=[pl.BlockSpec(memory_space=pltpu.SMEM),
                  pl.BlockSpec((sp, BLOCK, e), cur),
                  pl.BlockSpec((sp, BLOCK, kw), cur), pl.BlockSpec((sp, BLOCK, kw), prv),
                  pl.BlockSpec((sp, BLOCK, kw), cur), pl.BlockSpec((sp, BLOCK, kw), prv),
                  pl.BlockSpec((sp, BLOCK, e), cur)],
        out_specs=pl.BlockSpec((sp, BLOCK, e), lambda i, n: (i, n, 0)),
        out_shape=jax.ShapeDtypeStruct((n_batch, nb_out * BLOCK, e), BF16),
        compiler_params=_cparams(("arbitrary", "arbitrary")),
        name="attn_prompt",
    )(sinks, q, k, k, v, v, z).reshape(n_batch * nb_out * BLOCK, e)


def _attn_sample_kernel(sink_ref, q_ref, kc_ref, vc_ref, kn_ref, vn_ref, z_ref, o_ref, ko_ref, vo_ref):
    hd = HEAD_DIM
    win = kc_ref.shape[1]
    n_kv = kc_ref.shape[2] // hd
    nq = q_ref.shape[1]
    grp = nq // n_kv
    pad = 8
    kc = kc_ref[0]
    vc = vc_ref[0]
    kn = kn_ref[0]
    vn = vn_ref[0]
    first = lax.broadcasted_iota(jnp.int32, (pad, kc.shape[1]), 0) == 0
    k_all = jnp.concatenate([kc, jnp.where(first, kn, 0.0)], axis=0).astype(BF16)
    v_all = jnp.concatenate([vc, jnp.where(first, vn, 0.0)], axis=0).astype(BF16)
    col = lax.broadcasted_iota(jnp.int32, (grp, win + pad), 1)
    valid = (col <= win) & (win - col <= WINDOW)
    q = q_ref[0].astype(BF16)
    row_i = lax.broadcasted_iota(jnp.int32, (grp, 1), 0)
    hs = range(n_kv)
    sks = []
    for h in hs:
        sk = jnp.zeros((grp, 1), F32)
        for gi in range(grp):
            sk = jnp.where(row_i == gi, sink_ref[h * grp + gi] * LOG2E, sk)
        sks.append(sk)
    ss = [jnp.where(valid, _dot_nt(q[h * grp:(h + 1) * grp, :], k_all[:, h * hd:(h + 1) * hd]), -jnp.inf)
          for h in hs]
    ms = [jnp.maximum(jnp.max(ss[h], axis=-1, keepdims=True), sks[h]) for h in hs]
    ps = [jnp.exp2(ss[h] - ms[h]) for h in hs]
    dens = [jnp.sum(ps[h], axis=-1, keepdims=True) + jnp.exp2(sks[h] - ms[h]) for h in hs]
    outs = [_dot(ps[h].astype(BF16), v_all[:, h * hd:(h + 1) * hd]) / dens[h] for h in hs]
    att = jnp.concatenate(outs, axis=0)
    o_ref[0] = (att * _silu(z_ref[0])).astype(o_ref.dtype)
    last = lax.broadcasted_iota(jnp.int32, kc.shape, 0) == win - 1
    ko_ref[0] = jnp.where(last, kn, pltpu.roll(kc, win - 1, axis=0))
    vo_ref[0] = jnp.where(last, vn, pltpu.roll(vc, win - 1, axis=0))


def attn_sample(sinks, q, cache_k, cache_v, k_new, v_new, z):
    m, win, kw = cache_k.shape
    nq = q.shape[1]
    hd = HEAD_DIM
    tok = pl.BlockSpec((1, nq, hd), lambda i: (i, 0, 0))
    cache = pl.BlockSpec((1, win, kw), lambda i: (i, 0, 0))
    new = pl.BlockSpec((1, 1, kw), lambda i: (i, 0, 0))
    return pl.pallas_call(
        _attn_sample_kernel,
        grid=(m,),
        in_specs=[pl.BlockSpec(memory_space=pltpu.SMEM), tok, cache, cache, new, new, tok],
        out_specs=[tok, cache, cache],
        out_shape=[jax.ShapeDtypeStruct((m, nq, hd), BF16),
                   jax.ShapeDtypeStruct(cache_k.shape, F32), jax.ShapeDtypeStruct(cache_v.shape, F32)],
        compiler_params=_cparams(("arbitrary",)),
        name="attn_sample",
    )(sinks, q, cache_k, cache_v, k_new, v_new, z)


def _pad_lora(w_down, w_up):
    r = w_down.shape[1]
    return (jnp.pad(w_down, ((0, 0), (0, LORA_PAD - r))).astype(BF16),
            jnp.pad(w_up, ((0, LORA_PAD - r), (0, 0))).astype(BF16))


def kernel(x_prompt, x_sample, state_wkv, state_shift, cache_k, cache_v, meta_tokens, a_norm, a_mu, a_w_rkvz,
           a_w0, a_w1, a_w2, a_a0, a_a1, a_a2, a_k_k, a_k_a, a_r_k, a_gn_g, a_gn_b, a_w_out, kv_norm, w_kv,
           b_norm, b_w_qz, b_sinks, b_w_o, final_norm):
    nb, seq, d = x_prompt.shape
    db, dseq, _ = x_sample.shape
    assert dseq == 1 and a_norm.shape[0] == 1 and b_norm.shape[0] == 1
    e = a_w_rkvz.shape[3]
    win = cache_k.shape[1]
    p_len = LEAD + N_META + seq
    assert p_len % BLOCK == 0 and (LEAD + N_META) == BLOCK
    kvw = N_KV_HEADS * HEAD_DIM

    w1, w2 = _pad_lora(a_w1[0], a_w2[0])
    a1, a2 = _pad_lora(a_a1[0], a_a2[0])
    mu = a_mu[0]
    sinks = b_sinks[0]
    gains_b = jnp.stack([kv_norm, b_norm[0]])

    tm = p_len // 8

    head = jnp.concatenate([jnp.zeros((LEAD, d), F32), meta_tokens], axis=0)
    w4 = a_w_rkvz[0]
    xm, hw_p, ha_p, x_last, w_rkvz = norm_shift_prompt(x_prompt, head, a_norm[0], mu, w1, a1,
                                                      cast_weights=(w4.reshape(-1, w4.shape[2]),))
    w_rkvz = w_rkvz.reshape(w4.shape)
    p_state_shift = x_last.reshape(1, nb, d)
    rkvz = matmul_groups(xm, w_rkvz, 4 * tm, F32, tn=e // 2)
    yg, p_state, w_out, w_kv_bf, w_qz, w_o = wkv_prompt(
        rkvz, hw_p, ha_p, w2, a_w0[0], a2, a_a0[0], a_k_k[0], a_k_a[0], a_r_k[0].reshape(-1), a_gn_g[0],
        a_gn_b[0], nb, cast_weights=(a_w_out[0], w_kv, b_w_qz[0], b_w_o[0]))
    w_qz = w_qz[None]
    hp, hn_kv, hn_b = matmul_residual_norm_blocks(yg, w_out, x_prompt, gains_b, nb, BF16, True, head=head)

    pos_p = jnp.maximum(jnp.arange(p_len, dtype=jnp.int32) - LEAD, 0)
    tabs_p = rope_tables(pos_p)
    k_p, v_p = matmul_rope(hn_kv, w_kv_bf, tabs_p, 4 * tm, kvw, (F32, F32))
    q_p, = matmul_rope(hn_b, w_qz, tabs_p, 2 * tm, e, (BF16,), scale=Q_SCALE, n=e)
    z_p = matmul_groups(hn_b[None], w_qz, 4 * tm, F32, n=e, col=1, tn=e // 2)[0]
    skip = (LEAD + N_META) // BLOCK
    att = attn_prompt(sinks, q_p, k_p, v_p, z_p, nb, skip)
    y_prompt, = matmul_residual_norm_blocks(att, w_o, hp, final_norm[None], nb, F32, False, first_block=skip)
    y_prompt = y_prompt.reshape(nb, seq, d)
    tail = lambda t: t.reshape(nb, p_len, kvw)[:, -win:].reshape(nb, win, N_KV_HEADS, HEAD_DIM)
    p_cache_k = tail(k_p)
    p_cache_v = tail(v_p)

    hs = x_sample.reshape(db, d)
    xm_s, hw_s, ha_s, xn_s = norm_shift_sample(hs, state_shift[0], a_norm[0], mu, w1, a1)
    rkvz_s = matmul_groups(xm_s, w_rkvz, db, F32)
    wl_s, al_s = lora_up(hw_s, ha_s, w2, a_w0[0], a2, a_a0[0])
    yg_s, s_state = wkv_sample(rkvz_s, wl_s, al_s, a_k_k[0], a_k_a[0], a_r_k[0].reshape(-1), a_gn_g[0],
                               a_gn_b[0], state_wkv[0])
    hs, hn_kv_s, hn_b_s = matmul_residual_norm(yg_s.reshape(db, e), w_out, hs, gains_b, db, BF16)
    tabs_s = rope_tables(jnp.full((db,), PAST_LEN, jnp.int32))
    k_s, v_s = matmul_rope(hn_kv_s, w_kv_bf, tabs_s, db, kvw, (F32, F32))
    q_s, = matmul_rope(hn_b_s, w_qz, tabs_s, db, e, (F32,), scale=Q_SCALE, n=e)
    z_s = matmul_groups(hn_b_s[None], w_qz, db, F32, n=e, col=1)[0]
    nq = e // HEAD_DIM
    att_s, s_cache_k, s_cache_v = attn_sample(
        sinks, q_s.reshape(db, nq, HEAD_DIM), cache_k.reshape(db, win, kvw), cache_v.reshape(db, win, kvw),
        k_s.reshape(db, 1, kvw), v_s.reshape(db, 1, kvw), z_s.reshape(db, nq, HEAD_DIM))
    y_s, = matmul_residual_norm(att_s.reshape(db, e), w_o, hs, final_norm[None], db, F32, emit_h=False)
    y_sample = y_s.reshape(db, 1, d)

    return (y_prompt, y_sample, p_state[None], p_state_shift,
            p_cache_k, p_cache_v,
            s_state[None], xn_s[None],
            s_cache_k.reshape(cache_k.shape), s_cache_v.reshape(cache_v.shape))
```

```python
import functools
import math

import jax
import jax.numpy as jnp
from jax import lax
from jax.experimental import pallas as pl
from jax.experimental.pallas import tpu as pltpu

F32 = jnp.float32
BF16 = jnp.bfloat16

HEAD_DIM = 64
N_KV_HEADS = 8
WINDOW = 128
BLOCK = 128
ROPE_DIM = HEAD_DIM // 4
ROPE_THETA = 500000.0
N_META = 16
PAST_LEN = 16384
RMS_EPS = 1e-6
GN_EPS = 64e-5
NORM_FLOOR_SQ = 1e-24
LEAD = (-N_META) % BLOCK
CHUNK = 64
WKV_ROWS = 128
CAST_CHUNKS = 32
SEQS_PER_STEP = 2
RESIDUAL_SEQS_PER_STEP = 4
HEADS_PER_STREAM = 32
HEADS_PER_STEP = 32
LORA_PAD = 128
MXU_TILE = 256
ROPE_SLAB = 512
VMEM_LIMIT = 48 * 1024 * 1024
RESIDUAL_VMEM_LIMIT = 56 * 1024 * 1024
LOG2E = 1.0 / math.log(2.0)
DECAY_SCALE = math.exp(-0.5)
SOFTMAX_BATCH = 4
Q_SCALE = HEAD_DIM ** -0.5 * LOG2E


def _cparams(sem, vmem_limit=VMEM_LIMIT):
    return pltpu.CompilerParams(dimension_semantics=sem, vmem_limit_bytes=vmem_limit)


def _sigmoid(x):
    return 1.0 / (1.0 + jnp.exp2(x * (-LOG2E)))


def _silu(x):
    return x * _sigmoid(x)


def _dot(a, b):
    return jnp.dot(a, b, preferred_element_type=F32)


def _dot_nt(a, b):
    return lax.dot_general(a, b, (((1,), (1,)), ((), ())), preferred_element_type=F32)


def _dot_tn(a, b):
    return lax.dot_general(a, b, (((0,), (0,)), ((), ())), preferred_element_type=F32)


def _split_hi_lo(x):
    hi = x.astype(BF16)
    lo = (x - hi.astype(F32)).astype(BF16)
    return hi, lo


def _cast_job(weights, n_steps, step_of):
    n_chunks = min(CAST_CHUNKS, n_steps)
    every = n_steps // n_chunks
    assert all(w.shape[0] % (n_chunks * 16) == 0 for w in weights)
    chunk = lambda *idx: (jnp.minimum(step_of(*idx) // every, n_chunks - 1), 0)
    return [pl.BlockSpec((w.shape[0] // n_chunks, w.shape[1]), chunk) for w in weights], every


def _cast_chunks(step, every, srcs, dsts):
    @pl.when(step % every == 0)
    def _():
        for src, dst in zip(srcs, dsts):
            dst[...] = src[...].astype(dst.dtype)


def _mixes(xn, prev, mu_ref, w1_ref, a1_ref, xm_ref, hw_ref, ha_ref):
    xx = prev - xn
    n_proj = xm_ref.shape[0]
    for p in range(n_proj):
        xm_ref[p] = (xn + xx * mu_ref[p:p + 1, :]).astype(xm_ref.dtype)
    xw = (xn + xx * mu_ref[n_proj:n_proj + 1, :]).astype(BF16)
    xa = (xn + xx * mu_ref[n_proj + 1:n_proj + 2, :]).astype(BF16)
    hw_ref[...] = jnp.tanh(_dot(xw, w1_ref[...])).astype(hw_ref.dtype)
    ha_ref[...] = _dot(xa, a1_ref[...]).astype(ha_ref.dtype)


def _norm_shift_kernel(x_ref, head_ref, g_ref, mu_ref, w1_ref, a1_ref, *rest, n_cast, cast_every):
    cast_in = rest[:n_cast]
    xm_ref, hw_ref, ha_ref, last_ref = rest[n_cast:n_cast + 4]
    cast_out = rest[n_cast + 4:2 * n_cast + 4]
    carry_ref = rest[-1]

    @pl.when(pl.program_id(1) == 0)
    def _():
        carry_ref[...] = jnp.zeros_like(carry_ref)

    _cast_chunks(pl.program_id(0) * pl.num_programs(1) + pl.program_id(1), cast_every, cast_in, cast_out)

    is_head = pl.program_id(1) == 0
    for sq in range(x_ref.shape[0]):
        x = jnp.where(is_head, head_ref[...], x_ref[sq])
        tm = x.shape[0]
        xn = x * lax.rsqrt(jnp.mean(x * x, axis=-1, keepdims=True) + RMS_EPS) * g_ref[...]
        rolled = pltpu.roll(xn, 1, axis=0)
        row = lax.broadcasted_iota(jnp.int32, xn.shape, 0)
        prev = jnp.where(row == 0, carry_ref[sq, 0:1, :], rolled)
        _mixes(xn, prev, mu_ref, w1_ref, a1_ref, xm_ref.at[:, sq], hw_ref.at[sq], ha_ref.at[sq])
        carry_ref[sq, 0:1, :] = xn[tm - 1:tm, :]
        last_ref[sq] = xn[tm - 1:tm, :]


def norm_shift_prompt(x, head, g, mu, w1, a1, cast_weights=()):
    b, seq, d = x.shape
    tm = BLOCK
    p = tm + seq
    n_mix = mu.shape[0]
    n_proj = n_mix - 2
    lr = w1.shape[1]
    nt = p // tm
    sp = SEQS_PER_STEP
    const = lambda shape: pl.BlockSpec(shape, lambda i, t: (0,) * len(shape))
    hid = pl.BlockSpec((sp, tm, lr), lambda i, t: (i, t, 0))
    cast_specs, every = _cast_job(cast_weights, (b // sp) * nt, lambda i, t: i * nt + t)
    xm, hw_act, ha_act, last, *casts = pl.pallas_call(
        functools.partial(_norm_shift_kernel, n_cast=len(cast_weights), cast_every=every),
        grid=(b // sp, nt),
        in_specs=[pl.BlockSpec((sp, tm, d), lambda i, t: (i, jnp.maximum(t - 1, 0), 0)),
                  const((tm, d)), const((1, d)), const((n_mix, d)), const((d, lr)), const((d, lr))] + cast_specs,
        out_specs=[pl.BlockSpec((n_proj, sp, tm, d), lambda i, t: (0, i, t, 0)), hid, hid,
                   pl.BlockSpec((sp, 1, d), lambda i, t: (i, 0, 0))] + cast_specs,
        out_shape=[jax.ShapeDtypeStruct((n_proj, b, p, d), BF16),
                   jax.ShapeDtypeStruct((b, p, lr), BF16), jax.ShapeDtypeStruct((b, p, lr), BF16),
                   jax.ShapeDtypeStruct((b, 1, d), F32)] + [jax.ShapeDtypeStruct(w.shape, BF16) for w in cast_weights],
        scratch_shapes=[pltpu.VMEM((sp, 8, d), F32)],
        compiler_params=_cparams(("arbitrary", "arbitrary")),
        name="norm_shift_prompt",
    )(x, head, g.reshape(1, d), mu, w1, a1, *cast_weights)
    return (xm.reshape(n_proj, b * p, d), hw_act.reshape(b * p, lr), ha_act.reshape(b * p, lr), last, *casts)


def _norm_shift_sample_kernel(x_ref, prev_ref, g_ref, mu_ref, w1_ref, a1_ref, xm_ref, hw_ref, ha_ref, xn_ref):
    x = x_ref[...]
    xn = x * lax.rsqrt(jnp.mean(x * x, axis=-1, keepdims=True) + RMS_EPS) * g_ref[...]
    xn_ref[...] = xn
    _mixes(xn, prev_ref[...], mu_ref, w1_ref, a1_ref, xm_ref, hw_ref, ha_ref)


def norm_shift_sample(x, prev, g, mu, w1, a1):
    m, d = x.shape
    lr = w1.shape[1]
    return pl.pallas_call(
        _norm_shift_sample_kernel,
        out_shape=[jax.ShapeDtypeStruct((mu.shape[0] - 2, m, d), BF16),
                   jax.ShapeDtypeStruct((m, lr), BF16), jax.ShapeDtypeStruct((m, lr), BF16),
                   jax.ShapeDtypeStruct((m, d), F32)],
        name="norm_shift_sample",
    )(x, prev, g.reshape(1, d), mu, w1, a1)


def _rope(y, cos, sin_a, sin_b):
    half = ROPE_DIM // 2
    step = ROPE_SLAB
    rep = step // cos.shape[1]
    tile = lambda t: jnp.concatenate([t] * rep, axis=1)
    cos_t, sa_t, sb_t = tile(cos), tile(sin_a), tile(sin_b)
    outs = []
    for j in range(y.shape[1] // step):
        ys = y[:, j * step:(j + 1) * step]
        outs.append(ys * cos_t + pltpu.roll(ys, step - half, axis=1) * sa_t + pltpu.roll(ys, half, axis=1) * sb_t)
    return jnp.concatenate(outs, axis=1) if len(outs) > 1 else outs[0]


def _mm_group_kernel(x_ref, w_ref, o_ref):
    o_ref[0] = _dot(x_ref[0], w_ref[0]).astype(o_ref.dtype)


def matmul_groups(x, w, tm, out_dtype, n=None, col=0, tn=None):
    g = w.shape[0]
    n = w.shape[2] if n is None else n
    tn = n if tn is None else tn
    nt = n // tn
    _, m, kdim = x.shape
    return pl.pallas_call(
        _mm_group_kernel,
        grid=(g, nt, m // tm),
        in_specs=[pl.BlockSpec((1, tm, kdim), lambda q, j, i: (q, i, 0)),
                  pl.BlockSpec((1, kdim, tn), lambda q, j, i: (q, 0, col * nt + j))],
        out_specs=pl.BlockSpec((1, tm, tn), lambda q, j, i: (q, i, j)),
        out_shape=jax.ShapeDtypeStruct((g, m, n), out_dtype),
        compiler_params=_cparams(("arbitrary", "arbitrary", "arbitrary")),
        name="matmul_groups",
    )(x, w)


def _mm_rope_kernel(x_ref, w_ref, cos_ref, sa_ref, sb_ref, *o_refs, n_rope, scale):
    y = _dot(x_ref[...], w_ref[...])
    rot = _rope(y[:, :n_rope], cos_ref[...], sa_ref[...], sb_ref[...])
    if scale != 1.0:
        rot = rot * scale
    o_refs[0][...] = rot.astype(o_refs[0].dtype)
    if len(o_refs) > 1:
        o_refs[1][...] = y[:, n_rope:].astype(o_refs[1].dtype)


def matmul_rope(x, w, tables, tm, n_rope, out_dtypes, scale=1.0, n=None):
    m, kdim = x.shape
    n = w.shape[-1] if n is None else n
    w_block = (kdim, n) if w.ndim == 2 else (None, kdim, n)
    lanes = tables[0].shape[1]
    widths = [n_rope] + ([n - n_rope] if n > n_rope else [])
    tab_blocks = tables[0].shape[0] // tm
    tab = pl.BlockSpec((tm, lanes), lambda i: (i % tab_blocks, 0))
    outs = pl.pallas_call(
        functools.partial(_mm_rope_kernel, n_rope=n_rope, scale=scale),
        grid=(m // tm,),
        in_specs=[pl.BlockSpec((tm, kdim), lambda i: (i, 0)),
                  pl.BlockSpec(w_block, lambda i: (0,) * w.ndim, pipeline_mode=pl.Buffered(1)),
                  tab, tab, tab],
        out_specs=[pl.BlockSpec((tm, wd), lambda i: (i, 0)) for wd in widths],
        out_shape=[jax.ShapeDtypeStruct((m, wd), dt) for wd, dt in zip(widths, out_dtypes)],
        compiler_params=_cparams(("arbitrary",)),
        name="matmul_rope",
    )(x, w, *tables)
    return outs


def _mm_res_norm_kernel(x_ref, w_ref, res_ref, g_ref, *out_refs, emit_h):
    h = res_ref[...] + _dot(x_ref[...], w_ref[...])
    hn_refs = out_refs
    if emit_h:
        out_refs[0][...] = h
        hn_refs = out_refs[1:]
    inv = lax.rsqrt(jnp.mean(h * h, axis=-1, keepdims=True) + RMS_EPS)
    for j, hn_ref in enumerate(hn_refs):
        hn_ref[...] = (h * inv * g_ref[j:j + 1, :]).astype(hn_ref.dtype)


def matmul_residual_norm(x, w, res, gains, tm, norm_dtype, emit_h=True):
    m, kdim = x.shape
    n = w.shape[1]
    ng = gains.shape[0]
    row = lambda width: pl.BlockSpec((tm, width), lambda i: (i, 0))
    return pl.pallas_call(
        functools.partial(_mm_res_norm_kernel, emit_h=emit_h),
        grid=(m // tm,),
        in_specs=[row(kdim),
                  pl.BlockSpec((kdim, n), lambda i: (0, 0), pipeline_mode=pl.Buffered(1)),
                  row(n),
                  pl.BlockSpec((ng, n), lambda i: (0, 0))],
        out_specs=[row(n)] * (int(emit_h) + ng),
        out_shape=[jax.ShapeDtypeStruct((m, n), F32)] * int(emit_h) + [jax.ShapeDtypeStruct((m, n), norm_dtype)] * ng,
        compiler_params=_cparams(("arbitrary",)),
        name="matmul_residual_norm",
    )(x, w, res, gains)


def _mm_res_norm_blocks_kernel(x_ref, w_ref, res_ref, *rest, emit_h, head, first_block):
    if head:
        head_ref, g_ref, *out_refs = rest
        res = jnp.where(pl.program_id(1) + first_block == 0, head_ref[...][None], res_ref[...])
    else:
        g_ref, *out_refs = rest
        res = res_ref[...]
    seqs, blk, n = res.shape
    h = res.reshape(seqs * blk, n) + _dot(x_ref[...].reshape(seqs * blk, x_ref.shape[2]), w_ref[...])
    hn_refs = out_refs
    if emit_h:
        out_refs[0][...] = h.reshape(seqs, blk, n)
        hn_refs = out_refs[1:]
    inv = lax.rsqrt(jnp.mean(h * h, axis=-1, keepdims=True) + RMS_EPS)
    for j, hn_ref in enumerate(hn_refs):
        hn_ref[...] = (h * inv * g_ref[j:j + 1, :]).astype(hn_ref.dtype).reshape(seqs, blk, n)


def matmul_residual_norm_blocks(x, w, res, gains, n_batch, norm_dtype, emit_h, head=None, first_block=0):
    kdim = x.shape[1]
    n = w.shape[1]
    ng = gains.shape[0]
    nb_out = x.shape[0] // (n_batch * BLOCK)
    nb = nb_out + first_block
    sp = RESIDUAL_SEQS_PER_STEP
    blocks = lambda width, shift: pl.BlockSpec((sp, BLOCK, width), lambda i, t: (i, t + shift, 0))
    const = lambda shape: pl.BlockSpec(shape, lambda i, t: (0,) * len(shape))
    if head is not None:
        res_specs = [pl.BlockSpec((sp, BLOCK, n), lambda i, t: (i, jnp.maximum(t + first_block - 1, 0), 0)),
                     const((BLOCK, n))]
        res_args = [res, head]
    else:
        res_specs, res_args = [blocks(n, first_block)], [res.reshape(n_batch, nb * BLOCK, n)]
    outs = pl.pallas_call(
        functools.partial(_mm_res_norm_blocks_kernel, emit_h=emit_h, head=head is not None, first_block=first_block),
        grid=(n_batch // sp, nb_out),
        in_specs=[blocks(kdim, 0), pl.BlockSpec((kdim, n), lambda i, t: (0, 0), pipeline_mode=pl.Buffered(1))]
        + res_specs + [const((ng, n))],
        out_specs=[blocks(n, first_block)] * int(emit_h) + [blocks(n, 0)] * ng,
        out_shape=[jax.ShapeDtypeStruct((n_batch, nb * BLOCK, n), F32)] * int(emit_h)
        + [jax.ShapeDtypeStruct((n_batch, nb_out * BLOCK, n), norm_dtype)] * ng,
        compiler_params=_cparams(("arbitrary", "arbitrary"), RESIDUAL_VMEM_LIMIT),
        name="matmul_residual_norm_blocks",
    )(x.reshape(n_batch, nb_out * BLOCK, kdim), w, *res_args, gains)
    return [o.reshape(-1, n) for o in outs]


def _lora_up_kernel(hw_ref, ha_ref, w2_ref, w0_ref, a2_ref, a0_ref, wl_ref, al_ref):
    wl_ref[...] = w0_ref[...] + _dot(hw_ref[...], w2_ref[...])
    al_ref[...] = a0_ref[...] + _dot(ha_ref[...], a2_ref[...])


def lora_up(hw, ha, w2, w0, a2, a0):
    m = hw.shape[0]
    e = w2.shape[1]
    return pl.pallas_call(
        _lora_up_kernel,
        out_shape=[jax.ShapeDtypeStruct((m, e), F32), jax.ShapeDtypeStruct((m, e), F32)],
        name="lora_up",
    )(hw, ha, w2, w0.reshape(1, e), a2, a0.reshape(1, e))


def _seg_sum(x, ones_bd):
    hi = x.astype(BF16)
    outs = []
    for c in range(x.shape[1] // MXU_TILE):
        sl = slice(c * MXU_TILE, (c + 1) * MXU_TILE)
        outs.append(_dot(hi[:, sl], ones_bd))
    return jnp.concatenate(outs, axis=1) if len(outs) > 1 else outs[0]


def _wkv_prompt_kernel(r_ref, k_ref, v_ref, z_ref, hw_ref, ha_ref, w2_ref, w0_ref, a2_ref, a0_ref,
                       kk_ref, ka_ref, rk_ref, gg_ref, gb_ref, *rest, n_cast, cast_every):
    cast_in = rest[:n_cast]
    yg_ref, sout_ref = rest[n_cast:n_cast + 2]
    cast_out = rest[n_cast + 2:2 * n_cast + 2]
    s_ref = rest[-1]
    t_idx = pl.program_id(2)
    c = CHUNK
    hd = HEAD_DIM

    @pl.when(t_idx == 0)
    def _():
        s_ref[...] = jnp.zeros_like(s_ref)

    step = (pl.program_id(0) * pl.num_programs(1) + pl.program_id(1)) * pl.num_programs(2) + t_idx
    _cast_chunks(step, cast_every, cast_in, cast_out)

    tb = r_ref.shape[1]
    nh = HEADS_PER_STREAM
    hw = nh * hd
    nc = tb // c

    li = lax.broadcasted_iota(jnp.int32, (MXU_TILE, MXU_TILE), 0) // hd
    lj = lax.broadcasted_iota(jnp.int32, (MXU_TILE, MXU_TILE), 1) // hd
    ones_bd = jnp.where(li == lj, 1.0, 0.0).astype(BF16)
    bi_ = lax.broadcasted_iota(jnp.int32, (tb, tb), 0)
    bj_ = lax.broadcasted_iota(jnp.int32, (tb, tb), 1)
    tri_incl = jnp.where((bj_ <= bi_) & (bj_ // c == bi_ // c), 1.0, 0.0).astype(BF16)
    ti = lax.broadcasted_iota(jnp.int32, (c, c), 0)
    tj = lax.broadcasted_iota(jnp.int32, (c, c), 1)
    ai = lax.broadcasted_iota(jnp.int32, (c, 2 * c), 0)
    aj = lax.broadcasted_iota(jnp.int32, (c, 2 * c), 1)
    upper = aj >= c
    aj_mod = jnp.where(upper, aj - c, aj)
    masks = dict(
        strict=tj < ti,
        eye=jnp.where(ti == tj, 1.0, 0.0).astype(F32),
        top_k=upper & (aj_mod < ai),
        bot=aj_mod <= ai,
        mean_bd=jnp.where(li == lj, 1.0 / hd, 0.0).astype(BF16))

    for st in range(r_ref.shape[2] // hw):
        _wkv_stream(st, hw, nc, r_ref, k_ref, v_ref, z_ref, hw_ref, ha_ref, w2_ref, w0_ref, a2_ref, a0_ref,
                    kk_ref, ka_ref, rk_ref, gg_ref, gb_ref, yg_ref, s_ref, ones_bd, tri_incl, masks)

    @pl.when(t_idx == pl.num_programs(2) - 1)
    def _():
        sout_ref[0] = s_ref[...]


def _wkv_stream(st, hw, nc, r_ref, k_ref, v_ref, z_ref, hw_ref, ha_ref, w2_ref, w0_ref, a2_ref, a0_ref,
                kk_ref, ka_ref, rk_ref, gg_ref, gb_ref, yg_ref, s_ref, ones_bd, tri_incl, masks):
    c = CHUNK
    hd = HEAD_DIM
    nh = hw // hd
    ls = slice(st * hw, (st + 1) * hw)
    h0 = st * nh
    r = r_ref[0, :, ls]
    k = k_ref[0, :, ls]
    v = v_ref[0, :, ls]
    wl = w0_ref[:, ls] + _dot(hw_ref[...], w2_ref[:, ls])
    al = a0_ref[:, ls] + _dot(ha_ref[...], a2_ref[:, ls])
    a = _sigmoid(al)
    lw = (-DECAY_SCALE * LOG2E) * _sigmoid(wl)
    kk = k * kk_ref[:, ls]
    n2 = _seg_sum(kk * kk, ones_bd)
    kk = kk * lax.rsqrt(jnp.maximum(n2, NORM_FLOOR_SQ))
    ka = ka_ref[:, ls]
    k2 = k * (a * ka + (1.0 - ka))
    bb = kk * a

    lw_hi, lw_lo = _split_hi_lo(lw)
    g = _dot(tri_incl, lw_hi) + _dot(tri_incl, lw_lo)
    mid = lambda ci: g[ci * c + c // 2 - 1:ci * c + c // 2, :]
    gm = jnp.concatenate([jnp.broadcast_to(mid(ci), (c, hw)) for ci in range(nc)], axis=0)
    t = g - gm
    e_a = jnp.exp2(t)
    e_prev = jnp.exp2(t - lw)
    e_inv = jnp.exp2(-t)
    e1 = [jnp.exp2(mid(ci)) for ci in range(nc)]
    e2 = [jnp.exp2(g[ci * c + c - 1:ci * c + c, :] - mid(ci)) for ci in range(nc)]

    kkd = (kk * e_prev).astype(BF16)
    rd = (r * e_a).astype(BF16)
    bi = (bb * e_inv).astype(BF16)
    ki = (k2 * e_inv).astype(BF16)
    v_bf = v.astype(BF16)
    zeros_cv = jnp.zeros((c, hd), BF16)

    pairs = [(ci, h) for ci in range(nc) for h in range(nh)]
    rows = lambda ci: slice(ci * c, (ci + 1) * c)
    cols = lambda h: slice(h * hd, (h + 1) * hd)
    xs = {(ci, h): jnp.concatenate([kkd[rows(ci), cols(h)], rd[rows(ci), cols(h)]], axis=0) for ci, h in pairs}
    r1s = {(ci, h): jnp.concatenate([bi[rows(ci), cols(h)], ki[rows(ci), cols(h)]], axis=0) for ci, h in pairs}
    vs = {(ci, h): v_bf[rows(ci), cols(h)] for ci, h in pairs}
    a_mats = {p: _dot_nt(xs[p], r1s[p]) for p in pairs}
    lk_vs = {p: _dot(jnp.where(masks["top_k"], a_mats[p][:c, :], 0.0).astype(BF16),
                     jnp.concatenate([zeros_cv, vs[p]], axis=0)) for p in pairs}
    lps = {p: jnp.where(masks["strict"], a_mats[p][:c, :c], 0.0) for p in pairs}
    ts = {p: masks["eye"] - lps[p] for p in pairs}
    for _ in range(int(math.log2(c)) - 1):
        lpb = {p: lps[p].astype(BF16) for p in pairs}
        lps = {p: _dot(lpb[p], lpb[p]) for p in pairs}
        ts = {p: _dot(ts[p].astype(BF16), (masks["eye"] + lps[p]).astype(BF16)) for p in pairs}
    a_bots = {p: jnp.where(masks["bot"], a_mats[p][c:, :], 0.0).astype(BF16) for p in pairs}
    t_bf = {p: ts[p].astype(BF16) for p in pairs}

    state = [s_ref[h0 + h] for h in range(nh)]
    y_rows = []
    for ci in range(nc):
        hs = range(nh)
        sms = [state[h] * e1[ci][:, cols(h)] for h in hs]
        p_mats = [_dot_nt(xs[ci, h], sms[h].astype(BF16)) for h in hs]
        us = [-_dot(t_bf[ci, h], (p_mats[h][:c, :] + lk_vs[ci, h]).astype(BF16)) for h in hs]
        uvs = [jnp.concatenate([us[h].astype(BF16), vs[ci, h]], axis=0) for h in hs]
        ys = [p_mats[h][c:, :] + _dot(a_bots[ci, h], uvs[h]) for h in hs]
        state = [(sms[h] + _dot_tn(uvs[h], r1s[ci, h])) * e2[ci][:, cols(h)] for h in hs]
        y_rows.append(jnp.concatenate(ys, axis=1))
    for h in range(nh):
        s_ref[h0 + h] = state[h]
    y = jnp.concatenate(y_rows, axis=0) if nc > 1 else y_rows[0]

    mean = _seg_sum(y, masks["mean_bd"])
    yc = y - mean
    var = _seg_sum(yc * yc, masks["mean_bd"])
    yn = yc * lax.rsqrt(var + GN_EPS) * gg_ref[:, ls] + gb_ref[:, ls]
    bonus = _seg_sum(r * k2 * rk_ref[:, ls], ones_bd)
    yg_ref[:, ls] = ((yn + bonus * v) * _silu(z_ref[0, :, ls])).astype(BF16)


def wkv_prompt(rkvz, hw_act, ha_act, w2, w0, a2, a0, k_k, k_a, r_k, gn_g, gn_b, n_batch, cast_weights=()):
    _, m, e = rkvz.shape
    p = m // n_batch
    tb = WKV_ROWS
    nt = p // tb
    hw = HEADS_PER_STEP * HEAD_DIM
    nh = e // HEAD_DIM
    row = lambda i, g, t: (i * nt + t, g)
    proj = lambda q: pl.BlockSpec((1, tb, hw), lambda i, g, t: (q, i * nt + t, g))
    par = pl.BlockSpec((1, hw), lambda i, g, t: (0, g))
    lr = w2.shape[0]
    hid = pl.BlockSpec((tb, lr), lambda i, g, t: (i * nt + t, 0))
    up = pl.BlockSpec((lr, hw), lambda i, g, t: (0, g))
    n_groups = e // hw
    cast_specs, every = _cast_job(cast_weights, n_batch * n_groups * nt, lambda i, g, t: (i * n_groups + g) * nt + t)
    return pl.pallas_call(
        functools.partial(_wkv_prompt_kernel, n_cast=len(cast_weights), cast_every=every),
        grid=(n_batch, n_groups, nt),
        in_specs=[proj(0), proj(1), proj(2), proj(3), hid, hid, up, par, up, par,
                  par, par, par, par, par] + cast_specs,
        out_specs=[pl.BlockSpec((tb, hw), row),
                   pl.BlockSpec((1, HEADS_PER_STEP, HEAD_DIM, HEAD_DIM), lambda i, g, t: (i, g, 0, 0))] + cast_specs,
        out_shape=[jax.ShapeDtypeStruct((m, e), BF16),
                   jax.ShapeDtypeStruct((n_batch, nh, HEAD_DIM, HEAD_DIM), F32)]
        + [jax.ShapeDtypeStruct(w.shape, BF16) for w in cast_weights],
        scratch_shapes=[pltpu.VMEM((HEADS_PER_STEP, HEAD_DIM, HEAD_DIM), F32)],
        compiler_params=_cparams(("arbitrary", "arbitrary", "arbitrary")),
        name="wkv_prompt",
    )(rkvz, rkvz, rkvz, rkvz, hw_act, ha_act, w2, w0.reshape(1, e), a2, a0.reshape(1, e),
      k_k.reshape(1, e), k_a.reshape(1, e), r_k.reshape(1, e), gn_g.reshape(1, e), gn_b.reshape(1, e),
      *cast_weights)


def _wkv_sample_kernel(r_ref, k_ref, v_ref, z_ref, wl_ref, al_ref, kk_ref, ka_ref, rk_ref, gg_ref, gb_ref,
                       s_ref, yg_ref, sout_ref, y_scr):
    hd = HEAD_DIM
    r = r_ref[0]
    k = k_ref[0]
    v = v_ref[0]
    a = _sigmoid(al_ref[0])
    d = jnp.exp2((-DECAY_SCALE * LOG2E) * _sigmoid(wl_ref[0]))
    kk = k * kk_ref[...]
    kk = kk / jnp.maximum(jnp.sqrt(jnp.sum(kk * kk, axis=-1, keepdims=True)), 1e-12)
    k2 = k * (1.0 + (a - 1.0) * ka_ref[...])
    bb = kk * a
    nh = r.shape[0]
    ii = lax.broadcasted_iota(jnp.int32, (hd, hd), 0)
    jj = lax.broadcasted_iota(jnp.int32, (hd, hd), 1)
    eye = ii == jj

    row = lambda x, h: x[h:h + 1, :]
    group = 8
    for h0 in range(0, nh, group):
        hs = range(h0, h0 + group)
        s = {h: s_ref[0, h] for h in hs}
        sa = {h: jnp.sum(s[h] * row(kk, h), axis=-1, keepdims=True) for h in hs}
        v_col = {h: jnp.sum(jnp.where(eye, row(v, h), 0.0), axis=-1, keepdims=True) for h in hs}
        s_new = {h: s[h] * row(d, h) - sa[h] * row(bb, h) + v_col[h] * row(k2, h) for h in hs}
        y_col = {h: jnp.sum(s_new[h] * row(r, h), axis=-1, keepdims=True) for h in hs}
        for h in hs:
            sout_ref[0, h] = s_new[h]
            y_scr[h:h + 1, :] = jnp.sum(jnp.where(eye, y_col[h], 0.0), axis=0, keepdims=True)

    y = y_scr[...]
    mean = jnp.mean(y, axis=-1, keepdims=True)
    yc = y - mean
    var = jnp.mean(yc * yc, axis=-1, keepdims=True)
    yn = yc * lax.rsqrt(var + GN_EPS) * gg_ref[...] + gb_ref[...]
    bonus = jnp.sum(r * k2 * rk_ref[...], axis=-1, keepdims=True)
    yg_ref[0] = ((yn + bonus * v) * _silu(z_ref[0])).astype(BF16)


def wkv_sample(rkvz, wl, al, k_k, k_a, r_k, gn_g, gn_b, state):
    _, m, e = rkvz.shape
    nh = e // HEAD_DIM
    hd = HEAD_DIM
    rkvz4 = rkvz.reshape(4, m, nh, hd)
    proj = lambda q: pl.BlockSpec((None, 1, nh, hd), lambda i: (q, i, 0, 0))
    tok = pl.BlockSpec((1, nh, hd), lambda i: (i, 0, 0))
    par = pl.BlockSpec((nh, hd), lambda i: (0, 0))
    st = pl.BlockSpec((1, nh, hd, hd), lambda i: (i, 0, 0, 0))
    as_heads = lambda x: x.reshape(nh, hd)
    return pl.pallas_call(
        _wkv_sample_kernel,
        grid=(m,),
        in_specs=[proj(0), proj(1), proj(2), proj(3), tok, tok, par, par, par, par, par, st],
        out_specs=[tok, st],
        out_shape=[jax.ShapeDtypeStruct((m, nh, hd), BF16), jax.ShapeDtypeStruct(state.shape, F32)],
        scratch_shapes=[pltpu.VMEM((nh, hd), F32)],
        compiler_params=_cparams(("arbitrary",)),
        name="wkv_sample",
    )(rkvz4, rkvz4, rkvz4, rkvz4, wl.reshape(m, nh, hd), al.reshape(m, nh, hd),
      as_heads(k_k), as_heads(k_a), as_heads(r_k), as_heads(gn_g), as_heads(gn_b), state)


def rope_tables(pos):
    half = ROPE_DIM // 2
    inv_freq = ROPE_THETA ** (-jnp.arange(half, dtype=F32) * 2.0 / ROPE_DIM)
    ang = pos.astype(F32)[:, None] * inv_freq[None, :]
    cos = jnp.cos(ang)
    sin = jnp.sin(ang)
    rows = pos.shape[0]
    ones = jnp.ones((rows, HEAD_DIM - ROPE_DIM), F32)
    zeros_h = jnp.zeros((rows, half), F32)
    zeros_r = jnp.zeros((rows, HEAD_DIM - ROPE_DIM), F32)
    cos_h = jnp.concatenate([cos, cos, ones], axis=1)
    sa_h = jnp.concatenate([-sin, zeros_h, zeros_r], axis=1)
    sb_h = jnp.concatenate([zeros_h, sin, zeros_r], axis=1)
    two = lambda t: jnp.concatenate([t, t], axis=1)
    return two(cos_h), two(sa_h), two(sb_h)


def _attn_prompt_kernel(sink_ref, q_ref, kc_ref, kp_ref, vc_ref, vp_ref, z_ref, o_ref, *, first_block):
    n = pl.program_id(1) + first_block
    hd = HEAD_DIM
    seqs, blk, _ = q_ref.shape
    n_kv = kc_ref.shape[2] // hd
    grp = q_ref.shape[2] // (n_kv * hd)
    qi = lax.broadcasted_iota(jnp.int32, (blk, 2 * blk), 0)
    kj = lax.broadcasted_iota(jnp.int32, (blk, 2 * blk), 1) - blk
    kpos = n * blk + kj
    diff = qi - kj
    valid = (kpos >= LEAD) & (diff >= 0) & (diff <= WINDOW)
    for sq in range(seqs):
        k_all = jnp.concatenate([kp_ref[sq], kc_ref[sq]], axis=0).astype(BF16)
        v_all = jnp.concatenate([vp_ref[sq], vc_ref[sq]], axis=0).astype(BF16)

        def scores(h):
            k_h = k_all[:, h * hd:(h + 1) * hd]
            return [_dot_nt(q_ref[sq, :, (h * grp + gi) * hd:(h * grp + gi + 1) * hd], k_h) for gi in range(grp)]

        outs = []
        s_next = scores(0)
        for h in range(n_kv):
            s_cur = s_next
            if h + 1 < n_kv:
                s_next = scores(h + 1)
            v_h = v_all[:, h * hd:(h + 1) * hd]
            for g0 in range(0, grp, SOFTMAX_BATCH):
                gs = range(g0, min(g0 + SOFTMAX_BATCH, grp))
                sks = {gi: sink_ref[h * grp + gi] * LOG2E for gi in gs}
                ss = {gi: jnp.where(valid, s_cur[gi], -jnp.inf) for gi in gs}
                ms = {gi: jnp.maximum(jnp.max(ss[gi], axis=-1, keepdims=True), sks[gi]) for gi in gs}
                ps = {gi: jnp.exp2(ss[gi] - ms[gi]) for gi in gs}
                dens = {gi: jnp.sum(ps[gi], axis=-1, keepdims=True) + jnp.exp2(sks[gi] - ms[gi]) for gi in gs}
                outs += [_dot(ps[gi].astype(BF16), v_h) / dens[gi] for gi in gs]
        att = jnp.concatenate(outs, axis=1)
        o_ref[sq] = (att * _silu(z_ref[sq])).astype(o_ref.dtype)


def attn_prompt(sinks, q, k, v, z, n_batch, first_block):
    m, e = q.shape
    nb = m // (n_batch * BLOCK)
    nb_out = nb - first_block
    kw = k.shape[1]
    sp = SEQS_PER_STEP
    cur = lambda i, n: (i, n + first_block, 0)
    prv = lambda i, n: (i, jnp.maximum(n + first_block - 1, 0), 0)
    seq_rows = lambda t: t.reshape(n_batch, nb * BLOCK, t.shape[1])
    q, k, v, z = seq_rows(q), seq_rows(k), seq_rows(v), seq_rows(z)
    return pl.pallas_call(
        functools.partial(_attn_prompt_kernel, first_block=first_block),
        grid=(n_batch // sp, nb_out),
        in_specs=[pl.BlockSpec(memory_space=pltpu.SMEM),
                  pl.BlockSpec((sp, BLOCK, e), cur),
                  pl.BlockSpec((sp, BLOCK, kw), cur), pl.BlockSpec((sp, BLOCK, kw), prv),
                  pl.BlockSpec((sp, BLOCK, kw), cur), pl.BlockSpec((sp, BLOCK, kw), prv),
                  pl.BlockSpec((sp, BLOCK, e), cur)],
        out_specs=pl.BlockSpec((sp, BLOCK, e), lambda i, n: (i, n, 0)),
        out_shape=jax.ShapeDtypeStruct((n_batch, nb_out * BLOCK, e), BF16),
        compiler_params=_cparams(("arbitrary", "arbitrary")),
        name="attn_prompt",
    )(sinks, q, k, k, v, v, z).reshape(n_batch * nb_out * BLOCK, e)


def _attn_sample_kernel(sink_ref, q_ref, kc_ref, vc_ref, kn_ref, vn_ref, z_ref, o_ref, ko_ref, vo_ref):
    hd = HEAD_DIM
    win = kc_ref.shape[1]
    n_kv = kc_ref.shape[2] // hd
    nq = q_ref.shape[1]
    grp = nq // n_kv
    pad = 8
    kc = kc_ref[0]
    vc = vc_ref[0]
    kn = kn_ref[0]
    vn = vn_ref[0]
    first = lax.broadcasted_iota(jnp.int32, (pad, kc.shape[1]), 0) == 0
    k_all = jnp.concatenate([kc, jnp.where(first, kn, 0.0)], axis=0).astype(BF16)
    v_all = jnp.concatenate([vc, jnp.where(first, vn, 0.0)], axis=0).astype(BF16)
    col = lax.broadcasted_iota(jnp.int32, (grp, win + pad), 1)
    valid = (col <= win) & (win - col <= WINDOW)
    q = q_ref[0].astype(BF16)
    row_i = lax.broadcasted_iota(jnp.int32, (grp, 1), 0)
    hs = range(n_kv)
    sks = []
    for h in hs:
        sk = jnp.zeros((grp, 1), F32)
        for gi in range(grp):
            sk = jnp.where(row_i == gi, sink_ref[h * grp + gi] * LOG2E, sk)
        sks.append(sk)
    ss = [jnp.where(valid, _dot_nt(q[h * grp:(h + 1) * grp, :], k_all[:, h * hd:(h + 1) * hd]), -jnp.inf)
          for h in hs]
    ms = [jnp.maximum(jnp.max(ss[h], axis=-1, keepdims=True), sks[h]) for h in hs]
    ps = [jnp.exp2(ss[h] - ms[h]) for h in hs]
    dens = [jnp.sum(ps[h], axis=-1, keepdims=True) + jnp.exp2(sks[h] - ms[h]) for h in hs]
    outs = [_dot(ps[h].astype(BF16), v_all[:, h * hd:(h + 1) * hd]) / dens[h] for h in hs]
    att = jnp.concatenate(outs, axis=0)
    o_ref[0] = (att * _silu(z_ref[0])).astype(o_ref.dtype)
    last = lax.broadcasted_iota(jnp.int32, kc.shape, 0) == win - 1
    ko_ref[0] = jnp.where(last, kn, pltpu.roll(kc, win - 1, axis=0))
    vo_ref[0] = jnp.where(last, vn, pltpu.roll(vc, win - 1, axis=0))


def attn_sample(sinks, q, cache_k, cache_v, k_new, v_new, z):
    m, win, kw = cache_k.shape
    nq = q.shape[1]
    hd = HEAD_DIM
    tok = pl.BlockSpec((1, nq, hd), lambda i: (i, 0, 0))
    cache = pl.BlockSpec((1, win, kw), lambda i: (i, 0, 0))
    new = pl.BlockSpec((1, 1, kw), lambda i: (i, 0, 0))
    return pl.pallas_call(
        _attn_sample_kernel,
        grid=(m,),
        in_specs=[pl.BlockSpec(memory_space=pltpu.SMEM), tok, cache, cache, new, new, tok],
        out_specs=[tok, cache, cache],
        out_shape=[jax.ShapeDtypeStruct((m, nq, hd), BF16),
                   jax.ShapeDtypeStruct(cache_k.shape, F32), jax.ShapeDtypeStruct(cache_v.shape, F32)],
        compiler_params=_cparams(("arbitrary",)),
        name="attn_sample",
    )(sinks, q, cache_k, cache_v, k_new, v_new, z)


def _pad_lora(w_down, w_up):
    r = w_down.shape[1]
    return (jnp.pad(w_down, ((0, 0), (0, LORA_PAD - r))).astype(BF16),
            jnp.pad(w_up, ((0, LORA_PAD - r), (0, 0))).astype(BF16))


def kernel(x_prompt, x_sample, state_wkv, state_shift, cache_k, cache_v, meta_tokens, a_norm, a_mu, a_w_rkvz,
           a_w0, a_w1, a_w2, a_a0, a_a1, a_a2, a_k_k, a_k_a, a_r_k, a_gn_g, a_gn_b, a_w_out, kv_norm, w_kv,
           b_norm, b_w_qz, b_sinks, b_w_o, final_norm):
    nb, seq, d = x_prompt.shape
    db, dseq, _ = x_sample.shape
    assert dseq == 1 and a_norm.shape[0] == 1 and b_norm.shape[0] == 1
    e = a_w_rkvz.shape[3]
    win = cache_k.shape[1]
    p_len = LEAD + N_META + seq
    assert p_len % BLOCK == 0 and (LEAD + N_META) == BLOCK
    kvw = N_KV_HEADS * HEAD_DIM

    w1, w2 = _pad_lora(a_w1[0], a_w2[0])
    a1, a2 = _pad_lora(a_a1[0], a_a2[0])
    mu = a_mu[0]
    sinks = b_sinks[0]
    gains_b = jnp.stack([kv_norm, b_norm[0]])

    tm = p_len // 8

    head = jnp.concatenate([jnp.zeros((LEAD, d), F32), meta_tokens], axis=0)
    w4 = a_w_rkvz[0]
    xm, hw_p, ha_p, x_last, w_rkvz = norm_shift_prompt(x_prompt, head, a_norm[0], mu, w1, a1,
                                                      cast_weights=(w4.reshape(-1, w4.shape[2]),))
    w_rkvz = w_rkvz.reshape(w4.shape)
    p_state_shift = x_last.reshape(1, nb, d)
    rkvz = matmul_groups(xm, w_rkvz, 4 * tm, F32, tn=e // 2)
    yg, p_state, w_out, w_kv_bf, w_qz, w_o = wkv_prompt(
        rkvz, hw_p, ha_p, w2, a_w0[0], a2, a_a0[0], a_k_k[0], a_k_a[0], a_r_k[0].reshape(-1), a_gn_g[0],
        a_gn_b[0], nb, cast_weights=(a_w_out[0], w_kv, b_w_qz[0], b_w_o[0]))
    w_qz = w_qz[None]
    hp, hn_kv, hn_b = matmul_residual_norm_blocks(yg, w_out, x_prompt, gains_b, nb, BF16, True, head=head)

    pos_p = jnp.maximum(jnp.arange(p_len, dtype=jnp.int32) - LEAD, 0)
    tabs_p = rope_tables(pos_p)
    k_p, v_p = matmul_rope(hn_kv, w_kv_bf, tabs_p, 4 * tm, kvw, (F32, F32))
    q_p, = matmul_rope(hn_b, w_qz, tabs_p, 2 * tm, e, (BF16,), scale=Q_SCALE, n=e)
    z_p = matmul_groups(hn_b[None], w_qz, 4 * tm, F32, n=e, col=1, tn=e // 2)[0]
    skip = (LEAD + N_META) // BLOCK
    att = attn_prompt(sinks, q_p, k_p, v_p, z_p, nb, skip)
    y_prompt, = matmul_residual_norm_blocks(att, w_o, hp, final_norm[None], nb, F32, False, first_block=skip)
    y_prompt = y_prompt.reshape(nb, seq, d)
    tail = lambda t: t.reshape(nb, p_len, kvw)[:, -win:].reshape(nb, win, N_KV_HEADS, HEAD_DIM)
    p_cache_k = tail(k_p)
    p_cache_v = tail(v_p)

    hs = x_sample.reshape(db, d)
    xm_s, hw_s, ha_s, xn_s = norm_shift_sample(hs, state_shift[0], a_norm[0], mu, w1, a1)
    rkvz_s = matmul_groups(xm_s, w_rkvz, db, F32, tn=e // 4)
    wl_s, al_s = lora_up(hw_s, ha_s, w2, a_w0[0], a2, a_a0[0])
    yg_s, s_state = wkv_sample(rkvz_s, wl_s, al_s, a_k_k[0], a_k_a[0], a_r_k[0].reshape(-1), a_gn_g[0],
                               a_gn_b[0], state_wkv[0])
    hs, hn_kv_s, hn_b_s = matmul_residual_norm(yg_s.reshape(db, e), w_out, hs, gains_b, db, BF16)
    tabs_s = rope_tables(jnp.full((db,), PAST_LEN, jnp.int32))
    k_s, v_s = matmul_rope(hn_kv_s, w_kv_bf, tabs_s, db, kvw, (F32, F32))
    q_s, = matmul_rope(hn_b_s, w_qz, tabs_s, db, e, (F32,), scale=Q_SCALE, n=e)
    z_s = matmul_groups(hn_b_s[None], w_qz, db, F32, n=e, col=1, tn=e // 4)[0]
    nq = e // HEAD_DIM
    att_s, s_cache_k, s_cache_v = attn_sample(
        sinks, q_s.reshape(db, nq, HEAD_DIM), cache_k.reshape(db, win, kvw), cache_v.reshape(db, win, kvw),
        k_s.reshape(db, 1, kvw), v_s.reshape(db, 1, kvw), z_s.reshape(db, nq, HEAD_DIM))
    y_s, = matmul_residual_norm(att_s.reshape(db, e), w_o, hs, final_norm[None], db, F32, emit_h=False)
    y_sample = y_s.reshape(db, 1, d)

    return (y_prompt, y_sample, p_state[None], p_state_shift,
            p_cache_k, p_cache_v,
            s_state[None], xn_s[None],
            s_cache_k.reshape(cache_k.shape), s_cache_v.reshape(cache_v.shape))
```

```python
import functools
import math

import jax
import jax.numpy as jnp
from jax import lax
from jax.experimental import pallas as pl
from jax.experimental.pallas import tpu as pltpu

F32 = jnp.float32
BF16 = jnp.bfloat16

HEAD_DIM = 64
N_KV_HEADS = 8
WINDOW = 128
BLOCK = 128
ROPE_DIM = HEAD_DIM // 4
ROPE_THETA = 500000.0
N_META = 16
PAST_LEN = 16384
RMS_EPS = 1e-6
GN_EPS = 64e-5
NORM_FLOOR_SQ = 1e-24
LEAD = (-N_META) % BLOCK
CHUNK = 64
WKV_ROWS = 128
CAST_CHUNKS = 32
SEQS_PER_STEP = 2
RESIDUAL_SEQS_PER_STEP = 4
HEADS_PER_STREAM = 32
HEADS_PER_STEP = 32
LORA_PAD = 128
MXU_TILE = 256
ROPE_SLAB = 512
VMEM_LIMIT = 48 * 1024 * 1024
RESIDUAL_VMEM_LIMIT = 56 * 1024 * 1024
LOG2E = 1.0 / math.log(2.0)
DECAY_SCALE = math.exp(-0.5)
SOFTMAX_BATCH = 4
Q_SCALE = HEAD_DIM ** -0.5 * LOG2E


def _cparams(sem, vmem_limit=VMEM_LIMIT):
    return pltpu.CompilerParams(dimension_semantics=sem, vmem_limit_bytes=vmem_limit)


def _sigmoid(x):
    return 1.0 / (1.0 + jnp.exp2(x * (-LOG2E)))


def _silu(x):
    return x * _sigmoid(x)


def _dot(a, b):
    return jnp.dot(a, b, preferred_element_type=F32)


def _dot_nt(a, b):
    return lax.dot_general(a, b, (((1,), (1,)), ((), ())), preferred_element_type=F32)


def _dot_tn(a, b):
    return lax.dot_general(a, b, (((0,), (0,)), ((), ())), preferred_element_type=F32)


def _split_hi_lo(x):
    hi = x.astype(BF16)
    lo = (x - hi.astype(F32)).astype(BF16)
    return hi, lo


def _cast_job(weights, n_steps, step_of):
    n_chunks = min(CAST_CHUNKS, n_steps)
    every = n_steps // n_chunks
    assert all(w.shape[0] % (n_chunks * 16) == 0 for w in weights)
    chunk = lambda *idx: (jnp.minimum(step_of(*idx) // every, n_chunks - 1), 0)
    return [pl.BlockSpec((w.shape[0] // n_chunks, w.shape[1]), chunk) for w in weights], every


def _cast_chunks(step, every, srcs, dsts):
    @pl.when(step % every == 0)
    def _():
        for src, dst in zip(srcs, dsts):
            dst[...] = src[...].astype(dst.dtype)


def _mixes(xn, prev, mu_ref, w1_ref, a1_ref, xm_ref, hw_ref, ha_ref):
    xx = prev - xn
    n_proj = xm_ref.shape[0]
    for p in range(n_proj):
        xm_ref[p] = (xn + xx * mu_ref[p:p + 1, :]).astype(xm_ref.dtype)
    xw = (xn + xx * mu_ref[n_proj:n_proj + 1, :]).astype(BF16)
    xa = (xn + xx * mu_ref[n_proj + 1:n_proj + 2, :]).astype(BF16)
    hw_ref[...] = jnp.tanh(_dot(xw, w1_ref[...])).astype(hw_ref.dtype)
    ha_ref[...] = _dot(xa, a1_ref[...]).astype(ha_ref.dtype)


def _norm_shift_kernel(x_ref, head_ref, g_ref, mu_ref, w1_ref, a1_ref, *rest, n_cast, cast_every):
    cast_in = rest[:n_cast]
    xm_ref, hw_ref, ha_ref, last_ref = rest[n_cast:n_cast + 4]
    cast_out = rest[n_cast + 4:2 * n_cast + 4]
    carry_ref = rest[-1]

    @pl.when(pl.program_id(1) == 0)
    def _():
        carry_ref[...] = jnp.zeros_like(carry_ref)

    _cast_chunks(pl.program_id(0) * pl.num_programs(1) + pl.program_id(1), cast_every, cast_in, cast_out)

    is_head = pl.program_id(1) == 0
    for sq in range(x_ref.shape[0]):
        x = jnp.where(is_head, head_ref[...], x_ref[sq])
        tm = x.shape[0]
        xn = x * lax.rsqrt(jnp.mean(x * x, axis=-1, keepdims=True) + RMS_EPS) * g_ref[...]
        rolled = pltpu.roll(xn, 1, axis=0)
        row = lax.broadcasted_iota(jnp.int32, xn.shape, 0)
        prev = jnp.where(row == 0, carry_ref[sq, 0:1, :], rolled)
        _mixes(xn, prev, mu_ref, w1_ref, a1_ref, xm_ref.at[:, sq], hw_ref.at[sq], ha_ref.at[sq])
        carry_ref[sq, 0:1, :] = xn[tm - 1:tm, :]
        last_ref[sq] = xn[tm - 1:tm, :]


def norm_shift_prompt(x, head, g, mu, w1, a1, cast_weights=()):
    b, seq, d = x.shape
    tm = BLOCK
    p = tm + seq
    n_mix = mu.shape[0]
    n_proj = n_mix - 2
    lr = w1.shape[1]
    nt = p // tm
    sp = SEQS_PER_STEP
    const = lambda shape: pl.BlockSpec(shape, lambda i, t: (0,) * len(shape))
    hid = pl.BlockSpec((sp, tm, lr), lambda i, t: (i, t, 0))
    cast_specs, every = _cast_job(cast_weights, (b // sp) * nt, lambda i, t: i * nt + t)
    xm, hw_act, ha_act, last, *casts = pl.pallas_call(
        functools.partial(_norm_shift_kernel, n_cast=len(cast_weights), cast_every=every),
        grid=(b // sp, nt),
        in_specs=[pl.BlockSpec((sp, tm, d), lambda i, t: (i, jnp.maximum(t - 1, 0), 0)),
                  const((tm, d)), const((1, d)), const((n_mix, d)), const((d, lr)), const((d, lr))] + cast_specs,
        out_specs=[pl.BlockSpec((n_proj, sp, tm, d), lambda i, t: (0, i, t, 0)), hid, hid,
                   pl.BlockSpec((sp, 1, d), lambda i, t: (i, 0, 0))] + cast_specs,
        out_shape=[jax.ShapeDtypeStruct((n_proj, b, p, d), BF16),
                   jax.ShapeDtypeStruct((b, p, lr), BF16), jax.ShapeDtypeStruct((b, p, lr), BF16),
                   jax.ShapeDtypeStruct((b, 1, d), F32)] + [jax.ShapeDtypeStruct(w.shape, BF16) for w in cast_weights],
        scratch_shapes=[pltpu.VMEM((sp, 8, d), F32)],
        compiler_params=_cparams(("arbitrary", "arbitrary")),
        name="norm_shift_prompt",
    )(x, head, g.reshape(1, d), mu, w1, a1, *cast_weights)
    return (xm.reshape(n_proj, b * p, d), hw_act.reshape(b * p, lr), ha_act.reshape(b * p, lr), last, *casts)


def _norm_shift_sample_kernel(x_ref, prev_ref, g_ref, mu_ref, w1_ref, a1_ref, xm_ref, hw_ref, ha_ref, xn_ref):
    x = x_ref[...]
    xn = x * lax.rsqrt(jnp.mean(x * x, axis=-1, keepdims=True) + RMS_EPS) * g_ref[...]
    xn_ref[...] = xn
    _mixes(xn, prev_ref[...], mu_ref, w1_ref, a1_ref, xm_ref, hw_ref, ha_ref)


def norm_shift_sample(x, prev, g, mu, w1, a1):
    m, d = x.shape
    lr = w1.shape[1]
    return pl.pallas_call(
        _norm_shift_sample_kernel,
        out_shape=[jax.ShapeDtypeStruct((mu.shape[0] - 2, m, d), BF16),
                   jax.ShapeDtypeStruct((m, lr), BF16), jax.ShapeDtypeStruct((m, lr), BF16),
                   jax.ShapeDtypeStruct((m, d), F32)],
        name="norm_shift_sample",
    )(x, prev, g.reshape(1, d), mu, w1, a1)


def _rope(y, cos, sin_a, sin_b):
    half = ROPE_DIM // 2
    step = ROPE_SLAB
    rep = step // cos.shape[1]
    tile = lambda t: jnp.concatenate([t] * rep, axis=1)
    cos_t, sa_t, sb_t = tile(cos), tile(sin_a), tile(sin_b)
    outs = []
    for j in range(y.shape[1] // step):
        ys = y[:, j * step:(j + 1) * step]
        outs.append(ys * cos_t + pltpu.roll(ys, step - half, axis=1) * sa_t + pltpu.roll(ys, half, axis=1) * sb_t)
    return jnp.concatenate(outs, axis=1) if len(outs) > 1 else outs[0]


def _mm_group_kernel(x_ref, w_ref, o_ref):
    o_ref[0] = _dot(x_ref[0], w_ref[0]).astype(o_ref.dtype)


def matmul_groups(x, w, tm, out_dtype, n=None, col=0, tn=None):
    g = w.shape[0]
    n = w.shape[2] if n is None else n
    tn = n if tn is None else tn
    nt = n // tn
    _, m, kdim = x.shape
    return pl.pallas_call(
        _mm_group_kernel,
        grid=(g, nt, m // tm),
        in_specs=[pl.BlockSpec((1, tm, kdim), lambda q, j, i: (q, i, 0)),
                  pl.BlockSpec((1, kdim, tn), lambda q, j, i: (q, 0, col * nt + j))],
        out_specs=pl.BlockSpec((1, tm, tn), lambda q, j, i: (q, i, j)),
        out_shape=jax.ShapeDtypeStruct((g, m, n), out_dtype),
        compiler_params=_cparams(("arbitrary", "arbitrary", "arbitrary")),
        name="matmul_groups",
    )(x, w)


def _mm_rope_kernel(x_ref, w_ref, cos_ref, sa_ref, sb_ref, *o_refs, n_rope, scale):
    y = _dot(x_ref[...], w_ref[...])
    rot = _rope(y[:, :n_rope], cos_ref[...], sa_ref[...], sb_ref[...])
    if scale != 1.0:
        rot = rot * scale
    o_refs[0][...] = rot.astype(o_refs[0].dtype)
    if len(o_refs) > 1:
        o_refs[1][...] = y[:, n_rope:].astype(o_refs[1].dtype)


def matmul_rope(x, w, tables, tm, n_rope, out_dtypes, scale=1.0, n=None):
    m, kdim = x.shape
    n = w.shape[-1] if n is None else n
    w_block = (kdim, n) if w.ndim == 2 else (None, kdim, n)
    lanes = tables[0].shape[1]
    widths = [n_rope] + ([n - n_rope] if n > n_rope else [])
    tab_blocks = tables[0].shape[0] // tm
    tab = pl.BlockSpec((tm, lanes), lambda i: (i % tab_blocks, 0))
    outs = pl.pallas_call(
        functools.partial(_mm_rope_kernel, n_rope=n_rope, scale=scale),
        grid=(m // tm,),
        in_specs=[pl.BlockSpec((tm, kdim), lambda i: (i, 0)),
                  pl.BlockSpec(w_block, lambda i: (0,) * w.ndim, pipeline_mode=pl.Buffered(1)),
                  tab, tab, tab],
        out_specs=[pl.BlockSpec((tm, wd), lambda i: (i, 0)) for wd in widths],
        out_shape=[jax.ShapeDtypeStruct((m, wd), dt) for wd, dt in zip(widths, out_dtypes)],
        compiler_params=_cparams(("arbitrary",)),
        name="matmul_rope",
    )(x, w, *tables)
    return outs


def _mm_res_norm_kernel(x_ref, w_ref, res_ref, g_ref, *out_refs, emit_h):
    h = res_ref[...] + _dot(x_ref[...], w_ref[...])
    hn_refs = out_refs
    if emit_h:
        out_refs[0][...] = h
        hn_refs = out_refs[1:]
    inv = lax.rsqrt(jnp.mean(h * h, axis=-1, keepdims=True) + RMS_EPS)
    for j, hn_ref in enumerate(hn_refs):
        hn_ref[...] = (h * inv * g_ref[j:j + 1, :]).astype(hn_ref.dtype)


def matmul_residual_norm(x, w, res, gains, tm, norm_dtype, emit_h=True):
    m, kdim = x.shape
    n = w.shape[1]
    ng = gains.shape[0]
    row = lambda width: pl.BlockSpec((tm, width), lambda i: (i, 0))
    return pl.pallas_call(
        functools.partial(_mm_res_norm_kernel, emit_h=emit_h),
        grid=(m // tm,),
        in_specs=[row(kdim),
                  pl.BlockSpec((kdim, n), lambda i: (0, 0), pipeline_mode=pl.Buffered(1)),
                  row(n),
                  pl.BlockSpec((ng, n), lambda i: (0, 0))],
        out_specs=[row(n)] * (int(emit_h) + ng),
        out_shape=[jax.ShapeDtypeStruct((m, n), F32)] * int(emit_h) + [jax.ShapeDtypeStruct((m, n), norm_dtype)] * ng,
        compiler_params=_cparams(("arbitrary",)),
        name="matmul_residual_norm",
    )(x, w, res, gains)


def _mm_res_norm_blocks_kernel(x_ref, w_ref, res_ref, *rest, emit_h, head, first_block):
    if head:
        head_ref, g_ref, *out_refs = rest
        res = jnp.where(pl.program_id(1) + first_block == 0, head_ref[...][None], res_ref[...])
    else:
        g_ref, *out_refs = rest
        res = res_ref[...]
    seqs, blk, n = res.shape
    h = res.reshape(seqs * blk, n) + _dot(x_ref[...].reshape(seqs * blk, x_ref.shape[2]), w_ref[...])
    hn_refs = out_refs
    if emit_h:
        out_refs[0][...] = h.reshape(seqs, blk, n)
        hn_refs = out_refs[1:]
    inv = lax.rsqrt(jnp.mean(h * h, axis=-1, keepdims=True) + RMS_EPS)
    for j, hn_ref in enumerate(hn_refs):
        hn_ref[...] = (h * inv * g_ref[j:j + 1, :]).astype(hn_ref.dtype).reshape(seqs, blk, n)


def matmul_residual_norm_blocks(x, w, res, gains, n_batch, norm_dtype, emit_h, head=None, first_block=0):
    kdim = x.shape[1]
    n = w.shape[1]
    ng = gains.shape[0]
    nb_out = x.shape[0] // (n_batch * BLOCK)
    nb = nb_out + first_block
    sp = RESIDUAL_SEQS_PER_STEP
    blocks = lambda width, shift: pl.BlockSpec((sp, BLOCK, width), lambda i, t: (i, t + shift, 0))
    const = lambda shape: pl.BlockSpec(shape, lambda i, t: (0,) * len(shape))
    if head is not None:
        res_specs = [pl.BlockSpec((sp, BLOCK, n), lambda i, t: (i, jnp.maximum(t + first_block - 1, 0), 0)),
                     const((BLOCK, n))]
        res_args = [res, head]
    else:
        res_specs, res_args = [blocks(n, first_block)], [res.reshape(n_batch, nb * BLOCK, n)]
    outs = pl.pallas_call(
        functools.partial(_mm_res_norm_blocks_kernel, emit_h=emit_h, head=head is not None, first_block=first_block),
        grid=(n_batch // sp, nb_out),
        in_specs=[blocks(kdim, 0), pl.BlockSpec((kdim, n), lambda i, t: (0, 0), pipeline_mode=pl.Buffered(1))]
        + res_specs + [const((ng, n))],
        out_specs=[blocks(n, first_block)] * int(emit_h) + [blocks(n, 0)] * ng,
        out_shape=[jax.ShapeDtypeStruct((n_batch, nb * BLOCK, n), F32)] * int(emit_h)
        + [jax.ShapeDtypeStruct((n_batch, nb_out * BLOCK, n), norm_dtype)] * ng,
        compiler_params=_cparams(("arbitrary", "arbitrary"), RESIDUAL_VMEM_LIMIT),
        name="matmul_residual_norm_blocks",
    )(x.reshape(n_batch, nb_out * BLOCK, kdim), w, *res_args, gains)
    return [o.reshape(-1, n) for o in outs]


def _lora_up_kernel(hw_ref, ha_ref, w2_ref, w0_ref, a2_ref, a0_ref, wl_ref, al_ref):
    wl_ref[...] = w0_ref[...] + _dot(hw_ref[...], w2_ref[...])
    al_ref[...] = a0_ref[...] + _dot(ha_ref[...], a2_ref[...])


def lora_up(hw, ha, w2, w0, a2, a0):
    m = hw.shape[0]
    e = w2.shape[1]
    return pl.pallas_call(
        _lora_up_kernel,
        out_shape=[jax.ShapeDtypeStruct((m, e), F32), jax.ShapeDtypeStruct((m, e), F32)],
        name="lora_up",
    )(hw, ha, w2, w0.reshape(1, e), a2, a0.reshape(1, e))


def _seg_sum(x, ones_bd):
    hi = x.astype(BF16)
    outs = []
    for c in range(x.shape[1] // MXU_TILE):
        sl = slice(c * MXU_TILE, (c + 1) * MXU_TILE)
        outs.append(_dot(hi[:, sl], ones_bd))
    return jnp.concatenate(outs, axis=1) if len(outs) > 1 else outs[0]


def _wkv_prompt_kernel(r_ref, k_ref, v_ref, z_ref, hw_ref, ha_ref, w2_ref, w0_ref, a2_ref, a0_ref,
                       kk_ref, ka_ref, rk_ref, gg_ref, gb_ref, *rest, n_cast, cast_every):
    cast_in = rest[:n_cast]
    yg_ref, sout_ref = rest[n_cast:n_cast + 2]
    cast_out = rest[n_cast + 2:2 * n_cast + 2]
    s_ref = rest[-1]
    t_idx = pl.program_id(2)
    c = CHUNK
    hd = HEAD_DIM

    @pl.when(t_idx == 0)
    def _():
        s_ref[...] = jnp.zeros_like(s_ref)

    step = (pl.program_id(0) * pl.num_programs(1) + pl.program_id(1)) * pl.num_programs(2) + t_idx
    _cast_chunks(step, cast_every, cast_in, cast_out)

    tb = r_ref.shape[1]
    nh = HEADS_PER_STREAM
    hw = nh * hd
    nc = tb // c

    li = lax.broadcasted_iota(jnp.int32, (MXU_TILE, MXU_TILE), 0) // hd
    lj = lax.broadcasted_iota(jnp.int32, (MXU_TILE, MXU_TILE), 1) // hd
    ones_bd = jnp.where(li == lj, 1.0, 0.0).astype(BF16)
    bi_ = lax.broadcasted_iota(jnp.int32, (tb, tb), 0)
    bj_ = lax.broadcasted_iota(jnp.int32, (tb, tb), 1)
    tri_incl = jnp.where((bj_ <= bi_) & (bj_ // c == bi_ // c), 1.0, 0.0).astype(BF16)
    ti = lax.broadcasted_iota(jnp.int32, (c, c), 0)
    tj = lax.broadcasted_iota(jnp.int32, (c, c), 1)
    ai = lax.broadcasted_iota(jnp.int32, (c, 2 * c), 0)
    aj = lax.broadcasted_iota(jnp.int32, (c, 2 * c), 1)
    upper = aj >= c
    aj_mod = jnp.where(upper, aj - c, aj)
    masks = dict(
        strict=tj < ti,
        eye=jnp.where(ti == tj, 1.0, 0.0).astype(F32),
        top_k=upper & (aj_mod < ai),
        bot=aj_mod <= ai,
        mean_bd=jnp.where(li == lj, 1.0 / hd, 0.0).astype(BF16))

    for st in range(r_ref.shape[2] // hw):
        _wkv_stream(st, hw, nc, r_ref, k_ref, v_ref, z_ref, hw_ref, ha_ref, w2_ref, w0_ref, a2_ref, a0_ref,
                    kk_ref, ka_ref, rk_ref, gg_ref, gb_ref, yg_ref, s_ref, ones_bd, tri_incl, masks)

    @pl.when(t_idx == pl.num_programs(2) - 1)
    def _():
        sout_ref[0] = s_ref[...]


def _wkv_stream(st, hw, nc, r_ref, k_ref, v_ref, z_ref, hw_ref, ha_ref, w2_ref, w0_ref, a2_ref, a0_ref,
                kk_ref, ka_ref, rk_ref, gg_ref, gb_ref, yg_ref, s_ref, ones_bd, tri_incl, masks):
    c = CHUNK
    hd = HEAD_DIM
    nh = hw // hd
    ls = slice(st * hw, (st + 1) * hw)
    h0 = st * nh
    r = r_ref[0, :, ls]
    k = k_ref[0, :, ls]
    v = v_ref[0, :, ls]
    wl = w0_ref[:, ls] + _dot(hw_ref[...], w2_ref[:, ls])
    al = a0_ref[:, ls] + _dot(ha_ref[...], a2_ref[:, ls])
    a = _sigmoid(al)
    lw = (-DECAY_SCALE * LOG2E) * _sigmoid(wl)
    kk = k * kk_ref[:, ls]
    n2 = _seg_sum(kk * kk, ones_bd)
    kk = kk * lax.rsqrt(jnp.maximum(n2, NORM_FLOOR_SQ))
    ka = ka_ref[:, ls]
    k2 = k * (a * ka + (1.0 - ka))
    bb = kk * a

    lw_hi, lw_lo = _split_hi_lo(lw)
    g = _dot(tri_incl, lw_hi) + _dot(tri_incl, lw_lo)
    mid = lambda ci: g[ci * c + c // 2 - 1:ci * c + c // 2, :]
    gm = jnp.concatenate([jnp.broadcast_to(mid(ci), (c, hw)) for ci in range(nc)], axis=0)
    t = g - gm
    e_a = jnp.exp2(t)
    e_prev = jnp.exp2(t - lw)
    e_inv = jnp.exp2(-t)
    e1 = [jnp.exp2(mid(ci)) for ci in range(nc)]
    e2 = [jnp.exp2(g[ci * c + c - 1:ci * c + c, :] - mid(ci)) for ci in range(nc)]

    kkd = (kk * e_prev).astype(BF16)
    rd = (r * e_a).astype(BF16)
    bi = (bb * e_inv).astype(BF16)
    ki = (k2 * e_inv).astype(BF16)
    v_bf = v.astype(BF16)
    zeros_cv = jnp.zeros((c, hd), BF16)

    pairs = [(ci, h) for ci in range(nc) for h in range(nh)]
    rows = lambda ci: slice(ci * c, (ci + 1) * c)
    cols = lambda h: slice(h * hd, (h + 1) * hd)
    xs = {(ci, h): jnp.concatenate([kkd[rows(ci), cols(h)], rd[rows(ci), cols(h)]], axis=0) for ci, h in pairs}
    r1s = {(ci, h): jnp.concatenate([bi[rows(ci), cols(h)], ki[rows(ci), cols(h)]], axis=0) for ci, h in pairs}
    vs = {(ci, h): v_bf[rows(ci), cols(h)] for ci, h in pairs}
    a_mats = {p: _dot_nt(xs[p], r1s[p]) for p in pairs}
    lk_vs = {p: _dot(jnp.where(masks["top_k"], a_mats[p][:c, :], 0.0).astype(BF16),
                     jnp.concatenate([zeros_cv, vs[p]], axis=0)) for p in pairs}
    lps = {p: jnp.where(masks["strict"], a_mats[p][:c, :c], 0.0) for p in pairs}
    ts = {p: masks["eye"] - lps[p] for p in pairs}
    for _ in range(int(math.log2(c)) - 1):
        lpb = {p: lps[p].astype(BF16) for p in pairs}
        lps = {p: _dot(lpb[p], lpb[p]) for p in pairs}
        ts = {p: _dot(ts[p].astype(BF16), (masks["eye"] + lps[p]).astype(BF16)) for p in pairs}
    a_bots = {p: jnp.where(masks["bot"], a_mats[p][c:, :], 0.0).astype(BF16) for p in pairs}
    t_bf = {p: ts[p].astype(BF16) for p in pairs}

    state = [s_ref[h0 + h] for h in range(nh)]
    y_rows = []
    for ci in range(nc):
        hs = range(nh)
        sms = [state[h] * e1[ci][:, cols(h)] for h in hs]
        p_mats = [_dot_nt(xs[ci, h], sms[h].astype(BF16)) for h in hs]
        us = [-_dot(t_bf[ci, h], (p_mats[h][:c, :] + lk_vs[ci, h]).astype(BF16)) for h in hs]
        uvs = [jnp.concatenate([us[h].astype(BF16), vs[ci, h]], axis=0) for h in hs]
        ys = [p_mats[h][c:, :] + _dot(a_bots[ci, h], uvs[h]) for h in hs]
        state = [(sms[h] + _dot_tn(uvs[h], r1s[ci, h])) * e2[ci][:, cols(h)] for h in hs]
        y_rows.append(jnp.concatenate(ys, axis=1))
    for h in range(nh):
        s_ref[h0 + h] = state[h]
    y = jnp.concatenate(y_rows, axis=0) if nc > 1 else y_rows[0]

    mean = _seg_sum(y, masks["mean_bd"])
    yc = y - mean
    var = _seg_sum(yc * yc, masks["mean_bd"])
    yn = yc * lax.rsqrt(var + GN_EPS) * gg_ref[:, ls] + gb_ref[:, ls]
    bonus = _seg_sum(r * k2 * rk_ref[:, ls], ones_bd)
    yg_ref[:, ls] = ((yn + bonus * v) * _silu(z_ref[0, :, ls])).astype(BF16)


def wkv_prompt(rkvz, hw_act, ha_act, w2, w0, a2, a0, k_k, k_a, r_k, gn_g, gn_b, n_batch, cast_weights=()):
    _, m, e = rkvz.shape
    p = m // n_batch
    tb = WKV_ROWS
    nt = p // tb
    hw = HEADS_PER_STEP * HEAD_DIM
    nh = e // HEAD_DIM
    row = lambda i, g, t: (i * nt + t, g)
    proj = lambda q: pl.BlockSpec((1, tb, hw), lambda i, g, t: (q, i * nt + t, g))
    par = pl.BlockSpec((1, hw), lambda i, g, t: (0, g))
    lr = w2.shape[0]
    hid = pl.BlockSpec((tb, lr), lambda i, g, t: (i * nt + t, 0))
    up = pl.BlockSpec((lr, hw), lambda i, g, t: (0, g))
    n_groups = e // hw
    cast_specs, every = _cast_job(cast_weights, n_batch * n_groups * nt, lambda i, g, t: (i * n_groups + g) * nt + t)
    return pl.pallas_call(
        functools.partial(_wkv_prompt_kernel, n_cast=len(cast_weights), cast_every=every),
        grid=(n_batch, n_groups, nt),
        in_specs=[proj(0), proj(1), proj(2), proj(3), hid, hid, up, par, up, par,
                  par, par, par, par, par] + cast_specs,
        out_specs=[pl.BlockSpec((tb, hw), row),
                   pl.BlockSpec((1, HEADS_PER_STEP, HEAD_DIM, HEAD_DIM), lambda i, g, t: (i, g, 0, 0))] + cast_specs,
        out_shape=[jax.ShapeDtypeStruct((m, e), BF16),
                   jax.ShapeDtypeStruct((n_batch, nh, HEAD_DIM, HEAD_DIM), F32)]
        + [jax.ShapeDtypeStruct(w.shape, BF16) for w in cast_weights],
        scratch_shapes=[pltpu.VMEM((HEADS_PER_STEP, HEAD_DIM, HEAD_DIM), F32)],
        compiler_params=_cparams(("arbitrary", "arbitrary", "arbitrary")),
        name="wkv_prompt",
    )(rkvz, rkvz, rkvz, rkvz, hw_act, ha_act, w2, w0.reshape(1, e), a2, a0.reshape(1, e),
      k_k.reshape(1, e), k_a.reshape(1, e), r_k.reshape(1, e), gn_g.reshape(1, e), gn_b.reshape(1, e),
      *cast_weights)


def _wkv_sample_kernel(r_ref, k_ref, v_ref, z_ref, wl_ref, al_ref, kk_ref, ka_ref, rk_ref, gg_ref, gb_ref,
                       s_ref, yg_ref, sout_ref, y_scr):
    hd = HEAD_DIM
    r = r_ref[0]
    k = k_ref[0]
    v = v_ref[0]
    a = _sigmoid(al_ref[0])
    d = jnp.exp2((-DECAY_SCALE * LOG2E) * _sigmoid(wl_ref[0]))
    kk = k * kk_ref[...]
    kk = kk / jnp.maximum(jnp.sqrt(jnp.sum(kk * kk, axis=-1, keepdims=True)), 1e-12)
    k2 = k * (1.0 + (a - 1.0) * ka_ref[...])
    bb = kk * a
    nh = r.shape[0]
    ii = lax.broadcasted_iota(jnp.int32, (hd, hd), 0)
    jj = lax.broadcasted_iota(jnp.int32, (hd, hd), 1)
    eye = ii == jj

    row = lambda x, h: x[h:h + 1, :]
    group = 8
    for h0 in range(0, nh, group):
        hs = range(h0, h0 + group)
        s = {h: s_ref[0, h] for h in hs}
        sa = {h: jnp.sum(s[h] * row(kk, h), axis=-1, keepdims=True) for h in hs}
        v_col = {h: jnp.sum(jnp.where(eye, row(v, h), 0.0), axis=-1, keepdims=True) for h in hs}
        s_new = {h: s[h] * row(d, h) - sa[h] * row(bb, h) + v_col[h] * row(k2, h) for h in hs}
        y_col = {h: jnp.sum(s_new[h] * row(r, h), axis=-1, keepdims=True) for h in hs}
        for h in hs:
            sout_ref[0, h] = s_new[h]
            y_scr[h:h + 1, :] = jnp.sum(jnp.where(eye, y_col[h], 0.0), axis=0, keepdims=True)

    y = y_scr[...]
    mean = jnp.mean(y, axis=-1, keepdims=True)
    yc = y - mean
    var = jnp.mean(yc * yc, axis=-1, keepdims=True)
    yn = yc * lax.rsqrt(var + GN_EPS) * gg_ref[...] + gb_ref[...]
    bonus = jnp.sum(r * k2 * rk_ref[...], axis=-1, keepdims=True)
    yg_ref[0] = ((yn + bonus * v) * _silu(z_ref[0])).astype(BF16)


def wkv_sample(rkvz, wl, al, k_k, k_a, r_k, gn_g, gn_b, state):
    _, m, e = rkvz.shape
    nh = e // HEAD_DIM
    hd = HEAD_DIM
    rkvz4 = rkvz.reshape(4, m, nh, hd)
    proj = lambda q: pl.BlockSpec((None, 1, nh, hd), lambda i: (q, i, 0, 0))
    tok = pl.BlockSpec((1, nh, hd), lambda i: (i, 0, 0))
    par = pl.BlockSpec((nh, hd), lambda i: (0, 0))
    st = pl.BlockSpec((1, nh, hd, hd), lambda i: (i, 0, 0, 0))
    as_heads = lambda x: x.reshape(nh, hd)
    return pl.pallas_call(
        _wkv_sample_kernel,
        grid=(m,),
        in_specs=[proj(0), proj(1), proj(2), proj(3), tok, tok, par, par, par, par, par, st],
        out_specs=[tok, st],
        out_shape=[jax.ShapeDtypeStruct((m, nh, hd), BF16), jax.ShapeDtypeStruct(state.shape, F32)],
        scratch_shapes=[pltpu.VMEM((nh, hd), F32)],
        compiler_params=_cparams(("arbitrary",)),
        name="wkv_sample",
    )(rkvz4, rkvz4, rkvz4, rkvz4, wl.reshape(m, nh, hd), al.reshape(m, nh, hd),
      as_heads(k_k), as_heads(k_a), as_heads(r_k), as_heads(gn_g), as_heads(gn_b), state)


def rope_tables(pos):
    half = ROPE_DIM // 2
    inv_freq = ROPE_THETA ** (-jnp.arange(half, dtype=F32) * 2.0 / ROPE_DIM)
    ang = pos.astype(F32)[:, None] * inv_freq[None, :]
    cos = jnp.cos(ang)
    sin = jnp.sin(ang)
    rows = pos.shape[0]
    ones = jnp.ones((rows, HEAD_DIM - ROPE_DIM), F32)
    zeros_h = jnp.zeros((rows, half), F32)
    zeros_r = jnp.zeros((rows, HEAD_DIM - ROPE_DIM), F32)
    cos_h = jnp.concatenate([cos, cos, ones], axis=1)
    sa_h = jnp.concatenate([-sin, zeros_h, zeros_r], axis=1)
    sb_h = jnp.concatenate([zeros_h, sin, zeros_r], axis=1)
    two = lambda t: jnp.concatenate([t, t], axis=1)
    return two(cos_h), two(sa_h), two(sb_h)


def _attn_prompt_kernel(sink_ref, q_ref, kc_ref, kp_ref, vc_ref, vp_ref, z_ref, o_ref, *, first_block):
    n = pl.program_id(1) + first_block
    hd = HEAD_DIM
    seqs, blk, _ = q_ref.shape
    n_kv = kc_ref.shape[2] // hd
    grp = q_ref.shape[2] // (n_kv * hd)
    qi = lax.broadcasted_iota(jnp.int32, (blk, 2 * blk), 0)
    kj = lax.broadcasted_iota(jnp.int32, (blk, 2 * blk), 1) - blk
    kpos = n * blk + kj
    diff = qi - kj
    valid = (kpos >= LEAD) & (diff >= 0) & (diff <= WINDOW)
    for sq in range(seqs):
        k_all = jnp.concatenate([kp_ref[sq], kc_ref[sq]], axis=0).astype(BF16)
        v_all = jnp.concatenate([vp_ref[sq], vc_ref[sq]], axis=0).astype(BF16)

        def scores(h):
            k_h = k_all[:, h * hd:(h + 1) * hd]
            return [_dot_nt(q_ref[sq, :, (h * grp + gi) * hd:(h * grp + gi + 1) * hd], k_h) for gi in range(grp)]

        outs = []
        s_next = scores(0)
        for h in range(n_kv):
            s_cur = s_next
            if h + 1 < n_kv:
                s_next = scores(h + 1)
            v_h = v_all[:, h * hd:(h + 1) * hd]
            for g0 in range(0, grp, SOFTMAX_BATCH):
                gs = range(g0, min(g0 + SOFTMAX_BATCH, grp))
                sks = {gi: sink_ref[h * grp + gi] * LOG2E for gi in gs}
                ss = {gi: jnp.where(valid, s_cur[gi], -jnp.inf) for gi in gs}
                ms = {gi: jnp.maximum(jnp.max(ss[gi], axis=-1, keepdims=True), sks[gi]) for gi in gs}
                ps = {gi: jnp.exp2(ss[gi] - ms[gi]) for gi in gs}
                dens = {gi: jnp.sum(ps[gi], axis=-1, keepdims=True) + jnp.exp2(sks[gi] - ms[gi]) for gi in gs}
                outs += [_dot(ps[gi].astype(BF16), v_h) / dens[gi] for gi in gs]
        att = jnp.concatenate(outs, axis=1)
        o_ref[sq] = (att * _silu(z_ref[sq])).astype(o_ref.dtype)


def attn_prompt(sinks, q, k, v, z, n_batch, first_block):
    m, e = q.shape
    nb = m // (n_batch * BLOCK)
    nb_out = nb - first_block
    kw = k.shape[1]
    sp = RESIDUAL_SEQS_PER_STEP
    cur = lambda i, n: (i, n + first_block, 0)
    prv = lambda i, n: (i, jnp.maximum(n + first_block - 1, 0), 0)
    seq_rows = lambda t: t.reshape(n_batch, nb * BLOCK, t.shape[1])
    q, k, v, z = seq_rows(q), seq_rows(k), seq_rows(v), seq_rows(z)
    return pl.pallas_call(
        functools.partial(_attn_prompt_kernel, first_block=first_block),
        grid=(n_batch // sp, nb_out),
        in_specs=[pl.BlockSpec(memory_space=pltpu.SMEM),
                  pl.BlockSpec((sp, BLOCK, e), cur),
                  pl.BlockSpec((sp, BLOCK, kw), cur), pl.BlockSpec((sp, BLOCK, kw), prv),
                  pl.BlockSpec((sp, BLOCK, kw), cur), pl.BlockSpec((sp, BLOCK, kw), prv),
                  pl.BlockSpec((sp, BLOCK, e), cur)],
        out_specs=pl.BlockSpec((sp, BLOCK, e), lambda i, n: (i, n, 0)),
        out_shape=jax.ShapeDtypeStruct((n_batch, nb_out * BLOCK, e), BF16),
        compiler_params=_cparams(("arbitrary", "arbitrary")),
        name="attn_prompt",
    )(sinks, q, k, k, v, v, z).reshape(n_batch * nb_out * BLOCK, e)


def _attn_sample_kernel(sink_ref, q_ref, kc_ref, vc_ref, kn_ref, vn_ref, z_ref, o_ref, ko_ref, vo_ref):
    hd = HEAD_DIM
    win = kc_ref.shape[1]
    n_kv = kc_ref.shape[2] // hd
    nq = q_ref.shape[1]
    grp = nq // n_kv
    pad = 8
    kc = kc_ref[0]
    vc = vc_ref[0]
    kn = kn_ref[0]
    vn = vn_ref[0]
    first = lax.broadcasted_iota(jnp.int32, (pad, kc.shape[1]), 0) == 0
    k_all = jnp.concatenate([kc, jnp.where(first, kn, 0.0)], axis=0).astype(BF16)
    v_all = jnp.concatenate([vc, jnp.where(first, vn, 0.0)], axis=0).astype(BF16)
    col = lax.broadcasted_iota(jnp.int32, (grp, win + pad), 1)
    valid = (col <= win) & (win - col <= WINDOW)
    q = q_ref[0].astype(BF16)
    row_i = lax.broadcasted_iota(jnp.int32, (grp, 1), 0)
    hs = range(n_kv)
    sks = []
    for h in hs:
        sk = jnp.zeros((grp, 1), F32)
        for gi in range(grp):
            sk = jnp.where(row_i == gi, sink_ref[h * grp + gi] * LOG2E, sk)
        sks.append(sk)
    ss = [jnp.where(valid, _dot_nt(q[h * grp:(h + 1) * grp, :], k_all[:, h * hd:(h + 1) * hd]), -jnp.inf)
          for h in hs]
    ms = [jnp.maximum(jnp.max(ss[h], axis=-1, keepdims=True), sks[h]) for h in hs]
    ps = [jnp.exp2(ss[h] - ms[h]) for h in hs]
    dens = [jnp.sum(ps[h], axis=-1, keepdims=True) + jnp.exp2(sks[h] - ms[h]) for h in hs]
    outs = [_dot(ps[h].astype(BF16), v_all[:, h * hd:(h + 1) * hd]) / dens[h] for h in hs]
    att = jnp.concatenate(outs, axis=0)
    o_ref[0] = (att * _silu(z_ref[0])).astype(o_ref.dtype)
    last = lax.broadcasted_iota(jnp.int32, kc.shape, 0) == win - 1
    ko_ref[0] = jnp.where(last, kn, pltpu.roll(kc, win - 1, axis=0))
    vo_ref[0] = jnp.where(last, vn, pltpu.roll(vc, win - 1, axis=0))


def attn_sample(sinks, q, cache_k, cache_v, k_new, v_new, z):
    m, win, kw = cache_k.shape
    nq = q.shape[1]
    hd = HEAD_DIM
    tok = pl.BlockSpec((1, nq, hd), lambda i: (i, 0, 0))
    cache = pl.BlockSpec((1, win, kw), lambda i: (i, 0, 0))
    new = pl.BlockSpec((1, 1, kw), lambda i: (i, 0, 0))
    return pl.pallas_call(
        _attn_sample_kernel,
        grid=(m,),
        in_specs=[pl.BlockSpec(memory_space=pltpu.SMEM), tok, cache, cache, new, new, tok],
        out_specs=[tok, cache, cache],
        out_shape=[jax.ShapeDtypeStruct((m, nq, hd), BF16),
                   jax.ShapeDtypeStruct(cache_k.shape, F32), jax.ShapeDtypeStruct(cache_v.shape, F32)],
        compiler_params=_cparams(("arbitrary",)),
        name="attn_sample",
    )(sinks, q, cache_k, cache_v, k_new, v_new, z)


def _pad_lora(w_down, w_up):
    r = w_down.shape[1]
    return (jnp.pad(w_down, ((0, 0), (0, LORA_PAD - r))).astype(BF16),
            jnp.pad(w_up, ((0, LORA_PAD - r), (0, 0))).astype(BF16))


def kernel(x_prompt, x_sample, state_wkv, state_shift, cache_k, cache_v, meta_tokens, a_norm, a_mu, a_w_rkvz,
           a_w0, a_w1, a_w2, a_a0, a_a1, a_a2, a_k_k, a_k_a, a_r_k, a_gn_g, a_gn_b, a_w_out, kv_norm, w_kv,
           b_norm, b_w_qz, b_sinks, b_w_o, final_norm):
    nb, seq, d = x_prompt.shape
    db, dseq, _ = x_sample.shape
    assert dseq == 1 and a_norm.shape[0] == 1 and b_norm.shape[0] == 1
    e = a_w_rkvz.shape[3]
    win = cache_k.shape[1]
    p_len = LEAD + N_META + seq
    assert p_len % BLOCK == 0 and (LEAD + N_META) == BLOCK
    kvw = N_KV_HEADS * HEAD_DIM

    w1, w2 = _pad_lora(a_w1[0], a_w2[0])
    a1, a2 = _pad_lora(a_a1[0], a_a2[0])
    mu = a_mu[0]
    sinks = b_sinks[0]
    gains_b = jnp.stack([kv_norm, b_norm[0]])

    tm = p_len // 8

    head = jnp.concatenate([jnp.zeros((LEAD, d), F32), meta_tokens], axis=0)
    w4 = a_w_rkvz[0]
    xm, hw_p, ha_p, x_last, w_rkvz = norm_shift_prompt(x_prompt, head, a_norm[0], mu, w1, a1,
                                                      cast_weights=(w4.reshape(-1, w4.shape[2]),))
    w_rkvz = w_rkvz.reshape(w4.shape)
    p_state_shift = x_last.reshape(1, nb, d)
    rkvz = matmul_groups(xm, w_rkvz, 4 * tm, F32, tn=e // 2)
    yg, p_state, w_out, w_kv_bf, w_qz, w_o = wkv_prompt(
        rkvz, hw_p, ha_p, w2, a_w0[0], a2, a_a0[0], a_k_k[0], a_k_a[0], a_r_k[0].reshape(-1), a_gn_g[0],
        a_gn_b[0], nb, cast_weights=(a_w_out[0], w_kv, b_w_qz[0], b_w_o[0]))
    w_qz = w_qz[None]
    hp, hn_kv, hn_b = matmul_residual_norm_blocks(yg, w_out, x_prompt, gains_b, nb, BF16, True, head=head)

    pos_p = jnp.maximum(jnp.arange(p_len, dtype=jnp.int32) - LEAD, 0)
    tabs_p = rope_tables(pos_p)
    k_p, v_p = matmul_rope(hn_kv, w_kv_bf, tabs_p, 4 * tm, kvw, (F32, F32))
    q_p, = matmul_rope(hn_b, w_qz, tabs_p, 2 * tm, e, (BF16,), scale=Q_SCALE, n=e)
    z_p = matmul_groups(hn_b[None], w_qz, 4 * tm, F32, n=e, col=1, tn=e // 2)[0]
    skip = (LEAD + N_META) // BLOCK
    att = attn_prompt(sinks, q_p, k_p, v_p, z_p, nb, skip)
    y_prompt, = matmul_residual_norm_blocks(att, w_o, hp, final_norm[None], nb, F32, False, first_block=skip)
    y_prompt = y_prompt.reshape(nb, seq, d)
    tail = lambda t: t.reshape(nb, p_len, kvw)[:, -win:].reshape(nb, win, N_KV_HEADS, HEAD_DIM)
    p_cache_k = tail(k_p)
    p_cache_v = tail(v_p)

    hs = x_sample.reshape(db, d)
    xm_s, hw_s, ha_s, xn_s = norm_shift_sample(hs, state_shift[0], a_norm[0], mu, w1, a1)
    rkvz_s = matmul_groups(xm_s, w_rkvz, db, F32)
    wl_s, al_s = lora_up(hw_s, ha_s, w2, a_w0[0], a2, a_a0[0])
    yg_s, s_state = wkv_sample(rkvz_s, wl_s, al_s, a_k_k[0], a_k_a[0], a_r_k[0].reshape(-1), a_gn_g[0],
                               a_gn_b[0], state_wkv[0])
    hs, hn_kv_s, hn_b_s = matmul_residual_norm(yg_s.reshape(db, e), w_out, hs, gains_b, db, BF16)
    tabs_s = rope_tables(jnp.full((db,), PAST_LEN, jnp.int32))
    k_s, v_s = matmul_rope(hn_kv_s, w_kv_bf, tabs_s, db, kvw, (F32, F32))
    q_s, = matmul_rope(hn_b_s, w_qz, tabs_s, db, e, (F32,), scale=Q_SCALE, n=e)
    z_s = matmul_groups(hn_b_s[None], w_qz, db, F32, n=e, col=1)[0]
    nq = e // HEAD_DIM
    att_s, s_cache_k, s_cache_v = attn_sample(
        sinks, q_s.reshape(db, nq, HEAD_DIM), cache_k.reshape(db, win, kvw), cache_v.reshape(db, win, kvw),
        k_s.reshape(db, 1, kvw), v_s.reshape(db, 1, kvw), z_s.reshape(db, nq, HEAD_DIM))
    y_s, = matmul_residual_norm(att_s.reshape(db, e), w_o, hs, final_norm[None], db, F32, emit_h=False)
    y_sample = y_s.reshape(db, 1, d)

    return (y_prompt, y_sample, p_state[None], p_state_shift,
            p_cache_k, p_cache_v,
            s_state[None], xn_s[None],
            s_cache_k.reshape(cache_k.shape), s_cache_v.reshape(cache_v.shape))
```

```python
import functools
import math

import jax
import jax.numpy as jnp
from jax import lax
from jax.experimental import pallas as pl
from jax.experimental.pallas import tpu as pltpu

F32 = jnp.float32
BF16 = jnp.bfloat16

HEAD_DIM = 64
N_KV_HEADS = 8
WINDOW = 128
BLOCK = 128
ROPE_DIM = HEAD_DIM // 4
ROPE_THETA = 500000.0
N_META = 16
PAST_LEN = 16384
RMS_EPS = 1e-6
GN_EPS = 64e-5
NORM_FLOOR_SQ = 1e-24
LEAD = (-N_META) % BLOCK
CHUNK = 64
WKV_ROWS = 128
CAST_CHUNKS = 32
K_CHUNKS = 4
SEQS_PER_STEP = 2
RESIDUAL_SEQS_PER_STEP = 4
HEADS_PER_STREAM = 32
HEADS_PER_STEP = 32
LORA_PAD = 128
MXU_TILE = 256
ROPE_SLAB = 512
VMEM_LIMIT = 48 * 1024 * 1024
RESIDUAL_VMEM_LIMIT = 56 * 1024 * 1024
LOG2E = 1.0 / math.log(2.0)
DECAY_SCALE = math.exp(-0.5)
SOFTMAX_BATCH = 4
Q_SCALE = HEAD_DIM ** -0.5 * LOG2E


def _cparams(sem, vmem_limit=VMEM_LIMIT):
    return pltpu.CompilerParams(dimension_semantics=sem, vmem_limit_bytes=vmem_limit)


def _sigmoid(x):
    return 1.0 / (1.0 + jnp.exp2(x * (-LOG2E)))


def _silu(x):
    return x * _sigmoid(x)


def _dot(a, b):
    return jnp.dot(a, b, preferred_element_type=F32)


def _dot_nt(a, b):
    return lax.dot_general(a, b, (((1,), (1,)), ((), ())), preferred_element_type=F32)


def _dot_tn(a, b):
    return lax.dot_general(a, b, (((0,), (0,)), ((), ())), preferred_element_type=F32)


def _split_hi_lo(x):
    hi = x.astype(BF16)
    lo = (x - hi.astype(F32)).astype(BF16)
    return hi, lo


def _cast_job(weights, n_steps, step_of):
    n_chunks = min(CAST_CHUNKS, n_steps)
    every = n_steps // n_chunks
    assert all(w.shape[0] % (n_chunks * 16) == 0 for w in weights)
    chunk = lambda *idx: (jnp.minimum(step_of(*idx) // every, n_chunks - 1), 0)
    return [pl.BlockSpec((w.shape[0] // n_chunks, w.shape[1]), chunk) for w in weights], every


def _cast_chunks(step, every, srcs, dsts):
    @pl.when(step % every == 0)
    def _():
        for src, dst in zip(srcs, dsts):
            dst[...] = src[...].astype(dst.dtype)


def _mixes(xn, prev, mu_ref, w1_ref, a1_ref, xm_ref, hw_ref, ha_ref):
    xx = prev - xn
    n_proj = xm_ref.shape[0]
    for p in range(n_proj):
        xm_ref[p] = (xn + xx * mu_ref[p:p + 1, :]).astype(xm_ref.dtype)
    xw = (xn + xx * mu_ref[n_proj:n_proj + 1, :]).astype(BF16)
    xa = (xn + xx * mu_ref[n_proj + 1:n_proj + 2, :]).astype(BF16)
    hw_ref[...] = jnp.tanh(_dot(xw, w1_ref[...])).astype(hw_ref.dtype)
    ha_ref[...] = _dot(xa, a1_ref[...]).astype(ha_ref.dtype)


def _norm_shift_kernel(x_ref, head_ref, g_ref, mu_ref, w1_ref, a1_ref, *rest, n_cast, cast_every):
    cast_in = rest[:n_cast]
    xm_ref, hw_ref, ha_ref, last_ref = rest[n_cast:n_cast + 4]
    cast_out = rest[n_cast + 4:2 * n_cast + 4]
    carry_ref = rest[-1]

    @pl.when(pl.program_id(1) == 0)
    def _():
        carry_ref[...] = jnp.zeros_like(carry_ref)

    _cast_chunks(pl.program_id(0) * pl.num_programs(1) + pl.program_id(1), cast_every, cast_in, cast_out)

    is_head = pl.program_id(1) == 0
    for sq in range(x_ref.shape[0]):
        x = jnp.where(is_head, head_ref[...], x_ref[sq])
        tm = x.shape[0]
        xn = x * lax.rsqrt(jnp.mean(x * x, axis=-1, keepdims=True) + RMS_EPS) * g_ref[...]
        rolled = pltpu.roll(xn, 1, axis=0)
        row = lax.broadcasted_iota(jnp.int32, xn.shape, 0)
        prev = jnp.where(row == 0, carry_ref[sq, 0:1, :], rolled)
        _mixes(xn, prev, mu_ref, w1_ref, a1_ref, xm_ref.at[:, sq], hw_ref.at[sq], ha_ref.at[sq])
        carry_ref[sq, 0:1, :] = xn[tm - 1:tm, :]
        last_ref[sq] = xn[tm - 1:tm, :]


def norm_shift_prompt(x, head, g, mu, w1, a1, cast_weights=()):
    b, seq, d = x.shape
    tm = BLOCK
    p = tm + seq
    n_mix = mu.shape[0]
    n_proj = n_mix - 2
    lr = w1.shape[1]
    nt = p // tm
    sp = SEQS_PER_STEP
    const = lambda shape: pl.BlockSpec(shape, lambda i, t: (0,) * len(shape))
    hid = pl.BlockSpec((sp, tm, lr), lambda i, t: (i, t, 0))
    cast_specs, every = _cast_job(cast_weights, (b // sp) * nt, lambda i, t: i * nt + t)
    xm, hw_act, ha_act, last, *casts = pl.pallas_call(
        functools.partial(_norm_shift_kernel, n_cast=len(cast_weights), cast_every=every),
        grid=(b // sp, nt),
        in_specs=[pl.BlockSpec((sp, tm, d), lambda i, t: (i, jnp.maximum(t - 1, 0), 0)),
                  const((tm, d)), const((1, d)), const((n_mix, d)), const((d, lr)), const((d, lr))] + cast_specs,
        out_specs=[pl.BlockSpec((n_proj, sp, tm, d), lambda i, t: (0, i, t, 0)), hid, hid,
                   pl.BlockSpec((sp, 1, d), lambda i, t: (i, 0, 0))] + cast_specs,
        out_shape=[jax.ShapeDtypeStruct((n_proj, b, p, d), BF16),
                   jax.ShapeDtypeStruct((b, p, lr), BF16), jax.ShapeDtypeStruct((b, p, lr), BF16),
                   jax.ShapeDtypeStruct((b, 1, d), F32)] + [jax.ShapeDtypeStruct(w.shape, BF16) for w in cast_weights],
        scratch_shapes=[pltpu.VMEM((sp, 8, d), F32)],
        compiler_params=_cparams(("arbitrary", "arbitrary")),
        name="norm_shift_prompt",
    )(x, head, g.reshape(1, d), mu, w1, a1, *cast_weights)
    return (xm.reshape(n_proj, b * p, d), hw_act.reshape(b * p, lr), ha_act.reshape(b * p, lr), last, *casts)


def _norm_shift_sample_kernel(x_ref, prev_ref, g_ref, mu_ref, w1_ref, a1_ref, xm_ref, hw_ref, ha_ref, xn_ref):
    x = x_ref[...]
    xn = x * lax.rsqrt(jnp.mean(x * x, axis=-1, keepdims=True) + RMS_EPS) * g_ref[...]
    xn_ref[...] = xn
    _mixes(xn, prev_ref[...], mu_ref, w1_ref, a1_ref, xm_ref, hw_ref, ha_ref)


def norm_shift_sample(x, prev, g, mu, w1, a1):
    m, d = x.shape
    lr = w1.shape[1]
    return pl.pallas_call(
        _norm_shift_sample_kernel,
        out_shape=[jax.ShapeDtypeStruct((mu.shape[0] - 2, m, d), BF16),
                   jax.ShapeDtypeStruct((m, lr), BF16), jax.ShapeDtypeStruct((m, lr), BF16),
                   jax.ShapeDtypeStruct((m, d), F32)],
        name="norm_shift_sample",
    )(x, prev, g.reshape(1, d), mu, w1, a1)


def _rope(y, cos, sin_a, sin_b):
    half = ROPE_DIM // 2
    step = ROPE_SLAB
    rep = step // cos.shape[1]
    tile = lambda t: jnp.concatenate([t] * rep, axis=1)
    cos_t, sa_t, sb_t = tile(cos), tile(sin_a), tile(sin_b)
    outs = []
    for j in range(y.shape[1] // step):
        ys = y[:, j * step:(j + 1) * step]
        outs.append(ys * cos_t + pltpu.roll(ys, step - half, axis=1) * sa_t + pltpu.roll(ys, half, axis=1) * sb_t)
    return jnp.concatenate(outs, axis=1) if len(outs) > 1 else outs[0]


def _mm_group_kernel(x_ref, w_ref, o_ref):
    o_ref[0] = _dot(x_ref[0], w_ref[0]).astype(o_ref.dtype)


def matmul_groups(x, w, tm, out_dtype, n=None, col=0, tn=None):
    g = w.shape[0]
    n = w.shape[2] if n is None else n
    tn = n if tn is None else tn
    nt = n // tn
    _, m, kdim = x.shape
    return pl.pallas_call(
        _mm_group_kernel,
        grid=(g, nt, m // tm),
        in_specs=[pl.BlockSpec((1, tm, kdim), lambda q, j, i: (q, i, 0)),
                  pl.BlockSpec((1, kdim, tn), lambda q, j, i: (q, 0, col * nt + j))],
        out_specs=pl.BlockSpec((1, tm, tn), lambda q, j, i: (q, i, j)),
        out_shape=jax.ShapeDtypeStruct((g, m, n), out_dtype),
        compiler_params=_cparams(("arbitrary", "arbitrary", "arbitrary")),
        name="matmul_groups",
    )(x, w)


def _mm_rope_kernel(x_ref, w_ref, cos_ref, sa_ref, sb_ref, *o_refs, n_rope, scale):
    y = _dot(x_ref[...], w_ref[...])
    rot = _rope(y[:, :n_rope], cos_ref[...], sa_ref[...], sb_ref[...])
    if scale != 1.0:
        rot = rot * scale
    o_refs[0][...] = rot.astype(o_refs[0].dtype)
    if len(o_refs) > 1:
        o_refs[1][...] = y[:, n_rope:].astype(o_refs[1].dtype)


def matmul_rope(x, w, tables, tm, n_rope, out_dtypes, scale=1.0, n=None):
    m, kdim = x.shape
    n = w.shape[-1] if n is None else n
    w_block = (kdim, n) if w.ndim == 2 else (None, kdim, n)
    lanes = tables[0].shape[1]
    widths = [n_rope] + ([n - n_rope] if n > n_rope else [])
    tab_blocks = tables[0].shape[0] // tm
    tab = pl.BlockSpec((tm, lanes), lambda i: (i % tab_blocks, 0))
    outs = pl.pallas_call(
        functools.partial(_mm_rope_kernel, n_rope=n_rope, scale=scale),
        grid=(m // tm,),
        in_specs=[pl.BlockSpec((tm, kdim), lambda i: (i, 0)),
                  pl.BlockSpec(w_block, lambda i: (0,) * w.ndim, pipeline_mode=pl.Buffered(1)),
                  tab, tab, tab],
        out_specs=[pl.BlockSpec((tm, wd), lambda i: (i, 0)) for wd in widths],
        out_shape=[jax.ShapeDtypeStruct((m, wd), dt) for wd, dt in zip(widths, out_dtypes)],
        compiler_params=_cparams(("arbitrary",)),
        name="matmul_rope",
    )(x, w, *tables)
    return outs


def _mm_res_norm_kernel(x_ref, w_ref, res_ref, g_ref, *rest, emit_h):
    *out_refs, acc_ref = rest
    kc = pl.program_id(1)

    @pl.when(kc == 0)
    def _():
        acc_ref[...] = res_ref[...]

    acc_ref[...] += _dot(x_ref[...], w_ref[...])

    @pl.when(kc == pl.num_programs(1) - 1)
    def _():
        h = acc_ref[...]
        hn_refs = out_refs
        if emit_h:
            out_refs[0][...] = h
            hn_refs = out_refs[1:]
        inv = lax.rsqrt(jnp.mean(h * h, axis=-1, keepdims=True) + RMS_EPS)
        for j, hn_ref in enumerate(hn_refs):
            hn_ref[...] = (h * inv * g_ref[j:j + 1, :]).astype(hn_ref.dtype)


def matmul_residual_norm(x, w, res, gains, tm, norm_dtype, emit_h=True):
    m, kdim = x.shape
    n = w.shape[1]
    ng = gains.shape[0]
    row = lambda width: pl.BlockSpec((tm, width), lambda i, c: (i, 0))
    tk = kdim // K_CHUNKS
    return pl.pallas_call(
        functools.partial(_mm_res_norm_kernel, emit_h=emit_h),
        grid=(m // tm, K_CHUNKS),
        in_specs=[pl.BlockSpec((tm, tk), lambda i, c: (i, c)),
                  pl.BlockSpec((tk, n), lambda i, c: (c, 0)),
                  row(n),
                  pl.BlockSpec((ng, n), lambda i, c: (0, 0))],
        out_specs=[row(n)] * (int(emit_h) + ng),
        out_shape=[jax.ShapeDtypeStruct((m, n), F32)] * int(emit_h) + [jax.ShapeDtypeStruct((m, n), norm_dtype)] * ng,
        scratch_shapes=[pltpu.VMEM((tm, n), F32)],
        compiler_params=_cparams(("arbitrary", "arbitrary")),
        name="matmul_residual_norm",
    )(x, w, res, gains)


def _mm_res_norm_blocks_kernel(x_ref, w_ref, res_ref, *rest, emit_h, head, first_block):
    if head:
        head_ref, g_ref, *out_refs = rest
        res = jnp.where(pl.program_id(1) + first_block == 0, head_ref[...][None], res_ref[...])
    else:
        g_ref, *out_refs = rest
        res = res_ref[...]
    seqs, blk, n = res.shape
    h = res.reshape(seqs * blk, n) + _dot(x_ref[...].reshape(seqs * blk, x_ref.shape[2]), w_ref[...])
    hn_refs = out_refs
    if emit_h:
        out_refs[0][...] = h.reshape(seqs, blk, n)
        hn_refs = out_refs[1:]
    inv = lax.rsqrt(jnp.mean(h * h, axis=-1, keepdims=True) + RMS_EPS)
    for j, hn_ref in enumerate(hn_refs):
        hn_ref[...] = (h * inv * g_ref[j:j + 1, :]).astype(hn_ref.dtype).reshape(seqs, blk, n)


def matmul_residual_norm_blocks(x, w, res, gains, n_batch, norm_dtype, emit_h, head=None, first_block=0):
    kdim = x.shape[1]
    n = w.shape[1]
    ng = gains.shape[0]
    nb_out = x.shape[0] // (n_batch * BLOCK)
    nb = nb_out + first_block
    sp = RESIDUAL_SEQS_PER_STEP
    blocks = lambda width, shift: pl.BlockSpec((sp, BLOCK, width), lambda i, t: (i, t + shift, 0))
    const = lambda shape: pl.BlockSpec(shape, lambda i, t: (0,) * len(shape))
    if head is not None:
        res_specs = [pl.BlockSpec((sp, BLOCK, n), lambda i, t: (i, jnp.maximum(t + first_block - 1, 0), 0)),
                     const((BLOCK, n))]
        res_args = [res, head]
    else:
        res_specs, res_args = [blocks(n, first_block)], [res.reshape(n_batch, nb * BLOCK, n)]
    outs = pl.pallas_call(
        functools.partial(_mm_res_norm_blocks_kernel, emit_h=emit_h, head=head is not None, first_block=first_block),
        grid=(n_batch // sp, nb_out),
        in_specs=[blocks(kdim, 0), pl.BlockSpec((kdim, n), lambda i, t: (0, 0), pipeline_mode=pl.Buffered(1))]
        + res_specs + [const((ng, n))],
        out_specs=[blocks(n, first_block)] * int(emit_h) + [blocks(n, 0)] * ng,
        out_shape=[jax.ShapeDtypeStruct((n_batch, nb * BLOCK, n), F32)] * int(emit_h)
        + [jax.ShapeDtypeStruct((n_batch, nb_out * BLOCK, n), norm_dtype)] * ng,
        compiler_params=_cparams(("arbitrary", "arbitrary"), RESIDUAL_VMEM_LIMIT),
        name="matmul_residual_norm_blocks",
    )(x.reshape(n_batch, nb_out * BLOCK, kdim), w, *res_args, gains)
    return [o.reshape(-1, n) for o in outs]


def _lora_up_kernel(hw_ref, ha_ref, w2_ref, w0_ref, a2_ref, a0_ref, wl_ref, al_ref):
    wl_ref[...] = w0_ref[...] + _dot(hw_ref[...], w2_ref[...])
    al_ref[...] = a0_ref[...] + _dot(ha_ref[...], a2_ref[...])


def lora_up(hw, ha, w2, w0, a2, a0):
    m = hw.shape[0]
    e = w2.shape[1]
    return pl.pallas_call(
        _lora_up_kernel,
        out_shape=[jax.ShapeDtypeStruct((m, e), F32), jax.ShapeDtypeStruct((m, e), F32)],
        name="lora_up",
    )(hw, ha, w2, w0.reshape(1, e), a2, a0.reshape(1, e))


def _seg_sum(x, ones_bd):
    hi = x.astype(BF16)
    outs = []
    for c in range(x.shape[1] // MXU_TILE):
        sl = slice(c * MXU_TILE, (c + 1) * MXU_TILE)
        outs.append(_dot(hi[:, sl], ones_bd))
    return jnp.concatenate(outs, axis=1) if len(outs) > 1 else outs[0]


def _wkv_prompt_kernel(r_ref, k_ref, v_ref, z_ref, hw_ref, ha_ref, w2_ref, w0_ref, a2_ref, a0_ref,
                       kk_ref, ka_ref, rk_ref, gg_ref, gb_ref, *rest, n_cast, cast_every):
    cast_in = rest[:n_cast]
    yg_ref, sout_ref = rest[n_cast:n_cast + 2]
    cast_out = rest[n_cast + 2:2 * n_cast + 2]
    s_ref = rest[-1]
    t_idx = pl.program_id(2)
    c = CHUNK
    hd = HEAD_DIM

    @pl.when(t_idx == 0)
    def _():
        s_ref[...] = jnp.zeros_like(s_ref)

    step = (pl.program_id(0) * pl.num_programs(1) + pl.program_id(1)) * pl.num_programs(2) + t_idx
    _cast_chunks(step, cast_every, cast_in, cast_out)

    tb = r_ref.shape[1]
    nh = HEADS_PER_STREAM
    hw = nh * hd
    nc = tb // c

    li = lax.broadcasted_iota(jnp.int32, (MXU_TILE, MXU_TILE), 0) // hd
    lj = lax.broadcasted_iota(jnp.int32, (MXU_TILE, MXU_TILE), 1) // hd
    ones_bd = jnp.where(li == lj, 1.0, 0.0).astype(BF16)
    bi_ = lax.broadcasted_iota(jnp.int32, (tb, tb), 0)
    bj_ = lax.broadcasted_iota(jnp.int32, (tb, tb), 1)
    tri_incl = jnp.where((bj_ <= bi_) & (bj_ // c == bi_ // c), 1.0, 0.0).astype(BF16)
    ti = lax.broadcasted_iota(jnp.int32, (c, c), 0)
    tj = lax.broadcasted_iota(jnp.int32, (c, c), 1)
    ai = lax.broadcasted_iota(jnp.int32, (c, 2 * c), 0)
    aj = lax.broadcasted_iota(jnp.int32, (c, 2 * c), 1)
    upper = aj >= c
    aj_mod = jnp.where(upper, aj - c, aj)
    masks = dict(
        strict=tj < ti,
        eye=jnp.where(ti == tj, 1.0, 0.0).astype(F32),
        top_k=upper & (aj_mod < ai),
        bot=aj_mod <= ai,
        mean_bd=jnp.where(li == lj, 1.0 / hd, 0.0).astype(BF16))

    for st in range(r_ref.shape[2] // hw):
        _wkv_stream(st, hw, nc, r_ref, k_ref, v_ref, z_ref, hw_ref, ha_ref, w2_ref, w0_ref, a2_ref, a0_ref,
                    kk_ref, ka_ref, rk_ref, gg_ref, gb_ref, yg_ref, s_ref, ones_bd, tri_incl, masks)

    @pl.when(t_idx == pl.num_programs(2) - 1)
    def _():
        sout_ref[0] = s_ref[...]


def _wkv_stream(st, hw, nc, r_ref, k_ref, v_ref, z_ref, hw_ref, ha_ref, w2_ref, w0_ref, a2_ref, a0_ref,
                kk_ref, ka_ref, rk_ref, gg_ref, gb_ref, yg_ref, s_ref, ones_bd, tri_incl, masks):
    c = CHUNK
    hd = HEAD_DIM
    nh = hw // hd
    ls = slice(st * hw, (st + 1) * hw)
    h0 = st * nh
    r = r_ref[0, :, ls]
    k = k_ref[0, :, ls]
    v = v_ref[0, :, ls]
    wl = w0_ref[:, ls] + _dot(hw_ref[...], w2_ref[:, ls])
    al = a0_ref[:, ls] + _dot(ha_ref[...], a2_ref[:, ls])
    a = _sigmoid(al)
    lw = (-DECAY_SCALE * LOG2E) * _sigmoid(wl)
    kk = k * kk_ref[:, ls]
    n2 = _seg_sum(kk * kk, ones_bd)
    kk = kk * lax.rsqrt(jnp.maximum(n2, NORM_FLOOR_SQ))
    ka = ka_ref[:, ls]
    k2 = k * (a * ka + (1.0 - ka))
    bb = kk * a

    lw_hi, lw_lo = _split_hi_lo(lw)
    g = _dot(tri_incl, lw_hi) + _dot(tri_incl, lw_lo)
    mid = lambda ci: g[ci * c + c // 2 - 1:ci * c + c // 2, :]
    gm = jnp.concatenate([jnp.broadcast_to(mid(ci), (c, hw)) for ci in range(nc)], axis=0)
    t = g - gm
    e_a = jnp.exp2(t)
    e_prev = jnp.exp2(t - lw)
    e_inv = jnp.exp2(-t)
    e1 = [jnp.exp2(mid(ci)) for ci in range(nc)]
    e2 = [jnp.exp2(g[ci * c + c - 1:ci * c + c, :] - mid(ci)) for ci in range(nc)]

    kkd = (kk * e_prev).astype(BF16)
    rd = (r * e_a).astype(BF16)
    bi = (bb * e_inv).astype(BF16)
    ki = (k2 * e_inv).astype(BF16)
    v_bf = v.astype(BF16)
    zeros_cv = jnp.zeros((c, hd), BF16)

    pairs = [(ci, h) for ci in range(nc) for h in range(nh)]
    rows = lambda ci: slice(ci * c, (ci + 1) * c)
    cols = lambda h: slice(h * hd, (h + 1) * hd)
    xs = {(ci, h): jnp.concatenate([kkd[rows(ci), cols(h)], rd[rows(ci), cols(h)]], axis=0) for ci, h in pairs}
    r1s = {(ci, h): jnp.concatenate([bi[rows(ci), cols(h)], ki[rows(ci), cols(h)]], axis=0) for ci, h in pairs}
    vs = {(ci, h): v_bf[rows(ci), cols(h)] for ci, h in pairs}
    a_mats = {p: _dot_nt(xs[p], r1s[p]) for p in pairs}
    lk_vs = {p: _dot(jnp.where(masks["top_k"], a_mats[p][:c, :], 0.0).astype(BF16),
                     jnp.concatenate([zeros_cv, vs[p]], axis=0)) for p in pairs}
    lps = {p: jnp.where(masks["strict"], a_mats[p][:c, :c], 0.0) for p in pairs}
    ts = {p: masks["eye"] - lps[p] for p in pairs}
    for _ in range(int(math.log2(c)) - 1):
        lpb = {p: lps[p].astype(BF16) for p in pairs}
        lps = {p: _dot(lpb[p], lpb[p]) for p in pairs}
        ts = {p: _dot(ts[p].astype(BF16), (masks["eye"] + lps[p]).astype(BF16)) for p in pairs}
    a_bots = {p: jnp.where(masks["bot"], a_mats[p][c:, :], 0.0).astype(BF16) for p in pairs}
    t_bf = {p: ts[p].astype(BF16) for p in pairs}

    state = [s_ref[h0 + h] for h in range(nh)]
    y_rows = []
    for ci in range(nc):
        hs = range(nh)
        sms = [state[h] * e1[ci][:, cols(h)] for h in hs]
        p_mats = [_dot_nt(xs[ci, h], sms[h].astype(BF16)) for h in hs]
        us = [-_dot(t_bf[ci, h], (p_mats[h][:c, :] + lk_vs[ci, h]).astype(BF16)) for h in hs]
        uvs = [jnp.concatenate([us[h].astype(BF16), vs[ci, h]], axis=0) for h in hs]
        ys = [p_mats[h][c:, :] + _dot(a_bots[ci, h], uvs[h]) for h in hs]
        state = [(sms[h] + _dot_tn(uvs[h], r1s[ci, h])) * e2[ci][:, cols(h)] for h in hs]
        y_rows.append(jnp.concatenate(ys, axis=1))
    for h in range(nh):
        s_ref[h0 + h] = state[h]
    y = jnp.concatenate(y_rows, axis=0) if nc > 1 else y_rows[0]

    mean = _seg_sum(y, masks["mean_bd"])
    yc = y - mean
    var = _seg_sum(yc * yc, masks["mean_bd"])
    yn = yc * lax.rsqrt(var + GN_EPS) * gg_ref[:, ls] + gb_ref[:, ls]
    bonus = _seg_sum(r * k2 * rk_ref[:, ls], ones_bd)
    yg_ref[:, ls] = ((yn + bonus * v) * _silu(z_ref[0, :, ls])).astype(BF16)


def wkv_prompt(rkvz, hw_act, ha_act, w2, w0, a2, a0, k_k, k_a, r_k, gn_g, gn_b, n_batch, cast_weights=()):
    _, m, e = rkvz.shape
    p = m // n_batch
    tb = WKV_ROWS
    nt = p // tb
    hw = HEADS_PER_STEP * HEAD_DIM
    nh = e // HEAD_DIM
    row = lambda i, g, t: (i * nt + t, g)
    proj = lambda q: pl.BlockSpec((1, tb, hw), lambda i, g, t: (q, i * nt + t, g))
    par = pl.BlockSpec((1, hw), lambda i, g, t: (0, g))
    lr = w2.shape[0]
    hid = pl.BlockSpec((tb, lr), lambda i, g, t: (i * nt + t, 0))
    up = pl.BlockSpec((lr, hw), lambda i, g, t: (0, g))
    n_groups = e // hw
    cast_specs, every = _cast_job(cast_weights, n_batch * n_groups * nt, lambda i, g, t: (i * n_groups + g) * nt + t)
    return pl.pallas_call(
        functools.partial(_wkv_prompt_kernel, n_cast=len(cast_weights), cast_every=every),
        grid=(n_batch, n_groups, nt),
        in_specs=[proj(0), proj(1), proj(2), proj(3), hid, hid, up, par, up, par,
                  par, par, par, par, par] + cast_specs,
        out_specs=[pl.BlockSpec((tb, hw), row),
                   pl.BlockSpec((1, HEADS_PER_STEP, HEAD_DIM, HEAD_DIM), lambda i, g, t: (i, g, 0, 0))] + cast_specs,
        out_shape=[jax.ShapeDtypeStruct((m, e), BF16),
                   jax.ShapeDtypeStruct((n_batch, nh, HEAD_DIM, HEAD_DIM), F32)]
        + [jax.ShapeDtypeStruct(w.shape, BF16) for w in cast_weights],
        scratch_shapes=[pltpu.VMEM((HEADS_PER_STEP, HEAD_DIM, HEAD_DIM), F32)],
        compiler_params=_cparams(("arbitrary", "arbitrary", "arbitrary")),
        name="wkv_prompt",
    )(rkvz, rkvz, rkvz, rkvz, hw_act, ha_act, w2, w0.reshape(1, e), a2, a0.reshape(1, e),
      k_k.reshape(1, e), k_a.reshape(1, e), r_k.reshape(1, e), gn_g.reshape(1, e), gn_b.reshape(1, e),
      *cast_weights)


def _wkv_sample_kernel(r_ref, k_ref, v_ref, z_ref, wl_ref, al_ref, kk_ref, ka_ref, rk_ref, gg_ref, gb_ref,
                       s_ref, yg_ref, sout_ref, y_scr):
    hd = HEAD_DIM
    r = r_ref[0]
    k = k_ref[0]
    v = v_ref[0]
    a = _sigmoid(al_ref[0])
    d = jnp.exp2((-DECAY_SCALE * LOG2E) * _sigmoid(wl_ref[0]))
    kk = k * kk_ref[...]
    kk = kk / jnp.maximum(jnp.sqrt(jnp.sum(kk * kk, axis=-1, keepdims=True)), 1e-12)
    k2 = k * (1.0 + (a - 1.0) * ka_ref[...])
    bb = kk * a
    nh = r.shape[0]
    ii = lax.broadcasted_iota(jnp.int32, (hd, hd), 0)
    jj = lax.broadcasted_iota(jnp.int32, (hd, hd), 1)
    eye = ii == jj

    row = lambda x, h: x[h:h + 1, :]
    group = 8
    for h0 in range(0, nh, group):
        hs = range(h0, h0 + group)
        s = {h: s_ref[0, h] for h in hs}
        sa = {h: jnp.sum(s[h] * row(kk, h), axis=-1, keepdims=True) for h in hs}
        v_col = {h: jnp.sum(jnp.where(eye, row(v, h), 0.0), axis=-1, keepdims=True) for h in hs}
        s_new = {h: s[h] * row(d, h) - sa[h] * row(bb, h) + v_col[h] * row(k2, h) for h in hs}
        y_col = {h: jnp.sum(s_new[h] * row(r, h), axis=-1, keepdims=True) for h in hs}
        for h in hs:
            sout_ref[0, h] = s_new[h]
            y_scr[h:h + 1, :] = jnp.sum(jnp.where(eye, y_col[h], 0.0), axis=0, keepdims=True)

    y = y_scr[...]
    mean = jnp.mean(y, axis=-1, keepdims=True)
    yc = y - mean
    var = jnp.mean(yc * yc, axis=-1, keepdims=True)
    yn = yc * lax.rsqrt(var + GN_EPS) * gg_ref[...] + gb_ref[...]
    bonus = jnp.sum(r * k2 * rk_ref[...], axis=-1, keepdims=True)
    yg_ref[0] = ((yn + bonus * v) * _silu(z_ref[0])).astype(BF16)


def wkv_sample(rkvz, wl, al, k_k, k_a, r_k, gn_g, gn_b, state):
    _, m, e = rkvz.shape
    nh = e // HEAD_DIM
    hd = HEAD_DIM
    rkvz4 = rkvz.reshape(4, m, nh, hd)
    proj = lambda q: pl.BlockSpec((None, 1, nh, hd), lambda i: (q, i, 0, 0))
    tok = pl.BlockSpec((1, nh, hd), lambda i: (i, 0, 0))
    par = pl.BlockSpec((nh, hd), lambda i: (0, 0))
    st = pl.BlockSpec((1, nh, hd, hd), lambda i: (i, 0, 0, 0))
    as_heads = lambda x: x.reshape(nh, hd)
    return pl.pallas_call(
        _wkv_sample_kernel,
        grid=(m,),
        in_specs=[proj(0), proj(1), proj(2), proj(3), tok, tok, par, par, par, par, par, st],
        out_specs=[tok, st],
        out_shape=[jax.ShapeDtypeStruct((m, nh, hd), BF16), jax.ShapeDtypeStruct(state.shape, F32)],
        scratch_shapes=[pltpu.VMEM((nh, hd), F32)],
        compiler_params=_cparams(("arbitrary",)),
        name="wkv_sample",
    )(rkvz4, rkvz4, rkvz4, rkvz4, wl.reshape(m, nh, hd), al.reshape(m, nh, hd),
      as_heads(k_k), as_heads(k_a), as_heads(r_k), as_heads(gn_g), as_heads(gn_b), state)


def rope_tables(pos):
    half = ROPE_DIM // 2
    inv_freq = ROPE_THETA ** (-jnp.arange(half, dtype=F32) * 2.0 / ROPE_DIM)
    ang = pos.astype(F32)[:, None] * inv_freq[None, :]
    cos = jnp.cos(ang)
    sin = jnp.sin(ang)
    rows = pos.shape[0]
    ones = jnp.ones((rows, HEAD_DIM - ROPE_DIM), F32)
    zeros_h = jnp.zeros((rows, half), F32)
    zeros_r = jnp.zeros((rows, HEAD_DIM - ROPE_DIM), F32)
    cos_h = jnp.concatenate([cos, cos, ones], axis=1)
    sa_h = jnp.concatenate([-sin, zeros_h, zeros_r], axis=1)
    sb_h = jnp.concatenate([zeros_h, sin, zeros_r], axis=1)
    two = lambda t: jnp.concatenate([t, t], axis=1)
    return two(cos_h), two(sa_h), two(sb_h)


def _attn_prompt_kernel(sink_ref, q_ref, kc_ref, kp_ref, vc_ref, vp_ref, z_ref, o_ref, *, first_block):
    n = pl.program_id(1) + first_block
    hd = HEAD_DIM
    seqs, blk, _ = q_ref.shape
    n_kv = kc_ref.shape[2] // hd
    grp = q_ref.shape[2] // (n_kv * hd)
    qi = lax.broadcasted_iota(jnp.int32, (blk, 2 * blk), 0)
    kj = lax.broadcasted_iota(jnp.int32, (blk, 2 * blk), 1) - blk
    kpos = n * blk + kj
    diff = qi - kj
    valid = (kpos >= LEAD) & (diff >= 0) & (diff <= WINDOW)
    for sq in range(seqs):
        k_all = jnp.concatenate([kp_ref[sq], kc_ref[sq]], axis=0).astype(BF16)
        v_all = jnp.concatenate([vp_ref[sq], vc_ref[sq]], axis=0).astype(BF16)

        def scores(h):
            k_h = k_all[:, h * hd:(h + 1) * hd]
            return [_dot_nt(q_ref[sq, :, (h * grp + gi) * hd:(h * grp + gi + 1) * hd], k_h) for gi in range(grp)]

        outs = []
        s_next = scores(0)
        for h in range(n_kv):
            s_cur = s_next
            if h + 1 < n_kv:
                s_next = scores(h + 1)
            v_h = v_all[:, h * hd:(h + 1) * hd]
            for g0 in range(0, grp, SOFTMAX_BATCH):
                gs = range(g0, min(g0 + SOFTMAX_BATCH, grp))
                sks = {gi: sink_ref[h * grp + gi] * LOG2E for gi in gs}
                ss = {gi: jnp.where(valid, s_cur[gi], -jnp.inf) for gi in gs}
                ms = {gi: jnp.maximum(jnp.max(ss[gi], axis=-1, keepdims=True), sks[gi]) for gi in gs}
                ps = {gi: jnp.exp2(ss[gi] - ms[gi]) for gi in gs}
                dens = {gi: jnp.sum(ps[gi], axis=-1, keepdims=True) + jnp.exp2(sks[gi] - ms[gi]) for gi in gs}
                outs += [_dot(ps[gi].astype(BF16), v_h) / dens[gi] for gi in gs]
        att = jnp.concatenate(outs, axis=1)
        o_ref[sq] = (att * _silu(z_ref[sq])).astype(o_ref.dtype)


def attn_prompt(sinks, q, k, v, z, n_batch, first_block):
    m, e = q.shape
    nb = m // (n_batch * BLOCK)
    nb_out = nb - first_block
    kw = k.shape[1]
    sp = SEQS_PER_STEP
    cur = lambda i, n: (i, n + first_block, 0)
    prv = lambda i, n: (i, jnp.maximum(n + first_block - 1, 0), 0)
    seq_rows = lambda t: t.reshape(n_batch, nb * BLOCK, t.shape[1])
    q, k, v, z = seq_rows(q), seq_rows(k), seq_rows(v), seq_rows(z)
    return pl.pallas_call(
        functools.partial(_attn_prompt_kernel, first_block=first_block),
        grid=(n_batch // sp, nb_out),
        in_specs=[pl.BlockSpec(memory_space=pltpu.SMEM),
                  pl.BlockSpec((sp, BLOCK, e), cur),
                  pl.BlockSpec((sp, BLOCK, kw), cur), pl.BlockSpec((sp, BLOCK, kw), prv),
                  pl.BlockSpec((sp, BLOCK, kw), cur), pl.BlockSpec((sp, BLOCK, kw), prv),
                  pl.BlockSpec((sp, BLOCK, e), cur)],
        out_specs=pl.BlockSpec((sp, BLOCK, e), lambda i, n: (i, n, 0)),
        out_shape=jax.ShapeDtypeStruct((n_batch, nb_out * BLOCK, e), BF16),
        compiler_params=_cparams(("arbitrary", "arbitrary")),
        name="attn_prompt",
    )(sinks, q, k, k, v, v, z).reshape(n_batch * nb_out * BLOCK, e)


def _attn_sample_kernel(sink_ref, q_ref, kc_ref, vc_ref, kn_ref, vn_ref, z_ref, o_ref, ko_ref, vo_ref):
    hd = HEAD_DIM
    win = kc_ref.shape[1]
    n_kv = kc_ref.shape[2] // hd
    nq = q_ref.shape[1]
    grp = nq // n_kv
    pad = 8
    kc = kc_ref[0]
    vc = vc_ref[0]
    kn = kn_ref[0]
    vn = vn_ref[0]
    first = lax.broadcasted_iota(jnp.int32, (pad, kc.shape[1]), 0) == 0
    k_all = jnp.concatenate([kc, jnp.where(first, kn, 0.0)], axis=0).astype(BF16)
    v_all = jnp.concatenate([vc, jnp.where(first, vn, 0.0)], axis=0).astype(BF16)
    col = lax.broadcasted_iota(jnp.int32, (grp, win + pad), 1)
    valid = (col <= win) & (win - col <= WINDOW)
    q = q_ref[0].astype(BF16)
    row_i = lax.broadcasted_iota(jnp.int32, (grp, 1), 0)
    hs = range(n_kv)
    sks = []
    for h in hs:
        sk = jnp.zeros((grp, 1), F32)
        for gi in range(grp):
            sk = jnp.where(row_i == gi, sink_ref[h * grp + gi] * LOG2E, sk)
        sks.append(sk)
    ss = [jnp.where(valid, _dot_nt(q[h * grp:(h + 1) * grp, :], k_all[:, h * hd:(h + 1) * hd]), -jnp.inf)
          for h in hs]
    ms = [jnp.maximum(jnp.max(ss[h], axis=-1, keepdims=True), sks[h]) for h in hs]
    ps = [jnp.exp2(ss[h] - ms[h]) for h in hs]
    dens = [jnp.sum(ps[h], axis=-1, keepdims=True) + jnp.exp2(sks[h] - ms[h]) for h in hs]
    outs = [_dot(ps[h].astype(BF16), v_all[:, h * hd:(h + 1) * hd]) / dens[h] for h in hs]
    att = jnp.concatenate(outs, axis=0)
    o_ref[0] = (att * _silu(z_ref[0])).astype(o_ref.dtype)
    last = lax.broadcasted_iota(jnp.int32, kc.shape, 0) == win - 1
    ko_ref[0] = jnp.where(last, kn, pltpu.roll(kc, win - 1, axis=0))
    vo_ref[0] = jnp.where(last, vn, pltpu.roll(vc, win - 1, axis=0))


def attn_sample(sinks, q, cache_k, cache_v, k_new, v_new, z):
    m, win, kw = cache_k.shape
    nq = q.shape[1]
    hd = HEAD_DIM
    tok = pl.BlockSpec((1, nq, hd), lambda i: (i, 0, 0))
    cache = pl.BlockSpec((1, win, kw), lambda i: (i, 0, 0))
    new = pl.BlockSpec((1, 1, kw), lambda i: (i, 0, 0))
    return pl.pallas_call(
        _attn_sample_kernel,
        grid=(m,),
        in_specs=[pl.BlockSpec(memory_space=pltpu.SMEM), tok, cache, cache, new, new, tok],
        out_specs=[tok, cache, cache],
        out_shape=[jax.ShapeDtypeStruct((m, nq, hd), BF16),
                   jax.ShapeDtypeStruct(cache_k.shape, F32), jax.ShapeDtypeStruct(cache_v.shape, F32)],
        compiler_params=_cparams(("arbitrary",)),
        name="attn_sample",
    )(sinks, q, cache_k, cache_v, k_new, v_new, z)


def _pad_lora(w_down, w_up):
    r = w_down.shape[1]
    return (jnp.pad(w_down, ((0, 0), (0, LORA_PAD - r))).astype(BF16),
            jnp.pad(w_up, ((0, LORA_PAD - r), (0, 0))).astype(BF16))


def kernel(x_prompt, x_sample, state_wkv, state_shift, cache_k, cache_v, meta_tokens, a_norm, a_mu, a_w_rkvz,
           a_w0, a_w1, a_w2, a_a0, a_a1, a_a2, a_k_k, a_k_a, a_r_k, a_gn_g, a_gn_b, a_w_out, kv_norm, w_kv,
           b_norm, b_w_qz, b_sinks, b_w_o, final_norm):
    nb, seq, d = x_prompt.shape
    db, dseq, _ = x_sample.shape
    assert dseq == 1 and a_norm.shape[0] == 1 and b_norm.shape[0] == 1
    e = a_w_rkvz.shape[3]
    win = cache_k.shape[1]
    p_len = LEAD + N_META + seq
    assert p_len % BLOCK == 0 and (LEAD + N_META) == BLOCK
    kvw = N_KV_HEADS * HEAD_DIM

    w1, w2 = _pad_lora(a_w1[0], a_w2[0])
    a1, a2 = _pad_lora(a_a1[0], a_a2[0])
    mu = a_mu[0]
    sinks = b_sinks[0]
    gains_b = jnp.stack([kv_norm, b_norm[0]])

    tm = p_len // 8

    head = jnp.concatenate([jnp.zeros((LEAD, d), F32), meta_tokens], axis=0)
    w4 = a_w_rkvz[0]
    xm, hw_p, ha_p, x_last, w_rkvz = norm_shift_prompt(x_prompt, head, a_norm[0], mu, w1, a1,
                                                      cast_weights=(w4.reshape(-1, w4.shape[2]),))
    w_rkvz = w_rkvz.reshape(w4.shape)
    p_state_shift = x_last.reshape(1, nb, d)
    rkvz = matmul_groups(xm, w_rkvz, 4 * tm, F32, tn=e // 2)
    yg, p_state, w_out, w_kv_bf, w_qz, w_o = wkv_prompt(
        rkvz, hw_p, ha_p, w2, a_w0[0], a2, a_a0[0], a_k_k[0], a_k_a[0], a_r_k[0].reshape(-1), a_gn_g[0],
        a_gn_b[0], nb, cast_weights=(a_w_out[0], w_kv, b_w_qz[0], b_w_o[0]))
    w_qz = w_qz[None]
    hp, hn_kv, hn_b = matmul_residual_norm_blocks(yg, w_out, x_prompt, gains_b, nb, BF16, True, head=head)

    pos_p = jnp.maximum(jnp.arange(p_len, dtype=jnp.int32) - LEAD, 0)
    tabs_p = rope_tables(pos_p)
    k_p, v_p = matmul_rope(hn_kv, w_kv_bf, tabs_p, 4 * tm, kvw, (F32, F32))
    q_p, = matmul_rope(hn_b, w_qz, tabs_p, 2 * tm, e, (BF16,), scale=Q_SCALE, n=e)
    z_p = matmul_groups(hn_b[None], w_qz, 4 * tm, F32, n=e, col=1, tn=e // 2)[0]
    skip = (LEAD + N_META) // BLOCK
    att = attn_prompt(sinks, q_p, k_p, v_p, z_p, nb, skip)
    y_prompt, = matmul_residual_norm_blocks(att, w_o, hp, final_norm[None], nb, F32, False, first_block=skip)
    y_prompt = y_prompt.reshape(nb, seq, d)
    tail = lambda t: t.reshape(nb, p_len, kvw)[:, -win:].reshape(nb, win, N_KV_HEADS, HEAD_DIM)
    p_cache_k = tail(k_p)
    p_cache_v = tail(v_p)

    hs = x_sample.reshape(db, d)
    xm_s, hw_s, ha_s, xn_s = norm_shift_sample(hs, state_shift[0], a_norm[0], mu, w1, a1)
    rkvz_s = matmul_groups(xm_s, w_rkvz, db, F32)
    wl_s, al_s = lora_up(hw_s, ha_s, w2, a_w0[0], a2, a_a0[0])
    yg_s, s_state = wkv_sample(rkvz_s, wl_s, al_s, a_k_k[0], a_k_a[0], a_r_k[0].reshape(-1), a_gn_g[0],
                               a_gn_b[0], state_wkv[0])
    hs, hn_kv_s, hn_b_s = matmul_residual_norm(yg_s.reshape(db, e), w_out, hs, gains_b, db, BF16)
    tabs_s = rope_tables(jnp.full((db,), PAST_LEN, jnp.int32))
    k_s, v_s = matmul_rope(hn_kv_s, w_kv_bf, tabs_s, db, kvw, (F32, F32))
    q_s, = matmul_rope(hn_b_s, w_qz, tabs_s, db, e, (F32,), scale=Q_SCALE, n=e)
    z_s = matmul_groups(hn_b_s[None], w_qz, db, F32, n=e, col=1)[0]
    nq = e // HEAD_DIM
    att_s, s_cache_k, s_cache_v = attn_sample(
        sinks, q_s.reshape(db, nq, HEAD_DIM), cache_k.reshape(db, win, kvw), cache_v.reshape(db, win, kvw),
        k_s.reshape(db, 1, kvw), v_s.reshape(db, 1, kvw), z_s.reshape(db, nq, HEAD_DIM))
    y_s, = matmul_residual_norm(att_s.reshape(db, e), w_o, hs, final_norm[None], db, F32, emit_h=False)
    y_sample = y_s.reshape(db, 1, d)

    return (y_prompt, y_sample, p_state[None], p_state_shift,
            p_cache_k, p_cache_v,
            s_state[None], xn_s[None],
            s_cache_k.reshape(cache_k.shape), s_cache_v.reshape(cache_v.shape))
```
